```python
import jax, jax.numpy as jnp
from jax import lax
import numpy as np

D_MODEL = 1024
BATCH = 8
SEQ = 8192
DEPTH = 2

CHUNK = 64
D_CONV = D_MODEL // 4
CONV_WIDTH = 31
SB_HEAD_DIM = 64
D_SB = D_MODEL // 2
N_SB_HEADS = D_SB // SB_HEAD_DIM
RET_HEAD_DIM = 64
D_RET = D_MODEL // 4
N_RET_HEADS = D_RET // RET_HEAD_DIM
D_MIX = D_CONV + D_SB + D_RET
D_IN_PROJ = 2 * D_CONV + 3 * D_SB + 4 * D_RET
D_FF = ((8 * D_MODEL // 3 + 127) // 128) * 128
Q_BLOCK = 128
ROPE_BASE = 10000.0
EPS = 1e-6

kernel_name = "hybrid_conv_stickbreak_retention_macaron"


def rms_norm(x, g):
    xf = x.astype(jnp.float32)
    y = xf * lax.rsqrt(jnp.mean(xf * xf, axis=-1, keepdims=True) + EPS)
    return (y * g.astype(jnp.float32)).astype(x.dtype)


def layer_norm(x, g, b):
    xf = x.astype(jnp.float32)
    mu = jnp.mean(xf, axis=-1, keepdims=True)
    xc = xf - mu
    var = jnp.mean(xc * xc, axis=-1, keepdims=True)
    return (xc * lax.rsqrt(var + EPS) * g.astype(jnp.float32) + b.astype(jnp.float32)).astype(x.dtype)


def swiglu(h, w_in, w_out):
    gate, up = jnp.split(h @ w_in, 2, axis=-1)
    return (jax.nn.silu(gate) * up) @ w_out


def conv_module(u, conv_w, conv_b, ln_g, ln_b):
    a, b = jnp.split(u, 2, axis=-1)
    v = a * jax.nn.sigmoid(b)
    v = jnp.pad(v, ((0, 0), (CONV_WIDTH - 1, 0), (0, 0)))
    y = lax.conv_general_dilated(
        v, conv_w[:, None, :].astype(v.dtype), window_strides=(1,), padding="VALID",
        dimension_numbers=("NWC", "WIO", "NWC"), feature_group_count=D_CONV)
    y = y + conv_b
    return jax.nn.silu(layer_norm(y, ln_g, ln_b))


def stick_breaking(q, k, v):
    B, S, H, Dh = q.shape
    nb = S // Q_BLOCK
    qb = q.reshape(B, nb, Q_BLOCK, H, Dh).transpose(1, 0, 2, 3, 4)
    kpos = jnp.arange(S)
    scale = Dh ** -0.5

    def block(args):
        qi, i = args
        z = jnp.einsum("bqhd,bkhd->bhqk", qi, k,
                       preferred_element_type=jnp.float32) * scale
        qpos = i * Q_BLOCK + jnp.arange(Q_BLOCK)
        mask = kpos[None, :] < qpos[:, None]
        log_beta = jax.nn.log_sigmoid(z)
        log_not = jnp.where(mask, jax.nn.log_sigmoid(-z), 0.0)
        remain = lax.cumsum(log_not, axis=3, reverse=True) - log_not
        w = jnp.where(mask, jnp.exp(log_beta + remain), 0.0)
        return jnp.einsum("bhqk,bkhd->bqhd", w.astype(v.dtype), v)

    out = lax.map(block, (qb, jnp.arange(nb)))
    return out.transpose(1, 0, 2, 3, 4).reshape(B, S, H, Dh)


def rotary(x, pos):
    half = x.shape[-1] // 2
    inv = 1.0 / (ROPE_BASE ** (jnp.arange(half, dtype=jnp.float32) / half))
    ang = pos.astype(jnp.float32)[:, None] * inv[None, :]
    cos = jnp.cos(ang)[None, :, None, :]
    sin = jnp.sin(ang)[None, :, None, :]
    x1 = x[..., :half].astype(jnp.float32)
    x2 = x[..., half:].astype(jnp.float32)
    return jnp.concatenate([x1 * cos - x2 * sin, x1 * sin + x2 * cos], axis=-1).astype(x.dtype)


def retention(q, k, v):
    B, S, H, Dh = q.shape
    nc = S // CHUNK
    log_gamma = jnp.log1p(-jnp.exp2(-5.0 - jnp.arange(H, dtype=jnp.float32)))
    qc = (q * (Dh ** -0.5)).reshape(B, nc, CHUNK, H, Dh)
    kc = k.reshape(B, nc, CHUNK, H, Dh)
    vc = v.reshape(B, nc, CHUNK, H, Dh)
    idx = jnp.arange(CHUNK, dtype=jnp.float32)
    d_intra = jnp.exp(log_gamma[:, None, None] * jnp.abs(idx[:, None] - idx[None, :]))
    scores = jnp.einsum("bnihd,bnjhd->bnhij", qc, kc,
                        preferred_element_type=jnp.float32) * d_intra
    y_intra = jnp.einsum("bnhij,bnjhe->bnihe", scores, vc.astype(jnp.float32))
    k_decay = jnp.exp(log_gamma[None, :] * (CHUNK - 1 - idx)[:, None])
    kv = jnp.einsum("bnjhd,jh,bnjhe->bnhde", kc.astype(jnp.float32), k_decay,
                    vc.astype(jnp.float32))
    chunk_decay = jnp.exp(log_gamma * CHUNK)[None, :, None, None]

    def step(state, kv_n):
        return chunk_decay * state + kv_n, state

    _, s_prev = lax.scan(step, jnp.zeros((B, H, Dh, Dh), jnp.float32),
                         kv.transpose(1, 0, 2, 3, 4))
    s_prev = s_prev.transpose(1, 0, 2, 3, 4)
    q_decay = jnp.exp(log_gamma[None, :] * (idx + 1.0)[:, None])
    y_cross = jnp.einsum("bnihd,ih,bnhde->bnihe", qc.astype(jnp.float32), q_decay, s_prev)
    return (y_intra + y_cross).reshape(B, S, H, Dh)


def head_norm(y, g):
    B, S, H, Dh = y.shape
    mu = jnp.mean(y, axis=-1, keepdims=True)
    yc = y - mu
    var = jnp.mean(yc * yc, axis=-1, keepdims=True)
    return (yc * lax.rsqrt(var + EPS)).reshape(B, S, H * Dh) * g.astype(jnp.float32)


def hybrid_mixer(h, w_in, conv_w, conv_b, conv_ln_g, conv_ln_b, ret_norm_g, w_out, pos):
    B, S, _ = h.shape
    o1 = 2 * D_CONV
    o2 = o1 + D_SB
    o3 = o2 + D_SB
    o4 = o3 + D_SB
    o5 = o4 + D_RET
    o6 = o5 + D_RET
    o7 = o6 + D_RET
    u_conv, q_sb, k_sb, v_sb, q_r, k_r, v_r, g_r = jnp.split(
        h @ w_in, [o1, o2, o3, o4, o5, o6, o7], axis=-1)
    y_conv = conv_module(u_conv, conv_w, conv_b, conv_ln_g, conv_ln_b)
    sb = lambda t: t.reshape(B, S, N_SB_HEADS, SB_HEAD_DIM)
    y_sb = stick_breaking(sb(q_sb), sb(k_sb), sb(v_sb)).reshape(B, S, D_SB)
    rt = lambda t: t.reshape(B, S, N_RET_HEADS, RET_HEAD_DIM)
    y_r = retention(rotary(rt(q_r), pos), rotary(rt(k_r), pos), rt(v_r))
    y_r = jax.nn.silu(g_r.astype(jnp.float32)) * head_norm(y_r, ret_norm_g)
    y = jnp.concatenate([y_conv, y_sb, y_r.astype(h.dtype)], axis=-1)
    return y @ w_out


def _fwd_setup_inputs(seed: int = 0) -> dict:
    key = jax.random.key(seed)
    ks = jax.random.split(key, 20)
    f32 = jnp.float32
    nrm = lambda k, shape, scale: jax.random.normal(k, shape, f32) * scale
    gain = lambda k, shape: 1.0 + 0.02 * jax.random.normal(k, shape, f32)
    return {
        "x": jax.random.normal(ks[0], (BATCH, SEQ, D_MODEL), f32),
        "ffn1_norm": gain(ks[1], (DEPTH, D_MODEL)),
        "ffn1_w_in": nrm(ks[2], (DEPTH, D_MODEL, 2 * D_FF), D_MODEL ** -0.5),
        "ffn1_w_out": nrm(ks[3], (DEPTH, D_FF, D_MODEL), D_FF ** -0.5),
        "mix_norm": gain(ks[4], (DEPTH, D_MODEL)),
        "mix_w_in": nrm(ks[5], (DEPTH, D_MODEL, D_IN_PROJ), D_MODEL ** -0.5),
        "conv_w": nrm(ks[6], (DEPTH, CONV_WIDTH, D_CONV), CONV_WIDTH ** -0.5),
        "conv_b": nrm(ks[7], (DEPTH, D_CONV), 0.02),
        "conv_ln_g": gain(ks[8], (DEPTH, D_CONV)),
        "conv_ln_b": nrm(ks[9], (DEPTH, D_CONV), 0.02),
        "ret_norm_g": gain(ks[10], (DEPTH, D_RET)),
        "mix_w_out": nrm(ks[11], (DEPTH, D_MIX, D_MODEL), D_MIX ** -0.5),
        "ffn2_norm": gain(ks[12], (DEPTH, D_MODEL)),
        "ffn2_w_in": nrm(ks[13], (DEPTH, D_MODEL, 2 * D_FF), D_MODEL ** -0.5),
        "ffn2_w_out": nrm(ks[14], (DEPTH, D_FF, D_MODEL), D_FF ** -0.5),
        "final_norm": gain(ks[15], (D_MODEL,)),
    }


def _fwd_reference(x, ffn1_norm, ffn1_w_in, ffn1_w_out, mix_norm, mix_w_in, conv_w, conv_b,
              conv_ln_g, conv_ln_b, ret_norm_g, mix_w_out, ffn2_norm, ffn2_w_in, ffn2_w_out,
              final_norm):
    S = x.shape[1]
    pos = jnp.arange(S)
    for l in range(DEPTH):
        x = x + 0.5 * swiglu(rms_norm(x, ffn1_norm[l]), ffn1_w_in[l], ffn1_w_out[l])
        x = x + hybrid_mixer(rms_norm(x, mix_norm[l]), mix_w_in[l], conv_w[l], conv_b[l],
                             conv_ln_g[l], conv_ln_b[l], ret_norm_g[l], mix_w_out[l], pos)
        x = x + 0.5 * swiglu(rms_norm(x, ffn2_norm[l]), ffn2_w_in[l], ffn2_w_out[l])
    return rms_norm(x, final_norm)


import jax as _jax
import jax.numpy as _jnp

TWIN_FORMAT = 'train_step'
FWD_PARAMS = ['x', 'ffn1_norm', 'ffn1_w_in', 'ffn1_w_out', 'mix_norm', 'mix_w_in', 'conv_w', 'conv_b', 'conv_ln_g', 'conv_ln_b', 'ret_norm_g', 'mix_w_out', 'ffn2_norm', 'ffn2_w_in', 'ffn2_w_out', 'final_norm']
TWIN_WEIGHTS = ['ffn1_norm', 'ffn1_w_in', 'ffn1_w_out', 'mix_norm', 'mix_w_in', 'conv_w', 'conv_b', 'conv_ln_g', 'conv_ln_b', 'ret_norm_g', 'mix_w_out', 'ffn2_norm', 'ffn2_w_in', 'ffn2_w_out', 'final_norm']
TWIN_DIFF_INPUT = 'x'
TWIN_INPUTS = ['x', 'ffn1_norm', 'ffn1_w_in', 'ffn1_w_out', 'mix_norm', 'mix_w_in', 'conv_w', 'conv_b', 'conv_ln_g', 'conv_ln_b', 'ret_norm_g', 'mix_w_out', 'ffn2_norm', 'ffn2_w_in', 'ffn2_w_out', 'final_norm', 'loss_target', 'm_ffn1_norm', 'm_ffn1_w_in', 'm_ffn1_w_out', 'm_mix_norm', 'm_mix_w_in', 'm_conv_w', 'm_conv_b', 'm_conv_ln_g', 'm_conv_ln_b', 'm_ret_norm_g', 'm_mix_w_out', 'm_ffn2_norm', 'm_ffn2_w_in', 'm_ffn2_w_out', 'm_final_norm', 'v_ffn1_norm', 'v_ffn1_w_in', 'v_ffn1_w_out', 'v_mix_norm', 'v_mix_w_in', 'v_conv_w', 'v_conv_b', 'v_conv_ln_g', 'v_conv_ln_b', 'v_ret_norm_g', 'v_mix_w_out', 'v_ffn2_norm', 'v_ffn2_w_in', 'v_ffn2_w_out', 'v_final_norm']
TWIN_OUTPUTS = ['loss', 'grad_x', 'grad_ffn1_norm', 'grad_ffn1_w_in', 'grad_ffn1_w_out', 'grad_mix_norm', 'grad_mix_w_in', 'grad_conv_w', 'grad_conv_b', 'grad_conv_ln_g', 'grad_conv_ln_b', 'grad_ret_norm_g', 'grad_mix_w_out', 'grad_ffn2_norm', 'grad_ffn2_w_in', 'grad_ffn2_w_out', 'grad_final_norm', 'delta_ffn1_norm', 'delta_ffn1_w_in', 'delta_ffn1_w_out', 'delta_mix_norm', 'delta_mix_w_in', 'delta_conv_w', 'delta_conv_b', 'delta_conv_ln_g', 'delta_conv_ln_b', 'delta_ret_norm_g', 'delta_mix_w_out', 'delta_ffn2_norm', 'delta_ffn2_w_in', 'delta_ffn2_w_out', 'delta_final_norm', 'new_m_ffn1_norm', 'new_m_ffn1_w_in', 'new_m_ffn1_w_out', 'new_m_mix_norm', 'new_m_mix_w_in', 'new_m_conv_w', 'new_m_conv_b', 'new_m_conv_ln_g', 'new_m_conv_ln_b', 'new_m_ret_norm_g', 'new_m_mix_w_out', 'new_m_ffn2_norm', 'new_m_ffn2_w_in', 'new_m_ffn2_w_out', 'new_m_final_norm', 'new_v_ffn1_norm', 'new_v_ffn1_w_in', 'new_v_ffn1_w_out', 'new_v_mix_norm', 'new_v_mix_w_in', 'new_v_conv_w', 'new_v_conv_b', 'new_v_conv_ln_g', 'new_v_conv_ln_b', 'new_v_ret_norm_g', 'new_v_mix_w_out', 'new_v_ffn2_norm', 'new_v_ffn2_w_in', 'new_v_ffn2_w_out', 'new_v_final_norm']
TWIN_LEAF_KINDS = {'loss': 'loss', 'grad_x': 'grad_x', 'grad_ffn1_norm': 'grad_w', 'grad_ffn1_w_in': 'grad_w', 'grad_ffn1_w_out': 'grad_w', 'grad_mix_norm': 'grad_w', 'grad_mix_w_in': 'grad_w', 'grad_conv_w': 'grad_w', 'grad_conv_b': 'grad_w', 'grad_conv_ln_g': 'grad_w', 'grad_conv_ln_b': 'grad_w', 'grad_ret_norm_g': 'grad_w', 'grad_mix_w_out': 'grad_w', 'grad_ffn2_norm': 'grad_w', 'grad_ffn2_w_in': 'grad_w', 'grad_ffn2_w_out': 'grad_w', 'grad_final_norm': 'grad_w', 'delta_ffn1_norm': 'delta_w', 'delta_ffn1_w_in': 'delta_w', 'delta_ffn1_w_out': 'delta_w', 'delta_mix_norm': 'delta_w', 'delta_mix_w_in': 'delta_w', 'delta_conv_w': 'delta_w', 'delta_conv_b': 'delta_w', 'delta_conv_ln_g': 'delta_w', 'delta_conv_ln_b': 'delta_w', 'delta_ret_norm_g': 'delta_w', 'delta_mix_w_out': 'delta_w', 'delta_ffn2_norm': 'delta_w', 'delta_ffn2_w_in': 'delta_w', 'delta_ffn2_w_out': 'delta_w', 'delta_final_norm': 'delta_w', 'new_m_ffn1_norm': 'new_m', 'new_m_ffn1_w_in': 'new_m', 'new_m_ffn1_w_out': 'new_m', 'new_m_mix_norm': 'new_m', 'new_m_mix_w_in': 'new_m', 'new_m_conv_w': 'new_m', 'new_m_conv_b': 'new_m', 'new_m_conv_ln_g': 'new_m', 'new_m_conv_ln_b': 'new_m', 'new_m_ret_norm_g': 'new_m', 'new_m_mix_w_out': 'new_m', 'new_m_ffn2_norm': 'new_m', 'new_m_ffn2_w_in': 'new_m', 'new_m_ffn2_w_out': 'new_m', 'new_m_final_norm': 'new_m', 'new_v_ffn1_norm': 'new_v', 'new_v_ffn1_w_in': 'new_v', 'new_v_ffn1_w_out': 'new_v', 'new_v_mix_norm': 'new_v', 'new_v_mix_w_in': 'new_v', 'new_v_conv_w': 'new_v', 'new_v_conv_b': 'new_v', 'new_v_conv_ln_g': 'new_v', 'new_v_conv_ln_b': 'new_v', 'new_v_ret_norm_g': 'new_v', 'new_v_mix_w_out': 'new_v', 'new_v_ffn2_norm': 'new_v', 'new_v_ffn2_w_in': 'new_v', 'new_v_ffn2_w_out': 'new_v', 'new_v_final_norm': 'new_v'}


def _forward(args):
    return _fwd_reference(*[args[k] for k in FWD_PARAMS])


def _output_shape():
    def fwd():
        inp = _fwd_setup_inputs(0)
        return _fwd_reference(*[inp[k] for k in FWD_PARAMS])
    out = _jax.eval_shape(fwd)
    return out.shape, out.dtype

N_MICROBATCH = 1
ADAM_LR = 0.001
ADAM_B1 = 0.9
ADAM_B2 = 0.999
ADAM_EPS = 1e-08
ADAM_WD = 0.01
ADAM_STEP = 10
PER_EXAMPLE_BATCH_AXIS = {'x': 0, 'loss_target': 0}
SHARED_INPUTS = []
_WEIGHT_DTYPES = {'ffn1_norm': _jnp.float32, 'ffn1_w_in': _jnp.float32, 'ffn1_w_out': _jnp.float32, 'mix_norm': _jnp.float32, 'mix_w_in': _jnp.float32, 'conv_w': _jnp.float32, 'conv_b': _jnp.float32, 'conv_ln_g': _jnp.float32, 'conv_ln_b': _jnp.float32, 'ret_norm_g': _jnp.float32, 'mix_w_out': _jnp.float32, 'ffn2_norm': _jnp.float32, 'ffn2_w_in': _jnp.float32, 'ffn2_w_out': _jnp.float32, 'final_norm': _jnp.float32}
MOMENT_SCALE = {'ffn1_norm': 1.198596e-01, 'ffn1_w_in': 4.957715e-02, 'ffn1_w_out': 8.077608e-02, 'mix_norm': 1.931585e-01, 'mix_w_in': 1.088600e-01, 'conv_w': 1.339329e-01, 'conv_b': 2.856931e-01, 'conv_ln_g': 1.841014e-01, 'conv_ln_b': 1.666954e-01, 'ret_norm_g': 1.263603e-01, 'mix_w_out': 1.352619e-01, 'ffn2_norm': 8.716333e-02, 'ffn2_w_in': 3.704449e-02, 'ffn2_w_out': 6.041037e-02, 'final_norm': 6.400695e+01}


def _to_microbatches(a, axis):
    t = _jnp.moveaxis(a, axis, 0)
    t = t.reshape((N_MICROBATCH, t.shape[0] // N_MICROBATCH) + t.shape[1:])
    return _jnp.moveaxis(t, 1, axis + 1)


def setup_inputs(seed: int = 0) -> dict:
    inp = _fwd_setup_inputs(seed)
    key = _jax.random.fold_in(_jax.random.key(seed), 7919)
    shape, _ = _output_shape()
    out = dict(inp)
    out["loss_target"] = _jax.random.normal(_jax.random.fold_in(key, 0), shape, _jnp.float32)
    for i, name in enumerate(TWIN_WEIGHTS):
        w = inp[name].astype(_jnp.float32)
        if MOMENT_SCALE is None:
            s = _jnp.sqrt(_jnp.mean(_jnp.square(w)) + 1e-30)
        else:
            s = MOMENT_SCALE[name]
        km, kv = _jax.random.split(_jax.random.fold_in(key, i + 1))
        out[name] = w
        out["m_" + name] = s * _jax.random.normal(km, w.shape, _jnp.float32)
        out["v_" + name] = (s * s) * _jax.random.uniform(kv, w.shape, _jnp.float32, 0.5, 1.5)
    if N_MICROBATCH > 1:
        for name, axis in PER_EXAMPLE_BATCH_AXIS.items():
            out[name] = _to_microbatches(out[name], axis)
    return {'x': out['x'], 'ffn1_norm': out['ffn1_norm'], 'ffn1_w_in': out['ffn1_w_in'], 'ffn1_w_out': out['ffn1_w_out'], 'mix_norm': out['mix_norm'], 'mix_w_in': out['mix_w_in'], 'conv_w': out['conv_w'], 'conv_b': out['conv_b'], 'conv_ln_g': out['conv_ln_g'], 'conv_ln_b': out['conv_ln_b'], 'ret_norm_g': out['ret_norm_g'], 'mix_w_out': out['mix_w_out'], 'ffn2_norm': out['ffn2_norm'], 'ffn2_w_in': out['ffn2_w_in'], 'ffn2_w_out': out['ffn2_w_out'], 'final_norm': out['final_norm'], 'loss_target': out['loss_target'], 'm_ffn1_norm': out['m_ffn1_norm'], 'm_ffn1_w_in': out['m_ffn1_w_in'], 'm_ffn1_w_out': out['m_ffn1_w_out'], 'm_mix_norm': out['m_mix_norm'], 'm_mix_w_in': out['m_mix_w_in'], 'm_conv_w': out['m_conv_w'], 'm_conv_b': out['m_conv_b'], 'm_conv_ln_g': out['m_conv_ln_g'], 'm_conv_ln_b': out['m_conv_ln_b'], 'm_ret_norm_g': out['m_ret_norm_g'], 'm_mix_w_out': out['m_mix_w_out'], 'm_ffn2_norm': out['m_ffn2_norm'], 'm_ffn2_w_in': out['m_ffn2_w_in'], 'm_ffn2_w_out': out['m_ffn2_w_out'], 'm_final_norm': out['m_final_norm'], 'v_ffn1_norm': out['v_ffn1_norm'], 'v_ffn1_w_in': out['v_ffn1_w_in'], 'v_ffn1_w_out': out['v_ffn1_w_out'], 'v_mix_norm': out['v_mix_norm'], 'v_mix_w_in': out['v_mix_w_in'], 'v_conv_w': out['v_conv_w'], 'v_conv_b': out['v_conv_b'], 'v_conv_ln_g': out['v_conv_ln_g'], 'v_conv_ln_b': out['v_conv_ln_b'], 'v_ret_norm_g': out['v_ret_norm_g'], 'v_mix_w_out': out['v_mix_w_out'], 'v_ffn2_norm': out['v_ffn2_norm'], 'v_ffn2_w_in': out['v_ffn2_w_in'], 'v_ffn2_w_out': out['v_ffn2_w_out'], 'v_final_norm': out['v_final_norm']}


def _loss(weights, diff, rest, loss_target):
    with _jax.named_scope("forward"):
        args = {**rest, TWIN_DIFF_INPUT: diff, **{k: w.astype(_WEIGHT_DTYPES[k]) for k, w in weights.items()}}
        y = _forward(args)
    with _jax.named_scope("loss_head"):
        err = _jnp.square(y.astype(_jnp.float32) - loss_target)
        return 0.5 * _jnp.sum(_jnp.mean(err, axis=-1)) if err.ndim else 0.5 * err


def _adamw(w, g, m, v):
    m = ADAM_B1 * m + (1.0 - ADAM_B1) * g
    v = ADAM_B2 * v + (1.0 - ADAM_B2) * _jnp.square(g)
    m_hat = m / (1.0 - ADAM_B1 ** ADAM_STEP)
    v_hat = v / (1.0 - ADAM_B2 ** ADAM_STEP)
    delta = -ADAM_LR * (m_hat / (_jnp.sqrt(v_hat) + ADAM_EPS) + ADAM_WD * w)
    return delta, m, v


def reference(x, ffn1_norm, ffn1_w_in, ffn1_w_out, mix_norm, mix_w_in, conv_w, conv_b, conv_ln_g, conv_ln_b, ret_norm_g, mix_w_out, ffn2_norm, ffn2_w_in, ffn2_w_out, final_norm, loss_target, m_ffn1_norm, m_ffn1_w_in, m_ffn1_w_out, m_mix_norm, m_mix_w_in, m_conv_w, m_conv_b, m_conv_ln_g, m_conv_ln_b, m_ret_norm_g, m_mix_w_out, m_ffn2_norm, m_ffn2_w_in, m_ffn2_w_out, m_final_norm, v_ffn1_norm, v_ffn1_w_in, v_ffn1_w_out, v_mix_norm, v_mix_w_in, v_conv_w, v_conv_b, v_conv_ln_g, v_conv_ln_b, v_ret_norm_g, v_mix_w_out, v_ffn2_norm, v_ffn2_w_in, v_ffn2_w_out, v_final_norm):
    given = dict(x=x, ffn1_norm=ffn1_norm, ffn1_w_in=ffn1_w_in, ffn1_w_out=ffn1_w_out, mix_norm=mix_norm, mix_w_in=mix_w_in, conv_w=conv_w, conv_b=conv_b, conv_ln_g=conv_ln_g, conv_ln_b=conv_ln_b, ret_norm_g=ret_norm_g, mix_w_out=mix_w_out, ffn2_norm=ffn2_norm, ffn2_w_in=ffn2_w_in, ffn2_w_out=ffn2_w_out, final_norm=final_norm, loss_target=loss_target, m_ffn1_norm=m_ffn1_norm, m_ffn1_w_in=m_ffn1_w_in, m_ffn1_w_out=m_ffn1_w_out, m_mix_norm=m_mix_norm, m_mix_w_in=m_mix_w_in, m_conv_w=m_conv_w, m_conv_b=m_conv_b, m_conv_ln_g=m_conv_ln_g, m_conv_ln_b=m_conv_ln_b, m_ret_norm_g=m_ret_norm_g, m_mix_w_out=m_mix_w_out, m_ffn2_norm=m_ffn2_norm, m_ffn2_w_in=m_ffn2_w_in, m_ffn2_w_out=m_ffn2_w_out, m_final_norm=m_final_norm, v_ffn1_norm=v_ffn1_norm, v_ffn1_w_in=v_ffn1_w_in, v_ffn1_w_out=v_ffn1_w_out, v_mix_norm=v_mix_norm, v_mix_w_in=v_mix_w_in, v_conv_w=v_conv_w, v_conv_b=v_conv_b, v_conv_ln_g=v_conv_ln_g, v_conv_ln_b=v_conv_ln_b, v_ret_norm_g=v_ret_norm_g, v_mix_w_out=v_mix_w_out, v_ffn2_norm=v_ffn2_norm, v_ffn2_w_in=v_ffn2_w_in, v_ffn2_w_out=v_ffn2_w_out, v_final_norm=v_final_norm)
    weights = {n: given[n] for n in TWIN_WEIGHTS}
    shared = {n: given[n] for n in SHARED_INPUTS}
    per_example = {n: given[n] for n in ['x']}
    grad_fn = _jax.value_and_grad(_loss, argnums=(0, 1))

    def one_microbatch(ex, loss_target):
        ex = dict(ex)
        diff = ex.pop(TWIN_DIFF_INPUT)
        return grad_fn(weights, diff, {**shared, **ex}, loss_target)

    if N_MICROBATCH == 1:
        loss, (grad_w, grad_x) = one_microbatch(per_example, given["loss_target"])
    else:
        def body(carry, xs):
            loss_sum, grad_sum = carry
            l_k, (gw_k, gx_k) = one_microbatch(xs[0], xs[1])
            with _jax.named_scope("update"):
                return (loss_sum + l_k, _jax.tree.map(_jnp.add, grad_sum, gw_k)), gx_k

        init = (_jnp.zeros((), _jnp.float32), _jax.tree.map(_jnp.zeros_like, weights))
        (loss, grad_w), grad_x = _jax.lax.scan(body, init, (per_example, given["loss_target"]))
    with _jax.named_scope("update"):
        delta_w, new_m, new_v = {}, {}, {}
        for n in TWIN_WEIGHTS:
            delta_w[n], new_m[n], new_v[n] = _adamw(weights[n], grad_w[n], given["m_" + n], given["v_" + n])
    return (loss, grad_x, *[grad_w[n] for n in TWIN_WEIGHTS], *[delta_w[n] for n in TWIN_WEIGHTS],
            *[new_m[n] for n in TWIN_WEIGHTS], *[new_v[n] for n in TWIN_WEIGHTS])
```

```python
import functools

import numpy as np
import jax
import jax.numpy as jnp
from jax import lax
from jax.experimental import pallas as pl
from jax.experimental.pallas import tpu as pltpu

F32 = jnp.float32
BF16 = jnp.bfloat16

D_MODEL = 1024
D_FF = 2816
N_SHARD = 4
FF_SHARD = 2 * D_FF // N_SHARD
MIX_SHARD = 3072 // N_SHARD
HEAD = 64
SLAB = 256
N_SLAB = 3072 // SLAB
CONV_W = 31
CONV_PAD = 32
CHUNK = 64
EPS = 1e-6
ROPE_BASE = 10000.0
DEPTH = 2

LR, B1, B2, ADAM_EPS, WD, STEP = 0.001, 0.9, 0.999, 1e-08, 0.01, 10

VMEM_LIMIT = 56 * 1024 * 1024


def _params(n_grid, vmem=VMEM_LIMIT):
    return pltpu.CompilerParams(dimension_semantics=("arbitrary",) * n_grid, vmem_limit_bytes=vmem)


def _rw(name, body, grid, ins, in_specs, rows=(), row_specs=(), accs=(), acc_specs=()):
    n_in, n_row = len(ins), len(rows)

    def kern(*refs):
        vals = [r[...] for r in refs[:n_in]]
        row_vals, acc_vals = body(*vals)
        for r, v in zip(refs[n_in:n_in + n_row], row_vals):
            r[...] = v.astype(r.dtype)
        acc_refs = refs[n_in + n_row:]
        if acc_refs:
            first = functools.reduce(jnp.logical_and, [pl.program_id(a) == 0 for a in range(len(grid))])

            @pl.when(first)
            def _():
                for r in acc_refs:
                    r[...] = jnp.zeros(r.shape, r.dtype)

            for r, v in zip(acc_refs, acc_vals):
                r[...] += v.astype(r.dtype)

    return pl.pallas_call(
        kern, name=name, grid=grid, in_specs=list(in_specs),
        out_specs=list(row_specs) + list(acc_specs), out_shape=list(rows) + list(accs),
        compiler_params=_params(len(grid)))(*ins)


def _sds(shape, dtype):
    return jax.ShapeDtypeStruct(shape, dtype)


def _rms(x, g):
    return x * lax.rsqrt(jnp.mean(x * x, axis=-1, keepdims=True) + EPS) * g


def _row_spec(tm, c):
    return pl.BlockSpec((tm, c), lambda i: (i, 0))


def _vec_spec(c):
    return pl.BlockSpec((1, c), lambda i: (0, 0))


def _rms_fwd(name, x, g, tm):
    s, d = x.shape
    return _rw(name, lambda xb, gb: ((_rms(xb, gb),), ()), (s // tm,), [x, g],
               [_row_spec(tm, d), _vec_spec(d)], [_sds((s, d), BF16)], [_row_spec(tm, d)])[0]


def _rms_bwd(name, x, dh, dres, g, tm):
    s, d = x.shape

    def body(xb, dhb, drb, gb):
        _, vjp = jax.vjp(_rms, xb, gb)
        dx, dg = vjp(dhb)
        return (dx + drb,), (dg,)

    return _rw(name, body, (s // tm,), [x, dh, dres, g],
               [_row_spec(tm, d)] * 3 + [_vec_spec(d)], [_sds((s, d), F32)], [_row_spec(tm, d)],
               [_sds((1, d), F32)], [_vec_spec(d)])


def _swiglu(gate, up):
    return jax.nn.silu(gate) * up


def _swiglu_fwd(name, u, tm):
    _, s, c = u.shape
    return _rw(name, lambda ub: ((_swiglu(ub[0:2], ub[2:4]),), ()), (s // tm,), [u],
               [pl.BlockSpec((4, tm, c), lambda i: (0, i, 0))],
               [_sds((2, s, c), BF16)], [pl.BlockSpec((2, tm, c), lambda i: (0, i, 0))])[0]


def _swiglu_bwd(name, u, da, tm):
    _, s, c = u.shape

    def body(ub, dab):
        _, vjp = jax.vjp(_swiglu, ub[0:2], ub[2:4])
        dg, du = vjp(dab.astype(F32))
        return (jnp.concatenate([dg, du], axis=0),), ()

    return _rw(name, body, (s // tm,), [u, da],
               [pl.BlockSpec((4, tm, c), lambda i: (0, i, 0)), pl.BlockSpec((2, tm, c), lambda i: (0, i, 0))],
               [_sds((4, s, c), BF16)], [pl.BlockSpec((4, tm, c), lambda i: (0, i, 0))])[0]


def _ln_silu(y, g, b):
    mu = jnp.mean(y, axis=-1, keepdims=True)
    yc = y - mu
    var = jnp.mean(yc * yc, axis=-1, keepdims=True)
    return jax.nn.silu(yc * lax.rsqrt(var + EPS) * g + b)


def _ln_silu_fwd(name, y, g, b, tm):
    s, c = y.shape
    return _rw(name, lambda yb, gb, bb: ((_ln_silu(yb, gb, bb),), ()), (s // tm,), [y, g, b],
               [_row_spec(tm, c), _vec_spec(c), _vec_spec(c)], [_sds((s, c), BF16)], [_row_spec(tm, c)])[0]


def _ln_silu_bwd(name, y, dcat, g, b, tm):
    s, c = y.shape

    def body(yb, dob, gb, bb):
        _, vjp = jax.vjp(_ln_silu, yb, gb, bb)
        dy, dg, db = vjp(dob)
        return (dy,), (dg, db)

    return _rw(name, body, (s // tm,), [y, dcat, g, b],
               [_row_spec(tm, c), pl.BlockSpec((None, tm, c), lambda i: (0, i, 0)), _vec_spec(c), _vec_spec(c)],
               [_sds((s, c), F32)], [_row_spec(tm, c)],
               [_sds((1, c), F32)] * 2, [_vec_spec(c)] * 2)


def _head_masks():
    lane = np.arange(SLAB) // HEAD
    m = np.zeros((8, SLAB), np.float32)
    for h in range(SLAB // HEAD):
        m[h] = (lane == h)
    return jnp.asarray(m)


def _gated_head_norm(y, gate, g, hm):
    mu = jnp.zeros_like(y)
    for h in range(SLAB // HEAD):
        mu = mu + hm[h:h + 1] * (jnp.sum(y * hm[h:h + 1], axis=-1, keepdims=True) / HEAD)
    yc = y - mu
    var = jnp.zeros_like(y)
    for h in range(SLAB // HEAD):
        var = var + hm[h:h + 1] * (jnp.sum(yc * yc * hm[h:h + 1], axis=-1, keepdims=True) / HEAD)
    return jax.nn.silu(gate) * (yc * lax.rsqrt(var + EPS) * g)


def _slab_spec(tm, j):
    return pl.BlockSpec((None, tm, SLAB), lambda i: (j, i, 0))


def _ghn_fwd(name, y, p32, g, tm):
    s, c = y.shape
    hm = _head_masks()
    return _rw(name, lambda yb, gb, wb, hb: ((_gated_head_norm(yb, gb, wb, hb),), ()), (s // tm,),
               [y, p32, g, hm],
               [_row_spec(tm, c), _slab_spec(tm, 11), _vec_spec(c), pl.BlockSpec((8, c), lambda i: (0, 0))],
               [_sds((s, c), BF16)], [_row_spec(tm, c)])[0]


def _ghn_bwd(name, y, p32, dcat, g, tm):
    s, c = y.shape
    hm = _head_masks()

    def body(yb, gb, dob, wb, hb):
        _, vjp = jax.vjp(lambda a, b_, c_: _gated_head_norm(a, b_, c_, hb), yb, gb, wb)
        dy, dgate, dw = vjp(dob)
        return (dy, dgate), (dw,)

    return _rw(name, body, (s // tm,), [y, p32, dcat, g, hm],
               [_row_spec(tm, c), _slab_spec(tm, 11), _slab_spec(tm, 3), _vec_spec(c),
                pl.BlockSpec((8, c), lambda i: (0, 0))],
               [_sds((s, c), F32)] * 2, [_row_spec(tm, c)] * 2,
               [_sds((1, c), F32)], [_vec_spec(c)])


def _final(name, x, tgt, g, tm):
    s, d = x.shape

    def body(xb, tb, gb):
        yf, vjp = jax.vjp(_rms, xb, gb)
        err = yf - tb
        dx, dg = vjp(err * (1.0 / d))
        part = 0.5 * jnp.sum(jnp.mean(err * err, axis=-1, keepdims=True), axis=0, keepdims=True)
        return (dx,), (dg, jnp.broadcast_to(part, (1, 128)))

    return _rw(name, body, (s // tm,), [x, tgt, g],
               [_row_spec(tm, d), _row_spec(tm, d), _vec_spec(d)],
               [_sds((s, d), F32)], [_row_spec(tm, d)],
               [_sds((1, d), F32), _sds((1, 128), F32)], [_vec_spec(d), _vec_spec(128)])


NN = (((1,), (0,)), ((), ()))
NT = (((1,), (1,)), ((), ()))
TN = (((0,), (0,)), ((), ()))


def _mm(name, a, b, grid, a_spec, b_spec, outs, out_specs, acc_shape, dims, alpha=1.0, res=None, res_spec=None):
    nk = grid[-1]
    n_out = len(outs)

    def kern(*refs):
        a_ref, b_ref = refs[0], refs[1]
        res_ref = refs[2] if res is not None else None
        o_refs = refs[-1 - n_out:-1]
        acc_ref = refs[-1]
        k = pl.program_id(len(grid) - 1)
        part = lax.dot_general(a_ref[...].astype(BF16), b_ref[...].astype(BF16), dims,
                               preferred_element_type=F32)

        @pl.when(k == 0)
        def _():
            acc_ref[...] = part

        @pl.when(k > 0)
        def _():
            acc_ref[...] += part

        @pl.when(k == nk - 1)
        def _():
            r = acc_ref[...] * alpha if alpha != 1.0 else acc_ref[...]
            if res_ref is not None:
                r = r + res_ref[...]
            for o in o_refs:
                o[...] = r.astype(o.dtype)

    ins = [a, b] + ([res] if res is not None else [])
    in_specs = [a_spec, b_spec] + ([res_spec] if res is not None else [])
    return pl.pallas_call(
        kern, name=name, grid=grid, in_specs=in_specs, out_specs=list(out_specs), out_shape=list(outs),
        scratch_shapes=[pltpu.VMEM(acc_shape, F32)], compiler_params=_params(len(grid)))(*ins)


def _proj_in(name, h, w, l, tm):
    s, d = h.shape
    n = w.shape[-1]
    return _mm(name, h, w, (N_SHARD, s // tm, 1),
               pl.BlockSpec((tm, d), lambda b, i, k: (i, 0)),
               pl.BlockSpec((None, None, d, n), lambda b, i, k: (l, b, 0, 0)),
               [_sds((N_SHARD, s, n), F32)], [pl.BlockSpec((None, tm, n), lambda b, i, k: (b, i, 0))],
               (tm, n), NN)[0]


def _proj_mix(name, h, w, l, tm):
    s, d = h.shape
    per = MIX_SHARD // SLAB
    return _mm(name, h, w, (N_SLAB, s // tm, 1),
               pl.BlockSpec((tm, d), lambda j, i, k: (i, 0)),
               pl.BlockSpec((None, None, d, SLAB), lambda j, i, k: (l, j // per, 0, j % per)),
               [_sds((N_SLAB, s, SLAB), F32), _sds((N_SLAB, s, SLAB), BF16)],
               [pl.BlockSpec((None, tm, SLAB), lambda j, i, k: (j, i, 0))] * 2,
               (tm, SLAB), NN)


def _proj_out(name, a, w, l, res, alpha, tm):
    nk, s, r = a.shape
    d = w.shape[-1]
    return _mm(name, a, w, (s // tm, nk),
               pl.BlockSpec((None, tm, r), lambda i, k: (k, i, 0)),
               pl.BlockSpec((None, None, r, d), lambda i, k: (l, k, 0, 0)),
               [_sds((s, d), F32)], [pl.BlockSpec((tm, d), lambda i, k: (i, 0))],
               (tm, d), NN, alpha=alpha, res=res, res_spec=pl.BlockSpec((tm, d), lambda i, k: (i, 0)))[0]


def _back_out(name, dy, w, l, alpha, tm, out_dtype):
    s, d = dy.shape
    nk, r = w.shape[1], w.shape[2]
    return _mm(name, dy, w, (nk, s // tm, 1),
               pl.BlockSpec((tm, d), lambda b, i, k: (i, 0)),
               pl.BlockSpec((None, None, r, d), lambda b, i, k: (l, b, 0, 0)),
               [_sds((nk, s, r), out_dtype)], [pl.BlockSpec((None, tm, r), lambda b, i, k: (b, i, 0))],
               (tm, r), NT, alpha=alpha)[0]


def _back_in(name, du, w, l, tm, slab=None):
    nk, s, n = du.shape
    d = w.shape[2]
    if slab is None:
        b_spec = pl.BlockSpec((None, None, d, n), lambda i, k: (l, k, 0, 0))
    else:
        per = MIX_SHARD // SLAB
        b_spec = pl.BlockSpec((None, None, d, SLAB), lambda i, k: (l, k // per, 0, k % per))
    return _mm(name, du, w, (s // tm, nk),
               pl.BlockSpec((None, tm, n), lambda i, k: (k, i, 0)), b_spec,
               [_sds((s, d), F32)], [pl.BlockSpec((tm, d), lambda i, k: (i, 0))],
               (tm, d), NT)[0]


def _grad_in(name, h, du, ts, slab=False):
    s, d = h.shape
    nb, _, n = du.shape
    if not slab:
        out, out_spec = _sds((nb, d, n), BF16), pl.BlockSpec((None, d, n), lambda b, i, k: (b, 0, 0))
    else:
        per = MIX_SHARD // SLAB
        out = _sds((N_SHARD, d, MIX_SHARD), BF16)
        out_spec = pl.BlockSpec((None, d, SLAB), lambda b, i, k: (b // per, 0, b % per))
    return _mm(name, h, du, (nb, 1, s // ts),
               pl.BlockSpec((ts, d), lambda b, i, k: (k, 0)),
               pl.BlockSpec((None, ts, n), lambda b, i, k: (b, k, 0)),
               [out], [out_spec], (d, n), TN)[0]


def _grad_out(name, a, dy, alpha, ts):
    nb, s, r = a.shape
    d = dy.shape[1]
    return _mm(name, a, dy, (nb, 1, s // ts),
               pl.BlockSpec((None, ts, r), lambda b, i, k: (b, k, 0)),
               pl.BlockSpec((ts, d), lambda b, i, k: (k, 0)),
               [_sds((nb, r, d), BF16)], [pl.BlockSpec((None, r, d), lambda b, i, k: (b, 0, 0))],
               (r, d), TN, alpha=alpha)[0]


CONV_TILE = 256


def _shifted(win, off, rows):
    n = win.shape[0]
    return pltpu.roll(win, (n - off) % n, 0)[0:rows] if off % n else win[0:rows]


def _conv_fwd(name, p32, w, bias):
    s = p32.shape[1]
    cb = 128
    nt = s // CONV_TILE

    def kern(a_ref, b_ref, w_ref, bias_ref, y_ref, vpad):
        vpad[0:CONV_PAD, :] = jnp.zeros((CONV_PAD, cb), F32)

        def fill(i, c):
            r = pl.multiple_of(i * CONV_TILE, CONV_TILE)
            vpad[pl.ds(CONV_PAD + r, CONV_TILE), :] = (
                a_ref[pl.ds(r, CONV_TILE), :] * jax.nn.sigmoid(b_ref[pl.ds(r, CONV_TILE), :]))
            return c

        lax.fori_loop(0, nt, fill, 0)

        def tile(i, c):
            r = pl.multiple_of(i * CONV_TILE, CONV_TILE)
            win = vpad[pl.ds(r, CONV_TILE + CONV_PAD), :]
            acc = jnp.broadcast_to(bias_ref[...], (CONV_TILE, cb))
            for j in range(CONV_W):
                acc = acc + w_ref[j:j + 1, :] * _shifted(win, j + 2, CONV_TILE)
            y_ref[pl.ds(r, CONV_TILE), :] = acc
            return c

        lax.fori_loop(0, nt, tile, 0)

    return pl.pallas_call(
        kern, name=name, grid=(SLAB // cb,),
        in_specs=[pl.BlockSpec((None, s, cb), lambda c: (0, 0, c)),
                  pl.BlockSpec((None, s, cb), lambda c: (1, 0, c)),
                  pl.BlockSpec((CONV_W, cb), lambda c: (0, c)),
                  pl.BlockSpec((1, cb), lambda c: (0, c))],
        out_specs=pl.BlockSpec((s, cb), lambda c: (0, c)),
        out_shape=_sds((s, SLAB), F32),
        scratch_shapes=[pltpu.VMEM((s + CONV_PAD, cb), F32)],
        compiler_params=_params(1))(p32, p32, w, bias)


def _conv_bwd(name, p32, w, dy):
    s = p32.shape[1]
    cb = 128
    nt = s // CONV_TILE

    def kern(a_ref, b_ref, w_ref, dy_ref, da_ref, db_ref, dw_ref, dbias_ref, vpad, dpad):
        vpad[0:CONV_PAD, :] = jnp.zeros((CONV_PAD, cb), F32)
        dpad[s:s + CONV_PAD, :] = jnp.zeros((CONV_PAD, cb), F32)
        dw_ref[...] = jnp.zeros((CONV_PAD, cb), F32)
        dbias_ref[...] = jnp.zeros((1, cb), F32)

        def fill(i, c):
            r = pl.multiple_of(i * CONV_TILE, CONV_TILE)
            vpad[pl.ds(CONV_PAD + r, CONV_TILE), :] = (
                a_ref[pl.ds(r, CONV_TILE), :] * jax.nn.sigmoid(b_ref[pl.ds(r, CONV_TILE), :]))
            dpad[pl.ds(r, CONV_TILE), :] = dy_ref[pl.ds(r, CONV_TILE), :]
            return c

        lax.fori_loop(0, nt, fill, 0)

        def tile(i, c):
            r = pl.multiple_of(i * CONV_TILE, CONV_TILE)
            dwin = dpad[pl.ds(r, CONV_TILE + CONV_PAD), :]
            vwin = vpad[pl.ds(r, CONV_TILE + CONV_PAD), :]
            dyt = dwin[0:CONV_TILE]
            dv = jnp.zeros((CONV_TILE, cb), F32)
            for j in range(CONV_W):
                dv = dv + w_ref[j:j + 1, :] * _shifted(dwin, CONV_W - 1 - j, CONV_TILE)
                dw_ref[j:j + 1, :] += jnp.sum(dyt * _shifted(vwin, j + 2, CONV_TILE), axis=0, keepdims=True)
            dbias_ref[...] += jnp.sum(dyt, axis=0, keepdims=True)
            a = a_ref[pl.ds(r, CONV_TILE), :]
            sg = jax.nn.sigmoid(b_ref[pl.ds(r, CONV_TILE), :])
            da_ref[pl.ds(r, CONV_TILE), :] = dv * sg
            db_ref[pl.ds(r, CONV_TILE), :] = dv * a * sg * (1.0 - sg)
            return c

        lax.fori_loop(0, nt, tile, 0)

    col = pl.BlockSpec((s, cb), lambda c: (0, c))
    return pl.pallas_call(
        kern, name=name, grid=(SLAB // cb,),
        in_specs=[pl.BlockSpec((None, s, cb), lambda c: (0, 0, c)),
                  pl.BlockSpec((None, s, cb), lambda c: (1, 0, c)),
                  pl.BlockSpec((CONV_W, cb), lambda c: (0, c)), col],
        out_specs=[col, col, pl.BlockSpec((CONV_PAD, cb), lambda c: (0, c)), pl.BlockSpec((1, cb), lambda c: (0, c))],
        out_shape=[_sds((s, SLAB), F32), _sds((s, SLAB), F32), _sds((CONV_PAD, SLAB), F32), _sds((1, SLAB), F32)],
        scratch_shapes=[pltpu.VMEM((s + CONV_PAD, cb), F32), pltpu.VMEM((s + CONV_PAD, cb), F32)],
        compiler_params=_params(1))(p32, p32, w, dy)


SB_BLOCK = 256


def _split_dot(x, m):
    hi = x.astype(BF16)
    lo = (x - hi.astype(F32)).astype(BF16)
    return (jnp.dot(hi, m, preferred_element_type=F32) + jnp.dot(lo, m, preferred_element_type=F32))


def _sb_logits(qm, k, tri):
    z = lax.dot_general(qm, k, NT, preferred_element_type=F32)
    lb = jnp.minimum(z, 0.0) - jnp.log(1.0 + jnp.exp(-jnp.abs(z)))
    ln = lb - z
    if tri is not None:
        ln = jnp.where(tri, ln, 0.0)
    return lb, ln


def _sb_fwd(name, p16):
    s = p16.shape[1]
    bq = min(SB_BLOCK, s)
    nq = s // bq

    def kern(q_ref, k_ref, v_ref, o_ref, acc_ref, r_ref):
        qi, hh = pl.program_id(1), pl.program_id(2)
        hmask = lax.broadcasted_iota(jnp.int32, (1, SLAB), 1) // HEAD == hh
        qm = jnp.where(hmask, q_ref[...].astype(F32) * (HEAD ** -0.5), 0.0).astype(BF16)
        row = lax.broadcasted_iota(jnp.int32, (bq, bq), 0)
        col = lax.broadcasted_iota(jnp.int32, (bq, bq), 1)
        after = (row > col).astype(BF16)
        tri = col < row
        acc_ref[...] = jnp.zeros((bq, SLAB), F32)
        r_ref[...] = jnp.zeros((bq, 128), F32)

        def tile(kb, masked):
            rows = pl.ds(pl.multiple_of(kb * bq, bq), bq)
            lb, ln = _sb_logits(qm, k_ref[rows, :], tri if masked else None)
            w = jnp.exp(lb + _split_dot(ln, after) + r_ref[:, 0:1])
            if masked:
                w = jnp.where(tri, w, 0.0)
            acc_ref[...] += jnp.dot(w.astype(BF16), v_ref[rows, :], preferred_element_type=F32)
            r_ref[...] += jnp.sum(ln, axis=1, keepdims=True)

        tile(qi, True)

        def step(i, c):
            tile(qi - i, False)
            return c

        lax.fori_loop(1, qi + 1, step, 0)
        part = jnp.where(hmask, acc_ref[...], 0.0)

        @pl.when(hh == 0)
        def _():
            o_ref[...] = part

        @pl.when(hh > 0)
        def _():
            o_ref[...] += part

    return pl.pallas_call(
        kern, name=name, grid=(2, nq, SLAB // HEAD),
        in_specs=[pl.BlockSpec((None, bq, SLAB), lambda g, i, h: (2 + g, i, 0)),
                  pl.BlockSpec((None, s, SLAB), lambda g, i, h: (4 + g, 0, 0)),
                  pl.BlockSpec((None, s, SLAB), lambda g, i, h: (6 + g, 0, 0))],
        out_specs=pl.BlockSpec((None, bq, SLAB), lambda g, i, h: (g, i, 0)),
        out_shape=_sds((2, s, SLAB), F32),
        scratch_shapes=[pltpu.VMEM((bq, SLAB), F32), pltpu.VMEM((bq, 128), F32)],
        compiler_params=_params(3))(p16, p16, p16)


def _sb_bwd(name, p16, o, dcat):
    s = p16.shape[1]
    bq = min(SB_BLOCK, s)
    nq = s // bq
    nh = SLAB // HEAD

    def kern(q_ref, k_ref, v_ref, o_ref, do_ref, dq_ref, dk_hbm, dv_hbm, dk_acc, dv_acc, dq_acc, r_ref, c_ref, sem):
        g, qi, hh = pl.program_id(0), pl.program_id(1), pl.program_id(2)

        @pl.when(jnp.logical_and(qi == 0, hh == 0))
        def _():
            dk_acc[...] = jnp.zeros((s, SLAB), F32)
            dv_acc[...] = jnp.zeros((s, SLAB), F32)

        hmask = lax.broadcasted_iota(jnp.int32, (1, SLAB), 1) // HEAD == hh
        qm = jnp.where(hmask, q_ref[...].astype(F32) * (HEAD ** -0.5), 0.0).astype(BF16)
        do = jnp.where(hmask, do_ref[...], 0.0)
        dom = do.astype(BF16)
        total = jnp.sum(dom.astype(F32) * o_ref[...], axis=1, keepdims=True)
        row = lax.broadcasted_iota(jnp.int32, (bq, bq), 0)
        col = lax.broadcasted_iota(jnp.int32, (bq, bq), 1)
        after = (row > col).astype(BF16)
        from_ = (row >= col).astype(BF16)
        tri = col < row
        dq_acc[...] = jnp.zeros((bq, SLAB), F32)
        r_ref[...] = jnp.zeros((bq, 128), F32)
        c_ref[...] = jnp.zeros((bq, 128), F32)

        def tile(kb, masked):
            rows = pl.ds(pl.multiple_of(kb * bq, bq), bq)
            k, v = k_ref[rows, :], v_ref[rows, :]
            lb, ln = _sb_logits(qm, k, tri if masked else None)
            w = jnp.exp(lb + _split_dot(ln, after) + r_ref[:, 0:1])
            if masked:
                w = jnp.where(tri, w, 0.0)
            wb = w.astype(BF16)
            dl = wb.astype(F32) * lax.dot_general(dom, v, NT, preferred_element_type=F32)
            before = total - (_split_dot(dl, from_) + c_ref[:, 0:1])
            dz = dl - jnp.exp(lb) * (dl + before)
            if masked:
                dz = jnp.where(tri, dz, 0.0)
            dzb = dz.astype(BF16)
            dq_acc[...] += jnp.dot(dzb, k, preferred_element_type=F32)
            dk_acc[rows, :] += lax.dot_general(dzb, qm, TN, preferred_element_type=F32)
            dv_acc[rows, :] += lax.dot_general(wb, dom, TN, preferred_element_type=F32)
            r_ref[...] += jnp.sum(ln, axis=1, keepdims=True)
            c_ref[...] += jnp.sum(dl, axis=1, keepdims=True)

        tile(qi, True)

        def step(i, c):
            tile(qi - i, False)
            return c

        lax.fori_loop(1, qi + 1, step, 0)
        part = jnp.where(hmask, dq_acc[...] * (HEAD ** -0.5), 0.0)

        @pl.when(hh == 0)
        def _():
            dq_ref[...] = part

        @pl.when(hh > 0)
        def _():
            dq_ref[...] += part

        @pl.when(jnp.logical_and(qi == nq - 1, hh == nh - 1))
        def _():
            ck = pltpu.make_async_copy(dk_acc, dk_hbm.at[g], sem.at[0])
            cv = pltpu.make_async_copy(dv_acc, dv_hbm.at[g], sem.at[1])
            ck.start()
            cv.start()
            ck.wait()
            cv.wait()

    blk = lambda j0: pl.BlockSpec((None, bq, SLAB), lambda g, i, h: (j0 + g, i, 0))
    full = lambda j0: pl.BlockSpec((None, s, SLAB), lambda g, i, h: (j0 + g, 0, 0))
    return pl.pallas_call(
        kern, name=name, grid=(2, nq, nh),
        in_specs=[blk(2), full(4), full(6), blk(0), blk(1)],
        out_specs=[blk(0), pl.BlockSpec(memory_space=pl.ANY), pl.BlockSpec(memory_space=pl.ANY)],
        out_shape=[_sds((2, s, SLAB), F32)] * 3,
        scratch_shapes=[pltpu.VMEM((s, SLAB), F32), pltpu.VMEM((s, SLAB), F32), pltpu.VMEM((bq, SLAB), F32),
                        pltpu.VMEM((bq, 128), F32), pltpu.VMEM((bq, 128), F32), pltpu.SemaphoreType.DMA((2,))],
        compiler_params=_params(3))(p16, p16, p16, o, dcat)


RET_BLOCK = 256


def _ret_tables(s, bl):
    nh = SLAB // HEAD
    lane_h = np.arange(SLAB) // HEAD
    log_gamma = np.log1p(-np.exp2(-5.0 - np.arange(nh, dtype=np.float64)))
    lg_lane = log_gamma[lane_h]
    half = HEAD // 2
    inv = 1.0 / (ROPE_BASE ** (np.arange(half, dtype=np.float64) / half))
    ang = np.arange(s, dtype=np.float64)[:, None] * inv[None, :]
    within = np.arange(SLAB) % HEAD
    cos = np.cos(ang)[:, within % half]
    sin = np.sin(ang)[:, within % half] * np.where(within < half, -1.0, 1.0)[None, :]
    perm = np.zeros((SLAB, SLAB))
    partner = np.where(within < half, np.arange(SLAB) + half, np.arange(SLAB) - half)
    perm[partner, np.arange(SLAB)] = 1.0
    i = np.arange(bl)
    diff = i[:, None] - i[None, :]
    same = (i[:, None] // CHUNK) == (i[None, :] // CHUNK)
    earlier = (i[None, :] // CHUNK) < (i[:, None] // CHUNK)
    decay = np.zeros((nh, bl, bl))
    for h in range(nh):
        decay[h] = np.where(same, np.exp(log_gamma[h] * np.abs(diff)),
                            np.where(earlier, np.exp(log_gamma[h] * diff), 0.0))
    qd = np.exp(lg_lane[None, :] * (i[:, None] + 1.0))
    kd = np.exp(lg_lane[None, :] * (bl - 1.0 - i[:, None]))
    gam = np.exp(lg_lane * bl)[:, None] * np.ones((1, SLAB))
    bd = (lane_h[:, None] == lane_h[None, :]).astype(np.float64)
    f = lambda a: jnp.asarray(a, F32)
    return f(cos), f(sin), f(perm), f(decay), f(qd), f(kd), f(gam), f(bd)


def _ret_block(q, k, v, state, cos, sin, perm, decay, qd, kd, gam, bd, hm):
    qr = (q * cos + jnp.dot(q, perm, preferred_element_type=F32) * sin) * (HEAD ** -0.5)
    kr = k * cos + jnp.dot(k, perm, preferred_element_type=F32) * sin
    y = jnp.dot(qr * qd, state, preferred_element_type=F32)
    for h in range(SLAB // HEAD):
        m = hm[h:h + 1]
        sc = lax.dot_general(qr * m, kr, NT, preferred_element_type=F32) * decay[h]
        y = y + jnp.dot(sc, v * m, preferred_element_type=F32)
    new_state = gam * state + lax.dot_general(kr * kd, v, TN, preferred_element_type=F32) * bd
    return y, new_state


def _ret_specs(s, bl, rev):
    nb = s // bl
    pos = (lambda n: nb - 1 - n) if rev else (lambda n: n)
    slab = lambda j: pl.BlockSpec((None, bl, SLAB), lambda n: (j, pos(n), 0))
    const2 = lambda r: pl.BlockSpec((r, SLAB), lambda n: (0, 0))
    tab = [pl.BlockSpec((bl, SLAB), lambda n: (pos(n), 0))] * 2 + [
        const2(SLAB), pl.BlockSpec((SLAB // HEAD, bl, bl), lambda n: (0, 0, 0)),
        const2(bl), const2(bl), const2(SLAB), const2(SLAB), const2(8)]
    return nb, pos, slab, tab


def _ret_fwd(name, p32):
    s = p32.shape[1]
    bl = min(RET_BLOCK, s)
    nb, pos, slab, tab = _ret_specs(s, bl, False)
    tables = _ret_tables(s, bl) + (_head_masks(),)

    def kern(q_ref, k_ref, v_ref, *rest):
        t_refs, (y_ref, st_ref, state) = rest[:9], rest[9:]

        @pl.when(pl.program_id(0) == 0)
        def _():
            state[...] = jnp.zeros((SLAB, SLAB), F32)

        st_ref[...] = state[...]
        y, new = _ret_block(q_ref[...], k_ref[...], v_ref[...], state[...], *[t[...] for t in t_refs])
        y_ref[...] = y
        state[...] = new

    return pl.pallas_call(
        kern, name=name, grid=(nb,), in_specs=[slab(8), slab(9), slab(10)] + tab,
        out_specs=[pl.BlockSpec((bl, SLAB), lambda n: (n, 0)), pl.BlockSpec((None, SLAB, SLAB), lambda n: (n, 0, 0))],
        out_shape=[_sds((s, SLAB), F32), _sds((nb, SLAB, SLAB), F32)],
        scratch_shapes=[pltpu.VMEM((SLAB, SLAB), F32)], compiler_params=_params(1))(p32, p32, p32, *tables)


def _ret_bwd(name, p32, states, dy):
    s = p32.shape[1]
    bl = min(RET_BLOCK, s)
    nb, pos, slab, tab = _ret_specs(s, bl, True)
    tables = _ret_tables(s, bl) + (_head_masks(),)
    rowblk = pl.BlockSpec((bl, SLAB), lambda n: (pos(n), 0))

    def kern(q_ref, k_ref, v_ref, st_ref, dy_ref, *rest):
        t_refs, (dq_ref, dk_ref, dv_ref, dstate) = rest[:9], rest[9:]

        @pl.when(pl.program_id(0) == 0)
        def _():
            dstate[...] = jnp.zeros((SLAB, SLAB), F32)

        tv = [t[...] for t in t_refs]
        _, vjp = jax.vjp(lambda a, b, c, d: _ret_block(a, b, c, d, *tv),
                         q_ref[...], k_ref[...], v_ref[...], st_ref[...])
        dq, dk, dv, ds = vjp((dy_ref[...], dstate[...]))
        dq_ref[...] = dq
        dk_ref[...] = dk
        dv_ref[...] = dv
        dstate[...] = ds

    return pl.pallas_call(
        kern, name=name, grid=(nb,),
        in_specs=[slab(8), slab(9), slab(10), pl.BlockSpec((None, SLAB, SLAB), lambda n: (pos(n), 0, 0)), rowblk] + tab,
        out_specs=[rowblk] * 3, out_shape=[_sds((s, SLAB), F32)] * 3,
        scratch_shapes=[pltpu.VMEM((SLAB, SLAB), F32)], compiler_params=_params(1))(p32, p32, p32, states, dy, *tables)


TM = 512
TM_RW = 256
TM_FF = 128


def _ffn_fwd(tag, x, g, w_in, w_out, l):
    h = _rms_fwd(tag + "_rms", x, g, TM_RW)
    u = _proj_in(tag + "_in", h, w_in, l, TM)
    a = _swiglu_fwd(tag + "_act", u, TM_FF)
    w_out2 = w_out.reshape(DEPTH, 2, FF_SHARD, D_MODEL)
    xn = _proj_out(tag + "_out", a, w_out2, l, x, 0.5, TM)
    return xn, (x, h, u, a)


def _ffn_bwd(tag, saved, dxn, g, w_in, w_out, l):
    x, h, u, a = saved
    w_out2 = w_out.reshape(DEPTH, 2, FF_SHARD, D_MODEL)
    da = _back_out(tag + "_dact", dxn, w_out2, l, 0.5, TM, BF16)
    dw_out = _grad_out(tag + "_dwout", a, dxn, 0.5, TM)
    du = _swiglu_bwd(tag + "_dswi", u, da, TM_FF)
    dh = _back_in(tag + "_dh", du, w_in, l, TM)
    dw_in = _grad_in(tag + "_dwin", h, du, TM)
    dx, dg = _rms_bwd(tag + "_drms", x, dh, dxn, g, TM_RW)
    return dx, dg, dw_in, dw_out.reshape(N_SHARD, D_FF // N_SHARD, D_MODEL)


def _mix_fwd(tag, x, sm, w_in, w_out, l):
    h = _rms_fwd(tag + "_rms", x, sm["mix_norm"][l:l + 1], TM_RW)
    p32, p16 = _proj_mix(tag + "_in", h, w_in, l, TM)
    ypre = _conv_fwd(tag + "_conv", p32, sm["conv_w"][l], sm["conv_b"][l:l + 1])
    yconv = _ln_silu_fwd(tag + "_ln", ypre, sm["conv_ln_g"][l:l + 1], sm["conv_ln_b"][l:l + 1], TM_RW)
    osb = _sb_fwd(tag + "_sb", p16)
    yr, states = _ret_fwd(tag + "_ret", p32)
    yret = _ghn_fwd(tag + "_ghn", yr, p32, sm["ret_norm_g"][l:l + 1], TM_RW)
    ycat = jnp.stack([yconv, osb[0].astype(BF16), osb[1].astype(BF16), yret])
    xn = _proj_out(tag + "_out", ycat, w_out, l, x, 1.0, TM)
    return xn, (x, h, p32, p16, ypre, osb, yr, states, ycat)


def _mix_bwd(tag, saved, dxn, sm, w_in, w_out, l):
    x, h, p32, p16, ypre, osb, yr, states, ycat = saved
    dcat = _back_out(tag + "_dcat", dxn, w_out, l, 1.0, TM, F32)
    dw_out = _grad_out(tag + "_dwout", ycat, dxn, 1.0, TM)
    dypre, dlg, dlb = _ln_silu_bwd(tag + "_dln", ypre, dcat, sm["conv_ln_g"][l:l + 1], sm["conv_ln_b"][l:l + 1], TM_RW)
    da, db, dcw, dcb = _conv_bwd(tag + "_dconv", p32, sm["conv_w"][l], dypre)
    dq, dk, dv = _sb_bwd(tag + "_dsb", p16, osb, dcat)
    dyr, dgate, drg = _ghn_bwd(tag + "_dghn", yr, p32, dcat, sm["ret_norm_g"][l:l + 1], TM_RW)
    dqr, dkr, dvr = _ret_bwd(tag + "_dret", p32, states, dyr)
    dp = jnp.stack([da, db, dq[0], dq[1], dk[0], dk[1], dv[0], dv[1], dqr, dkr, dvr, dgate]).astype(BF16)
    dh = _back_in(tag + "_dh", dp, w_in, l, TM, slab=True)
    dw_in = _grad_in(tag + "_dwin", h, dp, TM, slab=True)
    dx, dg = _rms_bwd(tag + "_drms", x, dh, dxn, sm["mix_norm"][l:l + 1], TM_RW)
    small = dict(mix_norm=dg, conv_w=dcw[0:CONV_W], conv_b=dcb, conv_ln_g=dlg, conv_ln_b=dlb, ret_norm_g=drg)
    return dx, small, dw_in, dw_out


def _local_step(x, tgt, wt, sm):
    saved = []
    for l in range(DEPTH):
        x, s1 = _ffn_fwd(f"l{l}f1", x, sm["ffn1_norm"][l:l + 1], wt["ffn1_w_in"], wt["ffn1_w_out"], l)
        x, s2 = _mix_fwd(f"l{l}mx", x, sm, wt["mix_w_in"], wt["mix_w_out"], l)
        x, s3 = _ffn_fwd(f"l{l}f2", x, sm["ffn2_norm"][l:l + 1], wt["ffn2_w_in"], wt["ffn2_w_out"], l)
        saved.append((s1, s2, s3))
    dx, dfinal, loss = _final("final", x, tgt, sm["final_norm"][None, :], TM_RW)
    big = [None] * DEPTH
    small = [None] * DEPTH
    for l in reversed(range(DEPTH)):
        s1, s2, s3 = saved[l]
        dx, dg3, dwi3, dwo3 = _ffn_bwd(f"l{l}f2", s3, dx, sm["ffn2_norm"][l:l + 1], wt["ffn2_w_in"], wt["ffn2_w_out"], l)
        dx, sml, dwi2, dwo2 = _mix_bwd(f"l{l}mx", s2, dx, sm, wt["mix_w_in"], wt["mix_w_out"], l)
        dx, dg1, dwi1, dwo1 = _ffn_bwd(f"l{l}f1", s1, dx, sm["ffn1_norm"][l:l + 1], wt["ffn1_w_in"], wt["ffn1_w_out"], l)
        big[l] = dict(ffn1_w_in=dwi1, ffn1_w_out=dwo1, mix_w_in=dwi2, mix_w_out=dwo2, ffn2_w_in=dwi3, ffn2_w_out=dwo3)
        sml.update(ffn1_norm=dg1, ffn2_norm=dg3)
        small[l] = sml
    return loss, dx, big, small, dfinal


MESH = pl.DeviceIdType.MESH
ANY = pl.BlockSpec(memory_space=pl.ANY)
BIG = ("ffn1_w_in", "ffn1_w_out", "mix_w_in", "mix_w_out", "ffn2_w_in", "ffn2_w_out")


def _place():
    x, y, c = lax.axis_index("x"), lax.axis_index("y"), lax.axis_index("c")
    chips = [(1 - x, y), (x, 1 - y), (1 - x, 1 - y)]
    return x, y, c, chips


def _gather_weights(w16):
    n = len(w16)

    def kern(*refs):
        src, dst = refs[:n], refs[n:2 * n]
        send, recv, local = refs[2 * n:]
        x, y, c, chips = _place()
        mine = 2 * x + y
        firsts, passes, locals_ = [], [], []
        for a in range(n):
            h = src[a].shape[1] // 2
            half = pl.ds(c * h, h)
            other = pl.ds((1 - c) * h, h)
            lc = pltpu.make_async_copy(src[a], dst[a].at[:, mine], local.at[a])
            lc.start()
            locals_.append(lc)
            for j, (cx, cy) in enumerate(chips):
                cp = pltpu.make_async_remote_copy(
                    src_ref=src[a].at[:, half], dst_ref=dst[a].at[:, mine, half],
                    send_sem=send.at[6 * a + j], recv_sem=recv.at[6 * a + j],
                    device_id=(cx, cy, c), device_id_type=MESH)
                cp.start()
                firsts.append(cp)
        for a in range(n):
            h = src[a].shape[1] // 2
            half = pl.ds(c * h, h)
            for j, (cx, cy) in enumerate(chips):
                theirs = dst[a].at[:, 2 * cx + cy, half]
                pltpu.make_async_remote_copy(
                    src_ref=theirs, dst_ref=theirs, send_sem=send.at[6 * a + j], recv_sem=recv.at[6 * a + j],
                    device_id=(cx, cy, c), device_id_type=MESH).wait_recv()
                fw = pltpu.make_async_remote_copy(
                    src_ref=theirs, dst_ref=theirs, send_sem=send.at[6 * a + 3 + j], recv_sem=recv.at[6 * a + 3 + j],
                    device_id=(x, y, 1 - c), device_id_type=MESH)
                fw.start()
                passes.append(fw)
        for a in range(n):
            h = src[a].shape[1] // 2
            other = pl.ds((1 - c) * h, h)
            for j, (cx, cy) in enumerate(chips):
                got = dst[a].at[:, 2 * cx + cy, other]
                pltpu.make_async_remote_copy(
                    src_ref=got, dst_ref=got, send_sem=send.at[6 * a + 3 + j], recv_sem=recv.at[6 * a + 3 + j],
                    device_id=(x, y, 1 - c), device_id_type=MESH).wait_recv()
        for cp in firsts + passes:
            cp.wait_send()
        for lc in locals_:
            lc.wait()

    outs = [_sds((w.shape[0], N_SHARD) + w.shape[1:], w.dtype) for w in w16]
    return pl.pallas_call(
        kern, name="gather_weights", in_specs=[ANY] * n, out_specs=[ANY] * n, out_shape=outs,
        scratch_shapes=[pltpu.SemaphoreType.DMA((6 * n,)), pltpu.SemaphoreType.DMA((6 * n,)),
                        pltpu.SemaphoreType.DMA((n,))])(*w16)


def _pair_exchange(grads):
    n = len(grads)

    def kern(*refs):
        src, mine_o, got_o = refs[:n], refs[n:2 * n], refs[2 * n:3 * n]
        send, recv, local = refs[3 * n:]
        x, y, c, _ = _place()
        cps, lcs = [], []
        for a in range(n):
            h = src[a].shape[1] // 2
            lc = pltpu.make_async_copy(src[a].at[:, pl.ds(c * h, h)], mine_o[a], local.at[a])
            lc.start()
            lcs.append(lc)
            cp = pltpu.make_async_remote_copy(
                src_ref=src[a].at[:, pl.ds((1 - c) * h, h)], dst_ref=got_o[a],
                send_sem=send.at[a], recv_sem=recv.at[a], device_id=(x, y, 1 - c), device_id_type=MESH)
            cp.start()
            cps.append(cp)
        for cp in cps:
            cp.wait()
        for lc in lcs:
            lc.wait()

    halves = [_sds((g.shape[0], g.shape[1] // 2, g.shape[2]), g.dtype) for g in grads]
    out = pl.pallas_call(
        kern, name="pair_exchange", in_specs=[ANY] * n, out_specs=[ANY] * (2 * n), out_shape=halves + halves,
        scratch_shapes=[pltpu.SemaphoreType.DMA((n,)), pltpu.SemaphoreType.DMA((n,)),
                        pltpu.SemaphoreType.DMA((n,))])(*grads)
    return out[:n], out[n:]


def _chip_exchange(sums):
    n = len(sums)

    def kern(*refs):
        src, dst = refs[:n], refs[n:2 * n]
        send, recv, local = refs[2 * n:]
        x, y, c, chips = _place()
        mine = 2 * x + y
        cps, lcs = [], []
        for a in range(n):
            lc = pltpu.make_async_copy(src[a].at[mine], dst[a].at[0], local.at[a])
            lc.start()
            lcs.append(lc)
            for j, (cx, cy) in enumerate(chips):
                cp = pltpu.make_async_remote_copy(
                    src_ref=src[a].at[2 * cx + cy], dst_ref=dst[a].at[1 + j],
                    send_sem=send.at[3 * a + j], recv_sem=recv.at[3 * a + j],
                    device_id=(cx, cy, c), device_id_type=MESH)
                cp.start()
                cps.append(cp)
        for cp in cps:
            cp.wait()
        for lc in lcs:
            lc.wait()

    return pl.pallas_call(
        kern, name="chip_exchange", in_specs=[ANY] * n, out_specs=[ANY] * n,
        out_shape=[_sds(s_.shape, s_.dtype) for s_ in sums],
        scratch_shapes=[pltpu.SemaphoreType.DMA((3 * n,)), pltpu.SemaphoreType.DMA((3 * n,)),
                        pltpu.SemaphoreType.DMA((n,))])(*sums)


def _pair_join(halves):
    flat = [g for per in halves for g in per]
    n, nl = len(flat), len(halves[0])

    def kern(*refs):
        src, dst = refs[:n], refs[n:n + len(halves)]
        send, recv, local = refs[n + len(halves):]
        x, y, c, _ = _place()
        cps, lcs = [], []
        for a in range(n):
            w, l = divmod(a, nl)
            h = src[a].shape[0]
            rows = dst[w].at[l, pl.ds(c * h, h)]
            lc = pltpu.make_async_copy(src[a], rows, local.at[a])
            lc.start()
            lcs.append(lc)
            cp = pltpu.make_async_remote_copy(
                src_ref=src[a], dst_ref=rows, send_sem=send.at[a], recv_sem=recv.at[a],
                device_id=(x, y, 1 - c), device_id_type=MESH)
            cp.start()
            cps.append(cp)
        for a, cp in enumerate(cps):
            cp.wait_send()
            w, l = divmod(a, nl)
            h = src[a].shape[0]
            got = dst[w].at[l, pl.ds((1 - c) * h, h)]
            pltpu.make_async_remote_copy(
                src_ref=src[a], dst_ref=got, send_sem=send.at[a], recv_sem=recv.at[a],
                device_id=(x, y, 1 - c), device_id_type=MESH).wait_recv()
        for lc in lcs:
            lc.wait()

    outs = [_sds((nl, 2 * per[0].shape[0], per[0].shape[1]), F32) for per in halves]
    return pl.pallas_call(
        kern, name="pair_join", in_specs=[ANY] * n, out_specs=[ANY] * len(halves), out_shape=outs,
        scratch_shapes=[pltpu.SemaphoreType.DMA((n,)), pltpu.SemaphoreType.DMA((n,)),
                        pltpu.SemaphoreType.DMA((n,))])(*flat)


def _all_sum(name, v):
    r = v.shape[0]

    def kern(v_ref, o_ref, buf, send, recv):
        x, y, c, _ = _place()
        me = 4 * x + 2 * y + c
        buf[me] = v_ref[...]
        cps = []
        for k in range(1, 8):
            peer = (x ^ (k >> 2), y ^ ((k >> 1) & 1), c ^ (k & 1))
            cp = pltpu.make_async_remote_copy(
                src_ref=v_ref, dst_ref=buf.at[me], send_sem=send.at[k - 1], recv_sem=recv.at[k - 1],
                device_id=peer, device_id_type=MESH)
            cp.start()
            cps.append(cp)
        for k in range(1, 8):
            peer_id = me ^ k
            pltpu.make_async_remote_copy(
                src_ref=v_ref, dst_ref=buf.at[peer_id], send_sem=send.at[k - 1], recv_sem=recv.at[k - 1],
                device_id=(x, y, c), device_id_type=MESH).wait_recv()
        for cp in cps:
            cp.wait_send()
        acc = buf[0]
        for d in range(1, 8):
            acc = acc + buf[d]
        o_ref[...] = acc

    vm = pl.BlockSpec(memory_space=pltpu.VMEM)
    return pl.pallas_call(
        kern, name=name, in_specs=[vm], out_specs=vm, out_shape=_sds((r, 128), F32),
        scratch_shapes=[pltpu.VMEM((8, r, 128), F32), pltpu.SemaphoreType.DMA((7,)),
                        pltpu.SemaphoreType.DMA((7,))])(v)


def _cast_bf16(name, w):
    l, r, c = w.shape
    w2 = w.reshape(l * r, c)
    tr = 64
    out = _rw(name, lambda wb: ((wb,), ()), (l * r // tr,), [w2], [_row_spec(tr, c)],
              [_sds((l * r, c), BF16)], [_row_spec(tr, c)])[0]
    return out.reshape(l, r, c)


def _lead_spec(n, tr, c):
    return pl.BlockSpec((n, tr, c), lambda i: (0, i, 0))


def _add_halves(name, a, b):
    n, h, c = a.shape
    tr = 32
    return _rw(name, lambda ab, bb: ((ab.astype(F32) + bb.astype(F32),), ()), (h // tr,), [a, b],
               [_lead_spec(n, tr, c)] * 2, [_sds((n, h, c), BF16)], [_lead_spec(n, tr, c)])[0]


def _sum_parts(name, p):
    n, h, c = p.shape
    tr = 32

    def body(pb):
        acc = pb[0].astype(F32)
        for j in range(1, n):
            acc = acc + pb[j].astype(F32)
        return (acc,), ()

    return _rw(name, body, (h // tr,), [p], [_lead_spec(n, tr, c)], [_sds((h, c), F32)], [_row_spec(tr, c)])[0]


def _adamw_math(w, g, m, v):
    m = B1 * m + (1.0 - B1) * g
    v = B2 * v + (1.0 - B2) * (g * g)
    m_hat = m / (1.0 - B1 ** STEP)
    v_hat = v / (1.0 - B2 ** STEP)
    delta = -LR * (m_hat / (jnp.sqrt(v_hat) + ADAM_EPS) + WD * w)
    return delta, m, v


def _adamw(name, w, g, m, v):
    r, c = w.shape
    tr = 64 if r % 64 == 0 else 8
    spec = _row_spec(tr, c)
    return _rw(name, lambda *b: (_adamw_math(*b), ()), (r // tr,), [w, g, m, v], [spec] * 4,
               [_sds((r, c), F32)] * 3, [spec] * 3)


SMALL = (("ffn1_norm", (DEPTH, D_MODEL)), ("mix_norm", (DEPTH, D_MODEL)), ("ffn2_norm", (DEPTH, D_MODEL)),
         ("conv_b", (DEPTH, SLAB)), ("conv_ln_g", (DEPTH, SLAB)), ("conv_ln_b", (DEPTH, SLAB)),
         ("ret_norm_g", (DEPTH, SLAB)), ("final_norm", (D_MODEL,)), ("conv_w", (DEPTH, CONV_W, SLAB)))


def _pack(parts, rows):
    flat = jnp.concatenate([p.reshape(-1) for p in parts])
    return jnp.pad(flat, (0, rows * 128 - flat.shape[0])).reshape(rows, 128)


def _unpack(packed, shapes):
    flat = packed.reshape(-1)
    out, off = [], 0
    for shp in shapes:
        n = int(np.prod(shp))
        out.append(flat[off:off + n].reshape(shp))
        off += n
    return out


def kernel(x, ffn1_norm, ffn1_w_in, ffn1_w_out, mix_norm, mix_w_in, conv_w, conv_b, conv_ln_g, conv_ln_b, ret_norm_g, mix_w_out, ffn2_norm, ffn2_w_in, ffn2_w_out, final_norm, loss_target, m_ffn1_norm, m_ffn1_w_in, m_ffn1_w_out, m_mix_norm, m_mix_w_in, m_conv_w, m_conv_b, m_conv_ln_g, m_conv_ln_b, m_ret_norm_g, m_mix_w_out, m_ffn2_norm, m_ffn2_w_in, m_ffn2_w_out, m_final_norm, v_ffn1_norm, v_ffn1_w_in, v_ffn1_w_out, v_mix_norm, v_mix_w_in, v_conv_w, v_conv_b, v_conv_ln_g, v_conv_ln_b, v_ret_norm_g, v_mix_w_out, v_ffn2_norm, v_ffn2_w_in, v_ffn2_w_out, v_final_norm):
    given = dict(locals())
    names = [n for n, _ in SMALL] + list(BIG)
    chip = 2 * lax.axis_index("x") + lax.axis_index("y")
    core = lax.axis_index("c")

    cw_rows = 128
    placed = lax.dynamic_update_slice(jnp.zeros((DEPTH, CONV_W, SLAB), F32), conv_w, (0, 0, chip * HEAD))
    placed = placed * (core == 0).astype(F32)
    conv_w_full = _unpack(_all_sum("gather_conv_w", _pack([placed], cw_rows)), [(DEPTH, CONV_W, SLAB)])[0]

    wt = dict(zip(BIG, _gather_weights([_cast_bf16("cast_" + n, given[n]) for n in BIG])))
    sm = {n: given[n] for n, _ in SMALL}
    sm["conv_w"] = conv_w_full
    loss, dx, big, small, dfinal = _local_step(x[0], loss_target[0], wt, sm)

    grads = [big[l][n] for n in BIG for l in range(DEPTH)]
    mine, theirs = _pair_exchange(grads)
    sums = [_add_halves(f"chipsum{i}", a, b) for i, (a, b) in enumerate(zip(mine, theirs))]
    parts = _chip_exchange(sums)
    halves = [_sum_parts(f"shardsum{i}", p) for i, p in enumerate(parts)]
    g_big = dict(zip(BIG, _pair_join([halves[DEPTH * i:DEPTH * (i + 1)] for i in range(len(BIG))])))

    small_parts = []
    for n, shp in SMALL:
        if n == "final_norm":
            small_parts.append(dfinal)
        else:
            small_parts.append(jnp.stack([small[l][n].reshape(shp[1:]) for l in range(DEPTH)]))
    g_small = dict(zip([n for n, _ in SMALL], _unpack(_all_sum("sum_small", _pack(small_parts, 200)), [s_ for _, s_ in SMALL])))
    g_small["conv_w"] = lax.dynamic_slice(g_small["conv_w"], (0, 0, chip * HEAD), (DEPTH, CONV_W, HEAD))

    grad, delta, new_m, new_v = dict(g_small), {}, {}, {}
    grad.update(g_big)
    for n in BIG:
        l, r, c = given[n].shape
        f = lambda t: t.reshape(l * r, c)
        d_, m_, v_ = _adamw("adamw_" + n, f(given[n]), f(grad[n]), f(given["m_" + n]), f(given["v_" + n]))
        delta[n], new_m[n], new_v[n] = d_.reshape(l, r, c), m_.reshape(l, r, c), v_.reshape(l, r, c)
    snames = [n for n, _ in SMALL]
    shapes = [given[n].shape for n in snames]
    rows = 104
    d_, m_, v_ = _adamw("adamw_small", _pack([given[n] for n in snames], rows), _pack([grad[n] for n in snames], rows),
                        _pack([given["m_" + n] for n in snames], rows), _pack([given["v_" + n] for n in snames], rows))
    for dst, packed in ((delta, d_), (new_m, m_), (new_v, v_)):
        dst.update(zip(snames, _unpack(packed, shapes)))

    total = lax.psum(loss[0, 0], ("x", "y", "c"))
    order = ["ffn1_norm", "ffn1_w_in", "ffn1_w_out", "mix_norm", "mix_w_in", "conv_w", "conv_b", "conv_ln_g",
             "conv_ln_b", "ret_norm_g", "mix_w_out", "ffn2_norm", "ffn2_w_in", "ffn2_w_out", "final_norm"]
    return (total, dx[None], *[grad[n] for n in order], *[delta[n] for n in order],
            *[new_m[n] for n in order], *[new_v[n] for n in order])
```

```python
import functools

import numpy as np
import jax
import jax.numpy as jnp
from jax import lax
from jax.experimental import pallas as pl
from jax.experimental.pallas import tpu as pltpu

F32 = jnp.float32
BF16 = jnp.bfloat16

D_MODEL = 1024
D_FF = 2816
N_SHARD = 4
FF_SHARD = 2 * D_FF // N_SHARD
MIX_SHARD = 3072 // N_SHARD
HEAD = 64
SLAB = 256
N_SLAB = 3072 // SLAB
CONV_W = 31
CONV_PAD = 32
CHUNK = 64
EPS = 1e-6
ROPE_BASE = 10000.0
DEPTH = 2

LR, B1, B2, ADAM_EPS, WD, STEP = 0.001, 0.9, 0.999, 1e-08, 0.01, 10

VMEM_LIMIT = 56 * 1024 * 1024


def _params(n_grid, vmem=VMEM_LIMIT):
    return pltpu.CompilerParams(dimension_semantics=("arbitrary",) * n_grid, vmem_limit_bytes=vmem)


def _rw(name, body, grid, ins, in_specs, rows=(), row_specs=(), accs=(), acc_specs=(), aliases=None):
    n_in, n_row = len(ins), len(rows)
    carried = sorted(aliases) if aliases else []

    def kern(*refs):
        vals = [r[...] for i, r in enumerate(refs[:n_in]) if i not in carried]
        row_vals, acc_vals = body(*vals)
        for r, v in zip(refs[n_in:n_in + n_row], row_vals):
            r[...] = v.astype(r.dtype)
        acc_refs = refs[n_in + n_row:]
        if acc_refs:
            first = functools.reduce(jnp.logical_and, [pl.program_id(a) == 0 for a in range(len(grid))])

            @pl.when(first)
            def _():
                for r in acc_refs:
                    r[...] = jnp.zeros(r.shape, r.dtype)

            for r, v in zip(acc_refs, acc_vals):
                r[...] += v.astype(r.dtype)

    return pl.pallas_call(
        kern, name=name, grid=grid, in_specs=list(in_specs), out_specs=list(row_specs) + list(acc_specs),
        out_shape=list(rows) + list(accs), input_output_aliases=dict(aliases or {}),
        compiler_params=_params(len(grid)))(*ins)


def _sds(shape, dtype):
    return jax.ShapeDtypeStruct(shape, dtype)


def _rms(x, g):
    return x * lax.rsqrt(jnp.mean(x * x, axis=-1, keepdims=True) + EPS) * g


def _row_spec(tm, c):
    return pl.BlockSpec((tm, c), lambda i: (i, 0))


def _vec_spec(c):
    return pl.BlockSpec((1, c), lambda i: (0, 0))


def _rms_fwd(name, x, g, tm):
    s, d = x.shape
    return _rw(name, lambda xb, gb: ((_rms(xb, gb),), ()), (s // tm,), [x, g],
               [_row_spec(tm, d), _vec_spec(d)], [_sds((s, d), BF16)], [_row_spec(tm, d)])[0]


def _rms_bwd(name, x, dh, dres, g, tm):
    s, d = x.shape

    def body(xb, dhb, drb, gb):
        _, vjp = jax.vjp(_rms, xb, gb)
        dx, dg = vjp(dhb)
        return (dx + drb,), (dg,)

    return _rw(name, body, (s // tm,), [x, dh, dres, g],
               [_row_spec(tm, d)] * 3 + [_vec_spec(d)], [_sds((s, d), F32)], [_row_spec(tm, d)],
               [_sds((1, d), F32)], [_vec_spec(d)])


def _swiglu(gate, up):
    return jax.nn.silu(gate) * up


def _swiglu_fwd(name, u, tm):
    _, s, c = u.shape
    return _rw(name, lambda ub: ((_swiglu(ub[0:2], ub[2:4]),), ()), (s // tm,), [u],
               [pl.BlockSpec((4, tm, c), lambda i: (0, i, 0))],
               [_sds((2, s, c), BF16)], [pl.BlockSpec((2, tm, c), lambda i: (0, i, 0))])[0]


def _swiglu_bwd(name, u, da, tm):
    _, s, c = u.shape

    def body(ub, dab):
        _, vjp = jax.vjp(_swiglu, ub[0:2], ub[2:4])
        dg, du = vjp(dab.astype(F32))
        return (jnp.concatenate([dg, du], axis=0),), ()

    return _rw(name, body, (s // tm,), [u, da],
               [pl.BlockSpec((4, tm, c), lambda i: (0, i, 0)), pl.BlockSpec((2, tm, c), lambda i: (0, i, 0))],
               [_sds((4, s, c), BF16)], [pl.BlockSpec((4, tm, c), lambda i: (0, i, 0))])[0]


def _ln_silu(y, g, b):
    mu = jnp.mean(y, axis=-1, keepdims=True)
    yc = y - mu
    var = jnp.mean(yc * yc, axis=-1, keepdims=True)
    return jax.nn.silu(yc * lax.rsqrt(var + EPS) * g + b)


def _ln_silu_fwd(name, y, g, b, tm):
    s, c = y.shape
    return _rw(name, lambda yb, gb, bb: ((_ln_silu(yb, gb, bb),), ()), (s // tm,), [y, g, b],
               [_row_spec(tm, c), _vec_spec(c), _vec_spec(c)], [_sds((s, c), BF16)], [_row_spec(tm, c)])[0]


def _ln_silu_bwd(name, y, dcat, g, b, tm):
    s, c = y.shape

    def body(yb, dob, gb, bb):
        _, vjp = jax.vjp(_ln_silu, yb, gb, bb)
        dy, dg, db = vjp(dob)
        return (dy,), (dg, db)

    return _rw(name, body, (s // tm,), [y, dcat, g, b],
               [_row_spec(tm, c), pl.BlockSpec((None, tm, c), lambda i: (0, i, 0)), _vec_spec(c), _vec_spec(c)],
               [_sds((s, c), F32)], [_row_spec(tm, c)],
               [_sds((1, c), F32)] * 2, [_vec_spec(c)] * 2)


def _head_masks():
    lane = np.arange(SLAB) // HEAD
    m = np.zeros((8, SLAB), np.float32)
    for h in range(SLAB // HEAD):
        m[h] = (lane == h)
    return jnp.asarray(m)


def _gated_head_norm(y, gate, g, hm):
    mu = jnp.zeros_like(y)
    for h in range(SLAB // HEAD):
        mu = mu + hm[h:h + 1] * (jnp.sum(y * hm[h:h + 1], axis=-1, keepdims=True) / HEAD)
    yc = y - mu
    var = jnp.zeros_like(y)
    for h in range(SLAB // HEAD):
        var = var + hm[h:h + 1] * (jnp.sum(yc * yc * hm[h:h + 1], axis=-1, keepdims=True) / HEAD)
    return jax.nn.silu(gate) * (yc * lax.rsqrt(var + EPS) * g)


def _slab_spec(tm, j):
    return pl.BlockSpec((None, tm, SLAB), lambda i: (j, i, 0))


def _ghn_fwd(name, y, p32, g, tm):
    s, c = y.shape
    hm = _head_masks()
    return _rw(name, lambda yb, gb, wb, hb: ((_gated_head_norm(yb, gb, wb, hb),), ()), (s // tm,),
               [y, p32, g, hm],
               [_row_spec(tm, c), _slab_spec(tm, 11), _vec_spec(c), pl.BlockSpec((8, c), lambda i: (0, 0))],
               [_sds((s, c), BF16)], [_row_spec(tm, c)])[0]


def _ghn_bwd(name, y, p32, dcat, g, tm):
    s, c = y.shape
    hm = _head_masks()

    def body(yb, gb, dob, wb, hb):
        _, vjp = jax.vjp(lambda a, b_, c_: _gated_head_norm(a, b_, c_, hb), yb, gb, wb)
        dy, dgate, dw = vjp(dob)
        return (dy, dgate), (dw,)

    return _rw(name, body, (s // tm,), [y, p32, dcat, g, hm],
               [_row_spec(tm, c), _slab_spec(tm, 11), _slab_spec(tm, 3), _vec_spec(c),
                pl.BlockSpec((8, c), lambda i: (0, 0))],
               [_sds((s, c), F32)] * 2, [_row_spec(tm, c)] * 2,
               [_sds((1, c), F32)], [_vec_spec(c)])


def _final(name, x, tgt, g, tm):
    s, d = x.shape

    def body(xb, tb, gb):
        yf, vjp = jax.vjp(_rms, xb, gb)
        err = yf - tb
        dx, dg = vjp(err * (1.0 / d))
        part = 0.5 * jnp.sum(jnp.mean(err * err, axis=-1, keepdims=True), axis=0, keepdims=True)
        return (dx,), (dg, jnp.broadcast_to(part, (1, 128)))

    return _rw(name, body, (s // tm,), [x, tgt, g],
               [_row_spec(tm, d), _row_spec(tm, d), _vec_spec(d)],
               [_sds((s, d), F32)], [_row_spec(tm, d)],
               [_sds((1, d), F32), _sds((1, 128), F32)], [_vec_spec(d), _vec_spec(128)])


NN = (((1,), (0,)), ((), ()))
NT = (((1,), (1,)), ((), ()))
TN = (((0,), (0,)), ((), ()))


def _mm(name, a, b, grid, a_spec, b_spec, outs, out_specs, acc_shape, dims, alpha=1.0, res=None, res_spec=None):
    nk = grid[-1]
    n_out = len(outs)

    def kern(*refs):
        a_ref, b_ref = refs[0], refs[1]
        res_ref = refs[2] if res is not None else None
        first_out = 3 if res is not None else 2
        o_refs = refs[first_out:first_out + n_out]
        part = lax.dot_general(a_ref[...].astype(BF16), b_ref[...].astype(BF16), dims,
                               preferred_element_type=F32)

        def finish(r):
            if alpha != 1.0:
                r = r * alpha
            if res_ref is not None:
                r = r + res_ref[...]
            for o in o_refs:
                o[...] = r.astype(o.dtype)

        if nk == 1:
            finish(part)
            return
        acc_ref = refs[-1]
        k = pl.program_id(len(grid) - 1)

        @pl.when(k == 0)
        def _():
            acc_ref[...] = part

        @pl.when(jnp.logical_and(k > 0, k < nk - 1))
        def _():
            acc_ref[...] += part

        @pl.when(k == nk - 1)
        def _():
            finish(acc_ref[...] + part)

    ins = [a, b] + ([res] if res is not None else [])
    in_specs = [a_spec, b_spec] + ([res_spec] if res is not None else [])
    return pl.pallas_call(
        kern, name=name, grid=grid, in_specs=in_specs, out_specs=list(out_specs), out_shape=list(outs),
        scratch_shapes=[pltpu.VMEM(acc_shape, F32)] if nk > 1 else [],
        compiler_params=_params(len(grid)))(*ins)


def _proj_in(name, h, w, l, tm):
    s, d = h.shape
    n = w.shape[-1]
    return _mm(name, h, w, (N_SHARD, s // tm, 1),
               pl.BlockSpec((tm, d), lambda b, i, k: (i, 0)),
               pl.BlockSpec((None, None, d, n), lambda b, i, k: (l, b, 0, 0)),
               [_sds((N_SHARD, s, n), F32)], [pl.BlockSpec((None, tm, n), lambda b, i, k: (b, i, 0))],
               (tm, n), NN)[0]


def _proj_mix(name, h, w, l, tm):
    s, d = h.shape
    per = MIX_SHARD // SLAB
    return _mm(name, h, w, (N_SLAB, s // tm, 1),
               pl.BlockSpec((tm, d), lambda j, i, k: (i, 0)),
               pl.BlockSpec((None, None, d, SLAB), lambda j, i, k: (l, j // per, 0, j % per)),
               [_sds((N_SLAB, s, SLAB), F32), _sds((N_SLAB, s, SLAB), BF16)],
               [pl.BlockSpec((None, tm, SLAB), lambda j, i, k: (j, i, 0))] * 2,
               (tm, SLAB), NN)


def _proj_out(name, a, w, l, res, alpha, tm):
    nk, s, r = a.shape
    d = w.shape[-1]
    return _mm(name, a, w, (s // tm, nk),
               pl.BlockSpec((None, tm, r), lambda i, k: (k, i, 0)),
               pl.BlockSpec((None, None, r, d), lambda i, k: (l, k, 0, 0)),
               [_sds((s, d), F32)], [pl.BlockSpec((tm, d), lambda i, k: (i, 0))],
               (tm, d), NN, alpha=alpha, res=res, res_spec=pl.BlockSpec((tm, d), lambda i, k: (i, 0)))[0]


def _back_out(name, dy, w, l, alpha, tm, out_dtype):
    s, d = dy.shape
    nk, r = w.shape[1], w.shape[2]
    return _mm(name, dy, w, (nk, s // tm, 1),
               pl.BlockSpec((tm, d), lambda b, i, k: (i, 0)),
               pl.BlockSpec((None, None, r, d), lambda b, i, k: (l, b, 0, 0)),
               [_sds((nk, s, r), out_dtype)], [pl.BlockSpec((None, tm, r), lambda b, i, k: (b, i, 0))],
               (tm, r), NT, alpha=alpha)[0]


def _back_in(name, du, w, l, tm, slab=None):
    nk, s, n = du.shape
    d = w.shape[2]
    if slab is None:
        b_spec = pl.BlockSpec((None, None, d, n), lambda i, k: (l, k, 0, 0))
    else:
        per = MIX_SHARD // SLAB
        b_spec = pl.BlockSpec((None, None, d, SLAB), lambda i, k: (l, k // per, 0, k % per))
    return _mm(name, du, w, (s // tm, nk),
               pl.BlockSpec((None, tm, n), lambda i, k: (k, i, 0)), b_spec,
               [_sds((s, d), F32)], [pl.BlockSpec((tm, d), lambda i, k: (i, 0))],
               (tm, d), NT)[0]


def _grad_in(name, h, du, ts, slab=False):
    s, d = h.shape
    nb, _, n = du.shape
    if not slab:
        out, out_spec = _sds((nb, d, n), BF16), pl.BlockSpec((None, d, n), lambda b, i, k: (b, 0, 0))
    else:
        per = MIX_SHARD // SLAB
        out = _sds((N_SHARD, d, MIX_SHARD), BF16)
        out_spec = pl.BlockSpec((None, d, SLAB), lambda b, i, k: (b // per, 0, b % per))
    return _mm(name, h, du, (nb, 1, s // ts),
               pl.BlockSpec((ts, d), lambda b, i, k: (k, 0)),
               pl.BlockSpec((None, ts, n), lambda b, i, k: (b, k, 0)),
               [out], [out_spec], (d, n), TN)[0]


def _grad_out(name, a, dy, alpha, ts):
    nb, s, r = a.shape
    d = dy.shape[1]
    return _mm(name, a, dy, (nb, 1, s // ts),
               pl.BlockSpec((None, ts, r), lambda b, i, k: (b, k, 0)),
               pl.BlockSpec((ts, d), lambda b, i, k: (k, 0)),
               [_sds((nb, r, d), BF16)], [pl.BlockSpec((None, r, d), lambda b, i, k: (b, 0, 0))],
               (r, d), TN, alpha=alpha)[0]


CONV_TILE = 256


def _shifted(win, off, rows):
    n = win.shape[0]
    return pltpu.roll(win, (n - off) % n, 0)[0:rows] if off % n else win[0:rows]


def _conv_fwd(name, p32, w, bias):
    s = p32.shape[1]
    cb = 128
    nt = s // CONV_TILE

    def kern(a_ref, b_ref, w_ref, bias_ref, y_ref, vpad):
        vpad[0:CONV_PAD, :] = jnp.zeros((CONV_PAD, cb), F32)

        def fill(i, c):
            r = pl.multiple_of(i * CONV_TILE, CONV_TILE)
            vpad[pl.ds(CONV_PAD + r, CONV_TILE), :] = (
                a_ref[pl.ds(r, CONV_TILE), :] * jax.nn.sigmoid(b_ref[pl.ds(r, CONV_TILE), :]))
            return c

        lax.fori_loop(0, nt, fill, 0)

        def tile(i, c):
            r = pl.multiple_of(i * CONV_TILE, CONV_TILE)
            win = vpad[pl.ds(r, CONV_TILE + CONV_PAD), :]
            acc = jnp.broadcast_to(bias_ref[...], (CONV_TILE, cb))
            for j in range(CONV_W):
                acc = acc + w_ref[j:j + 1, :] * _shifted(win, j + 2, CONV_TILE)
            y_ref[pl.ds(r, CONV_TILE), :] = acc
            return c

        lax.fori_loop(0, nt, tile, 0)

    return pl.pallas_call(
        kern, name=name, grid=(SLAB // cb,),
        in_specs=[pl.BlockSpec((None, s, cb), lambda c: (0, 0, c)),
                  pl.BlockSpec((None, s, cb), lambda c: (1, 0, c)),
                  pl.BlockSpec((CONV_W, cb), lambda c: (0, c)),
                  pl.BlockSpec((1, cb), lambda c: (0, c))],
        out_specs=pl.BlockSpec((s, cb), lambda c: (0, c)),
        out_shape=_sds((s, SLAB), F32),
        scratch_shapes=[pltpu.VMEM((s + CONV_PAD, cb), F32)],
        compiler_params=_params(1))(p32, p32, w, bias)


def _conv_bwd(name, p32, w, dy):
    s = p32.shape[1]
    cb = 128
    nt = s // CONV_TILE

    def kern(a_ref, b_ref, w_ref, dy_ref, da_ref, db_ref, dw_ref, dbias_ref, vpad, dpad):
        vpad[0:CONV_PAD, :] = jnp.zeros((CONV_PAD, cb), F32)
        dpad[s:s + CONV_PAD, :] = jnp.zeros((CONV_PAD, cb), F32)
        dw_ref[...] = jnp.zeros((CONV_PAD, cb), F32)
        dbias_ref[...] = jnp.zeros((1, cb), F32)

        def fill(i, c):
            r = pl.multiple_of(i * CONV_TILE, CONV_TILE)
            vpad[pl.ds(CONV_PAD + r, CONV_TILE), :] = (
                a_ref[pl.ds(r, CONV_TILE), :] * jax.nn.sigmoid(b_ref[pl.ds(r, CONV_TILE), :]))
            dpad[pl.ds(r, CONV_TILE), :] = dy_ref[pl.ds(r, CONV_TILE), :]
            return c

        lax.fori_loop(0, nt, fill, 0)

        def tile(i, c):
            r = pl.multiple_of(i * CONV_TILE, CONV_TILE)
            dwin = dpad[pl.ds(r, CONV_TILE + CONV_PAD), :]
            vwin = vpad[pl.ds(r, CONV_TILE + CONV_PAD), :]
            dyt = dwin[0:CONV_TILE]
            dv = jnp.zeros((CONV_TILE, cb), F32)
            for j in range(CONV_W):
                dv = dv + w_ref[j:j + 1, :] * _shifted(dwin, CONV_W - 1 - j, CONV_TILE)
                dw_ref[j:j + 1, :] += jnp.sum(dyt * _shifted(vwin, j + 2, CONV_TILE), axis=0, keepdims=True)
            dbias_ref[...] += jnp.sum(dyt, axis=0, keepdims=True)
            a = a_ref[pl.ds(r, CONV_TILE), :]
            sg = jax.nn.sigmoid(b_ref[pl.ds(r, CONV_TILE), :])
            da_ref[pl.ds(r, CONV_TILE), :] = dv * sg
            db_ref[pl.ds(r, CONV_TILE), :] = dv * a * sg * (1.0 - sg)
            return c

        lax.fori_loop(0, nt, tile, 0)

    col = pl.BlockSpec((s, cb), lambda c: (0, c))
    return pl.pallas_call(
        kern, name=name, grid=(SLAB // cb,),
        in_specs=[pl.BlockSpec((None, s, cb), lambda c: (0, 0, c)),
                  pl.BlockSpec((None, s, cb), lambda c: (1, 0, c)),
                  pl.BlockSpec((CONV_W, cb), lambda c: (0, c)), col],
        out_specs=[col, col, pl.BlockSpec((CONV_PAD, cb), lambda c: (0, c)), pl.BlockSpec((1, cb), lambda c: (0, c))],
        out_shape=[_sds((s, SLAB), F32), _sds((s, SLAB), F32), _sds((CONV_PAD, SLAB), F32), _sds((1, SLAB), F32)],
        scratch_shapes=[pltpu.VMEM((s + CONV_PAD, cb), F32), pltpu.VMEM((s + CONV_PAD, cb), F32)],
        compiler_params=_params(1))(p32, p32, w, dy)


SB_BLOCK = 256
N_HEAD = SLAB // HEAD


def _split_dot(x, m):
    hi = x.astype(BF16)
    lo = (x - hi.astype(F32)).astype(BF16)
    return (jnp.dot(hi, m, preferred_element_type=F32) + jnp.dot(lo, m, preferred_element_type=F32))


def _sb_logits(qm, k, tri):
    z = lax.dot_general(qm, k, NT, preferred_element_type=F32)
    lb = jnp.minimum(z, 0.0) - jnp.log(1.0 + jnp.exp(-jnp.abs(z)))
    ln = lb - z
    if tri is not None:
        ln = jnp.where(tri, ln, 0.0)
    return lb, ln


def _first_col(x):
    return jnp.broadcast_to(x[:, 0:1], (x.shape[0], 128))


def _head_stack(dst, x, lane_head, bq):
    for h in range(N_HEAD):
        dst[h * bq:(h + 1) * bq, :] = jnp.where(lane_head == h, x, jnp.zeros_like(x))


def _sb_fwd(name, p16):
    s = p16.shape[1]
    bq = min(SB_BLOCK, s)
    nq = s // bq

    def kern(q_ref, k_ref, v_ref, o_ref, qm_ref, v4_ref, w4_ref, acc_ref, r_ref):
        qi = pl.program_id(1)
        lane_head = lax.broadcasted_iota(jnp.int32, (1, SLAB), 1) // HEAD
        _head_stack(qm_ref, (q_ref[...].astype(F32) * (HEAD ** -0.5)).astype(BF16), lane_head, bq)
        row = lax.broadcasted_iota(jnp.int32, (bq, bq), 0)
        col = lax.broadcasted_iota(jnp.int32, (bq, bq), 1)
        after = (row > col).astype(BF16)
        tri = col < row
        acc_ref[...] = jnp.zeros((bq, SLAB), F32)
        r_ref[...] = jnp.zeros((N_HEAD, bq, 128), F32)

        def tile(kb, masked):
            rows = pl.ds(pl.multiple_of(kb * bq, bq), bq)
            k = k_ref[rows, :]
            _head_stack(v4_ref, v_ref[rows, :], lane_head, bq)
            for h in range(N_HEAD):
                lb, ln = _sb_logits(qm_ref[h * bq:(h + 1) * bq, :], k, tri if masked else None)
                rem = _split_dot(ln, after)
                w = jnp.exp(lb + rem + r_ref[h][:, 0:1])
                if masked:
                    w = jnp.where(tri, w, 0.0)
                w4_ref[:, h * bq:(h + 1) * bq] = w.astype(BF16)
                r_ref[h] += _first_col(rem + ln)
            acc_ref[...] += jnp.dot(w4_ref[...], v4_ref[...], preferred_element_type=F32)

        tile(qi, True)

        def step(i, c):
            tile(qi - i, False)
            return c

        lax.fori_loop(1, qi + 1, step, 0)
        o_ref[...] = acc_ref[...]

    return pl.pallas_call(
        kern, name=name, grid=(2, nq),
        in_specs=[pl.BlockSpec((None, bq, SLAB), lambda g, i: (2 + g, i, 0)),
                  pl.BlockSpec((None, s, SLAB), lambda g, i: (4 + g, 0, 0)),
                  pl.BlockSpec((None, s, SLAB), lambda g, i: (6 + g, 0, 0))],
        out_specs=pl.BlockSpec((None, bq, SLAB), lambda g, i: (g, i, 0)),
        out_shape=_sds((2, s, SLAB), F32),
        scratch_shapes=[pltpu.VMEM((N_HEAD * bq, SLAB), BF16), pltpu.VMEM((N_HEAD * bq, SLAB), BF16),
                        pltpu.VMEM((bq, N_HEAD * bq), BF16), pltpu.VMEM((bq, SLAB), F32),
                        pltpu.VMEM((N_HEAD, bq, 128), F32)],
        compiler_params=_params(2))(p16, p16, p16)


def _sb_bwd(name, p16, o, dcat):
    s = p16.shape[1]
    bq = min(SB_BLOCK, s)
    nq = s // bq

    def kern(q_ref, k_ref, v_ref, o_ref, do_ref, dq_ref, dk_hbm, dv_hbm, dk_acc, dv_acc, dq_acc,
             qm_ref, dom_ref, k4_ref, dzc_ref, dzs_ref, ws_ref, t_ref, r_ref, c_ref, sem):
        g, qi = pl.program_id(0), pl.program_id(1)

        @pl.when(qi == 0)
        def _():
            dk_acc[...] = jnp.zeros((s, SLAB), F32)
            dv_acc[...] = jnp.zeros((s, SLAB), F32)

        lane_head = lax.broadcasted_iota(jnp.int32, (1, SLAB), 1) // HEAD
        _head_stack(qm_ref, (q_ref[...].astype(F32) * (HEAD ** -0.5)).astype(BF16), lane_head, bq)
        dob = do_ref[...].astype(BF16)
        _head_stack(dom_ref, dob, lane_head, bq)
        prod = dob.astype(F32) * o_ref[...]
        for h in range(N_HEAD):
            t_ref[h] = jnp.broadcast_to(
                jnp.sum(jnp.where(lane_head == h, prod, 0.0), axis=1, keepdims=True), (bq, 128))
        row = lax.broadcasted_iota(jnp.int32, (bq, bq), 0)
        col = lax.broadcasted_iota(jnp.int32, (bq, bq), 1)
        after = (row > col).astype(BF16)
        from_ = (row >= col).astype(BF16)
        tri = col < row
        dq_acc[...] = jnp.zeros((bq, SLAB), F32)
        r_ref[...] = jnp.zeros((N_HEAD, bq, 128), F32)
        c_ref[...] = jnp.zeros((N_HEAD, bq, 128), F32)

        def tile(kb, masked):
            rows = pl.ds(pl.multiple_of(kb * bq, bq), bq)
            k, v = k_ref[rows, :], v_ref[rows, :]
            _head_stack(k4_ref, k, lane_head, bq)
            for h in range(N_HEAD):
                mine = slice(h * bq, (h + 1) * bq)
                lb, ln = _sb_logits(qm_ref[mine, :], k, tri if masked else None)
                rem = _split_dot(ln, after)
                w = jnp.exp(lb + rem + r_ref[h][:, 0:1])
                if masked:
                    w = jnp.where(tri, w, 0.0)
                wb = w.astype(BF16)
                dl = wb.astype(F32) * lax.dot_general(dom_ref[mine, :], v, NT, preferred_element_type=F32)
                suffix = _split_dot(dl, from_)
                before = t_ref[h][:, 0:1] - (suffix + c_ref[h][:, 0:1])
                dz = dl - jnp.exp(lb) * (dl + before)
                if masked:
                    dz = jnp.where(tri, dz, 0.0)
                dzb = dz.astype(BF16)
                dzc_ref[:, mine] = dzb
                dzs_ref[mine, :] = dzb
                ws_ref[mine, :] = wb
                r_ref[h] += _first_col(rem + ln)
                c_ref[h] += _first_col(suffix)
            dq_acc[...] += jnp.dot(dzc_ref[...], k4_ref[...], preferred_element_type=F32)
            dk_acc[rows, :] += lax.dot_general(dzs_ref[...], qm_ref[...], TN, preferred_element_type=F32)
            dv_acc[rows, :] += lax.dot_general(ws_ref[...], dom_ref[...], TN, preferred_element_type=F32)

        tile(qi, True)

        def step(i, c):
            tile(qi - i, False)
            return c

        lax.fori_loop(1, qi + 1, step, 0)
        dq_ref[...] = dq_acc[...] * (HEAD ** -0.5)

        @pl.when(qi == nq - 1)
        def _():
            ck = pltpu.make_async_copy(dk_acc, dk_hbm.at[g], sem.at[0])
            cv = pltpu.make_async_copy(dv_acc, dv_hbm.at[g], sem.at[1])
            ck.start()
            cv.start()
            ck.wait()
            cv.wait()

    blk = lambda j0: pl.BlockSpec((None, bq, SLAB), lambda g, i: (j0 + g, i, 0))
    full = lambda j0: pl.BlockSpec((None, s, SLAB), lambda g, i: (j0 + g, 0, 0))
    stack16 = pltpu.VMEM((N_HEAD * bq, SLAB), BF16)
    return pl.pallas_call(
        kern, name=name, grid=(2, nq),
        in_specs=[blk(2), full(4), full(6), blk(0), blk(1)],
        out_specs=[blk(0), pl.BlockSpec(memory_space=pl.ANY), pl.BlockSpec(memory_space=pl.ANY)],
        out_shape=[_sds((2, s, SLAB), F32)] * 3,
        scratch_shapes=[pltpu.VMEM((s, SLAB), F32), pltpu.VMEM((s, SLAB), F32), pltpu.VMEM((bq, SLAB), F32),
                        stack16, stack16, stack16, pltpu.VMEM((bq, N_HEAD * bq), BF16),
                        pltpu.VMEM((N_HEAD * bq, bq), BF16), pltpu.VMEM((N_HEAD * bq, bq), BF16),
                        pltpu.VMEM((N_HEAD, bq, 128), F32), pltpu.VMEM((N_HEAD, bq, 128), F32),
                        pltpu.VMEM((N_HEAD, bq, 128), F32), pltpu.SemaphoreType.DMA((2,))],
        compiler_params=_params(2))(p16, p16, p16, o, dcat)


RET_BLOCK = 256


def _ret_tables(s, bl):
    nh = SLAB // HEAD
    lane_h = np.arange(SLAB) // HEAD
    log_gamma = np.log1p(-np.exp2(-5.0 - np.arange(nh, dtype=np.float64)))
    lg_lane = log_gamma[lane_h]
    half = HEAD // 2
    inv = 1.0 / (ROPE_BASE ** (np.arange(half, dtype=np.float64) / half))
    ang = np.arange(s, dtype=np.float64)[:, None] * inv[None, :]
    within = np.arange(SLAB) % HEAD
    cos = np.cos(ang)[:, within % half]
    sin = np.sin(ang)[:, within % half] * np.where(within < half, -1.0, 1.0)[None, :]
    perm = np.zeros((SLAB, SLAB))
    partner = np.where(within < half, np.arange(SLAB) + half, np.arange(SLAB) - half)
    perm[partner, np.arange(SLAB)] = 1.0
    i = np.arange(bl)
    diff = i[:, None] - i[None, :]
    same = (i[:, None] // CHUNK) == (i[None, :] // CHUNK)
    earlier = (i[None, :] // CHUNK) < (i[:, None] // CHUNK)
    decay = np.zeros((nh, bl, bl))
    for h in range(nh):
        decay[h] = np.where(same, np.exp(log_gamma[h] * np.abs(diff)),
                            np.where(earlier, np.exp(log_gamma[h] * diff), 0.0))
    qd = np.exp(lg_lane[None, :] * (i[:, None] + 1.0))
    kd = np.exp(lg_lane[None, :] * (bl - 1.0 - i[:, None]))
    gam = np.exp(lg_lane * bl)[:, None] * np.ones((1, SLAB))
    bd = (lane_h[:, None] == lane_h[None, :]).astype(np.float64)
    f = lambda a: jnp.asarray(a, F32)
    return f(cos), f(sin), f(perm), f(decay), f(qd), f(kd), f(gam), f(bd)


def _ret_block(q, k, v, state, cos, sin, perm, decay, qd, kd, gam, bd, hm):
    qr = (q * cos + jnp.dot(q, perm, preferred_element_type=F32) * sin) * (HEAD ** -0.5)
    kr = k * cos + jnp.dot(k, perm, preferred_element_type=F32) * sin
    y = jnp.dot(qr * qd, state, preferred_element_type=F32)
    for h in range(SLAB // HEAD):
        m = hm[h:h + 1]
        sc = lax.dot_general(qr * m, kr, NT, preferred_element_type=F32) * decay[h]
        y = y + jnp.dot(sc, v * m, preferred_element_type=F32)
    new_state = gam * state + lax.dot_general(kr * kd, v, TN, preferred_element_type=F32) * bd
    return y, new_state


def _ret_specs(s, bl, rev):
    nb = s // bl
    pos = (lambda n: nb - 1 - n) if rev else (lambda n: n)
    slab = lambda j: pl.BlockSpec((None, bl, SLAB), lambda n: (j, pos(n), 0))
    const2 = lambda r: pl.BlockSpec((r, SLAB), lambda n: (0, 0))
    tab = [pl.BlockSpec((bl, SLAB), lambda n: (pos(n), 0))] * 2 + [
        const2(SLAB), pl.BlockSpec((SLAB // HEAD, bl, bl), lambda n: (0, 0, 0)),
        const2(bl), const2(bl), const2(SLAB), const2(SLAB), const2(8)]
    return nb, pos, slab, tab


def _ret_fwd(name, p32):
    s = p32.shape[1]
    bl = min(RET_BLOCK, s)
    nb, pos, slab, tab = _ret_specs(s, bl, False)
    tables = _ret_tables(s, bl) + (_head_masks(),)

    def kern(q_ref, k_ref, v_ref, *rest):
        t_refs, (y_ref, st_ref, state) = rest[:9], rest[9:]

        @pl.when(pl.program_id(0) == 0)
        def _():
            state[...] = jnp.zeros((SLAB, SLAB), F32)

        st_ref[...] = state[...]
        y, new = _ret_block(q_ref[...], k_ref[...], v_ref[...], state[...], *[t[...] for t in t_refs])
        y_ref[...] = y
        state[...] = new

    return pl.pallas_call(
        kern, name=name, grid=(nb,), in_specs=[slab(8), slab(9), slab(10)] + tab,
        out_specs=[pl.BlockSpec((bl, SLAB), lambda n: (n, 0)), pl.BlockSpec((None, SLAB, SLAB), lambda n: (n, 0, 0))],
        out_shape=[_sds((s, SLAB), F32), _sds((nb, SLAB, SLAB), F32)],
        scratch_shapes=[pltpu.VMEM((SLAB, SLAB), F32)], compiler_params=_params(1))(p32, p32, p32, *tables)


def _ret_bwd(name, p32, states, dy):
    s = p32.shape[1]
    bl = min(RET_BLOCK, s)
    nb, pos, slab, tab = _ret_specs(s, bl, True)
    tables = _ret_tables(s, bl) + (_head_masks(),)
    rowblk = pl.BlockSpec((bl, SLAB), lambda n: (pos(n), 0))

    def kern(q_ref, k_ref, v_ref, st_ref, dy_ref, *rest):
        t_refs, (dq_ref, dk_ref, dv_ref, dstate) = rest[:9], rest[9:]

        @pl.when(pl.program_id(0) == 0)
        def _():
            dstate[...] = jnp.zeros((SLAB, SLAB), F32)

        tv = [t[...] for t in t_refs]
        _, vjp = jax.vjp(lambda a, b, c, d: _ret_block(a, b, c, d, *tv),
                         q_ref[...], k_ref[...], v_ref[...], st_ref[...])
        dq, dk, dv, ds = vjp((dy_ref[...], dstate[...]))
        dq_ref[...] = dq
        dk_ref[...] = dk
        dv_ref[...] = dv
        dstate[...] = ds

    return pl.pallas_call(
        kern, name=name, grid=(nb,),
        in_specs=[slab(8), slab(9), slab(10), pl.BlockSpec((None, SLAB, SLAB), lambda n: (pos(n), 0, 0)), rowblk] + tab,
        out_specs=[rowblk] * 3, out_shape=[_sds((s, SLAB), F32)] * 3,
        scratch_shapes=[pltpu.VMEM((SLAB, SLAB), F32)], compiler_params=_params(1))(p32, p32, p32, states, dy, *tables)


TM_FFN = 1024
TM_SLAB = 2048
TM_RW = 256
TM_FF = 128


def _ffn_fwd(tag, x, g, w_in, w_out, l):
    tm = min(TM_FFN, x.shape[0])
    h = _rms_fwd(tag + "_rms", x, g, TM_RW)
    u = _proj_in(tag + "_in", h, w_in, l, tm)
    a = _swiglu_fwd(tag + "_act", u, TM_FF)
    w_out2 = w_out.reshape(DEPTH, 2, FF_SHARD, D_MODEL)
    xn = _proj_out(tag + "_out", a, w_out2, l, x, 0.5, tm)
    return xn, (x, h, u, a)


def _ffn_bwd(tag, saved, dxn, g, w_in, w_out, l):
    x, h, u, a = saved
    tm = min(TM_FFN, x.shape[0])
    w_out2 = w_out.reshape(DEPTH, 2, FF_SHARD, D_MODEL)
    da = _back_out(tag + "_dact", dxn, w_out2, l, 0.5, tm, BF16)
    dw_out = _grad_out(tag + "_dwout", a, dxn, 0.5, tm)
    du = _swiglu_bwd(tag + "_dswi", u, da, TM_FF)
    dh = _back_in(tag + "_dh", du, w_in, l, tm)
    dw_in = _grad_in(tag + "_dwin", h, du, tm)
    dx, dg = _rms_bwd(tag + "_drms", x, dh, dxn, g, TM_RW)
    return dx, dg, dw_in, dw_out.reshape(N_SHARD, D_FF // N_SHARD, D_MODEL)


def _mix_fwd(tag, x, sm, w_in, w_out, l):
    ts = min(TM_SLAB, x.shape[0])
    h = _rms_fwd(tag + "_rms", x, sm["mix_norm"][l:l + 1], TM_RW)
    p32, p16 = _proj_mix(tag + "_in", h, w_in, l, ts)
    ypre = _conv_fwd(tag + "_conv", p32, sm["conv_w"][l], sm["conv_b"][l:l + 1])
    yconv = _ln_silu_fwd(tag + "_ln", ypre, sm["conv_ln_g"][l:l + 1], sm["conv_ln_b"][l:l + 1], TM_RW)
    osb = _sb_fwd(tag + "_sb", p16)
    yr, states = _ret_fwd(tag + "_ret", p32)
    yret = _ghn_fwd(tag + "_ghn", yr, p32, sm["ret_norm_g"][l:l + 1], TM_RW)
    ycat = jnp.stack([yconv, osb[0].astype(BF16), osb[1].astype(BF16), yret])
    xn = _proj_out(tag + "_out", ycat, w_out, l, x, 1.0, min(TM_FFN, x.shape[0]))
    return xn, (x, h, p32, p16, ypre, osb, yr, states, ycat)


def _mix_bwd(tag, saved, dxn, sm, w_in, w_out, l):
    x, h, p32, p16, ypre, osb, yr, states, ycat = saved
    ts = min(TM_SLAB, x.shape[0])
    dcat = _back_out(tag + "_dcat", dxn, w_out, l, 1.0, ts, F32)
    dw_out = _grad_out(tag + "_dwout", ycat, dxn, 1.0, ts)
    dypre, dlg, dlb = _ln_silu_bwd(tag + "_dln", ypre, dcat, sm["conv_ln_g"][l:l + 1], sm["conv_ln_b"][l:l + 1], TM_RW)
    da, db, dcw, dcb = _conv_bwd(tag + "_dconv", p32, sm["conv_w"][l], dypre)
    dq, dk, dv = _sb_bwd(tag + "_dsb", p16, osb, dcat)
    dyr, dgate, drg = _ghn_bwd(tag + "_dghn", yr, p32, dcat, sm["ret_norm_g"][l:l + 1], TM_RW)
    dqr, dkr, dvr = _ret_bwd(tag + "_dret", p32, states, dyr)
    dp = jnp.stack([da, db, dq[0], dq[1], dk[0], dk[1], dv[0], dv[1], dqr, dkr, dvr, dgate]).astype(BF16)
    dh = _back_in(tag + "_dh", dp, w_in, l, ts, slab=True)
    dw_in = _grad_in(tag + "_dwin", h, dp, ts, slab=True)
    dx, dg = _rms_bwd(tag + "_drms", x, dh, dxn, sm["mix_norm"][l:l + 1], TM_RW)
    small = dict(mix_norm=dg, conv_w=dcw[0:CONV_W], conv_b=dcb, conv_ln_g=dlg, conv_ln_b=dlb, ret_norm_g=drg)
    return dx, small, dw_in, dw_out


def _local_step(x, tgt, wt, sm):
    saved = []
    for l in range(DEPTH):
        x, s1 = _ffn_fwd(f"l{l}f1", x, sm["ffn1_norm"][l:l + 1], wt["ffn1_w_in"], wt["ffn1_w_out"], l)
        x, s2 = _mix_fwd(f"l{l}mx", x, sm, wt["mix_w_in"], wt["mix_w_out"], l)
        x, s3 = _ffn_fwd(f"l{l}f2", x, sm["ffn2_norm"][l:l + 1], wt["ffn2_w_in"], wt["ffn2_w_out"], l)
        saved.append((s1, s2, s3))
    dx, dfinal, loss = _final("final", x, tgt, sm["final_norm"][None, :], TM_RW)
    big = [None] * DEPTH
    small = [None] * DEPTH
    for l in reversed(range(DEPTH)):
        s1, s2, s3 = saved[l]
        dx, dg3, dwi3, dwo3 = _ffn_bwd(f"l{l}f2", s3, dx, sm["ffn2_norm"][l:l + 1], wt["ffn2_w_in"], wt["ffn2_w_out"], l)
        dx, sml, dwi2, dwo2 = _mix_bwd(f"l{l}mx", s2, dx, sm, wt["mix_w_in"], wt["mix_w_out"], l)
        dx, dg1, dwi1, dwo1 = _ffn_bwd(f"l{l}f1", s1, dx, sm["ffn1_norm"][l:l + 1], wt["ffn1_w_in"], wt["ffn1_w_out"], l)
        big[l] = dict(ffn1_w_in=dwi1, ffn1_w_out=dwo1, mix_w_in=dwi2, mix_w_out=dwo2, ffn2_w_in=dwi3, ffn2_w_out=dwo3)
        sml.update(ffn1_norm=dg1, ffn2_norm=dg3)
        small[l] = sml
    return loss, dx, big, small, dfinal


MESH = pl.DeviceIdType.MESH
ANY = pl.BlockSpec(memory_space=pl.ANY)
BIG = ("ffn1_w_in", "ffn1_w_out", "mix_w_in", "mix_w_out", "ffn2_w_in", "ffn2_w_out")


def _place():
    x, y, c = lax.axis_index("x"), lax.axis_index("y"), lax.axis_index("c")
    chips = [(1 - x, y), (x, 1 - y), (1 - x, 1 - y)]
    return x, y, c, chips


def _gather_weights(w16):
    n = len(w16)

    def kern(*refs):
        dst = refs[n:2 * n]
        send, recv = refs[2 * n:]
        x, y, c, chips = _place()
        mine = 2 * x + y
        firsts, passes = [], []
        for a in range(n):
            h = dst[a].shape[2] // 2
            own = dst[a].at[:, mine, pl.ds(c * h, h)]
            for j, (cx, cy) in enumerate(chips):
                cp = pltpu.make_async_remote_copy(
                    src_ref=own, dst_ref=own, send_sem=send.at[6 * a + j], recv_sem=recv.at[6 * a + j],
                    device_id=(cx, cy, c), device_id_type=MESH)
                cp.start()
                firsts.append(cp)
        for a in range(n):
            h = dst[a].shape[2] // 2
            half = pl.ds(c * h, h)
            for j, (cx, cy) in enumerate(chips):
                theirs = dst[a].at[:, 2 * cx + cy, half]
                pltpu.make_async_remote_copy(
                    src_ref=theirs, dst_ref=theirs, send_sem=send.at[6 * a + j], recv_sem=recv.at[6 * a + j],
                    device_id=(cx, cy, c), device_id_type=MESH).wait_recv()
                fw = pltpu.make_async_remote_copy(
                    src_ref=theirs, dst_ref=theirs, send_sem=send.at[6 * a + 3 + j], recv_sem=recv.at[6 * a + 3 + j],
                    device_id=(x, y, 1 - c), device_id_type=MESH)
                fw.start()
                passes.append(fw)
        for a in range(n):
            h = dst[a].shape[2] // 2
            other = pl.ds((1 - c) * h, h)
            for j, (cx, cy) in enumerate(chips):
                got = dst[a].at[:, 2 * cx + cy, other]
                pltpu.make_async_remote_copy(
                    src_ref=got, dst_ref=got, send_sem=send.at[6 * a + 3 + j], recv_sem=recv.at[6 * a + 3 + j],
                    device_id=(x, y, 1 - c), device_id_type=MESH).wait_recv()
        for cp in firsts + passes:
            cp.wait_send()

    return pl.pallas_call(
        kern, name="gather_weights", in_specs=[ANY] * n, out_specs=[ANY] * n,
        out_shape=[_sds(w.shape, w.dtype) for w in w16], input_output_aliases={a: a for a in range(n)},
        scratch_shapes=[pltpu.SemaphoreType.DMA((6 * n,)), pltpu.SemaphoreType.DMA((6 * n,))])(*w16)


def _pair_exchange(grads):
    n = len(grads)

    def kern(*refs):
        src, got_o = refs[:n], refs[n:2 * n]
        send, recv = refs[2 * n:]
        x, y, c, _ = _place()
        cps = []
        for a in range(n):
            h = src[a].shape[1] // 2
            cp = pltpu.make_async_remote_copy(
                src_ref=src[a].at[:, pl.ds((1 - c) * h, h)], dst_ref=got_o[a],
                send_sem=send.at[a], recv_sem=recv.at[a], device_id=(x, y, 1 - c), device_id_type=MESH)
            cp.start()
            cps.append(cp)
        for cp in cps:
            cp.wait()

    halves = [_sds((g.shape[0], g.shape[1] // 2, g.shape[2]), g.dtype) for g in grads]
    return pl.pallas_call(
        kern, name="pair_exchange", in_specs=[ANY] * n, out_specs=[ANY] * n, out_shape=halves,
        scratch_shapes=[pltpu.SemaphoreType.DMA((n,)), pltpu.SemaphoreType.DMA((n,))])(*grads)


def _chip_exchange(sums):
    n = len(sums)

    def kern(*refs):
        src, dst = refs[:n], refs[n:2 * n]
        send, recv = refs[2 * n:]
        x, y, c, chips = _place()
        cps = []
        for a in range(n):
            for j, (cx, cy) in enumerate(chips):
                cp = pltpu.make_async_remote_copy(
                    src_ref=src[a].at[2 * cx + cy], dst_ref=dst[a].at[j],
                    send_sem=send.at[3 * a + j], recv_sem=recv.at[3 * a + j],
                    device_id=(cx, cy, c), device_id_type=MESH)
                cp.start()
                cps.append(cp)
        for cp in cps:
            cp.wait()

    return pl.pallas_call(
        kern, name="chip_exchange", in_specs=[ANY] * n, out_specs=[ANY] * n,
        out_shape=[_sds((3,) + s_.shape[1:], s_.dtype) for s_ in sums],
        scratch_shapes=[pltpu.SemaphoreType.DMA((3 * n,)), pltpu.SemaphoreType.DMA((3 * n,))])(*sums)


def _pair_join(full):
    n = len(full)

    def kern(*refs):
        dst = refs[n:2 * n]
        send, recv = refs[2 * n:]
        x, y, c, _ = _place()
        cps = []
        for a in range(n):
            h = dst[a].shape[1] // 2
            mine = dst[a].at[:, pl.ds(c * h, h)]
            cp = pltpu.make_async_remote_copy(
                src_ref=mine, dst_ref=mine, send_sem=send.at[a], recv_sem=recv.at[a],
                device_id=(x, y, 1 - c), device_id_type=MESH)
            cp.start()
            cps.append(cp)
        for a, cp in enumerate(cps):
            cp.wait_send()
            h = dst[a].shape[1] // 2
            got = dst[a].at[:, pl.ds((1 - c) * h, h)]
            pltpu.make_async_remote_copy(
                src_ref=got, dst_ref=got, send_sem=send.at[a], recv_sem=recv.at[a],
                device_id=(x, y, 1 - c), device_id_type=MESH).wait_recv()

    return pl.pallas_call(
        kern, name="pair_join", in_specs=[ANY] * n, out_specs=[ANY] * n,
        out_shape=[_sds(f.shape, f.dtype) for f in full], input_output_aliases={a: a for a in range(n)},
        scratch_shapes=[pltpu.SemaphoreType.DMA((n,)), pltpu.SemaphoreType.DMA((n,))])(*full)


def _all_sum(name, v):
    r = v.shape[0]

    def kern(v_ref, o_ref, buf, send, recv):
        x, y, c, _ = _place()
        me = 4 * x + 2 * y + c
        buf[me] = v_ref[...]
        cps = []
        for k in range(1, 8):
            peer = (x ^ (k >> 2), y ^ ((k >> 1) & 1), c ^ (k & 1))
            cp = pltpu.make_async_remote_copy(
                src_ref=v_ref, dst_ref=buf.at[me], send_sem=send.at[k - 1], recv_sem=recv.at[k - 1],
                device_id=peer, device_id_type=MESH)
            cp.start()
            cps.append(cp)
        for k in range(1, 8):
            peer_id = me ^ k
            pltpu.make_async_remote_copy(
                src_ref=v_ref, dst_ref=buf.at[peer_id], send_sem=send.at[k - 1], recv_sem=recv.at[k - 1],
                device_id=(x, y, c), device_id_type=MESH).wait_recv()
        for cp in cps:
            cp.wait_send()
        acc = buf[0]
        for d in range(1, 8):
            acc = acc + buf[d]
        o_ref[...] = acc

    vm = pl.BlockSpec(memory_space=pltpu.VMEM)
    return pl.pallas_call(
        kern, name=name, in_specs=[vm], out_specs=vm, out_shape=_sds((r, 128), F32),
        scratch_shapes=[pltpu.VMEM((8, r, 128), F32), pltpu.SemaphoreType.DMA((7,)),
                        pltpu.SemaphoreType.DMA((7,))])(v)


def _my_chip():
    return 2 * lax.axis_index("x") + lax.axis_index("y")


def _my_core():
    return lax.axis_index("c")


def _cast_place(name, w):
    l, r, c = w.shape
    tr = 64
    return _rw(name, lambda wb: ((wb,), ()), (l, r // tr), [w],
               [pl.BlockSpec((None, tr, c), lambda j, i: (j, i, 0))],
               [_sds((l, N_SHARD, r, c), BF16)],
               [pl.BlockSpec((None, None, tr, c), lambda j, i: (j, _my_chip(), i, 0))])[0]


HALF_TILE = 32


def _add_halves(name, g, got):
    n, h, c = got.shape
    tr, nt = HALF_TILE, h // HALF_TILE
    return _rw(name, lambda ab, bb: ((ab.astype(F32) + bb.astype(F32),), ()), (nt,), [g, got],
               [pl.BlockSpec((n, tr, c), lambda i: (0, _my_core() * nt + i, 0)),
                pl.BlockSpec((n, tr, c), lambda i: (0, i, 0))],
               [_sds((n, h, c), BF16)], [pl.BlockSpec((n, tr, c), lambda i: (0, i, 0))])[0]


def _sum_parts(name, sums, parts, full, layer, n_layer):
    _, h, c = sums.shape
    tr, nt = HALF_TILE, h // HALF_TILE

    def body(own, pb):
        acc = own.astype(F32)
        for j in range(pb.shape[0]):
            acc = acc + pb[j].astype(F32)
        return (acc,), ()

    ins = [sums, parts] + ([full] if full is not None else [])
    in_specs = [pl.BlockSpec((None, tr, c), lambda i: (_my_chip(), i, 0)),
                pl.BlockSpec((parts.shape[0], tr, c), lambda i: (0, i, 0))] + ([ANY] if full is not None else [])
    return _rw(name, body, (nt,), ins, in_specs, [_sds((n_layer, 2 * h, c), F32)],
               [pl.BlockSpec((None, tr, c), lambda i: (layer, _my_core() * nt + i, 0))],
               aliases={2: 0} if full is not None else None)[0]


def _adamw_math(w, g, m, v):
    m = B1 * m + (1.0 - B1) * g
    v = B2 * v + (1.0 - B2) * (g * g)
    m_hat = m / (1.0 - B1 ** STEP)
    v_hat = v / (1.0 - B2 ** STEP)
    delta = -LR * (m_hat / (jnp.sqrt(v_hat) + ADAM_EPS) + WD * w)
    return delta, m, v


def _adamw(name, w, g, m, v):
    r, c = w.shape
    tr = 64 if r % 64 == 0 else 8
    spec = _row_spec(tr, c)
    return _rw(name, lambda *b: (_adamw_math(*b), ()), (r // tr,), [w, g, m, v], [spec] * 4,
               [_sds((r, c), F32)] * 3, [spec] * 3)


SMALL = (("ffn1_norm", (DEPTH, D_MODEL)), ("mix_norm", (DEPTH, D_MODEL)), ("ffn2_norm", (DEPTH, D_MODEL)),
         ("conv_b", (DEPTH, SLAB)), ("conv_ln_g", (DEPTH, SLAB)), ("conv_ln_b", (DEPTH, SLAB)),
         ("ret_norm_g", (DEPTH, SLAB)), ("final_norm", (D_MODEL,)), ("conv_w", (DEPTH, CONV_W, SLAB)))


def _pack(parts, rows):
    flat = jnp.concatenate([p.reshape(-1) for p in parts])
    return jnp.pad(flat, (0, rows * 128 - flat.shape[0])).reshape(rows, 128)


def _unpack(packed, shapes):
    flat = packed.reshape(-1)
    out, off = [], 0
    for shp in shapes:
        n = int(np.prod(shp))
        out.append(flat[off:off + n].reshape(shp))
        off += n
    return out


def kernel(x, ffn1_norm, ffn1_w_in, ffn1_w_out, mix_norm, mix_w_in, conv_w, conv_b, conv_ln_g, conv_ln_b, ret_norm_g, mix_w_out, ffn2_norm, ffn2_w_in, ffn2_w_out, final_norm, loss_target, m_ffn1_norm, m_ffn1_w_in, m_ffn1_w_out, m_mix_norm, m_mix_w_in, m_conv_w, m_conv_b, m_conv_ln_g, m_conv_ln_b, m_ret_norm_g, m_mix_w_out, m_ffn2_norm, m_ffn2_w_in, m_ffn2_w_out, m_final_norm, v_ffn1_norm, v_ffn1_w_in, v_ffn1_w_out, v_mix_norm, v_mix_w_in, v_conv_w, v_conv_b, v_conv_ln_g, v_conv_ln_b, v_ret_norm_g, v_mix_w_out, v_ffn2_norm, v_ffn2_w_in, v_ffn2_w_out, v_final_norm):
    given = dict(locals())
    names = [n for n, _ in SMALL] + list(BIG)
    chip = 2 * lax.axis_index("x") + lax.axis_index("y")
    core = lax.axis_index("c")

    cw_rows = 128
    placed = lax.dynamic_update_slice(jnp.zeros((DEPTH, CONV_W, SLAB), F32), conv_w, (0, 0, chip * HEAD))
    placed = placed * (core == 0).astype(F32)
    conv_w_full = _unpack(_all_sum("gather_conv_w", _pack([placed], cw_rows)), [(DEPTH, CONV_W, SLAB)])[0]

    wt = dict(zip(BIG, _gather_weights([_cast_place("cast_" + n, given[n]) for n in BIG])))
    sm = {n: given[n] for n, _ in SMALL}
    sm["conv_w"] = conv_w_full
    loss, dx, big, small, dfinal = _local_step(x[0], loss_target[0], wt, sm)

    grads = [big[l][n] for n in BIG for l in range(DEPTH)]
    theirs = _pair_exchange(grads)
    sums = [_add_halves(f"chipsum{i}", a, b) for i, (a, b) in enumerate(zip(grads, theirs))]
    parts = _chip_exchange(sums)
    full = []
    for i in range(len(BIG)):
        f = None
        for l in range(DEPTH):
            f = _sum_parts(f"shardsum{DEPTH * i + l}", sums[DEPTH * i + l], parts[DEPTH * i + l], f, l, DEPTH)
        full.append(f)
    g_big = dict(zip(BIG, _pair_join(full)))

    small_parts = []
    for n, shp in SMALL:
        if n == "final_norm":
            small_parts.append(dfinal)
        else:
            small_parts.append(jnp.stack([small[l][n].reshape(shp[1:]) for l in range(DEPTH)]))
    g_small = dict(zip([n for n, _ in SMALL], _unpack(_all_sum("sum_small", _pack(small_parts, 200)), [s_ for _, s_ in SMALL])))
    g_small["conv_w"] = lax.dynamic_slice(g_small["conv_w"], (0, 0, chip * HEAD), (DEPTH, CONV_W, HEAD))

    grad, delta, new_m, new_v = dict(g_small), {}, {}, {}
    grad.update(g_big)
    for n in BIG:
        l, r, c = given[n].shape
        f = lambda t: t.reshape(l * r, c)
        d_, m_, v_ = _adamw("adamw_" + n, f(given[n]), f(grad[n]), f(given["m_" + n]), f(given["v_" + n]))
        delta[n], new_m[n], new_v[n] = d_.reshape(l, r, c), m_.reshape(l, r, c), v_.reshape(l, r, c)
    snames = [n for n, _ in SMALL]
    shapes = [given[n].shape for n in snames]
    rows = 104
    d_, m_, v_ = _adamw("adamw_small", _pack([given[n] for n in snames], rows), _pack([grad[n] for n in snames], rows),
                        _pack([given["m_" + n] for n in snames], rows), _pack([given["v_" + n] for n in snames], rows))
    for dst, packed in ((delta, d_), (new_m, m_), (new_v, v_)):
        dst.update(zip(snames, _unpack(packed, shapes)))

    total = lax.psum(loss[0, 0], ("x", "y", "c"))
    order = ["ffn1_norm", "ffn1_w_in", "ffn1_w_out", "mix_norm", "mix_w_in", "conv_w", "conv_b", "conv_ln_g",
             "conv_ln_b", "ret_norm_g", "mix_w_out", "ffn2_norm", "ffn2_w_in", "ffn2_w_out", "final_norm"]
    return (total, dx[None], *[grad[n] for n in order], *[delta[n] for n in order],
            *[new_m[n] for n in order], *[new_v[n] for n in order])
```

```python
import functools

import numpy as np
import jax
import jax.numpy as jnp
from jax import lax
from jax.experimental import pallas as pl
from jax.experimental.pallas import tpu as pltpu

F32 = jnp.float32
BF16 = jnp.bfloat16

D_MODEL = 1024
D_FF = 2816
N_SHARD = 4
FF_SHARD = 2 * D_FF // N_SHARD
MIX_SHARD = 3072 // N_SHARD
HEAD = 64
SLAB = 256
N_SLAB = 3072 // SLAB
CONV_W = 31
CONV_PAD = 32
CHUNK = 64
EPS = 1e-6
ROPE_BASE = 10000.0
DEPTH = 2

LR, B1, B2, ADAM_EPS, WD, STEP = 0.001, 0.9, 0.999, 1e-08, 0.01, 10

VMEM_LIMIT = 56 * 1024 * 1024


def _params(n_grid, vmem=VMEM_LIMIT):
    return pltpu.CompilerParams(dimension_semantics=("arbitrary",) * n_grid, vmem_limit_bytes=vmem)


def _rw(name, body, grid, ins, in_specs, rows=(), row_specs=(), accs=(), acc_specs=(), aliases=None):
    n_in, n_row = len(ins), len(rows)
    carried = sorted(aliases) if aliases else []

    def kern(*refs):
        vals = [r[...] for i, r in enumerate(refs[:n_in]) if i not in carried]
        row_vals, acc_vals = body(*vals)
        for r, v in zip(refs[n_in:n_in + n_row], row_vals):
            r[...] = v.astype(r.dtype)
        acc_refs = refs[n_in + n_row:]
        if acc_refs:
            first = functools.reduce(jnp.logical_and, [pl.program_id(a) == 0 for a in range(len(grid))])

            @pl.when(first)
            def _():
                for r in acc_refs:
                    r[...] = jnp.zeros(r.shape, r.dtype)

            for r, v in zip(acc_refs, acc_vals):
                r[...] += v.astype(r.dtype)

    return pl.pallas_call(
        kern, name=name, grid=grid, in_specs=list(in_specs), out_specs=list(row_specs) + list(acc_specs),
        out_shape=list(rows) + list(accs), input_output_aliases=dict(aliases or {}),
        compiler_params=_params(len(grid)))(*ins)


def _sds(shape, dtype):
    return jax.ShapeDtypeStruct(shape, dtype)


def _rms(x, g):
    return x * lax.rsqrt(jnp.mean(x * x, axis=-1, keepdims=True) + EPS) * g


def _row_spec(tm, c):
    return pl.BlockSpec((tm, c), lambda i: (i, 0))


def _vec_spec(c):
    return pl.BlockSpec((1, c), lambda i: (0, 0))


def _rms_fwd(name, x, g, tm):
    s, d = x.shape
    return _rw(name, lambda xb, gb: ((_rms(xb, gb),), ()), (s // tm,), [x, g],
               [_row_spec(tm, d), _vec_spec(d)], [_sds((s, d), BF16)], [_row_spec(tm, d)])[0]


def _rms_bwd(name, x, dh, dres, g, tm):
    s, d = x.shape

    def body(xb, dhb, drb, gb):
        _, vjp = jax.vjp(_rms, xb, gb)
        dx, dg = vjp(dhb)
        return (dx + drb,), (dg,)

    return _rw(name, body, (s // tm,), [x, dh, dres, g],
               [_row_spec(tm, d)] * 3 + [_vec_spec(d)], [_sds((s, d), F32)], [_row_spec(tm, d)],
               [_sds((1, d), F32)], [_vec_spec(d)])


def _swiglu(gate, up):
    return jax.nn.silu(gate) * up


def _swiglu_fwd(name, u, tm):
    _, s, c = u.shape
    return _rw(name, lambda ub: ((_swiglu(ub[0:2].astype(F32), ub[2:4].astype(F32)),), ()), (s // tm,), [u],
               [pl.BlockSpec((4, tm, c), lambda i: (0, i, 0))],
               [_sds((2, s, c), BF16)], [pl.BlockSpec((2, tm, c), lambda i: (0, i, 0))])[0]


def _swiglu_bwd(name, u, da, tm):
    _, s, c = u.shape

    def body(ub, dab):
        _, vjp = jax.vjp(_swiglu, ub[0:2].astype(F32), ub[2:4].astype(F32))
        dg, du = vjp(dab.astype(F32))
        return (jnp.concatenate([dg, du], axis=0),), ()

    return _rw(name, body, (s // tm,), [u, da],
               [pl.BlockSpec((4, tm, c), lambda i: (0, i, 0)), pl.BlockSpec((2, tm, c), lambda i: (0, i, 0))],
               [_sds((4, s, c), BF16)], [pl.BlockSpec((4, tm, c), lambda i: (0, i, 0))])[0]


def _ln_silu(y, g, b):
    mu = jnp.mean(y, axis=-1, keepdims=True)
    yc = y - mu
    var = jnp.mean(yc * yc, axis=-1, keepdims=True)
    return jax.nn.silu(yc * lax.rsqrt(var + EPS) * g + b)


def _ln_silu_fwd(name, y, g, b, tm):
    s, c = y.shape
    return _rw(name, lambda yb, gb, bb: ((_ln_silu(yb, gb, bb),), ()), (s // tm,), [y, g, b],
               [_row_spec(tm, c), _vec_spec(c), _vec_spec(c)], [_sds((s, c), BF16)], [_row_spec(tm, c)])[0]


def _ln_silu_bwd(name, y, dcat, g, b, tm):
    s, c = y.shape

    def body(yb, dob, gb, bb):
        _, vjp = jax.vjp(_ln_silu, yb, gb, bb)
        dy, dg, db = vjp(dob)
        return (dy,), (dg, db)

    return _rw(name, body, (s // tm,), [y, dcat, g, b],
               [_row_spec(tm, c), pl.BlockSpec((None, tm, c), lambda i: (0, i, 0)), _vec_spec(c), _vec_spec(c)],
               [_sds((s, c), F32)], [_row_spec(tm, c)],
               [_sds((1, c), F32)] * 2, [_vec_spec(c)] * 2)


def _head_masks():
    lane = np.arange(SLAB) // HEAD
    m = np.zeros((8, SLAB), np.float32)
    for h in range(SLAB // HEAD):
        m[h] = (lane == h)
    return jnp.asarray(m)


def _gated_head_norm(y, gate, g, hm):
    mu = jnp.zeros_like(y)
    for h in range(SLAB // HEAD):
        mu = mu + hm[h:h + 1] * (jnp.sum(y * hm[h:h + 1], axis=-1, keepdims=True) / HEAD)
    yc = y - mu
    var = jnp.zeros_like(y)
    for h in range(SLAB // HEAD):
        var = var + hm[h:h + 1] * (jnp.sum(yc * yc * hm[h:h + 1], axis=-1, keepdims=True) / HEAD)
    return jax.nn.silu(gate) * (yc * lax.rsqrt(var + EPS) * g)


PER_SHARD = MIX_SHARD // SLAB


def _slab_spec(tm, j):
    return pl.BlockSpec((None, tm, SLAB), lambda i: (j, i, 0))


def _proj_slab_spec(tm, j):
    return pl.BlockSpec((None, tm, SLAB), lambda i: (j // PER_SHARD, i, j % PER_SHARD))


def _ghn_fwd(name, y, p32, g, tm):
    s, c = y.shape
    hm = _head_masks()
    return _rw(name, lambda yb, gb, wb, hb: ((_gated_head_norm(yb, gb, wb, hb),), ()), (s // tm,),
               [y, p32, g, hm],
               [_row_spec(tm, c), _proj_slab_spec(tm, 11), _vec_spec(c), pl.BlockSpec((8, c), lambda i: (0, 0))],
               [_sds((s, c), BF16)], [_row_spec(tm, c)])[0]


def _ghn_bwd(name, y, p32, dcat, g, tm):
    s, c = y.shape
    hm = _head_masks()

    def body(yb, gb, dob, wb, hb):
        _, vjp = jax.vjp(lambda a, b_, c_: _gated_head_norm(a, b_, c_, hb), yb, gb, wb)
        dy, dgate, dw = vjp(dob)
        return (dy, dgate), (dw,)

    return _rw(name, body, (s // tm,), [y, p32, dcat, g, hm],
               [_row_spec(tm, c), _proj_slab_spec(tm, 11), _slab_spec(tm, 3), _vec_spec(c),
                pl.BlockSpec((8, c), lambda i: (0, 0))],
               [_sds((s, c), F32)] * 2, [_row_spec(tm, c)] * 2,
               [_sds((1, c), F32)], [_vec_spec(c)])


def _final(name, x, tgt, g, tm):
    s, d = x.shape

    def body(xb, tb, gb):
        yf, vjp = jax.vjp(_rms, xb, gb)
        err = yf - tb
        dx, dg = vjp(err * (1.0 / d))
        part = 0.5 * jnp.sum(jnp.mean(err * err, axis=-1, keepdims=True), axis=0, keepdims=True)
        return (dx,), (dg, jnp.broadcast_to(part, (1, 128)))

    return _rw(name, body, (s // tm,), [x, tgt, g],
               [_row_spec(tm, d), _row_spec(tm, d), _vec_spec(d)],
               [_sds((s, d), F32)], [_row_spec(tm, d)],
               [_sds((1, d), F32), _sds((1, 128), F32)], [_vec_spec(d), _vec_spec(128)])


NN = (((1,), (0,)), ((), ()))
NT = (((1,), (1,)), ((), ()))
TN = (((0,), (0,)), ((), ()))


def _mm(name, a, b, grid, a_spec, b_spec, outs, out_specs, acc_shape, dims, alpha=1.0, res=None, res_spec=None):
    nk = grid[-1]
    n_out = len(outs)

    def kern(*refs):
        a_ref, b_ref = refs[0], refs[1]
        res_ref = refs[2] if res is not None else None
        first_out = 3 if res is not None else 2
        o_refs = refs[first_out:first_out + n_out]
        part = lax.dot_general(a_ref[...].astype(BF16), b_ref[...].astype(BF16), dims,
                               preferred_element_type=F32)

        def finish(r):
            if alpha != 1.0:
                r = r * alpha
            if res_ref is not None:
                r = r + res_ref[...]
            for o in o_refs:
                o[...] = r.astype(o.dtype)

        if nk == 1:
            finish(part)
            return
        acc_ref = refs[-1]
        k = pl.program_id(len(grid) - 1)

        @pl.when(k == 0)
        def _():
            acc_ref[...] = part

        @pl.when(jnp.logical_and(k > 0, k < nk - 1))
        def _():
            acc_ref[...] += part

        @pl.when(k == nk - 1)
        def _():
            finish(acc_ref[...] + part)

    ins = [a, b] + ([res] if res is not None else [])
    in_specs = [a_spec, b_spec] + ([res_spec] if res is not None else [])
    return pl.pallas_call(
        kern, name=name, grid=grid, in_specs=in_specs, out_specs=list(out_specs), out_shape=list(outs),
        scratch_shapes=[pltpu.VMEM(acc_shape, F32)] if nk > 1 else [],
        compiler_params=_params(len(grid)))(*ins)


def _proj_in(name, h, w, l, tm, dtypes):
    s, d = h.shape
    n = w.shape[-1]
    return _mm(name, h, w, (N_SHARD, s // tm, 1),
               pl.BlockSpec((tm, d), lambda b, i, k: (i, 0)),
               pl.BlockSpec((None, None, d, n), lambda b, i, k: (l, b, 0, 0)),
               [_sds((N_SHARD, s, n), t) for t in dtypes],
               [pl.BlockSpec((None, tm, n), lambda b, i, k: (b, i, 0))] * len(dtypes),
               (tm, n), NN)


def _proj_out(name, a, w, l, res, alpha, tm):
    nk, s, r = a.shape
    d = w.shape[-1]
    return _mm(name, a, w, (s // tm, nk),
               pl.BlockSpec((None, tm, r), lambda i, k: (k, i, 0)),
               pl.BlockSpec((None, None, r, d), lambda i, k: (l, k, 0, 0)),
               [_sds((s, d), F32)], [pl.BlockSpec((tm, d), lambda i, k: (i, 0))],
               (tm, d), NN, alpha=alpha, res=res, res_spec=pl.BlockSpec((tm, d), lambda i, k: (i, 0)))[0]


def _back_out(name, dy, w, l, alpha, tm, out_dtype):
    s, d = dy.shape
    nk, r = w.shape[1], w.shape[2]
    return _mm(name, dy, w, (nk, s // tm, 1),
               pl.BlockSpec((tm, d), lambda b, i, k: (i, 0)),
               pl.BlockSpec((None, None, r, d), lambda b, i, k: (l, b, 0, 0)),
               [_sds((nk, s, r), out_dtype)], [pl.BlockSpec((None, tm, r), lambda b, i, k: (b, i, 0))],
               (tm, r), NT, alpha=alpha)[0]


def _back_in(name, du, w, l, tm):
    nk, s, n = du.shape
    d = w.shape[2]
    return _mm(name, du, w, (s // tm, nk),
               pl.BlockSpec((None, tm, n), lambda i, k: (k, i, 0)),
               pl.BlockSpec((None, None, d, n), lambda i, k: (l, k, 0, 0)),
               [_sds((s, d), F32)], [pl.BlockSpec((tm, d), lambda i, k: (i, 0))],
               (tm, d), NT)[0]


def _grad_in(name, h, du, ts):
    s, d = h.shape
    nb, _, n = du.shape
    return _mm(name, h, du, (nb, 1, s // ts),
               pl.BlockSpec((ts, d), lambda b, i, k: (k, 0)),
               pl.BlockSpec((None, ts, n), lambda b, i, k: (b, k, 0)),
               [_sds((nb, d, n), BF16)], [pl.BlockSpec((None, d, n), lambda b, i, k: (b, 0, 0))],
               (d, n), TN)[0]


def _grad_out(name, a, dy, alpha, ts):
    nb, s, r = a.shape
    d = dy.shape[1]
    return _mm(name, a, dy, (nb, 1, s // ts),
               pl.BlockSpec((None, ts, r), lambda b, i, k: (b, k, 0)),
               pl.BlockSpec((ts, d), lambda b, i, k: (k, 0)),
               [_sds((nb, r, d), BF16)], [pl.BlockSpec((None, r, d), lambda b, i, k: (b, 0, 0))],
               (r, d), TN, alpha=alpha)[0]


CONV_TILE = 256


def _shifted(win, off, rows):
    n = win.shape[0]
    return pltpu.roll(win, (n - off) % n, 0)[0:rows] if off % n else win[0:rows]


def _conv_fwd(name, p32, w, bias):
    s = p32.shape[1]
    cb = 128
    nt = s // CONV_TILE

    def kern(a_ref, b_ref, w_ref, bias_ref, y_ref, vpad):
        vpad[0:CONV_PAD, :] = jnp.zeros((CONV_PAD, cb), F32)

        def fill(i, c):
            r = pl.multiple_of(i * CONV_TILE, CONV_TILE)
            vpad[pl.ds(CONV_PAD + r, CONV_TILE), :] = (
                a_ref[pl.ds(r, CONV_TILE), :] * jax.nn.sigmoid(b_ref[pl.ds(r, CONV_TILE), :]))
            return c

        lax.fori_loop(0, nt, fill, 0)

        def tile(i, c):
            r = pl.multiple_of(i * CONV_TILE, CONV_TILE)
            win = vpad[pl.ds(r, CONV_TILE + CONV_PAD), :]
            acc = jnp.broadcast_to(bias_ref[...], (CONV_TILE, cb))
            for j in range(CONV_W):
                acc = acc + w_ref[j:j + 1, :] * _shifted(win, j + 2, CONV_TILE)
            y_ref[pl.ds(r, CONV_TILE), :] = acc
            return c

        lax.fori_loop(0, nt, tile, 0)

    return pl.pallas_call(
        kern, name=name, grid=(SLAB // cb,),
        in_specs=[pl.BlockSpec((None, s, cb), lambda c: (0, 0, c)),
                  pl.BlockSpec((None, s, cb), lambda c: (0, 0, SLAB // cb + c)),
                  pl.BlockSpec((CONV_W, cb), lambda c: (0, c)),
                  pl.BlockSpec((1, cb), lambda c: (0, c))],
        out_specs=pl.BlockSpec((s, cb), lambda c: (0, c)),
        out_shape=_sds((s, SLAB), F32),
        scratch_shapes=[pltpu.VMEM((s + CONV_PAD, cb), F32)],
        compiler_params=_params(1))(p32, p32, w, bias)


def _conv_bwd(name, p32, w, dy):
    s = p32.shape[1]
    cb = 128
    nt = s // CONV_TILE

    def kern(a_ref, b_ref, w_ref, dy_ref, da_ref, db_ref, dw_ref, dbias_ref, vpad, dpad):
        vpad[0:CONV_PAD, :] = jnp.zeros((CONV_PAD, cb), F32)
        dpad[s:s + CONV_PAD, :] = jnp.zeros((CONV_PAD, cb), F32)
        dw_ref[...] = jnp.zeros((CONV_PAD, cb), F32)
        dbias_ref[...] = jnp.zeros((1, cb), F32)

        def fill(i, c):
            r = pl.multiple_of(i * CONV_TILE, CONV_TILE)
            vpad[pl.ds(CONV_PAD + r, CONV_TILE), :] = (
                a_ref[pl.ds(r, CONV_TILE), :] * jax.nn.sigmoid(b_ref[pl.ds(r, CONV_TILE), :]))
            dpad[pl.ds(r, CONV_TILE), :] = dy_ref[pl.ds(r, CONV_TILE), :]
            return c

        lax.fori_loop(0, nt, fill, 0)

        def tile(i, c):
            r = pl.multiple_of(i * CONV_TILE, CONV_TILE)
            dwin = dpad[pl.ds(r, CONV_TILE + CONV_PAD), :]
            vwin = vpad[pl.ds(r, CONV_TILE + CONV_PAD), :]
            dyt = dwin[0:CONV_TILE]
            dv = jnp.zeros((CONV_TILE, cb), F32)
            for j in range(CONV_W):
                dv = dv + w_ref[j:j + 1, :] * _shifted(dwin, CONV_W - 1 - j, CONV_TILE)
                dw_ref[j:j + 1, :] += jnp.sum(dyt * _shifted(vwin, j + 2, CONV_TILE), axis=0, keepdims=True)
            dbias_ref[...] += jnp.sum(dyt, axis=0, keepdims=True)
            a = a_ref[pl.ds(r, CONV_TILE), :]
            sg = jax.nn.sigmoid(b_ref[pl.ds(r, CONV_TILE), :])
            da_ref[pl.ds(r, CONV_TILE), :] = dv * sg
            db_ref[pl.ds(r, CONV_TILE), :] = dv * a * sg * (1.0 - sg)
            return c

        lax.fori_loop(0, nt, tile, 0)

    col = pl.BlockSpec((s, cb), lambda c: (0, c))
    return pl.pallas_call(
        kern, name=name, grid=(SLAB // cb,),
        in_specs=[pl.BlockSpec((None, s, cb), lambda c: (0, 0, c)),
                  pl.BlockSpec((None, s, cb), lambda c: (0, 0, SLAB // cb + c)),
                  pl.BlockSpec((CONV_W, cb), lambda c: (0, c)), col],
        out_specs=[col, col, pl.BlockSpec((CONV_PAD, cb), lambda c: (0, c)), pl.BlockSpec((1, cb), lambda c: (0, c))],
        out_shape=[_sds((s, SLAB), F32), _sds((s, SLAB), F32), _sds((CONV_PAD, SLAB), F32), _sds((1, SLAB), F32)],
        scratch_shapes=[pltpu.VMEM((s + CONV_PAD, cb), F32), pltpu.VMEM((s + CONV_PAD, cb), F32)],
        compiler_params=_params(1))(p32, p32, w, dy)


SB_BLOCK = 256
N_HEAD = SLAB // HEAD


def _split_dot(x, m):
    hi = x.astype(BF16)
    lo = (x - hi.astype(F32)).astype(BF16)
    return (jnp.dot(hi, m, preferred_element_type=F32) + jnp.dot(lo, m, preferred_element_type=F32))


def _sb_logits(qm, k, tri):
    z = lax.dot_general(qm, k, NT, preferred_element_type=F32)
    lb = jnp.minimum(z, 0.0) - jnp.log(1.0 + jnp.exp(-jnp.abs(z)))
    ln = lb - z
    if tri is not None:
        ln = jnp.where(tri, ln, 0.0)
    return lb, ln


def _first_col(x):
    return jnp.broadcast_to(x[:, 0:1], (x.shape[0], 128))


def _head_stack(dst, x, lane_head, bq):
    for h in range(N_HEAD):
        dst[h * bq:(h + 1) * bq, :] = jnp.where(lane_head == h, x, jnp.zeros_like(x))


def _sb_fwd(name, p16):
    s = p16.shape[1]
    bq = min(SB_BLOCK, s)
    nq = s // bq

    def kern(q_ref, k_ref, v_ref, o_ref, qm_ref, v4_ref, w4_ref, acc_ref, r_ref):
        qi = pl.program_id(1)
        lane_head = lax.broadcasted_iota(jnp.int32, (1, SLAB), 1) // HEAD
        _head_stack(qm_ref, (q_ref[...].astype(F32) * (HEAD ** -0.5)).astype(BF16), lane_head, bq)
        row = lax.broadcasted_iota(jnp.int32, (bq, bq), 0)
        col = lax.broadcasted_iota(jnp.int32, (bq, bq), 1)
        after = (row > col).astype(BF16)
        tri = col < row
        acc_ref[...] = jnp.zeros((bq, SLAB), F32)
        r_ref[...] = jnp.zeros((N_HEAD, bq, 128), F32)

        def tile(kb, masked):
            rows = pl.ds(pl.multiple_of(kb * bq, bq), bq)
            k = k_ref[rows, :]
            _head_stack(v4_ref, v_ref[rows, :], lane_head, bq)
            for h in range(N_HEAD):
                lb, ln = _sb_logits(qm_ref[h * bq:(h + 1) * bq, :], k, tri if masked else None)
                rem = jnp.dot(ln.astype(BF16), after, preferred_element_type=F32)
                w = jnp.exp(lb + rem + r_ref[h][:, 0:1])
                if masked:
                    w = jnp.where(tri, w, 0.0)
                w4_ref[:, h * bq:(h + 1) * bq] = w.astype(BF16)
                r_ref[h] += _first_col(rem + ln)
            acc_ref[...] += jnp.dot(w4_ref[...], v4_ref[...], preferred_element_type=F32)

        tile(qi, True)

        def step(i, c):
            tile(qi - i, False)
            return c

        lax.fori_loop(1, qi + 1, step, 0)
        o_ref[...] = acc_ref[...]

    return pl.pallas_call(
        kern, name=name, grid=(2, nq),
        in_specs=[pl.BlockSpec((None, bq, SLAB), lambda g, i: ((2 + g) // PER_SHARD, i, (2 + g) % PER_SHARD)),
                  pl.BlockSpec((None, s, SLAB), lambda g, i: ((4 + g) // PER_SHARD, 0, (4 + g) % PER_SHARD)),
                  pl.BlockSpec((None, s, SLAB), lambda g, i: ((6 + g) // PER_SHARD, 0, (6 + g) % PER_SHARD))],
        out_specs=pl.BlockSpec((None, bq, SLAB), lambda g, i: (g, i, 0)),
        out_shape=_sds((2, s, SLAB), F32),
        scratch_shapes=[pltpu.VMEM((N_HEAD * bq, SLAB), BF16), pltpu.VMEM((N_HEAD * bq, SLAB), BF16),
                        pltpu.VMEM((bq, N_HEAD * bq), BF16), pltpu.VMEM((bq, SLAB), F32),
                        pltpu.VMEM((N_HEAD, bq, 128), F32)],
        compiler_params=_params(2))(p16, p16, p16)


def _sb_bwd(name, p16, o, dcat):
    s = p16.shape[1]
    bq = min(SB_BLOCK, s)
    nq = s // bq

    def kern(q_ref, k_ref, v_ref, o_ref, do_ref, dq_ref, dk_hbm, dv_hbm, dk_acc, dv_acc, dq_acc,
             qm_ref, dom_ref, k4_ref, dzc_ref, dzs_ref, ws_ref, t_ref, r_ref, c_ref, sem):
        g, qi = pl.program_id(0), pl.program_id(1)

        @pl.when(qi == 0)
        def _():
            dk_acc[...] = jnp.zeros((s, SLAB), F32)
            dv_acc[...] = jnp.zeros((s, SLAB), F32)

        lane_head = lax.broadcasted_iota(jnp.int32, (1, SLAB), 1) // HEAD
        _head_stack(qm_ref, (q_ref[...].astype(F32) * (HEAD ** -0.5)).astype(BF16), lane_head, bq)
        dob = do_ref[...].astype(BF16)
        _head_stack(dom_ref, dob, lane_head, bq)
        prod = dob.astype(F32) * o_ref[...]
        for h in range(N_HEAD):
            t_ref[h] = jnp.broadcast_to(
                jnp.sum(jnp.where(lane_head == h, prod, 0.0), axis=1, keepdims=True), (bq, 128))
        row = lax.broadcasted_iota(jnp.int32, (bq, bq), 0)
        col = lax.broadcasted_iota(jnp.int32, (bq, bq), 1)
        after = (row > col).astype(BF16)
        from_ = (row >= col).astype(BF16)
        tri = col < row
        dq_acc[...] = jnp.zeros((bq, SLAB), F32)
        r_ref[...] = jnp.zeros((N_HEAD, bq, 128), F32)
        c_ref[...] = jnp.zeros((N_HEAD, bq, 128), F32)

        def tile(kb, masked):
            rows = pl.ds(pl.multiple_of(kb * bq, bq), bq)
            k, v = k_ref[rows, :], v_ref[rows, :]
            _head_stack(k4_ref, k, lane_head, bq)
            for h in range(N_HEAD):
                mine = slice(h * bq, (h + 1) * bq)
                lb, ln = _sb_logits(qm_ref[mine, :], k, tri if masked else None)
                rem = jnp.dot(ln.astype(BF16), after, preferred_element_type=F32)
                w = jnp.exp(lb + rem + r_ref[h][:, 0:1])
                if masked:
                    w = jnp.where(tri, w, 0.0)
                wb = w.astype(BF16)
                dl = wb.astype(F32) * lax.dot_general(dom_ref[mine, :], v, NT, preferred_element_type=F32)
                suffix = _split_dot(dl, from_)
                before = t_ref[h][:, 0:1] - (suffix + c_ref[h][:, 0:1])
                dz = dl - jnp.exp(lb) * (dl + before)
                if masked:
                    dz = jnp.where(tri, dz, 0.0)
                dzb = dz.astype(BF16)
                dzc_ref[:, mine] = dzb
                dzs_ref[mine, :] = dzb
                ws_ref[mine, :] = wb
                r_ref[h] += _first_col(rem + ln)
                c_ref[h] += _first_col(suffix)
            dq_acc[...] += jnp.dot(dzc_ref[...], k4_ref[...], preferred_element_type=F32)
            dk_acc[rows, :] += lax.dot_general(dzs_ref[...], qm_ref[...], TN, preferred_element_type=F32)
            dv_acc[rows, :] += lax.dot_general(ws_ref[...], dom_ref[...], TN, preferred_element_type=F32)

        tile(qi, True)

        def step(i, c):
            tile(qi - i, False)
            return c

        lax.fori_loop(1, qi + 1, step, 0)
        dq_ref[...] = dq_acc[...] * (HEAD ** -0.5)

        @pl.when(qi == nq - 1)
        def _():
            ck = pltpu.make_async_copy(dk_acc, dk_hbm.at[g], sem.at[0])
            cv = pltpu.make_async_copy(dv_acc, dv_hbm.at[g], sem.at[1])
            ck.start()
            cv.start()
            ck.wait()
            cv.wait()

    blk = lambda j0: pl.BlockSpec((None, bq, SLAB), lambda g, i: (j0 + g, i, 0))
    full = lambda j0: pl.BlockSpec((None, s, SLAB), lambda g, i: ((j0 + g) // PER_SHARD, 0, (j0 + g) % PER_SHARD))
    q_blk = pl.BlockSpec((None, bq, SLAB), lambda g, i: ((2 + g) // PER_SHARD, i, (2 + g) % PER_SHARD))
    stack16 = pltpu.VMEM((N_HEAD * bq, SLAB), BF16)
    return pl.pallas_call(
        kern, name=name, grid=(2, nq),
        in_specs=[q_blk, full(4), full(6), blk(0), blk(1)],
        out_specs=[blk(0), pl.BlockSpec(memory_space=pl.ANY), pl.BlockSpec(memory_space=pl.ANY)],
        out_shape=[_sds((2, s, SLAB), F32)] * 3,
        scratch_shapes=[pltpu.VMEM((s, SLAB), F32), pltpu.VMEM((s, SLAB), F32), pltpu.VMEM((bq, SLAB), F32),
                        stack16, stack16, stack16, pltpu.VMEM((bq, N_HEAD * bq), BF16),
                        pltpu.VMEM((N_HEAD * bq, bq), BF16), pltpu.VMEM((N_HEAD * bq, bq), BF16),
                        pltpu.VMEM((N_HEAD, bq, 128), F32), pltpu.VMEM((N_HEAD, bq, 128), F32),
                        pltpu.VMEM((N_HEAD, bq, 128), F32), pltpu.SemaphoreType.DMA((2,))],
        compiler_params=_params(2))(p16, p16, p16, o, dcat)


RET_BLOCK = 256


def _ret_tables(s, bl):
    nh = SLAB // HEAD
    lane_h = np.arange(SLAB) // HEAD
    log_gamma = np.log1p(-np.exp2(-5.0 - np.arange(nh, dtype=np.float64)))
    lg_lane = log_gamma[lane_h]
    half = HEAD // 2
    inv = 1.0 / (ROPE_BASE ** (np.arange(half, dtype=np.float64) / half))
    ang = np.arange(s, dtype=np.float64)[:, None] * inv[None, :]
    within = np.arange(SLAB) % HEAD
    cos = np.cos(ang)[:, within % half]
    sin = np.sin(ang)[:, within % half] * np.where(within < half, -1.0, 1.0)[None, :]
    perm = np.zeros((SLAB, SLAB))
    partner = np.where(within < half, np.arange(SLAB) + half, np.arange(SLAB) - half)
    perm[partner, np.arange(SLAB)] = 1.0
    i = np.arange(bl)
    diff = i[:, None] - i[None, :]
    same = (i[:, None] // CHUNK) == (i[None, :] // CHUNK)
    earlier = (i[None, :] // CHUNK) < (i[:, None] // CHUNK)
    decay = np.zeros((nh, bl, bl))
    for h in range(nh):
        decay[h] = np.where(same, np.exp(log_gamma[h] * np.abs(diff)),
                            np.where(earlier, np.exp(log_gamma[h] * diff), 0.0))
    qd = np.exp(lg_lane[None, :] * (i[:, None] + 1.0))
    kd = np.exp(lg_lane[None, :] * (bl - 1.0 - i[:, None]))
    gam = np.exp(lg_lane * bl)[:, None] * np.ones((1, SLAB))
    bd = (lane_h[:, None] == lane_h[None, :]).astype(np.float64)
    f = lambda a: jnp.asarray(a, F32)
    return f(cos), f(sin), f(perm), f(decay), f(qd), f(kd), f(gam), f(bd)


def _ret_block(q, k, v, state, cos, sin, perm, decay, qd, kd, gam, bd, hm):
    qr = (q * cos + jnp.dot(q, perm, preferred_element_type=F32) * sin) * (HEAD ** -0.5)
    kr = k * cos + jnp.dot(k, perm, preferred_element_type=F32) * sin
    y = jnp.dot(qr * qd, state, preferred_element_type=F32)
    for h in range(SLAB // HEAD):
        m = hm[h:h + 1]
        sc = lax.dot_general(qr * m, kr, NT, preferred_element_type=F32) * decay[h]
        y = y + jnp.dot(sc, v * m, preferred_element_type=F32)
    new_state = gam * state + lax.dot_general(kr * kd, v, TN, preferred_element_type=F32) * bd
    return y, new_state


def _ret_specs(s, bl, rev):
    nb = s // bl
    pos = (lambda n: nb - 1 - n) if rev else (lambda n: n)
    slab = lambda j: pl.BlockSpec((None, bl, SLAB), lambda n: (j // PER_SHARD, pos(n), j % PER_SHARD))
    const2 = lambda r: pl.BlockSpec((r, SLAB), lambda n: (0, 0))
    tab = [pl.BlockSpec((bl, SLAB), lambda n: (pos(n), 0))] * 2 + [
        const2(SLAB), pl.BlockSpec((SLAB // HEAD, bl, bl), lambda n: (0, 0, 0)),
        const2(bl), const2(bl), const2(SLAB), const2(SLAB), const2(8)]
    return nb, pos, slab, tab


def _ret_fwd(name, p32):
    s = p32.shape[1]
    bl = min(RET_BLOCK, s)
    nb, pos, slab, tab = _ret_specs(s, bl, False)
    tables = _ret_tables(s, bl) + (_head_masks(),)

    def kern(q_ref, k_ref, v_ref, *rest):
        t_refs, (y_ref, st_ref, state) = rest[:9], rest[9:]

        @pl.when(pl.program_id(0) == 0)
        def _():
            state[...] = jnp.zeros((SLAB, SLAB), F32)

        st_ref[...] = state[...]
        y, new = _ret_block(q_ref[...], k_ref[...], v_ref[...], state[...], *[t[...] for t in t_refs])
        y_ref[...] = y
        state[...] = new

    return pl.pallas_call(
        kern, name=name, grid=(nb,), in_specs=[slab(8), slab(9), slab(10)] + tab,
        out_specs=[pl.BlockSpec((bl, SLAB), lambda n: (n, 0)), pl.BlockSpec((None, SLAB, SLAB), lambda n: (n, 0, 0))],
        out_shape=[_sds((s, SLAB), F32), _sds((nb, SLAB, SLAB), F32)],
        scratch_shapes=[pltpu.VMEM((SLAB, SLAB), F32)], compiler_params=_params(1))(p32, p32, p32, *tables)


def _ret_bwd(name, p32, states, dy):
    s = p32.shape[1]
    bl = min(RET_BLOCK, s)
    nb, pos, slab, tab = _ret_specs(s, bl, True)
    tables = _ret_tables(s, bl) + (_head_masks(),)
    rowblk = pl.BlockSpec((bl, SLAB), lambda n: (pos(n), 0))

    def kern(q_ref, k_ref, v_ref, st_ref, dy_ref, *rest):
        t_refs, (dq_ref, dk_ref, dv_ref, dstate) = rest[:9], rest[9:]

        @pl.when(pl.program_id(0) == 0)
        def _():
            dstate[...] = jnp.zeros((SLAB, SLAB), F32)

        tv = [t[...] for t in t_refs]
        _, vjp = jax.vjp(lambda a, b, c, d: _ret_block(a, b, c, d, *tv),
                         q_ref[...], k_ref[...], v_ref[...], st_ref[...])
        dq, dk, dv, ds = vjp((dy_ref[...], dstate[...]))
        dq_ref[...] = dq
        dk_ref[...] = dk
        dv_ref[...] = dv
        dstate[...] = ds

    return pl.pallas_call(
        kern, name=name, grid=(nb,),
        in_specs=[slab(8), slab(9), slab(10), pl.BlockSpec((None, SLAB, SLAB), lambda n: (pos(n), 0, 0)), rowblk] + tab,
        out_specs=[rowblk] * 3, out_shape=[_sds((s, SLAB), F32)] * 3,
        scratch_shapes=[pltpu.VMEM((SLAB, SLAB), F32)], compiler_params=_params(1))(p32, p32, p32, states, dy, *tables)


TM_FFN = 1024
TM_SLAB = 2048
TM_RW = 256
TM_FF = 128


def _ffn_fwd(tag, x, g, w_in, w_out, l):
    tm = min(TM_FFN, x.shape[0])
    h = _rms_fwd(tag + "_rms", x, g, TM_RW)
    u = _proj_in(tag + "_in", h, w_in, l, tm, [BF16])[0]
    a = _swiglu_fwd(tag + "_act", u, TM_FF)
    w_out2 = w_out.reshape(DEPTH, 2, FF_SHARD, D_MODEL)
    xn = _proj_out(tag + "_out", a, w_out2, l, x, 0.5, tm)
    return xn, (x, h, u, a)


def _ffn_bwd(tag, saved, dxn, g, w_in, w_out, l):
    x, h, u, a = saved
    tm = min(TM_FFN, x.shape[0])
    w_out2 = w_out.reshape(DEPTH, 2, FF_SHARD, D_MODEL)
    da = _back_out(tag + "_dact", dxn, w_out2, l, 0.5, tm, BF16)
    dw_out = _grad_out(tag + "_dwout", a, dxn, 0.5, tm)
    du = _swiglu_bwd(tag + "_dswi", u, da, TM_FF)
    dh = _back_in(tag + "_dh", du, w_in, l, tm)
    dw_in = _grad_in(tag + "_dwin", h, du, tm)
    dx, dg = _rms_bwd(tag + "_drms", x, dh, dxn, g, TM_RW)
    return dx, dg, dw_in, dw_out.reshape(N_SHARD, D_FF // N_SHARD, D_MODEL)


def _mix_fwd(tag, x, sm, w_in, w_out, l):
    h = _rms_fwd(tag + "_rms", x, sm["mix_norm"][l:l + 1], TM_RW)
    p32, p16 = _proj_in(tag + "_in", h, w_in, l, min(TM_FFN, x.shape[0]), [F32, BF16])
    ypre = _conv_fwd(tag + "_conv", p32, sm["conv_w"][l], sm["conv_b"][l:l + 1])
    yconv = _ln_silu_fwd(tag + "_ln", ypre, sm["conv_ln_g"][l:l + 1], sm["conv_ln_b"][l:l + 1], TM_RW)
    osb = _sb_fwd(tag + "_sb", p16)
    yr, states = _ret_fwd(tag + "_ret", p32)
    yret = _ghn_fwd(tag + "_ghn", yr, p32, sm["ret_norm_g"][l:l + 1], TM_RW)
    ycat = jnp.stack([yconv, osb[0].astype(BF16), osb[1].astype(BF16), yret])
    xn = _proj_out(tag + "_out", ycat, w_out, l, x, 1.0, min(TM_FFN, x.shape[0]))
    return xn, (x, h, p32, p16, ypre, osb, yr, states, ycat)


def _mix_bwd(tag, saved, dxn, sm, w_in, w_out, l):
    x, h, p32, p16, ypre, osb, yr, states, ycat = saved
    ts = min(TM_SLAB, x.shape[0])
    dcat = _back_out(tag + "_dcat", dxn, w_out, l, 1.0, ts, F32)
    dw_out = _grad_out(tag + "_dwout", ycat, dxn, 1.0, ts)
    dypre, dlg, dlb = _ln_silu_bwd(tag + "_dln", ypre, dcat, sm["conv_ln_g"][l:l + 1], sm["conv_ln_b"][l:l + 1], TM_RW)
    da, db, dcw, dcb = _conv_bwd(tag + "_dconv", p32, sm["conv_w"][l], dypre)
    dq, dk, dv = _sb_bwd(tag + "_dsb", p16, osb, dcat)
    dyr, dgate, drg = _ghn_bwd(tag + "_dghn", yr, p32, dcat, sm["ret_norm_g"][l:l + 1], TM_RW)
    dqr, dkr, dvr = _ret_bwd(tag + "_dret", p32, states, dyr)
    slabs = [da, db, dq[0], dq[1], dk[0], dk[1], dv[0], dv[1], dqr, dkr, dvr, dgate]
    dp = jnp.stack([jnp.concatenate([t.astype(BF16) for t in slabs[PER_SHARD * b:PER_SHARD * (b + 1)]], axis=-1)
                    for b in range(N_SHARD)])
    tm = min(TM_FFN, x.shape[0])
    dh = _back_in(tag + "_dh", dp, w_in, l, tm)
    dw_in = _grad_in(tag + "_dwin", h, dp, tm)
    dx, dg = _rms_bwd(tag + "_drms", x, dh, dxn, sm["mix_norm"][l:l + 1], TM_RW)
    small = dict(mix_norm=dg, conv_w=dcw[0:CONV_W], conv_b=dcb, conv_ln_g=dlg, conv_ln_b=dlb, ret_norm_g=drg)
    return dx, small, dw_in, dw_out


def _local_step(x, tgt, wt, sm):
    saved = []
    for l in range(DEPTH):
        x, s1 = _ffn_fwd(f"l{l}f1", x, sm["ffn1_norm"][l:l + 1], wt["ffn1_w_in"], wt["ffn1_w_out"], l)
        x, s2 = _mix_fwd(f"l{l}mx", x, sm, wt["mix_w_in"], wt["mix_w_out"], l)
        x, s3 = _ffn_fwd(f"l{l}f2", x, sm["ffn2_norm"][l:l + 1], wt["ffn2_w_in"], wt["ffn2_w_out"], l)
        saved.append((s1, s2, s3))
    dx, dfinal, loss = _final("final", x, tgt, sm["final_norm"][None, :], TM_RW)
    big = [None] * DEPTH
    small = [None] * DEPTH
    for l in reversed(range(DEPTH)):
        s1, s2, s3 = saved[l]
        dx, dg3, dwi3, dwo3 = _ffn_bwd(f"l{l}f2", s3, dx, sm["ffn2_norm"][l:l + 1], wt["ffn2_w_in"], wt["ffn2_w_out"], l)
        dx, sml, dwi2, dwo2 = _mix_bwd(f"l{l}mx", s2, dx, sm, wt["mix_w_in"], wt["mix_w_out"], l)
        dx, dg1, dwi1, dwo1 = _ffn_bwd(f"l{l}f1", s1, dx, sm["ffn1_norm"][l:l + 1], wt["ffn1_w_in"], wt["ffn1_w_out"], l)
        big[l] = dict(ffn1_w_in=dwi1, ffn1_w_out=dwo1, mix_w_in=dwi2, mix_w_out=dwo2, ffn2_w_in=dwi3, ffn2_w_out=dwo3)
        sml.update(ffn1_norm=dg1, ffn2_norm=dg3)
        small[l] = sml
    return loss, dx, big, small, dfinal


MESH = pl.DeviceIdType.MESH
ANY = pl.BlockSpec(memory_space=pl.ANY)
BIG = ("ffn1_w_in", "ffn1_w_out", "mix_w_in", "mix_w_out", "ffn2_w_in", "ffn2_w_out")


def _place():
    x, y, c = lax.axis_index("x"), lax.axis_index("y"), lax.axis_index("c")
    chips = [(1 - x, y), (x, 1 - y), (1 - x, 1 - y)]
    return x, y, c, chips


def _gather_weights(w16):
    n = len(w16)

    def kern(*refs):
        dst = refs[n:2 * n]
        send, recv = refs[2 * n:]
        x, y, c, chips = _place()
        mine = 2 * x + y
        firsts, passes = [], []
        for a in range(n):
            h = dst[a].shape[2] // 2
            own = dst[a].at[:, mine, pl.ds(c * h, h)]
            for j, (cx, cy) in enumerate(chips):
                cp = pltpu.make_async_remote_copy(
                    src_ref=own, dst_ref=own, send_sem=send.at[6 * a + j], recv_sem=recv.at[6 * a + j],
                    device_id=(cx, cy, c), device_id_type=MESH)
                cp.start()
                firsts.append(cp)
        for a in range(n):
            h = dst[a].shape[2] // 2
            half = pl.ds(c * h, h)
            for j, (cx, cy) in enumerate(chips):
                theirs = dst[a].at[:, 2 * cx + cy, half]
                pltpu.make_async_remote_copy(
                    src_ref=theirs, dst_ref=theirs, send_sem=send.at[6 * a + j], recv_sem=recv.at[6 * a + j],
                    device_id=(cx, cy, c), device_id_type=MESH).wait_recv()
                fw = pltpu.make_async_remote_copy(
                    src_ref=theirs, dst_ref=theirs, send_sem=send.at[6 * a + 3 + j], recv_sem=recv.at[6 * a + 3 + j],
                    device_id=(x, y, 1 - c), device_id_type=MESH)
                fw.start()
                passes.append(fw)
        for a in range(n):
            h = dst[a].shape[2] // 2
            other = pl.ds((1 - c) * h, h)
            for j, (cx, cy) in enumerate(chips):
                got = dst[a].at[:, 2 * cx + cy, other]
                pltpu.make_async_remote_copy(
                    src_ref=got, dst_ref=got, send_sem=send.at[6 * a + 3 + j], recv_sem=recv.at[6 * a + 3 + j],
                    device_id=(x, y, 1 - c), device_id_type=MESH).wait_recv()
        for cp in firsts + passes:
            cp.wait_send()

    return pl.pallas_call(
        kern, name="gather_weights", in_specs=[ANY] * n, out_specs=[ANY] * n,
        out_shape=[_sds(w.shape, w.dtype) for w in w16], input_output_aliases={a: a for a in range(n)},
        scratch_shapes=[pltpu.SemaphoreType.DMA((6 * n,)), pltpu.SemaphoreType.DMA((6 * n,))])(*w16)


def _pair_exchange(grads):
    n = len(grads)

    def kern(*refs):
        src, got_o = refs[:n], refs[n:2 * n]
        send, recv = refs[2 * n:]
        x, y, c, _ = _place()
        cps = []
        for a in range(n):
            h = src[a].shape[1] // 2
            cp = pltpu.make_async_remote_copy(
                src_ref=src[a].at[:, pl.ds((1 - c) * h, h)], dst_ref=got_o[a],
                send_sem=send.at[a], recv_sem=recv.at[a], device_id=(x, y, 1 - c), device_id_type=MESH)
            cp.start()
            cps.append(cp)
        for cp in cps:
            cp.wait()

    halves = [_sds((g.shape[0], g.shape[1] // 2, g.shape[2]), g.dtype) for g in grads]
    return pl.pallas_call(
        kern, name="pair_exchange", in_specs=[ANY] * n, out_specs=[ANY] * n, out_shape=halves,
        scratch_shapes=[pltpu.SemaphoreType.DMA((n,)), pltpu.SemaphoreType.DMA((n,))])(*grads)


def _chip_exchange(sums):
    n = len(sums)

    def kern(*refs):
        src, dst = refs[:n], refs[n:2 * n]
        send, recv = refs[2 * n:]
        x, y, c, chips = _place()
        cps = []
        for a in range(n):
            for j, (cx, cy) in enumerate(chips):
                cp = pltpu.make_async_remote_copy(
                    src_ref=src[a].at[2 * cx + cy], dst_ref=dst[a].at[j],
                    send_sem=send.at[3 * a + j], recv_sem=recv.at[3 * a + j],
                    device_id=(cx, cy, c), device_id_type=MESH)
                cp.start()
                cps.append(cp)
        for cp in cps:
            cp.wait()

    return pl.pallas_call(
        kern, name="chip_exchange", in_specs=[ANY] * n, out_specs=[ANY] * n,
        out_shape=[_sds((3,) + s_.shape[1:], s_.dtype) for s_ in sums],
        scratch_shapes=[pltpu.SemaphoreType.DMA((3 * n,)), pltpu.SemaphoreType.DMA((3 * n,))])(*sums)


def _pair_join(full):
    n = len(full)

    def kern(*refs):
        dst = refs[n:2 * n]
        send, recv = refs[2 * n:]
        x, y, c, _ = _place()
        cps = []
        for a in range(n):
            h = dst[a].shape[1] // 2
            mine = dst[a].at[:, pl.ds(c * h, h)]
            cp = pltpu.make_async_remote_copy(
                src_ref=mine, dst_ref=mine, send_sem=send.at[a], recv_sem=recv.at[a],
                device_id=(x, y, 1 - c), device_id_type=MESH)
            cp.start()
            cps.append(cp)
        for a, cp in enumerate(cps):
            cp.wait_send()
            h = dst[a].shape[1] // 2
            got = dst[a].at[:, pl.ds((1 - c) * h, h)]
            pltpu.make_async_remote_copy(
                src_ref=got, dst_ref=got, send_sem=send.at[a], recv_sem=recv.at[a],
                device_id=(x, y, 1 - c), device_id_type=MESH).wait_recv()

    return pl.pallas_call(
        kern, name="pair_join", in_specs=[ANY] * n, out_specs=[ANY] * n,
        out_shape=[_sds(f.shape, f.dtype) for f in full], input_output_aliases={a: a for a in range(n)},
        scratch_shapes=[pltpu.SemaphoreType.DMA((n,)), pltpu.SemaphoreType.DMA((n,))])(*full)


def _all_sum(name, v):
    r = v.shape[0]

    def kern(v_ref, o_ref, buf, send, recv):
        x, y, c, _ = _place()
        me = 4 * x + 2 * y + c
        buf[me] = v_ref[...]
        cps = []
        for k in range(1, 8):
            peer = (x ^ (k >> 2), y ^ ((k >> 1) & 1), c ^ (k & 1))
            cp = pltpu.make_async_remote_copy(
                src_ref=v_ref, dst_ref=buf.at[me], send_sem=send.at[k - 1], recv_sem=recv.at[k - 1],
                device_id=peer, device_id_type=MESH)
            cp.start()
            cps.append(cp)
        for k in range(1, 8):
            peer_id = me ^ k
            pltpu.make_async_remote_copy(
                src_ref=v_ref, dst_ref=buf.at[peer_id], send_sem=send.at[k - 1], recv_sem=recv.at[k - 1],
                device_id=(x, y, c), device_id_type=MESH).wait_recv()
        for cp in cps:
            cp.wait_send()
        acc = buf[0]
        for d in range(1, 8):
            acc = acc + buf[d]
        o_ref[...] = acc

    vm = pl.BlockSpec(memory_space=pltpu.VMEM)
    return pl.pallas_call(
        kern, name=name, in_specs=[vm], out_specs=vm, out_shape=_sds((r, 128), F32),
        scratch_shapes=[pltpu.VMEM((8, r, 128), F32), pltpu.SemaphoreType.DMA((7,)),
                        pltpu.SemaphoreType.DMA((7,))])(v)


def _my_chip():
    return 2 * lax.axis_index("x") + lax.axis_index("y")


def _my_core():
    return lax.axis_index("c")


def _cast_place(name, w):
    l, r, c = w.shape
    tr = 64
    return _rw(name, lambda wb: ((wb,), ()), (l, r // tr), [w],
               [pl.BlockSpec((None, tr, c), lambda j, i: (j, i, 0))],
               [_sds((l, N_SHARD, r, c), BF16)],
               [pl.BlockSpec((None, None, tr, c), lambda j, i: (j, _my_chip(), i, 0))])[0]


HALF_TILE = 32


def _add_halves(name, g, got):
    n, h, c = got.shape
    tr, nt = HALF_TILE, h // HALF_TILE
    return _rw(name, lambda ab, bb: ((ab.astype(F32) + bb.astype(F32),), ()), (nt,), [g, got],
               [pl.BlockSpec((n, tr, c), lambda i: (0, _my_core() * nt + i, 0)),
                pl.BlockSpec((n, tr, c), lambda i: (0, i, 0))],
               [_sds((n, h, c), BF16)], [pl.BlockSpec((n, tr, c), lambda i: (0, i, 0))])[0]


def _sum_parts(name, sums, parts, full, layer, n_layer):
    _, h, c = sums.shape
    tr, nt = HALF_TILE, h // HALF_TILE

    def body(own, pb):
        acc = own.astype(F32)
        for j in range(pb.shape[0]):
            acc = acc + pb[j].astype(F32)
        return (acc,), ()

    ins = [sums, parts] + ([full] if full is not None else [])
    in_specs = [pl.BlockSpec((None, tr, c), lambda i: (_my_chip(), i, 0)),
                pl.BlockSpec((parts.shape[0], tr, c), lambda i: (0, i, 0))] + ([ANY] if full is not None else [])
    return _rw(name, body, (nt,), ins, in_specs, [_sds((n_layer, 2 * h, c), F32)],
               [pl.BlockSpec((None, tr, c), lambda i: (layer, _my_core() * nt + i, 0))],
               aliases={2: 0} if full is not None else None)[0]


def _adamw_math(w, g, m, v):
    m = B1 * m + (1.0 - B1) * g
    v = B2 * v + (1.0 - B2) * (g * g)
    m_hat = m / (1.0 - B1 ** STEP)
    v_hat = v / (1.0 - B2 ** STEP)
    delta = -LR * (m_hat / (jnp.sqrt(v_hat) + ADAM_EPS) + WD * w)
    return delta, m, v


def _adamw(name, w, g, m, v):
    r, c = w.shape
    tr = 64 if r % 64 == 0 else 8
    spec = _row_spec(tr, c)
    return _rw(name, lambda *b: (_adamw_math(*b), ()), (r // tr,), [w, g, m, v], [spec] * 4,
               [_sds((r, c), F32)] * 3, [spec] * 3)


SMALL = (("ffn1_norm", (DEPTH, D_MODEL)), ("mix_norm", (DEPTH, D_MODEL)), ("ffn2_norm", (DEPTH, D_MODEL)),
         ("conv_b", (DEPTH, SLAB)), ("conv_ln_g", (DEPTH, SLAB)), ("conv_ln_b", (DEPTH, SLAB)),
         ("ret_norm_g", (DEPTH, SLAB)), ("final_norm", (D_MODEL,)), ("conv_w", (DEPTH, CONV_W, SLAB)))


def _pack(parts, rows):
    flat = jnp.concatenate([p.reshape(-1) for p in parts])
    return jnp.pad(flat, (0, rows * 128 - flat.shape[0])).reshape(rows, 128)


def _unpack(packed, shapes):
    flat = packed.reshape(-1)
    out, off = [], 0
    for shp in shapes:
        n = int(np.prod(shp))
        out.append(flat[off:off + n].reshape(shp))
        off += n
    return out


def kernel(x, ffn1_norm, ffn1_w_in, ffn1_w_out, mix_norm, mix_w_in, conv_w, conv_b, conv_ln_g, conv_ln_b, ret_norm_g, mix_w_out, ffn2_norm, ffn2_w_in, ffn2_w_out, final_norm, loss_target, m_ffn1_norm, m_ffn1_w_in, m_ffn1_w_out, m_mix_norm, m_mix_w_in, m_conv_w, m_conv_b, m_conv_ln_g, m_conv_ln_b, m_ret_norm_g, m_mix_w_out, m_ffn2_norm, m_ffn2_w_in, m_ffn2_w_out, m_final_norm, v_ffn1_norm, v_ffn1_w_in, v_ffn1_w_out, v_mix_norm, v_mix_w_in, v_conv_w, v_conv_b, v_conv_ln_g, v_conv_ln_b, v_ret_norm_g, v_mix_w_out, v_ffn2_norm, v_ffn2_w_in, v_ffn2_w_out, v_final_norm):
    given = dict(locals())
    names = [n for n, _ in SMALL] + list(BIG)
    chip = 2 * lax.axis_index("x") + lax.axis_index("y")
    core = lax.axis_index("c")

    cw_rows = 128
    placed = lax.dynamic_update_slice(jnp.zeros((DEPTH, CONV_W, SLAB), F32), conv_w, (0, 0, chip * HEAD))
    placed = placed * (core == 0).astype(F32)
    conv_w_full = _unpack(_all_sum("gather_conv_w", _pack([placed], cw_rows)), [(DEPTH, CONV_W, SLAB)])[0]

    wt = dict(zip(BIG, _gather_weights([_cast_place("cast_" + n, given[n]) for n in BIG])))
    sm = {n: given[n] for n, _ in SMALL}
    sm["conv_w"] = conv_w_full
    loss, dx, big, small, dfinal = _local_step(x[0], loss_target[0], wt, sm)

    grads = [big[l][n] for n in BIG for l in range(DEPTH)]
    theirs = _pair_exchange(grads)
    sums = [_add_halves(f"chipsum{i}", a, b) for i, (a, b) in enumerate(zip(grads, theirs))]
    parts = _chip_exchange(sums)
    full = []
    for i in range(len(BIG)):
        f = None
        for l in range(DEPTH):
            f = _sum_parts(f"shardsum{DEPTH * i + l}", sums[DEPTH * i + l], parts[DEPTH * i + l], f, l, DEPTH)
        full.append(f)
    g_big = dict(zip(BIG, _pair_join(full)))

    small_parts = []
    for n, shp in SMALL:
        if n == "final_norm":
            small_parts.append(dfinal)
        else:
            small_parts.append(jnp.stack([small[l][n].reshape(shp[1:]) for l in range(DEPTH)]))
    g_small = dict(zip([n for n, _ in SMALL], _unpack(_all_sum("sum_small", _pack(small_parts, 200)), [s_ for _, s_ in SMALL])))
    g_small["conv_w"] = lax.dynamic_slice(g_small["conv_w"], (0, 0, chip * HEAD), (DEPTH, CONV_W, HEAD))

    grad, delta, new_m, new_v = dict(g_small), {}, {}, {}
    grad.update(g_big)
    for n in BIG:
        l, r, c = given[n].shape
        f = lambda t: t.reshape(l * r, c)
        d_, m_, v_ = _adamw("adamw_" + n, f(given[n]), f(grad[n]), f(given["m_" + n]), f(given["v_" + n]))
        delta[n], new_m[n], new_v[n] = d_.reshape(l, r, c), m_.reshape(l, r, c), v_.reshape(l, r, c)
    snames = [n for n, _ in SMALL]
    shapes = [given[n].shape for n in snames]
    rows = 104
    d_, m_, v_ = _adamw("adamw_small", _pack([given[n] for n in snames], rows), _pack([grad[n] for n in snames], rows),
                        _pack([given["m_" + n] for n in snames], rows), _pack([given["v_" + n] for n in snames], rows))
    for dst, packed in ((delta, d_), (new_m, m_), (new_v, v_)):
        dst.update(zip(snames, _unpack(packed, shapes)))

    total = lax.psum(loss[0, 0], ("x", "y", "c"))
    order = ["ffn1_norm", "ffn1_w_in", "ffn1_w_out", "mix_norm", "mix_w_in", "conv_w", "conv_b", "conv_ln_g",
             "conv_ln_b", "ret_norm_g", "mix_w_out", "ffn2_norm", "ffn2_w_in", "ffn2_w_out", "final_norm"]
    return (total, dx[None], *[grad[n] for n in order], *[delta[n] for n in order],
            *[new_m[n] for n in order], *[new_v[n] for n in order])
```

```python
import functools

import numpy as np
import jax
import jax.numpy as jnp
from jax import lax
from jax.experimental import pallas as pl
from jax.experimental.pallas import tpu as pltpu

F32 = jnp.float32
BF16 = jnp.bfloat16

D_MODEL = 1024
D_FF = 2816
N_SHARD = 4
FF_SHARD = 2 * D_FF // N_SHARD
MIX_SHARD = 3072 // N_SHARD
HEAD = 64
SLAB = 256
N_SLAB = 3072 // SLAB
CONV_W = 31
CONV_PAD = 32
CHUNK = 64
EPS = 1e-6
ROPE_BASE = 10000.0
DEPTH = 2

LR, B1, B2, ADAM_EPS, WD, STEP = 0.001, 0.9, 0.999, 1e-08, 0.01, 10

VMEM_LIMIT = 56 * 1024 * 1024


def _params(n_grid):
    return pltpu.CompilerParams(dimension_semantics=("arbitrary",) * n_grid, vmem_limit_bytes=VMEM_LIMIT)


def _rw(name, body, grid, ins, in_specs, rows=(), row_specs=(), accs=(), acc_specs=(), aliases=None):
    n_in, n_row = len(ins), len(rows)
    carried = sorted(aliases) if aliases else []

    def kern(*refs):
        vals = [r[...] for i, r in enumerate(refs[:n_in]) if i not in carried]
        row_vals, acc_vals = body(*vals)
        for r, v in zip(refs[n_in:n_in + n_row], row_vals):
            r[...] = v.astype(r.dtype)
        acc_refs = refs[n_in + n_row:]
        if acc_refs:
            first = functools.reduce(jnp.logical_and, [pl.program_id(a) == 0 for a in range(len(grid))])

            @pl.when(first)
            def _():
                for r in acc_refs:
                    r[...] = jnp.zeros(r.shape, r.dtype)

            for r, v in zip(acc_refs, acc_vals):
                r[...] += v.astype(r.dtype)

    return pl.pallas_call(
        kern, name=name, grid=grid, in_specs=list(in_specs), out_specs=list(row_specs) + list(acc_specs),
        out_shape=list(rows) + list(accs), input_output_aliases=dict(aliases or {}),
        compiler_params=_params(len(grid)))(*ins)


def _sds(shape, dtype):
    return jax.ShapeDtypeStruct(shape, dtype)


def _rms(x, g):
    return x * lax.rsqrt(jnp.mean(x * x, axis=-1, keepdims=True) + EPS) * g


def _row_spec(tm, c):
    return pl.BlockSpec((tm, c), lambda i: (i, 0))


def _vec_spec(c):
    return pl.BlockSpec((1, c), lambda i: (0, 0))


def _swiglu(gate, up):
    return jax.nn.silu(gate) * up


def _swiglu_fwd(name, u, tm):
    _, s, c = u.shape
    return _rw(name, lambda ub: ((_swiglu(ub[0:2].astype(F32), ub[2:4].astype(F32)),), ()), (s // tm,), [u],
               [pl.BlockSpec((4, tm, c), lambda i: (0, i, 0))],
               [_sds((2, s, c), BF16)], [pl.BlockSpec((2, tm, c), lambda i: (0, i, 0))])[0]


def _swiglu_bwd(name, u, da, tm):
    _, s, c = u.shape

    def body(ub, dab):
        _, vjp = jax.vjp(_swiglu, ub[0:2].astype(F32), ub[2:4].astype(F32))
        dg, du = vjp(dab.astype(F32))
        return (jnp.concatenate([dg, du], axis=0),), ()

    return _rw(name, body, (s // tm,), [u, da],
               [pl.BlockSpec((4, tm, c), lambda i: (0, i, 0)), pl.BlockSpec((2, tm, c), lambda i: (0, i, 0))],
               [_sds((4, s, c), BF16)], [pl.BlockSpec((4, tm, c), lambda i: (0, i, 0))])[0]


def _ln_silu(y, g, b):
    mu = jnp.mean(y, axis=-1, keepdims=True)
    yc = y - mu
    var = jnp.mean(yc * yc, axis=-1, keepdims=True)
    return jax.nn.silu(yc * lax.rsqrt(var + EPS) * g + b)


def _ln_silu_fwd(name, y, g, b, tm):
    s, c = y.shape
    return _rw(name, lambda yb, gb, bb: ((_ln_silu(yb, gb, bb),), ()), (s // tm,), [y, g, b],
               [_row_spec(tm, c), _vec_spec(c), _vec_spec(c)], [_sds((s, c), BF16)], [_row_spec(tm, c)])[0]


def _ln_silu_bwd(name, y, dcat, g, b, tm):
    s, c = y.shape

    def body(yb, dob, gb, bb):
        _, vjp = jax.vjp(_ln_silu, yb, gb, bb)
        dy, dg, db = vjp(dob)
        return (dy,), (dg, db)

    return _rw(name, body, (s // tm,), [y, dcat, g, b],
               [_row_spec(tm, c), pl.BlockSpec((None, tm, c), lambda i: (0, i, 0)), _vec_spec(c), _vec_spec(c)],
               [_sds((s, c), F32)], [_row_spec(tm, c)],
               [_sds((1, c), F32)] * 2, [_vec_spec(c)] * 2)


def _head_masks():
    lane = np.arange(SLAB) // HEAD
    m = np.zeros((8, SLAB), np.float32)
    for h in range(SLAB // HEAD):
        m[h] = (lane == h)
    return jnp.asarray(m)


def _gated_head_norm(y, gate, g, hm):
    mu = jnp.zeros_like(y)
    for h in range(SLAB // HEAD):
        mu = mu + hm[h:h + 1] * (jnp.sum(y * hm[h:h + 1], axis=-1, keepdims=True) / HEAD)
    yc = y - mu
    var = jnp.zeros_like(y)
    for h in range(SLAB // HEAD):
        var = var + hm[h:h + 1] * (jnp.sum(yc * yc * hm[h:h + 1], axis=-1, keepdims=True) / HEAD)
    return jax.nn.silu(gate) * (yc * lax.rsqrt(var + EPS) * g)


PER_SHARD = MIX_SHARD // SLAB


def _slab_spec(tm, j):
    return pl.BlockSpec((None, tm, SLAB), lambda i: (j, i, 0))


def _proj_slab_spec(tm, j):
    return pl.BlockSpec((None, tm, SLAB), lambda i: (j // PER_SHARD, i, j % PER_SHARD))


def _ghn_fwd(name, y, p32, g, tm):
    s, c = y.shape
    hm = _head_masks()
    return _rw(name, lambda yb, gb, wb, hb: ((_gated_head_norm(yb, gb, wb, hb),), ()), (s // tm,),
               [y, p32, g, hm],
               [_row_spec(tm, c), _proj_slab_spec(tm, 11), _vec_spec(c), pl.BlockSpec((8, c), lambda i: (0, 0))],
               [_sds((s, c), BF16)], [_row_spec(tm, c)])[0]


def _ghn_bwd(name, y, p32, dcat, g, tm):
    s, c = y.shape
    hm = _head_masks()

    def body(yb, gb, dob, wb, hb):
        _, vjp = jax.vjp(lambda a, b_, c_: _gated_head_norm(a, b_, c_, hb), yb, gb, wb)
        dy, dgate, dw = vjp(dob)
        return (dy, dgate), (dw,)

    return _rw(name, body, (s // tm,), [y, p32, dcat, g, hm],
               [_row_spec(tm, c), _proj_slab_spec(tm, 11), _slab_spec(tm, 3), _vec_spec(c),
                pl.BlockSpec((8, c), lambda i: (0, 0))],
               [_sds((s, c), F32)] * 2, [_row_spec(tm, c)] * 2,
               [_sds((1, c), F32)], [_vec_spec(c)])


def _assemble(name, parts, per, tm):
    s = parts[0][0].shape[-2]
    specs = [_row_spec(tm, SLAB) if j is None else pl.BlockSpec((None, tm, SLAB), lambda i, j=j: (j, i, 0))
             for _, j in parts]

    def body(*blocks):
        rows = [jnp.concatenate([b.astype(BF16) for b in blocks[per * q:per * (q + 1)]], axis=-1)
                for q in range(len(blocks) // per)]
        return (jnp.stack(rows),), ()

    nq = len(parts) // per
    return _rw(name, body, (s // tm,), [a for a, _ in parts], specs, [_sds((nq, s, per * SLAB), BF16)],
               [pl.BlockSpec((nq, tm, per * SLAB), lambda i: (0, i, 0))])[0]


def _final(name, x, tgt, g, tm):
    s, d = x.shape

    def body(xb, tb, gb):
        yf, vjp = jax.vjp(_rms, xb, gb)
        err = yf - tb
        dx, dg = vjp(err * (1.0 / d))
        part = 0.5 * jnp.sum(jnp.mean(err * err, axis=-1, keepdims=True), axis=0, keepdims=True)
        return (dx,), (dg, jnp.broadcast_to(part, (1, 128)))

    return _rw(name, body, (s // tm,), [x, tgt, g],
               [_row_spec(tm, d), _row_spec(tm, d), _vec_spec(d)],
               [_sds((s, d), F32)], [_row_spec(tm, d)],
               [_sds((1, d), F32), _sds((1, 128), F32)], [_vec_spec(d), _vec_spec(128)])


NN = (((1,), (0,)), ((), ()))
NT = (((1,), (1,)), ((), ()))
TN = (((0,), (0,)), ((), ()))


def _mm(name, a, b, grid, a_spec, b_spec, outs, out_specs, acc_shape, dims, alpha=1.0, res=None, res_spec=None):
    nk = grid[-1]
    n_out = len(outs)

    def kern(*refs):
        a_ref, b_ref = refs[0], refs[1]
        res_ref = refs[2] if res is not None else None
        first_out = 3 if res is not None else 2
        o_refs = refs[first_out:first_out + n_out]
        part = lax.dot_general(a_ref[...].astype(BF16), b_ref[...].astype(BF16), dims,
                               preferred_element_type=F32)

        def finish(r):
            if alpha != 1.0:
                r = r * alpha
            if res_ref is not None:
                r = r + res_ref[...]
            for o in o_refs:
                o[...] = r.astype(o.dtype)

        if nk == 1:
            finish(part)
            return
        acc_ref = refs[-1]
        k = pl.program_id(len(grid) - 1)

        @pl.when(k == 0)
        def _():
            acc_ref[...] = part

        @pl.when(jnp.logical_and(k > 0, k < nk - 1))
        def _():
            acc_ref[...] += part

        @pl.when(k == nk - 1)
        def _():
            finish(acc_ref[...] + part)

    ins = [a, b] + ([res] if res is not None else [])
    in_specs = [a_spec, b_spec] + ([res_spec] if res is not None else [])
    return pl.pallas_call(
        kern, name=name, grid=grid, in_specs=in_specs, out_specs=list(out_specs), out_shape=list(outs),
        scratch_shapes=[pltpu.VMEM(acc_shape, F32)] if nk > 1 else [],
        compiler_params=_params(len(grid)))(*ins)


def _norm_proj_in(name, x, g, w, l, tm, dtypes):
    s, d = x.shape
    n = w.shape[-1]
    n_out = len(dtypes)

    def kern(x_ref, g_ref, w_ref, h_ref, *rest):
        o_refs, h_vmem = rest[:n_out], rest[n_out]

        @pl.when(pl.program_id(1) == 0)
        def _():
            h = _rms(x_ref[...], g_ref[...]).astype(BF16)
            h_vmem[...] = h
            h_ref[...] = h

        r = jnp.dot(h_vmem[...], w_ref[...], preferred_element_type=F32)
        for o in o_refs:
            o[...] = r.astype(o.dtype)

    out = pl.pallas_call(
        kern, name=name, grid=(s // tm, N_SHARD),
        in_specs=[pl.BlockSpec((tm, d), lambda i, b: (i, 0)), pl.BlockSpec((1, d), lambda i, b: (0, 0)),
                  pl.BlockSpec((None, None, d, n), lambda i, b: (l, b, 0, 0))],
        out_specs=[pl.BlockSpec((tm, d), lambda i, b: (i, 0))] +
                  [pl.BlockSpec((None, tm, n), lambda i, b: (b, i, 0))] * n_out,
        out_shape=[_sds((s, d), BF16)] + [_sds((N_SHARD, s, n), t) for t in dtypes],
        scratch_shapes=[pltpu.VMEM((tm, d), BF16)], compiler_params=_params(2))(x, g, w)
    return out[0], out[1:]


def _back_in_norm(name, du, w, l, tm, x, dres, g):
    nk, s, n = du.shape
    d = w.shape[2]

    def kern(du_ref, w_ref, x_ref, dres_ref, g_ref, dx_ref, dg_ref, acc_ref):
        i, k = pl.program_id(0), pl.program_id(1)
        part = lax.dot_general(du_ref[...], w_ref[...], NT, preferred_element_type=F32)

        @pl.when(k == 0)
        def _():
            acc_ref[...] = part

        @pl.when(jnp.logical_and(k > 0, k < nk - 1))
        def _():
            acc_ref[...] += part

        @pl.when(k == nk - 1)
        def _():
            _, vjp = jax.vjp(_rms, x_ref[...], g_ref[...])
            dx, dg = vjp(acc_ref[...] + part)
            dx_ref[...] = dx + dres_ref[...]

            @pl.when(i == 0)
            def _():
                dg_ref[...] = dg

            @pl.when(i > 0)
            def _():
                dg_ref[...] += dg

    row = pl.BlockSpec((tm, d), lambda i, k: (i, 0))
    vec = pl.BlockSpec((1, d), lambda i, k: (0, 0))
    return pl.pallas_call(
        kern, name=name, grid=(s // tm, nk),
        in_specs=[pl.BlockSpec((None, tm, n), lambda i, k: (k, i, 0)),
                  pl.BlockSpec((None, None, d, n), lambda i, k: (l, k, 0, 0)), row, row, vec],
        out_specs=[row, vec], out_shape=[_sds((s, d), F32), _sds((1, d), F32)],
        scratch_shapes=[pltpu.VMEM((tm, d), F32)], compiler_params=_params(2))(du, w, x, dres, g)


def _proj_out(name, a, w, l, res, alpha, tm):
    nk, s, r = a.shape
    d = w.shape[-1]
    return _mm(name, a, w, (s // tm, nk),
               pl.BlockSpec((None, tm, r), lambda i, k: (k, i, 0)),
               pl.BlockSpec((None, None, r, d), lambda i, k: (l, k, 0, 0)),
               [_sds((s, d), F32)], [pl.BlockSpec((tm, d), lambda i, k: (i, 0))],
               (tm, d), NN, alpha=alpha, res=res, res_spec=pl.BlockSpec((tm, d), lambda i, k: (i, 0)))[0]


def _back_out(name, dy, w, l, alpha, tm, out_dtype):
    s, d = dy.shape
    nk, r = w.shape[1], w.shape[2]
    return _mm(name, dy, w, (nk, s // tm, 1),
               pl.BlockSpec((tm, d), lambda b, i, k: (i, 0)),
               pl.BlockSpec((None, None, r, d), lambda b, i, k: (l, b, 0, 0)),
               [_sds((nk, s, r), out_dtype)], [pl.BlockSpec((None, tm, r), lambda b, i, k: (b, i, 0))],
               (tm, r), NT, alpha=alpha)[0]


def _grad_in(name, h, du, ts):
    s, d = h.shape
    nb, _, n = du.shape
    return _mm(name, h, du, (nb, 1, s // ts),
               pl.BlockSpec((ts, d), lambda b, i, k: (k, 0)),
               pl.BlockSpec((None, ts, n), lambda b, i, k: (b, k, 0)),
               [_sds((nb, d, n), BF16)], [pl.BlockSpec((None, d, n), lambda b, i, k: (b, 0, 0))],
               (d, n), TN)[0]


def _grad_out(name, a, dy, alpha, ts):
    nb, s, r = a.shape
    d = dy.shape[1]
    return _mm(name, a, dy, (nb, 1, s // ts),
               pl.BlockSpec((None, ts, r), lambda b, i, k: (b, k, 0)),
               pl.BlockSpec((ts, d), lambda b, i, k: (k, 0)),
               [_sds((nb, r, d), BF16)], [pl.BlockSpec((None, r, d), lambda b, i, k: (b, 0, 0))],
               (r, d), TN, alpha=alpha)[0]


CONV_TILE = 256


def _shifted(win, off, rows):
    n = win.shape[0]
    return pltpu.roll(win, (n - off) % n, 0)[0:rows] if off % n else win[0:rows]


def _conv_fwd(name, p32, w, bias):
    s = p32.shape[1]
    cb = 128
    nt = s // CONV_TILE

    def kern(a_ref, b_ref, w_ref, bias_ref, y_ref, vpad):
        vpad[0:CONV_PAD, :] = jnp.zeros((CONV_PAD, cb), F32)

        def fill(i, c):
            r = pl.multiple_of(i * CONV_TILE, CONV_TILE)
            vpad[pl.ds(CONV_PAD + r, CONV_TILE), :] = (
                a_ref[pl.ds(r, CONV_TILE), :] * jax.nn.sigmoid(b_ref[pl.ds(r, CONV_TILE), :]))
            return c

        lax.fori_loop(0, nt, fill, 0)

        def tile(i, c):
            r = pl.multiple_of(i * CONV_TILE, CONV_TILE)
            win = vpad[pl.ds(r, CONV_TILE + CONV_PAD), :]
            acc = jnp.broadcast_to(bias_ref[...], (CONV_TILE, cb))
            for j in range(CONV_W):
                acc = acc + w_ref[j:j + 1, :] * _shifted(win, j + 2, CONV_TILE)
            y_ref[pl.ds(r, CONV_TILE), :] = acc
            return c

        lax.fori_loop(0, nt, tile, 0)

    return pl.pallas_call(
        kern, name=name, grid=(SLAB // cb,),
        in_specs=[pl.BlockSpec((None, s, cb), lambda c: (0, 0, c)),
                  pl.BlockSpec((None, s, cb), lambda c: (0, 0, SLAB // cb + c)),
                  pl.BlockSpec((CONV_W, cb), lambda c: (0, c)),
                  pl.BlockSpec((1, cb), lambda c: (0, c))],
        out_specs=pl.BlockSpec((s, cb), lambda c: (0, c)),
        out_shape=_sds((s, SLAB), F32),
        scratch_shapes=[pltpu.VMEM((s + CONV_PAD, cb), F32)],
        compiler_params=_params(1))(p32, p32, w, bias)


def _conv_bwd(name, p32, w, dy):
    s = p32.shape[1]
    cb = 128
    nt = s // CONV_TILE

    def kern(a_ref, b_ref, w_ref, dy_ref, da_ref, db_ref, dw_ref, dbias_ref, vpad, dpad):
        vpad[0:CONV_PAD, :] = jnp.zeros((CONV_PAD, cb), F32)
        dpad[s:s + CONV_PAD, :] = jnp.zeros((CONV_PAD, cb), F32)
        dw_ref[...] = jnp.zeros((CONV_PAD, cb), F32)
        dbias_ref[...] = jnp.zeros((1, cb), F32)

        def fill(i, c):
            r = pl.multiple_of(i * CONV_TILE, CONV_TILE)
            vpad[pl.ds(CONV_PAD + r, CONV_TILE), :] = (
                a_ref[pl.ds(r, CONV_TILE), :] * jax.nn.sigmoid(b_ref[pl.ds(r, CONV_TILE), :]))
            dpad[pl.ds(r, CONV_TILE), :] = dy_ref[pl.ds(r, CONV_TILE), :]
            return c

        lax.fori_loop(0, nt, fill, 0)

        def tile(i, c):
            r = pl.multiple_of(i * CONV_TILE, CONV_TILE)
            dwin = dpad[pl.ds(r, CONV_TILE + CONV_PAD), :]
            vwin = vpad[pl.ds(r, CONV_TILE + CONV_PAD), :]
            dyt = dwin[0:CONV_TILE]
            dv = jnp.zeros((CONV_TILE, cb), F32)
            for j in range(CONV_W):
                dv = dv + w_ref[j:j + 1, :] * _shifted(dwin, CONV_W - 1 - j, CONV_TILE)
                dw_ref[j:j + 1, :] += jnp.sum(dyt * _shifted(vwin, j + 2, CONV_TILE), axis=0, keepdims=True)
            dbias_ref[...] += jnp.sum(dyt, axis=0, keepdims=True)
            a = a_ref[pl.ds(r, CONV_TILE), :]
            sg = jax.nn.sigmoid(b_ref[pl.ds(r, CONV_TILE), :])
            da_ref[pl.ds(r, CONV_TILE), :] = dv * sg
            db_ref[pl.ds(r, CONV_TILE), :] = dv * a * sg * (1.0 - sg)
            return c

        lax.fori_loop(0, nt, tile, 0)

    col = pl.BlockSpec((s, cb), lambda c: (0, c))
    return pl.pallas_call(
        kern, name=name, grid=(SLAB // cb,),
        in_specs=[pl.BlockSpec((None, s, cb), lambda c: (0, 0, c)),
                  pl.BlockSpec((None, s, cb), lambda c: (0, 0, SLAB // cb + c)),
                  pl.BlockSpec((CONV_W, cb), lambda c: (0, c)), col],
        out_specs=[col, col, pl.BlockSpec((CONV_PAD, cb), lambda c: (0, c)), pl.BlockSpec((1, cb), lambda c: (0, c))],
        out_shape=[_sds((s, SLAB), F32), _sds((s, SLAB), F32), _sds((CONV_PAD, SLAB), F32), _sds((1, SLAB), F32)],
        scratch_shapes=[pltpu.VMEM((s + CONV_PAD, cb), F32), pltpu.VMEM((s + CONV_PAD, cb), F32)],
        compiler_params=_params(1))(p32, p32, w, dy)


SB_BLOCK = 256
N_HEAD = SLAB // HEAD


def _split_dot(x, m):
    hi = x.astype(BF16)
    lo = (x - hi.astype(F32)).astype(BF16)
    return (jnp.dot(hi, m, preferred_element_type=F32) + jnp.dot(lo, m, preferred_element_type=F32))


def _sb_logits(qm, k, tri):
    z = lax.dot_general(qm, k, NT, preferred_element_type=F32)
    lb = jnp.minimum(z, 0.0) - jnp.log(1.0 + jnp.exp(-jnp.abs(z)))
    ln = lb - z
    if tri is not None:
        ln = jnp.where(tri, ln, 0.0)
    return lb, ln


def _first_col(x):
    return jnp.broadcast_to(x[:, 0:1], (x.shape[0], 128))


def _head_stack(dst, x, lane_head, bq):
    for h in range(N_HEAD):
        dst[h * bq:(h + 1) * bq, :] = jnp.where(lane_head == h, x, jnp.zeros_like(x))


def _sb_fwd(name, p16):
    s = p16.shape[1]
    bq = min(SB_BLOCK, s)
    nq = s // bq

    def kern(q_ref, k_ref, v_ref, o_ref, qm_ref, v4_ref, w4_ref, acc_ref, r_ref):
        qi = pl.program_id(1)
        lane_head = lax.broadcasted_iota(jnp.int32, (1, SLAB), 1) // HEAD
        _head_stack(qm_ref, (q_ref[...].astype(F32) * (HEAD ** -0.5)).astype(BF16), lane_head, bq)
        row = lax.broadcasted_iota(jnp.int32, (bq, bq), 0)
        col = lax.broadcasted_iota(jnp.int32, (bq, bq), 1)
        after = (row > col).astype(BF16)
        tri = col < row
        acc_ref[...] = jnp.zeros((bq, SLAB), F32)
        r_ref[...] = jnp.zeros((N_HEAD, bq, 128), F32)

        def tile(kb, masked):
            rows = pl.ds(pl.multiple_of(kb * bq, bq), bq)
            k = k_ref[rows, :]
            _head_stack(v4_ref, v_ref[rows, :], lane_head, bq)
            for h in range(N_HEAD):
                lb, ln = _sb_logits(qm_ref[h * bq:(h + 1) * bq, :], k, tri if masked else None)
                rem = jnp.dot(ln.astype(BF16), after, preferred_element_type=F32)
                w = jnp.exp(lb + rem + r_ref[h][:, 0:1])
                if masked:
                    w = jnp.where(tri, w, 0.0)
                w4_ref[:, h * bq:(h + 1) * bq] = w.astype(BF16)
                r_ref[h] += _first_col(rem + ln)
            acc_ref[...] += jnp.dot(w4_ref[...], v4_ref[...], preferred_element_type=F32)

        tile(qi, True)

        def step(i, c):
            tile(qi - i, False)
            return c

        lax.fori_loop(1, qi + 1, step, 0)
        o_ref[...] = acc_ref[...]

    return pl.pallas_call(
        kern, name=name, grid=(2, nq),
        in_specs=[pl.BlockSpec((None, bq, SLAB), lambda g, i: ((2 + g) // PER_SHARD, i, (2 + g) % PER_SHARD)),
                  pl.BlockSpec((None, s, SLAB), lambda g, i: ((4 + g) // PER_SHARD, 0, (4 + g) % PER_SHARD)),
                  pl.BlockSpec((None, s, SLAB), lambda g, i: ((6 + g) // PER_SHARD, 0, (6 + g) % PER_SHARD))],
        out_specs=pl.BlockSpec((None, bq, SLAB), lambda g, i: (g, i, 0)),
        out_shape=_sds((2, s, SLAB), F32),
        scratch_shapes=[pltpu.VMEM((N_HEAD * bq, SLAB), BF16), pltpu.VMEM((N_HEAD * bq, SLAB), BF16),
                        pltpu.VMEM((bq, N_HEAD * bq), BF16), pltpu.VMEM((bq, SLAB), F32),
                        pltpu.VMEM((N_HEAD, bq, 128), F32)],
        compiler_params=_params(2))(p16, p16, p16)


def _sb_bwd(name, p16, o, dcat):
    s = p16.shape[1]
    bq = min(SB_BLOCK, s)
    nq = s // bq

    def kern(q_ref, k_ref, v_ref, o_ref, do_ref, dq_ref, dk_hbm, dv_hbm, dk_acc, dv_acc, dq_acc,
             qm_ref, dom_ref, k4_ref, dzc_ref, dzs_ref, ws_ref, t_ref, r_ref, c_ref, sem):
        g, qi = pl.program_id(0), pl.program_id(1)

        @pl.when(qi == 0)
        def _():
            dk_acc[...] = jnp.zeros((s, SLAB), F32)
            dv_acc[...] = jnp.zeros((s, SLAB), F32)

        lane_head = lax.broadcasted_iota(jnp.int32, (1, SLAB), 1) // HEAD
        _head_stack(qm_ref, (q_ref[...].astype(F32) * (HEAD ** -0.5)).astype(BF16), lane_head, bq)
        dob = do_ref[...].astype(BF16)
        _head_stack(dom_ref, dob, lane_head, bq)
        prod = dob.astype(F32) * o_ref[...]
        for h in range(N_HEAD):
            t_ref[h] = jnp.broadcast_to(
                jnp.sum(jnp.where(lane_head == h, prod, 0.0), axis=1, keepdims=True), (bq, 128))
        row = lax.broadcasted_iota(jnp.int32, (bq, bq), 0)
        col = lax.broadcasted_iota(jnp.int32, (bq, bq), 1)
        after = (row > col).astype(BF16)
        from_ = (row >= col).astype(BF16)
        tri = col < row
        dq_acc[...] = jnp.zeros((bq, SLAB), F32)
        r_ref[...] = jnp.zeros((N_HEAD, bq, 128), F32)
        c_ref[...] = jnp.zeros((N_HEAD, bq, 128), F32)

        def tile(kb, masked):
            rows = pl.ds(pl.multiple_of(kb * bq, bq), bq)
            k, v = k_ref[rows, :], v_ref[rows, :]
            _head_stack(k4_ref, k, lane_head, bq)
            for h in range(N_HEAD):
                mine = slice(h * bq, (h + 1) * bq)
                lb, ln = _sb_logits(qm_ref[mine, :], k, tri if masked else None)
                rem = jnp.dot(ln.astype(BF16), after, preferred_element_type=F32)
                w = jnp.exp(lb + rem + r_ref[h][:, 0:1])
                if masked:
                    w = jnp.where(tri, w, 0.0)
                wb = w.astype(BF16)
                dl = wb.astype(F32) * lax.dot_general(dom_ref[mine, :], v, NT, preferred_element_type=F32)
                suffix = _split_dot(dl, from_)
                before = t_ref[h][:, 0:1] - (suffix + c_ref[h][:, 0:1])
                dz = dl - jnp.exp(lb) * (dl + before)
                if masked:
                    dz = jnp.where(tri, dz, 0.0)
                dzb = dz.astype(BF16)
                dzc_ref[:, mine] = dzb
                dzs_ref[mine, :] = dzb
                ws_ref[mine, :] = wb
                r_ref[h] += _first_col(rem + ln)
                c_ref[h] += _first_col(suffix)
            dq_acc[...] += jnp.dot(dzc_ref[...], k4_ref[...], preferred_element_type=F32)
            dk_acc[rows, :] += lax.dot_general(dzs_ref[...], qm_ref[...], TN, preferred_element_type=F32)
            dv_acc[rows, :] += lax.dot_general(ws_ref[...], dom_ref[...], TN, preferred_element_type=F32)

        tile(qi, True)

        def step(i, c):
            tile(qi - i, False)
            return c

        lax.fori_loop(1, qi + 1, step, 0)
        dq_ref[...] = dq_acc[...] * (HEAD ** -0.5)

        @pl.when(qi == nq - 1)
        def _():
            ck = pltpu.make_async_copy(dk_acc, dk_hbm.at[g], sem.at[0])
            cv = pltpu.make_async_copy(dv_acc, dv_hbm.at[g], sem.at[1])
            ck.start()
            cv.start()
            ck.wait()
            cv.wait()

    blk = lambda j0: pl.BlockSpec((None, bq, SLAB), lambda g, i: (j0 + g, i, 0))
    full = lambda j0: pl.BlockSpec((None, s, SLAB), lambda g, i: ((j0 + g) // PER_SHARD, 0, (j0 + g) % PER_SHARD))
    q_blk = pl.BlockSpec((None, bq, SLAB), lambda g, i: ((2 + g) // PER_SHARD, i, (2 + g) % PER_SHARD))
    stack16 = pltpu.VMEM((N_HEAD * bq, SLAB), BF16)
    return pl.pallas_call(
        kern, name=name, grid=(2, nq),
        in_specs=[q_blk, full(4), full(6), blk(0), blk(1)],
        out_specs=[blk(0), pl.BlockSpec(memory_space=pl.ANY), pl.BlockSpec(memory_space=pl.ANY)],
        out_shape=[_sds((2, s, SLAB), F32)] * 3,
        scratch_shapes=[pltpu.VMEM((s, SLAB), F32), pltpu.VMEM((s, SLAB), F32), pltpu.VMEM((bq, SLAB), F32),
                        stack16, stack16, stack16, pltpu.VMEM((bq, N_HEAD * bq), BF16),
                        pltpu.VMEM((N_HEAD * bq, bq), BF16), pltpu.VMEM((N_HEAD * bq, bq), BF16),
                        pltpu.VMEM((N_HEAD, bq, 128), F32), pltpu.VMEM((N_HEAD, bq, 128), F32),
                        pltpu.VMEM((N_HEAD, bq, 128), F32), pltpu.SemaphoreType.DMA((2,))],
        compiler_params=_params(2))(p16, p16, p16, o, dcat)


RET_BLOCK = 256


def _ret_tables(s, bl):
    nh = SLAB // HEAD
    lane_h = np.arange(SLAB) // HEAD
    log_gamma = np.log1p(-np.exp2(-5.0 - np.arange(nh, dtype=np.float64)))
    lg_lane = log_gamma[lane_h]
    half = HEAD // 2
    inv = 1.0 / (ROPE_BASE ** (np.arange(half, dtype=np.float64) / half))
    ang = np.arange(s, dtype=np.float64)[:, None] * inv[None, :]
    within = np.arange(SLAB) % HEAD
    cos = np.cos(ang)[:, within % half]
    sin = np.sin(ang)[:, within % half] * np.where(within < half, -1.0, 1.0)[None, :]
    perm = np.zeros((SLAB, SLAB))
    partner = np.where(within < half, np.arange(SLAB) + half, np.arange(SLAB) - half)
    perm[partner, np.arange(SLAB)] = 1.0
    i = np.arange(bl)
    diff = i[:, None] - i[None, :]
    same = (i[:, None] // CHUNK) == (i[None, :] // CHUNK)
    earlier = (i[None, :] // CHUNK) < (i[:, None] // CHUNK)
    decay = np.zeros((nh, bl, bl))
    for h in range(nh):
        decay[h] = np.where(same, np.exp(log_gamma[h] * np.abs(diff)),
                            np.where(earlier, np.exp(log_gamma[h] * diff), 0.0))
    qd = np.exp(lg_lane[None, :] * (i[:, None] + 1.0))
    kd = np.exp(lg_lane[None, :] * (bl - 1.0 - i[:, None]))
    gam = np.exp(lg_lane * bl)[:, None] * np.ones((1, SLAB))
    bd = (lane_h[:, None] == lane_h[None, :]).astype(np.float64)
    f = lambda a: jnp.asarray(a, F32)
    return f(cos), f(sin), f(perm), f(decay), f(qd), f(kd), f(gam), f(bd)


def _ret_block(q, k, v, state, cos, sin, perm, decay, qd, kd, gam, bd, hm):
    qr = (q * cos + jnp.dot(q, perm, preferred_element_type=F32) * sin) * (HEAD ** -0.5)
    kr = k * cos + jnp.dot(k, perm, preferred_element_type=F32) * sin
    y = jnp.dot(qr * qd, state, preferred_element_type=F32)
    for h in range(SLAB // HEAD):
        m = hm[h:h + 1]
        sc = lax.dot_general(qr * m, kr, NT, preferred_element_type=F32) * decay[h]
        y = y + jnp.dot(sc, v * m, preferred_element_type=F32)
    new_state = gam * state + lax.dot_general(kr * kd, v, TN, preferred_element_type=F32) * bd
    return y, new_state


def _ret_specs(s, bl, rev):
    nb = s // bl
    pos = (lambda n: nb - 1 - n) if rev else (lambda n: n)
    slab = lambda j: pl.BlockSpec((None, bl, SLAB), lambda n: (j // PER_SHARD, pos(n), j % PER_SHARD))
    const2 = lambda r: pl.BlockSpec((r, SLAB), lambda n: (0, 0))
    tab = [pl.BlockSpec((bl, SLAB), lambda n: (pos(n), 0))] * 2 + [
        const2(SLAB), pl.BlockSpec((SLAB // HEAD, bl, bl), lambda n: (0, 0, 0)),
        const2(bl), const2(bl), const2(SLAB), const2(SLAB), const2(8)]
    return nb, pos, slab, tab


def _ret_fwd(name, p32):
    s = p32.shape[1]
    bl = min(RET_BLOCK, s)
    nb, pos, slab, tab = _ret_specs(s, bl, False)
    tables = _ret_tables(s, bl) + (_head_masks(),)

    def kern(q_ref, k_ref, v_ref, *rest):
        t_refs, (y_ref, st_ref, state) = rest[:9], rest[9:]

        @pl.when(pl.program_id(0) == 0)
        def _():
            state[...] = jnp.zeros((SLAB, SLAB), F32)

        st_ref[...] = state[...]
        y, new = _ret_block(q_ref[...], k_ref[...], v_ref[...], state[...], *[t[...] for t in t_refs])
        y_ref[...] = y
        state[...] = new

    return pl.pallas_call(
        kern, name=name, grid=(nb,), in_specs=[slab(8), slab(9), slab(10)] + tab,
        out_specs=[pl.BlockSpec((bl, SLAB), lambda n: (n, 0)), pl.BlockSpec((None, SLAB, SLAB), lambda n: (n, 0, 0))],
        out_shape=[_sds((s, SLAB), F32), _sds((nb, SLAB, SLAB), F32)],
        scratch_shapes=[pltpu.VMEM((SLAB, SLAB), F32)], compiler_params=_params(1))(p32, p32, p32, *tables)


def _ret_bwd(name, p32, states, dy):
    s = p32.shape[1]
    bl = min(RET_BLOCK, s)
    nb, pos, slab, tab = _ret_specs(s, bl, True)
    tables = _ret_tables(s, bl) + (_head_masks(),)
    rowblk = pl.BlockSpec((bl, SLAB), lambda n: (pos(n), 0))

    def kern(q_ref, k_ref, v_ref, st_ref, dy_ref, *rest):
        t_refs, (dq_ref, dk_ref, dv_ref, dstate) = rest[:9], rest[9:]

        @pl.when(pl.program_id(0) == 0)
        def _():
            dstate[...] = jnp.zeros((SLAB, SLAB), F32)

        tv = [t[...] for t in t_refs]
        _, vjp = jax.vjp(lambda a, b, c, d: _ret_block(a, b, c, d, *tv),
                         q_ref[...], k_ref[...], v_ref[...], st_ref[...])
        dq, dk, dv, ds = vjp((dy_ref[...], dstate[...]))
        dq_ref[...] = dq
        dk_ref[...] = dk
        dv_ref[...] = dv
        dstate[...] = ds

    return pl.pallas_call(
        kern, name=name, grid=(nb,),
        in_specs=[slab(8), slab(9), slab(10), pl.BlockSpec((None, SLAB, SLAB), lambda n: (pos(n), 0, 0)), rowblk] + tab,
        out_specs=[rowblk] * 3, out_shape=[_sds((s, SLAB), F32)] * 3,
        scratch_shapes=[pltpu.VMEM((SLAB, SLAB), F32)], compiler_params=_params(1))(p32, p32, p32, states, dy, *tables)


TM_FFN = 1024
TM_SLAB = 2048
TM_NORM = 512
TM_RW = 256
TM_FF = 128


def _ffn_fwd(tag, x, g, w_in, w_out, l):
    tm = min(TM_FFN, x.shape[0])
    h, (u,) = _norm_proj_in(tag + "_in", x, g, w_in, l, tm, [BF16])
    a = _swiglu_fwd(tag + "_act", u, TM_FF)
    w_out2 = w_out.reshape(DEPTH, 2, FF_SHARD, D_MODEL)
    xn = _proj_out(tag + "_out", a, w_out2, l, x, 0.5, tm)
    return xn, (x, h, u, a)


def _ffn_bwd(tag, saved, dxn, g, w_in, w_out, l):
    x, h, u, a = saved
    tm = min(TM_FFN, x.shape[0])
    w_out2 = w_out.reshape(DEPTH, 2, FF_SHARD, D_MODEL)
    da = _back_out(tag + "_dact", dxn, w_out2, l, 0.5, tm, BF16)
    dw_out = _grad_out(tag + "_dwout", a, dxn, 0.5, tm)
    du = _swiglu_bwd(tag + "_dswi", u, da, TM_FF)
    dx, dg = _back_in_norm(tag + "_dh", du, w_in, l, min(TM_NORM, x.shape[0]), x, dxn, g)
    dw_in = _grad_in(tag + "_dwin", h, du, tm)
    return dx, dg, dw_in, dw_out.reshape(N_SHARD, D_FF // N_SHARD, D_MODEL)


def _mix_fwd(tag, x, sm, w_in, w_out, l):
    h, (p32, p16) = _norm_proj_in(tag + "_in", x, sm["mix_norm"][l:l + 1], w_in, l, min(TM_FFN, x.shape[0]),
                                  [F32, BF16])
    ypre = _conv_fwd(tag + "_conv", p32, sm["conv_w"][l], sm["conv_b"][l:l + 1])
    yconv = _ln_silu_fwd(tag + "_ln", ypre, sm["conv_ln_g"][l:l + 1], sm["conv_ln_b"][l:l + 1], TM_RW)
    osb = _sb_fwd(tag + "_sb", p16)
    yr, states = _ret_fwd(tag + "_ret", p32)
    yret = _ghn_fwd(tag + "_ghn", yr, p32, sm["ret_norm_g"][l:l + 1], TM_RW)
    ycat = _assemble(tag + "_cat", [(yconv, None), (osb, 0), (osb, 1), (yret, None)], 1, TM_RW)
    xn = _proj_out(tag + "_out", ycat, w_out, l, x, 1.0, min(TM_FFN, x.shape[0]))
    return xn, (x, h, p32, p16, ypre, osb, yr, states, ycat)


def _mix_bwd(tag, saved, dxn, sm, w_in, w_out, l):
    x, h, p32, p16, ypre, osb, yr, states, ycat = saved
    ts = min(TM_SLAB, x.shape[0])
    dcat = _back_out(tag + "_dcat", dxn, w_out, l, 1.0, ts, F32)
    dw_out = _grad_out(tag + "_dwout", ycat, dxn, 1.0, ts)
    dypre, dlg, dlb = _ln_silu_bwd(tag + "_dln", ypre, dcat, sm["conv_ln_g"][l:l + 1], sm["conv_ln_b"][l:l + 1], TM_RW)
    da, db, dcw, dcb = _conv_bwd(tag + "_dconv", p32, sm["conv_w"][l], dypre)
    dq, dk, dv = _sb_bwd(tag + "_dsb", p16, osb, dcat)
    dyr, dgate, drg = _ghn_bwd(tag + "_dghn", yr, p32, dcat, sm["ret_norm_g"][l:l + 1], TM_RW)
    dqr, dkr, dvr = _ret_bwd(tag + "_dret", p32, states, dyr)
    dp = _assemble(tag + "_dp", [(da, None), (db, None), (dq, 0), (dq, 1), (dk, 0), (dk, 1), (dv, 0), (dv, 1),
                                 (dqr, None), (dkr, None), (dvr, None), (dgate, None)], PER_SHARD, TM_RW)
    tm = min(TM_FFN, x.shape[0])
    dx, dg = _back_in_norm(tag + "_dh", dp, w_in, l, min(TM_NORM, x.shape[0]), x, dxn, sm["mix_norm"][l:l + 1])
    dw_in = _grad_in(tag + "_dwin", h, dp, tm)
    small = dict(mix_norm=dg, conv_w=dcw[0:CONV_W], conv_b=dcb, conv_ln_g=dlg, conv_ln_b=dlb, ret_norm_g=drg)
    return dx, small, dw_in, dw_out


def _local_step(x, tgt, wt, sm):
    saved = []
    for l in range(DEPTH):
        x, s1 = _ffn_fwd(f"l{l}f1", x, sm["ffn1_norm"][l:l + 1], wt["ffn1_w_in"], wt["ffn1_w_out"], l)
        x, s2 = _mix_fwd(f"l{l}mx", x, sm, wt["mix_w_in"], wt["mix_w_out"], l)
        x, s3 = _ffn_fwd(f"l{l}f2", x, sm["ffn2_norm"][l:l + 1], wt["ffn2_w_in"], wt["ffn2_w_out"], l)
        saved.append((s1, s2, s3))
    dx, dfinal, loss = _final("final", x, tgt, sm["final_norm"][None, :], TM_RW)
    big = [None] * DEPTH
    small = [None] * DEPTH
    for l in reversed(range(DEPTH)):
        s1, s2, s3 = saved[l]
        dx, dg3, dwi3, dwo3 = _ffn_bwd(f"l{l}f2", s3, dx, sm["ffn2_norm"][l:l + 1], wt["ffn2_w_in"], wt["ffn2_w_out"], l)
        dx, sml, dwi2, dwo2 = _mix_bwd(f"l{l}mx", s2, dx, sm, wt["mix_w_in"], wt["mix_w_out"], l)
        dx, dg1, dwi1, dwo1 = _ffn_bwd(f"l{l}f1", s1, dx, sm["ffn1_norm"][l:l + 1], wt["ffn1_w_in"], wt["ffn1_w_out"], l)
        big[l] = dict(ffn1_w_in=dwi1, ffn1_w_out=dwo1, mix_w_in=dwi2, mix_w_out=dwo2, ffn2_w_in=dwi3, ffn2_w_out=dwo3)
        sml.update(ffn1_norm=dg1, ffn2_norm=dg3)
        small[l] = sml
    return loss, dx, big, small, dfinal


MESH = pl.DeviceIdType.MESH
ANY = pl.BlockSpec(memory_space=pl.ANY)
BIG = ("ffn1_w_in", "ffn1_w_out", "mix_w_in", "mix_w_out", "ffn2_w_in", "ffn2_w_out")


def _place():
    x, y, c = lax.axis_index("x"), lax.axis_index("y"), lax.axis_index("c")
    chips = [(1 - x, y), (x, 1 - y), (1 - x, 1 - y)]
    return x, y, c, chips


def _gather_weights(w16):
    n = len(w16)

    def kern(*refs):
        dst = refs[n:2 * n]
        send, recv = refs[2 * n:]
        x, y, c, chips = _place()
        mine = 2 * x + y
        firsts, passes = [], []
        for a in range(n):
            h = dst[a].shape[2] // 2
            own = dst[a].at[:, mine, pl.ds(c * h, h)]
            for j, (cx, cy) in enumerate(chips):
                cp = pltpu.make_async_remote_copy(
                    src_ref=own, dst_ref=own, send_sem=send.at[6 * a + j], recv_sem=recv.at[6 * a + j],
                    device_id=(cx, cy, c), device_id_type=MESH)
                cp.start()
                firsts.append(cp)
        for a in range(n):
            h = dst[a].shape[2] // 2
            half = pl.ds(c * h, h)
            for j, (cx, cy) in enumerate(chips):
                theirs = dst[a].at[:, 2 * cx + cy, half]
                pltpu.make_async_remote_copy(
                    src_ref=theirs, dst_ref=theirs, send_sem=send.at[6 * a + j], recv_sem=recv.at[6 * a + j],
                    device_id=(cx, cy, c), device_id_type=MESH).wait_recv()
                fw = pltpu.make_async_remote_copy(
                    src_ref=theirs, dst_ref=theirs, send_sem=send.at[6 * a + 3 + j], recv_sem=recv.at[6 * a + 3 + j],
                    device_id=(x, y, 1 - c), device_id_type=MESH)
                fw.start()
                passes.append(fw)
        for a in range(n):
            h = dst[a].shape[2] // 2
            other = pl.ds((1 - c) * h, h)
            for j, (cx, cy) in enumerate(chips):
                got = dst[a].at[:, 2 * cx + cy, other]
                pltpu.make_async_remote_copy(
                    src_ref=got, dst_ref=got, send_sem=send.at[6 * a + 3 + j], recv_sem=recv.at[6 * a + 3 + j],
                    device_id=(x, y, 1 - c), device_id_type=MESH).wait_recv()
        for cp in firsts + passes:
            cp.wait_send()

    return pl.pallas_call(
        kern, name="gather_weights", in_specs=[ANY] * n, out_specs=[ANY] * n,
        out_shape=[_sds(w.shape, w.dtype) for w in w16], input_output_aliases={a: a for a in range(n)},
        scratch_shapes=[pltpu.SemaphoreType.DMA((6 * n,)), pltpu.SemaphoreType.DMA((6 * n,))])(*w16)


def _pair_exchange(grads):
    n = len(grads)

    def kern(*refs):
        src, got_o = refs[:n], refs[n:2 * n]
        send, recv = refs[2 * n:]
        x, y, c, _ = _place()
        cps = []
        for a in range(n):
            h = src[a].shape[1] // 2
            cp = pltpu.make_async_remote_copy(
                src_ref=src[a].at[:, pl.ds((1 - c) * h, h)], dst_ref=got_o[a],
                send_sem=send.at[a], recv_sem=recv.at[a], device_id=(x, y, 1 - c), device_id_type=MESH)
            cp.start()
            cps.append(cp)
        for cp in cps:
            cp.wait()

    halves = [_sds((g.shape[0], g.shape[1] // 2, g.shape[2]), g.dtype) for g in grads]
    return pl.pallas_call(
        kern, name="pair_exchange", in_specs=[ANY] * n, out_specs=[ANY] * n, out_shape=halves,
        scratch_shapes=[pltpu.SemaphoreType.DMA((n,)), pltpu.SemaphoreType.DMA((n,))])(*grads)


def _chip_exchange(sums):
    n = len(sums)

    def kern(*refs):
        src, dst = refs[:n], refs[n:2 * n]
        send, recv = refs[2 * n:]
        x, y, c, chips = _place()
        cps = []
        for a in range(n):
            for j, (cx, cy) in enumerate(chips):
                cp = pltpu.make_async_remote_copy(
                    src_ref=src[a].at[2 * cx + cy], dst_ref=dst[a].at[j],
                    send_sem=send.at[3 * a + j], recv_sem=recv.at[3 * a + j],
                    device_id=(cx, cy, c), device_id_type=MESH)
                cp.start()
                cps.append(cp)
        for cp in cps:
            cp.wait()

    return pl.pallas_call(
        kern, name="chip_exchange", in_specs=[ANY] * n, out_specs=[ANY] * n,
        out_shape=[_sds((3,) + s_.shape[1:], s_.dtype) for s_ in sums],
        scratch_shapes=[pltpu.SemaphoreType.DMA((3 * n,)), pltpu.SemaphoreType.DMA((3 * n,))])(*sums)


def _pair_join(full):
    n = len(full)

    def kern(*refs):
        dst = refs[n:2 * n]
        send, recv = refs[2 * n:]
        x, y, c, _ = _place()
        cps = []
        for a in range(n):
            h = dst[a].shape[1] // 2
            mine = dst[a].at[:, pl.ds(c * h, h)]
            cp = pltpu.make_async_remote_copy(
                src_ref=mine, dst_ref=mine, send_sem=send.at[a], recv_sem=recv.at[a],
                device_id=(x, y, 1 - c), device_id_type=MESH)
            cp.start()
            cps.append(cp)
        for a, cp in enumerate(cps):
            cp.wait_send()
            h = dst[a].shape[1] // 2
            got = dst[a].at[:, pl.ds((1 - c) * h, h)]
            pltpu.make_async_remote_copy(
                src_ref=got, dst_ref=got, send_sem=send.at[a], recv_sem=recv.at[a],
                device_id=(x, y, 1 - c), device_id_type=MESH).wait_recv()

    return pl.pallas_call(
        kern, name="pair_join", in_specs=[ANY] * n, out_specs=[ANY] * n,
        out_shape=[_sds(f.shape, f.dtype) for f in full], input_output_aliases={a: a for a in range(n)},
        scratch_shapes=[pltpu.SemaphoreType.DMA((n,)), pltpu.SemaphoreType.DMA((n,))])(*full)


def _all_sum(name, v):
    r = v.shape[0]

    def kern(v_ref, o_ref, buf, send, recv):
        x, y, c, _ = _place()
        me = 4 * x + 2 * y + c
        buf[me] = v_ref[...]
        cps = []
        for k in range(1, 8):
            peer = (x ^ (k >> 2), y ^ ((k >> 1) & 1), c ^ (k & 1))
            cp = pltpu.make_async_remote_copy(
                src_ref=v_ref, dst_ref=buf.at[me], send_sem=send.at[k - 1], recv_sem=recv.at[k - 1],
                device_id=peer, device_id_type=MESH)
            cp.start()
            cps.append(cp)
        for k in range(1, 8):
            peer_id = me ^ k
            pltpu.make_async_remote_copy(
                src_ref=v_ref, dst_ref=buf.at[peer_id], send_sem=send.at[k - 1], recv_sem=recv.at[k - 1],
                device_id=(x, y, c), device_id_type=MESH).wait_recv()
        for cp in cps:
            cp.wait_send()
        acc = buf[0]
        for d in range(1, 8):
            acc = acc + buf[d]
        o_ref[...] = acc

    vm = pl.BlockSpec(memory_space=pltpu.VMEM)
    return pl.pallas_call(
        kern, name=name, in_specs=[vm], out_specs=vm, out_shape=_sds((r, 128), F32),
        scratch_shapes=[pltpu.VMEM((8, r, 128), F32), pltpu.SemaphoreType.DMA((7,)),
                        pltpu.SemaphoreType.DMA((7,))])(v)


def _my_chip():
    return 2 * lax.axis_index("x") + lax.axis_index("y")


def _my_core():
    return lax.axis_index("c")


def _cast_place(name, w):
    l, r, c = w.shape
    tr = 64
    return _rw(name, lambda wb: ((wb,), ()), (l, r // tr), [w],
               [pl.BlockSpec((None, tr, c), lambda j, i: (j, i, 0))],
               [_sds((l, N_SHARD, r, c), BF16)],
               [pl.BlockSpec((None, None, tr, c), lambda j, i: (j, _my_chip(), i, 0))])[0]


HALF_TILE = 32


def _add_halves(name, g, got):
    n, h, c = got.shape
    tr, nt = HALF_TILE, h // HALF_TILE
    return _rw(name, lambda ab, bb: ((ab.astype(F32) + bb.astype(F32),), ()), (nt,), [g, got],
               [pl.BlockSpec((n, tr, c), lambda i: (0, _my_core() * nt + i, 0)),
                pl.BlockSpec((n, tr, c), lambda i: (0, i, 0))],
               [_sds((n, h, c), BF16)], [pl.BlockSpec((n, tr, c), lambda i: (0, i, 0))])[0]


def _sum_parts(name, sums, parts, full, layer, n_layer):
    _, h, c = sums.shape
    tr, nt = HALF_TILE, h // HALF_TILE

    def body(own, pb):
        acc = own.astype(F32)
        for j in range(pb.shape[0]):
            acc = acc + pb[j].astype(F32)
        return (acc,), ()

    ins = [sums, parts] + ([full] if full is not None else [])
    in_specs = [pl.BlockSpec((None, tr, c), lambda i: (_my_chip(), i, 0)),
                pl.BlockSpec((parts.shape[0], tr, c), lambda i: (0, i, 0))] + ([ANY] if full is not None else [])
    return _rw(name, body, (nt,), ins, in_specs, [_sds((n_layer, 2 * h, c), F32)],
               [pl.BlockSpec((None, tr, c), lambda i: (layer, _my_core() * nt + i, 0))],
               aliases={2: 0} if full is not None else None)[0]


def _adamw_math(w, g, m, v):
    m = B1 * m + (1.0 - B1) * g
    v = B2 * v + (1.0 - B2) * (g * g)
    m_hat = m / (1.0 - B1 ** STEP)
    v_hat = v / (1.0 - B2 ** STEP)
    delta = -LR * (m_hat / (jnp.sqrt(v_hat) + ADAM_EPS) + WD * w)
    return delta, m, v


def _adamw(name, w, g, m, v):
    r, c = w.shape
    tr = 64 if r % 64 == 0 else 8
    spec = _row_spec(tr, c)
    return _rw(name, lambda *b: (_adamw_math(*b), ()), (r // tr,), [w, g, m, v], [spec] * 4,
               [_sds((r, c), F32)] * 3, [spec] * 3)


SMALL = (("ffn1_norm", (DEPTH, D_MODEL)), ("mix_norm", (DEPTH, D_MODEL)), ("ffn2_norm", (DEPTH, D_MODEL)),
         ("conv_b", (DEPTH, SLAB)), ("conv_ln_g", (DEPTH, SLAB)), ("conv_ln_b", (DEPTH, SLAB)),
         ("ret_norm_g", (DEPTH, SLAB)), ("final_norm", (D_MODEL,)), ("conv_w", (DEPTH, CONV_W, SLAB)))


def _pack(parts, rows):
    flat = jnp.concatenate([p.reshape(-1) for p in parts])
    return jnp.pad(flat, (0, rows * 128 - flat.shape[0])).reshape(rows, 128)


def _unpack(packed, shapes):
    flat = packed.reshape(-1)
    out, off = [], 0
    for shp in shapes:
        n = int(np.prod(shp))
        out.append(flat[off:off + n].reshape(shp))
        off += n
    return out


def kernel(x, ffn1_norm, ffn1_w_in, ffn1_w_out, mix_norm, mix_w_in, conv_w, conv_b, conv_ln_g, conv_ln_b, ret_norm_g, mix_w_out, ffn2_norm, ffn2_w_in, ffn2_w_out, final_norm, loss_target, m_ffn1_norm, m_ffn1_w_in, m_ffn1_w_out, m_mix_norm, m_mix_w_in, m_conv_w, m_conv_b, m_conv_ln_g, m_conv_ln_b, m_ret_norm_g, m_mix_w_out, m_ffn2_norm, m_ffn2_w_in, m_ffn2_w_out, m_final_norm, v_ffn1_norm, v_ffn1_w_in, v_ffn1_w_out, v_mix_norm, v_mix_w_in, v_conv_w, v_conv_b, v_conv_ln_g, v_conv_ln_b, v_ret_norm_g, v_mix_w_out, v_ffn2_norm, v_ffn2_w_in, v_ffn2_w_out, v_final_norm):
    given = dict(locals())
    names = [n for n, _ in SMALL] + list(BIG)
    chip = 2 * lax.axis_index("x") + lax.axis_index("y")
    core = lax.axis_index("c")

    cw_rows = 128
    placed = lax.dynamic_update_slice(jnp.zeros((DEPTH, CONV_W, SLAB), F32), conv_w, (0, 0, chip * HEAD))
    placed = placed * (core == 0).astype(F32)
    conv_w_full = _unpack(_all_sum("gather_conv_w", _pack([placed], cw_rows)), [(DEPTH, CONV_W, SLAB)])[0]

    wt = dict(zip(BIG, _gather_weights([_cast_place("cast_" + n, given[n]) for n in BIG])))
    sm = {n: given[n] for n, _ in SMALL}
    sm["conv_w"] = conv_w_full
    loss, dx, big, small, dfinal = _local_step(x[0], loss_target[0], wt, sm)

    grads = [big[l][n] for n in BIG for l in range(DEPTH)]
    theirs = _pair_exchange(grads)
    sums = [_add_halves(f"chipsum{i}", a, b) for i, (a, b) in enumerate(zip(grads, theirs))]
    parts = _chip_exchange(sums)
    full = []
    for i in range(len(BIG)):
        f = None
        for l in range(DEPTH):
            f = _sum_parts(f"shardsum{DEPTH * i + l}", sums[DEPTH * i + l], parts[DEPTH * i + l], f, l, DEPTH)
        full.append(f)
    g_big = dict(zip(BIG, _pair_join(full)))

    small_parts = []
    for n, shp in SMALL:
        if n == "final_norm":
            small_parts.append(dfinal)
        else:
            small_parts.append(jnp.stack([small[l][n].reshape(shp[1:]) for l in range(DEPTH)]))
    g_small = dict(zip([n for n, _ in SMALL], _unpack(_all_sum("sum_small", _pack(small_parts, 200)), [s_ for _, s_ in SMALL])))
    g_small["conv_w"] = lax.dynamic_slice(g_small["conv_w"], (0, 0, chip * HEAD), (DEPTH, CONV_W, HEAD))

    grad, delta, new_m, new_v = dict(g_small), {}, {}, {}
    grad.update(g_big)
    for n in BIG:
        l, r, c = given[n].shape
        f = lambda t: t.reshape(l * r, c)
        d_, m_, v_ = _adamw("adamw_" + n, f(given[n]), f(grad[n]), f(given["m_" + n]), f(given["v_" + n]))
        delta[n], new_m[n], new_v[n] = d_.reshape(l, r, c), m_.reshape(l, r, c), v_.reshape(l, r, c)
    snames = [n for n, _ in SMALL]
    shapes = [given[n].shape for n in snames]
    rows = 104
    d_, m_, v_ = _adamw("adamw_small", _pack([given[n] for n in snames], rows), _pack([grad[n] for n in snames], rows),
                        _pack([given["m_" + n] for n in snames], rows), _pack([given["v_" + n] for n in snames], rows))
    for dst, packed in ((delta, d_), (new_m, m_), (new_v, v_)):
        dst.update(zip(snames, _unpack(packed, shapes)))

    total = lax.psum(loss[0, 0], ("x", "y", "c"))
    order = ["ffn1_norm", "ffn1_w_in", "ffn1_w_out", "mix_norm", "mix_w_in", "conv_w", "conv_b", "conv_ln_g",
             "conv_ln_b", "ret_norm_g", "mix_w_out", "ffn2_norm", "ffn2_w_in", "ffn2_w_out", "final_norm"]
    return (total, dx[None], *[grad[n] for n in order], *[delta[n] for n in order],
            *[new_m[n] for n in order], *[new_v[n] for n in order])
```

```python
import functools

import numpy as np
import jax
import jax.numpy as jnp
from jax import lax
from jax.experimental import pallas as pl
from jax.experimental.pallas import tpu as pltpu

F32 = jnp.float32
BF16 = jnp.bfloat16

D_MODEL = 1024
D_FF = 2816
N_SHARD = 4
FF_SHARD = 2 * D_FF // N_SHARD
MIX_SHARD = 3072 // N_SHARD
HEAD = 64
SLAB = 256
N_SLAB = 3072 // SLAB
CONV_W = 31
CONV_PAD = 32
CHUNK = 64
EPS = 1e-6
ROPE_BASE = 10000.0
DEPTH = 2

LR, B1, B2, ADAM_EPS, WD, STEP = 0.001, 0.9, 0.999, 1e-08, 0.01, 10

VMEM_LIMIT = 56 * 1024 * 1024


def _params(n_grid):
    return pltpu.CompilerParams(dimension_semantics=("arbitrary",) * n_grid, vmem_limit_bytes=VMEM_LIMIT)


def _rw(name, body, grid, ins, in_specs, rows=(), row_specs=(), accs=(), acc_specs=(), aliases=None):
    n_in, n_row = len(ins), len(rows)
    carried = sorted(aliases) if aliases else []

    def kern(*refs):
        vals = [r[...] for i, r in enumerate(refs[:n_in]) if i not in carried]
        row_vals, acc_vals = body(*vals)
        for r, v in zip(refs[n_in:n_in + n_row], row_vals):
            r[...] = v.astype(r.dtype)
        acc_refs = refs[n_in + n_row:]
        if acc_refs:
            first = functools.reduce(jnp.logical_and, [pl.program_id(a) == 0 for a in range(len(grid))])

            @pl.when(first)
            def _():
                for r in acc_refs:
                    r[...] = jnp.zeros(r.shape, r.dtype)

            for r, v in zip(acc_refs, acc_vals):
                r[...] += v.astype(r.dtype)

    return pl.pallas_call(
        kern, name=name, grid=grid, in_specs=list(in_specs), out_specs=list(row_specs) + list(acc_specs),
        out_shape=list(rows) + list(accs), input_output_aliases=dict(aliases or {}),
        compiler_params=_params(len(grid)))(*ins)


def _sds(shape, dtype):
    return jax.ShapeDtypeStruct(shape, dtype)


def _rms(x, g):
    return x * lax.rsqrt(jnp.mean(x * x, axis=-1, keepdims=True) + EPS) * g


def _row_spec(tm, c):
    return pl.BlockSpec((tm, c), lambda i: (i, 0))


def _vec_spec(c):
    return pl.BlockSpec((1, c), lambda i: (0, 0))


def _swiglu(gate, up):
    return jax.nn.silu(gate) * up


def _swiglu_fwd(name, u, tm):
    _, s, c = u.shape
    return _rw(name, lambda ub: ((_swiglu(ub[0:2].astype(F32), ub[2:4].astype(F32)),), ()), (s // tm,), [u],
               [pl.BlockSpec((4, tm, c), lambda i: (0, i, 0))],
               [_sds((2, s, c), BF16)], [pl.BlockSpec((2, tm, c), lambda i: (0, i, 0))])[0]


def _swiglu_bwd(name, u, da, tm):
    _, s, c = u.shape

    def body(ub, dab):
        _, vjp = jax.vjp(_swiglu, ub[0:2].astype(F32), ub[2:4].astype(F32))
        dg, du = vjp(dab.astype(F32))
        return (jnp.concatenate([dg, du], axis=0),), ()

    return _rw(name, body, (s // tm,), [u, da],
               [pl.BlockSpec((4, tm, c), lambda i: (0, i, 0)), pl.BlockSpec((2, tm, c), lambda i: (0, i, 0))],
               [_sds((4, s, c), BF16)], [pl.BlockSpec((4, tm, c), lambda i: (0, i, 0))])[0]


def _ln_silu(y, g, b):
    mu = jnp.mean(y, axis=-1, keepdims=True)
    yc = y - mu
    var = jnp.mean(yc * yc, axis=-1, keepdims=True)
    return jax.nn.silu(yc * lax.rsqrt(var + EPS) * g + b)


def _ln_silu_fwd(name, y, g, b, tm):
    s, c = y.shape
    return _rw(name, lambda yb, gb, bb: ((_ln_silu(yb, gb, bb),), ()), (s // tm,), [y, g, b],
               [_row_spec(tm, c), _vec_spec(c), _vec_spec(c)], [_sds((s, c), BF16)], [_row_spec(tm, c)])[0]


def _ln_silu_bwd(name, y, dcat, g, b, tm):
    s, c = y.shape

    def body(yb, dob, gb, bb):
        _, vjp = jax.vjp(_ln_silu, yb, gb, bb)
        dy, dg, db = vjp(dob)
        return (dy,), (dg, db)

    return _rw(name, body, (s // tm,), [y, dcat, g, b],
               [_row_spec(tm, c), pl.BlockSpec((None, tm, c), lambda i: (0, i, 0)), _vec_spec(c), _vec_spec(c)],
               [_sds((s, c), F32)], [_row_spec(tm, c)],
               [_sds((1, c), F32)] * 2, [_vec_spec(c)] * 2)


def _head_masks():
    lane = np.arange(SLAB) // HEAD
    m = np.zeros((8, SLAB), np.float32)
    for h in range(SLAB // HEAD):
        m[h] = (lane == h)
    return jnp.asarray(m)


def _gated_head_norm(y, gate, g, hm):
    mu = jnp.zeros_like(y)
    for h in range(SLAB // HEAD):
        mu = mu + hm[h:h + 1] * (jnp.sum(y * hm[h:h + 1], axis=-1, keepdims=True) / HEAD)
    yc = y - mu
    var = jnp.zeros_like(y)
    for h in range(SLAB // HEAD):
        var = var + hm[h:h + 1] * (jnp.sum(yc * yc * hm[h:h + 1], axis=-1, keepdims=True) / HEAD)
    return jax.nn.silu(gate) * (yc * lax.rsqrt(var + EPS) * g)


PER_SHARD = MIX_SHARD // SLAB


def _slab_spec(tm, j):
    return pl.BlockSpec((None, tm, SLAB), lambda i: (j, i, 0))


def _proj_slab_spec(tm, j):
    return pl.BlockSpec((None, tm, SLAB), lambda i: (j // PER_SHARD, i, j % PER_SHARD))


def _ghn_fwd(name, y, p32, g, tm):
    s, c = y.shape
    hm = _head_masks()
    return _rw(name, lambda yb, gb, wb, hb: ((_gated_head_norm(yb, gb, wb, hb),), ()), (s // tm,),
               [y, p32, g, hm],
               [_row_spec(tm, c), _proj_slab_spec(tm, 11), _vec_spec(c), pl.BlockSpec((8, c), lambda i: (0, 0))],
               [_sds((s, c), BF16)], [_row_spec(tm, c)])[0]


def _ghn_bwd(name, y, p32, dcat, g, tm):
    s, c = y.shape
    hm = _head_masks()

    def body(yb, gb, dob, wb, hb):
        _, vjp = jax.vjp(lambda a, b_, c_: _gated_head_norm(a, b_, c_, hb), yb, gb, wb)
        dy, dgate, dw = vjp(dob)
        return (dy, dgate), (dw,)

    return _rw(name, body, (s // tm,), [y, p32, dcat, g, hm],
               [_row_spec(tm, c), _proj_slab_spec(tm, 11), _slab_spec(tm, 3), _vec_spec(c),
                pl.BlockSpec((8, c), lambda i: (0, 0))],
               [_sds((s, c), F32)] * 2, [_row_spec(tm, c)] * 2,
               [_sds((1, c), F32)], [_vec_spec(c)])


def _assemble(name, parts, per, tm):
    s = parts[0][0].shape[-2]
    specs = [_row_spec(tm, SLAB) if j is None else pl.BlockSpec((None, tm, SLAB), lambda i, j=j: (j, i, 0))
             for _, j in parts]

    def body(*blocks):
        rows = [jnp.concatenate([b.astype(BF16) for b in blocks[per * q:per * (q + 1)]], axis=-1)
                for q in range(len(blocks) // per)]
        return (jnp.stack(rows),), ()

    nq = len(parts) // per
    return _rw(name, body, (s // tm,), [a for a, _ in parts], specs, [_sds((nq, s, per * SLAB), BF16)],
               [pl.BlockSpec((nq, tm, per * SLAB), lambda i: (0, i, 0))])[0]


def _final(name, x, tgt, g, tm):
    s, d = x.shape

    def body(xb, tb, gb):
        yf, vjp = jax.vjp(_rms, xb, gb)
        err = yf - tb
        dx, dg = vjp(err * (1.0 / d))
        part = 0.5 * jnp.sum(jnp.mean(err * err, axis=-1, keepdims=True), axis=0, keepdims=True)
        return (dx,), (dg, jnp.broadcast_to(part, (1, 128)))

    return _rw(name, body, (s // tm,), [x, tgt, g],
               [_row_spec(tm, d), _row_spec(tm, d), _vec_spec(d)],
               [_sds((s, d), F32)], [_row_spec(tm, d)],
               [_sds((1, d), F32), _sds((1, 128), F32)], [_vec_spec(d), _vec_spec(128)])


NN = (((1,), (0,)), ((), ()))
NT = (((1,), (1,)), ((), ()))
TN = (((0,), (0,)), ((), ()))


def _mm(name, a, b, grid, a_spec, b_spec, outs, out_specs, acc_shape, dims, alpha=1.0, res=None, res_spec=None):
    nk = grid[-1]
    n_out = len(outs)

    def kern(*refs):
        a_ref, b_ref = refs[0], refs[1]
        res_ref = refs[2] if res is not None else None
        first_out = 3 if res is not None else 2
        o_refs = refs[first_out:first_out + n_out]
        part = lax.dot_general(a_ref[...].astype(BF16), b_ref[...].astype(BF16), dims,
                               preferred_element_type=F32)

        def finish(r):
            if alpha != 1.0:
                r = r * alpha
            if res_ref is not None:
                r = r + res_ref[...]
            for o in o_refs:
                o[...] = r.astype(o.dtype)

        if nk == 1:
            finish(part)
            return
        acc_ref = refs[-1]
        k = pl.program_id(len(grid) - 1)

        @pl.when(k == 0)
        def _():
            acc_ref[...] = part

        @pl.when(jnp.logical_and(k > 0, k < nk - 1))
        def _():
            acc_ref[...] += part

        @pl.when(k == nk - 1)
        def _():
            finish(acc_ref[...] + part)

    ins = [a, b] + ([res] if res is not None else [])
    in_specs = [a_spec, b_spec] + ([res_spec] if res is not None else [])
    return pl.pallas_call(
        kern, name=name, grid=grid, in_specs=in_specs, out_specs=list(out_specs), out_shape=list(outs),
        scratch_shapes=[pltpu.VMEM(acc_shape, F32)] if nk > 1 else [],
        compiler_params=_params(len(grid)))(*ins)


def _norm_proj_in(name, x, g, w, l, tm, dtypes):
    s, d = x.shape
    n = w.shape[-1]
    n_out = len(dtypes)

    def kern(x_ref, g_ref, w_ref, h_ref, *rest):
        o_refs, h_vmem = rest[:n_out], rest[n_out]

        @pl.when(pl.program_id(1) == 0)
        def _():
            h = _rms(x_ref[...], g_ref[...]).astype(BF16)
            h_vmem[...] = h
            h_ref[...] = h

        r = jnp.dot(h_vmem[...], w_ref[...], preferred_element_type=F32)
        for o in o_refs:
            o[...] = r.astype(o.dtype)

    out = pl.pallas_call(
        kern, name=name, grid=(s // tm, N_SHARD),
        in_specs=[pl.BlockSpec((tm, d), lambda i, b: (i, 0)), pl.BlockSpec((1, d), lambda i, b: (0, 0)),
                  pl.BlockSpec((None, None, d, n), lambda i, b: (l, b, 0, 0))],
        out_specs=[pl.BlockSpec((tm, d), lambda i, b: (i, 0))] +
                  [pl.BlockSpec((None, tm, n), lambda i, b: (b, i, 0))] * n_out,
        out_shape=[_sds((s, d), BF16)] + [_sds((N_SHARD, s, n), t) for t in dtypes],
        scratch_shapes=[pltpu.VMEM((tm, d), BF16)], compiler_params=_params(2))(x, g, w)
    return out[0], out[1:]


def _back_in_norm(name, du, w, l, tm, x, dres, g):
    nk, s, n = du.shape
    d = w.shape[2]

    def kern(du_ref, w_ref, x_ref, dres_ref, g_ref, dx_ref, dg_ref, acc_ref):
        i, k = pl.program_id(0), pl.program_id(1)
        part = lax.dot_general(du_ref[...], w_ref[...], NT, preferred_element_type=F32)

        @pl.when(k == 0)
        def _():
            acc_ref[...] = part

        @pl.when(jnp.logical_and(k > 0, k < nk - 1))
        def _():
            acc_ref[...] += part

        @pl.when(k == nk - 1)
        def _():
            _, vjp = jax.vjp(_rms, x_ref[...], g_ref[...])
            dx, dg = vjp(acc_ref[...] + part)
            dx_ref[...] = dx + dres_ref[...]

            @pl.when(i == 0)
            def _():
                dg_ref[...] = dg

            @pl.when(i > 0)
            def _():
                dg_ref[...] += dg

    row = pl.BlockSpec((tm, d), lambda i, k: (i, 0))
    vec = pl.BlockSpec((1, d), lambda i, k: (0, 0))
    return pl.pallas_call(
        kern, name=name, grid=(s // tm, nk),
        in_specs=[pl.BlockSpec((None, tm, n), lambda i, k: (k, i, 0)),
                  pl.BlockSpec((None, None, d, n), lambda i, k: (l, k, 0, 0)), row, row, vec],
        out_specs=[row, vec], out_shape=[_sds((s, d), F32), _sds((1, d), F32)],
        scratch_shapes=[pltpu.VMEM((tm, d), F32)], compiler_params=_params(2))(du, w, x, dres, g)


def _proj_out(name, a, w, l, res, alpha, tm):
    nk, s, r = a.shape
    d = w.shape[-1]
    return _mm(name, a, w, (s // tm, nk),
               pl.BlockSpec((None, tm, r), lambda i, k: (k, i, 0)),
               pl.BlockSpec((None, None, r, d), lambda i, k: (l, k, 0, 0)),
               [_sds((s, d), F32)], [pl.BlockSpec((tm, d), lambda i, k: (i, 0))],
               (tm, d), NN, alpha=alpha, res=res, res_spec=pl.BlockSpec((tm, d), lambda i, k: (i, 0)))[0]


def _back_out(name, dy, w, l, alpha, tm, out_dtype):
    s, d = dy.shape
    nk, r = w.shape[1], w.shape[2]
    return _mm(name, dy, w, (nk, s // tm, 1),
               pl.BlockSpec((tm, d), lambda b, i, k: (i, 0)),
               pl.BlockSpec((None, None, r, d), lambda b, i, k: (l, b, 0, 0)),
               [_sds((nk, s, r), out_dtype)], [pl.BlockSpec((None, tm, r), lambda b, i, k: (b, i, 0))],
               (tm, r), NT, alpha=alpha)[0]


def _grad_in(name, h, du, ts):
    s, d = h.shape
    nb, _, n = du.shape
    return _mm(name, h, du, (nb, 1, s // ts),
               pl.BlockSpec((ts, d), lambda b, i, k: (k, 0)),
               pl.BlockSpec((None, ts, n), lambda b, i, k: (b, k, 0)),
               [_sds((nb, d, n), BF16)], [pl.BlockSpec((None, d, n), lambda b, i, k: (b, 0, 0))],
               (d, n), TN)[0]


def _grad_out(name, a, dy, alpha, ts):
    nb, s, r = a.shape
    d = dy.shape[1]
    return _mm(name, a, dy, (nb, 1, s // ts),
               pl.BlockSpec((None, ts, r), lambda b, i, k: (b, k, 0)),
               pl.BlockSpec((ts, d), lambda b, i, k: (k, 0)),
               [_sds((nb, r, d), BF16)], [pl.BlockSpec((None, r, d), lambda b, i, k: (b, 0, 0))],
               (r, d), TN, alpha=alpha)[0]


CONV_TILE = 256


def _shifted(win, off, rows):
    n = win.shape[0]
    return pltpu.roll(win, (n - off) % n, 0)[0:rows] if off % n else win[0:rows]


def _conv_fwd(name, p32, w, bias):
    s = p32.shape[1]
    cb = 128
    nt = s // CONV_TILE

    def kern(a_ref, b_ref, w_ref, bias_ref, y_ref, vpad):
        vpad[0:CONV_PAD, :] = jnp.zeros((CONV_PAD, cb), F32)

        def fill(i, c):
            r = pl.multiple_of(i * CONV_TILE, CONV_TILE)
            vpad[pl.ds(CONV_PAD + r, CONV_TILE), :] = (
                a_ref[pl.ds(r, CONV_TILE), :] * jax.nn.sigmoid(b_ref[pl.ds(r, CONV_TILE), :]))
            return c

        lax.fori_loop(0, nt, fill, 0)

        def tile(i, c):
            r = pl.multiple_of(i * CONV_TILE, CONV_TILE)
            win = vpad[pl.ds(r, CONV_TILE + CONV_PAD), :]
            acc = jnp.broadcast_to(bias_ref[...], (CONV_TILE, cb))
            for j in range(CONV_W):
                acc = acc + w_ref[j:j + 1, :] * _shifted(win, j + 2, CONV_TILE)
            y_ref[pl.ds(r, CONV_TILE), :] = acc
            return c

        lax.fori_loop(0, nt, tile, 0)

    return pl.pallas_call(
        kern, name=name, grid=(SLAB // cb,),
        in_specs=[pl.BlockSpec((None, s, cb), lambda c: (0, 0, c)),
                  pl.BlockSpec((None, s, cb), lambda c: (0, 0, SLAB // cb + c)),
                  pl.BlockSpec((CONV_W, cb), lambda c: (0, c)),
                  pl.BlockSpec((1, cb), lambda c: (0, c))],
        out_specs=pl.BlockSpec((s, cb), lambda c: (0, c)),
        out_shape=_sds((s, SLAB), F32),
        scratch_shapes=[pltpu.VMEM((s + CONV_PAD, cb), F32)],
        compiler_params=_params(1))(p32, p32, w, bias)


def _conv_bwd(name, p32, w, dy):
    s = p32.shape[1]
    cb = 128
    nt = s // CONV_TILE

    def kern(a_ref, b_ref, w_ref, dy_ref, da_ref, db_ref, dw_ref, dbias_ref, vpad, dpad):
        vpad[0:CONV_PAD, :] = jnp.zeros((CONV_PAD, cb), F32)
        dpad[s:s + CONV_PAD, :] = jnp.zeros((CONV_PAD, cb), F32)
        dw_ref[...] = jnp.zeros((CONV_PAD, cb), F32)
        dbias_ref[...] = jnp.zeros((1, cb), F32)

        def fill(i, c):
            r = pl.multiple_of(i * CONV_TILE, CONV_TILE)
            vpad[pl.ds(CONV_PAD + r, CONV_TILE), :] = (
                a_ref[pl.ds(r, CONV_TILE), :] * jax.nn.sigmoid(b_ref[pl.ds(r, CONV_TILE), :]))
            dpad[pl.ds(r, CONV_TILE), :] = dy_ref[pl.ds(r, CONV_TILE), :]
            return c

        lax.fori_loop(0, nt, fill, 0)

        def tile(i, c):
            r = pl.multiple_of(i * CONV_TILE, CONV_TILE)
            dwin = dpad[pl.ds(r, CONV_TILE + CONV_PAD), :]
            vwin = vpad[pl.ds(r, CONV_TILE + CONV_PAD), :]
            dyt = dwin[0:CONV_TILE]
            dv = jnp.zeros((CONV_TILE, cb), F32)
            for j in range(CONV_W):
                dv = dv + w_ref[j:j + 1, :] * _shifted(dwin, CONV_W - 1 - j, CONV_TILE)
                dw_ref[j:j + 1, :] += jnp.sum(dyt * _shifted(vwin, j + 2, CONV_TILE), axis=0, keepdims=True)
            dbias_ref[...] += jnp.sum(dyt, axis=0, keepdims=True)
            a = a_ref[pl.ds(r, CONV_TILE), :]
            sg = jax.nn.sigmoid(b_ref[pl.ds(r, CONV_TILE), :])
            da_ref[pl.ds(r, CONV_TILE), :] = dv * sg
            db_ref[pl.ds(r, CONV_TILE), :] = dv * a * sg * (1.0 - sg)
            return c

        lax.fori_loop(0, nt, tile, 0)

    col = pl.BlockSpec((s, cb), lambda c: (0, c))
    return pl.pallas_call(
        kern, name=name, grid=(SLAB // cb,),
        in_specs=[pl.BlockSpec((None, s, cb), lambda c: (0, 0, c)),
                  pl.BlockSpec((None, s, cb), lambda c: (0, 0, SLAB // cb + c)),
                  pl.BlockSpec((CONV_W, cb), lambda c: (0, c)), col],
        out_specs=[col, col, pl.BlockSpec((CONV_PAD, cb), lambda c: (0, c)), pl.BlockSpec((1, cb), lambda c: (0, c))],
        out_shape=[_sds((s, SLAB), F32), _sds((s, SLAB), F32), _sds((CONV_PAD, SLAB), F32), _sds((1, SLAB), F32)],
        scratch_shapes=[pltpu.VMEM((s + CONV_PAD, cb), F32), pltpu.VMEM((s + CONV_PAD, cb), F32)],
        compiler_params=_params(1))(p32, p32, w, dy)


SB_BLOCK = 256
N_HEAD = SLAB // HEAD


def _split_dot(x, m):
    hi = x.astype(BF16)
    lo = (x - hi.astype(F32)).astype(BF16)
    return (jnp.dot(hi, m, preferred_element_type=F32) + jnp.dot(lo, m, preferred_element_type=F32))


def _sb_logits(qm, k, tri):
    z = lax.dot_general(qm, k, NT, preferred_element_type=F32)
    lb = jnp.minimum(z, 0.0) - jnp.log(1.0 + jnp.exp(-jnp.abs(z)))
    ln = lb - z
    if tri is not None:
        ln = jnp.where(tri, ln, 0.0)
    return lb, ln


def _first_col(x):
    return jnp.broadcast_to(x[:, 0:1], (x.shape[0], 128))


def _head_stack(dst, x, lane_head, bq):
    for h in range(N_HEAD):
        dst[h * bq:(h + 1) * bq, :] = jnp.where(lane_head == h, x, jnp.zeros_like(x))


def _sb_fwd(name, p16):
    s = p16.shape[1]
    bq = min(SB_BLOCK, s)
    nq = s // bq

    def kern(q_ref, k_ref, v_ref, o_ref, w_hbm, lb_hbm, qm_ref, v4_ref, w4_ref, ws_ref, lbs_ref, acc_ref, r_ref, sem):
        g, qi = pl.program_id(0), pl.program_id(1)
        lane_head = lax.broadcasted_iota(jnp.int32, (1, SLAB), 1) // HEAD
        _head_stack(qm_ref, (q_ref[...].astype(F32) * (HEAD ** -0.5)).astype(BF16), lane_head, bq)
        row = lax.broadcasted_iota(jnp.int32, (bq, bq), 0)
        col = lax.broadcasted_iota(jnp.int32, (bq, bq), 1)
        after = (row > col).astype(BF16)
        tri = col < row
        acc_ref[...] = jnp.zeros((bq, SLAB), F32)
        r_ref[...] = jnp.zeros((N_HEAD, bq, 128), F32)

        def saves(slot, kb):
            return (pltpu.make_async_copy(ws_ref.at[slot], w_hbm.at[g, qi, kb], sem.at[0, slot]),
                    pltpu.make_async_copy(lbs_ref.at[slot], lb_hbm.at[g, qi, kb], sem.at[1, slot]))

        def tile(i, masked):
            kb, slot = qi - i, i % 2
            rows = pl.ds(pl.multiple_of(kb * bq, bq), bq)
            k = k_ref[rows, :]
            _head_stack(v4_ref, v_ref[rows, :], lane_head, bq)
            for h in range(N_HEAD):
                mine = pl.ds(h * bq, bq)
                lb, ln = _sb_logits(qm_ref[h * bq:(h + 1) * bq, :], k, tri if masked else None)
                rem = jnp.dot(ln.astype(BF16), after, preferred_element_type=F32)
                w = jnp.exp(lb + rem + r_ref[h][:, 0:1])
                if masked:
                    w = jnp.where(tri, w, 0.0)
                wb = w.astype(BF16)
                w4_ref[:, h * bq:(h + 1) * bq] = wb
                ws_ref[slot, mine, :] = wb
                lbs_ref[slot, mine, :] = lb.astype(BF16)
                r_ref[h] += _first_col(rem + ln)
            acc_ref[...] += jnp.dot(w4_ref[...], v4_ref[...], preferred_element_type=F32)
            if not masked:
                for cp in saves(1 - slot, kb):
                    cp.wait()
            for cp in saves(slot, kb):
                cp.start()

        tile(0, True)

        def step(i, c):
            tile(i, False)
            return c

        lax.fori_loop(1, qi + 1, step, 0)
        o_ref[...] = acc_ref[...]
        for cp in saves(qi % 2, 0):
            cp.wait()

    saved = _sds((2, nq, nq, N_HEAD * bq, bq), BF16)
    return pl.pallas_call(
        kern, name=name, grid=(2, nq),
        in_specs=[pl.BlockSpec((None, bq, SLAB), lambda g, i: ((2 + g) // PER_SHARD, i, (2 + g) % PER_SHARD)),
                  pl.BlockSpec((None, s, SLAB), lambda g, i: ((4 + g) // PER_SHARD, 0, (4 + g) % PER_SHARD)),
                  pl.BlockSpec((None, s, SLAB), lambda g, i: ((6 + g) // PER_SHARD, 0, (6 + g) % PER_SHARD))],
        out_specs=[pl.BlockSpec((None, bq, SLAB), lambda g, i: (g, i, 0)),
                   pl.BlockSpec(memory_space=pl.ANY), pl.BlockSpec(memory_space=pl.ANY)],
        out_shape=[_sds((2, s, SLAB), F32), saved, saved],
        scratch_shapes=[pltpu.VMEM((N_HEAD * bq, SLAB), BF16), pltpu.VMEM((N_HEAD * bq, SLAB), BF16),
                        pltpu.VMEM((bq, N_HEAD * bq), BF16), pltpu.VMEM((2, N_HEAD * bq, bq), BF16),
                        pltpu.VMEM((2, N_HEAD * bq, bq), BF16), pltpu.VMEM((bq, SLAB), F32),
                        pltpu.VMEM((N_HEAD, bq, 128), F32), pltpu.SemaphoreType.DMA((2, 2))],
        compiler_params=_params(2))(p16, p16, p16)


def _sb_bwd(name, p16, o, w_saved, lb_saved, dcat):
    s = p16.shape[1]
    bq = min(SB_BLOCK, s)
    nq = s // bq

    def kern(q_ref, k_ref, v_ref, o_ref, do_ref, w_hbm, lb_hbm, dq_ref, dk_hbm, dv_hbm, dk_acc, dv_acc, dq_acc,
             qm_ref, dom_ref, k4_ref, dzc_ref, dzs_ref, ws_ref, lbs_ref, t_ref, c_ref, sem, lsem):
        g, qi = pl.program_id(0), pl.program_id(1)

        @pl.when(qi == 0)
        def _():
            dk_acc[...] = jnp.zeros((s, SLAB), F32)
            dv_acc[...] = jnp.zeros((s, SLAB), F32)

        lane_head = lax.broadcasted_iota(jnp.int32, (1, SLAB), 1) // HEAD
        _head_stack(qm_ref, (q_ref[...].astype(F32) * (HEAD ** -0.5)).astype(BF16), lane_head, bq)
        dob = do_ref[...].astype(BF16)
        _head_stack(dom_ref, dob, lane_head, bq)
        prod = dob.astype(F32) * o_ref[...]
        for h in range(N_HEAD):
            t_ref[h] = jnp.broadcast_to(
                jnp.sum(jnp.where(lane_head == h, prod, 0.0), axis=1, keepdims=True), (bq, 128))
        row = lax.broadcasted_iota(jnp.int32, (bq, bq), 0)
        col = lax.broadcasted_iota(jnp.int32, (bq, bq), 1)
        from_ = (row >= col).astype(BF16)
        tri = col < row
        dq_acc[...] = jnp.zeros((bq, SLAB), F32)
        c_ref[...] = jnp.zeros((N_HEAD, bq, 128), F32)

        def loads(i):
            slot, kb = i % 2, jnp.maximum(qi - i, 0)
            return (pltpu.make_async_copy(w_hbm.at[g, qi, kb], ws_ref.at[slot], lsem.at[0, slot]),
                    pltpu.make_async_copy(lb_hbm.at[g, qi, kb], lbs_ref.at[slot], lsem.at[1, slot]))

        def tile(i, masked):
            kb, slot = qi - i, i % 2
            for cp in loads(i + 1):
                cp.start()
            for cp in loads(i):
                cp.wait()
            rows = pl.ds(pl.multiple_of(kb * bq, bq), bq)
            k, v = k_ref[rows, :], v_ref[rows, :]
            _head_stack(k4_ref, k, lane_head, bq)
            for h in range(N_HEAD):
                mine = slice(h * bq, (h + 1) * bq)
                wb = ws_ref[slot, pl.ds(h * bq, bq), :]
                dl = wb.astype(F32) * lax.dot_general(dom_ref[mine, :], v, NT, preferred_element_type=F32)
                suffix = _split_dot(dl, from_)
                before = t_ref[h][:, 0:1] - (suffix + c_ref[h][:, 0:1])
                sig = jnp.exp(lbs_ref[slot, pl.ds(h * bq, bq), :].astype(F32))
                dz = dl - sig * (dl + before)
                if masked:
                    dz = jnp.where(tri, dz, 0.0)
                dzb = dz.astype(BF16)
                dzc_ref[:, mine] = dzb
                dzs_ref[mine, :] = dzb
                c_ref[h] += _first_col(suffix)
            dq_acc[...] += jnp.dot(dzc_ref[...], k4_ref[...], preferred_element_type=F32)
            dk_acc[rows, :] += lax.dot_general(dzs_ref[...], qm_ref[...], TN, preferred_element_type=F32)
            dv_acc[rows, :] += lax.dot_general(ws_ref[slot], dom_ref[...], TN, preferred_element_type=F32)

        for cp in loads(0):
            cp.start()
        tile(0, True)

        def step(i, c):
            tile(i, False)
            return c

        lax.fori_loop(1, qi + 1, step, 0)
        for cp in loads(qi + 1):
            cp.wait()
        dq_ref[...] = dq_acc[...] * (HEAD ** -0.5)

        @pl.when(qi == nq - 1)
        def _():
            ck = pltpu.make_async_copy(dk_acc, dk_hbm.at[g], sem.at[0])
            cv = pltpu.make_async_copy(dv_acc, dv_hbm.at[g], sem.at[1])
            ck.start()
            cv.start()
            ck.wait()
            cv.wait()

    blk = lambda j0: pl.BlockSpec((None, bq, SLAB), lambda g, i: (j0 + g, i, 0))
    full = lambda j0: pl.BlockSpec((None, s, SLAB), lambda g, i: ((j0 + g) // PER_SHARD, 0, (j0 + g) % PER_SHARD))
    q_blk = pl.BlockSpec((None, bq, SLAB), lambda g, i: ((2 + g) // PER_SHARD, i, (2 + g) % PER_SHARD))
    stack16 = pltpu.VMEM((N_HEAD * bq, SLAB), BF16)
    return pl.pallas_call(
        kern, name=name, grid=(2, nq),
        in_specs=[q_blk, full(4), full(6), blk(0), blk(1),
                  pl.BlockSpec(memory_space=pl.ANY), pl.BlockSpec(memory_space=pl.ANY)],
        out_specs=[blk(0), pl.BlockSpec(memory_space=pl.ANY), pl.BlockSpec(memory_space=pl.ANY)],
        out_shape=[_sds((2, s, SLAB), F32)] * 3,
        scratch_shapes=[pltpu.VMEM((s, SLAB), F32), pltpu.VMEM((s, SLAB), F32), pltpu.VMEM((bq, SLAB), F32),
                        stack16, stack16, stack16, pltpu.VMEM((bq, N_HEAD * bq), BF16),
                        pltpu.VMEM((N_HEAD * bq, bq), BF16), pltpu.VMEM((2, N_HEAD * bq, bq), BF16),
                        pltpu.VMEM((2, N_HEAD * bq, bq), BF16),
                        pltpu.VMEM((N_HEAD, bq, 128), F32), pltpu.VMEM((N_HEAD, bq, 128), F32),
                        pltpu.SemaphoreType.DMA((2,)), pltpu.SemaphoreType.DMA((2, 2))],
        compiler_params=_params(2))(p16, p16, p16, o, dcat, w_saved, lb_saved)


RET_BLOCK = 256


def _ret_tables(s, bl):
    nh = SLAB // HEAD
    lane_h = np.arange(SLAB) // HEAD
    log_gamma = np.log1p(-np.exp2(-5.0 - np.arange(nh, dtype=np.float64)))
    lg_lane = log_gamma[lane_h]
    half = HEAD // 2
    inv = 1.0 / (ROPE_BASE ** (np.arange(half, dtype=np.float64) / half))
    ang = np.arange(s, dtype=np.float64)[:, None] * inv[None, :]
    within = np.arange(SLAB) % HEAD
    cos = np.cos(ang)[:, within % half]
    sin = np.sin(ang)[:, within % half] * np.where(within < half, -1.0, 1.0)[None, :]
    perm = np.zeros((SLAB, SLAB))
    partner = np.where(within < half, np.arange(SLAB) + half, np.arange(SLAB) - half)
    perm[partner, np.arange(SLAB)] = 1.0
    i = np.arange(bl)
    diff = i[:, None] - i[None, :]
    same = (i[:, None] // CHUNK) == (i[None, :] // CHUNK)
    earlier = (i[None, :] // CHUNK) < (i[:, None] // CHUNK)
    decay = np.zeros((nh, bl, bl))
    for h in range(nh):
        decay[h] = np.where(same, np.exp(log_gamma[h] * np.abs(diff)),
                            np.where(earlier, np.exp(log_gamma[h] * diff), 0.0))
    qd = np.exp(lg_lane[None, :] * (i[:, None] + 1.0))
    kd = np.exp(lg_lane[None, :] * (bl - 1.0 - i[:, None]))
    gam = np.exp(lg_lane * bl)[:, None] * np.ones((1, SLAB))
    bd = (lane_h[:, None] == lane_h[None, :]).astype(np.float64)
    f = lambda a: jnp.asarray(a, F32)
    return f(cos), f(sin), f(perm), f(decay), f(qd), f(kd), f(gam), f(bd)


def _ret_block(q, k, v, state, cos, sin, perm, decay, qd, kd, gam, bd, hm):
    qr = (q * cos + jnp.dot(q, perm, preferred_element_type=F32) * sin) * (HEAD ** -0.5)
    kr = k * cos + jnp.dot(k, perm, preferred_element_type=F32) * sin
    y = jnp.dot(qr * qd, state, preferred_element_type=F32)
    for h in range(SLAB // HEAD):
        m = hm[h:h + 1]
        sc = lax.dot_general(qr * m, kr, NT, preferred_element_type=F32) * decay[h]
        y = y + jnp.dot(sc, v * m, preferred_element_type=F32)
    new_state = gam * state + lax.dot_general(kr * kd, v, TN, preferred_element_type=F32) * bd
    return y, new_state


def _ret_specs(s, bl, rev):
    nb = s // bl
    pos = (lambda n: nb - 1 - n) if rev else (lambda n: n)
    slab = lambda j: pl.BlockSpec((None, bl, SLAB), lambda n: (j // PER_SHARD, pos(n), j % PER_SHARD))
    const2 = lambda r: pl.BlockSpec((r, SLAB), lambda n: (0, 0))
    tab = [pl.BlockSpec((bl, SLAB), lambda n: (pos(n), 0))] * 2 + [
        const2(SLAB), pl.BlockSpec((SLAB // HEAD, bl, bl), lambda n: (0, 0, 0)),
        const2(bl), const2(bl), const2(SLAB), const2(SLAB), const2(8)]
    return nb, pos, slab, tab


def _ret_fwd(name, p32):
    s = p32.shape[1]
    bl = min(RET_BLOCK, s)
    nb, pos, slab, tab = _ret_specs(s, bl, False)
    tables = _ret_tables(s, bl) + (_head_masks(),)

    def kern(q_ref, k_ref, v_ref, *rest):
        t_refs, (y_ref, st_ref, state) = rest[:9], rest[9:]

        @pl.when(pl.program_id(0) == 0)
        def _():
            state[...] = jnp.zeros((SLAB, SLAB), F32)

        st_ref[...] = state[...]
        y, new = _ret_block(q_ref[...], k_ref[...], v_ref[...], state[...], *[t[...] for t in t_refs])
        y_ref[...] = y
        state[...] = new

    return pl.pallas_call(
        kern, name=name, grid=(nb,), in_specs=[slab(8), slab(9), slab(10)] + tab,
        out_specs=[pl.BlockSpec((bl, SLAB), lambda n: (n, 0)), pl.BlockSpec((None, SLAB, SLAB), lambda n: (n, 0, 0))],
        out_shape=[_sds((s, SLAB), F32), _sds((nb, SLAB, SLAB), F32)],
        scratch_shapes=[pltpu.VMEM((SLAB, SLAB), F32)], compiler_params=_params(1))(p32, p32, p32, *tables)


def _ret_bwd(name, p32, states, dy):
    s = p32.shape[1]
    bl = min(RET_BLOCK, s)
    nb, pos, slab, tab = _ret_specs(s, bl, True)
    tables = _ret_tables(s, bl) + (_head_masks(),)
    rowblk = pl.BlockSpec((bl, SLAB), lambda n: (pos(n), 0))

    def kern(q_ref, k_ref, v_ref, st_ref, dy_ref, *rest):
        t_refs, (dq_ref, dk_ref, dv_ref, dstate) = rest[:9], rest[9:]

        @pl.when(pl.program_id(0) == 0)
        def _():
            dstate[...] = jnp.zeros((SLAB, SLAB), F32)

        tv = [t[...] for t in t_refs]
        _, vjp = jax.vjp(lambda a, b, c, d: _ret_block(a, b, c, d, *tv),
                         q_ref[...], k_ref[...], v_ref[...], st_ref[...])
        dq, dk, dv, ds = vjp((dy_ref[...], dstate[...]))
        dq_ref[...] = dq
        dk_ref[...] = dk
        dv_ref[...] = dv
        dstate[...] = ds

    return pl.pallas_call(
        kern, name=name, grid=(nb,),
        in_specs=[slab(8), slab(9), slab(10), pl.BlockSpec((None, SLAB, SLAB), lambda n: (pos(n), 0, 0)), rowblk] + tab,
        out_specs=[rowblk] * 3, out_shape=[_sds((s, SLAB), F32)] * 3,
        scratch_shapes=[pltpu.VMEM((SLAB, SLAB), F32)], compiler_params=_params(1))(p32, p32, p32, states, dy, *tables)


TM_FFN = 1024
TM_SLAB = 2048
TM_NORM = 512
TM_RW = 256
TM_FF = 128


def _ffn_fwd(tag, x, g, w_in, w_out, l):
    tm = min(TM_FFN, x.shape[0])
    h, (u,) = _norm_proj_in(tag + "_in", x, g, w_in, l, tm, [BF16])
    a = _swiglu_fwd(tag + "_act", u, TM_FF)
    w_out2 = w_out.reshape(DEPTH, 2, FF_SHARD, D_MODEL)
    xn = _proj_out(tag + "_out", a, w_out2, l, x, 0.5, tm)
    return xn, (x, h, u, a)


def _ffn_bwd(tag, saved, dxn, g, w_in, w_out, l):
    x, h, u, a = saved
    tm = min(TM_FFN, x.shape[0])
    w_out2 = w_out.reshape(DEPTH, 2, FF_SHARD, D_MODEL)
    da = _back_out(tag + "_dact", dxn, w_out2, l, 0.5, tm, BF16)
    dw_out = _grad_out(tag + "_dwout", a, dxn, 0.5, tm)
    du = _swiglu_bwd(tag + "_dswi", u, da, TM_FF)
    dx, dg = _back_in_norm(tag + "_dh", du, w_in, l, min(TM_NORM, x.shape[0]), x, dxn, g)
    dw_in = _grad_in(tag + "_dwin", h, du, tm)
    return dx, dg, dw_in, dw_out.reshape(N_SHARD, D_FF // N_SHARD, D_MODEL)


def _mix_fwd(tag, x, sm, w_in, w_out, l):
    h, (p32, p16) = _norm_proj_in(tag + "_in", x, sm["mix_norm"][l:l + 1], w_in, l, min(TM_FFN, x.shape[0]),
                                  [F32, BF16])
    ypre = _conv_fwd(tag + "_conv", p32, sm["conv_w"][l], sm["conv_b"][l:l + 1])
    yconv = _ln_silu_fwd(tag + "_ln", ypre, sm["conv_ln_g"][l:l + 1], sm["conv_ln_b"][l:l + 1], TM_RW)
    osb, w_sb, lb_sb = _sb_fwd(tag + "_sb", p16)
    yr, states = _ret_fwd(tag + "_ret", p32)
    yret = _ghn_fwd(tag + "_ghn", yr, p32, sm["ret_norm_g"][l:l + 1], TM_RW)
    ycat = _assemble(tag + "_cat", [(yconv, None), (osb, 0), (osb, 1), (yret, None)], 1, TM_RW)
    xn = _proj_out(tag + "_out", ycat, w_out, l, x, 1.0, min(TM_FFN, x.shape[0]))
    return xn, (x, h, p32, p16, ypre, (osb, w_sb, lb_sb), yr, states, ycat)


def _mix_bwd(tag, saved, dxn, sm, w_in, w_out, l):
    x, h, p32, p16, ypre, osb, yr, states, ycat = saved
    ts = min(TM_SLAB, x.shape[0])
    dcat = _back_out(tag + "_dcat", dxn, w_out, l, 1.0, ts, F32)
    dw_out = _grad_out(tag + "_dwout", ycat, dxn, 1.0, ts)
    dypre, dlg, dlb = _ln_silu_bwd(tag + "_dln", ypre, dcat, sm["conv_ln_g"][l:l + 1], sm["conv_ln_b"][l:l + 1], TM_RW)
    da, db, dcw, dcb = _conv_bwd(tag + "_dconv", p32, sm["conv_w"][l], dypre)
    dq, dk, dv = _sb_bwd(tag + "_dsb", p16, *osb, dcat)
    dyr, dgate, drg = _ghn_bwd(tag + "_dghn", yr, p32, dcat, sm["ret_norm_g"][l:l + 1], TM_RW)
    dqr, dkr, dvr = _ret_bwd(tag + "_dret", p32, states, dyr)
    dp = _assemble(tag + "_dp", [(da, None), (db, None), (dq, 0), (dq, 1), (dk, 0), (dk, 1), (dv, 0), (dv, 1),
                                 (dqr, None), (dkr, None), (dvr, None), (dgate, None)], PER_SHARD, TM_RW)
    tm = min(TM_FFN, x.shape[0])
    dx, dg = _back_in_norm(tag + "_dh", dp, w_in, l, min(TM_NORM, x.shape[0]), x, dxn, sm["mix_norm"][l:l + 1])
    dw_in = _grad_in(tag + "_dwin", h, dp, tm)
    small = dict(mix_norm=dg, conv_w=dcw[0:CONV_W], conv_b=dcb, conv_ln_g=dlg, conv_ln_b=dlb, ret_norm_g=drg)
    return dx, small, dw_in, dw_out


def _local_step(x, tgt, wt, sm):
    saved = []
    for l in range(DEPTH):
        x, s1 = _ffn_fwd(f"l{l}f1", x, sm["ffn1_norm"][l:l + 1], wt["ffn1_w_in"], wt["ffn1_w_out"], l)
        x, s2 = _mix_fwd(f"l{l}mx", x, sm, wt["mix_w_in"], wt["mix_w_out"], l)
        x, s3 = _ffn_fwd(f"l{l}f2", x, sm["ffn2_norm"][l:l + 1], wt["ffn2_w_in"], wt["ffn2_w_out"], l)
        saved.append((s1, s2, s3))
    dx, dfinal, loss = _final("final", x, tgt, sm["final_norm"][None, :], TM_RW)
    big = [None] * DEPTH
    small = [None] * DEPTH
    for l in reversed(range(DEPTH)):
        s1, s2, s3 = saved[l]
        dx, dg3, dwi3, dwo3 = _ffn_bwd(f"l{l}f2", s3, dx, sm["ffn2_norm"][l:l + 1], wt["ffn2_w_in"], wt["ffn2_w_out"], l)
        dx, sml, dwi2, dwo2 = _mix_bwd(f"l{l}mx", s2, dx, sm, wt["mix_w_in"], wt["mix_w_out"], l)
        dx, dg1, dwi1, dwo1 = _ffn_bwd(f"l{l}f1", s1, dx, sm["ffn1_norm"][l:l + 1], wt["ffn1_w_in"], wt["ffn1_w_out"], l)
        big[l] = dict(ffn1_w_in=dwi1, ffn1_w_out=dwo1, mix_w_in=dwi2, mix_w_out=dwo2, ffn2_w_in=dwi3, ffn2_w_out=dwo3)
        sml.update(ffn1_norm=dg1, ffn2_norm=dg3)
        small[l] = sml
    return loss, dx, big, small, dfinal


MESH = pl.DeviceIdType.MESH
ANY = pl.BlockSpec(memory_space=pl.ANY)
BIG = ("ffn1_w_in", "ffn1_w_out", "mix_w_in", "mix_w_out", "ffn2_w_in", "ffn2_w_out")


def _place():
    x, y, c = lax.axis_index("x"), lax.axis_index("y"), lax.axis_index("c")
    chips = [(1 - x, y), (x, 1 - y), (1 - x, 1 - y)]
    return x, y, c, chips


def _gather_weights(w16):
    n = len(w16)

    def kern(*refs):
        dst = refs[n:2 * n]
        send, recv = refs[2 * n:]
        x, y, c, chips = _place()
        mine = 2 * x + y
        firsts, passes = [], []
        for a in range(n):
            h = dst[a].shape[2] // 2
            own = dst[a].at[:, mine, pl.ds(c * h, h)]
            for j, (cx, cy) in enumerate(chips):
                cp = pltpu.make_async_remote_copy(
                    src_ref=own, dst_ref=own, send_sem=send.at[6 * a + j], recv_sem=recv.at[6 * a + j],
                    device_id=(cx, cy, c), device_id_type=MESH)
                cp.start()
                firsts.append(cp)
        for a in range(n):
            h = dst[a].shape[2] // 2
            half = pl.ds(c * h, h)
            for j, (cx, cy) in enumerate(chips):
                theirs = dst[a].at[:, 2 * cx + cy, half]
                pltpu.make_async_remote_copy(
                    src_ref=theirs, dst_ref=theirs, send_sem=send.at[6 * a + j], recv_sem=recv.at[6 * a + j],
                    device_id=(cx, cy, c), device_id_type=MESH).wait_recv()
                fw = pltpu.make_async_remote_copy(
                    src_ref=theirs, dst_ref=theirs, send_sem=send.at[6 * a + 3 + j], recv_sem=recv.at[6 * a + 3 + j],
                    device_id=(x, y, 1 - c), device_id_type=MESH)
                fw.start()
                passes.append(fw)
        for a in range(n):
            h = dst[a].shape[2] // 2
            other = pl.ds((1 - c) * h, h)
            for j, (cx, cy) in enumerate(chips):
                got = dst[a].at[:, 2 * cx + cy, other]
                pltpu.make_async_remote_copy(
                    src_ref=got, dst_ref=got, send_sem=send.at[6 * a + 3 + j], recv_sem=recv.at[6 * a + 3 + j],
                    device_id=(x, y, 1 - c), device_id_type=MESH).wait_recv()
        for cp in firsts + passes:
            cp.wait_send()

    return pl.pallas_call(
        kern, name="gather_weights", in_specs=[ANY] * n, out_specs=[ANY] * n,
        out_shape=[_sds(w.shape, w.dtype) for w in w16], input_output_aliases={a: a for a in range(n)},
        scratch_shapes=[pltpu.SemaphoreType.DMA((6 * n,)), pltpu.SemaphoreType.DMA((6 * n,))])(*w16)


def _pair_exchange(grads):
    n = len(grads)

    def kern(*refs):
        src, got_o = refs[:n], refs[n:2 * n]
        send, recv = refs[2 * n:]
        x, y, c, _ = _place()
        cps = []
        for a in range(n):
            h = src[a].shape[1] // 2
            cp = pltpu.make_async_remote_copy(
                src_ref=src[a].at[:, pl.ds((1 - c) * h, h)], dst_ref=got_o[a],
                send_sem=send.at[a], recv_sem=recv.at[a], device_id=(x, y, 1 - c), device_id_type=MESH)
            cp.start()
            cps.append(cp)
        for cp in cps:
            cp.wait()

    halves = [_sds((g.shape[0], g.shape[1] // 2, g.shape[2]), g.dtype) for g in grads]
    return pl.pallas_call(
        kern, name="pair_exchange", in_specs=[ANY] * n, out_specs=[ANY] * n, out_shape=halves,
        scratch_shapes=[pltpu.SemaphoreType.DMA((n,)), pltpu.SemaphoreType.DMA((n,))])(*grads)


def _chip_exchange(sums):
    n = len(sums)

    def kern(*refs):
        src, dst = refs[:n], refs[n:2 * n]
        send, recv = refs[2 * n:]
        x, y, c, chips = _place()
        cps = []
        for a in range(n):
            for j, (cx, cy) in enumerate(chips):
                cp = pltpu.make_async_remote_copy(
                    src_ref=src[a].at[2 * cx + cy], dst_ref=dst[a].at[j],
                    send_sem=send.at[3 * a + j], recv_sem=recv.at[3 * a + j],
                    device_id=(cx, cy, c), device_id_type=MESH)
                cp.start()
                cps.append(cp)
        for cp in cps:
            cp.wait()

    return pl.pallas_call(
        kern, name="chip_exchange", in_specs=[ANY] * n, out_specs=[ANY] * n,
        out_shape=[_sds((3,) + s_.shape[1:], s_.dtype) for s_ in sums],
        scratch_shapes=[pltpu.SemaphoreType.DMA((3 * n,)), pltpu.SemaphoreType.DMA((3 * n,))])(*sums)


def _pair_join(full):
    n = len(full)

    def kern(*refs):
        dst = refs[n:2 * n]
        send, recv = refs[2 * n:]
        x, y, c, _ = _place()
        cps = []
        for a in range(n):
            h = dst[a].shape[1] // 2
            mine = dst[a].at[:, pl.ds(c * h, h)]
            cp = pltpu.make_async_remote_copy(
                src_ref=mine, dst_ref=mine, send_sem=send.at[a], recv_sem=recv.at[a],
                device_id=(x, y, 1 - c), device_id_type=MESH)
            cp.start()
            cps.append(cp)
        for a, cp in enumerate(cps):
            cp.wait_send()
            h = dst[a].shape[1] // 2
            got = dst[a].at[:, pl.ds((1 - c) * h, h)]
            pltpu.make_async_remote_copy(
                src_ref=got, dst_ref=got, send_sem=send.at[a], recv_sem=recv.at[a],
                device_id=(x, y, 1 - c), device_id_type=MESH).wait_recv()

    return pl.pallas_call(
        kern, name="pair_join", in_specs=[ANY] * n, out_specs=[ANY] * n,
        out_shape=[_sds(f.shape, f.dtype) for f in full], input_output_aliases={a: a for a in range(n)},
        scratch_shapes=[pltpu.SemaphoreType.DMA((n,)), pltpu.SemaphoreType.DMA((n,))])(*full)


def _all_sum(name, v):
    r = v.shape[0]

    def kern(v_ref, o_ref, buf, send, recv):
        x, y, c, _ = _place()
        me = 4 * x + 2 * y + c
        buf[me] = v_ref[...]
        cps = []
        for k in range(1, 8):
            peer = (x ^ (k >> 2), y ^ ((k >> 1) & 1), c ^ (k & 1))
            cp = pltpu.make_async_remote_copy(
                src_ref=v_ref, dst_ref=buf.at[me], send_sem=send.at[k - 1], recv_sem=recv.at[k - 1],
                device_id=peer, device_id_type=MESH)
            cp.start()
            cps.append(cp)
        for k in range(1, 8):
            peer_id = me ^ k
            pltpu.make_async_remote_copy(
                src_ref=v_ref, dst_ref=buf.at[peer_id], send_sem=send.at[k - 1], recv_sem=recv.at[k - 1],
                device_id=(x, y, c), device_id_type=MESH).wait_recv()
        for cp in cps:
            cp.wait_send()
        acc = buf[0]
        for d in range(1, 8):
            acc = acc + buf[d]
        o_ref[...] = acc

    vm = pl.BlockSpec(memory_space=pltpu.VMEM)
    return pl.pallas_call(
        kern, name=name, in_specs=[vm], out_specs=vm, out_shape=_sds((r, 128), F32),
        scratch_shapes=[pltpu.VMEM((8, r, 128), F32), pltpu.SemaphoreType.DMA((7,)),
                        pltpu.SemaphoreType.DMA((7,))])(v)


def _my_chip():
    return 2 * lax.axis_index("x") + lax.axis_index("y")


def _my_core():
    return lax.axis_index("c")


def _cast_place(name, w):
    l, r, c = w.shape
    tr = 64
    return _rw(name, lambda wb: ((wb,), ()), (l, r // tr), [w],
               [pl.BlockSpec((None, tr, c), lambda j, i: (j, i, 0))],
               [_sds((l, N_SHARD, r, c), BF16)],
               [pl.BlockSpec((None, None, tr, c), lambda j, i: (j, _my_chip(), i, 0))])[0]


HALF_TILE = 32


def _add_halves(name, g, got):
    n, h, c = got.shape
    tr, nt = HALF_TILE, h // HALF_TILE
    return _rw(name, lambda ab, bb: ((ab.astype(F32) + bb.astype(F32),), ()), (nt,), [g, got],
               [pl.BlockSpec((n, tr, c), lambda i: (0, _my_core() * nt + i, 0)),
                pl.BlockSpec((n, tr, c), lambda i: (0, i, 0))],
               [_sds((n, h, c), BF16)], [pl.BlockSpec((n, tr, c), lambda i: (0, i, 0))])[0]


def _sum_parts(name, sums, parts, full, layer, n_layer):
    _, h, c = sums.shape
    tr, nt = HALF_TILE, h // HALF_TILE

    def body(own, pb):
        acc = own.astype(F32)
        for j in range(pb.shape[0]):
            acc = acc + pb[j].astype(F32)
        return (acc,), ()

    ins = [sums, parts] + ([full] if full is not None else [])
    in_specs = [pl.BlockSpec((None, tr, c), lambda i: (_my_chip(), i, 0)),
                pl.BlockSpec((parts.shape[0], tr, c), lambda i: (0, i, 0))] + ([ANY] if full is not None else [])
    return _rw(name, body, (nt,), ins, in_specs, [_sds((n_layer, 2 * h, c), F32)],
               [pl.BlockSpec((None, tr, c), lambda i: (layer, _my_core() * nt + i, 0))],
               aliases={2: 0} if full is not None else None)[0]


def _adamw_math(w, g, m, v):
    m = B1 * m + (1.0 - B1) * g
    v = B2 * v + (1.0 - B2) * (g * g)
    m_hat = m / (1.0 - B1 ** STEP)
    v_hat = v / (1.0 - B2 ** STEP)
    delta = -LR * (m_hat / (jnp.sqrt(v_hat) + ADAM_EPS) + WD * w)
    return delta, m, v


def _adamw(name, w, g, m, v):
    r, c = w.shape
    tr = 64 if r % 64 == 0 else 8
    spec = _row_spec(tr, c)
    return _rw(name, lambda *b: (_adamw_math(*b), ()), (r // tr,), [w, g, m, v], [spec] * 4,
               [_sds((r, c), F32)] * 3, [spec] * 3)


SMALL = (("ffn1_norm", (DEPTH, D_MODEL)), ("mix_norm", (DEPTH, D_MODEL)), ("ffn2_norm", (DEPTH, D_MODEL)),
         ("conv_b", (DEPTH, SLAB)), ("conv_ln_g", (DEPTH, SLAB)), ("conv_ln_b", (DEPTH, SLAB)),
         ("ret_norm_g", (DEPTH, SLAB)), ("final_norm", (D_MODEL,)), ("conv_w", (DEPTH, CONV_W, SLAB)))


def _pack(parts, rows):
    flat = jnp.concatenate([p.reshape(-1) for p in parts])
    return jnp.pad(flat, (0, rows * 128 - flat.shape[0])).reshape(rows, 128)


def _unpack(packed, shapes):
    flat = packed.reshape(-1)
    out, off = [], 0
    for shp in shapes:
        n = int(np.prod(shp))
        out.append(flat[off:off + n].reshape(shp))
        off += n
    return out


def kernel(x, ffn1_norm, ffn1_w_in, ffn1_w_out, mix_norm, mix_w_in, conv_w, conv_b, conv_ln_g, conv_ln_b, ret_norm_g, mix_w_out, ffn2_norm, ffn2_w_in, ffn2_w_out, final_norm, loss_target, m_ffn1_norm, m_ffn1_w_in, m_ffn1_w_out, m_mix_norm, m_mix_w_in, m_conv_w, m_conv_b, m_conv_ln_g, m_conv_ln_b, m_ret_norm_g, m_mix_w_out, m_ffn2_norm, m_ffn2_w_in, m_ffn2_w_out, m_final_norm, v_ffn1_norm, v_ffn1_w_in, v_ffn1_w_out, v_mix_norm, v_mix_w_in, v_conv_w, v_conv_b, v_conv_ln_g, v_conv_ln_b, v_ret_norm_g, v_mix_w_out, v_ffn2_norm, v_ffn2_w_in, v_ffn2_w_out, v_final_norm):
    given = dict(locals())
    names = [n for n, _ in SMALL] + list(BIG)
    chip = 2 * lax.axis_index("x") + lax.axis_index("y")
    core = lax.axis_index("c")

    cw_rows = 128
    placed = lax.dynamic_update_slice(jnp.zeros((DEPTH, CONV_W, SLAB), F32), conv_w, (0, 0, chip * HEAD))
    placed = placed * (core == 0).astype(F32)
    conv_w_full = _unpack(_all_sum("gather_conv_w", _pack([placed], cw_rows)), [(DEPTH, CONV_W, SLAB)])[0]

    wt = dict(zip(BIG, _gather_weights([_cast_place("cast_" + n, given[n]) for n in BIG])))
    sm = {n: given[n] for n, _ in SMALL}
    sm["conv_w"] = conv_w_full
    loss, dx, big, small, dfinal = _local_step(x[0], loss_target[0], wt, sm)

    grads = [big[l][n] for n in BIG for l in range(DEPTH)]
    theirs = _pair_exchange(grads)
    sums = [_add_halves(f"chipsum{i}", a, b) for i, (a, b) in enumerate(zip(grads, theirs))]
    parts = _chip_exchange(sums)
    full = []
    for i in range(len(BIG)):
        f = None
        for l in range(DEPTH):
            f = _sum_parts(f"shardsum{DEPTH * i + l}", sums[DEPTH * i + l], parts[DEPTH * i + l], f, l, DEPTH)
        full.append(f)
    g_big = dict(zip(BIG, _pair_join(full)))

    small_parts = []
    for n, shp in SMALL:
        if n == "final_norm":
            small_parts.append(dfinal)
        else:
            small_parts.append(jnp.stack([small[l][n].reshape(shp[1:]) for l in range(DEPTH)]))
    g_small = dict(zip([n for n, _ in SMALL], _unpack(_all_sum("sum_small", _pack(small_parts, 200)), [s_ for _, s_ in SMALL])))
    g_small["conv_w"] = lax.dynamic_slice(g_small["conv_w"], (0, 0, chip * HEAD), (DEPTH, CONV_W, HEAD))

    grad, delta, new_m, new_v = dict(g_small), {}, {}, {}
    grad.update(g_big)
    for n in BIG:
        l, r, c = given[n].shape
        f = lambda t: t.reshape(l * r, c)
        d_, m_, v_ = _adamw("adamw_" + n, f(given[n]), f(grad[n]), f(given["m_" + n]), f(given["v_" + n]))
        delta[n], new_m[n], new_v[n] = d_.reshape(l, r, c), m_.reshape(l, r, c), v_.reshape(l, r, c)
    snames = [n for n, _ in SMALL]
    shapes = [given[n].shape for n in snames]
    rows = 104
    d_, m_, v_ = _adamw("adamw_small", _pack([given[n] for n in snames], rows), _pack([grad[n] for n in snames], rows),
                        _pack([given["m_" + n] for n in snames], rows), _pack([given["v_" + n] for n in snames], rows))
    for dst, packed in ((delta, d_), (new_m, m_), (new_v, v_)):
        dst.update(zip(snames, _unpack(packed, shapes)))

    total = lax.psum(loss[0, 0], ("x", "y", "c"))
    order = ["ffn1_norm", "ffn1_w_in", "ffn1_w_out", "mix_norm", "mix_w_in", "conv_w", "conv_b", "conv_ln_g",
             "conv_ln_b", "ret_norm_g", "mix_w_out", "ffn2_norm", "ffn2_w_in", "ffn2_w_out", "final_norm"]
    return (total, dx[None], *[grad[n] for n in order], *[delta[n] for n in order],
            *[new_m[n] for n in order], *[new_v[n] for n in order])
```

```python
import functools

import numpy as np
import jax
import jax.numpy as jnp
from jax import lax
from jax.experimental import pallas as pl
from jax.experimental.pallas import tpu as pltpu

F32 = jnp.float32
BF16 = jnp.bfloat16

D_MODEL = 1024
D_FF = 2816
N_SHARD = 4
FF_SHARD = 2 * D_FF // N_SHARD
MIX_SHARD = 3072 // N_SHARD
HEAD = 64
SLAB = 256
N_SLAB = 3072 // SLAB
CONV_W = 31
CONV_PAD = 32
CHUNK = 64
EPS = 1e-6
ROPE_BASE = 10000.0
DEPTH = 2

LR, B1, B2, ADAM_EPS, WD, STEP = 0.001, 0.9, 0.999, 1e-08, 0.01, 10

VMEM_LIMIT = 56 * 1024 * 1024


def _params(n_grid):
    return pltpu.CompilerParams(dimension_semantics=("arbitrary",) * n_grid, vmem_limit_bytes=VMEM_LIMIT)


def _rw(name, body, grid, ins, in_specs, rows=(), row_specs=(), accs=(), acc_specs=(), aliases=None):
    n_in, n_row = len(ins), len(rows)
    carried = sorted(aliases) if aliases else []

    def kern(*refs):
        vals = [r[...] for i, r in enumerate(refs[:n_in]) if i not in carried]
        row_vals, acc_vals = body(*vals)
        for r, v in zip(refs[n_in:n_in + n_row], row_vals):
            r[...] = v.astype(r.dtype)
        acc_refs = refs[n_in + n_row:]
        if acc_refs:
            first = functools.reduce(jnp.logical_and, [pl.program_id(a) == 0 for a in range(len(grid))])

            @pl.when(first)
            def _():
                for r in acc_refs:
                    r[...] = jnp.zeros(r.shape, r.dtype)

            for r, v in zip(acc_refs, acc_vals):
                r[...] += v.astype(r.dtype)

    return pl.pallas_call(
        kern, name=name, grid=grid, in_specs=list(in_specs), out_specs=list(row_specs) + list(acc_specs),
        out_shape=list(rows) + list(accs), input_output_aliases=dict(aliases or {}),
        compiler_params=_params(len(grid)))(*ins)


def _sds(shape, dtype):
    return jax.ShapeDtypeStruct(shape, dtype)


def _rms(x, g):
    return x * lax.rsqrt(jnp.mean(x * x, axis=-1, keepdims=True) + EPS) * g


def _row_spec(tm, c):
    return pl.BlockSpec((tm, c), lambda i: (i, 0))


def _vec_spec(c):
    return pl.BlockSpec((1, c), lambda i: (0, 0))


def _swiglu(gate, up):
    return jax.nn.silu(gate) * up


def _swiglu_fwd(name, u, tm):
    _, s, c = u.shape
    return _rw(name, lambda ub: ((_swiglu(ub[0:2].astype(F32), ub[2:4].astype(F32)),), ()), (s // tm,), [u],
               [pl.BlockSpec((4, tm, c), lambda i: (0, i, 0))],
               [_sds((2, s, c), BF16)], [pl.BlockSpec((2, tm, c), lambda i: (0, i, 0))])[0]


def _swiglu_bwd(name, u, da, tm):
    _, s, c = u.shape

    def body(ub, dab):
        _, vjp = jax.vjp(_swiglu, ub[0:2].astype(F32), ub[2:4].astype(F32))
        dg, du = vjp(dab.astype(F32))
        return (jnp.concatenate([dg, du], axis=0),), ()

    return _rw(name, body, (s // tm,), [u, da],
               [pl.BlockSpec((4, tm, c), lambda i: (0, i, 0)), pl.BlockSpec((2, tm, c), lambda i: (0, i, 0))],
               [_sds((4, s, c), BF16)], [pl.BlockSpec((4, tm, c), lambda i: (0, i, 0))])[0]


def _ln_silu(y, g, b):
    mu = jnp.mean(y, axis=-1, keepdims=True)
    yc = y - mu
    var = jnp.mean(yc * yc, axis=-1, keepdims=True)
    return jax.nn.silu(yc * lax.rsqrt(var + EPS) * g + b)


def _ln_silu_fwd(name, y, g, b, tm):
    s, c = y.shape
    return _rw(name, lambda yb, gb, bb: ((_ln_silu(yb, gb, bb),), ()), (s // tm,), [y, g, b],
               [_row_spec(tm, c), _vec_spec(c), _vec_spec(c)], [_sds((s, c), BF16)], [_row_spec(tm, c)])[0]


def _ln_silu_bwd(name, y, dcat, g, b, tm):
    s, c = y.shape

    def body(yb, dob, gb, bb):
        _, vjp = jax.vjp(_ln_silu, yb, gb, bb)
        dy, dg, db = vjp(dob)
        return (dy,), (dg, db)

    return _rw(name, body, (s // tm,), [y, dcat, g, b],
               [_row_spec(tm, c), pl.BlockSpec((None, tm, c), lambda i: (0, i, 0)), _vec_spec(c), _vec_spec(c)],
               [_sds((s, c), F32)], [_row_spec(tm, c)],
               [_sds((1, c), F32)] * 2, [_vec_spec(c)] * 2)


def _head_masks():
    lane = np.arange(SLAB) // HEAD
    m = np.zeros((8, SLAB), np.float32)
    for h in range(SLAB // HEAD):
        m[h] = (lane == h)
    return jnp.asarray(m)


def _gated_head_norm(y, gate, g, hm):
    mu = jnp.zeros_like(y)
    for h in range(SLAB // HEAD):
        mu = mu + hm[h:h + 1] * (jnp.sum(y * hm[h:h + 1], axis=-1, keepdims=True) / HEAD)
    yc = y - mu
    var = jnp.zeros_like(y)
    for h in range(SLAB // HEAD):
        var = var + hm[h:h + 1] * (jnp.sum(yc * yc * hm[h:h + 1], axis=-1, keepdims=True) / HEAD)
    return jax.nn.silu(gate) * (yc * lax.rsqrt(var + EPS) * g)


PER_SHARD = MIX_SHARD // SLAB


def _slab_spec(tm, j):
    return pl.BlockSpec((None, tm, SLAB), lambda i: (j, i, 0))


def _proj_slab_spec(tm, j):
    return pl.BlockSpec((None, tm, SLAB), lambda i: (j // PER_SHARD, i, j % PER_SHARD))


def _ghn_fwd(name, y, p32, g, tm):
    s, c = y.shape
    hm = _head_masks()
    return _rw(name, lambda yb, gb, wb, hb: ((_gated_head_norm(yb, gb, wb, hb),), ()), (s // tm,),
               [y, p32, g, hm],
               [_row_spec(tm, c), _proj_slab_spec(tm, 11), _vec_spec(c), pl.BlockSpec((8, c), lambda i: (0, 0))],
               [_sds((s, c), BF16)], [_row_spec(tm, c)])[0]


def _ghn_bwd(name, y, p32, dcat, g, tm):
    s, c = y.shape
    hm = _head_masks()

    def body(yb, gb, dob, wb, hb):
        _, vjp = jax.vjp(lambda a, b_, c_: _gated_head_norm(a, b_, c_, hb), yb, gb, wb)
        dy, dgate, dw = vjp(dob)
        return (dy, dgate), (dw,)

    return _rw(name, body, (s // tm,), [y, p32, dcat, g, hm],
               [_row_spec(tm, c), _proj_slab_spec(tm, 11), _slab_spec(tm, 3), _vec_spec(c),
                pl.BlockSpec((8, c), lambda i: (0, 0))],
               [_sds((s, c), F32)] * 2, [_row_spec(tm, c)] * 2,
               [_sds((1, c), F32)], [_vec_spec(c)])


def _assemble(name, parts, per, tm):
    s = parts[0][0].shape[-2]
    specs = [_row_spec(tm, SLAB) if j is None else pl.BlockSpec((None, tm, SLAB), lambda i, j=j: (j, i, 0))
             for _, j in parts]

    def body(*blocks):
        rows = [jnp.concatenate([b.astype(BF16) for b in blocks[per * q:per * (q + 1)]], axis=-1)
                for q in range(len(blocks) // per)]
        return (jnp.stack(rows),), ()

    nq = len(parts) // per
    return _rw(name, body, (s // tm,), [a for a, _ in parts], specs, [_sds((nq, s, per * SLAB), BF16)],
               [pl.BlockSpec((nq, tm, per * SLAB), lambda i: (0, i, 0))])[0]


def _final(name, x, tgt, g, tm):
    s, d = x.shape

    def body(xb, tb, gb):
        yf, vjp = jax.vjp(_rms, xb, gb)
        err = yf - tb
        dx, dg = vjp(err * (1.0 / d))
        part = 0.5 * jnp.sum(jnp.mean(err * err, axis=-1, keepdims=True), axis=0, keepdims=True)
        return (dx,), (dg, jnp.broadcast_to(part, (1, 128)))

    return _rw(name, body, (s // tm,), [x, tgt, g],
               [_row_spec(tm, d), _row_spec(tm, d), _vec_spec(d)],
               [_sds((s, d), F32)], [_row_spec(tm, d)],
               [_sds((1, d), F32), _sds((1, 128), F32)], [_vec_spec(d), _vec_spec(128)])


NN = (((1,), (0,)), ((), ()))
NT = (((1,), (1,)), ((), ()))
TN = (((0,), (0,)), ((), ()))


def _mm(name, a, b, grid, a_spec, b_spec, outs, out_specs, acc_shape, dims, alpha=1.0):
    nk = grid[-1]
    n_out = len(outs)

    def kern(*refs):
        a_ref, b_ref = refs[0], refs[1]
        o_refs = refs[2:2 + n_out]
        part = lax.dot_general(a_ref[...].astype(BF16), b_ref[...].astype(BF16), dims,
                               preferred_element_type=F32)

        def finish(r):
            if alpha != 1.0:
                r = r * alpha
            for o in o_refs:
                o[...] = r.astype(o.dtype)

        if nk == 1:
            finish(part)
            return
        acc_ref = refs[-1]
        k = pl.program_id(len(grid) - 1)

        @pl.when(k == 0)
        def _():
            acc_ref[...] = part

        @pl.when(jnp.logical_and(k > 0, k < nk - 1))
        def _():
            acc_ref[...] += part

        @pl.when(k == nk - 1)
        def _():
            finish(acc_ref[...] + part)

    return pl.pallas_call(
        kern, name=name, grid=grid, in_specs=[a_spec, b_spec], out_specs=list(out_specs), out_shape=list(outs),
        scratch_shapes=[pltpu.VMEM(acc_shape, F32)] if nk > 1 else [],
        compiler_params=_params(len(grid)))(a, b)


def _norm_proj_in(name, x, g, w, l, tm, dtypes):
    s, d = x.shape
    n = w.shape[-1]
    n_out = len(dtypes)

    def kern(x_ref, g_ref, w_ref, h_ref, *rest):
        o_refs, h_vmem = rest[:n_out], rest[n_out]

        @pl.when(pl.program_id(1) == 0)
        def _():
            h = _rms(x_ref[...], g_ref[...]).astype(BF16)
            h_vmem[...] = h
            h_ref[...] = h

        r = jnp.dot(h_vmem[...], w_ref[...], preferred_element_type=F32)
        for o in o_refs:
            o[...] = r.astype(o.dtype)

    out = pl.pallas_call(
        kern, name=name, grid=(s // tm, N_SHARD),
        in_specs=[pl.BlockSpec((tm, d), lambda i, b: (i, 0)), pl.BlockSpec((1, d), lambda i, b: (0, 0)),
                  pl.BlockSpec((None, None, d, n), lambda i, b: (l, b, 0, 0))],
        out_specs=[pl.BlockSpec((tm, d), lambda i, b: (i, 0))] +
                  [pl.BlockSpec((None, tm, n), lambda i, b: (b, i, 0))] * n_out,
        out_shape=[_sds((s, d), BF16)] + [_sds((N_SHARD, s, n), t) for t in dtypes],
        scratch_shapes=[pltpu.VMEM((tm, d), BF16)], compiler_params=_params(2))(x, g, w)
    return out[0], out[1:]


def _back_in_norm(name, du, w, l, tm, x, dres, g):
    nk, s, n = du.shape
    d = w.shape[2]

    def kern(du_ref, w_ref, x_ref, dres_ref, g_ref, dx_ref, dg_ref):
        dh = lax.dot_general(du_ref[0], w_ref[0], NT, preferred_element_type=F32)
        for k in range(1, nk):
            dh = dh + lax.dot_general(du_ref[k], w_ref[k], NT, preferred_element_type=F32)
        _, vjp = jax.vjp(_rms, x_ref[...], g_ref[...])
        dx, dg = vjp(dh)
        dx_ref[...] = dx + dres_ref[...]

        @pl.when(pl.program_id(0) == 0)
        def _():
            dg_ref[...] = dg

        @pl.when(pl.program_id(0) > 0)
        def _():
            dg_ref[...] += dg

    row = pl.BlockSpec((tm, d), lambda i: (i, 0))
    vec = pl.BlockSpec((1, d), lambda i: (0, 0))
    return pl.pallas_call(
        kern, name=name, grid=(s // tm,),
        in_specs=[pl.BlockSpec((nk, tm, n), lambda i: (0, i, 0)),
                  pl.BlockSpec((None, nk, d, n), lambda i: (l, 0, 0, 0)), row, row, vec],
        out_specs=[row, vec], out_shape=[_sds((s, d), F32), _sds((1, d), F32)],
        compiler_params=_params(1))(du, w, x, dres, g)


def _proj_out(name, a, w, l, res, alpha, tm):
    nk, s, r = a.shape
    d = w.shape[-1]

    def kern(a_ref, w_ref, res_ref, o_ref):
        y = jnp.dot(a_ref[0], w_ref[0], preferred_element_type=F32)
        for k in range(1, nk):
            y = y + jnp.dot(a_ref[k], w_ref[k], preferred_element_type=F32)
        o_ref[...] = res_ref[...] + (y * alpha if alpha != 1.0 else y)

    row = pl.BlockSpec((tm, d), lambda i: (i, 0))
    return pl.pallas_call(
        kern, name=name, grid=(s // tm,),
        in_specs=[pl.BlockSpec((nk, tm, r), lambda i: (0, i, 0)),
                  pl.BlockSpec((None, nk, r, d), lambda i: (l, 0, 0, 0)), row],
        out_specs=row, out_shape=_sds((s, d), F32), compiler_params=_params(1))(a, w, res)


def _back_out(name, dy, w, l, alpha, tm, out_dtype):
    s, d = dy.shape
    nk, r = w.shape[1], w.shape[2]
    return _mm(name, dy, w, (nk, s // tm, 1),
               pl.BlockSpec((tm, d), lambda b, i, k: (i, 0)),
               pl.BlockSpec((None, None, r, d), lambda b, i, k: (l, b, 0, 0)),
               [_sds((nk, s, r), out_dtype)], [pl.BlockSpec((None, tm, r), lambda b, i, k: (b, i, 0))],
               (tm, r), NT, alpha=alpha)[0]


def _grad_in(name, h, du, ts):
    s, d = h.shape
    nb, _, n = du.shape
    return _mm(name, h, du, (nb, 1, s // ts),
               pl.BlockSpec((ts, d), lambda b, i, k: (k, 0)),
               pl.BlockSpec((None, ts, n), lambda b, i, k: (b, k, 0)),
               [_sds((nb, d, n), BF16)], [pl.BlockSpec((None, d, n), lambda b, i, k: (b, 0, 0))],
               (d, n), TN)[0]


def _grad_out(name, a, dy, alpha, ts):
    nb, s, r = a.shape
    d = dy.shape[1]
    return _mm(name, a, dy, (nb, 1, s // ts),
               pl.BlockSpec((None, ts, r), lambda b, i, k: (b, k, 0)),
               pl.BlockSpec((ts, d), lambda b, i, k: (k, 0)),
               [_sds((nb, r, d), BF16)], [pl.BlockSpec((None, r, d), lambda b, i, k: (b, 0, 0))],
               (r, d), TN, alpha=alpha)[0]


CONV_TILE = 256


def _shifted(win, off, rows):
    n = win.shape[0]
    return pltpu.roll(win, (n - off) % n, 0)[0:rows] if off % n else win[0:rows]


def _conv_fwd(name, p32, w, bias):
    s = p32.shape[1]
    cb = 128
    nt = s // CONV_TILE

    def kern(a_ref, b_ref, w_ref, bias_ref, y_ref, vpad):
        vpad[0:CONV_PAD, :] = jnp.zeros((CONV_PAD, cb), F32)

        def fill(i, c):
            r = pl.multiple_of(i * CONV_TILE, CONV_TILE)
            vpad[pl.ds(CONV_PAD + r, CONV_TILE), :] = (
                a_ref[pl.ds(r, CONV_TILE), :] * jax.nn.sigmoid(b_ref[pl.ds(r, CONV_TILE), :]))
            return c

        lax.fori_loop(0, nt, fill, 0)

        def tile(i, c):
            r = pl.multiple_of(i * CONV_TILE, CONV_TILE)
            win = vpad[pl.ds(r, CONV_TILE + CONV_PAD), :]
            acc = jnp.broadcast_to(bias_ref[...], (CONV_TILE, cb))
            for j in range(CONV_W):
                acc = acc + w_ref[j:j + 1, :] * _shifted(win, j + 2, CONV_TILE)
            y_ref[pl.ds(r, CONV_TILE), :] = acc
            return c

        lax.fori_loop(0, nt, tile, 0)

    return pl.pallas_call(
        kern, name=name, grid=(SLAB // cb,),
        in_specs=[pl.BlockSpec((None, s, cb), lambda c: (0, 0, c)),
                  pl.BlockSpec((None, s, cb), lambda c: (0, 0, SLAB // cb + c)),
                  pl.BlockSpec((CONV_W, cb), lambda c: (0, c)),
                  pl.BlockSpec((1, cb), lambda c: (0, c))],
        out_specs=pl.BlockSpec((s, cb), lambda c: (0, c)),
        out_shape=_sds((s, SLAB), F32),
        scratch_shapes=[pltpu.VMEM((s + CONV_PAD, cb), F32)],
        compiler_params=_params(1))(p32, p32, w, bias)


def _conv_bwd(name, p32, w, dy):
    s = p32.shape[1]
    cb = 128
    nt = s // CONV_TILE

    def kern(a_ref, b_ref, w_ref, dy_ref, da_ref, db_ref, dw_ref, dbias_ref, vpad, dpad):
        vpad[0:CONV_PAD, :] = jnp.zeros((CONV_PAD, cb), F32)
        dpad[s:s + CONV_PAD, :] = jnp.zeros((CONV_PAD, cb), F32)
        dw_ref[...] = jnp.zeros((CONV_PAD, cb), F32)
        dbias_ref[...] = jnp.zeros((1, cb), F32)

        def fill(i, c):
            r = pl.multiple_of(i * CONV_TILE, CONV_TILE)
            vpad[pl.ds(CONV_PAD + r, CONV_TILE), :] = (
                a_ref[pl.ds(r, CONV_TILE), :] * jax.nn.sigmoid(b_ref[pl.ds(r, CONV_TILE), :]))
            dpad[pl.ds(r, CONV_TILE), :] = dy_ref[pl.ds(r, CONV_TILE), :]
            return c

        lax.fori_loop(0, nt, fill, 0)

        def tile(i, c):
            r = pl.multiple_of(i * CONV_TILE, CONV_TILE)
            dwin = dpad[pl.ds(r, CONV_TILE + CONV_PAD), :]
            vwin = vpad[pl.ds(r, CONV_TILE + CONV_PAD), :]
            dyt = dwin[0:CONV_TILE]
            dv = jnp.zeros((CONV_TILE, cb), F32)
            for j in range(CONV_W):
                dv = dv + w_ref[j:j + 1, :] * _shifted(dwin, CONV_W - 1 - j, CONV_TILE)
                dw_ref[j:j + 1, :] += jnp.sum(dyt * _shifted(vwin, j + 2, CONV_TILE), axis=0, keepdims=True)
            dbias_ref[...] += jnp.sum(dyt, axis=0, keepdims=True)
            a = a_ref[pl.ds(r, CONV_TILE), :]
            sg = jax.nn.sigmoid(b_ref[pl.ds(r, CONV_TILE), :])
            da_ref[pl.ds(r, CONV_TILE), :] = dv * sg
            db_ref[pl.ds(r, CONV_TILE), :] = dv * a * sg * (1.0 - sg)
            return c

        lax.fori_loop(0, nt, tile, 0)

    col = pl.BlockSpec((s, cb), lambda c: (0, c))
    return pl.pallas_call(
        kern, name=name, grid=(SLAB // cb,),
        in_specs=[pl.BlockSpec((None, s, cb), lambda c: (0, 0, c)),
                  pl.BlockSpec((None, s, cb), lambda c: (0, 0, SLAB // cb + c)),
                  pl.BlockSpec((CONV_W, cb), lambda c: (0, c)), col],
        out_specs=[col, col, pl.BlockSpec((CONV_PAD, cb), lambda c: (0, c)), pl.BlockSpec((1, cb), lambda c: (0, c))],
        out_shape=[_sds((s, SLAB), F32), _sds((s, SLAB), F32), _sds((CONV_PAD, SLAB), F32), _sds((1, SLAB), F32)],
        scratch_shapes=[pltpu.VMEM((s + CONV_PAD, cb), F32), pltpu.VMEM((s + CONV_PAD, cb), F32)],
        compiler_params=_params(1))(p32, p32, w, dy)


SB_BLOCK = 256
N_HEAD = SLAB // HEAD


def _split_dot(x, m):
    hi = x.astype(BF16)
    lo = (x - hi.astype(F32)).astype(BF16)
    return (jnp.dot(hi, m, preferred_element_type=F32) + jnp.dot(lo, m, preferred_element_type=F32))


def _sb_logits(qm, k, tri):
    z = lax.dot_general(qm, k, NT, preferred_element_type=F32)
    sign_bit = jnp.uint32(0x80000000)
    neg_abs = lax.bitcast_convert_type(lax.bitcast_convert_type(z, jnp.uint32) | sign_bit, F32)
    lb = jnp.minimum(z, 0.0) - jnp.log(1.0 + jnp.exp(neg_abs))
    ln = lb - z
    if tri is not None:
        ln = jnp.where(tri, ln, 0.0)
    return lb, ln


def _first_col(x):
    return jnp.broadcast_to(x[:, 0:1], (x.shape[0], 128))


def _head_stack(dst, x, lane_head, bq):
    for h in range(N_HEAD):
        dst[h * bq:(h + 1) * bq, :] = jnp.where(lane_head == h, x, jnp.zeros_like(x))


def _sb_fwd(name, p16):
    s = p16.shape[1]
    bq = min(SB_BLOCK, s)
    nq = s // bq

    def kern(q_ref, k_ref, v_ref, o_ref, w_hbm, lb_hbm, qm_ref, v4_ref, w4_ref, ws_ref, lbs_ref, acc_ref, r_ref, sem):
        g, qi = pl.program_id(0), pl.program_id(1)
        lane_head = lax.broadcasted_iota(jnp.int32, (1, SLAB), 1) // HEAD
        _head_stack(qm_ref, (q_ref[...].astype(F32) * (HEAD ** -0.5)).astype(BF16), lane_head, bq)
        row = lax.broadcasted_iota(jnp.int32, (bq, bq), 0)
        col = lax.broadcasted_iota(jnp.int32, (bq, bq), 1)
        after = (row > col).astype(BF16)
        tri = col < row
        acc_ref[...] = jnp.zeros((bq, SLAB), F32)
        r_ref[...] = jnp.zeros((N_HEAD, bq, 128), F32)

        def saves(slot, kb):
            return (pltpu.make_async_copy(ws_ref.at[slot], w_hbm.at[g, qi, kb], sem.at[0, slot]),
                    pltpu.make_async_copy(lbs_ref.at[slot], lb_hbm.at[g, qi, kb], sem.at[1, slot]))

        def tile(i, masked):
            kb, slot = qi - i, i % 2
            rows = pl.ds(pl.multiple_of(kb * bq, bq), bq)
            k = k_ref[rows, :]
            _head_stack(v4_ref, v_ref[rows, :], lane_head, bq)
            for h in range(N_HEAD):
                mine = pl.ds(h * bq, bq)
                lb, ln = _sb_logits(qm_ref[h * bq:(h + 1) * bq, :], k, tri if masked else None)
                rem = jnp.dot(ln.astype(BF16), after, preferred_element_type=F32)
                w = jnp.exp(lb + rem + r_ref[h][:, 0:1])
                if masked:
                    w = jnp.where(tri, w, 0.0)
                wb = w.astype(BF16)
                w4_ref[:, h * bq:(h + 1) * bq] = wb
                ws_ref[slot, mine, :] = wb
                lbs_ref[slot, mine, :] = lb.astype(BF16)
                r_ref[h] += _first_col(rem[:, 0:128] + ln[:, 0:128])
            acc_ref[...] += jnp.dot(w4_ref[...], v4_ref[...], preferred_element_type=F32)
            if not masked:
                for cp in saves(1 - slot, kb):
                    cp.wait()
            for cp in saves(slot, kb):
                cp.start()

        tile(0, True)

        def step(i, c):
            tile(i, False)
            return c

        lax.fori_loop(1, qi + 1, step, 0)
        o_ref[...] = acc_ref[...]
        for cp in saves(qi % 2, 0):
            cp.wait()

    saved = _sds((2, nq, nq, N_HEAD * bq, bq), BF16)
    return pl.pallas_call(
        kern, name=name, grid=(2, nq),
        in_specs=[pl.BlockSpec((None, bq, SLAB), lambda g, i: ((2 + g) // PER_SHARD, i, (2 + g) % PER_SHARD)),
                  pl.BlockSpec((None, s, SLAB), lambda g, i: ((4 + g) // PER_SHARD, 0, (4 + g) % PER_SHARD)),
                  pl.BlockSpec((None, s, SLAB), lambda g, i: ((6 + g) // PER_SHARD, 0, (6 + g) % PER_SHARD))],
        out_specs=[pl.BlockSpec((None, bq, SLAB), lambda g, i: (g, i, 0)),
                   pl.BlockSpec(memory_space=pl.ANY), pl.BlockSpec(memory_space=pl.ANY)],
        out_shape=[_sds((2, s, SLAB), F32), saved, saved],
        scratch_shapes=[pltpu.VMEM((N_HEAD * bq, SLAB), BF16), pltpu.VMEM((N_HEAD * bq, SLAB), BF16),
                        pltpu.VMEM((bq, N_HEAD * bq), BF16), pltpu.VMEM((2, N_HEAD * bq, bq), BF16),
                        pltpu.VMEM((2, N_HEAD * bq, bq), BF16), pltpu.VMEM((bq, SLAB), F32),
                        pltpu.VMEM((N_HEAD, bq, 128), F32), pltpu.SemaphoreType.DMA((2, 2))],
        compiler_params=_params(2))(p16, p16, p16)


def _sb_bwd(name, p16, o, w_saved, lb_saved, dcat):
    s = p16.shape[1]
    bq = min(SB_BLOCK, s)
    nq = s // bq

    def kern(q_ref, k_ref, v_ref, o_ref, do_ref, w_hbm, lb_hbm, dq_ref, dk_hbm, dv_hbm, dk_acc, dv_acc, dq_acc,
             qm_ref, dom_ref, k4_ref, dzc_ref, dzs_ref, ws_ref, lbs_ref, t_ref, c_ref, sem, lsem):
        g, qi = pl.program_id(0), pl.program_id(1)

        @pl.when(qi == 0)
        def _():
            dk_acc[...] = jnp.zeros((s, SLAB), F32)
            dv_acc[...] = jnp.zeros((s, SLAB), F32)

        lane_head = lax.broadcasted_iota(jnp.int32, (1, SLAB), 1) // HEAD
        _head_stack(qm_ref, (q_ref[...].astype(F32) * (HEAD ** -0.5)).astype(BF16), lane_head, bq)
        dob = do_ref[...].astype(BF16)
        _head_stack(dom_ref, dob, lane_head, bq)
        prod = dob.astype(F32) * o_ref[...]
        for h in range(N_HEAD):
            t_ref[h] = jnp.broadcast_to(
                jnp.sum(jnp.where(lane_head == h, prod, 0.0), axis=1, keepdims=True), (bq, 128))
        row = lax.broadcasted_iota(jnp.int32, (bq, bq), 0)
        col = lax.broadcasted_iota(jnp.int32, (bq, bq), 1)
        from_ = (row >= col).astype(BF16)
        tri = col < row
        dq_acc[...] = jnp.zeros((bq, SLAB), F32)
        c_ref[...] = jnp.zeros((N_HEAD, bq, 128), F32)

        def loads(i):
            slot, kb = i % 2, jnp.maximum(qi - i, 0)
            return (pltpu.make_async_copy(w_hbm.at[g, qi, kb], ws_ref.at[slot], lsem.at[0, slot]),
                    pltpu.make_async_copy(lb_hbm.at[g, qi, kb], lbs_ref.at[slot], lsem.at[1, slot]))

        def tile(i, masked):
            kb, slot = qi - i, i % 2
            for cp in loads(i + 1):
                cp.start()
            for cp in loads(i):
                cp.wait()
            rows = pl.ds(pl.multiple_of(kb * bq, bq), bq)
            k, v = k_ref[rows, :], v_ref[rows, :]
            _head_stack(k4_ref, k, lane_head, bq)
            for h in range(N_HEAD):
                mine = slice(h * bq, (h + 1) * bq)
                wb = ws_ref[slot, pl.ds(h * bq, bq), :]
                dl = wb.astype(F32) * lax.dot_general(dom_ref[mine, :], v, NT, preferred_element_type=F32)
                suffix = _split_dot(dl, from_)
                before = t_ref[h][:, 0:1] - (suffix + c_ref[h][:, 0:1])
                sig = jnp.exp(lbs_ref[slot, pl.ds(h * bq, bq), :].astype(F32))
                dz = dl - sig * (dl + before)
                if masked:
                    dz = jnp.where(tri, dz, 0.0)
                dzb = dz.astype(BF16)
                dzc_ref[:, mine] = dzb
                dzs_ref[mine, :] = dzb
                c_ref[h] += _first_col(suffix)
            dq_acc[...] += jnp.dot(dzc_ref[...], k4_ref[...], preferred_element_type=F32)
            dk_acc[rows, :] += lax.dot_general(dzs_ref[...], qm_ref[...], TN, preferred_element_type=F32)
            dv_acc[rows, :] += lax.dot_general(ws_ref[slot], dom_ref[...], TN, preferred_element_type=F32)

        for cp in loads(0):
            cp.start()
        tile(0, True)

        def step(i, c):
            tile(i, False)
            return c

        lax.fori_loop(1, qi + 1, step, 0)
        for cp in loads(qi + 1):
            cp.wait()
        dq_ref[...] = dq_acc[...] * (HEAD ** -0.5)

        @pl.when(qi == nq - 1)
        def _():
            ck = pltpu.make_async_copy(dk_acc, dk_hbm.at[g], sem.at[0])
            cv = pltpu.make_async_copy(dv_acc, dv_hbm.at[g], sem.at[1])
            ck.start()
            cv.start()
            ck.wait()
            cv.wait()

    blk = lambda j0: pl.BlockSpec((None, bq, SLAB), lambda g, i: (j0 + g, i, 0))
    full = lambda j0: pl.BlockSpec((None, s, SLAB), lambda g, i: ((j0 + g) // PER_SHARD, 0, (j0 + g) % PER_SHARD))
    q_blk = pl.BlockSpec((None, bq, SLAB), lambda g, i: ((2 + g) // PER_SHARD, i, (2 + g) % PER_SHARD))
    stack16 = pltpu.VMEM((N_HEAD * bq, SLAB), BF16)
    return pl.pallas_call(
        kern, name=name, grid=(2, nq),
        in_specs=[q_blk, full(4), full(6), blk(0), blk(1),
                  pl.BlockSpec(memory_space=pl.ANY), pl.BlockSpec(memory_space=pl.ANY)],
        out_specs=[blk(0), pl.BlockSpec(memory_space=pl.ANY), pl.BlockSpec(memory_space=pl.ANY)],
        out_shape=[_sds((2, s, SLAB), F32)] * 3,
        scratch_shapes=[pltpu.VMEM((s, SLAB), F32), pltpu.VMEM((s, SLAB), F32), pltpu.VMEM((bq, SLAB), F32),
                        stack16, stack16, stack16, pltpu.VMEM((bq, N_HEAD * bq), BF16),
                        pltpu.VMEM((N_HEAD * bq, bq), BF16), pltpu.VMEM((2, N_HEAD * bq, bq), BF16),
                        pltpu.VMEM((2, N_HEAD * bq, bq), BF16),
                        pltpu.VMEM((N_HEAD, bq, 128), F32), pltpu.VMEM((N_HEAD, bq, 128), F32),
                        pltpu.SemaphoreType.DMA((2,)), pltpu.SemaphoreType.DMA((2, 2))],
        compiler_params=_params(2))(p16, p16, p16, o, dcat, w_saved, lb_saved)


RET_BLOCK = 256


def _ret_tables(s, bl):
    nh = SLAB // HEAD
    lane_h = np.arange(SLAB) // HEAD
    log_gamma = np.log1p(-np.exp2(-5.0 - np.arange(nh, dtype=np.float64)))
    lg_lane = log_gamma[lane_h]
    half = HEAD // 2
    inv = 1.0 / (ROPE_BASE ** (np.arange(half, dtype=np.float64) / half))
    ang = np.arange(s, dtype=np.float64)[:, None] * inv[None, :]
    within = np.arange(SLAB) % HEAD
    cos = np.cos(ang)[:, within % half]
    sin = np.sin(ang)[:, within % half] * np.where(within < half, -1.0, 1.0)[None, :]
    perm = np.zeros((SLAB, SLAB))
    partner = np.where(within < half, np.arange(SLAB) + half, np.arange(SLAB) - half)
    perm[partner, np.arange(SLAB)] = 1.0
    i = np.arange(bl)
    diff = i[:, None] - i[None, :]
    same = (i[:, None] // CHUNK) == (i[None, :] // CHUNK)
    earlier = (i[None, :] // CHUNK) < (i[:, None] // CHUNK)
    decay = np.zeros((nh, bl, bl))
    for h in range(nh):
        decay[h] = np.where(same, np.exp(log_gamma[h] * np.abs(diff)),
                            np.where(earlier, np.exp(log_gamma[h] * diff), 0.0))
    qd = np.exp(lg_lane[None, :] * (i[:, None] + 1.0))
    kd = np.exp(lg_lane[None, :] * (bl - 1.0 - i[:, None]))
    gam = np.exp(lg_lane * bl)[:, None] * np.ones((1, SLAB))
    bd = (lane_h[:, None] == lane_h[None, :]).astype(np.float64)
    f = lambda a: jnp.asarray(a, F32)
    return f(cos), f(sin), f(perm), f(decay), f(qd), f(kd), f(gam), f(bd)


def _ret_block(q, k, v, state, cos, sin, perm, decay, qd, kd, gam, bd, hm):
    qr = (q * cos + jnp.dot(q, perm, preferred_element_type=F32) * sin) * (HEAD ** -0.5)
    kr = k * cos + jnp.dot(k, perm, preferred_element_type=F32) * sin
    y = jnp.dot(qr * qd, state, preferred_element_type=F32)
    for h in range(SLAB // HEAD):
        m = hm[h:h + 1]
        sc = lax.dot_general(qr * m, kr, NT, preferred_element_type=F32) * decay[h]
        y = y + jnp.dot(sc, v * m, preferred_element_type=F32)
    new_state = gam * state + lax.dot_general(kr * kd, v, TN, preferred_element_type=F32) * bd
    return y, new_state


def _ret_specs(s, bl, rev):
    nb = s // bl
    pos = (lambda n: nb - 1 - n) if rev else (lambda n: n)
    slab = lambda j: pl.BlockSpec((None, bl, SLAB), lambda n: (j // PER_SHARD, pos(n), j % PER_SHARD))
    const2 = lambda r: pl.BlockSpec((r, SLAB), lambda n: (0, 0))
    tab = [pl.BlockSpec((bl, SLAB), lambda n: (pos(n), 0))] * 2 + [
        const2(SLAB), pl.BlockSpec((SLAB // HEAD, bl, bl), lambda n: (0, 0, 0)),
        const2(bl), const2(bl), const2(SLAB), const2(SLAB), const2(8)]
    return nb, pos, slab, tab


def _ret_fwd(name, p32):
    s = p32.shape[1]
    bl = min(RET_BLOCK, s)
    nb, pos, slab, tab = _ret_specs(s, bl, False)
    tables = _ret_tables(s, bl) + (_head_masks(),)

    def kern(q_ref, k_ref, v_ref, *rest):
        t_refs, (y_ref, st_ref, state) = rest[:9], rest[9:]

        @pl.when(pl.program_id(0) == 0)
        def _():
            state[...] = jnp.zeros((SLAB, SLAB), F32)

        st_ref[...] = state[...]
        y, new = _ret_block(q_ref[...], k_ref[...], v_ref[...], state[...], *[t[...] for t in t_refs])
        y_ref[...] = y
        state[...] = new

    return pl.pallas_call(
        kern, name=name, grid=(nb,), in_specs=[slab(8), slab(9), slab(10)] + tab,
        out_specs=[pl.BlockSpec((bl, SLAB), lambda n: (n, 0)), pl.BlockSpec((None, SLAB, SLAB), lambda n: (n, 0, 0))],
        out_shape=[_sds((s, SLAB), F32), _sds((nb, SLAB, SLAB), F32)],
        scratch_shapes=[pltpu.VMEM((SLAB, SLAB), F32)], compiler_params=_params(1))(p32, p32, p32, *tables)


def _ret_bwd(name, p32, states, dy):
    s = p32.shape[1]
    bl = min(RET_BLOCK, s)
    nb, pos, slab, tab = _ret_specs(s, bl, True)
    tables = _ret_tables(s, bl) + (_head_masks(),)
    rowblk = pl.BlockSpec((bl, SLAB), lambda n: (pos(n), 0))

    def kern(q_ref, k_ref, v_ref, st_ref, dy_ref, *rest):
        t_refs, (dq_ref, dk_ref, dv_ref, dstate) = rest[:9], rest[9:]

        @pl.when(pl.program_id(0) == 0)
        def _():
            dstate[...] = jnp.zeros((SLAB, SLAB), F32)

        tv = [t[...] for t in t_refs]
        _, vjp = jax.vjp(lambda a, b, c, d: _ret_block(a, b, c, d, *tv),
                         q_ref[...], k_ref[...], v_ref[...], st_ref[...])
        dq, dk, dv, ds = vjp((dy_ref[...], dstate[...]))
        dq_ref[...] = dq
        dk_ref[...] = dk
        dv_ref[...] = dv
        dstate[...] = ds

    return pl.pallas_call(
        kern, name=name, grid=(nb,),
        in_specs=[slab(8), slab(9), slab(10), pl.BlockSpec((None, SLAB, SLAB), lambda n: (pos(n), 0, 0)), rowblk] + tab,
        out_specs=[rowblk] * 3, out_shape=[_sds((s, SLAB), F32)] * 3,
        scratch_shapes=[pltpu.VMEM((SLAB, SLAB), F32)], compiler_params=_params(1))(p32, p32, p32, states, dy, *tables)


TM_FFN = 1024
TM_SLAB = 2048
TM_NORM = 256
TM_OUT = 512
TM_RW = 256
TM_FF = 128


def _ffn_fwd(tag, x, g, w_in, w_out, l):
    tm = min(TM_FFN, x.shape[0])
    h, (u,) = _norm_proj_in(tag + "_in", x, g, w_in, l, tm, [BF16])
    a = _swiglu_fwd(tag + "_act", u, TM_FF)
    w_out2 = w_out.reshape(DEPTH, 2, FF_SHARD, D_MODEL)
    xn = _proj_out(tag + "_out", a, w_out2, l, x, 0.5, min(TM_OUT, x.shape[0]))
    return xn, (x, h, u, a)


def _ffn_bwd(tag, saved, dxn, g, w_in, w_out, l):
    x, h, u, a = saved
    tm = min(TM_FFN, x.shape[0])
    w_out2 = w_out.reshape(DEPTH, 2, FF_SHARD, D_MODEL)
    da = _back_out(tag + "_dact", dxn, w_out2, l, 0.5, tm, BF16)
    dw_out = _grad_out(tag + "_dwout", a, dxn, 0.5, tm)
    du = _swiglu_bwd(tag + "_dswi", u, da, TM_FF)
    dx, dg = _back_in_norm(tag + "_dh", du, w_in, l, min(TM_NORM, x.shape[0]), x, dxn, g)
    dw_in = _grad_in(tag + "_dwin", h, du, tm)
    return dx, dg, dw_in, dw_out.reshape(N_SHARD, D_FF // N_SHARD, D_MODEL)


def _mix_fwd(tag, x, sm, w_in, w_out, l):
    h, (p32, p16) = _norm_proj_in(tag + "_in", x, sm["mix_norm"][l:l + 1], w_in, l, min(TM_FFN, x.shape[0]),
                                  [F32, BF16])
    ypre = _conv_fwd(tag + "_conv", p32, sm["conv_w"][l], sm["conv_b"][l:l + 1])
    yconv = _ln_silu_fwd(tag + "_ln", ypre, sm["conv_ln_g"][l:l + 1], sm["conv_ln_b"][l:l + 1], TM_RW)
    osb, w_sb, lb_sb = _sb_fwd(tag + "_sb", p16)
    yr, states = _ret_fwd(tag + "_ret", p32)
    yret = _ghn_fwd(tag + "_ghn", yr, p32, sm["ret_norm_g"][l:l + 1], TM_RW)
    ycat = _assemble(tag + "_cat", [(yconv, None), (osb, 0), (osb, 1), (yret, None)], 1, TM_RW)
    xn = _proj_out(tag + "_out", ycat, w_out, l, x, 1.0, min(TM_OUT, x.shape[0]))
    return xn, (x, h, p32, p16, ypre, (osb, w_sb, lb_sb), yr, states, ycat)


def _mix_bwd(tag, saved, dxn, sm, w_in, w_out, l):
    x, h, p32, p16, ypre, osb, yr, states, ycat = saved
    ts = min(TM_SLAB, x.shape[0])
    dcat = _back_out(tag + "_dcat", dxn, w_out, l, 1.0, ts, F32)
    dw_out = _grad_out(tag + "_dwout", ycat, dxn, 1.0, ts)
    dypre, dlg, dlb = _ln_silu_bwd(tag + "_dln", ypre, dcat, sm["conv_ln_g"][l:l + 1], sm["conv_ln_b"][l:l + 1], TM_RW)
    da, db, dcw, dcb = _conv_bwd(tag + "_dconv", p32, sm["conv_w"][l], dypre)
    dq, dk, dv = _sb_bwd(tag + "_dsb", p16, *osb, dcat)
    dyr, dgate, drg = _ghn_bwd(tag + "_dghn", yr, p32, dcat, sm["ret_norm_g"][l:l + 1], TM_RW)
    dqr, dkr, dvr = _ret_bwd(tag + "_dret", p32, states, dyr)
    dp = _assemble(tag + "_dp", [(da, None), (db, None), (dq, 0), (dq, 1), (dk, 0), (dk, 1), (dv, 0), (dv, 1),
                                 (dqr, None), (dkr, None), (dvr, None), (dgate, None)], PER_SHARD, TM_RW)
    tm = min(TM_FFN, x.shape[0])
    dx, dg = _back_in_norm(tag + "_dh", dp, w_in, l, min(TM_NORM, x.shape[0]), x, dxn, sm["mix_norm"][l:l + 1])
    dw_in = _grad_in(tag + "_dwin", h, dp, tm)
    small = dict(mix_norm=dg, conv_w=dcw[0:CONV_W], conv_b=dcb, conv_ln_g=dlg, conv_ln_b=dlb, ret_norm_g=drg)
    return dx, small, dw_in, dw_out


def _local_step(x, tgt, wt, sm):
    saved = []
    for l in range(DEPTH):
        x, s1 = _ffn_fwd(f"l{l}f1", x, sm["ffn1_norm"][l:l + 1], wt["ffn1_w_in"], wt["ffn1_w_out"], l)
        x, s2 = _mix_fwd(f"l{l}mx", x, sm, wt["mix_w_in"], wt["mix_w_out"], l)
        x, s3 = _ffn_fwd(f"l{l}f2", x, sm["ffn2_norm"][l:l + 1], wt["ffn2_w_in"], wt["ffn2_w_out"], l)
        saved.append((s1, s2, s3))
    dx, dfinal, loss = _final("final", x, tgt, sm["final_norm"][None, :], TM_RW)
    big = [None] * DEPTH
    small = [None] * DEPTH
    for l in reversed(range(DEPTH)):
        s1, s2, s3 = saved[l]
        dx, dg3, dwi3, dwo3 = _ffn_bwd(f"l{l}f2", s3, dx, sm["ffn2_norm"][l:l + 1], wt["ffn2_w_in"], wt["ffn2_w_out"], l)
        dx, sml, dwi2, dwo2 = _mix_bwd(f"l{l}mx", s2, dx, sm, wt["mix_w_in"], wt["mix_w_out"], l)
        dx, dg1, dwi1, dwo1 = _ffn_bwd(f"l{l}f1", s1, dx, sm["ffn1_norm"][l:l + 1], wt["ffn1_w_in"], wt["ffn1_w_out"], l)
        big[l] = dict(ffn1_w_in=dwi1, ffn1_w_out=dwo1, mix_w_in=dwi2, mix_w_out=dwo2, ffn2_w_in=dwi3, ffn2_w_out=dwo3)
        sml.update(ffn1_norm=dg1, ffn2_norm=dg3)
        small[l] = sml
    return loss, dx, big, small, dfinal


MESH = pl.DeviceIdType.MESH
ANY = pl.BlockSpec(memory_space=pl.ANY)
BIG = ("ffn1_w_in", "ffn1_w_out", "mix_w_in", "mix_w_out", "ffn2_w_in", "ffn2_w_out")


def _place():
    x, y, c = lax.axis_index("x"), lax.axis_index("y"), lax.axis_index("c")
    chips = [(1 - x, y), (x, 1 - y), (1 - x, 1 - y)]
    return x, y, c, chips


def _gather_weights(w16):
    n = len(w16)

    def kern(*refs):
        dst = refs[n:2 * n]
        send, recv = refs[2 * n:]
        x, y, c, chips = _place()
        mine = 2 * x + y
        firsts, passes = [], []
        for a in range(n):
            h = dst[a].shape[2] // 2
            own = dst[a].at[:, mine, pl.ds(c * h, h)]
            for j, (cx, cy) in enumerate(chips):
                cp = pltpu.make_async_remote_copy(
                    src_ref=own, dst_ref=own, send_sem=send.at[6 * a + j], recv_sem=recv.at[6 * a + j],
                    device_id=(cx, cy, c), device_id_type=MESH)
                cp.start()
                firsts.append(cp)
        for a in range(n):
            h = dst[a].shape[2] // 2
            half = pl.ds(c * h, h)
            for j, (cx, cy) in enumerate(chips):
                theirs = dst[a].at[:, 2 * cx + cy, half]
                pltpu.make_async_remote_copy(
                    src_ref=theirs, dst_ref=theirs, send_sem=send.at[6 * a + j], recv_sem=recv.at[6 * a + j],
                    device_id=(cx, cy, c), device_id_type=MESH).wait_recv()
                fw = pltpu.make_async_remote_copy(
                    src_ref=theirs, dst_ref=theirs, send_sem=send.at[6 * a + 3 + j], recv_sem=recv.at[6 * a + 3 + j],
                    device_id=(x, y, 1 - c), device_id_type=MESH)
                fw.start()
                passes.append(fw)
        for a in range(n):
            h = dst[a].shape[2] // 2
            other = pl.ds((1 - c) * h, h)
            for j, (cx, cy) in enumerate(chips):
                got = dst[a].at[:, 2 * cx + cy, other]
                pltpu.make_async_remote_copy(
                    src_ref=got, dst_ref=got, send_sem=send.at[6 * a + 3 + j], recv_sem=recv.at[6 * a + 3 + j],
                    device_id=(x, y, 1 - c), device_id_type=MESH).wait_recv()
        for cp in firsts + passes:
            cp.wait_send()

    return pl.pallas_call(
        kern, name="gather_weights", in_specs=[ANY] * n, out_specs=[ANY] * n,
        out_shape=[_sds(w.shape, w.dtype) for w in w16], input_output_aliases={a: a for a in range(n)},
        scratch_shapes=[pltpu.SemaphoreType.DMA((6 * n,)), pltpu.SemaphoreType.DMA((6 * n,))])(*w16)


def _pair_exchange(grads):
    n = len(grads)

    def kern(*refs):
        src, got_o = refs[:n], refs[n:2 * n]
        send, recv = refs[2 * n:]
        x, y, c, _ = _place()
        cps = []
        for a in range(n):
            h = src[a].shape[1] // 2
            cp = pltpu.make_async_remote_copy(
                src_ref=src[a].at[:, pl.ds((1 - c) * h, h)], dst_ref=got_o[a],
                send_sem=send.at[a], recv_sem=recv.at[a], device_id=(x, y, 1 - c), device_id_type=MESH)
            cp.start()
            cps.append(cp)
        for cp in cps:
            cp.wait()

    halves = [_sds((g.shape[0], g.shape[1] // 2, g.shape[2]), g.dtype) for g in grads]
    return pl.pallas_call(
        kern, name="pair_exchange", in_specs=[ANY] * n, out_specs=[ANY] * n, out_shape=halves,
        scratch_shapes=[pltpu.SemaphoreType.DMA((n,)), pltpu.SemaphoreType.DMA((n,))])(*grads)


def _chip_exchange(sums):
    n = len(sums)

    def kern(*refs):
        src, dst = refs[:n], refs[n:2 * n]
        send, recv = refs[2 * n:]
        x, y, c, chips = _place()
        cps = []
        for a in range(n):
            for j, (cx, cy) in enumerate(chips):
                cp = pltpu.make_async_remote_copy(
                    src_ref=src[a].at[2 * cx + cy], dst_ref=dst[a].at[j],
                    send_sem=send.at[3 * a + j], recv_sem=recv.at[3 * a + j],
                    device_id=(cx, cy, c), device_id_type=MESH)
                cp.start()
                cps.append(cp)
        for cp in cps:
            cp.wait()

    return pl.pallas_call(
        kern, name="chip_exchange", in_specs=[ANY] * n, out_specs=[ANY] * n,
        out_shape=[_sds((3,) + s_.shape[1:], s_.dtype) for s_ in sums],
        scratch_shapes=[pltpu.SemaphoreType.DMA((3 * n,)), pltpu.SemaphoreType.DMA((3 * n,))])(*sums)


def _pair_join(full):
    n = len(full)

    def kern(*refs):
        dst = refs[n:2 * n]
        send, recv = refs[2 * n:]
        x, y, c, _ = _place()
        cps = []
        for a in range(n):
            h = dst[a].shape[1] // 2
            mine = dst[a].at[:, pl.ds(c * h, h)]
            cp = pltpu.make_async_remote_copy(
                src_ref=mine, dst_ref=mine, send_sem=send.at[a], recv_sem=recv.at[a],
                device_id=(x, y, 1 - c), device_id_type=MESH)
            cp.start()
            cps.append(cp)
        for a, cp in enumerate(cps):
            cp.wait_send()
            h = dst[a].shape[1] // 2
            got = dst[a].at[:, pl.ds((1 - c) * h, h)]
            pltpu.make_async_remote_copy(
                src_ref=got, dst_ref=got, send_sem=send.at[a], recv_sem=recv.at[a],
                device_id=(x, y, 1 - c), device_id_type=MESH).wait_recv()

    return pl.pallas_call(
        kern, name="pair_join", in_specs=[ANY] * n, out_specs=[ANY] * n,
        out_shape=[_sds(f.shape, f.dtype) for f in full], input_output_aliases={a: a for a in range(n)},
        scratch_shapes=[pltpu.SemaphoreType.DMA((n,)), pltpu.SemaphoreType.DMA((n,))])(*full)


def _all_sum(name, v):
    r = v.shape[0]

    def kern(v_ref, o_ref, buf, send, recv):
        x, y, c, _ = _place()
        me = 4 * x + 2 * y + c
        buf[me] = v_ref[...]
        cps = []
        for k in range(1, 8):
            peer = (x ^ (k >> 2), y ^ ((k >> 1) & 1), c ^ (k & 1))
            cp = pltpu.make_async_remote_copy(
                src_ref=v_ref, dst_ref=buf.at[me], send_sem=send.at[k - 1], recv_sem=recv.at[k - 1],
                device_id=peer, device_id_type=MESH)
            cp.start()
            cps.append(cp)
        for k in range(1, 8):
            peer_id = me ^ k
            pltpu.make_async_remote_copy(
                src_ref=v_ref, dst_ref=buf.at[peer_id], send_sem=send.at[k - 1], recv_sem=recv.at[k - 1],
                device_id=(x, y, c), device_id_type=MESH).wait_recv()
        for cp in cps:
            cp.wait_send()
        acc = buf[0]
        for d in range(1, 8):
            acc = acc + buf[d]
        o_ref[...] = acc

    vm = pl.BlockSpec(memory_space=pltpu.VMEM)
    return pl.pallas_call(
        kern, name=name, in_specs=[vm], out_specs=vm, out_shape=_sds((r, 128), F32),
        scratch_shapes=[pltpu.VMEM((8, r, 128), F32), pltpu.SemaphoreType.DMA((7,)),
                        pltpu.SemaphoreType.DMA((7,))])(v)


def _my_chip():
    return 2 * lax.axis_index("x") + lax.axis_index("y")


def _my_core():
    return lax.axis_index("c")


def _cast_place(name, w):
    l, r, c = w.shape
    tr = 64
    return _rw(name, lambda wb: ((wb,), ()), (l, r // tr), [w],
               [pl.BlockSpec((None, tr, c), lambda j, i: (j, i, 0))],
               [_sds((l, N_SHARD, r, c), BF16)],
               [pl.BlockSpec((None, None, tr, c), lambda j, i: (j, _my_chip(), i, 0))])[0]


HALF_TILE = 32


def _add_halves(name, g, got):
    n, h, c = got.shape
    tr, nt = HALF_TILE, h // HALF_TILE
    return _rw(name, lambda ab, bb: ((ab.astype(F32) + bb.astype(F32),), ()), (nt,), [g, got],
               [pl.BlockSpec((n, tr, c), lambda i: (0, _my_core() * nt + i, 0)),
                pl.BlockSpec((n, tr, c), lambda i: (0, i, 0))],
               [_sds((n, h, c), BF16)], [pl.BlockSpec((n, tr, c), lambda i: (0, i, 0))])[0]


def _sum_parts(name, sums, parts, full, layer, n_layer):
    _, h, c = sums.shape
    tr, nt = HALF_TILE, h // HALF_TILE

    def body(own, pb):
        acc = own.astype(F32)
        for j in range(pb.shape[0]):
            acc = acc + pb[j].astype(F32)
        return (acc,), ()

    ins = [sums, parts] + ([full] if full is not None else [])
    in_specs = [pl.BlockSpec((None, tr, c), lambda i: (_my_chip(), i, 0)),
                pl.BlockSpec((parts.shape[0], tr, c), lambda i: (0, i, 0))] + ([ANY] if full is not None else [])
    return _rw(name, body, (nt,), ins, in_specs, [_sds((n_layer, 2 * h, c), F32)],
               [pl.BlockSpec((None, tr, c), lambda i: (layer, _my_core() * nt + i, 0))],
               aliases={2: 0} if full is not None else None)[0]


def _adamw_math(w, g, m, v):
    m = B1 * m + (1.0 - B1) * g
    v = B2 * v + (1.0 - B2) * (g * g)
    m_hat = m / (1.0 - B1 ** STEP)
    v_hat = v / (1.0 - B2 ** STEP)
    delta = -LR * (m_hat / (jnp.sqrt(v_hat) + ADAM_EPS) + WD * w)
    return delta, m, v


def _adamw(name, w, g, m, v):
    r, c = w.shape
    tr = 64 if r % 64 == 0 else 8
    spec = _row_spec(tr, c)
    return _rw(name, lambda *b: (_adamw_math(*b), ()), (r // tr,), [w, g, m, v], [spec] * 4,
               [_sds((r, c), F32)] * 3, [spec] * 3)


SMALL = (("ffn1_norm", (DEPTH, D_MODEL)), ("mix_norm", (DEPTH, D_MODEL)), ("ffn2_norm", (DEPTH, D_MODEL)),
         ("conv_b", (DEPTH, SLAB)), ("conv_ln_g", (DEPTH, SLAB)), ("conv_ln_b", (DEPTH, SLAB)),
         ("ret_norm_g", (DEPTH, SLAB)), ("final_norm", (D_MODEL,)), ("conv_w", (DEPTH, CONV_W, SLAB)))


def _pack(parts, rows):
    flat = jnp.concatenate([p.reshape(-1) for p in parts])
    return jnp.pad(flat, (0, rows * 128 - flat.shape[0])).reshape(rows, 128)


def _unpack(packed, shapes):
    flat = packed.reshape(-1)
    out, off = [], 0
    for shp in shapes:
        n = int(np.prod(shp))
        out.append(flat[off:off + n].reshape(shp))
        off += n
    return out


def kernel(x, ffn1_norm, ffn1_w_in, ffn1_w_out, mix_norm, mix_w_in, conv_w, conv_b, conv_ln_g, conv_ln_b, ret_norm_g, mix_w_out, ffn2_norm, ffn2_w_in, ffn2_w_out, final_norm, loss_target, m_ffn1_norm, m_ffn1_w_in, m_ffn1_w_out, m_mix_norm, m_mix_w_in, m_conv_w, m_conv_b, m_conv_ln_g, m_conv_ln_b, m_ret_norm_g, m_mix_w_out, m_ffn2_norm, m_ffn2_w_in, m_ffn2_w_out, m_final_norm, v_ffn1_norm, v_ffn1_w_in, v_ffn1_w_out, v_mix_norm, v_mix_w_in, v_conv_w, v_conv_b, v_conv_ln_g, v_conv_ln_b, v_ret_norm_g, v_mix_w_out, v_ffn2_norm, v_ffn2_w_in, v_ffn2_w_out, v_final_norm):
    given = dict(locals())
    names = [n for n, _ in SMALL] + list(BIG)
    chip = 2 * lax.axis_index("x") + lax.axis_index("y")
    core = lax.axis_index("c")

    cw_rows = 128
    placed = lax.dynamic_update_slice(jnp.zeros((DEPTH, CONV_W, SLAB), F32), conv_w, (0, 0, chip * HEAD))
    placed = placed * (core == 0).astype(F32)
    conv_w_full = _unpack(_all_sum("gather_conv_w", _pack([placed], cw_rows)), [(DEPTH, CONV_W, SLAB)])[0]

    wt = dict(zip(BIG, _gather_weights([_cast_place("cast_" + n, given[n]) for n in BIG])))
    sm = {n: given[n] for n, _ in SMALL}
    sm["conv_w"] = conv_w_full
    loss, dx, big, small, dfinal = _local_step(x[0], loss_target[0], wt, sm)

    grads = [big[l][n] for n in BIG for l in range(DEPTH)]
    theirs = _pair_exchange(grads)
    sums = [_add_halves(f"chipsum{i}", a, b) for i, (a, b) in enumerate(zip(grads, theirs))]
    parts = _chip_exchange(sums)
    full = []
    for i in range(len(BIG)):
        f = None
        for l in range(DEPTH):
            f = _sum_parts(f"shardsum{DEPTH * i + l}", sums[DEPTH * i + l], parts[DEPTH * i + l], f, l, DEPTH)
        full.append(f)
    g_big = dict(zip(BIG, _pair_join(full)))

    small_parts = []
    for n, shp in SMALL:
        if n == "final_norm":
            small_parts.append(dfinal)
        else:
            small_parts.append(jnp.stack([small[l][n].reshape(shp[1:]) for l in range(DEPTH)]))
    g_small = dict(zip([n for n, _ in SMALL], _unpack(_all_sum("sum_small", _pack(small_parts, 200)), [s_ for _, s_ in SMALL])))
    g_small["conv_w"] = lax.dynamic_slice(g_small["conv_w"], (0, 0, chip * HEAD), (DEPTH, CONV_W, HEAD))

    grad, delta, new_m, new_v = dict(g_small), {}, {}, {}
    grad.update(g_big)
    for n in BIG:
        l, r, c = given[n].shape
        f = lambda t: t.reshape(l * r, c)
        d_, m_, v_ = _adamw("adamw_" + n, f(given[n]), f(grad[n]), f(given["m_" + n]), f(given["v_" + n]))
        delta[n], new_m[n], new_v[n] = d_.reshape(l, r, c), m_.reshape(l, r, c), v_.reshape(l, r, c)
    snames = [n for n, _ in SMALL]
    shapes = [given[n].shape for n in snames]
    rows = 104
    d_, m_, v_ = _adamw("adamw_small", _pack([given[n] for n in snames], rows), _pack([grad[n] for n in snames], rows),
                        _pack([given["m_" + n] for n in snames], rows), _pack([given["v_" + n] for n in snames], rows))
    for dst, packed in ((delta, d_), (new_m, m_), (new_v, v_)):
        dst.update(zip(snames, _unpack(packed, shapes)))

    total = lax.psum(loss[0, 0], ("x", "y", "c"))
    order = ["ffn1_norm", "ffn1_w_in", "ffn1_w_out", "mix_norm", "mix_w_in", "conv_w", "conv_b", "conv_ln_g",
             "conv_ln_b", "ret_norm_g", "mix_w_out", "ffn2_norm", "ffn2_w_in", "ffn2_w_out", "final_norm"]
    return (total, dx[None], *[grad[n] for n in order], *[delta[n] for n in order],
            *[new_m[n] for n in order], *[new_v[n] for n in order])
```

```python
import functools

import numpy as np
import jax
import jax.numpy as jnp
from jax import lax
from jax.experimental import pallas as pl
from jax.experimental.pallas import tpu as pltpu

F32 = jnp.float32
BF16 = jnp.bfloat16

D_MODEL = 1024
D_FF = 2816
N_SHARD = 4
FF_SHARD = 2 * D_FF // N_SHARD
MIX_SHARD = 3072 // N_SHARD
HEAD = 64
SLAB = 256
N_SLAB = 3072 // SLAB
CONV_W = 31
CONV_PAD = 32
CHUNK = 64
EPS = 1e-6
ROPE_BASE = 10000.0
DEPTH = 2

LR, B1, B2, ADAM_EPS, WD, STEP = 0.001, 0.9, 0.999, 1e-08, 0.01, 10

VMEM_LIMIT = 56 * 1024 * 1024


def _params(n_grid):
    return pltpu.CompilerParams(dimension_semantics=("arbitrary",) * n_grid, vmem_limit_bytes=VMEM_LIMIT)


def _rw(name, body, grid, ins, in_specs, rows=(), row_specs=(), accs=(), acc_specs=(), aliases=None):
    n_in, n_row = len(ins), len(rows)
    carried = sorted(aliases) if aliases else []

    def kern(*refs):
        vals = [r[...] for i, r in enumerate(refs[:n_in]) if i not in carried]
        row_vals, acc_vals = body(*vals)
        for r, v in zip(refs[n_in:n_in + n_row], row_vals):
            r[...] = v.astype(r.dtype)
        acc_refs = refs[n_in + n_row:]
        if acc_refs:
            first = functools.reduce(jnp.logical_and, [pl.program_id(a) == 0 for a in range(len(grid))])

            @pl.when(first)
            def _():
                for r in acc_refs:
                    r[...] = jnp.zeros(r.shape, r.dtype)

            for r, v in zip(acc_refs, acc_vals):
                r[...] += v.astype(r.dtype)

    return pl.pallas_call(
        kern, name=name, grid=grid, in_specs=list(in_specs), out_specs=list(row_specs) + list(acc_specs),
        out_shape=list(rows) + list(accs), input_output_aliases=dict(aliases or {}),
        compiler_params=_params(len(grid)))(*ins)


def _sds(shape, dtype):
    return jax.ShapeDtypeStruct(shape, dtype)


def _rms(x, g):
    return x * lax.rsqrt(jnp.mean(x * x, axis=-1, keepdims=True) + EPS) * g


def _row_spec(tm, c):
    return pl.BlockSpec((tm, c), lambda i: (i, 0))


def _vec_spec(c):
    return pl.BlockSpec((1, c), lambda i: (0, 0))


def _swiglu(gate, up):
    return jax.nn.silu(gate) * up


def _swiglu_fwd(name, u, tm):
    _, s, c = u.shape
    return _rw(name, lambda ub: ((_swiglu(ub[0:2].astype(F32), ub[2:4].astype(F32)),), ()), (s // tm,), [u],
               [pl.BlockSpec((4, tm, c), lambda i: (0, i, 0))],
               [_sds((2, s, c), BF16)], [pl.BlockSpec((2, tm, c), lambda i: (0, i, 0))])[0]


def _swiglu_bwd(name, u, da, tm):
    _, s, c = u.shape

    def body(ub, dab):
        _, vjp = jax.vjp(_swiglu, ub[0:2].astype(F32), ub[2:4].astype(F32))
        dg, du = vjp(dab.astype(F32))
        return (jnp.concatenate([dg, du], axis=0),), ()

    return _rw(name, body, (s // tm,), [u, da],
               [pl.BlockSpec((4, tm, c), lambda i: (0, i, 0)), pl.BlockSpec((2, tm, c), lambda i: (0, i, 0))],
               [_sds((4, s, c), BF16)], [pl.BlockSpec((4, tm, c), lambda i: (0, i, 0))])[0]


def _ln_silu(y, g, b):
    mu = jnp.mean(y, axis=-1, keepdims=True)
    yc = y - mu
    var = jnp.mean(yc * yc, axis=-1, keepdims=True)
    return jax.nn.silu(yc * lax.rsqrt(var + EPS) * g + b)


def _ln_silu_fwd(name, y, g, b, tm):
    s, c = y.shape
    return _rw(name, lambda yb, gb, bb: ((_ln_silu(yb, gb, bb),), ()), (s // tm,), [y, g, b],
               [_row_spec(tm, c), _vec_spec(c), _vec_spec(c)], [_sds((s, c), BF16)], [_row_spec(tm, c)])[0]


def _ln_silu_bwd(name, y, dcat, g, b, tm):
    s, c = y.shape

    def body(yb, dob, gb, bb):
        _, vjp = jax.vjp(_ln_silu, yb, gb, bb)
        dy, dg, db = vjp(dob)
        return (dy,), (dg, db)

    return _rw(name, body, (s // tm,), [y, dcat, g, b],
               [_row_spec(tm, c), pl.BlockSpec((None, tm, c), lambda i: (0, i, 0)), _vec_spec(c), _vec_spec(c)],
               [_sds((s, c), F32)], [_row_spec(tm, c)],
               [_sds((1, c), F32)] * 2, [_vec_spec(c)] * 2)


def _head_masks():
    lane = np.arange(SLAB) // HEAD
    m = np.zeros((8, SLAB), np.float32)
    for h in range(SLAB // HEAD):
        m[h] = (lane == h)
    return jnp.asarray(m)


def _gated_head_norm(y, gate, g, hm):
    mu = jnp.zeros_like(y)
    for h in range(SLAB // HEAD):
        mu = mu + hm[h:h + 1] * (jnp.sum(y * hm[h:h + 1], axis=-1, keepdims=True) / HEAD)
    yc = y - mu
    var = jnp.zeros_like(y)
    for h in range(SLAB // HEAD):
        var = var + hm[h:h + 1] * (jnp.sum(yc * yc * hm[h:h + 1], axis=-1, keepdims=True) / HEAD)
    return jax.nn.silu(gate) * (yc * lax.rsqrt(var + EPS) * g)


PER_SHARD = MIX_SHARD // SLAB


def _slab_spec(tm, j):
    return pl.BlockSpec((None, tm, SLAB), lambda i: (j, i, 0))


def _proj_slab_spec(tm, j):
    return pl.BlockSpec((None, tm, SLAB), lambda i: (j // PER_SHARD, i, j % PER_SHARD))


def _ghn_fwd(name, y, p32, g, tm):
    s, c = y.shape
    hm = _head_masks()
    return _rw(name, lambda yb, gb, wb, hb: ((_gated_head_norm(yb, gb, wb, hb),), ()), (s // tm,),
               [y, p32, g, hm],
               [_row_spec(tm, c), _proj_slab_spec(tm, 11), _vec_spec(c), pl.BlockSpec((8, c), lambda i: (0, 0))],
               [_sds((s, c), BF16)], [_row_spec(tm, c)])[0]


def _ghn_bwd(name, y, p32, dcat, g, tm):
    s, c = y.shape
    hm = _head_masks()

    def body(yb, gb, dob, wb, hb):
        _, vjp = jax.vjp(lambda a, b_, c_: _gated_head_norm(a, b_, c_, hb), yb, gb, wb)
        dy, dgate, dw = vjp(dob)
        return (dy, dgate), (dw,)

    return _rw(name, body, (s // tm,), [y, p32, dcat, g, hm],
               [_row_spec(tm, c), _proj_slab_spec(tm, 11), _slab_spec(tm, 3), _vec_spec(c),
                pl.BlockSpec((8, c), lambda i: (0, 0))],
               [_sds((s, c), F32)] * 2, [_row_spec(tm, c)] * 2,
               [_sds((1, c), F32)], [_vec_spec(c)])


def _assemble(name, parts, per, tm):
    s = parts[0][0].shape[-2]
    specs = [_row_spec(tm, SLAB) if j is None else pl.BlockSpec((None, tm, SLAB), lambda i, j=j: (j, i, 0))
             for _, j in parts]

    def body(*blocks):
        rows = [jnp.concatenate([b.astype(BF16) for b in blocks[per * q:per * (q + 1)]], axis=-1)
                for q in range(len(blocks) // per)]
        return (jnp.stack(rows),), ()

    nq = len(parts) // per
    return _rw(name, body, (s // tm,), [a for a, _ in parts], specs, [_sds((nq, s, per * SLAB), BF16)],
               [pl.BlockSpec((nq, tm, per * SLAB), lambda i: (0, i, 0))])[0]


def _final(name, x, tgt, g, tm):
    s, d = x.shape

    def body(xb, tb, gb):
        yf, vjp = jax.vjp(_rms, xb, gb)
        err = yf - tb
        dx, dg = vjp(err * (1.0 / d))
        part = 0.5 * jnp.sum(jnp.mean(err * err, axis=-1, keepdims=True), axis=0, keepdims=True)
        return (dx,), (dg, jnp.broadcast_to(part, (1, 128)))

    return _rw(name, body, (s // tm,), [x, tgt, g],
               [_row_spec(tm, d), _row_spec(tm, d), _vec_spec(d)],
               [_sds((s, d), F32)], [_row_spec(tm, d)],
               [_sds((1, d), F32), _sds((1, 128), F32)], [_vec_spec(d), _vec_spec(128)])


NN = (((1,), (0,)), ((), ()))
NT = (((1,), (1,)), ((), ()))
TN = (((0,), (0,)), ((), ()))


def _mm(name, a, b, grid, a_spec, b_spec, outs, out_specs, acc_shape, dims, alpha=1.0):
    nk = grid[-1]
    n_out = len(outs)

    def kern(*refs):
        a_ref, b_ref = refs[0], refs[1]
        o_refs = refs[2:2 + n_out]
        part = lax.dot_general(a_ref[...].astype(BF16), b_ref[...].astype(BF16), dims,
                               preferred_element_type=F32)

        def finish(r):
            if alpha != 1.0:
                r = r * alpha
            for o in o_refs:
                o[...] = r.astype(o.dtype)

        if nk == 1:
            finish(part)
            return
        acc_ref = refs[-1]
        k = pl.program_id(len(grid) - 1)

        @pl.when(k == 0)
        def _():
            acc_ref[...] = part

        @pl.when(jnp.logical_and(k > 0, k < nk - 1))
        def _():
            acc_ref[...] += part

        @pl.when(k == nk - 1)
        def _():
            finish(acc_ref[...] + part)

    return pl.pallas_call(
        kern, name=name, grid=grid, in_specs=[a_spec, b_spec], out_specs=list(out_specs), out_shape=list(outs),
        scratch_shapes=[pltpu.VMEM(acc_shape, F32)] if nk > 1 else [],
        compiler_params=_params(len(grid)))(a, b)


def _norm_proj_in(name, x, g, w, l, tm, dtypes):
    s, d = x.shape
    n = w.shape[-1]
    n_out = len(dtypes)

    def kern(x_ref, g_ref, w_ref, h_ref, *rest):
        o_refs, h_vmem = rest[:n_out], rest[n_out]

        @pl.when(pl.program_id(1) == 0)
        def _():
            h = _rms(x_ref[...], g_ref[...]).astype(BF16)
            h_vmem[...] = h
            h_ref[...] = h

        r = jnp.dot(h_vmem[...], w_ref[...], preferred_element_type=F32)
        for o in o_refs:
            o[...] = r.astype(o.dtype)

    out = pl.pallas_call(
        kern, name=name, grid=(s // tm, N_SHARD),
        in_specs=[pl.BlockSpec((tm, d), lambda i, b: (i, 0)), pl.BlockSpec((1, d), lambda i, b: (0, 0)),
                  pl.BlockSpec((None, None, d, n), lambda i, b: (l, b, 0, 0))],
        out_specs=[pl.BlockSpec((tm, d), lambda i, b: (i, 0))] +
                  [pl.BlockSpec((None, tm, n), lambda i, b: (b, i, 0))] * n_out,
        out_shape=[_sds((s, d), BF16)] + [_sds((N_SHARD, s, n), t) for t in dtypes],
        scratch_shapes=[pltpu.VMEM((tm, d), BF16)], compiler_params=_params(2))(x, g, w)
    return out[0], out[1:]


def _back_in_norm(name, du, w, l, tm, x, dres, g):
    nk, s, n = du.shape
    d = w.shape[2]

    def kern(du_ref, w_ref, x_ref, dres_ref, g_ref, dx_ref, dg_ref):
        dh = lax.dot_general(du_ref[0], w_ref[0], NT, preferred_element_type=F32)
        for k in range(1, nk):
            dh = dh + lax.dot_general(du_ref[k], w_ref[k], NT, preferred_element_type=F32)
        _, vjp = jax.vjp(_rms, x_ref[...], g_ref[...])
        dx, dg = vjp(dh)
        dx_ref[...] = dx + dres_ref[...]

        @pl.when(pl.program_id(0) == 0)
        def _():
            dg_ref[...] = dg

        @pl.when(pl.program_id(0) > 0)
        def _():
            dg_ref[...] += dg

    row = pl.BlockSpec((tm, d), lambda i: (i, 0))
    vec = pl.BlockSpec((1, d), lambda i: (0, 0))
    return pl.pallas_call(
        kern, name=name, grid=(s // tm,),
        in_specs=[pl.BlockSpec((nk, tm, n), lambda i: (0, i, 0)),
                  pl.BlockSpec((None, nk, d, n), lambda i: (l, 0, 0, 0)), row, row, vec],
        out_specs=[row, vec], out_shape=[_sds((s, d), F32), _sds((1, d), F32)],
        compiler_params=_params(1))(du, w, x, dres, g)


def _proj_out(name, a, w, l, res, alpha, tm):
    nk, s, r = a.shape
    d = w.shape[-1]

    def kern(a_ref, w_ref, res_ref, o_ref):
        y = jnp.dot(a_ref[0], w_ref[0], preferred_element_type=F32)
        for k in range(1, nk):
            y = y + jnp.dot(a_ref[k], w_ref[k], preferred_element_type=F32)
        o_ref[...] = res_ref[...] + (y * alpha if alpha != 1.0 else y)

    row = pl.BlockSpec((tm, d), lambda i: (i, 0))
    return pl.pallas_call(
        kern, name=name, grid=(s // tm,),
        in_specs=[pl.BlockSpec((nk, tm, r), lambda i: (0, i, 0)),
                  pl.BlockSpec((None, nk, r, d), lambda i: (l, 0, 0, 0)), row],
        out_specs=row, out_shape=_sds((s, d), F32), compiler_params=_params(1))(a, w, res)


def _back_out(name, dy, w, l, alpha, tm, out_dtype):
    s, d = dy.shape
    nk, r = w.shape[1], w.shape[2]
    return _mm(name, dy, w, (nk, s // tm, 1),
               pl.BlockSpec((tm, d), lambda b, i, k: (i, 0)),
               pl.BlockSpec((None, None, r, d), lambda b, i, k: (l, b, 0, 0)),
               [_sds((nk, s, r), out_dtype)], [pl.BlockSpec((None, tm, r), lambda b, i, k: (b, i, 0))],
               (tm, r), NT, alpha=alpha)[0]


def _grad_in(name, h, du, ts):
    s, d = h.shape
    nb, _, n = du.shape
    return _mm(name, h, du, (nb, 1, s // ts),
               pl.BlockSpec((ts, d), lambda b, i, k: (k, 0)),
               pl.BlockSpec((None, ts, n), lambda b, i, k: (b, k, 0)),
               [_sds((nb, d, n), BF16)], [pl.BlockSpec((None, d, n), lambda b, i, k: (b, 0, 0))],
               (d, n), TN)[0]


def _grad_out(name, a, dy, alpha, ts):
    nb, s, r = a.shape
    d = dy.shape[1]
    return _mm(name, a, dy, (nb, 1, s // ts),
               pl.BlockSpec((None, ts, r), lambda b, i, k: (b, k, 0)),
               pl.BlockSpec((ts, d), lambda b, i, k: (k, 0)),
               [_sds((nb, r, d), BF16)], [pl.BlockSpec((None, r, d), lambda b, i, k: (b, 0, 0))],
               (r, d), TN, alpha=alpha)[0]


CONV_TILE = 256


def _shifted(win, off, rows):
    n = win.shape[0]
    return pltpu.roll(win, (n - off) % n, 0)[0:rows] if off % n else win[0:rows]


def _conv_fwd(name, p32, w, bias):
    s = p32.shape[1]
    cb = 128
    nt = s // CONV_TILE

    def kern(a_ref, b_ref, w_ref, bias_ref, y_ref, vpad):
        vpad[0:CONV_PAD, :] = jnp.zeros((CONV_PAD, cb), F32)

        def fill(i, c):
            r = pl.multiple_of(i * CONV_TILE, CONV_TILE)
            vpad[pl.ds(CONV_PAD + r, CONV_TILE), :] = (
                a_ref[pl.ds(r, CONV_TILE), :] * jax.nn.sigmoid(b_ref[pl.ds(r, CONV_TILE), :]))
            return c

        lax.fori_loop(0, nt, fill, 0)

        def tile(i, c):
            r = pl.multiple_of(i * CONV_TILE, CONV_TILE)
            win = vpad[pl.ds(r, CONV_TILE + CONV_PAD), :]
            acc = jnp.broadcast_to(bias_ref[...], (CONV_TILE, cb))
            for j in range(CONV_W):
                acc = acc + w_ref[j:j + 1, :] * _shifted(win, j + 2, CONV_TILE)
            y_ref[pl.ds(r, CONV_TILE), :] = acc
            return c

        lax.fori_loop(0, nt, tile, 0)

    return pl.pallas_call(
        kern, name=name, grid=(SLAB // cb,),
        in_specs=[pl.BlockSpec((None, s, cb), lambda c: (0, 0, c)),
                  pl.BlockSpec((None, s, cb), lambda c: (0, 0, SLAB // cb + c)),
                  pl.BlockSpec((CONV_W, cb), lambda c: (0, c)),
                  pl.BlockSpec((1, cb), lambda c: (0, c))],
        out_specs=pl.BlockSpec((s, cb), lambda c: (0, c)),
        out_shape=_sds((s, SLAB), F32),
        scratch_shapes=[pltpu.VMEM((s + CONV_PAD, cb), F32)],
        compiler_params=_params(1))(p32, p32, w, bias)


def _conv_bwd(name, p32, w, dy):
    s = p32.shape[1]
    cb = 128
    nt = s // CONV_TILE

    def kern(a_ref, b_ref, w_ref, dy_ref, da_ref, db_ref, dw_ref, dbias_ref, vpad, dpad):
        vpad[0:CONV_PAD, :] = jnp.zeros((CONV_PAD, cb), F32)
        dpad[s:s + CONV_PAD, :] = jnp.zeros((CONV_PAD, cb), F32)
        dw_ref[...] = jnp.zeros((CONV_PAD, cb), F32)
        dbias_ref[...] = jnp.zeros((1, cb), F32)

        def fill(i, c):
            r = pl.multiple_of(i * CONV_TILE, CONV_TILE)
            vpad[pl.ds(CONV_PAD + r, CONV_TILE), :] = (
                a_ref[pl.ds(r, CONV_TILE), :] * jax.nn.sigmoid(b_ref[pl.ds(r, CONV_TILE), :]))
            dpad[pl.ds(r, CONV_TILE), :] = dy_ref[pl.ds(r, CONV_TILE), :]
            return c

        lax.fori_loop(0, nt, fill, 0)

        def tile(i, c):
            r = pl.multiple_of(i * CONV_TILE, CONV_TILE)
            dwin = dpad[pl.ds(r, CONV_TILE + CONV_PAD), :]
            vwin = vpad[pl.ds(r, CONV_TILE + CONV_PAD), :]
            dyt = dwin[0:CONV_TILE]
            dv = jnp.zeros((CONV_TILE, cb), F32)
            for j in range(CONV_W):
                dv = dv + w_ref[j:j + 1, :] * _shifted(dwin, CONV_W - 1 - j, CONV_TILE)
                dw_ref[j:j + 1, :] += jnp.sum(dyt * _shifted(vwin, j + 2, CONV_TILE), axis=0, keepdims=True)
            dbias_ref[...] += jnp.sum(dyt, axis=0, keepdims=True)
            a = a_ref[pl.ds(r, CONV_TILE), :]
            sg = jax.nn.sigmoid(b_ref[pl.ds(r, CONV_TILE), :])
            da_ref[pl.ds(r, CONV_TILE), :] = dv * sg
            db_ref[pl.ds(r, CONV_TILE), :] = dv * a * sg * (1.0 - sg)
            return c

        lax.fori_loop(0, nt, tile, 0)

    col = pl.BlockSpec((s, cb), lambda c: (0, c))
    return pl.pallas_call(
        kern, name=name, grid=(SLAB // cb,),
        in_specs=[pl.BlockSpec((None, s, cb), lambda c: (0, 0, c)),
                  pl.BlockSpec((None, s, cb), lambda c: (0, 0, SLAB // cb + c)),
                  pl.BlockSpec((CONV_W, cb), lambda c: (0, c)), col],
        out_specs=[col, col, pl.BlockSpec((CONV_PAD, cb), lambda c: (0, c)), pl.BlockSpec((1, cb), lambda c: (0, c))],
        out_shape=[_sds((s, SLAB), F32), _sds((s, SLAB), F32), _sds((CONV_PAD, SLAB), F32), _sds((1, SLAB), F32)],
        scratch_shapes=[pltpu.VMEM((s + CONV_PAD, cb), F32), pltpu.VMEM((s + CONV_PAD, cb), F32)],
        compiler_params=_params(1))(p32, p32, w, dy)


SB_BLOCK = 256
N_HEAD = SLAB // HEAD


def _sb_logits(qm, k, tri):
    z = lax.dot_general(qm, k, NT, preferred_element_type=F32)
    sign_bit = jnp.uint32(0x80000000)
    neg_abs = lax.bitcast_convert_type(lax.bitcast_convert_type(z, jnp.uint32) | sign_bit, F32)
    lb = jnp.minimum(z, 0.0) - jnp.log(1.0 + jnp.exp(neg_abs))
    ln = lb - z
    if tri is not None:
        ln = jnp.where(tri, ln, 0.0)
    return lb, ln


def _first_col(x):
    return jnp.broadcast_to(x[:, 0:1], (x.shape[0], 128))


def _head_stack(dst, x, lane_head, bq):
    for h in range(N_HEAD):
        dst[h * bq:(h + 1) * bq, :] = jnp.where(lane_head == h, x, jnp.zeros_like(x))


def _sb_fwd(name, p16):
    s = p16.shape[1]
    bq = min(SB_BLOCK, s)
    nq = s // bq

    def kern(q_ref, k_ref, v_ref, o_ref, w_hbm, lb_hbm, qm_ref, v4_ref, w4_ref, ws_ref, lbs_ref, acc_ref, r_ref, sem):
        g, qi = pl.program_id(0), pl.program_id(1)
        lane_head = lax.broadcasted_iota(jnp.int32, (1, SLAB), 1) // HEAD
        _head_stack(qm_ref, (q_ref[...].astype(F32) * (HEAD ** -0.5)).astype(BF16), lane_head, bq)
        row = lax.broadcasted_iota(jnp.int32, (bq, bq), 0)
        col = lax.broadcasted_iota(jnp.int32, (bq, bq), 1)
        after = (row > col).astype(BF16)
        tri = col < row
        acc_ref[...] = jnp.zeros((bq, SLAB), F32)
        r_ref[...] = jnp.zeros((N_HEAD, bq, 128), F32)

        def saves(slot, kb):
            return (pltpu.make_async_copy(ws_ref.at[slot], w_hbm.at[g, qi, kb], sem.at[0, slot]),
                    pltpu.make_async_copy(lbs_ref.at[slot], lb_hbm.at[g, qi, kb], sem.at[1, slot]))

        def tile(i, masked):
            kb, slot = qi - i, i % 2
            rows = pl.ds(pl.multiple_of(kb * bq, bq), bq)
            k = k_ref[rows, :]
            _head_stack(v4_ref, v_ref[rows, :], lane_head, bq)
            for h in range(N_HEAD):
                mine = pl.ds(h * bq, bq)
                lb, ln = _sb_logits(qm_ref[h * bq:(h + 1) * bq, :], k, tri if masked else None)
                rem = jnp.dot(ln.astype(BF16), after, preferred_element_type=F32)
                w = jnp.exp(lb + rem + r_ref[h][:, 0:1])
                if masked:
                    w = jnp.where(tri, w, 0.0)
                wb = w.astype(BF16)
                w4_ref[:, h * bq:(h + 1) * bq] = wb
                ws_ref[slot, mine, :] = wb
                lbs_ref[slot, mine, :] = lb.astype(BF16)
                r_ref[h] += _first_col(rem[:, 0:128] + ln[:, 0:128])
            acc_ref[...] += jnp.dot(w4_ref[...], v4_ref[...], preferred_element_type=F32)
            if not masked:
                for cp in saves(1 - slot, kb):
                    cp.wait()
            for cp in saves(slot, kb):
                cp.start()

        tile(0, True)

        def step(i, c):
            tile(i, False)
            return c

        lax.fori_loop(1, qi + 1, step, 0)
        o_ref[...] = acc_ref[...]
        for cp in saves(qi % 2, 0):
            cp.wait()

    saved = _sds((2, nq, nq, N_HEAD * bq, bq), BF16)
    return pl.pallas_call(
        kern, name=name, grid=(2, nq),
        in_specs=[pl.BlockSpec((None, bq, SLAB), lambda g, i: ((2 + g) // PER_SHARD, i, (2 + g) % PER_SHARD)),
                  pl.BlockSpec((None, s, SLAB), lambda g, i: ((4 + g) // PER_SHARD, 0, (4 + g) % PER_SHARD)),
                  pl.BlockSpec((None, s, SLAB), lambda g, i: ((6 + g) // PER_SHARD, 0, (6 + g) % PER_SHARD))],
        out_specs=[pl.BlockSpec((None, bq, SLAB), lambda g, i: (g, i, 0)),
                   pl.BlockSpec(memory_space=pl.ANY), pl.BlockSpec(memory_space=pl.ANY)],
        out_shape=[_sds((2, s, SLAB), F32), saved, saved],
        scratch_shapes=[pltpu.VMEM((N_HEAD * bq, SLAB), BF16), pltpu.VMEM((N_HEAD * bq, SLAB), BF16),
                        pltpu.VMEM((bq, N_HEAD * bq), BF16), pltpu.VMEM((2, N_HEAD * bq, bq), BF16),
                        pltpu.VMEM((2, N_HEAD * bq, bq), BF16), pltpu.VMEM((bq, SLAB), F32),
                        pltpu.VMEM((N_HEAD, bq, 128), F32), pltpu.SemaphoreType.DMA((2, 2))],
        compiler_params=_params(2))(p16, p16, p16)


def _sb_bwd(name, p16, w_saved, lb_saved, dcat):
    s = p16.shape[1]
    bq = min(SB_BLOCK, s)
    nq = s // bq

    def kern(q_ref, k_ref, v_ref, do_ref, w_hbm, lb_hbm, dq_ref, dk_hbm, dv_hbm, dk_acc, dv_acc, dq_acc,
             qm_ref, dom_ref, k4_ref, dzc_ref, dzs_ref, ws_ref, lbs_ref, c_ref, sem, lsem):
        g, qi = pl.program_id(0), pl.program_id(1)

        @pl.when(qi == 0)
        def _():
            dk_acc[...] = jnp.zeros((s, SLAB), F32)
            dv_acc[...] = jnp.zeros((s, SLAB), F32)

        lane_head = lax.broadcasted_iota(jnp.int32, (1, SLAB), 1) // HEAD
        _head_stack(qm_ref, (q_ref[...].astype(F32) * (HEAD ** -0.5)).astype(BF16), lane_head, bq)
        _head_stack(dom_ref, do_ref[...].astype(BF16), lane_head, bq)
        row = lax.broadcasted_iota(jnp.int32, (bq, bq), 0)
        col = lax.broadcasted_iota(jnp.int32, (bq, bq), 1)
        earlier = (row < col).astype(BF16)
        tri = col < row
        dq_acc[...] = jnp.zeros((bq, SLAB), F32)
        c_ref[...] = jnp.zeros((N_HEAD, bq, 128), F32)

        def loads(kb):
            slot, kb = kb % 2, jnp.minimum(kb, qi)
            return (pltpu.make_async_copy(w_hbm.at[g, qi, kb], ws_ref.at[slot], lsem.at[0, slot]),
                    pltpu.make_async_copy(lb_hbm.at[g, qi, kb], lbs_ref.at[slot], lsem.at[1, slot]))

        def tile(kb, masked):
            slot = kb % 2
            for cp in loads(kb + 1):
                cp.start()
            for cp in loads(kb):
                cp.wait()
            rows = pl.ds(pl.multiple_of(kb * bq, bq), bq)
            k, v = k_ref[rows, :], v_ref[rows, :]
            _head_stack(k4_ref, k, lane_head, bq)
            for h in range(N_HEAD):
                mine = slice(h * bq, (h + 1) * bq)
                wb = ws_ref[slot, pl.ds(h * bq, bq), :]
                dl = wb.astype(F32) * lax.dot_general(dom_ref[mine, :], v, NT, preferred_element_type=F32)
                prefix = jnp.dot(dl.astype(BF16), earlier, preferred_element_type=F32)
                before = prefix + c_ref[h][:, 0:1]
                sig = jnp.exp(lbs_ref[slot, pl.ds(h * bq, bq), :].astype(F32))
                dz = dl - sig * (dl + before)
                if masked:
                    dz = jnp.where(tri, dz, 0.0)
                dzb = dz.astype(BF16)
                dzc_ref[:, mine] = dzb
                dzs_ref[mine, :] = dzb
                tail = prefix[:, bq - 128:] + dl[:, bq - 128:]
                c_ref[h] += jnp.broadcast_to(tail[:, 127:128], (bq, 128))
            dq_acc[...] += jnp.dot(dzc_ref[...], k4_ref[...], preferred_element_type=F32)
            dk_acc[rows, :] += lax.dot_general(dzs_ref[...], qm_ref[...], TN, preferred_element_type=F32)
            dv_acc[rows, :] += lax.dot_general(ws_ref[slot], dom_ref[...], TN, preferred_element_type=F32)

        for cp in loads(0):
            cp.start()

        def step(kb, c):
            tile(kb, False)
            return c

        lax.fori_loop(0, qi, step, 0)
        tile(qi, True)
        for cp in loads(qi + 1):
            cp.wait()
        dq_ref[...] = dq_acc[...] * (HEAD ** -0.5)

        @pl.when(qi == nq - 1)
        def _():
            ck = pltpu.make_async_copy(dk_acc, dk_hbm.at[g], sem.at[0])
            cv = pltpu.make_async_copy(dv_acc, dv_hbm.at[g], sem.at[1])
            ck.start()
            cv.start()
            ck.wait()
            cv.wait()

    blk = lambda j0: pl.BlockSpec((None, bq, SLAB), lambda g, i: (j0 + g, i, 0))
    full = lambda j0: pl.BlockSpec((None, s, SLAB), lambda g, i: ((j0 + g) // PER_SHARD, 0, (j0 + g) % PER_SHARD))
    q_blk = pl.BlockSpec((None, bq, SLAB), lambda g, i: ((2 + g) // PER_SHARD, i, (2 + g) % PER_SHARD))
    stack16 = pltpu.VMEM((N_HEAD * bq, SLAB), BF16)
    return pl.pallas_call(
        kern, name=name, grid=(2, nq),
        in_specs=[q_blk, full(4), full(6), blk(1),
                  pl.BlockSpec(memory_space=pl.ANY), pl.BlockSpec(memory_space=pl.ANY)],
        out_specs=[blk(0), pl.BlockSpec(memory_space=pl.ANY), pl.BlockSpec(memory_space=pl.ANY)],
        out_shape=[_sds((2, s, SLAB), F32)] * 3,
        scratch_shapes=[pltpu.VMEM((s, SLAB), F32), pltpu.VMEM((s, SLAB), F32), pltpu.VMEM((bq, SLAB), F32),
                        stack16, stack16, stack16, pltpu.VMEM((bq, N_HEAD * bq), BF16),
                        pltpu.VMEM((N_HEAD * bq, bq), BF16), pltpu.VMEM((2, N_HEAD * bq, bq), BF16),
                        pltpu.VMEM((2, N_HEAD * bq, bq), BF16), pltpu.VMEM((N_HEAD, bq, 128), F32),
                        pltpu.SemaphoreType.DMA((2,)), pltpu.SemaphoreType.DMA((2, 2))],
        compiler_params=_params(2))(p16, p16, p16, dcat, w_saved, lb_saved)


RET_BLOCK = 256


def _ret_tables(s, bl):
    nh = SLAB // HEAD
    lane_h = np.arange(SLAB) // HEAD
    log_gamma = np.log1p(-np.exp2(-5.0 - np.arange(nh, dtype=np.float64)))
    lg_lane = log_gamma[lane_h]
    half = HEAD // 2
    inv = 1.0 / (ROPE_BASE ** (np.arange(half, dtype=np.float64) / half))
    ang = np.arange(s, dtype=np.float64)[:, None] * inv[None, :]
    within = np.arange(SLAB) % HEAD
    cos = np.cos(ang)[:, within % half]
    sin = np.sin(ang)[:, within % half] * np.where(within < half, -1.0, 1.0)[None, :]
    perm = np.zeros((SLAB, SLAB))
    partner = np.where(within < half, np.arange(SLAB) + half, np.arange(SLAB) - half)
    perm[partner, np.arange(SLAB)] = 1.0
    i = np.arange(bl)
    diff = i[:, None] - i[None, :]
    same = (i[:, None] // CHUNK) == (i[None, :] // CHUNK)
    earlier = (i[None, :] // CHUNK) < (i[:, None] // CHUNK)
    decay = np.zeros((nh, bl, bl))
    for h in range(nh):
        decay[h] = np.where(same, np.exp(log_gamma[h] * np.abs(diff)),
                            np.where(earlier, np.exp(log_gamma[h] * diff), 0.0))
    qd = np.exp(lg_lane[None, :] * (i[:, None] + 1.0))
    kd = np.exp(lg_lane[None, :] * (bl - 1.0 - i[:, None]))
    gam = np.exp(lg_lane * bl)[:, None] * np.ones((1, SLAB))
    bd = (lane_h[:, None] == lane_h[None, :]).astype(np.float64)
    f = lambda a: jnp.asarray(a, F32)
    return f(cos), f(sin), f(perm), f(decay), f(qd), f(kd), f(gam), f(bd)


def _ret_block(q, k, v, state, cos, sin, perm, decay, qd, kd, gam, bd, hm):
    qr = (q * cos + jnp.dot(q, perm, preferred_element_type=F32) * sin) * (HEAD ** -0.5)
    kr = k * cos + jnp.dot(k, perm, preferred_element_type=F32) * sin
    y = jnp.dot(qr * qd, state, preferred_element_type=F32)
    for h in range(SLAB // HEAD):
        m = hm[h:h + 1]
        sc = lax.dot_general(qr * m, kr, NT, preferred_element_type=F32) * decay[h]
        y = y + jnp.dot(sc, v * m, preferred_element_type=F32)
    new_state = gam * state + lax.dot_general(kr * kd, v, TN, preferred_element_type=F32) * bd
    return y, new_state


def _ret_specs(s, bl, rev):
    nb = s // bl
    pos = (lambda n: nb - 1 - n) if rev else (lambda n: n)
    slab = lambda j: pl.BlockSpec((None, bl, SLAB), lambda n: (j // PER_SHARD, pos(n), j % PER_SHARD))
    const2 = lambda r: pl.BlockSpec((r, SLAB), lambda n: (0, 0))
    tab = [pl.BlockSpec((bl, SLAB), lambda n: (pos(n), 0))] * 2 + [
        const2(SLAB), pl.BlockSpec((SLAB // HEAD, bl, bl), lambda n: (0, 0, 0)),
        const2(bl), const2(bl), const2(SLAB), const2(SLAB), const2(8)]
    return nb, pos, slab, tab


def _ret_fwd(name, p32):
    s = p32.shape[1]
    bl = min(RET_BLOCK, s)
    nb, pos, slab, tab = _ret_specs(s, bl, False)
    tables = _ret_tables(s, bl) + (_head_masks(),)

    def kern(q_ref, k_ref, v_ref, *rest):
        t_refs, (y_ref, st_ref, state) = rest[:9], rest[9:]

        @pl.when(pl.program_id(0) == 0)
        def _():
            state[...] = jnp.zeros((SLAB, SLAB), F32)

        st_ref[...] = state[...]
        y, new = _ret_block(q_ref[...], k_ref[...], v_ref[...], state[...], *[t[...] for t in t_refs])
        y_ref[...] = y
        state[...] = new

    return pl.pallas_call(
        kern, name=name, grid=(nb,), in_specs=[slab(8), slab(9), slab(10)] + tab,
        out_specs=[pl.BlockSpec((bl, SLAB), lambda n: (n, 0)), pl.BlockSpec((None, SLAB, SLAB), lambda n: (n, 0, 0))],
        out_shape=[_sds((s, SLAB), F32), _sds((nb, SLAB, SLAB), F32)],
        scratch_shapes=[pltpu.VMEM((SLAB, SLAB), F32)], compiler_params=_params(1))(p32, p32, p32, *tables)


def _ret_bwd(name, p32, states, dy):
    s = p32.shape[1]
    bl = min(RET_BLOCK, s)
    nb, pos, slab, tab = _ret_specs(s, bl, True)
    tables = _ret_tables(s, bl) + (_head_masks(),)
    rowblk = pl.BlockSpec((bl, SLAB), lambda n: (pos(n), 0))

    def kern(q_ref, k_ref, v_ref, st_ref, dy_ref, *rest):
        t_refs, (dq_ref, dk_ref, dv_ref, dstate) = rest[:9], rest[9:]

        @pl.when(pl.program_id(0) == 0)
        def _():
            dstate[...] = jnp.zeros((SLAB, SLAB), F32)

        tv = [t[...] for t in t_refs]
        _, vjp = jax.vjp(lambda a, b, c, d: _ret_block(a, b, c, d, *tv),
                         q_ref[...], k_ref[...], v_ref[...], st_ref[...])
        dq, dk, dv, ds = vjp((dy_ref[...], dstate[...]))
        dq_ref[...] = dq
        dk_ref[...] = dk
        dv_ref[...] = dv
        dstate[...] = ds

    return pl.pallas_call(
        kern, name=name, grid=(nb,),
        in_specs=[slab(8), slab(9), slab(10), pl.BlockSpec((None, SLAB, SLAB), lambda n: (pos(n), 0, 0)), rowblk] + tab,
        out_specs=[rowblk] * 3, out_shape=[_sds((s, SLAB), F32)] * 3,
        scratch_shapes=[pltpu.VMEM((SLAB, SLAB), F32)], compiler_params=_params(1))(p32, p32, p32, states, dy, *tables)


TM_FFN = 1024
TM_SLAB = 2048
TM_NORM = 256
TM_OUT = 512
TM_RW = 256
TM_FF = 256


def _ffn_fwd(tag, x, g, w_in, w_out, l):
    tm = min(TM_FFN, x.shape[0])
    h, (u,) = _norm_proj_in(tag + "_in", x, g, w_in, l, tm, [BF16])
    a = _swiglu_fwd(tag + "_act", u, TM_FF)
    w_out2 = w_out.reshape(DEPTH, 2, FF_SHARD, D_MODEL)
    xn = _proj_out(tag + "_out", a, w_out2, l, x, 0.5, min(TM_OUT, x.shape[0]))
    return xn, (x, h, u, a)


def _ffn_bwd(tag, saved, dxn, g, w_in, w_out, l):
    x, h, u, a = saved
    tm = min(TM_FFN, x.shape[0])
    w_out2 = w_out.reshape(DEPTH, 2, FF_SHARD, D_MODEL)
    da = _back_out(tag + "_dact", dxn, w_out2, l, 0.5, tm, BF16)
    dw_out = _grad_out(tag + "_dwout", a, dxn, 0.5, tm)
    du = _swiglu_bwd(tag + "_dswi", u, da, TM_FF)
    dx, dg = _back_in_norm(tag + "_dh", du, w_in, l, min(TM_NORM, x.shape[0]), x, dxn, g)
    dw_in = _grad_in(tag + "_dwin", h, du, tm)
    return dx, dg, dw_in, dw_out.reshape(N_SHARD, D_FF // N_SHARD, D_MODEL)


def _mix_fwd(tag, x, sm, w_in, w_out, l):
    h, (p32, p16) = _norm_proj_in(tag + "_in", x, sm["mix_norm"][l:l + 1], w_in, l, min(TM_FFN, x.shape[0]),
                                  [F32, BF16])
    ypre = _conv_fwd(tag + "_conv", p32, sm["conv_w"][l], sm["conv_b"][l:l + 1])
    yconv = _ln_silu_fwd(tag + "_ln", ypre, sm["conv_ln_g"][l:l + 1], sm["conv_ln_b"][l:l + 1], TM_RW)
    osb, w_sb, lb_sb = _sb_fwd(tag + "_sb", p16)
    yr, states = _ret_fwd(tag + "_ret", p32)
    yret = _ghn_fwd(tag + "_ghn", yr, p32, sm["ret_norm_g"][l:l + 1], TM_RW)
    ycat = _assemble(tag + "_cat", [(yconv, None), (osb, 0), (osb, 1), (yret, None)], 1, TM_RW)
    xn = _proj_out(tag + "_out", ycat, w_out, l, x, 1.0, min(TM_OUT, x.shape[0]))
    return xn, (x, h, p32, p16, ypre, (osb, w_sb, lb_sb), yr, states, ycat)


def _mix_bwd(tag, saved, dxn, sm, w_in, w_out, l):
    x, h, p32, p16, ypre, osb, yr, states, ycat = saved
    ts = min(TM_SLAB, x.shape[0])
    dcat = _back_out(tag + "_dcat", dxn, w_out, l, 1.0, ts, F32)
    dw_out = _grad_out(tag + "_dwout", ycat, dxn, 1.0, ts)
    dypre, dlg, dlb = _ln_silu_bwd(tag + "_dln", ypre, dcat, sm["conv_ln_g"][l:l + 1], sm["conv_ln_b"][l:l + 1], TM_RW)
    da, db, dcw, dcb = _conv_bwd(tag + "_dconv", p32, sm["conv_w"][l], dypre)
    dq, dk, dv = _sb_bwd(tag + "_dsb", p16, osb[1], osb[2], dcat)
    dyr, dgate, drg = _ghn_bwd(tag + "_dghn", yr, p32, dcat, sm["ret_norm_g"][l:l + 1], TM_RW)
    dqr, dkr, dvr = _ret_bwd(tag + "_dret", p32, states, dyr)
    dp = _assemble(tag + "_dp", [(da, None), (db, None), (dq, 0), (dq, 1), (dk, 0), (dk, 1), (dv, 0), (dv, 1),
                                 (dqr, None), (dkr, None), (dvr, None), (dgate, None)], PER_SHARD, TM_RW)
    tm = min(TM_FFN, x.shape[0])
    dx, dg = _back_in_norm(tag + "_dh", dp, w_in, l, min(TM_NORM, x.shape[0]), x, dxn, sm["mix_norm"][l:l + 1])
    dw_in = _grad_in(tag + "_dwin", h, dp, tm)
    small = dict(mix_norm=dg, conv_w=dcw[0:CONV_W], conv_b=dcb, conv_ln_g=dlg, conv_ln_b=dlb, ret_norm_g=drg)
    return dx, small, dw_in, dw_out


def _local_step(x, tgt, wt, sm):
    saved = []
    for l in range(DEPTH):
        x, s1 = _ffn_fwd(f"l{l}f1", x, sm["ffn1_norm"][l:l + 1], wt["ffn1_w_in"], wt["ffn1_w_out"], l)
        x, s2 = _mix_fwd(f"l{l}mx", x, sm, wt["mix_w_in"], wt["mix_w_out"], l)
        x, s3 = _ffn_fwd(f"l{l}f2", x, sm["ffn2_norm"][l:l + 1], wt["ffn2_w_in"], wt["ffn2_w_out"], l)
        saved.append((s1, s2, s3))
    dx, dfinal, loss = _final("final", x, tgt, sm["final_norm"][None, :], TM_RW)
    big = [None] * DEPTH
    small = [None] * DEPTH
    for l in reversed(range(DEPTH)):
        s1, s2, s3 = saved[l]
        dx, dg3, dwi3, dwo3 = _ffn_bwd(f"l{l}f2", s3, dx, sm["ffn2_norm"][l:l + 1], wt["ffn2_w_in"], wt["ffn2_w_out"], l)
        dx, sml, dwi2, dwo2 = _mix_bwd(f"l{l}mx", s2, dx, sm, wt["mix_w_in"], wt["mix_w_out"], l)
        dx, dg1, dwi1, dwo1 = _ffn_bwd(f"l{l}f1", s1, dx, sm["ffn1_norm"][l:l + 1], wt["ffn1_w_in"], wt["ffn1_w_out"], l)
        big[l] = dict(ffn1_w_in=dwi1, ffn1_w_out=dwo1, mix_w_in=dwi2, mix_w_out=dwo2, ffn2_w_in=dwi3, ffn2_w_out=dwo3)
        sml.update(ffn1_norm=dg1, ffn2_norm=dg3)
        small[l] = sml
    return loss, dx, big, small, dfinal


MESH = pl.DeviceIdType.MESH
ANY = pl.BlockSpec(memory_space=pl.ANY)
BIG = ("ffn1_w_in", "ffn1_w_out", "mix_w_in", "mix_w_out", "ffn2_w_in", "ffn2_w_out")


def _place():
    x, y, c = lax.axis_index("x"), lax.axis_index("y"), lax.axis_index("c")
    chips = [(1 - x, y), (x, 1 - y), (1 - x, 1 - y)]
    return x, y, c, chips


def _gather_weights(w16):
    n = len(w16)

    def kern(*refs):
        dst = refs[n:2 * n]
        send, recv = refs[2 * n:]
        x, y, c, chips = _place()
        mine = 2 * x + y
        firsts, passes = [], []
        for a in range(n):
            h = dst[a].shape[2] // 2
            own = dst[a].at[:, mine, pl.ds(c * h, h)]
            for j, (cx, cy) in enumerate(chips):
                cp = pltpu.make_async_remote_copy(
                    src_ref=own, dst_ref=own, send_sem=send.at[6 * a + j], recv_sem=recv.at[6 * a + j],
                    device_id=(cx, cy, c), device_id_type=MESH)
                cp.start()
                firsts.append(cp)
        for a in range(n):
            h = dst[a].shape[2] // 2
            half = pl.ds(c * h, h)
            for j, (cx, cy) in enumerate(chips):
                theirs = dst[a].at[:, 2 * cx + cy, half]
                pltpu.make_async_remote_copy(
                    src_ref=theirs, dst_ref=theirs, send_sem=send.at[6 * a + j], recv_sem=recv.at[6 * a + j],
                    device_id=(cx, cy, c), device_id_type=MESH).wait_recv()
                fw = pltpu.make_async_remote_copy(
                    src_ref=theirs, dst_ref=theirs, send_sem=send.at[6 * a + 3 + j], recv_sem=recv.at[6 * a + 3 + j],
                    device_id=(x, y, 1 - c), device_id_type=MESH)
                fw.start()
                passes.append(fw)
        for a in range(n):
            h = dst[a].shape[2] // 2
            other = pl.ds((1 - c) * h, h)
            for j, (cx, cy) in enumerate(chips):
                got = dst[a].at[:, 2 * cx + cy, other]
                pltpu.make_async_remote_copy(
                    src_ref=got, dst_ref=got, send_sem=send.at[6 * a + 3 + j], recv_sem=recv.at[6 * a + 3 + j],
                    device_id=(x, y, 1 - c), device_id_type=MESH).wait_recv()
        for cp in firsts + passes:
            cp.wait_send()

    return pl.pallas_call(
        kern, name="gather_weights", in_specs=[ANY] * n, out_specs=[ANY] * n,
        out_shape=[_sds(w.shape, w.dtype) for w in w16], input_output_aliases={a: a for a in range(n)},
        scratch_shapes=[pltpu.SemaphoreType.DMA((6 * n,)), pltpu.SemaphoreType.DMA((6 * n,))])(*w16)


def _pair_exchange(grads):
    n = len(grads)

    def kern(*refs):
        src, got_o = refs[:n], refs[n:2 * n]
        send, recv = refs[2 * n:]
        x, y, c, _ = _place()
        cps = []
        for a in range(n):
            h = src[a].shape[1] // 2
            cp = pltpu.make_async_remote_copy(
                src_ref=src[a].at[:, pl.ds((1 - c) * h, h)], dst_ref=got_o[a],
                send_sem=send.at[a], recv_sem=recv.at[a], device_id=(x, y, 1 - c), device_id_type=MESH)
            cp.start()
            cps.append(cp)
        for cp in cps:
            cp.wait()

    halves = [_sds((g.shape[0], g.shape[1] // 2, g.shape[2]), g.dtype) for g in grads]
    return pl.pallas_call(
        kern, name="pair_exchange", in_specs=[ANY] * n, out_specs=[ANY] * n, out_shape=halves,
        scratch_shapes=[pltpu.SemaphoreType.DMA((n,)), pltpu.SemaphoreType.DMA((n,))])(*grads)


def _chip_exchange(sums):
    n = len(sums)

    def kern(*refs):
        src, dst = refs[:n], refs[n:2 * n]
        send, recv = refs[2 * n:]
        x, y, c, chips = _place()
        cps = []
        for a in range(n):
            for j, (cx, cy) in enumerate(chips):
                cp = pltpu.make_async_remote_copy(
                    src_ref=src[a].at[2 * cx + cy], dst_ref=dst[a].at[j],
                    send_sem=send.at[3 * a + j], recv_sem=recv.at[3 * a + j],
                    device_id=(cx, cy, c), device_id_type=MESH)
                cp.start()
                cps.append(cp)
        for cp in cps:
            cp.wait()

    return pl.pallas_call(
        kern, name="chip_exchange", in_specs=[ANY] * n, out_specs=[ANY] * n,
        out_shape=[_sds((3,) + s_.shape[1:], s_.dtype) for s_ in sums],
        scratch_shapes=[pltpu.SemaphoreType.DMA((3 * n,)), pltpu.SemaphoreType.DMA((3 * n,))])(*sums)


def _pair_join(full):
    n = len(full)

    def kern(*refs):
        dst = refs[n:2 * n]
        send, recv = refs[2 * n:]
        x, y, c, _ = _place()
        cps = []
        for a in range(n):
            h = dst[a].shape[1] // 2
            mine = dst[a].at[:, pl.ds(c * h, h)]
            cp = pltpu.make_async_remote_copy(
                src_ref=mine, dst_ref=mine, send_sem=send.at[a], recv_sem=recv.at[a],
                device_id=(x, y, 1 - c), device_id_type=MESH)
            cp.start()
            cps.append(cp)
        for a, cp in enumerate(cps):
            cp.wait_send()
            h = dst[a].shape[1] // 2
            got = dst[a].at[:, pl.ds((1 - c) * h, h)]
            pltpu.make_async_remote_copy(
                src_ref=got, dst_ref=got, send_sem=send.at[a], recv_sem=recv.at[a],
                device_id=(x, y, 1 - c), device_id_type=MESH).wait_recv()

    return pl.pallas_call(
        kern, name="pair_join", in_specs=[ANY] * n, out_specs=[ANY] * n,
        out_shape=[_sds(f.shape, f.dtype) for f in full], input_output_aliases={a: a for a in range(n)},
        scratch_shapes=[pltpu.SemaphoreType.DMA((n,)), pltpu.SemaphoreType.DMA((n,))])(*full)


def _all_sum(name, v):
    r = v.shape[0]

    def kern(v_ref, o_ref, buf, send, recv):
        x, y, c, _ = _place()
        me = 4 * x + 2 * y + c
        buf[me] = v_ref[...]
        cps = []
        for k in range(1, 8):
            peer = (x ^ (k >> 2), y ^ ((k >> 1) & 1), c ^ (k & 1))
            cp = pltpu.make_async_remote_copy(
                src_ref=v_ref, dst_ref=buf.at[me], send_sem=send.at[k - 1], recv_sem=recv.at[k - 1],
                device_id=peer, device_id_type=MESH)
            cp.start()
            cps.append(cp)
        for k in range(1, 8):
            peer_id = me ^ k
            pltpu.make_async_remote_copy(
                src_ref=v_ref, dst_ref=buf.at[peer_id], send_sem=send.at[k - 1], recv_sem=recv.at[k - 1],
                device_id=(x, y, c), device_id_type=MESH).wait_recv()
        for cp in cps:
            cp.wait_send()
        acc = buf[0]
        for d in range(1, 8):
            acc = acc + buf[d]
        o_ref[...] = acc

    vm = pl.BlockSpec(memory_space=pltpu.VMEM)
    return pl.pallas_call(
        kern, name=name, in_specs=[vm], out_specs=vm, out_shape=_sds((r, 128), F32),
        scratch_shapes=[pltpu.VMEM((8, r, 128), F32), pltpu.SemaphoreType.DMA((7,)),
                        pltpu.SemaphoreType.DMA((7,))])(v)


def _my_chip():
    return 2 * lax.axis_index("x") + lax.axis_index("y")


def _my_core():
    return lax.axis_index("c")


def _cast_place(name, w):
    l, r, c = w.shape
    tr = 64
    return _rw(name, lambda wb: ((wb,), ()), (l, r // tr), [w],
               [pl.BlockSpec((None, tr, c), lambda j, i: (j, i, 0))],
               [_sds((l, N_SHARD, r, c), BF16)],
               [pl.BlockSpec((None, None, tr, c), lambda j, i: (j, _my_chip(), i, 0))])[0]


HALF_TILE = 32


def _add_halves(name, g, got):
    n, h, c = got.shape
    tr, nt = HALF_TILE, h // HALF_TILE
    return _rw(name, lambda ab, bb: ((ab.astype(F32) + bb.astype(F32),), ()), (nt,), [g, got],
               [pl.BlockSpec((n, tr, c), lambda i: (0, _my_core() * nt + i, 0)),
                pl.BlockSpec((n, tr, c), lambda i: (0, i, 0))],
               [_sds((n, h, c), BF16)], [pl.BlockSpec((n, tr, c), lambda i: (0, i, 0))])[0]


def _sum_parts(name, sums, parts, full, layer, n_layer):
    _, h, c = sums.shape
    tr, nt = HALF_TILE, h // HALF_TILE

    def body(own, pb):
        acc = own.astype(F32)
        for j in range(pb.shape[0]):
            acc = acc + pb[j].astype(F32)
        return (acc,), ()

    ins = [sums, parts] + ([full] if full is not None else [])
    in_specs = [pl.BlockSpec((None, tr, c), lambda i: (_my_chip(), i, 0)),
                pl.BlockSpec((parts.shape[0], tr, c), lambda i: (0, i, 0))] + ([ANY] if full is not None else [])
    return _rw(name, body, (nt,), ins, in_specs, [_sds((n_layer, 2 * h, c), F32)],
               [pl.BlockSpec((None, tr, c), lambda i: (layer, _my_core() * nt + i, 0))],
               aliases={2: 0} if full is not None else None)[0]


def _adamw_math(w, g, m, v):
    m = B1 * m + (1.0 - B1) * g
    v = B2 * v + (1.0 - B2) * (g * g)
    m_hat = m / (1.0 - B1 ** STEP)
    v_hat = v / (1.0 - B2 ** STEP)
    delta = -LR * (m_hat / (jnp.sqrt(v_hat) + ADAM_EPS) + WD * w)
    return delta, m, v


def _adamw(name, w, g, m, v):
    r, c = w.shape
    tr = 64 if r % 64 == 0 else 8
    spec = _row_spec(tr, c)
    return _rw(name, lambda *b: (_adamw_math(*b), ()), (r // tr,), [w, g, m, v], [spec] * 4,
               [_sds((r, c), F32)] * 3, [spec] * 3)


SMALL = (("ffn1_norm", (DEPTH, D_MODEL)), ("mix_norm", (DEPTH, D_MODEL)), ("ffn2_norm", (DEPTH, D_MODEL)),
         ("conv_b", (DEPTH, SLAB)), ("conv_ln_g", (DEPTH, SLAB)), ("conv_ln_b", (DEPTH, SLAB)),
         ("ret_norm_g", (DEPTH, SLAB)), ("final_norm", (D_MODEL,)), ("conv_w", (DEPTH, CONV_W, SLAB)))


def _pack(parts, rows):
    flat = jnp.concatenate([p.reshape(-1) for p in parts])
    return jnp.pad(flat, (0, rows * 128 - flat.shape[0])).reshape(rows, 128)


def _unpack(packed, shapes):
    flat = packed.reshape(-1)
    out, off = [], 0
    for shp in shapes:
        n = int(np.prod(shp))
        out.append(flat[off:off + n].reshape(shp))
        off += n
    return out


def kernel(x, ffn1_norm, ffn1_w_in, ffn1_w_out, mix_norm, mix_w_in, conv_w, conv_b, conv_ln_g, conv_ln_b, ret_norm_g, mix_w_out, ffn2_norm, ffn2_w_in, ffn2_w_out, final_norm, loss_target, m_ffn1_norm, m_ffn1_w_in, m_ffn1_w_out, m_mix_norm, m_mix_w_in, m_conv_w, m_conv_b, m_conv_ln_g, m_conv_ln_b, m_ret_norm_g, m_mix_w_out, m_ffn2_norm, m_ffn2_w_in, m_ffn2_w_out, m_final_norm, v_ffn1_norm, v_ffn1_w_in, v_ffn1_w_out, v_mix_norm, v_mix_w_in, v_conv_w, v_conv_b, v_conv_ln_g, v_conv_ln_b, v_ret_norm_g, v_mix_w_out, v_ffn2_norm, v_ffn2_w_in, v_ffn2_w_out, v_final_norm):
    given = dict(locals())
    names = [n for n, _ in SMALL] + list(BIG)
    chip = 2 * lax.axis_index("x") + lax.axis_index("y")
    core = lax.axis_index("c")

    cw_rows = 128
    placed = lax.dynamic_update_slice(jnp.zeros((DEPTH, CONV_W, SLAB), F32), conv_w, (0, 0, chip * HEAD))
    placed = placed * (core == 0).astype(F32)
    conv_w_full = _unpack(_all_sum("gather_conv_w", _pack([placed], cw_rows)), [(DEPTH, CONV_W, SLAB)])[0]

    wt = dict(zip(BIG, _gather_weights([_cast_place("cast_" + n, given[n]) for n in BIG])))
    sm = {n: given[n] for n, _ in SMALL}
    sm["conv_w"] = conv_w_full
    loss, dx, big, small, dfinal = _local_step(x[0], loss_target[0], wt, sm)

    grads = [big[l][n] for n in BIG for l in range(DEPTH)]
    theirs = _pair_exchange(grads)
    sums = [_add_halves(f"chipsum{i}", a, b) for i, (a, b) in enumerate(zip(grads, theirs))]
    parts = _chip_exchange(sums)
    full = []
    for i in range(len(BIG)):
        f = None
        for l in range(DEPTH):
            f = _sum_parts(f"shardsum{DEPTH * i + l}", sums[DEPTH * i + l], parts[DEPTH * i + l], f, l, DEPTH)
        full.append(f)
    g_big = dict(zip(BIG, _pair_join(full)))

    small_parts = []
    for n, shp in SMALL:
        if n == "final_norm":
            small_parts.append(dfinal)
        else:
            small_parts.append(jnp.stack([small[l][n].reshape(shp[1:]) for l in range(DEPTH)]))
    g_small = dict(zip([n for n, _ in SMALL], _unpack(_all_sum("sum_small", _pack(small_parts, 200)), [s_ for _, s_ in SMALL])))
    g_small["conv_w"] = lax.dynamic_slice(g_small["conv_w"], (0, 0, chip * HEAD), (DEPTH, CONV_W, HEAD))

    grad, delta, new_m, new_v = dict(g_small), {}, {}, {}
    grad.update(g_big)
    for n in BIG:
        l, r, c = given[n].shape
        f = lambda t: t.reshape(l * r, c)
        d_, m_, v_ = _adamw("adamw_" + n, f(given[n]), f(grad[n]), f(given["m_" + n]), f(given["v_" + n]))
        delta[n], new_m[n], new_v[n] = d_.reshape(l, r, c), m_.reshape(l, r, c), v_.reshape(l, r, c)
    snames = [n for n, _ in SMALL]
    shapes = [given[n].shape for n in snames]
    rows = 104
    d_, m_, v_ = _adamw("adamw_small", _pack([given[n] for n in snames], rows), _pack([grad[n] for n in snames], rows),
                        _pack([given["m_" + n] for n in snames], rows), _pack([given["v_" + n] for n in snames], rows))
    for dst, packed in ((delta, d_), (new_m, m_), (new_v, v_)):
        dst.update(zip(snames, _unpack(packed, shapes)))

    total = lax.psum(loss[0, 0], ("x", "y", "c"))
    order = ["ffn1_norm", "ffn1_w_in", "ffn1_w_out", "mix_norm", "mix_w_in", "conv_w", "conv_b", "conv_ln_g",
             "conv_ln_b", "ret_norm_g", "mix_w_out", "ffn2_norm", "ffn2_w_in", "ffn2_w_out", "final_norm"]
    return (total, dx[None], *[grad[n] for n in order], *[delta[n] for n in order],
            *[new_m[n] for n in order], *[new_v[n] for n in order])
```

```python
import functools

import numpy as np
import jax
import jax.numpy as jnp
from jax import lax
from jax.experimental import pallas as pl
from jax.experimental.pallas import tpu as pltpu

F32 = jnp.float32
BF16 = jnp.bfloat16

D_MODEL = 1024
D_FF = 2816
N_SHARD = 4
FF_SHARD = 2 * D_FF // N_SHARD
MIX_SHARD = 3072 // N_SHARD
HEAD = 64
SLAB = 256
N_SLAB = 3072 // SLAB
CONV_W = 31
CONV_PAD = 32
CHUNK = 64
EPS = 1e-6
ROPE_BASE = 10000.0
DEPTH = 2

LR, B1, B2, ADAM_EPS, WD, STEP = 0.001, 0.9, 0.999, 1e-08, 0.01, 10

VMEM_LIMIT = 56 * 1024 * 1024


def _params(n_grid):
    return pltpu.CompilerParams(dimension_semantics=("arbitrary",) * n_grid, vmem_limit_bytes=VMEM_LIMIT)


def _rw(name, body, grid, ins, in_specs, rows=(), row_specs=(), accs=(), acc_specs=(), aliases=None):
    n_in, n_row = len(ins), len(rows)
    carried = sorted(aliases) if aliases else []

    def kern(*refs):
        vals = [r[...] for i, r in enumerate(refs[:n_in]) if i not in carried]
        row_vals, acc_vals = body(*vals)
        for r, v in zip(refs[n_in:n_in + n_row], row_vals):
            r[...] = v.astype(r.dtype)
        acc_refs = refs[n_in + n_row:]
        if acc_refs:
            first = functools.reduce(jnp.logical_and, [pl.program_id(a) == 0 for a in range(len(grid))])

            @pl.when(first)
            def _():
                for r in acc_refs:
                    r[...] = jnp.zeros(r.shape, r.dtype)

            for r, v in zip(acc_refs, acc_vals):
                r[...] += v.astype(r.dtype)

    return pl.pallas_call(
        kern, name=name, grid=grid, in_specs=list(in_specs), out_specs=list(row_specs) + list(acc_specs),
        out_shape=list(rows) + list(accs), input_output_aliases=dict(aliases or {}),
        compiler_params=_params(len(grid)))(*ins)


def _sds(shape, dtype):
    return jax.ShapeDtypeStruct(shape, dtype)


def _rms(x, g):
    return x * lax.rsqrt(jnp.mean(x * x, axis=-1, keepdims=True) + EPS) * g


def _row_spec(tm, c):
    return pl.BlockSpec((tm, c), lambda i: (i, 0))


def _vec_spec(c):
    return pl.BlockSpec((1, c), lambda i: (0, 0))


def _swiglu(gate, up):
    return jax.nn.silu(gate) * up


def _swiglu_fwd(name, u, tm):
    _, s, c = u.shape
    return _rw(name, lambda ub: ((_swiglu(ub[0:2].astype(F32), ub[2:4].astype(F32)),), ()), (s // tm,), [u],
               [pl.BlockSpec((4, tm, c), lambda i: (0, i, 0))],
               [_sds((2, s, c), BF16)], [pl.BlockSpec((2, tm, c), lambda i: (0, i, 0))])[0]


def _swiglu_bwd(name, u, da, tm):
    _, s, c = u.shape

    def body(ub, dab):
        _, vjp = jax.vjp(_swiglu, ub[0:2].astype(F32), ub[2:4].astype(F32))
        dg, du = vjp(dab.astype(F32))
        return (jnp.concatenate([dg, du], axis=0),), ()

    return _rw(name, body, (s // tm,), [u, da],
               [pl.BlockSpec((4, tm, c), lambda i: (0, i, 0)), pl.BlockSpec((2, tm, c), lambda i: (0, i, 0))],
               [_sds((4, s, c), BF16)], [pl.BlockSpec((4, tm, c), lambda i: (0, i, 0))])[0]


def _ln_silu(y, g, b):
    mu = jnp.mean(y, axis=-1, keepdims=True)
    yc = y - mu
    var = jnp.mean(yc * yc, axis=-1, keepdims=True)
    return jax.nn.silu(yc * lax.rsqrt(var + EPS) * g + b)


def _ln_silu_fwd(name, y, g, b, tm):
    s, c = y.shape
    return _rw(name, lambda yb, gb, bb: ((_ln_silu(yb, gb, bb),), ()), (s // tm,), [y, g, b],
               [_row_spec(tm, c), _vec_spec(c), _vec_spec(c)], [_sds((s, c), BF16)], [_row_spec(tm, c)])[0]


def _ln_silu_bwd(name, y, dcat, g, b, tm):
    s, c = y.shape

    def body(yb, dob, gb, bb):
        _, vjp = jax.vjp(_ln_silu, yb, gb, bb)
        dy, dg, db = vjp(dob)
        return (dy,), (dg, db)

    return _rw(name, body, (s // tm,), [y, dcat, g, b],
               [_row_spec(tm, c), pl.BlockSpec((None, tm, c), lambda i: (0, i, 0)), _vec_spec(c), _vec_spec(c)],
               [_sds((s, c), F32)], [_row_spec(tm, c)],
               [_sds((1, c), F32)] * 2, [_vec_spec(c)] * 2)


def _head_masks():
    lane = np.arange(SLAB) // HEAD
    m = np.zeros((8, SLAB), np.float32)
    for h in range(SLAB // HEAD):
        m[h] = (lane == h)
    return jnp.asarray(m)


def _gated_head_norm(y, gate, g, hm):
    mu = jnp.zeros_like(y)
    for h in range(SLAB // HEAD):
        mu = mu + hm[h:h + 1] * (jnp.sum(y * hm[h:h + 1], axis=-1, keepdims=True) / HEAD)
    yc = y - mu
    var = jnp.zeros_like(y)
    for h in range(SLAB // HEAD):
        var = var + hm[h:h + 1] * (jnp.sum(yc * yc * hm[h:h + 1], axis=-1, keepdims=True) / HEAD)
    return jax.nn.silu(gate) * (yc * lax.rsqrt(var + EPS) * g)


PER_SHARD = MIX_SHARD // SLAB


def _slab_spec(tm, j):
    return pl.BlockSpec((None, tm, SLAB), lambda i: (j, i, 0))


def _proj_slab_spec(tm, j):
    return pl.BlockSpec((None, tm, SLAB), lambda i: (j // PER_SHARD, i, j % PER_SHARD))


def _ghn_fwd(name, y, p32, g, tm):
    s, c = y.shape
    hm = _head_masks()
    return _rw(name, lambda yb, gb, wb, hb: ((_gated_head_norm(yb, gb, wb, hb),), ()), (s // tm,),
               [y, p32, g, hm],
               [_row_spec(tm, c), _proj_slab_spec(tm, 11), _vec_spec(c), pl.BlockSpec((8, c), lambda i: (0, 0))],
               [_sds((s, c), BF16)], [_row_spec(tm, c)])[0]


def _ghn_bwd(name, y, p32, dcat, g, tm):
    s, c = y.shape
    hm = _head_masks()

    def body(yb, gb, dob, wb, hb):
        _, vjp = jax.vjp(lambda a, b_, c_: _gated_head_norm(a, b_, c_, hb), yb, gb, wb)
        dy, dgate, dw = vjp(dob)
        return (dy, dgate), (dw,)

    return _rw(name, body, (s // tm,), [y, p32, dcat, g, hm],
               [_row_spec(tm, c), _proj_slab_spec(tm, 11), _slab_spec(tm, 3), _vec_spec(c),
                pl.BlockSpec((8, c), lambda i: (0, 0))],
               [_sds((s, c), F32)] * 2, [_row_spec(tm, c)] * 2,
               [_sds((1, c), F32)], [_vec_spec(c)])


def _assemble(name, parts, per, tm):
    s = parts[0][0].shape[-2]
    specs = [_row_spec(tm, SLAB) if j is None else pl.BlockSpec((None, tm, SLAB), lambda i, j=j: (j, i, 0))
             for _, j in parts]

    def body(*blocks):
        rows = [jnp.concatenate([b.astype(BF16) for b in blocks[per * q:per * (q + 1)]], axis=-1)
                for q in range(len(blocks) // per)]
        return (jnp.stack(rows),), ()

    nq = len(parts) // per
    return _rw(name, body, (s // tm,), [a for a, _ in parts], specs, [_sds((nq, s, per * SLAB), BF16)],
               [pl.BlockSpec((nq, tm, per * SLAB), lambda i: (0, i, 0))])[0]


def _final(name, x, tgt, g, tm):
    s, d = x.shape

    def body(xb, tb, gb):
        yf, vjp = jax.vjp(_rms, xb, gb)
        err = yf - tb
        dx, dg = vjp(err * (1.0 / d))
        part = 0.5 * jnp.sum(jnp.mean(err * err, axis=-1, keepdims=True), axis=0, keepdims=True)
        return (dx,), (dg, jnp.broadcast_to(part, (1, 128)))

    return _rw(name, body, (s // tm,), [x, tgt, g],
               [_row_spec(tm, d), _row_spec(tm, d), _vec_spec(d)],
               [_sds((s, d), F32)], [_row_spec(tm, d)],
               [_sds((1, d), F32), _sds((1, 128), F32)], [_vec_spec(d), _vec_spec(128)])


NN = (((1,), (0,)), ((), ()))
NT = (((1,), (1,)), ((), ()))
TN = (((0,), (0,)), ((), ()))


def _mm(name, a, b, grid, a_spec, b_spec, outs, out_specs, acc_shape, dims, alpha=1.0):
    nk = grid[-1]
    n_out = len(outs)

    def kern(*refs):
        a_ref, b_ref = refs[0], refs[1]
        o_refs = refs[2:2 + n_out]
        part = lax.dot_general(a_ref[...].astype(BF16), b_ref[...].astype(BF16), dims,
                               preferred_element_type=F32)

        def finish(r):
            if alpha != 1.0:
                r = r * alpha
            for o in o_refs:
                o[...] = r.astype(o.dtype)

        if nk == 1:
            finish(part)
            return
        acc_ref = refs[-1]
        k = pl.program_id(len(grid) - 1)

        @pl.when(k == 0)
        def _():
            acc_ref[...] = part

        @pl.when(jnp.logical_and(k > 0, k < nk - 1))
        def _():
            acc_ref[...] += part

        @pl.when(k == nk - 1)
        def _():
            finish(acc_ref[...] + part)

    return pl.pallas_call(
        kern, name=name, grid=grid, in_specs=[a_spec, b_spec], out_specs=list(out_specs), out_shape=list(outs),
        scratch_shapes=[pltpu.VMEM(acc_shape, F32)] if nk > 1 else [],
        compiler_params=_params(len(grid)))(a, b)


def _norm_proj_in(name, x, g, w, l, tm, dtypes):
    s, d = x.shape
    n = w.shape[-1]
    n_out = len(dtypes)

    def kern(x_ref, g_ref, w_ref, h_ref, *rest):
        o_refs, h_vmem = rest[:n_out], rest[n_out]

        @pl.when(pl.program_id(1) == 0)
        def _():
            h = _rms(x_ref[...], g_ref[...]).astype(BF16)
            h_vmem[...] = h
            h_ref[...] = h

        r = jnp.dot(h_vmem[...], w_ref[...], preferred_element_type=F32)
        for o in o_refs:
            o[...] = r.astype(o.dtype)

    out = pl.pallas_call(
        kern, name=name, grid=(s // tm, N_SHARD),
        in_specs=[pl.BlockSpec((tm, d), lambda i, b: (i, 0)), pl.BlockSpec((1, d), lambda i, b: (0, 0)),
                  pl.BlockSpec((None, None, d, n), lambda i, b: (l, b, 0, 0))],
        out_specs=[pl.BlockSpec((tm, d), lambda i, b: (i, 0))] +
                  [pl.BlockSpec((None, tm, n), lambda i, b: (b, i, 0))] * n_out,
        out_shape=[_sds((s, d), BF16)] + [_sds((N_SHARD, s, n), t) for t in dtypes],
        scratch_shapes=[pltpu.VMEM((tm, d), BF16)], compiler_params=_params(2))(x, g, w)
    return out[0], out[1:]


def _back_in_norm(name, du, w, l, tm, x, dres, g):
    nk, s, n = du.shape
    d = w.shape[2]

    def kern(du_ref, w_ref, x_ref, dres_ref, g_ref, dx_ref, dg_ref):
        dh = lax.dot_general(du_ref[0], w_ref[0], NT, preferred_element_type=F32)
        for k in range(1, nk):
            dh = dh + lax.dot_general(du_ref[k], w_ref[k], NT, preferred_element_type=F32)
        _, vjp = jax.vjp(_rms, x_ref[...], g_ref[...])
        dx, dg = vjp(dh)
        dx_ref[...] = dx + dres_ref[...]

        @pl.when(pl.program_id(0) == 0)
        def _():
            dg_ref[...] = dg

        @pl.when(pl.program_id(0) > 0)
        def _():
            dg_ref[...] += dg

    row = pl.BlockSpec((tm, d), lambda i: (i, 0))
    vec = pl.BlockSpec((1, d), lambda i: (0, 0))
    return pl.pallas_call(
        kern, name=name, grid=(s // tm,),
        in_specs=[pl.BlockSpec((nk, tm, n), lambda i: (0, i, 0)),
                  pl.BlockSpec((None, nk, d, n), lambda i: (l, 0, 0, 0)), row, row, vec],
        out_specs=[row, vec], out_shape=[_sds((s, d), F32), _sds((1, d), F32)],
        compiler_params=_params(1))(du, w, x, dres, g)


def _proj_out(name, a, w, l, res, alpha, tm):
    nk, s, r = a.shape
    d = w.shape[-1]

    def kern(a_ref, w_ref, res_ref, o_ref):
        y = jnp.dot(a_ref[0], w_ref[0], preferred_element_type=F32)
        for k in range(1, nk):
            y = y + jnp.dot(a_ref[k], w_ref[k], preferred_element_type=F32)
        o_ref[...] = res_ref[...] + (y * alpha if alpha != 1.0 else y)

    row = pl.BlockSpec((tm, d), lambda i: (i, 0))
    return pl.pallas_call(
        kern, name=name, grid=(s // tm,),
        in_specs=[pl.BlockSpec((nk, tm, r), lambda i: (0, i, 0)),
                  pl.BlockSpec((None, nk, r, d), lambda i: (l, 0, 0, 0)), row],
        out_specs=row, out_shape=_sds((s, d), F32), compiler_params=_params(1))(a, w, res)


def _back_out(name, dy, w, l, alpha, tm, out_dtype):
    s, d = dy.shape
    nk, r = w.shape[1], w.shape[2]
    return _mm(name, dy, w, (nk, s // tm, 1),
               pl.BlockSpec((tm, d), lambda b, i, k: (i, 0)),
               pl.BlockSpec((None, None, r, d), lambda b, i, k: (l, b, 0, 0)),
               [_sds((nk, s, r), out_dtype)], [pl.BlockSpec((None, tm, r), lambda b, i, k: (b, i, 0))],
               (tm, r), NT, alpha=alpha)[0]


def _grad_in(name, h, du, ts):
    s, d = h.shape
    nb, _, n = du.shape
    return _mm(name, h, du, (nb, 1, s // ts),
               pl.BlockSpec((ts, d), lambda b, i, k: (k, 0)),
               pl.BlockSpec((None, ts, n), lambda b, i, k: (b, k, 0)),
               [_sds((nb, d, n), BF16)], [pl.BlockSpec((None, d, n), lambda b, i, k: (b, 0, 0))],
               (d, n), TN)[0]


def _grad_out(name, a, dy, alpha, ts):
    nb, s, r = a.shape
    d = dy.shape[1]
    return _mm(name, a, dy, (nb, 1, s // ts),
               pl.BlockSpec((None, ts, r), lambda b, i, k: (b, k, 0)),
               pl.BlockSpec((ts, d), lambda b, i, k: (k, 0)),
               [_sds((nb, r, d), BF16)], [pl.BlockSpec((None, r, d), lambda b, i, k: (b, 0, 0))],
               (r, d), TN, alpha=alpha)[0]


CONV_TILE = 256


def _shifted(win, off, rows):
    n = win.shape[0]
    return pltpu.roll(win, (n - off) % n, 0)[0:rows] if off % n else win[0:rows]


def _conv_fwd(name, p32, w, bias):
    s = p32.shape[1]
    cb = 128
    nt = s // CONV_TILE

    def kern(a_ref, b_ref, w_ref, bias_ref, y_ref, vpad):
        vpad[0:CONV_PAD, :] = jnp.zeros((CONV_PAD, cb), F32)

        def fill(i, c):
            r = pl.multiple_of(i * CONV_TILE, CONV_TILE)
            vpad[pl.ds(CONV_PAD + r, CONV_TILE), :] = (
                a_ref[pl.ds(r, CONV_TILE), :] * jax.nn.sigmoid(b_ref[pl.ds(r, CONV_TILE), :]))
            return c

        lax.fori_loop(0, nt, fill, 0)

        def tile(i, c):
            r = pl.multiple_of(i * CONV_TILE, CONV_TILE)
            win = vpad[pl.ds(r, CONV_TILE + CONV_PAD), :]
            acc = jnp.broadcast_to(bias_ref[...], (CONV_TILE, cb))
            for j in range(CONV_W):
                acc = acc + w_ref[j:j + 1, :] * _shifted(win, j + 2, CONV_TILE)
            y_ref[pl.ds(r, CONV_TILE), :] = acc
            return c

        lax.fori_loop(0, nt, tile, 0)

    return pl.pallas_call(
        kern, name=name, grid=(SLAB // cb,),
        in_specs=[pl.BlockSpec((None, s, cb), lambda c: (0, 0, c)),
                  pl.BlockSpec((None, s, cb), lambda c: (0, 0, SLAB // cb + c)),
                  pl.BlockSpec((CONV_W, cb), lambda c: (0, c)),
                  pl.BlockSpec((1, cb), lambda c: (0, c))],
        out_specs=pl.BlockSpec((s, cb), lambda c: (0, c)),
        out_shape=_sds((s, SLAB), F32),
        scratch_shapes=[pltpu.VMEM((s + CONV_PAD, cb), F32)],
        compiler_params=_params(1))(p32, p32, w, bias)


def _conv_bwd(name, p32, w, dy):
    s = p32.shape[1]
    cb = 128
    nt = s // CONV_TILE

    def kern(a_ref, b_ref, w_ref, dy_ref, da_ref, db_ref, dw_ref, dbias_ref, vpad, dpad):
        vpad[0:CONV_PAD, :] = jnp.zeros((CONV_PAD, cb), F32)
        dpad[s:s + CONV_PAD, :] = jnp.zeros((CONV_PAD, cb), F32)
        dw_ref[...] = jnp.zeros((CONV_PAD, cb), F32)
        dbias_ref[...] = jnp.zeros((1, cb), F32)

        def fill(i, c):
            r = pl.multiple_of(i * CONV_TILE, CONV_TILE)
            vpad[pl.ds(CONV_PAD + r, CONV_TILE), :] = (
                a_ref[pl.ds(r, CONV_TILE), :] * jax.nn.sigmoid(b_ref[pl.ds(r, CONV_TILE), :]))
            dpad[pl.ds(r, CONV_TILE), :] = dy_ref[pl.ds(r, CONV_TILE), :]
            return c

        lax.fori_loop(0, nt, fill, 0)

        def tile(i, c):
            r = pl.multiple_of(i * CONV_TILE, CONV_TILE)
            dwin = dpad[pl.ds(r, CONV_TILE + CONV_PAD), :]
            vwin = vpad[pl.ds(r, CONV_TILE + CONV_PAD), :]
            dyt = dwin[0:CONV_TILE]
            dv = jnp.zeros((CONV_TILE, cb), F32)
            for j in range(CONV_W):
                dv = dv + w_ref[j:j + 1, :] * _shifted(dwin, CONV_W - 1 - j, CONV_TILE)
                dw_ref[j:j + 1, :] += jnp.sum(dyt * _shifted(vwin, j + 2, CONV_TILE), axis=0, keepdims=True)
            dbias_ref[...] += jnp.sum(dyt, axis=0, keepdims=True)
            a = a_ref[pl.ds(r, CONV_TILE), :]
            sg = jax.nn.sigmoid(b_ref[pl.ds(r, CONV_TILE), :])
            da_ref[pl.ds(r, CONV_TILE), :] = dv * sg
            db_ref[pl.ds(r, CONV_TILE), :] = dv * a * sg * (1.0 - sg)
            return c

        lax.fori_loop(0, nt, tile, 0)

    col = pl.BlockSpec((s, cb), lambda c: (0, c))
    return pl.pallas_call(
        kern, name=name, grid=(SLAB // cb,),
        in_specs=[pl.BlockSpec((None, s, cb), lambda c: (0, 0, c)),
                  pl.BlockSpec((None, s, cb), lambda c: (0, 0, SLAB // cb + c)),
                  pl.BlockSpec((CONV_W, cb), lambda c: (0, c)), col],
        out_specs=[col, col, pl.BlockSpec((CONV_PAD, cb), lambda c: (0, c)), pl.BlockSpec((1, cb), lambda c: (0, c))],
        out_shape=[_sds((s, SLAB), F32), _sds((s, SLAB), F32), _sds((CONV_PAD, SLAB), F32), _sds((1, SLAB), F32)],
        scratch_shapes=[pltpu.VMEM((s + CONV_PAD, cb), F32), pltpu.VMEM((s + CONV_PAD, cb), F32)],
        compiler_params=_params(1))(p32, p32, w, dy)


SB_BLOCK = 256
N_HEAD = SLAB // HEAD


def _sb_logits(qm, k, tri):
    z = lax.dot_general(qm, k, NT, preferred_element_type=F32)
    sign_bit = jnp.uint32(0x80000000)
    neg_abs = lax.bitcast_convert_type(lax.bitcast_convert_type(z, jnp.uint32) | sign_bit, F32)
    lb = jnp.minimum(z, 0.0) - jnp.log(1.0 + jnp.exp(neg_abs))
    ln = lb - z
    if tri is not None:
        ln = jnp.where(tri, ln, 0.0)
    return lb, ln


def _first_col(x):
    return jnp.broadcast_to(x[:, 0:1], (x.shape[0], 128))


def _head_stack(dst, x, lane_head, bq):
    for h in range(N_HEAD):
        dst[h * bq:(h + 1) * bq, :] = jnp.where(lane_head == h, x, jnp.zeros_like(x))


def _sb_fwd(name, p16):
    s = p16.shape[1]
    bq = min(SB_BLOCK, s)
    nq = s // bq

    def kern(q_ref, k_ref, v_ref, o_ref, w_hbm, lb_hbm, qm_ref, v4_ref, w4_ref, ws_ref, lbs_ref, acc_ref, r_ref, sem):
        g, qi = pl.program_id(0), pl.program_id(1)
        lane_head = lax.broadcasted_iota(jnp.int32, (1, SLAB), 1) // HEAD
        _head_stack(qm_ref, (q_ref[...].astype(F32) * (HEAD ** -0.5)).astype(BF16), lane_head, bq)
        row = lax.broadcasted_iota(jnp.int32, (bq, bq), 0)
        col = lax.broadcasted_iota(jnp.int32, (bq, bq), 1)
        after = (row > col).astype(BF16)
        tri = col < row
        acc_ref[...] = jnp.zeros((bq, SLAB), F32)
        r_ref[...] = jnp.zeros((N_HEAD, bq, 128), F32)

        def saves(slot, kb):
            return (pltpu.make_async_copy(ws_ref.at[slot], w_hbm.at[g, qi, kb], sem.at[0, slot]),
                    pltpu.make_async_copy(lbs_ref.at[slot], lb_hbm.at[g, qi, kb], sem.at[1, slot]))

        def tile(i, masked):
            kb, slot = qi - i, i % 2
            rows = pl.ds(pl.multiple_of(kb * bq, bq), bq)
            k = k_ref[rows, :]
            _head_stack(v4_ref, v_ref[rows, :], lane_head, bq)
            for h in range(N_HEAD):
                mine = pl.ds(h * bq, bq)
                lb, ln = _sb_logits(qm_ref[h * bq:(h + 1) * bq, :], k, tri if masked else None)
                rem = jnp.dot(ln.astype(BF16), after, preferred_element_type=F32)
                w = jnp.exp(lb + rem + r_ref[h][:, 0:1])
                if masked:
                    w = jnp.where(tri, w, 0.0)
                wb = w.astype(BF16)
                w4_ref[:, h * bq:(h + 1) * bq] = wb
                ws_ref[slot, mine, :] = wb
                lbs_ref[slot, mine, :] = lb.astype(BF16)
                r_ref[h] += _first_col(rem[:, 0:128] + ln[:, 0:128])
            acc_ref[...] += jnp.dot(w4_ref[...], v4_ref[...], preferred_element_type=F32)
            if not masked:
                for cp in saves(1 - slot, kb):
                    cp.wait()
            for cp in saves(slot, kb):
                cp.start()

        tile(0, True)

        def step(i, c):
            tile(i, False)
            return c

        lax.fori_loop(1, qi + 1, step, 0)
        o_ref[...] = acc_ref[...]
        for cp in saves(qi % 2, 0):
            cp.wait()

    saved = _sds((2, nq, nq, N_HEAD * bq, bq), BF16)
    return pl.pallas_call(
        kern, name=name, grid=(2, nq),
        in_specs=[pl.BlockSpec((None, bq, SLAB), lambda g, i: ((2 + g) // PER_SHARD, i, (2 + g) % PER_SHARD)),
                  pl.BlockSpec((None, s, SLAB), lambda g, i: ((4 + g) // PER_SHARD, 0, (4 + g) % PER_SHARD)),
                  pl.BlockSpec((None, s, SLAB), lambda g, i: ((6 + g) // PER_SHARD, 0, (6 + g) % PER_SHARD))],
        out_specs=[pl.BlockSpec((None, bq, SLAB), lambda g, i: (g, i, 0)),
                   pl.BlockSpec(memory_space=pl.ANY), pl.BlockSpec(memory_space=pl.ANY)],
        out_shape=[_sds((2, s, SLAB), F32), saved, saved],
        scratch_shapes=[pltpu.VMEM((N_HEAD * bq, SLAB), BF16), pltpu.VMEM((N_HEAD * bq, SLAB), BF16),
                        pltpu.VMEM((bq, N_HEAD * bq), BF16), pltpu.VMEM((2, N_HEAD * bq, bq), BF16),
                        pltpu.VMEM((2, N_HEAD * bq, bq), BF16), pltpu.VMEM((bq, SLAB), F32),
                        pltpu.VMEM((N_HEAD, bq, 128), F32), pltpu.SemaphoreType.DMA((2, 2))],
        compiler_params=_params(2))(p16, p16, p16)


def _sb_bwd(name, p16, w_saved, lb_saved, dcat):
    s = p16.shape[1]
    bq = min(SB_BLOCK, s)
    nq = s // bq

    def kern(q_ref, k_ref, v_ref, do_ref, w_hbm, lb_hbm, dq_ref, dk_hbm, dv_hbm, dk_acc, dv_acc, dq_acc,
             qm_ref, dom_ref, k4_ref, dzc_ref, dzs_ref, ws_ref, lbs_ref, c_ref, sem, lsem):
        g, qi = pl.program_id(0), pl.program_id(1)

        @pl.when(qi == 0)
        def _():
            dk_acc[...] = jnp.zeros((s, SLAB), F32)
            dv_acc[...] = jnp.zeros((s, SLAB), F32)

        lane_head = lax.broadcasted_iota(jnp.int32, (1, SLAB), 1) // HEAD
        _head_stack(qm_ref, (q_ref[...].astype(F32) * (HEAD ** -0.5)).astype(BF16), lane_head, bq)
        _head_stack(dom_ref, do_ref[...].astype(BF16), lane_head, bq)
        row = lax.broadcasted_iota(jnp.int32, (bq, bq), 0)
        col = lax.broadcasted_iota(jnp.int32, (bq, bq), 1)
        earlier = (row < col).astype(BF16)
        tri = col < row
        dq_acc[...] = jnp.zeros((bq, SLAB), F32)
        c_ref[...] = jnp.zeros((N_HEAD, bq, 128), F32)

        def loads(kb):
            slot, kb = kb % 2, jnp.minimum(kb, qi)
            return (pltpu.make_async_copy(w_hbm.at[g, qi, kb], ws_ref.at[slot], lsem.at[0, slot]),
                    pltpu.make_async_copy(lb_hbm.at[g, qi, kb], lbs_ref.at[slot], lsem.at[1, slot]))

        def tile(kb, masked):
            slot = kb % 2
            for cp in loads(kb + 1):
                cp.start()
            for cp in loads(kb):
                cp.wait()
            rows = pl.ds(pl.multiple_of(kb * bq, bq), bq)
            k, v = k_ref[rows, :], v_ref[rows, :]
            _head_stack(k4_ref, k, lane_head, bq)
            for h in range(N_HEAD):
                mine = slice(h * bq, (h + 1) * bq)
                wb = ws_ref[slot, pl.ds(h * bq, bq), :]
                dl = wb.astype(F32) * lax.dot_general(dom_ref[mine, :], v, NT, preferred_element_type=F32)
                prefix = jnp.dot(dl.astype(BF16), earlier, preferred_element_type=F32)
                before = prefix + c_ref[h][:, 0:1]
                sig = jnp.exp(lbs_ref[slot, pl.ds(h * bq, bq), :].astype(F32))
                dz = dl - sig * (dl + before)
                if masked:
                    dz = jnp.where(tri, dz, 0.0)
                dzb = dz.astype(BF16)
                dzc_ref[:, mine] = dzb
                dzs_ref[mine, :] = dzb
                tail = prefix[:, bq - 128:] + dl[:, bq - 128:]
                c_ref[h] += jnp.broadcast_to(tail[:, 127:128], (bq, 128))
            dq_acc[...] += jnp.dot(dzc_ref[...], k4_ref[...], preferred_element_type=F32)
            dk_acc[rows, :] += lax.dot_general(dzs_ref[...], qm_ref[...], TN, preferred_element_type=F32)
            dv_acc[rows, :] += lax.dot_general(ws_ref[slot], dom_ref[...], TN, preferred_element_type=F32)

        for cp in loads(0):
            cp.start()

        def step(kb, c):
            tile(kb, False)
            return c

        lax.fori_loop(0, qi, step, 0)
        tile(qi, True)
        for cp in loads(qi + 1):
            cp.wait()
        dq_ref[...] = dq_acc[...] * (HEAD ** -0.5)

        @pl.when(qi == nq - 1)
        def _():
            ck = pltpu.make_async_copy(dk_acc, dk_hbm.at[g], sem.at[0])
            cv = pltpu.make_async_copy(dv_acc, dv_hbm.at[g], sem.at[1])
            ck.start()
            cv.start()
            ck.wait()
            cv.wait()

    blk = lambda j0: pl.BlockSpec((None, bq, SLAB), lambda g, i: (j0 + g, i, 0))
    full = lambda j0: pl.BlockSpec((None, s, SLAB), lambda g, i: ((j0 + g) // PER_SHARD, 0, (j0 + g) % PER_SHARD))
    q_blk = pl.BlockSpec((None, bq, SLAB), lambda g, i: ((2 + g) // PER_SHARD, i, (2 + g) % PER_SHARD))
    stack16 = pltpu.VMEM((N_HEAD * bq, SLAB), BF16)
    return pl.pallas_call(
        kern, name=name, grid=(2, nq),
        in_specs=[q_blk, full(4), full(6), blk(1),
                  pl.BlockSpec(memory_space=pl.ANY), pl.BlockSpec(memory_space=pl.ANY)],
        out_specs=[blk(0), pl.BlockSpec(memory_space=pl.ANY), pl.BlockSpec(memory_space=pl.ANY)],
        out_shape=[_sds((2, s, SLAB), F32)] * 3,
        scratch_shapes=[pltpu.VMEM((s, SLAB), F32), pltpu.VMEM((s, SLAB), F32), pltpu.VMEM((bq, SLAB), F32),
                        stack16, stack16, stack16, pltpu.VMEM((bq, N_HEAD * bq), BF16),
                        pltpu.VMEM((N_HEAD * bq, bq), BF16), pltpu.VMEM((2, N_HEAD * bq, bq), BF16),
                        pltpu.VMEM((2, N_HEAD * bq, bq), BF16), pltpu.VMEM((N_HEAD, bq, 128), F32),
                        pltpu.SemaphoreType.DMA((2,)), pltpu.SemaphoreType.DMA((2, 2))],
        compiler_params=_params(2))(p16, p16, p16, dcat, w_saved, lb_saved)


RET_BLOCK = 256


def _ret_tables(s, bl):
    nh = SLAB // HEAD
    lane_h = np.arange(SLAB) // HEAD
    log_gamma = np.log1p(-np.exp2(-5.0 - np.arange(nh, dtype=np.float64)))
    lg_lane = log_gamma[lane_h]
    half = HEAD // 2
    inv = 1.0 / (ROPE_BASE ** (np.arange(half, dtype=np.float64) / half))
    ang = np.arange(s, dtype=np.float64)[:, None] * inv[None, :]
    within = np.arange(SLAB) % HEAD
    cos = np.cos(ang)[:, within % half]
    sin = np.sin(ang)[:, within % half] * np.where(within < half, -1.0, 1.0)[None, :]
    perm = np.zeros((SLAB, SLAB))
    partner = np.where(within < half, np.arange(SLAB) + half, np.arange(SLAB) - half)
    perm[partner, np.arange(SLAB)] = 1.0
    i = np.arange(bl)
    diff = i[:, None] - i[None, :]
    same = (i[:, None] // CHUNK) == (i[None, :] // CHUNK)
    earlier = (i[None, :] // CHUNK) < (i[:, None] // CHUNK)
    decay = np.zeros((nh, bl, bl))
    for h in range(nh):
        decay[h] = np.where(same, np.exp(log_gamma[h] * np.abs(diff)),
                            np.where(earlier, np.exp(log_gamma[h] * diff), 0.0))
    qd = np.exp(lg_lane[None, :] * (i[:, None] + 1.0))
    kd = np.exp(lg_lane[None, :] * (bl - 1.0 - i[:, None]))
    gam = np.exp(lg_lane * bl)[:, None] * np.ones((1, SLAB))
    bd = (lane_h[:, None] == lane_h[None, :]).astype(np.float64)
    f = lambda a: jnp.asarray(a, F32)
    return f(cos), f(sin), f(perm), f(decay), f(qd), f(kd), f(gam), f(bd)


def _ret_block(q, k, v, state, cos, sin, perm, decay, qd, kd, gam, bd, hm):
    qr = (q * cos + jnp.dot(q, perm, preferred_element_type=F32) * sin) * (HEAD ** -0.5)
    kr = k * cos + jnp.dot(k, perm, preferred_element_type=F32) * sin
    y = jnp.dot(qr * qd, state, preferred_element_type=F32)
    for h in range(SLAB // HEAD):
        m = hm[h:h + 1]
        sc = lax.dot_general(qr * m, kr, NT, preferred_element_type=F32) * decay[h]
        y = y + jnp.dot(sc, v * m, preferred_element_type=F32)
    new_state = gam * state + lax.dot_general(kr * kd, v, TN, preferred_element_type=F32) * bd
    return y, new_state


def _ret_specs(s, bl, rev):
    nb = s // bl
    pos = (lambda n: nb - 1 - n) if rev else (lambda n: n)
    slab = lambda j: pl.BlockSpec((None, bl, SLAB), lambda n: (j // PER_SHARD, pos(n), j % PER_SHARD))
    const2 = lambda r: pl.BlockSpec((r, SLAB), lambda n: (0, 0))
    tab = [pl.BlockSpec((bl, SLAB), lambda n: (pos(n), 0))] * 2 + [
        const2(SLAB), pl.BlockSpec((SLAB // HEAD, bl, bl), lambda n: (0, 0, 0)),
        const2(bl), const2(bl), const2(SLAB), const2(SLAB), const2(8)]
    return nb, pos, slab, tab


def _ret_fwd(name, p32):
    s = p32.shape[1]
    bl = min(RET_BLOCK, s)
    nb, pos, slab, tab = _ret_specs(s, bl, False)
    tables = _ret_tables(s, bl) + (_head_masks(),)

    def kern(q_ref, k_ref, v_ref, *rest):
        t_refs, (y_ref, st_ref, state) = rest[:9], rest[9:]

        @pl.when(pl.program_id(0) == 0)
        def _():
            state[...] = jnp.zeros((SLAB, SLAB), F32)

        st_ref[...] = state[...]
        y, new = _ret_block(q_ref[...], k_ref[...], v_ref[...], state[...], *[t[...] for t in t_refs])
        y_ref[...] = y
        state[...] = new

    return pl.pallas_call(
        kern, name=name, grid=(nb,), in_specs=[slab(8), slab(9), slab(10)] + tab,
        out_specs=[pl.BlockSpec((bl, SLAB), lambda n: (n, 0)), pl.BlockSpec((None, SLAB, SLAB), lambda n: (n, 0, 0))],
        out_shape=[_sds((s, SLAB), F32), _sds((nb, SLAB, SLAB), F32)],
        scratch_shapes=[pltpu.VMEM((SLAB, SLAB), F32)], compiler_params=_params(1))(p32, p32, p32, *tables)


def _ret_bwd(name, p32, states, dy):
    s = p32.shape[1]
    bl = min(RET_BLOCK, s)
    nb, pos, slab, tab = _ret_specs(s, bl, True)
    tables = _ret_tables(s, bl) + (_head_masks(),)
    rowblk = pl.BlockSpec((bl, SLAB), lambda n: (pos(n), 0))

    def kern(q_ref, k_ref, v_ref, st_ref, dy_ref, *rest):
        t_refs, (dq_ref, dk_ref, dv_ref, dstate) = rest[:9], rest[9:]

        @pl.when(pl.program_id(0) == 0)
        def _():
            dstate[...] = jnp.zeros((SLAB, SLAB), F32)

        tv = [t[...] for t in t_refs]
        _, vjp = jax.vjp(lambda a, b, c, d: _ret_block(a, b, c, d, *tv),
                         q_ref[...], k_ref[...], v_ref[...], st_ref[...])
        dq, dk, dv, ds = vjp((dy_ref[...], dstate[...]))
        dq_ref[...] = dq
        dk_ref[...] = dk
        dv_ref[...] = dv
        dstate[...] = ds

    return pl.pallas_call(
        kern, name=name, grid=(nb,),
        in_specs=[slab(8), slab(9), slab(10), pl.BlockSpec((None, SLAB, SLAB), lambda n: (pos(n), 0, 0)), rowblk] + tab,
        out_specs=[rowblk] * 3, out_shape=[_sds((s, SLAB), F32)] * 3,
        scratch_shapes=[pltpu.VMEM((SLAB, SLAB), F32)], compiler_params=_params(1))(p32, p32, p32, states, dy, *tables)


TM_FFN = 1024
TM_SLAB = 2048
TM_NORM = 256
TM_OUT = 512
TM_GRAD = 2048
TM_RW = 256
TM_FF = 256


def _ffn_fwd(tag, x, g, w_in, w_out, l):
    tm = min(TM_FFN, x.shape[0])
    h, (u,) = _norm_proj_in(tag + "_in", x, g, w_in, l, tm, [BF16])
    a = _swiglu_fwd(tag + "_act", u, TM_FF)
    w_out2 = w_out.reshape(DEPTH, 2, FF_SHARD, D_MODEL)
    xn = _proj_out(tag + "_out", a, w_out2, l, x, 0.5, min(TM_OUT, x.shape[0]))
    return xn, (x, h, u, a)


def _ffn_bwd(tag, saved, dxn, g, w_in, w_out, l):
    x, h, u, a = saved
    tm = min(TM_FFN, x.shape[0])
    w_out2 = w_out.reshape(DEPTH, 2, FF_SHARD, D_MODEL)
    da = _back_out(tag + "_dact", dxn, w_out2, l, 0.5, tm, BF16)
    tk = min(TM_GRAD, x.shape[0])
    dw_out = _grad_out(tag + "_dwout", a, dxn, 0.5, tk)
    du = _swiglu_bwd(tag + "_dswi", u, da, TM_FF)
    dx, dg = _back_in_norm(tag + "_dh", du, w_in, l, min(TM_NORM, x.shape[0]), x, dxn, g)
    dw_in = _grad_in(tag + "_dwin", h, du, tk)
    return dx, dg, dw_in, dw_out.reshape(N_SHARD, D_FF // N_SHARD, D_MODEL)


def _mix_fwd(tag, x, sm, w_in, w_out, l):
    h, (p32, p16) = _norm_proj_in(tag + "_in", x, sm["mix_norm"][l:l + 1], w_in, l, min(TM_FFN, x.shape[0]),
                                  [F32, BF16])
    ypre = _conv_fwd(tag + "_conv", p32, sm["conv_w"][l], sm["conv_b"][l:l + 1])
    yconv = _ln_silu_fwd(tag + "_ln", ypre, sm["conv_ln_g"][l:l + 1], sm["conv_ln_b"][l:l + 1], TM_RW)
    osb, w_sb, lb_sb = _sb_fwd(tag + "_sb", p16)
    yr, states = _ret_fwd(tag + "_ret", p32)
    yret = _ghn_fwd(tag + "_ghn", yr, p32, sm["ret_norm_g"][l:l + 1], TM_RW)
    ycat = _assemble(tag + "_cat", [(yconv, None), (osb, 0), (osb, 1), (yret, None)], 1, TM_RW)
    xn = _proj_out(tag + "_out", ycat, w_out, l, x, 1.0, min(TM_OUT, x.shape[0]))
    return xn, (x, h, p32, p16, ypre, (osb, w_sb, lb_sb), yr, states, ycat)


def _mix_bwd(tag, saved, dxn, sm, w_in, w_out, l):
    x, h, p32, p16, ypre, osb, yr, states, ycat = saved
    ts = min(TM_SLAB, x.shape[0])
    dcat = _back_out(tag + "_dcat", dxn, w_out, l, 1.0, ts, F32)
    dw_out = _grad_out(tag + "_dwout", ycat, dxn, 1.0, ts)
    dypre, dlg, dlb = _ln_silu_bwd(tag + "_dln", ypre, dcat, sm["conv_ln_g"][l:l + 1], sm["conv_ln_b"][l:l + 1], TM_RW)
    da, db, dcw, dcb = _conv_bwd(tag + "_dconv", p32, sm["conv_w"][l], dypre)
    dq, dk, dv = _sb_bwd(tag + "_dsb", p16, osb[1], osb[2], dcat)
    dyr, dgate, drg = _ghn_bwd(tag + "_dghn", yr, p32, dcat, sm["ret_norm_g"][l:l + 1], TM_RW)
    dqr, dkr, dvr = _ret_bwd(tag + "_dret", p32, states, dyr)
    dp = _assemble(tag + "_dp", [(da, None), (db, None), (dq, 0), (dq, 1), (dk, 0), (dk, 1), (dv, 0), (dv, 1),
                                 (dqr, None), (dkr, None), (dvr, None), (dgate, None)], PER_SHARD, TM_RW)
    dx, dg = _back_in_norm(tag + "_dh", dp, w_in, l, min(TM_NORM, x.shape[0]), x, dxn, sm["mix_norm"][l:l + 1])
    dw_in = _grad_in(tag + "_dwin", h, dp, min(TM_GRAD, x.shape[0]))
    small = dict(mix_norm=dg, conv_w=dcw[0:CONV_W], conv_b=dcb, conv_ln_g=dlg, conv_ln_b=dlb, ret_norm_g=drg)
    return dx, small, dw_in, dw_out


def _local_step(x, tgt, wt, sm):
    saved = []
    for l in range(DEPTH):
        x, s1 = _ffn_fwd(f"l{l}f1", x, sm["ffn1_norm"][l:l + 1], wt["ffn1_w_in"], wt["ffn1_w_out"], l)
        x, s2 = _mix_fwd(f"l{l}mx", x, sm, wt["mix_w_in"], wt["mix_w_out"], l)
        x, s3 = _ffn_fwd(f"l{l}f2", x, sm["ffn2_norm"][l:l + 1], wt["ffn2_w_in"], wt["ffn2_w_out"], l)
        saved.append((s1, s2, s3))
    dx, dfinal, loss = _final("final", x, tgt, sm["final_norm"][None, :], TM_RW)
    big = [None] * DEPTH
    small = [None] * DEPTH
    for l in reversed(range(DEPTH)):
        s1, s2, s3 = saved[l]
        dx, dg3, dwi3, dwo3 = _ffn_bwd(f"l{l}f2", s3, dx, sm["ffn2_norm"][l:l + 1], wt["ffn2_w_in"], wt["ffn2_w_out"], l)
        dx, sml, dwi2, dwo2 = _mix_bwd(f"l{l}mx", s2, dx, sm, wt["mix_w_in"], wt["mix_w_out"], l)
        dx, dg1, dwi1, dwo1 = _ffn_bwd(f"l{l}f1", s1, dx, sm["ffn1_norm"][l:l + 1], wt["ffn1_w_in"], wt["ffn1_w_out"], l)
        big[l] = dict(ffn1_w_in=dwi1, ffn1_w_out=dwo1, mix_w_in=dwi2, mix_w_out=dwo2, ffn2_w_in=dwi3, ffn2_w_out=dwo3)
        sml.update(ffn1_norm=dg1, ffn2_norm=dg3)
        small[l] = sml
    return loss, dx, big, small, dfinal


MESH = pl.DeviceIdType.MESH
ANY = pl.BlockSpec(memory_space=pl.ANY)
BIG = ("ffn1_w_in", "ffn1_w_out", "mix_w_in", "mix_w_out", "ffn2_w_in", "ffn2_w_out")


def _place():
    x, y, c = lax.axis_index("x"), lax.axis_index("y"), lax.axis_index("c")
    chips = [(1 - x, y), (x, 1 - y), (1 - x, 1 - y)]
    return x, y, c, chips


def _gather_weights(w16):
    n = len(w16)

    def kern(*refs):
        dst = refs[n:2 * n]
        send, recv = refs[2 * n:]
        x, y, c, chips = _place()
        mine = 2 * x + y
        firsts, passes = [], []
        for a in range(n):
            h = dst[a].shape[2] // 2
            own = dst[a].at[:, mine, pl.ds(c * h, h)]
            for j, (cx, cy) in enumerate(chips):
                cp = pltpu.make_async_remote_copy(
                    src_ref=own, dst_ref=own, send_sem=send.at[6 * a + j], recv_sem=recv.at[6 * a + j],
                    device_id=(cx, cy, c), device_id_type=MESH)
                cp.start()
                firsts.append(cp)
        for a in range(n):
            h = dst[a].shape[2] // 2
            half = pl.ds(c * h, h)
            for j, (cx, cy) in enumerate(chips):
                theirs = dst[a].at[:, 2 * cx + cy, half]
                pltpu.make_async_remote_copy(
                    src_ref=theirs, dst_ref=theirs, send_sem=send.at[6 * a + j], recv_sem=recv.at[6 * a + j],
                    device_id=(cx, cy, c), device_id_type=MESH).wait_recv()
                fw = pltpu.make_async_remote_copy(
                    src_ref=theirs, dst_ref=theirs, send_sem=send.at[6 * a + 3 + j], recv_sem=recv.at[6 * a + 3 + j],
                    device_id=(x, y, 1 - c), device_id_type=MESH)
                fw.start()
                passes.append(fw)
        for a in range(n):
            h = dst[a].shape[2] // 2
            other = pl.ds((1 - c) * h, h)
            for j, (cx, cy) in enumerate(chips):
                got = dst[a].at[:, 2 * cx + cy, other]
                pltpu.make_async_remote_copy(
                    src_ref=got, dst_ref=got, send_sem=send.at[6 * a + 3 + j], recv_sem=recv.at[6 * a + 3 + j],
                    device_id=(x, y, 1 - c), device_id_type=MESH).wait_recv()
        for cp in firsts + passes:
            cp.wait_send()

    return pl.pallas_call(
        kern, name="gather_weights", in_specs=[ANY] * n, out_specs=[ANY] * n,
        out_shape=[_sds(w.shape, w.dtype) for w in w16], input_output_aliases={a: a for a in range(n)},
        scratch_shapes=[pltpu.SemaphoreType.DMA((6 * n,)), pltpu.SemaphoreType.DMA((6 * n,))])(*w16)


def _pair_exchange(grads):
    n = len(grads)

    def kern(*refs):
        src, got_o = refs[:n], refs[n:2 * n]
        send, recv = refs[2 * n:]
        x, y, c, _ = _place()
        cps = []
        for a in range(n):
            h = src[a].shape[1] // 2
            cp = pltpu.make_async_remote_copy(
                src_ref=src[a].at[:, pl.ds((1 - c) * h, h)], dst_ref=got_o[a],
                send_sem=send.at[a], recv_sem=recv.at[a], device_id=(x, y, 1 - c), device_id_type=MESH)
            cp.start()
            cps.append(cp)
        for cp in cps:
            cp.wait()

    halves = [_sds((g.shape[0], g.shape[1] // 2, g.shape[2]), g.dtype) for g in grads]
    return pl.pallas_call(
        kern, name="pair_exchange", in_specs=[ANY] * n, out_specs=[ANY] * n, out_shape=halves,
        scratch_shapes=[pltpu.SemaphoreType.DMA((n,)), pltpu.SemaphoreType.DMA((n,))])(*grads)


def _chip_exchange(sums):
    n = len(sums)

    def kern(*refs):
        src, dst = refs[:n], refs[n:2 * n]
        send, recv = refs[2 * n:]
        x, y, c, chips = _place()
        cps = []
        for a in range(n):
            for j, (cx, cy) in enumerate(chips):
                cp = pltpu.make_async_remote_copy(
                    src_ref=src[a].at[2 * cx + cy], dst_ref=dst[a].at[j],
                    send_sem=send.at[3 * a + j], recv_sem=recv.at[3 * a + j],
                    device_id=(cx, cy, c), device_id_type=MESH)
                cp.start()
                cps.append(cp)
        for cp in cps:
            cp.wait()

    return pl.pallas_call(
        kern, name="chip_exchange", in_specs=[ANY] * n, out_specs=[ANY] * n,
        out_shape=[_sds((3,) + s_.shape[1:], s_.dtype) for s_ in sums],
        scratch_shapes=[pltpu.SemaphoreType.DMA((3 * n,)), pltpu.SemaphoreType.DMA((3 * n,))])(*sums)


def _pair_join(full):
    n = len(full)

    def kern(*refs):
        dst = refs[n:2 * n]
        send, recv = refs[2 * n:]
        x, y, c, _ = _place()
        cps = []
        for a in range(n):
            h = dst[a].shape[1] // 2
            mine = dst[a].at[:, pl.ds(c * h, h)]
            cp = pltpu.make_async_remote_copy(
                src_ref=mine, dst_ref=mine, send_sem=send.at[a], recv_sem=recv.at[a],
                device_id=(x, y, 1 - c), device_id_type=MESH)
            cp.start()
            cps.append(cp)
        for a, cp in enumerate(cps):
            cp.wait_send()
            h = dst[a].shape[1] // 2
            got = dst[a].at[:, pl.ds((1 - c) * h, h)]
            pltpu.make_async_remote_copy(
                src_ref=got, dst_ref=got, send_sem=send.at[a], recv_sem=recv.at[a],
                device_id=(x, y, 1 - c), device_id_type=MESH).wait_recv()

    return pl.pallas_call(
        kern, name="pair_join", in_specs=[ANY] * n, out_specs=[ANY] * n,
        out_shape=[_sds(f.shape, f.dtype) for f in full], input_output_aliases={a: a for a in range(n)},
        scratch_shapes=[pltpu.SemaphoreType.DMA((n,)), pltpu.SemaphoreType.DMA((n,))])(*full)


def _all_sum(name, v):
    r = v.shape[0]

    def kern(v_ref, o_ref, buf, send, recv):
        x, y, c, _ = _place()
        me = 4 * x + 2 * y + c
        buf[me] = v_ref[...]
        cps = []
        for k in range(1, 8):
            peer = (x ^ (k >> 2), y ^ ((k >> 1) & 1), c ^ (k & 1))
            cp = pltpu.make_async_remote_copy(
                src_ref=v_ref, dst_ref=buf.at[me], send_sem=send.at[k - 1], recv_sem=recv.at[k - 1],
                device_id=peer, device_id_type=MESH)
            cp.start()
            cps.append(cp)
        for k in range(1, 8):
            peer_id = me ^ k
            pltpu.make_async_remote_copy(
                src_ref=v_ref, dst_ref=buf.at[peer_id], send_sem=send.at[k - 1], recv_sem=recv.at[k - 1],
                device_id=(x, y, c), device_id_type=MESH).wait_recv()
        for cp in cps:
            cp.wait_send()
        acc = buf[0]
        for d in range(1, 8):
            acc = acc + buf[d]
        o_ref[...] = acc

    vm = pl.BlockSpec(memory_space=pltpu.VMEM)
    return pl.pallas_call(
        kern, name=name, in_specs=[vm], out_specs=vm, out_shape=_sds((r, 128), F32),
        scratch_shapes=[pltpu.VMEM((8, r, 128), F32), pltpu.SemaphoreType.DMA((7,)),
                        pltpu.SemaphoreType.DMA((7,))])(v)


def _my_chip():
    return 2 * lax.axis_index("x") + lax.axis_index("y")


def _my_core():
    return lax.axis_index("c")


def _cast_place(name, w):
    l, r, c = w.shape
    tr = r // 4
    return _rw(name, lambda wb: ((wb,), ()), (l, r // tr), [w],
               [pl.BlockSpec((None, tr, c), lambda j, i: (j, i, 0))],
               [_sds((l, N_SHARD, r, c), BF16)],
               [pl.BlockSpec((None, None, tr, c), lambda j, i: (j, _my_chip(), i, 0))])[0]


HALF_STEPS = 2


def _add_halves(name, g, got):
    n, h, c = got.shape
    tr, nt = h // HALF_STEPS, HALF_STEPS
    return _rw(name, lambda ab, bb: ((ab.astype(F32) + bb.astype(F32),), ()), (nt,), [g, got],
               [pl.BlockSpec((n, tr, c), lambda i: (0, _my_core() * nt + i, 0)),
                pl.BlockSpec((n, tr, c), lambda i: (0, i, 0))],
               [_sds((n, h, c), BF16)], [pl.BlockSpec((n, tr, c), lambda i: (0, i, 0))])[0]


def _sum_parts(name, sums, parts, full, layer, n_layer):
    _, h, c = sums.shape
    tr, nt = h // HALF_STEPS, HALF_STEPS

    def body(own, pb):
        acc = own.astype(F32)
        for j in range(pb.shape[0]):
            acc = acc + pb[j].astype(F32)
        return (acc,), ()

    ins = [sums, parts] + ([full] if full is not None else [])
    in_specs = [pl.BlockSpec((None, tr, c), lambda i: (_my_chip(), i, 0)),
                pl.BlockSpec((parts.shape[0], tr, c), lambda i: (0, i, 0))] + ([ANY] if full is not None else [])
    return _rw(name, body, (nt,), ins, in_specs, [_sds((n_layer, 2 * h, c), F32)],
               [pl.BlockSpec((None, tr, c), lambda i: (layer, _my_core() * nt + i, 0))],
               aliases={2: 0} if full is not None else None)[0]


def _adamw_math(w, g, m, v):
    m = B1 * m + (1.0 - B1) * g
    v = B2 * v + (1.0 - B2) * (g * g)
    m_hat = m / (1.0 - B1 ** STEP)
    v_hat = v / (1.0 - B2 ** STEP)
    delta = -LR * (m_hat / (jnp.sqrt(v_hat) + ADAM_EPS) + WD * w)
    return delta, m, v


def _adamw(name, w, g, m, v):
    r, c = w.shape
    tr = 64 if r % 64 == 0 else 8
    spec = _row_spec(tr, c)
    return _rw(name, lambda *b: (_adamw_math(*b), ()), (r // tr,), [w, g, m, v], [spec] * 4,
               [_sds((r, c), F32)] * 3, [spec] * 3)


SMALL = (("ffn1_norm", (DEPTH, D_MODEL)), ("mix_norm", (DEPTH, D_MODEL)), ("ffn2_norm", (DEPTH, D_MODEL)),
         ("conv_b", (DEPTH, SLAB)), ("conv_ln_g", (DEPTH, SLAB)), ("conv_ln_b", (DEPTH, SLAB)),
         ("ret_norm_g", (DEPTH, SLAB)), ("final_norm", (D_MODEL,)), ("conv_w", (DEPTH, CONV_W, SLAB)))


def _pack(parts, rows):
    flat = jnp.concatenate([p.reshape(-1) for p in parts])
    return jnp.pad(flat, (0, rows * 128 - flat.shape[0])).reshape(rows, 128)


def _unpack(packed, shapes):
    flat = packed.reshape(-1)
    out, off = [], 0
    for shp in shapes:
        n = int(np.prod(shp))
        out.append(flat[off:off + n].reshape(shp))
        off += n
    return out


def kernel(x, ffn1_norm, ffn1_w_in, ffn1_w_out, mix_norm, mix_w_in, conv_w, conv_b, conv_ln_g, conv_ln_b, ret_norm_g, mix_w_out, ffn2_norm, ffn2_w_in, ffn2_w_out, final_norm, loss_target, m_ffn1_norm, m_ffn1_w_in, m_ffn1_w_out, m_mix_norm, m_mix_w_in, m_conv_w, m_conv_b, m_conv_ln_g, m_conv_ln_b, m_ret_norm_g, m_mix_w_out, m_ffn2_norm, m_ffn2_w_in, m_ffn2_w_out, m_final_norm, v_ffn1_norm, v_ffn1_w_in, v_ffn1_w_out, v_mix_norm, v_mix_w_in, v_conv_w, v_conv_b, v_conv_ln_g, v_conv_ln_b, v_ret_norm_g, v_mix_w_out, v_ffn2_norm, v_ffn2_w_in, v_ffn2_w_out, v_final_norm):
    given = dict(locals())
    names = [n for n, _ in SMALL] + list(BIG)
    chip = 2 * lax.axis_index("x") + lax.axis_index("y")
    core = lax.axis_index("c")

    cw_rows = 128
    placed = lax.dynamic_update_slice(jnp.zeros((DEPTH, CONV_W, SLAB), F32), conv_w, (0, 0, chip * HEAD))
    placed = placed * (core == 0).astype(F32)
    conv_w_full = _unpack(_all_sum("gather_conv_w", _pack([placed], cw_rows)), [(DEPTH, CONV_W, SLAB)])[0]

    wt = dict(zip(BIG, _gather_weights([_cast_place("cast_" + n, given[n]) for n in BIG])))
    sm = {n: given[n] for n, _ in SMALL}
    sm["conv_w"] = conv_w_full
    loss, dx, big, small, dfinal = _local_step(x[0], loss_target[0], wt, sm)

    grads = [big[l][n] for n in BIG for l in range(DEPTH)]
    theirs = _pair_exchange(grads)
    sums = [_add_halves(f"chipsum{i}", a, b) for i, (a, b) in enumerate(zip(grads, theirs))]
    parts = _chip_exchange(sums)
    full = []
    for i in range(len(BIG)):
        f = None
        for l in range(DEPTH):
            f = _sum_parts(f"shardsum{DEPTH * i + l}", sums[DEPTH * i + l], parts[DEPTH * i + l], f, l, DEPTH)
        full.append(f)
    g_big = dict(zip(BIG, _pair_join(full)))

    small_parts = []
    for n, shp in SMALL:
        if n == "final_norm":
            small_parts.append(dfinal)
        else:
            small_parts.append(jnp.stack([small[l][n].reshape(shp[1:]) for l in range(DEPTH)]))
    g_small = dict(zip([n for n, _ in SMALL], _unpack(_all_sum("sum_small", _pack(small_parts, 200)), [s_ for _, s_ in SMALL])))
    g_small["conv_w"] = lax.dynamic_slice(g_small["conv_w"], (0, 0, chip * HEAD), (DEPTH, CONV_W, HEAD))

    grad, delta, new_m, new_v = dict(g_small), {}, {}, {}
    grad.update(g_big)
    for n in BIG:
        l, r, c = given[n].shape
        f = lambda t: t.reshape(l * r, c)
        d_, m_, v_ = _adamw("adamw_" + n, f(given[n]), f(grad[n]), f(given["m_" + n]), f(given["v_" + n]))
        delta[n], new_m[n], new_v[n] = d_.reshape(l, r, c), m_.reshape(l, r, c), v_.reshape(l, r, c)
    snames = [n for n, _ in SMALL]
    shapes = [given[n].shape for n in snames]
    rows = 104
    d_, m_, v_ = _adamw("adamw_small", _pack([given[n] for n in snames], rows), _pack([grad[n] for n in snames], rows),
                        _pack([given["m_" + n] for n in snames], rows), _pack([given["v_" + n] for n in snames], rows))
    for dst, packed in ((delta, d_), (new_m, m_), (new_v, v_)):
        dst.update(zip(snames, _unpack(packed, shapes)))

    total = lax.psum(loss[0, 0], ("x", "y", "c"))
    order = ["ffn1_norm", "ffn1_w_in", "ffn1_w_out", "mix_norm", "mix_w_in", "conv_w", "conv_b", "conv_ln_g",
             "conv_ln_b", "ret_norm_g", "mix_w_out", "ffn2_norm", "ffn2_w_in", "ffn2_w_out", "final_norm"]
    return (total, dx[None], *[grad[n] for n in order], *[delta[n] for n in order],
            *[new_m[n] for n in order], *[new_v[n] for n in order])
```

```python
import functools

import numpy as np
import jax
import jax.numpy as jnp
from jax import lax
from jax.experimental import pallas as pl
from jax.experimental.pallas import tpu as pltpu

F32 = jnp.float32
BF16 = jnp.bfloat16

D_MODEL = 1024
D_FF = 2816
N_SHARD = 4
FF_SHARD = 2 * D_FF // N_SHARD
MIX_SHARD = 3072 // N_SHARD
HEAD = 64
SLAB = 256
N_SLAB = 3072 // SLAB
CONV_W = 31
CONV_PAD = 32
CHUNK = 64
EPS = 1e-6
ROPE_BASE = 10000.0
DEPTH = 2

LR, B1, B2, ADAM_EPS, WD, STEP = 0.001, 0.9, 0.999, 1e-08, 0.01, 10

VMEM_LIMIT = 56 * 1024 * 1024


def _params(n_grid):
    return pltpu.CompilerParams(dimension_semantics=("arbitrary",) * n_grid, vmem_limit_bytes=VMEM_LIMIT)


def _rw(name, body, grid, ins, in_specs, rows=(), row_specs=(), accs=(), acc_specs=(), aliases=None):
    n_in, n_row = len(ins), len(rows)
    carried = sorted(aliases) if aliases else []

    def kern(*refs):
        vals = [r[...] for i, r in enumerate(refs[:n_in]) if i not in carried]
        row_vals, acc_vals = body(*vals)
        for r, v in zip(refs[n_in:n_in + n_row], row_vals):
            r[...] = v.astype(r.dtype)
        acc_refs = refs[n_in + n_row:]
        if acc_refs:
            first = functools.reduce(jnp.logical_and, [pl.program_id(a) == 0 for a in range(len(grid))])

            @pl.when(first)
            def _():
                for r in acc_refs:
                    r[...] = jnp.zeros(r.shape, r.dtype)

            for r, v in zip(acc_refs, acc_vals):
                r[...] += v.astype(r.dtype)

    return pl.pallas_call(
        kern, name=name, grid=grid, in_specs=list(in_specs), out_specs=list(row_specs) + list(acc_specs),
        out_shape=list(rows) + list(accs), input_output_aliases=dict(aliases or {}),
        compiler_params=_params(len(grid)))(*ins)


def _sds(shape, dtype):
    return jax.ShapeDtypeStruct(shape, dtype)


def _rms(x, g):
    return x * lax.rsqrt(jnp.mean(x * x, axis=-1, keepdims=True) + EPS) * g


def _row_spec(tm, c):
    return pl.BlockSpec((tm, c), lambda i: (i, 0))


def _vec_spec(c):
    return pl.BlockSpec((1, c), lambda i: (0, 0))


def _swiglu(gate, up):
    return jax.nn.silu(gate) * up


def _swiglu_fwd(name, u, tm):
    _, s, c = u.shape
    return _rw(name, lambda ub: ((_swiglu(ub[0:2].astype(F32), ub[2:4].astype(F32)),), ()), (s // tm,), [u],
               [pl.BlockSpec((4, tm, c), lambda i: (0, i, 0))],
               [_sds((2, s, c), BF16)], [pl.BlockSpec((2, tm, c), lambda i: (0, i, 0))])[0]


def _swiglu_bwd(name, u, da, tm):
    _, s, c = u.shape

    def body(ub, dab):
        _, vjp = jax.vjp(_swiglu, ub[0:2].astype(F32), ub[2:4].astype(F32))
        dg, du = vjp(dab.astype(F32))
        return (jnp.concatenate([dg, du], axis=0),), ()

    return _rw(name, body, (s // tm,), [u, da],
               [pl.BlockSpec((4, tm, c), lambda i: (0, i, 0)), pl.BlockSpec((2, tm, c), lambda i: (0, i, 0))],
               [_sds((4, s, c), BF16)], [pl.BlockSpec((4, tm, c), lambda i: (0, i, 0))])[0]


def _ln_silu(y, g, b):
    mu = jnp.mean(y, axis=-1, keepdims=True)
    yc = y - mu
    var = jnp.mean(yc * yc, axis=-1, keepdims=True)
    return jax.nn.silu(yc * lax.rsqrt(var + EPS) * g + b)


def _ln_silu_fwd(name, y, g, b, tm):
    s, c = y.shape
    return _rw(name, lambda yb, gb, bb: ((_ln_silu(yb, gb, bb),), ()), (s // tm,), [y, g, b],
               [_row_spec(tm, c), _vec_spec(c), _vec_spec(c)], [_sds((s, c), BF16)], [_row_spec(tm, c)])[0]


def _ln_silu_bwd(name, y, dcat, g, b, tm):
    s, c = y.shape

    def body(yb, dob, gb, bb):
        _, vjp = jax.vjp(_ln_silu, yb, gb, bb)
        dy, dg, db = vjp(dob)
        return (dy,), (dg, db)

    return _rw(name, body, (s // tm,), [y, dcat, g, b],
               [_row_spec(tm, c), pl.BlockSpec((None, tm, c), lambda i: (0, i, 0)), _vec_spec(c), _vec_spec(c)],
               [_sds((s, c), F32)], [_row_spec(tm, c)],
               [_sds((1, c), F32)] * 2, [_vec_spec(c)] * 2)


def _head_masks():
    lane = np.arange(SLAB) // HEAD
    m = np.zeros((8, SLAB), np.float32)
    for h in range(SLAB // HEAD):
        m[h] = (lane == h)
    return jnp.asarray(m)


def _gated_head_norm(y, gate, g, hm):
    mu = jnp.zeros_like(y)
    for h in range(SLAB // HEAD):
        mu = mu + hm[h:h + 1] * (jnp.sum(y * hm[h:h + 1], axis=-1, keepdims=True) / HEAD)
    yc = y - mu
    var = jnp.zeros_like(y)
    for h in range(SLAB // HEAD):
        var = var + hm[h:h + 1] * (jnp.sum(yc * yc * hm[h:h + 1], axis=-1, keepdims=True) / HEAD)
    return jax.nn.silu(gate) * (yc * lax.rsqrt(var + EPS) * g)


PER_SHARD = MIX_SHARD // SLAB


def _slab_spec(tm, j):
    return pl.BlockSpec((None, tm, SLAB), lambda i: (j, i, 0))


def _proj_slab_spec(tm, j):
    return pl.BlockSpec((None, tm, SLAB), lambda i: (j // PER_SHARD, i, j % PER_SHARD))


def _ghn_fwd(name, y, p32, g, tm):
    s, c = y.shape
    hm = _head_masks()
    return _rw(name, lambda yb, gb, wb, hb: ((_gated_head_norm(yb, gb, wb, hb),), ()), (s // tm,),
               [y, p32, g, hm],
               [_row_spec(tm, c), _proj_slab_spec(tm, 11), _vec_spec(c), pl.BlockSpec((8, c), lambda i: (0, 0))],
               [_sds((s, c), BF16)], [_row_spec(tm, c)])[0]


def _ghn_bwd(name, y, p32, dcat, g, tm):
    s, c = y.shape
    hm = _head_masks()

    def body(yb, gb, dob, wb, hb):
        _, vjp = jax.vjp(lambda a, b_, c_: _gated_head_norm(a, b_, c_, hb), yb, gb, wb)
        dy, dgate, dw = vjp(dob)
        return (dy, dgate), (dw,)

    return _rw(name, body, (s // tm,), [y, p32, dcat, g, hm],
               [_row_spec(tm, c), _proj_slab_spec(tm, 11), _slab_spec(tm, 3), _vec_spec(c),
                pl.BlockSpec((8, c), lambda i: (0, 0))],
               [_sds((s, c), F32)] * 2, [_row_spec(tm, c)] * 2,
               [_sds((1, c), F32)], [_vec_spec(c)])


def _assemble(name, parts, per, tm):
    s = parts[0][0].shape[-2]
    specs = [_row_spec(tm, SLAB) if j is None else pl.BlockSpec((None, tm, SLAB), lambda i, j=j: (j, i, 0))
             for _, j in parts]

    def body(*blocks):
        rows = [jnp.concatenate([b.astype(BF16) for b in blocks[per * q:per * (q + 1)]], axis=-1)
                for q in range(len(blocks) // per)]
        return (jnp.stack(rows),), ()

    nq = len(parts) // per
    return _rw(name, body, (s // tm,), [a for a, _ in parts], specs, [_sds((nq, s, per * SLAB), BF16)],
               [pl.BlockSpec((nq, tm, per * SLAB), lambda i: (0, i, 0))])[0]


def _final(name, x, tgt, g, tm):
    s, d = x.shape

    def body(xb, tb, gb):
        yf, vjp = jax.vjp(_rms, xb, gb)
        err = yf - tb
        dx, dg = vjp(err * (1.0 / d))
        part = 0.5 * jnp.sum(jnp.mean(err * err, axis=-1, keepdims=True), axis=0, keepdims=True)
        return (dx,), (dg, jnp.broadcast_to(part, (1, 128)))

    return _rw(name, body, (s // tm,), [x, tgt, g],
               [_row_spec(tm, d), _row_spec(tm, d), _vec_spec(d)],
               [_sds((s, d), F32)], [_row_spec(tm, d)],
               [_sds((1, d), F32), _sds((1, 128), F32)], [_vec_spec(d), _vec_spec(128)])


NN = (((1,), (0,)), ((), ()))
NT = (((1,), (1,)), ((), ()))
TN = (((0,), (0,)), ((), ()))


def _mm(name, a, b, grid, a_spec, b_spec, outs, out_specs, acc_shape, dims, alpha=1.0):
    nk = grid[-1]
    n_out = len(outs)

    def kern(*refs):
        a_ref, b_ref = refs[0], refs[1]
        o_refs = refs[2:2 + n_out]
        part = lax.dot_general(a_ref[...].astype(BF16), b_ref[...].astype(BF16), dims,
                               preferred_element_type=F32)

        def finish(r):
            if alpha != 1.0:
                r = r * alpha
            for o in o_refs:
                o[...] = r.astype(o.dtype)

        if nk == 1:
            finish(part)
            return
        acc_ref = refs[-1]
        k = pl.program_id(len(grid) - 1)

        @pl.when(k == 0)
        def _():
            acc_ref[...] = part

        @pl.when(jnp.logical_and(k > 0, k < nk - 1))
        def _():
            acc_ref[...] += part

        @pl.when(k == nk - 1)
        def _():
            finish(acc_ref[...] + part)

    return pl.pallas_call(
        kern, name=name, grid=grid, in_specs=[a_spec, b_spec], out_specs=list(out_specs), out_shape=list(outs),
        scratch_shapes=[pltpu.VMEM(acc_shape, F32)] if nk > 1 else [],
        compiler_params=_params(len(grid)))(a, b)


def _norm_proj_in(name, x, g, w, l, tm, dtypes):
    s, d = x.shape
    n = w.shape[-1]
    n_out = len(dtypes)

    def kern(x_ref, g_ref, w_ref, h_ref, *rest):
        o_refs, h_vmem = rest[:n_out], rest[n_out]

        @pl.when(pl.program_id(1) == 0)
        def _():
            h = _rms(x_ref[...], g_ref[...]).astype(BF16)
            h_vmem[...] = h
            h_ref[...] = h

        r = jnp.dot(h_vmem[...], w_ref[...], preferred_element_type=F32)
        for o in o_refs:
            o[...] = r.astype(o.dtype)

    out = pl.pallas_call(
        kern, name=name, grid=(s // tm, N_SHARD),
        in_specs=[pl.BlockSpec((tm, d), lambda i, b: (i, 0)), pl.BlockSpec((1, d), lambda i, b: (0, 0)),
                  pl.BlockSpec((None, None, d, n), lambda i, b: (l, b, 0, 0))],
        out_specs=[pl.BlockSpec((tm, d), lambda i, b: (i, 0))] +
                  [pl.BlockSpec((None, tm, n), lambda i, b: (b, i, 0))] * n_out,
        out_shape=[_sds((s, d), BF16)] + [_sds((N_SHARD, s, n), t) for t in dtypes],
        scratch_shapes=[pltpu.VMEM((tm, d), BF16)], compiler_params=_params(2))(x, g, w)
    return out[0], out[1:]


def _back_in_norm(name, du, w, l, tm, x, dres, g):
    nk, s, n = du.shape
    d = w.shape[2]

    def kern(du_ref, w_ref, x_ref, dres_ref, g_ref, dx_ref, dg_ref):
        dh = lax.dot_general(du_ref[0], w_ref[0], NT, preferred_element_type=F32)
        for k in range(1, nk):
            dh = dh + lax.dot_general(du_ref[k], w_ref[k], NT, preferred_element_type=F32)
        _, vjp = jax.vjp(_rms, x_ref[...], g_ref[...])
        dx, dg = vjp(dh)
        dx_ref[...] = dx + dres_ref[...]

        @pl.when(pl.program_id(0) == 0)
        def _():
            dg_ref[...] = dg

        @pl.when(pl.program_id(0) > 0)
        def _():
            dg_ref[...] += dg

    row = pl.BlockSpec((tm, d), lambda i: (i, 0))
    vec = pl.BlockSpec((1, d), lambda i: (0, 0))
    return pl.pallas_call(
        kern, name=name, grid=(s // tm,),
        in_specs=[pl.BlockSpec((nk, tm, n), lambda i: (0, i, 0)),
                  pl.BlockSpec((None, nk, d, n), lambda i: (l, 0, 0, 0)), row, row, vec],
        out_specs=[row, vec], out_shape=[_sds((s, d), F32), _sds((1, d), F32)],
        compiler_params=_params(1))(du, w, x, dres, g)


def _proj_out(name, a, w, l, res, alpha, tm):
    nk, s, r = a.shape
    d = w.shape[-1]

    def kern(a_ref, w_ref, res_ref, o_ref):
        y = jnp.dot(a_ref[0], w_ref[0], preferred_element_type=F32)
        for k in range(1, nk):
            y = y + jnp.dot(a_ref[k], w_ref[k], preferred_element_type=F32)
        o_ref[...] = res_ref[...] + (y * alpha if alpha != 1.0 else y)

    row = pl.BlockSpec((tm, d), lambda i: (i, 0))
    return pl.pallas_call(
        kern, name=name, grid=(s // tm,),
        in_specs=[pl.BlockSpec((nk, tm, r), lambda i: (0, i, 0)),
                  pl.BlockSpec((None, nk, r, d), lambda i: (l, 0, 0, 0)), row],
        out_specs=row, out_shape=_sds((s, d), F32), compiler_params=_params(1))(a, w, res)


def _back_out(name, dy, w, l, alpha, tm, out_dtype):
    s, d = dy.shape
    nk, r = w.shape[1], w.shape[2]
    return _mm(name, dy, w, (nk, s // tm, 1),
               pl.BlockSpec((tm, d), lambda b, i, k: (i, 0)),
               pl.BlockSpec((None, None, r, d), lambda b, i, k: (l, b, 0, 0)),
               [_sds((nk, s, r), out_dtype)], [pl.BlockSpec((None, tm, r), lambda b, i, k: (b, i, 0))],
               (tm, r), NT, alpha=alpha)[0]


def _grad_in(name, h, du, ts):
    s, d = h.shape
    nb, _, n = du.shape
    return _mm(name, h, du, (nb, 1, s // ts),
               pl.BlockSpec((ts, d), lambda b, i, k: (k, 0)),
               pl.BlockSpec((None, ts, n), lambda b, i, k: (b, k, 0)),
               [_sds((nb, d, n), BF16)], [pl.BlockSpec((None, d, n), lambda b, i, k: (b, 0, 0))],
               (d, n), TN)[0]


def _grad_out(name, a, dy, alpha, ts):
    nb, s, r = a.shape
    d = dy.shape[1]
    return _mm(name, a, dy, (nb, 1, s // ts),
               pl.BlockSpec((None, ts, r), lambda b, i, k: (b, k, 0)),
               pl.BlockSpec((ts, d), lambda b, i, k: (k, 0)),
               [_sds((nb, r, d), BF16)], [pl.BlockSpec((None, r, d), lambda b, i, k: (b, 0, 0))],
               (r, d), TN, alpha=alpha)[0]


CONV_TILE = 256


def _shifted(win, off, rows):
    n = win.shape[0]
    return pltpu.roll(win, (n - off) % n, 0)[0:rows] if off % n else win[0:rows]


def _conv_fwd(name, p32, w, bias):
    s = p32.shape[1]
    cb = 128
    nt = s // CONV_TILE

    def kern(a_ref, b_ref, w_ref, bias_ref, y_ref, vpad):
        vpad[0:CONV_PAD, :] = jnp.zeros((CONV_PAD, cb), F32)

        def fill(i, c):
            r = pl.multiple_of(i * CONV_TILE, CONV_TILE)
            vpad[pl.ds(CONV_PAD + r, CONV_TILE), :] = (
                a_ref[pl.ds(r, CONV_TILE), :] * jax.nn.sigmoid(b_ref[pl.ds(r, CONV_TILE), :]))
            return c

        lax.fori_loop(0, nt, fill, 0)

        def tile(i, c):
            r = pl.multiple_of(i * CONV_TILE, CONV_TILE)
            win = vpad[pl.ds(r, CONV_TILE + CONV_PAD), :]
            acc = jnp.broadcast_to(bias_ref[...], (CONV_TILE, cb))
            for j in range(CONV_W):
                acc = acc + w_ref[j:j + 1, :] * _shifted(win, j + 2, CONV_TILE)
            y_ref[pl.ds(r, CONV_TILE), :] = acc
            return c

        lax.fori_loop(0, nt, tile, 0)

    return pl.pallas_call(
        kern, name=name, grid=(SLAB // cb,),
        in_specs=[pl.BlockSpec((None, s, cb), lambda c: (0, 0, c)),
                  pl.BlockSpec((None, s, cb), lambda c: (0, 0, SLAB // cb + c)),
                  pl.BlockSpec((CONV_W, cb), lambda c: (0, c)),
                  pl.BlockSpec((1, cb), lambda c: (0, c))],
        out_specs=pl.BlockSpec((s, cb), lambda c: (0, c)),
        out_shape=_sds((s, SLAB), F32),
        scratch_shapes=[pltpu.VMEM((s + CONV_PAD, cb), F32)],
        compiler_params=_params(1))(p32, p32, w, bias)


def _conv_bwd(name, p32, w, dy):
    s = p32.shape[1]
    cb = 128
    nt = s // CONV_TILE

    def kern(a_ref, b_ref, w_ref, dy_ref, da_ref, db_ref, dw_ref, dbias_ref, vpad, dpad):
        vpad[0:CONV_PAD, :] = jnp.zeros((CONV_PAD, cb), F32)
        dpad[s:s + CONV_PAD, :] = jnp.zeros((CONV_PAD, cb), F32)
        dw_ref[...] = jnp.zeros((CONV_PAD, cb), F32)
        dbias_ref[...] = jnp.zeros((1, cb), F32)

        def fill(i, c):
            r = pl.multiple_of(i * CONV_TILE, CONV_TILE)
            vpad[pl.ds(CONV_PAD + r, CONV_TILE), :] = (
                a_ref[pl.ds(r, CONV_TILE), :] * jax.nn.sigmoid(b_ref[pl.ds(r, CONV_TILE), :]))
            dpad[pl.ds(r, CONV_TILE), :] = dy_ref[pl.ds(r, CONV_TILE), :]
            return c

        lax.fori_loop(0, nt, fill, 0)

        def tile(i, c):
            r = pl.multiple_of(i * CONV_TILE, CONV_TILE)
            dwin = dpad[pl.ds(r, CONV_TILE + CONV_PAD), :]
            vwin = vpad[pl.ds(r, CONV_TILE + CONV_PAD), :]
            dyt = dwin[0:CONV_TILE]
            dv = jnp.zeros((CONV_TILE, cb), F32)
            for j in range(CONV_W):
                dv = dv + w_ref[j:j + 1, :] * _shifted(dwin, CONV_W - 1 - j, CONV_TILE)
                dw_ref[j:j + 1, :] += jnp.sum(dyt * _shifted(vwin, j + 2, CONV_TILE), axis=0, keepdims=True)
            dbias_ref[...] += jnp.sum(dyt, axis=0, keepdims=True)
            a = a_ref[pl.ds(r, CONV_TILE), :]
            sg = jax.nn.sigmoid(b_ref[pl.ds(r, CONV_TILE), :])
            da_ref[pl.ds(r, CONV_TILE), :] = dv * sg
            db_ref[pl.ds(r, CONV_TILE), :] = dv * a * sg * (1.0 - sg)
            return c

        lax.fori_loop(0, nt, tile, 0)

    col = pl.BlockSpec((s, cb), lambda c: (0, c))
    return pl.pallas_call(
        kern, name=name, grid=(SLAB // cb,),
        in_specs=[pl.BlockSpec((None, s, cb), lambda c: (0, 0, c)),
                  pl.BlockSpec((None, s, cb), lambda c: (0, 0, SLAB // cb + c)),
                  pl.BlockSpec((CONV_W, cb), lambda c: (0, c)), col],
        out_specs=[col, col, pl.BlockSpec((CONV_PAD, cb), lambda c: (0, c)), pl.BlockSpec((1, cb), lambda c: (0, c))],
        out_shape=[_sds((s, SLAB), F32), _sds((s, SLAB), F32), _sds((CONV_PAD, SLAB), F32), _sds((1, SLAB), F32)],
        scratch_shapes=[pltpu.VMEM((s + CONV_PAD, cb), F32), pltpu.VMEM((s + CONV_PAD, cb), F32)],
        compiler_params=_params(1))(p32, p32, w, dy)


SB_BLOCK = 256
N_HEAD = SLAB // HEAD


def _sb_logits(qm, k, tri):
    z = lax.dot_general(qm, k, NT, preferred_element_type=F32)
    sign_bit = jnp.uint32(0x80000000)
    neg_abs = lax.bitcast_convert_type(lax.bitcast_convert_type(z, jnp.uint32) | sign_bit, F32)
    lb = jnp.minimum(z, 0.0) - jnp.log(1.0 + jnp.exp(neg_abs))
    ln = lb - z
    if tri is not None:
        ln = jnp.where(tri, ln, 0.0)
    return lb, ln


def _first_col(x):
    return jnp.broadcast_to(x[:, 0:1], (x.shape[0], 128))


def _head_stack(dst, x, lane_head, bq):
    for h in range(N_HEAD):
        dst[h * bq:(h + 1) * bq, :] = jnp.where(lane_head == h, x, jnp.zeros_like(x))


def _sb_fwd(name, p16):
    s = p16.shape[1]
    bq = min(SB_BLOCK, s)
    nq = s // bq

    def kern(q_ref, k_ref, v_ref, o_ref, w_hbm, lb_hbm, qm_ref, v4_refs, w4_refs, ws_ref, lbs_ref, acc_ref, r_ref,
             sem):
        g, qi = pl.program_id(0), pl.program_id(1)
        lane_head = lax.broadcasted_iota(jnp.int32, (1, SLAB), 1) // HEAD
        _head_stack(qm_ref, (q_ref[...].astype(F32) * (HEAD ** -0.5)).astype(BF16), lane_head, bq)
        row = lax.broadcasted_iota(jnp.int32, (bq, bq), 0)
        col = lax.broadcasted_iota(jnp.int32, (bq, bq), 1)
        after = (row > col).astype(BF16)
        tri = col < row
        acc_ref[...] = jnp.zeros((bq, SLAB), F32)
        r_ref[...] = jnp.zeros((N_HEAD, bq, 128), F32)

        def saves(slot, kb):
            return (pltpu.make_async_copy(ws_ref.at[slot], w_hbm.at[g, qi, kb], sem.at[0, slot]),
                    pltpu.make_async_copy(lbs_ref.at[slot], lb_hbm.at[g, qi, kb], sem.at[1, slot]))

        def tile(i, masked, u):
            kb, slot = qi - i, i % 4
            v4_ref, w4_ref = v4_refs.at[u], w4_refs.at[u]
            rows = pl.ds(pl.multiple_of(kb * bq, bq), bq)
            k = k_ref[rows, :]
            _head_stack(v4_ref, v_ref[rows, :], lane_head, bq)
            for h in range(N_HEAD):
                mine = pl.ds(h * bq, bq)
                lb, ln = _sb_logits(qm_ref[h * bq:(h + 1) * bq, :], k, tri if masked else None)
                rem = jnp.dot(ln.astype(BF16), after, preferred_element_type=F32)
                w = jnp.exp(lb + rem + r_ref[h][:, 0:1])
                if masked:
                    w = jnp.where(tri, w, 0.0)
                wb = w.astype(BF16)
                w4_ref[:, h * bq:(h + 1) * bq] = wb
                ws_ref[slot, mine, :] = wb
                lbs_ref[slot, mine, :] = lb.astype(BF16)
                r_ref[h] += _first_col(rem[:, 0:128] + ln[:, 0:128])
            acc_ref[...] += jnp.dot(w4_ref[...], v4_ref[...], preferred_element_type=F32)

        def save(i, start):
            for cp in saves(i % 4, qi - i):
                cp.start() if start else cp.wait()

        @pl.when(qi == 0)
        def _():
            tile(0, True, 0)
            save(0, True)
            save(0, False)

        @pl.when(qi >= 1)
        def _():
            tile(0, True, 0)
            tile(1, False, 1)
            save(0, True)
            save(1, True)

        def pair(j, c):
            tile(2 * j, False, 0)
            tile(2 * j + 1, False, 1)
            save(2 * j - 2, False)
            save(2 * j - 1, False)
            save(2 * j, True)
            save(2 * j + 1, True)
            return c

        n_pair = (qi + 1) // 2
        lax.fori_loop(1, n_pair, pair, 0)

        @pl.when(jnp.logical_and(qi >= 1, qi % 2 == 1))
        def _():
            save(qi - 1, False)
            save(qi, False)

        @pl.when(jnp.logical_and(qi >= 2, qi % 2 == 0))
        def _():
            tile(qi, False, 0)
            save(qi - 2, False)
            save(qi - 1, False)
            save(qi, True)
            save(qi, False)

        o_ref[...] = acc_ref[...]

    saved = _sds((2, nq, nq, N_HEAD * bq, bq), BF16)
    return pl.pallas_call(
        kern, name=name, grid=(2, nq),
        in_specs=[pl.BlockSpec((None, bq, SLAB), lambda g, i: ((2 + g) // PER_SHARD, i, (2 + g) % PER_SHARD)),
                  pl.BlockSpec((None, s, SLAB), lambda g, i: ((4 + g) // PER_SHARD, 0, (4 + g) % PER_SHARD)),
                  pl.BlockSpec((None, s, SLAB), lambda g, i: ((6 + g) // PER_SHARD, 0, (6 + g) % PER_SHARD))],
        out_specs=[pl.BlockSpec((None, bq, SLAB), lambda g, i: (g, i, 0)),
                   pl.BlockSpec(memory_space=pl.ANY), pl.BlockSpec(memory_space=pl.ANY)],
        out_shape=[_sds((2, s, SLAB), F32), saved, saved],
        scratch_shapes=[pltpu.VMEM((N_HEAD * bq, SLAB), BF16), pltpu.VMEM((2, N_HEAD * bq, SLAB), BF16),
                        pltpu.VMEM((2, bq, N_HEAD * bq), BF16), pltpu.VMEM((4, N_HEAD * bq, bq), BF16),
                        pltpu.VMEM((4, N_HEAD * bq, bq), BF16), pltpu.VMEM((bq, SLAB), F32),
                        pltpu.VMEM((N_HEAD, bq, 128), F32), pltpu.SemaphoreType.DMA((2, 4))],
        compiler_params=_params(2))(p16, p16, p16)


def _sb_bwd(name, p16, w_saved, lb_saved, dcat):
    s = p16.shape[1]
    bq = min(SB_BLOCK, s)
    nq = s // bq

    def kern(q_ref, k_ref, v_ref, do_ref, w_hbm, lb_hbm, dq_ref, dk_hbm, dv_hbm, dk_acc, dv_acc, dq_acc,
             qm_ref, dom_ref, k4_refs, dzc_refs, dzs_refs, ws_ref, lbs_ref, c_ref, sem, lsem):
        g, qi = pl.program_id(0), pl.program_id(1)

        @pl.when(qi == 0)
        def _():
            dk_acc[...] = jnp.zeros((s, SLAB), F32)
            dv_acc[...] = jnp.zeros((s, SLAB), F32)

        lane_head = lax.broadcasted_iota(jnp.int32, (1, SLAB), 1) // HEAD
        _head_stack(qm_ref, (q_ref[...].astype(F32) * (HEAD ** -0.5)).astype(BF16), lane_head, bq)
        _head_stack(dom_ref, do_ref[...].astype(BF16), lane_head, bq)
        row = lax.broadcasted_iota(jnp.int32, (bq, bq), 0)
        col = lax.broadcasted_iota(jnp.int32, (bq, bq), 1)
        earlier = (row < col).astype(BF16)
        tri = col < row
        dq_acc[...] = jnp.zeros((bq, SLAB), F32)
        c_ref[...] = jnp.zeros((N_HEAD, bq, 128), F32)

        def loads(kb):
            slot, kb = kb % 4, jnp.minimum(kb, qi)
            return (pltpu.make_async_copy(w_hbm.at[g, qi, kb], ws_ref.at[slot], lsem.at[0, slot]),
                    pltpu.make_async_copy(lb_hbm.at[g, qi, kb], lbs_ref.at[slot], lsem.at[1, slot]))

        def tile(kb, masked, u):
            slot = kb % 4
            k4_ref, dzc_ref, dzs_ref = k4_refs.at[u], dzc_refs.at[u], dzs_refs.at[u]
            rows = pl.ds(pl.multiple_of(kb * bq, bq), bq)
            k, v = k_ref[rows, :], v_ref[rows, :]
            _head_stack(k4_ref, k, lane_head, bq)
            for h in range(N_HEAD):
                mine = slice(h * bq, (h + 1) * bq)
                wb = ws_ref[slot, pl.ds(h * bq, bq), :]
                dl = wb.astype(F32) * lax.dot_general(dom_ref[mine, :], v, NT, preferred_element_type=F32)
                prefix = jnp.dot(dl.astype(BF16), earlier, preferred_element_type=F32)
                before = prefix + c_ref[h][:, 0:1]
                sig = jnp.exp(lbs_ref[slot, pl.ds(h * bq, bq), :].astype(F32))
                dz = dl - sig * (dl + before)
                if masked:
                    dz = jnp.where(tri, dz, 0.0)
                dzb = dz.astype(BF16)
                dzc_ref[:, mine] = dzb
                dzs_ref[mine, :] = dzb
                tail = prefix[:, bq - 128:] + dl[:, bq - 128:]
                c_ref[h] += jnp.broadcast_to(tail[:, 127:128], (bq, 128))
            dq_acc[...] += jnp.dot(dzc_ref[...], k4_ref[...], preferred_element_type=F32)
            dk_acc[rows, :] += lax.dot_general(dzs_ref[...], qm_ref[...], TN, preferred_element_type=F32)
            dv_acc[rows, :] += lax.dot_general(ws_ref[slot], dom_ref[...], TN, preferred_element_type=F32)

        for kb in (0, 1):
            for cp in loads(kb):
                cp.start()

        def pair(j, c):
            for kb in (2 * j + 2, 2 * j + 3):
                for cp in loads(kb):
                    cp.start()
            for kb in (2 * j, 2 * j + 1):
                for cp in loads(kb):
                    cp.wait()
            tile(2 * j, False, 0)
            tile(2 * j + 1, False, 1)
            return c

        lax.fori_loop(0, qi // 2, pair, 0)
        for kb in (qi - qi % 2, qi - qi % 2 + 1):
            for cp in loads(kb):
                cp.wait()

        @pl.when(qi % 2 == 1)
        def _():
            tile(qi - 1, False, 0)
            tile(qi, True, 1)

        @pl.when(qi % 2 == 0)
        def _():
            tile(qi, True, 0)

        dq_ref[...] = dq_acc[...] * (HEAD ** -0.5)

        @pl.when(qi == nq - 1)
        def _():
            ck = pltpu.make_async_copy(dk_acc, dk_hbm.at[g], sem.at[0])
            cv = pltpu.make_async_copy(dv_acc, dv_hbm.at[g], sem.at[1])
            ck.start()
            cv.start()
            ck.wait()
            cv.wait()

    blk = lambda j0: pl.BlockSpec((None, bq, SLAB), lambda g, i: (j0 + g, i, 0))
    full = lambda j0: pl.BlockSpec((None, s, SLAB), lambda g, i: ((j0 + g) // PER_SHARD, 0, (j0 + g) % PER_SHARD))
    q_blk = pl.BlockSpec((None, bq, SLAB), lambda g, i: ((2 + g) // PER_SHARD, i, (2 + g) % PER_SHARD))
    stack16 = pltpu.VMEM((N_HEAD * bq, SLAB), BF16)
    return pl.pallas_call(
        kern, name=name, grid=(2, nq),
        in_specs=[q_blk, full(4), full(6), blk(1),
                  pl.BlockSpec(memory_space=pl.ANY), pl.BlockSpec(memory_space=pl.ANY)],
        out_specs=[blk(0), pl.BlockSpec(memory_space=pl.ANY), pl.BlockSpec(memory_space=pl.ANY)],
        out_shape=[_sds((2, s, SLAB), F32)] * 3,
        scratch_shapes=[pltpu.VMEM((s, SLAB), F32), pltpu.VMEM((s, SLAB), F32), pltpu.VMEM((bq, SLAB), F32),
                        stack16, stack16, pltpu.VMEM((2, N_HEAD * bq, SLAB), BF16),
                        pltpu.VMEM((2, bq, N_HEAD * bq), BF16), pltpu.VMEM((2, N_HEAD * bq, bq), BF16),
                        pltpu.VMEM((4, N_HEAD * bq, bq), BF16), pltpu.VMEM((4, N_HEAD * bq, bq), BF16),
                        pltpu.VMEM((N_HEAD, bq, 128), F32),
                        pltpu.SemaphoreType.DMA((2,)), pltpu.SemaphoreType.DMA((2, 4))],
        compiler_params=_params(2))(p16, p16, p16, dcat, w_saved, lb_saved)


RET_BLOCK = 256


def _ret_tables(s, bl):
    nh = SLAB // HEAD
    lane_h = np.arange(SLAB) // HEAD
    log_gamma = np.log1p(-np.exp2(-5.0 - np.arange(nh, dtype=np.float64)))
    lg_lane = log_gamma[lane_h]
    half = HEAD // 2
    inv = 1.0 / (ROPE_BASE ** (np.arange(half, dtype=np.float64) / half))
    ang = np.arange(s, dtype=np.float64)[:, None] * inv[None, :]
    within = np.arange(SLAB) % HEAD
    cos = np.cos(ang)[:, within % half]
    sin = np.sin(ang)[:, within % half] * np.where(within < half, -1.0, 1.0)[None, :]
    perm = np.zeros((SLAB, SLAB))
    partner = np.where(within < half, np.arange(SLAB) + half, np.arange(SLAB) - half)
    perm[partner, np.arange(SLAB)] = 1.0
    i = np.arange(bl)
    diff = i[:, None] - i[None, :]
    same = (i[:, None] // CHUNK) == (i[None, :] // CHUNK)
    earlier = (i[None, :] // CHUNK) < (i[:, None] // CHUNK)
    decay = np.zeros((nh, bl, bl))
    for h in range(nh):
        decay[h] = np.where(same, np.exp(log_gamma[h] * np.abs(diff)),
                            np.where(earlier, np.exp(log_gamma[h] * diff), 0.0))
    qd = np.exp(lg_lane[None, :] * (i[:, None] + 1.0))
    kd = np.exp(lg_lane[None, :] * (bl - 1.0 - i[:, None]))
    gam = np.exp(lg_lane * bl)[:, None] * np.ones((1, SLAB))
    bd = (lane_h[:, None] == lane_h[None, :]).astype(np.float64)
    f = lambda a: jnp.asarray(a, F32)
    return f(cos), f(sin), f(perm), f(decay), f(qd), f(kd), f(gam), f(bd)


def _ret_block(q, k, v, state, cos, sin, perm, decay, qd, kd, gam, bd, hm):
    qr = (q * cos + jnp.dot(q, perm, preferred_element_type=F32) * sin) * (HEAD ** -0.5)
    kr = k * cos + jnp.dot(k, perm, preferred_element_type=F32) * sin
    y = jnp.dot(qr * qd, state, preferred_element_type=F32)
    for h in range(SLAB // HEAD):
        m = hm[h:h + 1]
        sc = lax.dot_general(qr * m, kr, NT, preferred_element_type=F32) * decay[h]
        y = y + jnp.dot(sc, v * m, preferred_element_type=F32)
    new_state = gam * state + lax.dot_general(kr * kd, v, TN, preferred_element_type=F32) * bd
    return y, new_state


def _ret_specs(s, bl, rev):
    nb = s // bl
    pos = (lambda n: nb - 1 - n) if rev else (lambda n: n)
    slab = lambda j: pl.BlockSpec((None, bl, SLAB), lambda n: (j // PER_SHARD, pos(n), j % PER_SHARD))
    const2 = lambda r: pl.BlockSpec((r, SLAB), lambda n: (0, 0))
    tab = [pl.BlockSpec((bl, SLAB), lambda n: (pos(n), 0))] * 2 + [
        const2(SLAB), pl.BlockSpec((SLAB // HEAD, bl, bl), lambda n: (0, 0, 0)),
        const2(bl), const2(bl), const2(SLAB), const2(SLAB), const2(8)]
    return nb, pos, slab, tab


def _ret_fwd(name, p32):
    s = p32.shape[1]
    bl = min(RET_BLOCK, s)
    nb, pos, slab, tab = _ret_specs(s, bl, False)
    tables = _ret_tables(s, bl) + (_head_masks(),)

    def kern(q_ref, k_ref, v_ref, *rest):
        t_refs, (y_ref, st_ref, state) = rest[:9], rest[9:]

        @pl.when(pl.program_id(0) == 0)
        def _():
            state[...] = jnp.zeros((SLAB, SLAB), F32)

        st_ref[...] = state[...]
        y, new = _ret_block(q_ref[...], k_ref[...], v_ref[...], state[...], *[t[...] for t in t_refs])
        y_ref[...] = y
        state[...] = new

    return pl.pallas_call(
        kern, name=name, grid=(nb,), in_specs=[slab(8), slab(9), slab(10)] + tab,
        out_specs=[pl.BlockSpec((bl, SLAB), lambda n: (n, 0)), pl.BlockSpec((None, SLAB, SLAB), lambda n: (n, 0, 0))],
        out_shape=[_sds((s, SLAB), F32), _sds((nb, SLAB, SLAB), F32)],
        scratch_shapes=[pltpu.VMEM((SLAB, SLAB), F32)], compiler_params=_params(1))(p32, p32, p32, *tables)


def _ret_bwd(name, p32, states, dy):
    s = p32.shape[1]
    bl = min(RET_BLOCK, s)
    nb, pos, slab, tab = _ret_specs(s, bl, True)
    tables = _ret_tables(s, bl) + (_head_masks(),)
    rowblk = pl.BlockSpec((bl, SLAB), lambda n: (pos(n), 0))

    def kern(q_ref, k_ref, v_ref, st_ref, dy_ref, *rest):
        t_refs, (dq_ref, dk_ref, dv_ref, dstate) = rest[:9], rest[9:]

        @pl.when(pl.program_id(0) == 0)
        def _():
            dstate[...] = jnp.zeros((SLAB, SLAB), F32)

        tv = [t[...] for t in t_refs]
        _, vjp = jax.vjp(lambda a, b, c, d: _ret_block(a, b, c, d, *tv),
                         q_ref[...], k_ref[...], v_ref[...], st_ref[...])
        dq, dk, dv, ds = vjp((dy_ref[...], dstate[...]))
        dq_ref[...] = dq
        dk_ref[...] = dk
        dv_ref[...] = dv
        dstate[...] = ds

    return pl.pallas_call(
        kern, name=name, grid=(nb,),
        in_specs=[slab(8), slab(9), slab(10), pl.BlockSpec((None, SLAB, SLAB), lambda n: (pos(n), 0, 0)), rowblk] + tab,
        out_specs=[rowblk] * 3, out_shape=[_sds((s, SLAB), F32)] * 3,
        scratch_shapes=[pltpu.VMEM((SLAB, SLAB), F32)], compiler_params=_params(1))(p32, p32, p32, states, dy, *tables)


TM_FFN = 1024
TM_SLAB = 2048
TM_NORM = 256
TM_OUT = 512
TM_GRAD = 2048
TM_RW = 256
TM_FF = 256


def _ffn_fwd(tag, x, g, w_in, w_out, l):
    tm = min(TM_FFN, x.shape[0])
    h, (u,) = _norm_proj_in(tag + "_in", x, g, w_in, l, tm, [BF16])
    a = _swiglu_fwd(tag + "_act", u, TM_FF)
    w_out2 = w_out.reshape(DEPTH, 2, FF_SHARD, D_MODEL)
    xn = _proj_out(tag + "_out", a, w_out2, l, x, 0.5, min(TM_OUT, x.shape[0]))
    return xn, (x, h, u, a)


def _ffn_bwd(tag, saved, dxn, g, w_in, w_out, l):
    x, h, u, a = saved
    tm = min(TM_FFN, x.shape[0])
    w_out2 = w_out.reshape(DEPTH, 2, FF_SHARD, D_MODEL)
    da = _back_out(tag + "_dact", dxn, w_out2, l, 0.5, tm, BF16)
    tk = min(TM_GRAD, x.shape[0])
    dw_out = _grad_out(tag + "_dwout", a, dxn, 0.5, tk)
    du = _swiglu_bwd(tag + "_dswi", u, da, TM_FF)
    dx, dg = _back_in_norm(tag + "_dh", du, w_in, l, min(TM_NORM, x.shape[0]), x, dxn, g)
    dw_in = _grad_in(tag + "_dwin", h, du, tk)
    return dx, dg, dw_in, dw_out.reshape(N_SHARD, D_FF // N_SHARD, D_MODEL)


def _mix_fwd(tag, x, sm, w_in, w_out, l):
    h, (p32, p16) = _norm_proj_in(tag + "_in", x, sm["mix_norm"][l:l + 1], w_in, l, min(TM_FFN, x.shape[0]),
                                  [F32, BF16])
    ypre = _conv_fwd(tag + "_conv", p32, sm["conv_w"][l], sm["conv_b"][l:l + 1])
    yconv = _ln_silu_fwd(tag + "_ln", ypre, sm["conv_ln_g"][l:l + 1], sm["conv_ln_b"][l:l + 1], TM_RW)
    osb, w_sb, lb_sb = _sb_fwd(tag + "_sb", p16)
    yr, states = _ret_fwd(tag + "_ret", p32)
    yret = _ghn_fwd(tag + "_ghn", yr, p32, sm["ret_norm_g"][l:l + 1], TM_RW)
    ycat = _assemble(tag + "_cat", [(yconv, None), (osb, 0), (osb, 1), (yret, None)], 1, TM_RW)
    xn = _proj_out(tag + "_out", ycat, w_out, l, x, 1.0, min(TM_OUT, x.shape[0]))
    return xn, (x, h, p32, p16, ypre, (osb, w_sb, lb_sb), yr, states, ycat)


def _mix_bwd(tag, saved, dxn, sm, w_in, w_out, l):
    x, h, p32, p16, ypre, osb, yr, states, ycat = saved
    ts = min(TM_SLAB, x.shape[0])
    dcat = _back_out(tag + "_dcat", dxn, w_out, l, 1.0, ts, F32)
    dw_out = _grad_out(tag + "_dwout", ycat, dxn, 1.0, ts)
    dypre, dlg, dlb = _ln_silu_bwd(tag + "_dln", ypre, dcat, sm["conv_ln_g"][l:l + 1], sm["conv_ln_b"][l:l + 1], TM_RW)
    da, db, dcw, dcb = _conv_bwd(tag + "_dconv", p32, sm["conv_w"][l], dypre)
    dq, dk, dv = _sb_bwd(tag + "_dsb", p16, osb[1], osb[2], dcat)
    dyr, dgate, drg = _ghn_bwd(tag + "_dghn", yr, p32, dcat, sm["ret_norm_g"][l:l + 1], TM_RW)
    dqr, dkr, dvr = _ret_bwd(tag + "_dret", p32, states, dyr)
    dp = _assemble(tag + "_dp", [(da, None), (db, None), (dq, 0), (dq, 1), (dk, 0), (dk, 1), (dv, 0), (dv, 1),
                                 (dqr, None), (dkr, None), (dvr, None), (dgate, None)], PER_SHARD, TM_RW)
    dx, dg = _back_in_norm(tag + "_dh", dp, w_in, l, min(TM_NORM, x.shape[0]), x, dxn, sm["mix_norm"][l:l + 1])
    dw_in = _grad_in(tag + "_dwin", h, dp, min(TM_GRAD, x.shape[0]))
    small = dict(mix_norm=dg, conv_w=dcw[0:CONV_W], conv_b=dcb, conv_ln_g=dlg, conv_ln_b=dlb, ret_norm_g=drg)
    return dx, small, dw_in, dw_out


def _local_step(x, tgt, wt, sm):
    saved = []
    for l in range(DEPTH):
        x, s1 = _ffn_fwd(f"l{l}f1", x, sm["ffn1_norm"][l:l + 1], wt["ffn1_w_in"], wt["ffn1_w_out"], l)
        x, s2 = _mix_fwd(f"l{l}mx", x, sm, wt["mix_w_in"], wt["mix_w_out"], l)
        x, s3 = _ffn_fwd(f"l{l}f2", x, sm["ffn2_norm"][l:l + 1], wt["ffn2_w_in"], wt["ffn2_w_out"], l)
        saved.append((s1, s2, s3))
    dx, dfinal, loss = _final("final", x, tgt, sm["final_norm"][None, :], TM_RW)
    big = [None] * DEPTH
    small = [None] * DEPTH
    for l in reversed(range(DEPTH)):
        s1, s2, s3 = saved[l]
        dx, dg3, dwi3, dwo3 = _ffn_bwd(f"l{l}f2", s3, dx, sm["ffn2_norm"][l:l + 1], wt["ffn2_w_in"], wt["ffn2_w_out"], l)
        dx, sml, dwi2, dwo2 = _mix_bwd(f"l{l}mx", s2, dx, sm, wt["mix_w_in"], wt["mix_w_out"], l)
        dx, dg1, dwi1, dwo1 = _ffn_bwd(f"l{l}f1", s1, dx, sm["ffn1_norm"][l:l + 1], wt["ffn1_w_in"], wt["ffn1_w_out"], l)
        big[l] = dict(ffn1_w_in=dwi1, ffn1_w_out=dwo1, mix_w_in=dwi2, mix_w_out=dwo2, ffn2_w_in=dwi3, ffn2_w_out=dwo3)
        sml.update(ffn1_norm=dg1, ffn2_norm=dg3)
        small[l] = sml
    return loss, dx, big, small, dfinal


MESH = pl.DeviceIdType.MESH
ANY = pl.BlockSpec(memory_space=pl.ANY)
BIG = ("ffn1_w_in", "ffn1_w_out", "mix_w_in", "mix_w_out", "ffn2_w_in", "ffn2_w_out")


def _place():
    x, y, c = lax.axis_index("x"), lax.axis_index("y"), lax.axis_index("c")
    chips = [(1 - x, y), (x, 1 - y), (1 - x, 1 - y)]
    return x, y, c, chips


def _gather_weights(w16):
    n = len(w16)

    def kern(*refs):
        dst = refs[n:2 * n]
        send, recv = refs[2 * n:]
        x, y, c, chips = _place()
        mine = 2 * x + y
        firsts, passes = [], []
        for a in range(n):
            h = dst[a].shape[2] // 2
            own = dst[a].at[:, mine, pl.ds(c * h, h)]
            for j, (cx, cy) in enumerate(chips):
                cp = pltpu.make_async_remote_copy(
                    src_ref=own, dst_ref=own, send_sem=send.at[6 * a + j], recv_sem=recv.at[6 * a + j],
                    device_id=(cx, cy, c), device_id_type=MESH)
                cp.start()
                firsts.append(cp)
        for a in range(n):
            h = dst[a].shape[2] // 2
            half = pl.ds(c * h, h)
            for j, (cx, cy) in enumerate(chips):
                theirs = dst[a].at[:, 2 * cx + cy, half]
                pltpu.make_async_remote_copy(
                    src_ref=theirs, dst_ref=theirs, send_sem=send.at[6 * a + j], recv_sem=recv.at[6 * a + j],
                    device_id=(cx, cy, c), device_id_type=MESH).wait_recv()
                fw = pltpu.make_async_remote_copy(
                    src_ref=theirs, dst_ref=theirs, send_sem=send.at[6 * a + 3 + j], recv_sem=recv.at[6 * a + 3 + j],
                    device_id=(x, y, 1 - c), device_id_type=MESH)
                fw.start()
                passes.append(fw)
        for a in range(n):
            h = dst[a].shape[2] // 2
            other = pl.ds((1 - c) * h, h)
            for j, (cx, cy) in enumerate(chips):
                got = dst[a].at[:, 2 * cx + cy, other]
                pltpu.make_async_remote_copy(
                    src_ref=got, dst_ref=got, send_sem=send.at[6 * a + 3 + j], recv_sem=recv.at[6 * a + 3 + j],
                    device_id=(x, y, 1 - c), device_id_type=MESH).wait_recv()
        for cp in firsts + passes:
            cp.wait_send()

    return pl.pallas_call(
        kern, name="gather_weights", in_specs=[ANY] * n, out_specs=[ANY] * n,
        out_shape=[_sds(w.shape, w.dtype) for w in w16], input_output_aliases={a: a for a in range(n)},
        scratch_shapes=[pltpu.SemaphoreType.DMA((6 * n,)), pltpu.SemaphoreType.DMA((6 * n,))])(*w16)


def _pair_exchange(grads):
    n = len(grads)

    def kern(*refs):
        src, got_o = refs[:n], refs[n:2 * n]
        send, recv = refs[2 * n:]
        x, y, c, _ = _place()
        cps = []
        for a in range(n):
            h = src[a].shape[1] // 2
            cp = pltpu.make_async_remote_copy(
                src_ref=src[a].at[:, pl.ds((1 - c) * h, h)], dst_ref=got_o[a],
                send_sem=send.at[a], recv_sem=recv.at[a], device_id=(x, y, 1 - c), device_id_type=MESH)
            cp.start()
            cps.append(cp)
        for cp in cps:
            cp.wait()

    halves = [_sds((g.shape[0], g.shape[1] // 2, g.shape[2]), g.dtype) for g in grads]
    return pl.pallas_call(
        kern, name="pair_exchange", in_specs=[ANY] * n, out_specs=[ANY] * n, out_shape=halves,
        scratch_shapes=[pltpu.SemaphoreType.DMA((n,)), pltpu.SemaphoreType.DMA((n,))])(*grads)


def _chip_exchange(sums):
    n = len(sums)

    def kern(*refs):
        src, dst = refs[:n], refs[n:2 * n]
        send, recv = refs[2 * n:]
        x, y, c, chips = _place()
        cps = []
        for a in range(n):
            for j, (cx, cy) in enumerate(chips):
                cp = pltpu.make_async_remote_copy(
                    src_ref=src[a].at[2 * cx + cy], dst_ref=dst[a].at[j],
                    send_sem=send.at[3 * a + j], recv_sem=recv.at[3 * a + j],
                    device_id=(cx, cy, c), device_id_type=MESH)
                cp.start()
                cps.append(cp)
        for cp in cps:
            cp.wait()

    return pl.pallas_call(
        kern, name="chip_exchange", in_specs=[ANY] * n, out_specs=[ANY] * n,
        out_shape=[_sds((3,) + s_.shape[1:], s_.dtype) for s_ in sums],
        scratch_shapes=[pltpu.SemaphoreType.DMA((3 * n,)), pltpu.SemaphoreType.DMA((3 * n,))])(*sums)


def _pair_join(full):
    n = len(full)

    def kern(*refs):
        dst = refs[n:2 * n]
        send, recv = refs[2 * n:]
        x, y, c, _ = _place()
        cps = []
        for a in range(n):
            h = dst[a].shape[1] // 2
            mine = dst[a].at[:, pl.ds(c * h, h)]
            cp = pltpu.make_async_remote_copy(
                src_ref=mine, dst_ref=mine, send_sem=send.at[a], recv_sem=recv.at[a],
                device_id=(x, y, 1 - c), device_id_type=MESH)
            cp.start()
            cps.append(cp)
        for a, cp in enumerate(cps):
            cp.wait_send()
            h = dst[a].shape[1] // 2
            got = dst[a].at[:, pl.ds((1 - c) * h, h)]
            pltpu.make_async_remote_copy(
                src_ref=got, dst_ref=got, send_sem=send.at[a], recv_sem=recv.at[a],
                device_id=(x, y, 1 - c), device_id_type=MESH).wait_recv()

    return pl.pallas_call(
        kern, name="pair_join", in_specs=[ANY] * n, out_specs=[ANY] * n,
        out_shape=[_sds(f.shape, f.dtype) for f in full], input_output_aliases={a: a for a in range(n)},
        scratch_shapes=[pltpu.SemaphoreType.DMA((n,)), pltpu.SemaphoreType.DMA((n,))])(*full)


def _all_sum(name, v):
    r = v.shape[0]

    def kern(v_ref, o_ref, buf, send, recv):
        x, y, c, _ = _place()
        me = 4 * x + 2 * y + c
        buf[me] = v_ref[...]
        cps = []
        for k in range(1, 8):
            peer = (x ^ (k >> 2), y ^ ((k >> 1) & 1), c ^ (k & 1))
            cp = pltpu.make_async_remote_copy(
                src_ref=v_ref, dst_ref=buf.at[me], send_sem=send.at[k - 1], recv_sem=recv.at[k - 1],
                device_id=peer, device_id_type=MESH)
            cp.start()
            cps.append(cp)
        for k in range(1, 8):
            peer_id = me ^ k
            pltpu.make_async_remote_copy(
                src_ref=v_ref, dst_ref=buf.at[peer_id], send_sem=send.at[k - 1], recv_sem=recv.at[k - 1],
                device_id=(x, y, c), device_id_type=MESH).wait_recv()
        for cp in cps:
            cp.wait_send()
        acc = buf[0]
        for d in range(1, 8):
            acc = acc + buf[d]
        o_ref[...] = acc

    vm = pl.BlockSpec(memory_space=pltpu.VMEM)
    return pl.pallas_call(
        kern, name=name, in_specs=[vm], out_specs=vm, out_shape=_sds((r, 128), F32),
        scratch_shapes=[pltpu.VMEM((8, r, 128), F32), pltpu.SemaphoreType.DMA((7,)),
                        pltpu.SemaphoreType.DMA((7,))])(v)


def _my_chip():
    return 2 * lax.axis_index("x") + lax.axis_index("y")


def _my_core():
    return lax.axis_index("c")


def _cast_place(name, w):
    l, r, c = w.shape
    tr = r // 4
    return _rw(name, lambda wb: ((wb,), ()), (l, r // tr), [w],
               [pl.BlockSpec((None, tr, c), lambda j, i: (j, i, 0))],
               [_sds((l, N_SHARD, r, c), BF16)],
               [pl.BlockSpec((None, None, tr, c), lambda j, i: (j, _my_chip(), i, 0))])[0]


HALF_STEPS = 2


def _add_halves(name, g, got):
    n, h, c = got.shape
    tr, nt = h // HALF_STEPS, HALF_STEPS
    return _rw(name, lambda ab, bb: ((ab.astype(F32) + bb.astype(F32),), ()), (nt,), [g, got],
               [pl.BlockSpec((n, tr, c), lambda i: (0, _my_core() * nt + i, 0)),
                pl.BlockSpec((n, tr, c), lambda i: (0, i, 0))],
               [_sds((n, h, c), BF16)], [pl.BlockSpec((n, tr, c), lambda i: (0, i, 0))])[0]


def _sum_parts(name, sums, parts, full, layer, n_layer):
    _, h, c = sums.shape
    tr, nt = h // HALF_STEPS, HALF_STEPS

    def body(own, pb):
        acc = own.astype(F32)
        for j in range(pb.shape[0]):
            acc = acc + pb[j].astype(F32)
        return (acc,), ()

    ins = [sums, parts] + ([full] if full is not None else [])
    in_specs = [pl.BlockSpec((None, tr, c), lambda i: (_my_chip(), i, 0)),
                pl.BlockSpec((parts.shape[0], tr, c), lambda i: (0, i, 0))] + ([ANY] if full is not None else [])
    return _rw(name, body, (nt,), ins, in_specs, [_sds((n_layer, 2 * h, c), F32)],
               [pl.BlockSpec((None, tr, c), lambda i: (layer, _my_core() * nt + i, 0))],
               aliases={2: 0} if full is not None else None)[0]


def _adamw_math(w, g, m, v):
    m = B1 * m + (1.0 - B1) * g
    v = B2 * v + (1.0 - B2) * (g * g)
    m_hat = m / (1.0 - B1 ** STEP)
    v_hat = v / (1.0 - B2 ** STEP)
    delta = -LR * (m_hat / (jnp.sqrt(v_hat) + ADAM_EPS) + WD * w)
    return delta, m, v


def _adamw(name, w, g, m, v):
    r, c = w.shape
    tr = 64 if r % 64 == 0 else 8
    spec = _row_spec(tr, c)
    return _rw(name, lambda *b: (_adamw_math(*b), ()), (r // tr,), [w, g, m, v], [spec] * 4,
               [_sds((r, c), F32)] * 3, [spec] * 3)


SMALL = (("ffn1_norm", (DEPTH, D_MODEL)), ("mix_norm", (DEPTH, D_MODEL)), ("ffn2_norm", (DEPTH, D_MODEL)),
         ("conv_b", (DEPTH, SLAB)), ("conv_ln_g", (DEPTH, SLAB)), ("conv_ln_b", (DEPTH, SLAB)),
         ("ret_norm_g", (DEPTH, SLAB)), ("final_norm", (D_MODEL,)), ("conv_w", (DEPTH, CONV_W, SLAB)))


def _pack(parts, rows):
    flat = jnp.concatenate([p.reshape(-1) for p in parts])
    return jnp.pad(flat, (0, rows * 128 - flat.shape[0])).reshape(rows, 128)


def _unpack(packed, shapes):
    flat = packed.reshape(-1)
    out, off = [], 0
    for shp in shapes:
        n = int(np.prod(shp))
        out.append(flat[off:off + n].reshape(shp))
        off += n
    return out


def kernel(x, ffn1_norm, ffn1_w_in, ffn1_w_out, mix_norm, mix_w_in, conv_w, conv_b, conv_ln_g, conv_ln_b, ret_norm_g, mix_w_out, ffn2_norm, ffn2_w_in, ffn2_w_out, final_norm, loss_target, m_ffn1_norm, m_ffn1_w_in, m_ffn1_w_out, m_mix_norm, m_mix_w_in, m_conv_w, m_conv_b, m_conv_ln_g, m_conv_ln_b, m_ret_norm_g, m_mix_w_out, m_ffn2_norm, m_ffn2_w_in, m_ffn2_w_out, m_final_norm, v_ffn1_norm, v_ffn1_w_in, v_ffn1_w_out, v_mix_norm, v_mix_w_in, v_conv_w, v_conv_b, v_conv_ln_g, v_conv_ln_b, v_ret_norm_g, v_mix_w_out, v_ffn2_norm, v_ffn2_w_in, v_ffn2_w_out, v_final_norm):
    given = dict(locals())
    names = [n for n, _ in SMALL] + list(BIG)
    chip = 2 * lax.axis_index("x") + lax.axis_index("y")
    core = lax.axis_index("c")

    cw_rows = 128
    placed = lax.dynamic_update_slice(jnp.zeros((DEPTH, CONV_W, SLAB), F32), conv_w, (0, 0, chip * HEAD))
    placed = placed * (core == 0).astype(F32)
    conv_w_full = _unpack(_all_sum("gather_conv_w", _pack([placed], cw_rows)), [(DEPTH, CONV_W, SLAB)])[0]

    wt = dict(zip(BIG, _gather_weights([_cast_place("cast_" + n, given[n]) for n in BIG])))
    sm = {n: given[n] for n, _ in SMALL}
    sm["conv_w"] = conv_w_full
    loss, dx, big, small, dfinal = _local_step(x[0], loss_target[0], wt, sm)

    grads = [big[l][n] for n in BIG for l in range(DEPTH)]
    theirs = _pair_exchange(grads)
    sums = [_add_halves(f"chipsum{i}", a, b) for i, (a, b) in enumerate(zip(grads, theirs))]
    parts = _chip_exchange(sums)
    full = []
    for i in range(len(BIG)):
        f = None
        for l in range(DEPTH):
            f = _sum_parts(f"shardsum{DEPTH * i + l}", sums[DEPTH * i + l], parts[DEPTH * i + l], f, l, DEPTH)
        full.append(f)
    g_big = dict(zip(BIG, _pair_join(full)))

    small_parts = []
    for n, shp in SMALL:
        if n == "final_norm":
            small_parts.append(dfinal)
        else:
            small_parts.append(jnp.stack([small[l][n].reshape(shp[1:]) for l in range(DEPTH)]))
    g_small = dict(zip([n for n, _ in SMALL], _unpack(_all_sum("sum_small", _pack(small_parts, 200)), [s_ for _, s_ in SMALL])))
    g_small["conv_w"] = lax.dynamic_slice(g_small["conv_w"], (0, 0, chip * HEAD), (DEPTH, CONV_W, HEAD))

    grad, delta, new_m, new_v = dict(g_small), {}, {}, {}
    grad.update(g_big)
    for n in BIG:
        l, r, c = given[n].shape
        f = lambda t: t.reshape(l * r, c)
        d_, m_, v_ = _adamw("adamw_" + n, f(given[n]), f(grad[n]), f(given["m_" + n]), f(given["v_" + n]))
        delta[n], new_m[n], new_v[n] = d_.reshape(l, r, c), m_.reshape(l, r, c), v_.reshape(l, r, c)
    snames = [n for n, _ in SMALL]
    shapes = [given[n].shape for n in snames]
    rows = 104
    d_, m_, v_ = _adamw("adamw_small", _pack([given[n] for n in snames], rows), _pack([grad[n] for n in snames], rows),
                        _pack([given["m_" + n] for n in snames], rows), _pack([given["v_" + n] for n in snames], rows))
    for dst, packed in ((delta, d_), (new_m, m_), (new_v, v_)):
        dst.update(zip(snames, _unpack(packed, shapes)))

    total = lax.psum(loss[0, 0], ("x", "y", "c"))
    order = ["ffn1_norm", "ffn1_w_in", "ffn1_w_out", "mix_norm", "mix_w_in", "conv_w", "conv_b", "conv_ln_g",
             "conv_ln_b", "ret_norm_g", "mix_w_out", "ffn2_norm", "ffn2_w_in", "ffn2_w_out", "final_norm"]
    return (total, dx[None], *[grad[n] for n in order], *[delta[n] for n in order],
            *[new_m[n] for n in order], *[new_v[n] for n in order])
```

```python
import functools

import numpy as np
import jax
import jax.numpy as jnp
from jax import lax
from jax.experimental import pallas as pl
from jax.experimental.pallas import tpu as pltpu

F32 = jnp.float32
BF16 = jnp.bfloat16

D_MODEL = 1024
D_FF = 2816
N_SHARD = 4
FF_SHARD = 2 * D_FF // N_SHARD
MIX_SHARD = 3072 // N_SHARD
HEAD = 64
SLAB = 256
N_SLAB = 3072 // SLAB
CONV_W = 31
CONV_PAD = 32
CHUNK = 64
EPS = 1e-6
ROPE_BASE = 10000.0
DEPTH = 2

LR, B1, B2, ADAM_EPS, WD, STEP = 0.001, 0.9, 0.999, 1e-08, 0.01, 10

VMEM_LIMIT = 56 * 1024 * 1024


def _params(n_grid):
    return pltpu.CompilerParams(dimension_semantics=("arbitrary",) * n_grid, vmem_limit_bytes=VMEM_LIMIT)


def _rw(name, body, grid, ins, in_specs, rows=(), row_specs=(), accs=(), acc_specs=(), aliases=None):
    n_in, n_row = len(ins), len(rows)
    carried = sorted(aliases) if aliases else []

    def kern(*refs):
        vals = [r[...] for i, r in enumerate(refs[:n_in]) if i not in carried]
        row_vals, acc_vals = body(*vals)
        for r, v in zip(refs[n_in:n_in + n_row], row_vals):
            r[...] = v.astype(r.dtype)
        acc_refs = refs[n_in + n_row:]
        if acc_refs:
            first = functools.reduce(jnp.logical_and, [pl.program_id(a) == 0 for a in range(len(grid))])

            @pl.when(first)
            def _():
                for r in acc_refs:
                    r[...] = jnp.zeros(r.shape, r.dtype)

            for r, v in zip(acc_refs, acc_vals):
                r[...] += v.astype(r.dtype)

    return pl.pallas_call(
        kern, name=name, grid=grid, in_specs=list(in_specs), out_specs=list(row_specs) + list(acc_specs),
        out_shape=list(rows) + list(accs), input_output_aliases=dict(aliases or {}),
        compiler_params=_params(len(grid)))(*ins)


def _sds(shape, dtype):
    return jax.ShapeDtypeStruct(shape, dtype)


def _rms(x, g):
    return x * lax.rsqrt(jnp.mean(x * x, axis=-1, keepdims=True) + EPS) * g


def _row_spec(tm, c):
    return pl.BlockSpec((tm, c), lambda i: (i, 0))


def _vec_spec(c):
    return pl.BlockSpec((1, c), lambda i: (0, 0))


def _swiglu(gate, up):
    return jax.nn.silu(gate) * up


def _swiglu_fwd(name, u, tm):
    _, s, c = u.shape
    return _rw(name, lambda ub: ((_swiglu(ub[0:2].astype(F32), ub[2:4].astype(F32)),), ()), (s // tm,), [u],
               [pl.BlockSpec((4, tm, c), lambda i: (0, i, 0))],
               [_sds((2, s, c), BF16)], [pl.BlockSpec((2, tm, c), lambda i: (0, i, 0))])[0]


def _swiglu_bwd(name, u, da, tm):
    _, s, c = u.shape

    def body(ub, dab):
        _, vjp = jax.vjp(_swiglu, ub[0:2].astype(F32), ub[2:4].astype(F32))
        dg, du = vjp(dab.astype(F32))
        return (jnp.concatenate([dg, du], axis=0),), ()

    return _rw(name, body, (s // tm,), [u, da],
               [pl.BlockSpec((4, tm, c), lambda i: (0, i, 0)), pl.BlockSpec((2, tm, c), lambda i: (0, i, 0))],
               [_sds((4, s, c), BF16)], [pl.BlockSpec((4, tm, c), lambda i: (0, i, 0))])[0]


def _ln_silu(y, g, b):
    mu = jnp.mean(y, axis=-1, keepdims=True)
    yc = y - mu
    var = jnp.mean(yc * yc, axis=-1, keepdims=True)
    return jax.nn.silu(yc * lax.rsqrt(var + EPS) * g + b)


def _ln_silu_fwd(name, y, g, b, tm):
    s, c = y.shape
    return _rw(name, lambda yb, gb, bb: ((_ln_silu(yb, gb, bb),), ()), (s // tm,), [y, g, b],
               [_row_spec(tm, c), _vec_spec(c), _vec_spec(c)], [_sds((s, c), BF16)], [_row_spec(tm, c)])[0]


def _ln_silu_bwd(name, y, dcat, g, b, tm):
    s, c = y.shape

    def body(yb, dob, gb, bb):
        _, vjp = jax.vjp(_ln_silu, yb, gb, bb)
        dy, dg, db = vjp(dob)
        return (dy,), (dg, db)

    return _rw(name, body, (s // tm,), [y, dcat, g, b],
               [_row_spec(tm, c), pl.BlockSpec((None, tm, c), lambda i: (0, i, 0)), _vec_spec(c), _vec_spec(c)],
               [_sds((s, c), F32)], [_row_spec(tm, c)],
               [_sds((1, c), F32)] * 2, [_vec_spec(c)] * 2)


def _head_masks():
    lane = np.arange(SLAB) // HEAD
    m = np.zeros((8, SLAB), np.float32)
    for h in range(SLAB // HEAD):
        m[h] = (lane == h)
    return jnp.asarray(m)


def _gated_head_norm(y, gate, g, hm):
    mu = jnp.zeros_like(y)
    for h in range(SLAB // HEAD):
        mu = mu + hm[h:h + 1] * (jnp.sum(y * hm[h:h + 1], axis=-1, keepdims=True) / HEAD)
    yc = y - mu
    var = jnp.zeros_like(y)
    for h in range(SLAB // HEAD):
        var = var + hm[h:h + 1] * (jnp.sum(yc * yc * hm[h:h + 1], axis=-1, keepdims=True) / HEAD)
    return jax.nn.silu(gate) * (yc * lax.rsqrt(var + EPS) * g)


PER_SHARD = MIX_SHARD // SLAB


def _slab_spec(tm, j):
    return pl.BlockSpec((None, tm, SLAB), lambda i: (j, i, 0))


def _proj_slab_spec(tm, j):
    return pl.BlockSpec((None, tm, SLAB), lambda i: (j // PER_SHARD, i, j % PER_SHARD))


def _ghn_fwd(name, y, p32, g, tm):
    s, c = y.shape
    hm = _head_masks()
    return _rw(name, lambda yb, gb, wb, hb: ((_gated_head_norm(yb, gb, wb, hb),), ()), (s // tm,),
               [y, p32, g, hm],
               [_row_spec(tm, c), _proj_slab_spec(tm, 11), _vec_spec(c), pl.BlockSpec((8, c), lambda i: (0, 0))],
               [_sds((s, c), BF16)], [_row_spec(tm, c)])[0]


def _ghn_bwd(name, y, p32, dcat, g, tm):
    s, c = y.shape
    hm = _head_masks()

    def body(yb, gb, dob, wb, hb):
        _, vjp = jax.vjp(lambda a, b_, c_: _gated_head_norm(a, b_, c_, hb), yb, gb, wb)
        dy, dgate, dw = vjp(dob)
        return (dy, dgate), (dw,)

    return _rw(name, body, (s // tm,), [y, p32, dcat, g, hm],
               [_row_spec(tm, c), _proj_slab_spec(tm, 11), _slab_spec(tm, 3), _vec_spec(c),
                pl.BlockSpec((8, c), lambda i: (0, 0))],
               [_sds((s, c), F32)] * 2, [_row_spec(tm, c)] * 2,
               [_sds((1, c), F32)], [_vec_spec(c)])


def _assemble(name, parts, per, tm):
    s = parts[0][0].shape[-2]
    specs = [_row_spec(tm, SLAB) if j is None else pl.BlockSpec((None, tm, SLAB), lambda i, j=j: (j, i, 0))
             for _, j in parts]

    def body(*blocks):
        rows = [jnp.concatenate([b.astype(BF16) for b in blocks[per * q:per * (q + 1)]], axis=-1)
                for q in range(len(blocks) // per)]
        return (jnp.stack(rows),), ()

    nq = len(parts) // per
    return _rw(name, body, (s // tm,), [a for a, _ in parts], specs, [_sds((nq, s, per * SLAB), BF16)],
               [pl.BlockSpec((nq, tm, per * SLAB), lambda i: (0, i, 0))])[0]


def _final(name, x, tgt, g, tm):
    s, d = x.shape

    def body(xb, tb, gb):
        yf, vjp = jax.vjp(_rms, xb, gb)
        err = yf - tb
        dx, dg = vjp(err * (1.0 / d))
        part = 0.5 * jnp.sum(jnp.mean(err * err, axis=-1, keepdims=True), axis=0, keepdims=True)
        return (dx,), (dg, jnp.broadcast_to(part, (1, 128)))

    return _rw(name, body, (s // tm,), [x, tgt, g],
               [_row_spec(tm, d), _row_spec(tm, d), _vec_spec(d)],
               [_sds((s, d), F32)], [_row_spec(tm, d)],
               [_sds((1, d), F32), _sds((1, 128), F32)], [_vec_spec(d), _vec_spec(128)])


NN = (((1,), (0,)), ((), ()))
NT = (((1,), (1,)), ((), ()))
TN = (((0,), (0,)), ((), ()))


def _mm(name, a, b, grid, a_spec, b_spec, outs, out_specs, acc_shape, dims, alpha=1.0):
    nk = grid[-1]
    n_out = len(outs)

    def kern(*refs):
        a_ref, b_ref = refs[0], refs[1]
        o_refs = refs[2:2 + n_out]
        part = lax.dot_general(a_ref[...].astype(BF16), b_ref[...].astype(BF16), dims,
                               preferred_element_type=F32)

        def finish(r):
            if alpha != 1.0:
                r = r * alpha
            for o in o_refs:
                o[...] = r.astype(o.dtype)

        if nk == 1:
            finish(part)
            return
        acc_ref = refs[-1]
        k = pl.program_id(len(grid) - 1)

        @pl.when(k == 0)
        def _():
            acc_ref[...] = part

        @pl.when(jnp.logical_and(k > 0, k < nk - 1))
        def _():
            acc_ref[...] += part

        @pl.when(k == nk - 1)
        def _():
            finish(acc_ref[...] + part)

    return pl.pallas_call(
        kern, name=name, grid=grid, in_specs=[a_spec, b_spec], out_specs=list(out_specs), out_shape=list(outs),
        scratch_shapes=[pltpu.VMEM(acc_shape, F32)] if nk > 1 else [],
        compiler_params=_params(len(grid)))(a, b)


def _norm_proj_in(name, x, g, w, l, tm, dtypes):
    s, d = x.shape
    n = w.shape[-1]
    n_out = len(dtypes)

    def kern(x_ref, g_ref, w_ref, h_ref, *rest):
        o_refs, h_vmem = rest[:n_out], rest[n_out]

        @pl.when(pl.program_id(1) == 0)
        def _():
            h = _rms(x_ref[...], g_ref[...]).astype(BF16)
            h_vmem[...] = h
            h_ref[...] = h

        r = jnp.dot(h_vmem[...], w_ref[...], preferred_element_type=F32)
        for o in o_refs:
            o[...] = r.astype(o.dtype)

    out = pl.pallas_call(
        kern, name=name, grid=(s // tm, N_SHARD),
        in_specs=[pl.BlockSpec((tm, d), lambda i, b: (i, 0)), pl.BlockSpec((1, d), lambda i, b: (0, 0)),
                  pl.BlockSpec((None, None, d, n), lambda i, b: (l, b, 0, 0))],
        out_specs=[pl.BlockSpec((tm, d), lambda i, b: (i, 0))] +
                  [pl.BlockSpec((None, tm, n), lambda i, b: (b, i, 0))] * n_out,
        out_shape=[_sds((s, d), BF16)] + [_sds((N_SHARD, s, n), t) for t in dtypes],
        scratch_shapes=[pltpu.VMEM((tm, d), BF16)], compiler_params=_params(2))(x, g, w)
    return out[0], out[1:]


def _back_in_norm(name, du, w, l, tm, x, dres, g):
    nk, s, n = du.shape
    d = w.shape[2]

    def kern(du_ref, w_ref, x_ref, dres_ref, g_ref, dx_ref, dg_ref):
        dh = lax.dot_general(du_ref[0], w_ref[0], NT, preferred_element_type=F32)
        for k in range(1, nk):
            dh = dh + lax.dot_general(du_ref[k], w_ref[k], NT, preferred_element_type=F32)
        _, vjp = jax.vjp(_rms, x_ref[...], g_ref[...])
        dx, dg = vjp(dh)
        dx_ref[...] = dx + dres_ref[...]

        @pl.when(pl.program_id(0) == 0)
        def _():
            dg_ref[...] = dg

        @pl.when(pl.program_id(0) > 0)
        def _():
            dg_ref[...] += dg

    row = pl.BlockSpec((tm, d), lambda i: (i, 0))
    vec = pl.BlockSpec((1, d), lambda i: (0, 0))
    return pl.pallas_call(
        kern, name=name, grid=(s // tm,),
        in_specs=[pl.BlockSpec((nk, tm, n), lambda i: (0, i, 0)),
                  pl.BlockSpec((None, nk, d, n), lambda i: (l, 0, 0, 0)), row, row, vec],
        out_specs=[row, vec], out_shape=[_sds((s, d), F32), _sds((1, d), F32)],
        compiler_params=_params(1))(du, w, x, dres, g)


def _proj_out(name, a, w, l, res, alpha, tm):
    nk, s, r = a.shape
    d = w.shape[-1]

    def kern(a_ref, w_ref, res_ref, o_ref):
        y = jnp.dot(a_ref[0], w_ref[0], preferred_element_type=F32)
        for k in range(1, nk):
            y = y + jnp.dot(a_ref[k], w_ref[k], preferred_element_type=F32)
        o_ref[...] = res_ref[...] + (y * alpha if alpha != 1.0 else y)

    row = pl.BlockSpec((tm, d), lambda i: (i, 0))
    return pl.pallas_call(
        kern, name=name, grid=(s // tm,),
        in_specs=[pl.BlockSpec((nk, tm, r), lambda i: (0, i, 0)),
                  pl.BlockSpec((None, nk, r, d), lambda i: (l, 0, 0, 0)), row],
        out_specs=row, out_shape=_sds((s, d), F32), compiler_params=_params(1))(a, w, res)


def _back_out(name, dy, w, l, alpha, tm, out_dtype):
    s, d = dy.shape
    nk, r = w.shape[1], w.shape[2]
    return _mm(name, dy, w, (nk, s // tm, 1),
               pl.BlockSpec((tm, d), lambda b, i, k: (i, 0)),
               pl.BlockSpec((None, None, r, d), lambda b, i, k: (l, b, 0, 0)),
               [_sds((nk, s, r), out_dtype)], [pl.BlockSpec((None, tm, r), lambda b, i, k: (b, i, 0))],
               (tm, r), NT, alpha=alpha)[0]


def _grad_in(name, h, du, ts):
    s, d = h.shape
    nb, _, n = du.shape
    return _mm(name, h, du, (nb, 1, s // ts),
               pl.BlockSpec((ts, d), lambda b, i, k: (k, 0)),
               pl.BlockSpec((None, ts, n), lambda b, i, k: (b, k, 0)),
               [_sds((nb, d, n), BF16)], [pl.BlockSpec((None, d, n), lambda b, i, k: (b, 0, 0))],
               (d, n), TN)[0]


def _grad_out(name, a, dy, alpha, ts):
    nb, s, r = a.shape
    d = dy.shape[1]
    return _mm(name, a, dy, (nb, 1, s // ts),
               pl.BlockSpec((None, ts, r), lambda b, i, k: (b, k, 0)),
               pl.BlockSpec((ts, d), lambda b, i, k: (k, 0)),
               [_sds((nb, r, d), BF16)], [pl.BlockSpec((None, r, d), lambda b, i, k: (b, 0, 0))],
               (r, d), TN, alpha=alpha)[0]


CONV_TILE = 256


def _shifted(win, off, rows):
    n = win.shape[0]
    return pltpu.roll(win, (n - off) % n, 0)[0:rows] if off % n else win[0:rows]


def _conv_fwd(name, p32, w, bias):
    s = p32.shape[1]
    cb = 128
    nt = s // CONV_TILE

    def kern(a_ref, b_ref, w_ref, bias_ref, y_ref, vpad):
        vpad[0:CONV_PAD, :] = jnp.zeros((CONV_PAD, cb), F32)

        def fill(i, c):
            r = pl.multiple_of(i * CONV_TILE, CONV_TILE)
            vpad[pl.ds(CONV_PAD + r, CONV_TILE), :] = (
                a_ref[pl.ds(r, CONV_TILE), :] * jax.nn.sigmoid(b_ref[pl.ds(r, CONV_TILE), :]))
            return c

        lax.fori_loop(0, nt, fill, 0)

        def tile(i, c):
            r = pl.multiple_of(i * CONV_TILE, CONV_TILE)
            win = vpad[pl.ds(r, CONV_TILE + CONV_PAD), :]
            acc = jnp.broadcast_to(bias_ref[...], (CONV_TILE, cb))
            for j in range(CONV_W):
                acc = acc + w_ref[j:j + 1, :] * _shifted(win, j + 2, CONV_TILE)
            y_ref[pl.ds(r, CONV_TILE), :] = acc
            return c

        lax.fori_loop(0, nt, tile, 0)

    return pl.pallas_call(
        kern, name=name, grid=(SLAB // cb,),
        in_specs=[pl.BlockSpec((None, s, cb), lambda c: (0, 0, c)),
                  pl.BlockSpec((None, s, cb), lambda c: (0, 0, SLAB // cb + c)),
                  pl.BlockSpec((CONV_W, cb), lambda c: (0, c)),
                  pl.BlockSpec((1, cb), lambda c: (0, c))],
        out_specs=pl.BlockSpec((s, cb), lambda c: (0, c)),
        out_shape=_sds((s, SLAB), F32),
        scratch_shapes=[pltpu.VMEM((s + CONV_PAD, cb), F32)],
        compiler_params=_params(1))(p32, p32, w, bias)


def _conv_bwd(name, p32, w, dy):
    s = p32.shape[1]
    cb = 128
    nt = s // CONV_TILE

    def kern(a_ref, b_ref, w_ref, dy_ref, da_ref, db_ref, dw_ref, dbias_ref, vpad, dpad):
        vpad[0:CONV_PAD, :] = jnp.zeros((CONV_PAD, cb), F32)
        dpad[s:s + CONV_PAD, :] = jnp.zeros((CONV_PAD, cb), F32)
        dw_ref[...] = jnp.zeros((CONV_PAD, cb), F32)
        dbias_ref[...] = jnp.zeros((1, cb), F32)

        def fill(i, c):
            r = pl.multiple_of(i * CONV_TILE, CONV_TILE)
            vpad[pl.ds(CONV_PAD + r, CONV_TILE), :] = (
                a_ref[pl.ds(r, CONV_TILE), :] * jax.nn.sigmoid(b_ref[pl.ds(r, CONV_TILE), :]))
            dpad[pl.ds(r, CONV_TILE), :] = dy_ref[pl.ds(r, CONV_TILE), :]
            return c

        lax.fori_loop(0, nt, fill, 0)

        def tile(i, c):
            r = pl.multiple_of(i * CONV_TILE, CONV_TILE)
            dwin = dpad[pl.ds(r, CONV_TILE + CONV_PAD), :]
            vwin = vpad[pl.ds(r, CONV_TILE + CONV_PAD), :]
            dyt = dwin[0:CONV_TILE]
            dv = jnp.zeros((CONV_TILE, cb), F32)
            for j in range(CONV_W):
                dv = dv + w_ref[j:j + 1, :] * _shifted(dwin, CONV_W - 1 - j, CONV_TILE)
                dw_ref[j:j + 1, :] += jnp.sum(dyt * _shifted(vwin, j + 2, CONV_TILE), axis=0, keepdims=True)
            dbias_ref[...] += jnp.sum(dyt, axis=0, keepdims=True)
            a = a_ref[pl.ds(r, CONV_TILE), :]
            sg = jax.nn.sigmoid(b_ref[pl.ds(r, CONV_TILE), :])
            da_ref[pl.ds(r, CONV_TILE), :] = dv * sg
            db_ref[pl.ds(r, CONV_TILE), :] = dv * a * sg * (1.0 - sg)
            return c

        lax.fori_loop(0, nt, tile, 0)

    col = pl.BlockSpec((s, cb), lambda c: (0, c))
    return pl.pallas_call(
        kern, name=name, grid=(SLAB // cb,),
        in_specs=[pl.BlockSpec((None, s, cb), lambda c: (0, 0, c)),
                  pl.BlockSpec((None, s, cb), lambda c: (0, 0, SLAB // cb + c)),
                  pl.BlockSpec((CONV_W, cb), lambda c: (0, c)), col],
        out_specs=[col, col, pl.BlockSpec((CONV_PAD, cb), lambda c: (0, c)), pl.BlockSpec((1, cb), lambda c: (0, c))],
        out_shape=[_sds((s, SLAB), F32), _sds((s, SLAB), F32), _sds((CONV_PAD, SLAB), F32), _sds((1, SLAB), F32)],
        scratch_shapes=[pltpu.VMEM((s + CONV_PAD, cb), F32), pltpu.VMEM((s + CONV_PAD, cb), F32)],
        compiler_params=_params(1))(p32, p32, w, dy)


SB_BLOCK = 256
N_HEAD = SLAB // HEAD


def _sb_logits(qm, k, tri):
    z = lax.dot_general(qm, k, NT, preferred_element_type=F32)
    sign_bit = jnp.uint32(0x80000000)
    neg_abs = lax.bitcast_convert_type(lax.bitcast_convert_type(z, jnp.uint32) | sign_bit, F32)
    lb = jnp.minimum(z, 0.0) - jnp.log(1.0 + jnp.exp(neg_abs))
    ln = lb - z
    if tri is not None:
        ln = jnp.where(tri, ln, 0.0)
    return lb, ln


def _first_col(x):
    return jnp.broadcast_to(x[:, 0:1], (x.shape[0], 128))


def _head_stack(dst, x, lane_head, bq):
    for h in range(N_HEAD):
        dst[h * bq:(h + 1) * bq, :] = jnp.where(lane_head == h, x, jnp.zeros_like(x))


def _sb_fwd(name, p16):
    s = p16.shape[1]
    bq = min(SB_BLOCK, s)
    nq = s // bq

    def kern(q_ref, k_ref, v_ref, o_ref, w_hbm, lb_hbm, qm_ref, v4_refs, w4_refs, ws_ref, lbs_ref, acc_ref, r_ref,
             sem):
        g, qi = pl.program_id(0), pl.program_id(1)
        lane_head = lax.broadcasted_iota(jnp.int32, (1, SLAB), 1) // HEAD
        _head_stack(qm_ref, (q_ref[...].astype(F32) * (HEAD ** -0.5)).astype(BF16), lane_head, bq)
        row = lax.broadcasted_iota(jnp.int32, (bq, bq), 0)
        col = lax.broadcasted_iota(jnp.int32, (bq, bq), 1)
        after = (row > col).astype(BF16)
        tri = col < row
        acc_ref[...] = jnp.zeros((bq, SLAB), F32)
        r_ref[...] = jnp.zeros((N_HEAD, bq, 128), F32)

        def saves(slot, kb):
            return (pltpu.make_async_copy(ws_ref.at[slot], w_hbm.at[g, qi, kb], sem.at[0, slot]),
                    pltpu.make_async_copy(lbs_ref.at[slot], lb_hbm.at[g, qi, kb], sem.at[1, slot]))

        def tile(i, masked, u):
            kb, slot = qi - i, i % 4
            v4_ref, w4_ref = v4_refs.at[u], w4_refs.at[u]
            rows = pl.ds(pl.multiple_of(kb * bq, bq), bq)
            k = k_ref[rows, :]
            _head_stack(v4_ref, v_ref[rows, :], lane_head, bq)
            for h in range(N_HEAD):
                mine = pl.ds(h * bq, bq)
                lb, ln = _sb_logits(qm_ref[h * bq:(h + 1) * bq, :], k, tri if masked else None)
                rem = jnp.dot(ln.astype(BF16), after, preferred_element_type=F32)
                w = jnp.exp(lb + rem + r_ref[h][:, 0:1])
                if masked:
                    w = jnp.where(tri, w, 0.0)
                wb = w.astype(BF16)
                w4_ref[:, h * bq:(h + 1) * bq] = wb
                ws_ref[slot, mine, :] = wb
                lbs_ref[slot, mine, :] = lb.astype(BF16)
                r_ref[h] += _first_col(rem[:, 0:128] + ln[:, 0:128])
            acc_ref[...] += jnp.dot(w4_ref[...], v4_ref[...], preferred_element_type=F32)

        def save(i, start):
            for cp in saves(i % 4, qi - i):
                cp.start() if start else cp.wait()

        @pl.when(qi == 0)
        def _():
            tile(0, True, 0)
            save(0, True)
            save(0, False)

        @pl.when(qi >= 1)
        def _():
            tile(0, True, 0)
            tile(1, False, 1)
            save(0, True)
            save(1, True)

        def pair(j, c):
            tile(2 * j, False, 0)
            tile(2 * j + 1, False, 1)
            save(2 * j - 2, False)
            save(2 * j - 1, False)
            save(2 * j, True)
            save(2 * j + 1, True)
            return c

        n_pair = (qi + 1) // 2
        lax.fori_loop(1, n_pair, pair, 0)

        @pl.when(jnp.logical_and(qi >= 1, qi % 2 == 1))
        def _():
            save(qi - 1, False)
            save(qi, False)

        @pl.when(jnp.logical_and(qi >= 2, qi % 2 == 0))
        def _():
            tile(qi, False, 0)
            save(qi - 2, False)
            save(qi - 1, False)
            save(qi, True)
            save(qi, False)

        o_ref[...] = acc_ref[...]

    saved = _sds((2, nq, nq, N_HEAD * bq, bq), BF16)
    return pl.pallas_call(
        kern, name=name, grid=(2, nq),
        in_specs=[pl.BlockSpec((None, bq, SLAB), lambda g, i: ((2 + g) // PER_SHARD, i, (2 + g) % PER_SHARD)),
                  pl.BlockSpec((None, s, SLAB), lambda g, i: ((4 + g) // PER_SHARD, 0, (4 + g) % PER_SHARD)),
                  pl.BlockSpec((None, s, SLAB), lambda g, i: ((6 + g) // PER_SHARD, 0, (6 + g) % PER_SHARD))],
        out_specs=[pl.BlockSpec((None, bq, SLAB), lambda g, i: (g, i, 0)),
                   pl.BlockSpec(memory_space=pl.ANY), pl.BlockSpec(memory_space=pl.ANY)],
        out_shape=[_sds((2, s, SLAB), F32), saved, saved],
        scratch_shapes=[pltpu.VMEM((N_HEAD * bq, SLAB), BF16), pltpu.VMEM((2, N_HEAD * bq, SLAB), BF16),
                        pltpu.VMEM((2, bq, N_HEAD * bq), BF16), pltpu.VMEM((4, N_HEAD * bq, bq), BF16),
                        pltpu.VMEM((4, N_HEAD * bq, bq), BF16), pltpu.VMEM((bq, SLAB), F32),
                        pltpu.VMEM((N_HEAD, bq, 128), F32), pltpu.SemaphoreType.DMA((2, 4))],
        compiler_params=_params(2))(p16, p16, p16)


def _sb_bwd(name, p16, w_saved, lb_saved, dcat):
    s = p16.shape[1]
    bq = min(SB_BLOCK, s)
    nq = s // bq

    def kern(q_ref, k_ref, v_ref, do_ref, w_hbm, lb_hbm, dq_ref, dk_hbm, dv_hbm, dk_acc, dv_acc, dq_acc,
             qm_ref, dom_ref, k4_refs, dzc_refs, dzs_refs, ws_ref, lbs_ref, c_ref, sem, lsem):
        g, qi = pl.program_id(0), pl.program_id(1)

        @pl.when(qi == 0)
        def _():
            dk_acc[...] = jnp.zeros((s, SLAB), F32)
            dv_acc[...] = jnp.zeros((s, SLAB), F32)

        lane_head = lax.broadcasted_iota(jnp.int32, (1, SLAB), 1) // HEAD
        _head_stack(qm_ref, (q_ref[...].astype(F32) * (HEAD ** -0.5)).astype(BF16), lane_head, bq)
        _head_stack(dom_ref, do_ref[...].astype(BF16), lane_head, bq)
        row = lax.broadcasted_iota(jnp.int32, (bq, bq), 0)
        col = lax.broadcasted_iota(jnp.int32, (bq, bq), 1)
        earlier = (row < col).astype(BF16)
        tri = col < row
        dq_acc[...] = jnp.zeros((bq, SLAB), F32)
        c_ref[...] = jnp.zeros((N_HEAD, bq, 128), F32)

        def loads(kb):
            slot, kb = kb % 4, jnp.minimum(kb, qi)
            return (pltpu.make_async_copy(w_hbm.at[g, qi, kb], ws_ref.at[slot], lsem.at[0, slot]),
                    pltpu.make_async_copy(lb_hbm.at[g, qi, kb], lbs_ref.at[slot], lsem.at[1, slot]))

        def tile(kb, masked, u):
            slot = kb % 4
            k4_ref, dzc_ref, dzs_ref = k4_refs.at[u], dzc_refs.at[u], dzs_refs.at[u]
            rows = pl.ds(pl.multiple_of(kb * bq, bq), bq)
            k, v = k_ref[rows, :], v_ref[rows, :]
            _head_stack(k4_ref, k, lane_head, bq)
            for h in range(N_HEAD):
                mine = slice(h * bq, (h + 1) * bq)
                wb = ws_ref[slot, pl.ds(h * bq, bq), :]
                dl = wb.astype(F32) * lax.dot_general(dom_ref[mine, :], v, NT, preferred_element_type=F32)
                prefix = jnp.dot(dl.astype(BF16), earlier, preferred_element_type=F32)
                before = prefix + c_ref[h][:, 0:1]
                sig = jnp.exp(lbs_ref[slot, pl.ds(h * bq, bq), :].astype(F32))
                dz = dl - sig * (dl + before)
                if masked:
                    dz = jnp.where(tri, dz, 0.0)
                dzb = dz.astype(BF16)
                dzc_ref[:, mine] = dzb
                dzs_ref[mine, :] = dzb
                tail = prefix[:, bq - 128:] + dl[:, bq - 128:]
                c_ref[h] += jnp.broadcast_to(tail[:, 127:128], (bq, 128))
            dq_acc[...] += jnp.dot(dzc_ref[...], k4_ref[...], preferred_element_type=F32)
            dk_acc[rows, :] += lax.dot_general(dzs_ref[...], qm_ref[...], TN, preferred_element_type=F32)
            dv_acc[rows, :] += lax.dot_general(ws_ref[slot], dom_ref[...], TN, preferred_element_type=F32)

        for kb in (0, 1):
            for cp in loads(kb):
                cp.start()

        def pair(j, c):
            for kb in (2 * j + 2, 2 * j + 3):
                for cp in loads(kb):
                    cp.start()
            for kb in (2 * j, 2 * j + 1):
                for cp in loads(kb):
                    cp.wait()
            tile(2 * j, False, 0)
            tile(2 * j + 1, False, 1)
            return c

        lax.fori_loop(0, qi // 2, pair, 0)
        for kb in (qi - qi % 2, qi - qi % 2 + 1):
            for cp in loads(kb):
                cp.wait()

        @pl.when(qi % 2 == 1)
        def _():
            tile(qi - 1, False, 0)
            tile(qi, True, 1)

        @pl.when(qi % 2 == 0)
        def _():
            tile(qi, True, 0)

        dq_ref[...] = dq_acc[...] * (HEAD ** -0.5)

        @pl.when(qi == nq - 1)
        def _():
            ck = pltpu.make_async_copy(dk_acc, dk_hbm.at[g], sem.at[0])
            cv = pltpu.make_async_copy(dv_acc, dv_hbm.at[g], sem.at[1])
            ck.start()
            cv.start()
            ck.wait()
            cv.wait()

    blk = lambda j0: pl.BlockSpec((None, bq, SLAB), lambda g, i: (j0 + g, i, 0))
    full = lambda j0: pl.BlockSpec((None, s, SLAB), lambda g, i: ((j0 + g) // PER_SHARD, 0, (j0 + g) % PER_SHARD))
    q_blk = pl.BlockSpec((None, bq, SLAB), lambda g, i: ((2 + g) // PER_SHARD, i, (2 + g) % PER_SHARD))
    stack16 = pltpu.VMEM((N_HEAD * bq, SLAB), BF16)
    return pl.pallas_call(
        kern, name=name, grid=(2, nq),
        in_specs=[q_blk, full(4), full(6), blk(1),
                  pl.BlockSpec(memory_space=pl.ANY), pl.BlockSpec(memory_space=pl.ANY)],
        out_specs=[blk(0), pl.BlockSpec(memory_space=pl.ANY), pl.BlockSpec(memory_space=pl.ANY)],
        out_shape=[_sds((2, s, SLAB), F32)] * 3,
        scratch_shapes=[pltpu.VMEM((s, SLAB), F32), pltpu.VMEM((s, SLAB), F32), pltpu.VMEM((bq, SLAB), F32),
                        stack16, stack16, pltpu.VMEM((2, N_HEAD * bq, SLAB), BF16),
                        pltpu.VMEM((2, bq, N_HEAD * bq), BF16), pltpu.VMEM((2, N_HEAD * bq, bq), BF16),
                        pltpu.VMEM((4, N_HEAD * bq, bq), BF16), pltpu.VMEM((4, N_HEAD * bq, bq), BF16),
                        pltpu.VMEM((N_HEAD, bq, 128), F32),
                        pltpu.SemaphoreType.DMA((2,)), pltpu.SemaphoreType.DMA((2, 4))],
        compiler_params=_params(2))(p16, p16, p16, dcat, w_saved, lb_saved)


RET_BLOCK = 256


def _ret_tables(s, bl):
    nh = SLAB // HEAD
    lane_h = np.arange(SLAB) // HEAD
    log_gamma = np.log1p(-np.exp2(-5.0 - np.arange(nh, dtype=np.float64)))
    lg_lane = log_gamma[lane_h]
    half = HEAD // 2
    inv = 1.0 / (ROPE_BASE ** (np.arange(half, dtype=np.float64) / half))
    ang = np.arange(s, dtype=np.float64)[:, None] * inv[None, :]
    within = np.arange(SLAB) % HEAD
    cos = np.cos(ang)[:, within % half]
    sin = np.sin(ang)[:, within % half] * np.where(within < half, -1.0, 1.0)[None, :]
    perm = np.zeros((SLAB, SLAB))
    partner = np.where(within < half, np.arange(SLAB) + half, np.arange(SLAB) - half)
    perm[partner, np.arange(SLAB)] = 1.0
    i = np.arange(bl)
    diff = i[:, None] - i[None, :]
    same = (i[:, None] // CHUNK) == (i[None, :] // CHUNK)
    earlier = (i[None, :] // CHUNK) < (i[:, None] // CHUNK)
    decay = np.zeros((nh, bl, bl))
    for h in range(nh):
        decay[h] = np.where(same, np.exp(log_gamma[h] * np.abs(diff)),
                            np.where(earlier, np.exp(log_gamma[h] * diff), 0.0))
    qd = np.exp(lg_lane[None, :] * (i[:, None] + 1.0))
    kd = np.exp(lg_lane[None, :] * (bl - 1.0 - i[:, None]))
    gam = np.exp(lg_lane * bl)[:, None] * np.ones((1, SLAB))
    bd = (lane_h[:, None] == lane_h[None, :]).astype(np.float64)
    f = lambda a: jnp.asarray(a, F32)
    return f(cos), f(sin), f(perm), f(decay), f(qd), f(kd), f(gam), f(bd)


def _ret_block(q, k, v, state, cos, sin, perm, decay, qd, kd, gam, bd, hm):
    qr = (q * cos + jnp.dot(q, perm, preferred_element_type=F32) * sin) * (HEAD ** -0.5)
    kr = k * cos + jnp.dot(k, perm, preferred_element_type=F32) * sin
    y = jnp.dot(qr * qd, state, preferred_element_type=F32)
    for h in range(SLAB // HEAD):
        m = hm[h:h + 1]
        sc = lax.dot_general(qr * m, kr, NT, preferred_element_type=F32) * decay[h]
        y = y + jnp.dot(sc, v * m, preferred_element_type=F32)
    new_state = gam * state + lax.dot_general(kr * kd, v, TN, preferred_element_type=F32) * bd
    return y, new_state


def _ret_specs(s, bl, rev):
    nb = s // bl
    pos = (lambda n: nb - 1 - n) if rev else (lambda n: n)
    slab = lambda j: pl.BlockSpec((None, bl, SLAB), lambda n: (j // PER_SHARD, pos(n), j % PER_SHARD))
    const2 = lambda r: pl.BlockSpec((r, SLAB), lambda n: (0, 0))
    tab = [pl.BlockSpec((bl, SLAB), lambda n: (pos(n), 0))] * 2 + [
        const2(SLAB), pl.BlockSpec((SLAB // HEAD, bl, bl), lambda n: (0, 0, 0)),
        const2(bl), const2(bl), const2(SLAB), const2(SLAB), const2(8)]
    return nb, pos, slab, tab


def _ret_fwd(name, p32):
    s = p32.shape[1]
    bl = min(RET_BLOCK, s)
    nb, pos, slab, tab = _ret_specs(s, bl, False)
    tables = _ret_tables(s, bl) + (_head_masks(),)

    def kern(q_ref, k_ref, v_ref, *rest):
        t_refs, (y_ref, st_ref, state) = rest[:9], rest[9:]

        @pl.when(pl.program_id(0) == 0)
        def _():
            state[...] = jnp.zeros((SLAB, SLAB), F32)

        st_ref[...] = state[...]
        y, new = _ret_block(q_ref[...], k_ref[...], v_ref[...], state[...], *[t[...] for t in t_refs])
        y_ref[...] = y
        state[...] = new

    return pl.pallas_call(
        kern, name=name, grid=(nb,), in_specs=[slab(8), slab(9), slab(10)] + tab,
        out_specs=[pl.BlockSpec((bl, SLAB), lambda n: (n, 0)), pl.BlockSpec((None, SLAB, SLAB), lambda n: (n, 0, 0))],
        out_shape=[_sds((s, SLAB), F32), _sds((nb, SLAB, SLAB), F32)],
        scratch_shapes=[pltpu.VMEM((SLAB, SLAB), F32)], compiler_params=_params(1))(p32, p32, p32, *tables)


def _ret_bwd(name, p32, states, dy):
    s = p32.shape[1]
    bl = min(RET_BLOCK, s)
    nb, pos, slab, tab = _ret_specs(s, bl, True)
    tables = _ret_tables(s, bl) + (_head_masks(),)
    rowblk = pl.BlockSpec((bl, SLAB), lambda n: (pos(n), 0))

    def kern(q_ref, k_ref, v_ref, st_ref, dy_ref, *rest):
        t_refs, (dq_ref, dk_ref, dv_ref, dstate) = rest[:9], rest[9:]

        @pl.when(pl.program_id(0) == 0)
        def _():
            dstate[...] = jnp.zeros((SLAB, SLAB), F32)

        tv = [t[...] for t in t_refs]
        _, vjp = jax.vjp(lambda a, b, c, d: _ret_block(a, b, c, d, *tv),
                         q_ref[...], k_ref[...], v_ref[...], st_ref[...])
        dq, dk, dv, ds = vjp((dy_ref[...], dstate[...]))
        dq_ref[...] = dq
        dk_ref[...] = dk
        dv_ref[...] = dv
        dstate[...] = ds

    return pl.pallas_call(
        kern, name=name, grid=(nb,),
        in_specs=[slab(8), slab(9), slab(10), pl.BlockSpec((None, SLAB, SLAB), lambda n: (pos(n), 0, 0)), rowblk] + tab,
        out_specs=[rowblk] * 3, out_shape=[_sds((s, SLAB), F32)] * 3,
        scratch_shapes=[pltpu.VMEM((SLAB, SLAB), F32)], compiler_params=_params(1))(p32, p32, p32, states, dy, *tables)


TM_FFN = 1024
TM_SLAB = 2048
TM_NORM = 256
TM_OUT = 512
TM_GRAD = 2048
TM_RW = 256
TM_FF = 256


def _ffn_fwd(tag, x, g, w_in, w_out, l):
    tm = min(TM_FFN, x.shape[0])
    h, (u,) = _norm_proj_in(tag + "_in", x, g, w_in, l, tm, [BF16])
    a = _swiglu_fwd(tag + "_act", u, TM_FF)
    w_out2 = w_out.reshape(DEPTH, 2, FF_SHARD, D_MODEL)
    xn = _proj_out(tag + "_out", a, w_out2, l, x, 0.5, min(TM_OUT, x.shape[0]))
    return xn, (x, h, u, a)


def _ffn_bwd(tag, saved, dxn, g, w_in, w_out, l):
    x, h, u, a = saved
    tm = min(TM_FFN, x.shape[0])
    w_out2 = w_out.reshape(DEPTH, 2, FF_SHARD, D_MODEL)
    da = _back_out(tag + "_dact", dxn, w_out2, l, 0.5, tm, BF16)
    tk = min(TM_GRAD, x.shape[0])
    dw_out = _grad_out(tag + "_dwout", a, dxn, 0.5, tk)
    du = _swiglu_bwd(tag + "_dswi", u, da, TM_FF)
    dx, dg = _back_in_norm(tag + "_dh", du, w_in, l, min(TM_NORM, x.shape[0]), x, dxn, g)
    dw_in = _grad_in(tag + "_dwin", h, du, tk)
    return dx, dg, dw_in, dw_out.reshape(N_SHARD, D_FF // N_SHARD, D_MODEL)


def _mix_fwd(tag, x, sm, w_in, w_out, l):
    h, (p32, p16) = _norm_proj_in(tag + "_in", x, sm["mix_norm"][l:l + 1], w_in, l, min(TM_FFN, x.shape[0]),
                                  [F32, BF16])
    ypre = _conv_fwd(tag + "_conv", p32, sm["conv_w"][l], sm["conv_b"][l:l + 1])
    yconv = _ln_silu_fwd(tag + "_ln", ypre, sm["conv_ln_g"][l:l + 1], sm["conv_ln_b"][l:l + 1], TM_RW)
    osb, w_sb, lb_sb = _sb_fwd(tag + "_sb", p16)
    yr, states = _ret_fwd(tag + "_ret", p32)
    yret = _ghn_fwd(tag + "_ghn", yr, p32, sm["ret_norm_g"][l:l + 1], TM_RW)
    ycat = _assemble(tag + "_cat", [(yconv, None), (osb, 0), (osb, 1), (yret, None)], 1, TM_RW)
    xn = _proj_out(tag + "_out", ycat, w_out, l, x, 1.0, min(TM_OUT, x.shape[0]))
    return xn, (x, h, p32, p16, ypre, (osb, w_sb, lb_sb), yr, states, ycat)


def _mix_bwd(tag, saved, dxn, sm, w_in, w_out, l):
    x, h, p32, p16, ypre, osb, yr, states, ycat = saved
    ts = min(TM_SLAB, x.shape[0])
    dcat = _back_out(tag + "_dcat", dxn, w_out, l, 1.0, ts, F32)
    dw_out = _grad_out(tag + "_dwout", ycat, dxn, 1.0, ts)
    dypre, dlg, dlb = _ln_silu_bwd(tag + "_dln", ypre, dcat, sm["conv_ln_g"][l:l + 1], sm["conv_ln_b"][l:l + 1], TM_RW)
    da, db, dcw, dcb = _conv_bwd(tag + "_dconv", p32, sm["conv_w"][l], dypre)
    dq, dk, dv = _sb_bwd(tag + "_dsb", p16, osb[1], osb[2], dcat)
    dyr, dgate, drg = _ghn_bwd(tag + "_dghn", yr, p32, dcat, sm["ret_norm_g"][l:l + 1], TM_RW)
    dqr, dkr, dvr = _ret_bwd(tag + "_dret", p32, states, dyr)
    dp = _assemble(tag + "_dp", [(da, None), (db, None), (dq, 0), (dq, 1), (dk, 0), (dk, 1), (dv, 0), (dv, 1),
                                 (dqr, None), (dkr, None), (dvr, None), (dgate, None)], PER_SHARD, TM_RW)
    dx, dg = _back_in_norm(tag + "_dh", dp, w_in, l, min(TM_NORM, x.shape[0]), x, dxn, sm["mix_norm"][l:l + 1])
    dw_in = _grad_in(tag + "_dwin", h, dp, min(TM_GRAD, x.shape[0]))
    small = dict(mix_norm=dg, conv_w=dcw[0:CONV_W], conv_b=dcb, conv_ln_g=dlg, conv_ln_b=dlb, ret_norm_g=drg)
    return dx, small, dw_in, dw_out


def _local_step(x, tgt, wt, sm, on_last_layer=None):
    saved = []
    for l in range(DEPTH):
        x, s1 = _ffn_fwd(f"l{l}f1", x, sm["ffn1_norm"][l:l + 1], wt["ffn1_w_in"], wt["ffn1_w_out"], l)
        x, s2 = _mix_fwd(f"l{l}mx", x, sm, wt["mix_w_in"], wt["mix_w_out"], l)
        x, s3 = _ffn_fwd(f"l{l}f2", x, sm["ffn2_norm"][l:l + 1], wt["ffn2_w_in"], wt["ffn2_w_out"], l)
        saved.append((s1, s2, s3))
    dx, dfinal, loss = _final("final", x, tgt, sm["final_norm"][None, :], TM_RW)
    big = [None] * DEPTH
    small = [None] * DEPTH
    for l in reversed(range(DEPTH)):
        s1, s2, s3 = saved[l]
        dx, dg3, dwi3, dwo3 = _ffn_bwd(f"l{l}f2", s3, dx, sm["ffn2_norm"][l:l + 1], wt["ffn2_w_in"], wt["ffn2_w_out"], l)
        dx, sml, dwi2, dwo2 = _mix_bwd(f"l{l}mx", s2, dx, sm, wt["mix_w_in"], wt["mix_w_out"], l)
        dx, dg1, dwi1, dwo1 = _ffn_bwd(f"l{l}f1", s1, dx, sm["ffn1_norm"][l:l + 1], wt["ffn1_w_in"], wt["ffn1_w_out"], l)
        big[l] = dict(ffn1_w_in=dwi1, ffn1_w_out=dwo1, mix_w_in=dwi2, mix_w_out=dwo2, ffn2_w_in=dwi3, ffn2_w_out=dwo3)
        sml.update(ffn1_norm=dg1, ffn2_norm=dg3)
        small[l] = sml
        if on_last_layer is not None and l == DEPTH - 1:
            dx = dx + on_last_layer(big[l])[0:1, 0:1]
    return loss, dx, big, small, dfinal


MESH = pl.DeviceIdType.MESH
ANY = pl.BlockSpec(memory_space=pl.ANY)
BIG = ("ffn1_w_in", "ffn1_w_out", "mix_w_in", "mix_w_out", "ffn2_w_in", "ffn2_w_out")


def _place():
    x, y, c = lax.axis_index("x"), lax.axis_index("y"), lax.axis_index("c")
    chips = [(1 - x, y), (x, 1 - y), (1 - x, 1 - y)]
    return x, y, c, chips


def _gather_weights(w16):
    n = len(w16)

    def kern(*refs):
        dst = refs[n:2 * n]
        send, recv = refs[2 * n:]
        x, y, c, chips = _place()
        mine = 2 * x + y
        firsts, passes = [], []
        for a in range(n):
            h = dst[a].shape[2] // 2
            own = dst[a].at[:, mine, pl.ds(c * h, h)]
            for j, (cx, cy) in enumerate(chips):
                cp = pltpu.make_async_remote_copy(
                    src_ref=own, dst_ref=own, send_sem=send.at[6 * a + j], recv_sem=recv.at[6 * a + j],
                    device_id=(cx, cy, c), device_id_type=MESH)
                cp.start()
                firsts.append(cp)
        for a in range(n):
            h = dst[a].shape[2] // 2
            half = pl.ds(c * h, h)
            for j, (cx, cy) in enumerate(chips):
                theirs = dst[a].at[:, 2 * cx + cy, half]
                pltpu.make_async_remote_copy(
                    src_ref=theirs, dst_ref=theirs, send_sem=send.at[6 * a + j], recv_sem=recv.at[6 * a + j],
                    device_id=(cx, cy, c), device_id_type=MESH).wait_recv()
                fw = pltpu.make_async_remote_copy(
                    src_ref=theirs, dst_ref=theirs, send_sem=send.at[6 * a + 3 + j], recv_sem=recv.at[6 * a + 3 + j],
                    device_id=(x, y, 1 - c), device_id_type=MESH)
                fw.start()
                passes.append(fw)
        for a in range(n):
            h = dst[a].shape[2] // 2
            other = pl.ds((1 - c) * h, h)
            for j, (cx, cy) in enumerate(chips):
                got = dst[a].at[:, 2 * cx + cy, other]
                pltpu.make_async_remote_copy(
                    src_ref=got, dst_ref=got, send_sem=send.at[6 * a + 3 + j], recv_sem=recv.at[6 * a + 3 + j],
                    device_id=(x, y, 1 - c), device_id_type=MESH).wait_recv()
        for cp in firsts + passes:
            cp.wait_send()

    return pl.pallas_call(
        kern, name="gather_weights", in_specs=[ANY] * n, out_specs=[ANY] * n,
        out_shape=[_sds(w.shape, w.dtype) for w in w16], input_output_aliases={a: a for a in range(n)},
        scratch_shapes=[pltpu.SemaphoreType.DMA((6 * n,)), pltpu.SemaphoreType.DMA((6 * n,))])(*w16)


def _pair_exchange(name, grads):
    n = len(grads)

    def kern(*refs):
        src, got_o = refs[:n], refs[n:2 * n]
        send, recv = refs[2 * n:]
        x, y, c, _ = _place()
        cps = []
        for a in range(n):
            h = src[a].shape[1] // 2
            cp = pltpu.make_async_remote_copy(
                src_ref=src[a].at[:, pl.ds((1 - c) * h, h)], dst_ref=got_o[a],
                send_sem=send.at[a], recv_sem=recv.at[a], device_id=(x, y, 1 - c), device_id_type=MESH)
            cp.start()
            cps.append(cp)
        for cp in cps:
            cp.wait()

    halves = [_sds((g.shape[0], g.shape[1] // 2, g.shape[2]), g.dtype) for g in grads]
    return pl.pallas_call(
        kern, name=name, in_specs=[ANY] * n, out_specs=[ANY] * n, out_shape=halves,
        scratch_shapes=[pltpu.SemaphoreType.DMA((n,)), pltpu.SemaphoreType.DMA((n,))])(*grads)


def _chip_exchange(name, sums):
    n = len(sums)

    def kern(*refs):
        src, dst = refs[:n], refs[n:2 * n]
        send, recv = refs[2 * n:]
        x, y, c, chips = _place()
        cps = _chip_copies(src, dst, send, recv, x, y, c, chips)
        for cp in cps:
            cp.start()
        for cp in cps:
            cp.wait()

    return pl.pallas_call(
        kern, name=name, in_specs=[ANY] * n, out_specs=[ANY] * n,
        out_shape=[_sds((3,) + s_.shape[1:], s_.dtype) for s_ in sums],
        scratch_shapes=[pltpu.SemaphoreType.DMA((3 * n,)), pltpu.SemaphoreType.DMA((3 * n,))])(*sums)


HBM = pl.BlockSpec(memory_space=pltpu.HBM)
SEM = pl.BlockSpec(memory_space=pltpu.SEMAPHORE)
EFFECT = pltpu.SideEffectType.DATAFLOW_SIDE_EFFECTING


def _chip_copies(src, land, send, recv, x, y, c, chips):
    return [pltpu.make_async_remote_copy(
        src_ref=src[a].at[2 * cx + cy], dst_ref=land[a].at[j], send_sem=send.at[3 * a + j],
        recv_sem=recv.at[3 * a + j], device_id=(cx, cy, c), device_id_type=MESH)
        for a in range(len(src)) for j, (cx, cy) in enumerate(chips)]


def _chip_exchange_start(sums):
    n = len(sums)

    def kern(*refs):
        src, land = refs[:n], refs[n:2 * n]
        send, recv, token = refs[2 * n], refs[2 * n + 1], refs[-1]
        x, y, c, chips = _place()
        for cp in _chip_copies(src, land, send, recv, x, y, c, chips):
            cp.start()
        token[...] = jnp.zeros_like(token)

    lands = [(3,) + s_.shape[1:] for s_ in sums]
    out = pl.pallas_call(
        kern, name="chip_exchange_start",
        out_shape=(pltpu.SemaphoreType.DMA((3 * n,)), pltpu.SemaphoreType.DMA((3 * n,)),
                   *[pltpu.HBM(s_.shape, s_.dtype) for s_ in sums],
                   *[pltpu.HBM(shp, s_.dtype) for shp, s_ in zip(lands, sums)], _sds((8, 128), F32)),
        in_specs=[HBM] * (2 * n),
        out_specs=(SEM, SEM, *[HBM] * (2 * n), pl.BlockSpec(memory_space=pltpu.VMEM)),
        input_output_aliases={i: 2 + i for i in range(2 * n)},
        compiler_params=pltpu.CompilerParams(has_side_effects=EFFECT),
    )(*[pltpu.with_memory_space_constraint(s_, pltpu.HBM) for s_ in sums],
      *[pltpu.with_memory_space_constraint(lax.empty(shp, s_.dtype), pltpu.HBM) for shp, s_ in zip(lands, sums)])
    return out[0], out[1], out[2:2 + n], out[2 + n:2 + 2 * n], out[-1]


def _chip_exchange_wait(send, recv, sums, lands, after):
    n = len(sums)

    def kern(*refs):
        src, land = refs[:n], refs[n:2 * n]
        send_sem, recv_sem = refs[2 * n], refs[2 * n + 1]
        x, y, c, chips = _place()
        for cp in _chip_copies(src, land, send_sem, recv_sem, x, y, c, chips):
            cp.wait_send()
            cp.wait_recv()

    out = pl.pallas_call(
        kern, name="chip_exchange_wait",
        out_shape=tuple(pltpu.HBM(t.shape, t.dtype) for t in list(sums) + list(lands)),
        in_specs=[HBM] * (2 * n) + [SEM, SEM, ANY], out_specs=tuple([HBM] * (2 * n)),
        input_output_aliases={i: i for i in range(2 * n)},
        compiler_params=pltpu.CompilerParams(has_side_effects=EFFECT),
    )(*sums, *lands, send, recv, after)
    return out[:n], out[n:]


def _pair_join(full):
    n = len(full)

    def kern(*refs):
        dst = refs[n:2 * n]
        send, recv = refs[2 * n:]
        x, y, c, _ = _place()
        cps = []
        for a in range(n):
            h = dst[a].shape[1] // 2
            mine = dst[a].at[:, pl.ds(c * h, h)]
            cp = pltpu.make_async_remote_copy(
                src_ref=mine, dst_ref=mine, send_sem=send.at[a], recv_sem=recv.at[a],
                device_id=(x, y, 1 - c), device_id_type=MESH)
            cp.start()
            cps.append(cp)
        for a, cp in enumerate(cps):
            cp.wait_send()
            h = dst[a].shape[1] // 2
            got = dst[a].at[:, pl.ds((1 - c) * h, h)]
            pltpu.make_async_remote_copy(
                src_ref=got, dst_ref=got, send_sem=send.at[a], recv_sem=recv.at[a],
                device_id=(x, y, 1 - c), device_id_type=MESH).wait_recv()

    return pl.pallas_call(
        kern, name="pair_join", in_specs=[ANY] * n, out_specs=[ANY] * n,
        out_shape=[_sds(f.shape, f.dtype) for f in full], input_output_aliases={a: a for a in range(n)},
        scratch_shapes=[pltpu.SemaphoreType.DMA((n,)), pltpu.SemaphoreType.DMA((n,))])(*full)


def _all_sum(name, v):
    r = v.shape[0]

    def kern(v_ref, o_ref, buf, send, recv):
        x, y, c, _ = _place()
        me = 4 * x + 2 * y + c
        buf[me] = v_ref[...]
        cps = []
        for k in range(1, 8):
            peer = (x ^ (k >> 2), y ^ ((k >> 1) & 1), c ^ (k & 1))
            cp = pltpu.make_async_remote_copy(
                src_ref=v_ref, dst_ref=buf.at[me], send_sem=send.at[k - 1], recv_sem=recv.at[k - 1],
                device_id=peer, device_id_type=MESH)
            cp.start()
            cps.append(cp)
        for k in range(1, 8):
            peer_id = me ^ k
            pltpu.make_async_remote_copy(
                src_ref=v_ref, dst_ref=buf.at[peer_id], send_sem=send.at[k - 1], recv_sem=recv.at[k - 1],
                device_id=(x, y, c), device_id_type=MESH).wait_recv()
        for cp in cps:
            cp.wait_send()
        acc = buf[0]
        for d in range(1, 8):
            acc = acc + buf[d]
        o_ref[...] = acc

    vm = pl.BlockSpec(memory_space=pltpu.VMEM)
    return pl.pallas_call(
        kern, name=name, in_specs=[vm], out_specs=vm, out_shape=_sds((r, 128), F32),
        scratch_shapes=[pltpu.VMEM((8, r, 128), F32), pltpu.SemaphoreType.DMA((7,)),
                        pltpu.SemaphoreType.DMA((7,))])(v)


def _my_chip():
    return 2 * lax.axis_index("x") + lax.axis_index("y")


def _my_core():
    return lax.axis_index("c")


def _cast_place(name, w):
    l, r, c = w.shape
    tr = r // 4
    return _rw(name, lambda wb: ((wb,), ()), (l, r // tr), [w],
               [pl.BlockSpec((None, tr, c), lambda j, i: (j, i, 0))],
               [_sds((l, N_SHARD, r, c), BF16)],
               [pl.BlockSpec((None, None, tr, c), lambda j, i: (j, _my_chip(), i, 0))])[0]


HALF_STEPS = 2


def _add_halves(name, g, got):
    n, h, c = got.shape
    tr, nt = h // HALF_STEPS, HALF_STEPS
    return _rw(name, lambda ab, bb: ((ab.astype(F32) + bb.astype(F32),), ()), (nt,), [g, got],
               [pl.BlockSpec((n, tr, c), lambda i: (0, _my_core() * nt + i, 0)),
                pl.BlockSpec((n, tr, c), lambda i: (0, i, 0))],
               [_sds((n, h, c), BF16)], [pl.BlockSpec((n, tr, c), lambda i: (0, i, 0))])[0]


def _sum_parts(name, sums, parts, full, layer, n_layer):
    _, h, c = sums.shape
    tr, nt = h // HALF_STEPS, HALF_STEPS

    def body(own, pb):
        acc = own.astype(F32)
        for j in range(pb.shape[0]):
            acc = acc + pb[j].astype(F32)
        return (acc,), ()

    ins = [sums, parts] + ([full] if full is not None else [])
    in_specs = [pl.BlockSpec((None, tr, c), lambda i: (_my_chip(), i, 0)),
                pl.BlockSpec((parts.shape[0], tr, c), lambda i: (0, i, 0))] + ([ANY] if full is not None else [])
    return _rw(name, body, (nt,), ins, in_specs, [_sds((n_layer, 2 * h, c), F32)],
               [pl.BlockSpec((None, tr, c), lambda i: (layer, _my_core() * nt + i, 0))],
               aliases={2: 0} if full is not None else None)[0]


def _adamw_math(w, g, m, v):
    m = B1 * m + (1.0 - B1) * g
    v = B2 * v + (1.0 - B2) * (g * g)
    m_hat = m / (1.0 - B1 ** STEP)
    v_hat = v / (1.0 - B2 ** STEP)
    delta = -LR * (m_hat / (jnp.sqrt(v_hat) + ADAM_EPS) + WD * w)
    return delta, m, v


def _adamw(name, w, g, m, v):
    r, c = w.shape
    tr = 64 if r % 64 == 0 else 8
    spec = _row_spec(tr, c)
    return _rw(name, lambda *b: (_adamw_math(*b), ()), (r // tr,), [w, g, m, v], [spec] * 4,
               [_sds((r, c), F32)] * 3, [spec] * 3)


SMALL = (("ffn1_norm", (DEPTH, D_MODEL)), ("mix_norm", (DEPTH, D_MODEL)), ("ffn2_norm", (DEPTH, D_MODEL)),
         ("conv_b", (DEPTH, SLAB)), ("conv_ln_g", (DEPTH, SLAB)), ("conv_ln_b", (DEPTH, SLAB)),
         ("ret_norm_g", (DEPTH, SLAB)), ("final_norm", (D_MODEL,)), ("conv_w", (DEPTH, CONV_W, SLAB)))


def _pack(parts, rows):
    flat = jnp.concatenate([p.reshape(-1) for p in parts])
    return jnp.pad(flat, (0, rows * 128 - flat.shape[0])).reshape(rows, 128)


def _unpack(packed, shapes):
    flat = packed.reshape(-1)
    out, off = [], 0
    for shp in shapes:
        n = int(np.prod(shp))
        out.append(flat[off:off + n].reshape(shp))
        off += n
    return out


def kernel(x, ffn1_norm, ffn1_w_in, ffn1_w_out, mix_norm, mix_w_in, conv_w, conv_b, conv_ln_g, conv_ln_b, ret_norm_g, mix_w_out, ffn2_norm, ffn2_w_in, ffn2_w_out, final_norm, loss_target, m_ffn1_norm, m_ffn1_w_in, m_ffn1_w_out, m_mix_norm, m_mix_w_in, m_conv_w, m_conv_b, m_conv_ln_g, m_conv_ln_b, m_ret_norm_g, m_mix_w_out, m_ffn2_norm, m_ffn2_w_in, m_ffn2_w_out, m_final_norm, v_ffn1_norm, v_ffn1_w_in, v_ffn1_w_out, v_mix_norm, v_mix_w_in, v_conv_w, v_conv_b, v_conv_ln_g, v_conv_ln_b, v_ret_norm_g, v_mix_w_out, v_ffn2_norm, v_ffn2_w_in, v_ffn2_w_out, v_final_norm):
    given = dict(locals())
    names = [n for n, _ in SMALL] + list(BIG)
    chip = 2 * lax.axis_index("x") + lax.axis_index("y")
    core = lax.axis_index("c")

    cw_rows = 128
    placed = lax.dynamic_update_slice(jnp.zeros((DEPTH, CONV_W, SLAB), F32), conv_w, (0, 0, chip * HEAD))
    placed = placed * (core == 0).astype(F32)
    conv_w_full = _unpack(_all_sum("gather_conv_w", _pack([placed], cw_rows)), [(DEPTH, CONV_W, SLAB)])[0]

    wt = dict(zip(BIG, _gather_weights([_cast_place("cast_" + n, given[n]) for n in BIG])))
    sm = {n: given[n] for n, _ in SMALL}
    sm["conv_w"] = conv_w_full
    def chip_sums(tag, layer_grads):
        grads = [layer_grads[n] for n in BIG]
        theirs = _pair_exchange("pair_exchange_" + tag, grads)
        return [_add_halves(f"chipsum_{tag}{i}", a, b) for i, (a, b) in enumerate(zip(grads, theirs))]

    in_flight = []

    def start_last(layer_grads):
        in_flight.extend(_chip_exchange_start(chip_sums("last", layer_grads)))
        return in_flight[4]

    loss, dx, big, small, dfinal = _local_step(x[0], loss_target[0], wt, sm, start_last)
    sums, parts = [None] * DEPTH, [None] * DEPTH
    sums[DEPTH - 1], parts[DEPTH - 1] = _chip_exchange_wait(*in_flight[:4], dx)
    for l in range(DEPTH - 1):
        sums[l] = chip_sums(f"l{l}", big[l])
        parts[l] = _chip_exchange(f"chip_exchange_l{l}", sums[l])
    full = []
    for i in range(len(BIG)):
        f = None
        for l in range(DEPTH):
            f = _sum_parts(f"shardsum{DEPTH * i + l}", sums[l][i], parts[l][i], f, l, DEPTH)
        full.append(f)
    g_big = dict(zip(BIG, _pair_join(full)))

    small_parts = []
    for n, shp in SMALL:
        if n == "final_norm":
            small_parts.append(dfinal)
        else:
            small_parts.append(jnp.stack([small[l][n].reshape(shp[1:]) for l in range(DEPTH)]))
    g_small = dict(zip([n for n, _ in SMALL], _unpack(_all_sum("sum_small", _pack(small_parts, 200)), [s_ for _, s_ in SMALL])))
    g_small["conv_w"] = lax.dynamic_slice(g_small["conv_w"], (0, 0, chip * HEAD), (DEPTH, CONV_W, HEAD))

    grad, delta, new_m, new_v = dict(g_small), {}, {}, {}
    grad.update(g_big)
    for n in BIG:
        l, r, c = given[n].shape
        f = lambda t: t.reshape(l * r, c)
        d_, m_, v_ = _adamw("adamw_" + n, f(given[n]), f(grad[n]), f(given["m_" + n]), f(given["v_" + n]))
        delta[n], new_m[n], new_v[n] = d_.reshape(l, r, c), m_.reshape(l, r, c), v_.reshape(l, r, c)
    snames = [n for n, _ in SMALL]
    shapes = [given[n].shape for n in snames]
    rows = 104
    d_, m_, v_ = _adamw("adamw_small", _pack([given[n] for n in snames], rows), _pack([grad[n] for n in snames], rows),
                        _pack([given["m_" + n] for n in snames], rows), _pack([given["v_" + n] for n in snames], rows))
    for dst, packed in ((delta, d_), (new_m, m_), (new_v, v_)):
        dst.update(zip(snames, _unpack(packed, shapes)))

    total = lax.psum(loss[0, 0], ("x", "y", "c"))
    order = ["ffn1_norm", "ffn1_w_in", "ffn1_w_out", "mix_norm", "mix_w_in", "conv_w", "conv_b", "conv_ln_g",
             "conv_ln_b", "ret_norm_g", "mix_w_out", "ffn2_norm", "ffn2_w_in", "ffn2_w_out", "final_norm"]
    return (total, dx[None], *[grad[n] for n in order], *[delta[n] for n in order],
            *[new_m[n] for n in order], *[new_v[n] for n in order])
```

```python
import functools

import numpy as np
import jax
import jax.numpy as jnp
from jax import lax
from jax.experimental import pallas as pl
from jax.experimental.pallas import tpu as pltpu

F32 = jnp.float32
BF16 = jnp.bfloat16

D_MODEL = 1024
D_FF = 2816
N_SHARD = 4
FF_SHARD = 2 * D_FF // N_SHARD
MIX_SHARD = 3072 // N_SHARD
HEAD = 64
SLAB = 256
N_SLAB = 3072 // SLAB
CONV_W = 31
CONV_PAD = 32
CHUNK = 64
EPS = 1e-6
ROPE_BASE = 10000.0
DEPTH = 2

LR, B1, B2, ADAM_EPS, WD, STEP = 0.001, 0.9, 0.999, 1e-08, 0.01, 10

VMEM_LIMIT = 56 * 1024 * 1024


def _params(n_grid):
    return pltpu.CompilerParams(dimension_semantics=("arbitrary",) * n_grid, vmem_limit_bytes=VMEM_LIMIT)


def _rw(name, body, grid, ins, in_specs, rows=(), row_specs=(), accs=(), acc_specs=(), aliases=None):
    n_in, n_row = len(ins), len(rows)
    carried = sorted(aliases) if aliases else []

    def kern(*refs):
        vals = [r[...] for i, r in enumerate(refs[:n_in]) if i not in carried]
        row_vals, acc_vals = body(*vals)
        for r, v in zip(refs[n_in:n_in + n_row], row_vals):
            r[...] = v.astype(r.dtype)
        acc_refs = refs[n_in + n_row:]
        if acc_refs:
            first = functools.reduce(jnp.logical_and, [pl.program_id(a) == 0 for a in range(len(grid))])

            @pl.when(first)
            def _():
                for r in acc_refs:
                    r[...] = jnp.zeros(r.shape, r.dtype)

            for r, v in zip(acc_refs, acc_vals):
                r[...] += v.astype(r.dtype)

    return pl.pallas_call(
        kern, name=name, grid=grid, in_specs=list(in_specs), out_specs=list(row_specs) + list(acc_specs),
        out_shape=list(rows) + list(accs), input_output_aliases=dict(aliases or {}),
        compiler_params=_params(len(grid)))(*ins)


def _sds(shape, dtype):
    return jax.ShapeDtypeStruct(shape, dtype)


def _rms(x, g):
    return x * lax.rsqrt(jnp.mean(x * x, axis=-1, keepdims=True) + EPS) * g


def _row_spec(tm, c):
    return pl.BlockSpec((tm, c), lambda i: (i, 0))


def _vec_spec(c):
    return pl.BlockSpec((1, c), lambda i: (0, 0))


def _swiglu(gate, up):
    return jax.nn.silu(gate) * up


def _swiglu_fwd(name, u, tm):
    _, s, c = u.shape
    return _rw(name, lambda ub: ((_swiglu(ub[0:2].astype(F32), ub[2:4].astype(F32)),), ()), (s // tm,), [u],
               [pl.BlockSpec((4, tm, c), lambda i: (0, i, 0))],
               [_sds((2, s, c), BF16)], [pl.BlockSpec((2, tm, c), lambda i: (0, i, 0))])[0]


def _swiglu_bwd(name, u, da, tm):
    _, s, c = u.shape

    def body(ub, dab):
        _, vjp = jax.vjp(_swiglu, ub[0:2].astype(F32), ub[2:4].astype(F32))
        dg, du = vjp(dab.astype(F32))
        return (jnp.concatenate([dg, du], axis=0),), ()

    return _rw(name, body, (s // tm,), [u, da],
               [pl.BlockSpec((4, tm, c), lambda i: (0, i, 0)), pl.BlockSpec((2, tm, c), lambda i: (0, i, 0))],
               [_sds((4, s, c), BF16)], [pl.BlockSpec((4, tm, c), lambda i: (0, i, 0))])[0]


def _ln_silu(y, g, b):
    mu = jnp.mean(y, axis=-1, keepdims=True)
    yc = y - mu
    var = jnp.mean(yc * yc, axis=-1, keepdims=True)
    return jax.nn.silu(yc * lax.rsqrt(var + EPS) * g + b)


def _ln_silu_fwd(name, y, g, b, tm):
    s, c = y.shape
    return _rw(name, lambda yb, gb, bb: ((_ln_silu(yb, gb, bb),), ()), (s // tm,), [y, g, b],
               [_row_spec(tm, c), _vec_spec(c), _vec_spec(c)], [_sds((s, c), BF16)], [_row_spec(tm, c)])[0]


def _ln_silu_bwd(name, y, dcat, g, b, tm):
    s, c = y.shape

    def body(yb, dob, gb, bb):
        _, vjp = jax.vjp(_ln_silu, yb, gb, bb)
        dy, dg, db = vjp(dob)
        return (dy,), (dg, db)

    return _rw(name, body, (s // tm,), [y, dcat, g, b],
               [_row_spec(tm, c), pl.BlockSpec((None, tm, c), lambda i: (0, i, 0)), _vec_spec(c), _vec_spec(c)],
               [_sds((s, c), F32)], [_row_spec(tm, c)],
               [_sds((1, c), F32)] * 2, [_vec_spec(c)] * 2)


def _head_masks():
    lane = np.arange(SLAB) // HEAD
    m = np.zeros((8, SLAB), np.float32)
    for h in range(SLAB // HEAD):
        m[h] = (lane == h)
    return jnp.asarray(m)


def _gated_head_norm(y, gate, g, hm):
    mu = jnp.zeros_like(y)
    for h in range(SLAB // HEAD):
        mu = mu + hm[h:h + 1] * (jnp.sum(y * hm[h:h + 1], axis=-1, keepdims=True) / HEAD)
    yc = y - mu
    var = jnp.zeros_like(y)
    for h in range(SLAB // HEAD):
        var = var + hm[h:h + 1] * (jnp.sum(yc * yc * hm[h:h + 1], axis=-1, keepdims=True) / HEAD)
    return jax.nn.silu(gate) * (yc * lax.rsqrt(var + EPS) * g)


PER_SHARD = MIX_SHARD // SLAB


def _slab_spec(tm, j):
    return pl.BlockSpec((None, tm, SLAB), lambda i: (j, i, 0))


def _proj_slab_spec(tm, j):
    return pl.BlockSpec((None, tm, SLAB), lambda i: (j // PER_SHARD, i, j % PER_SHARD))


def _ghn_fwd(name, y, p32, g, tm):
    s, c = y.shape
    hm = _head_masks()
    return _rw(name, lambda yb, gb, wb, hb: ((_gated_head_norm(yb, gb, wb, hb),), ()), (s // tm,),
               [y, p32, g, hm],
               [_row_spec(tm, c), _proj_slab_spec(tm, 11), _vec_spec(c), pl.BlockSpec((8, c), lambda i: (0, 0))],
               [_sds((s, c), BF16)], [_row_spec(tm, c)])[0]


def _ghn_bwd(name, y, p32, dcat, g, tm):
    s, c = y.shape
    hm = _head_masks()

    def body(yb, gb, dob, wb, hb):
        _, vjp = jax.vjp(lambda a, b_, c_: _gated_head_norm(a, b_, c_, hb), yb, gb, wb)
        dy, dgate, dw = vjp(dob)
        return (dy, dgate), (dw,)

    return _rw(name, body, (s // tm,), [y, p32, dcat, g, hm],
               [_row_spec(tm, c), _proj_slab_spec(tm, 11), _slab_spec(tm, 3), _vec_spec(c),
                pl.BlockSpec((8, c), lambda i: (0, 0))],
               [_sds((s, c), F32)] * 2, [_row_spec(tm, c)] * 2,
               [_sds((1, c), F32)], [_vec_spec(c)])


def _assemble(name, parts, per, tm):
    s = parts[0][0].shape[-2]
    specs = [_row_spec(tm, SLAB) if j is None else pl.BlockSpec((None, tm, SLAB), lambda i, j=j: (j, i, 0))
             for _, j in parts]

    def body(*blocks):
        rows = [jnp.concatenate([b.astype(BF16) for b in blocks[per * q:per * (q + 1)]], axis=-1)
                for q in range(len(blocks) // per)]
        return (jnp.stack(rows),), ()

    nq = len(parts) // per
    return _rw(name, body, (s // tm,), [a for a, _ in parts], specs, [_sds((nq, s, per * SLAB), BF16)],
               [pl.BlockSpec((nq, tm, per * SLAB), lambda i: (0, i, 0))])[0]


def _final(name, x, tgt, g, tm):
    s, d = x.shape

    def body(xb, tb, gb):
        yf, vjp = jax.vjp(_rms, xb, gb)
        err = yf - tb
        dx, dg = vjp(err * (1.0 / d))
        part = 0.5 * jnp.sum(jnp.mean(err * err, axis=-1, keepdims=True), axis=0, keepdims=True)
        return (dx,), (dg, jnp.broadcast_to(part, (1, 128)))

    return _rw(name, body, (s // tm,), [x, tgt, g],
               [_row_spec(tm, d), _row_spec(tm, d), _vec_spec(d)],
               [_sds((s, d), F32)], [_row_spec(tm, d)],
               [_sds((1, d), F32), _sds((1, 128), F32)], [_vec_spec(d), _vec_spec(128)])


NN = (((1,), (0,)), ((), ()))
NT = (((1,), (1,)), ((), ()))
TN = (((0,), (0,)), ((), ()))


def _mm(name, a, b, grid, a_spec, b_spec, outs, out_specs, acc_shape, dims, alpha=1.0):
    nk = grid[-1]
    n_out = len(outs)

    def kern(*refs):
        a_ref, b_ref = refs[0], refs[1]
        o_refs = refs[2:2 + n_out]
        part = lax.dot_general(a_ref[...].astype(BF16), b_ref[...].astype(BF16), dims,
                               preferred_element_type=F32)

        def finish(r):
            if alpha != 1.0:
                r = r * alpha
            for o in o_refs:
                o[...] = r.astype(o.dtype)

        if nk == 1:
            finish(part)
            return
        acc_ref = refs[-1]
        k = pl.program_id(len(grid) - 1)

        @pl.when(k == 0)
        def _():
            acc_ref[...] = part

        @pl.when(jnp.logical_and(k > 0, k < nk - 1))
        def _():
            acc_ref[...] += part

        @pl.when(k == nk - 1)
        def _():
            finish(acc_ref[...] + part)

    return pl.pallas_call(
        kern, name=name, grid=grid, in_specs=[a_spec, b_spec], out_specs=list(out_specs), out_shape=list(outs),
        scratch_shapes=[pltpu.VMEM(acc_shape, F32)] if nk > 1 else [],
        compiler_params=_params(len(grid)))(a, b)


def _norm_proj_in(name, x, g, w, l, tm, dtypes):
    s, d = x.shape
    n = w.shape[-1]
    n_out = len(dtypes)

    def kern(x_ref, g_ref, w_ref, h_ref, *rest):
        o_refs, h_vmem = rest[:n_out], rest[n_out]

        @pl.when(pl.program_id(1) == 0)
        def _():
            h = _rms(x_ref[...], g_ref[...]).astype(BF16)
            h_vmem[...] = h
            h_ref[...] = h

        r = jnp.dot(h_vmem[...], w_ref[...], preferred_element_type=F32)
        for o in o_refs:
            o[...] = r.astype(o.dtype)

    out = pl.pallas_call(
        kern, name=name, grid=(s // tm, N_SHARD),
        in_specs=[pl.BlockSpec((tm, d), lambda i, b: (i, 0)), pl.BlockSpec((1, d), lambda i, b: (0, 0)),
                  pl.BlockSpec((None, None, d, n), lambda i, b: (l, b, 0, 0))],
        out_specs=[pl.BlockSpec((tm, d), lambda i, b: (i, 0))] +
                  [pl.BlockSpec((None, tm, n), lambda i, b: (b, i, 0))] * n_out,
        out_shape=[_sds((s, d), BF16)] + [_sds((N_SHARD, s, n), t) for t in dtypes],
        scratch_shapes=[pltpu.VMEM((tm, d), BF16)], compiler_params=_params(2))(x, g, w)
    return out[0], out[1:]


def _back_in_norm(name, du, w, l, tm, x, dres, g):
    nk, s, n = du.shape
    d = w.shape[2]

    def kern(du_ref, w_ref, x_ref, dres_ref, g_ref, dx_ref, dg_ref):
        dh = lax.dot_general(du_ref[0], w_ref[0], NT, preferred_element_type=F32)
        for k in range(1, nk):
            dh = dh + lax.dot_general(du_ref[k], w_ref[k], NT, preferred_element_type=F32)
        _, vjp = jax.vjp(_rms, x_ref[...], g_ref[...])
        dx, dg = vjp(dh)
        dx_ref[...] = dx + dres_ref[...]

        @pl.when(pl.program_id(0) == 0)
        def _():
            dg_ref[...] = dg

        @pl.when(pl.program_id(0) > 0)
        def _():
            dg_ref[...] += dg

    row = pl.BlockSpec((tm, d), lambda i: (i, 0))
    vec = pl.BlockSpec((1, d), lambda i: (0, 0))
    return pl.pallas_call(
        kern, name=name, grid=(s // tm,),
        in_specs=[pl.BlockSpec((nk, tm, n), lambda i: (0, i, 0)),
                  pl.BlockSpec((None, nk, d, n), lambda i: (l, 0, 0, 0)), row, row, vec],
        out_specs=[row, vec], out_shape=[_sds((s, d), F32), _sds((1, d), F32)],
        compiler_params=_params(1))(du, w, x, dres, g)


def _proj_out(name, a, w, l, res, alpha, tm):
    nk, s, r = a.shape
    d = w.shape[-1]

    def kern(a_ref, w_ref, res_ref, o_ref):
        y = jnp.dot(a_ref[0], w_ref[0], preferred_element_type=F32)
        for k in range(1, nk):
            y = y + jnp.dot(a_ref[k], w_ref[k], preferred_element_type=F32)
        o_ref[...] = res_ref[...] + (y * alpha if alpha != 1.0 else y)

    row = pl.BlockSpec((tm, d), lambda i: (i, 0))
    return pl.pallas_call(
        kern, name=name, grid=(s // tm,),
        in_specs=[pl.BlockSpec((nk, tm, r), lambda i: (0, i, 0)),
                  pl.BlockSpec((None, nk, r, d), lambda i: (l, 0, 0, 0)), row],
        out_specs=row, out_shape=_sds((s, d), F32), compiler_params=_params(1))(a, w, res)


def _back_out(name, dy, w, l, alpha, tm, out_dtype):
    s, d = dy.shape
    nk, r = w.shape[1], w.shape[2]
    return _mm(name, dy, w, (nk, s // tm, 1),
               pl.BlockSpec((tm, d), lambda b, i, k: (i, 0)),
               pl.BlockSpec((None, None, r, d), lambda b, i, k: (l, b, 0, 0)),
               [_sds((nk, s, r), out_dtype)], [pl.BlockSpec((None, tm, r), lambda b, i, k: (b, i, 0))],
               (tm, r), NT, alpha=alpha)[0]


def _grad_in(name, h, du, ts):
    s, d = h.shape
    nb, _, n = du.shape
    return _mm(name, h, du, (nb, 1, s // ts),
               pl.BlockSpec((ts, d), lambda b, i, k: (k, 0)),
               pl.BlockSpec((None, ts, n), lambda b, i, k: (b, k, 0)),
               [_sds((nb, d, n), BF16)], [pl.BlockSpec((None, d, n), lambda b, i, k: (b, 0, 0))],
               (d, n), TN)[0]


def _grad_out(name, a, dy, alpha, ts):
    nb, s, r = a.shape
    d = dy.shape[1]
    return _mm(name, a, dy, (nb, 1, s // ts),
               pl.BlockSpec((None, ts, r), lambda b, i, k: (b, k, 0)),
               pl.BlockSpec((ts, d), lambda b, i, k: (k, 0)),
               [_sds((nb, r, d), BF16)], [pl.BlockSpec((None, r, d), lambda b, i, k: (b, 0, 0))],
               (r, d), TN, alpha=alpha)[0]


CONV_TILE = 256


def _shifted(win, off, rows):
    n = win.shape[0]
    return pltpu.roll(win, (n - off) % n, 0)[0:rows] if off % n else win[0:rows]


def _conv_fwd(name, p32, w, bias):
    s = p32.shape[1]
    cb = 128
    nt = s // CONV_TILE

    def kern(a_ref, b_ref, w_ref, bias_ref, y_ref, vpad):
        vpad[0:CONV_PAD, :] = jnp.zeros((CONV_PAD, cb), F32)

        def fill(i, c):
            r = pl.multiple_of(i * CONV_TILE, CONV_TILE)
            vpad[pl.ds(CONV_PAD + r, CONV_TILE), :] = (
                a_ref[pl.ds(r, CONV_TILE), :] * jax.nn.sigmoid(b_ref[pl.ds(r, CONV_TILE), :]))
            return c

        lax.fori_loop(0, nt, fill, 0)

        def tile(i, c):
            r = pl.multiple_of(i * CONV_TILE, CONV_TILE)
            win = vpad[pl.ds(r, CONV_TILE + CONV_PAD), :]
            acc = jnp.broadcast_to(bias_ref[...], (CONV_TILE, cb))
            for j in range(CONV_W):
                acc = acc + w_ref[j:j + 1, :] * _shifted(win, j + 2, CONV_TILE)
            y_ref[pl.ds(r, CONV_TILE), :] = acc
            return c

        lax.fori_loop(0, nt, tile, 0)

    return pl.pallas_call(
        kern, name=name, grid=(SLAB // cb,),
        in_specs=[pl.BlockSpec((None, s, cb), lambda c: (0, 0, c)),
                  pl.BlockSpec((None, s, cb), lambda c: (0, 0, SLAB // cb + c)),
                  pl.BlockSpec((CONV_W, cb), lambda c: (0, c)),
                  pl.BlockSpec((1, cb), lambda c: (0, c))],
        out_specs=pl.BlockSpec((s, cb), lambda c: (0, c)),
        out_shape=_sds((s, SLAB), F32),
        scratch_shapes=[pltpu.VMEM((s + CONV_PAD, cb), F32)],
        compiler_params=_params(1))(p32, p32, w, bias)


def _conv_bwd(name, p32, w, dy):
    s = p32.shape[1]
    cb = 128
    nt = s // CONV_TILE

    def kern(a_ref, b_ref, w_ref, dy_ref, da_ref, db_ref, dw_ref, dbias_ref, vpad, dpad):
        vpad[0:CONV_PAD, :] = jnp.zeros((CONV_PAD, cb), F32)
        dpad[s:s + CONV_PAD, :] = jnp.zeros((CONV_PAD, cb), F32)
        dw_ref[...] = jnp.zeros((CONV_PAD, cb), F32)
        dbias_ref[...] = jnp.zeros((1, cb), F32)

        def fill(i, c):
            r = pl.multiple_of(i * CONV_TILE, CONV_TILE)
            vpad[pl.ds(CONV_PAD + r, CONV_TILE), :] = (
                a_ref[pl.ds(r, CONV_TILE), :] * jax.nn.sigmoid(b_ref[pl.ds(r, CONV_TILE), :]))
            dpad[pl.ds(r, CONV_TILE), :] = dy_ref[pl.ds(r, CONV_TILE), :]
            return c

        lax.fori_loop(0, nt, fill, 0)

        def tile(i, c):
            r = pl.multiple_of(i * CONV_TILE, CONV_TILE)
            dwin = dpad[pl.ds(r, CONV_TILE + CONV_PAD), :]
            vwin = vpad[pl.ds(r, CONV_TILE + CONV_PAD), :]
            dyt = dwin[0:CONV_TILE]
            dv = jnp.zeros((CONV_TILE, cb), F32)
            for j in range(CONV_W):
                dv = dv + w_ref[j:j + 1, :] * _shifted(dwin, CONV_W - 1 - j, CONV_TILE)
                dw_ref[j:j + 1, :] += jnp.sum(dyt * _shifted(vwin, j + 2, CONV_TILE), axis=0, keepdims=True)
            dbias_ref[...] += jnp.sum(dyt, axis=0, keepdims=True)
            a = a_ref[pl.ds(r, CONV_TILE), :]
            sg = jax.nn.sigmoid(b_ref[pl.ds(r, CONV_TILE), :])
            da_ref[pl.ds(r, CONV_TILE), :] = dv * sg
            db_ref[pl.ds(r, CONV_TILE), :] = dv * a * sg * (1.0 - sg)
            return c

        lax.fori_loop(0, nt, tile, 0)

    col = pl.BlockSpec((s, cb), lambda c: (0, c))
    return pl.pallas_call(
        kern, name=name, grid=(SLAB // cb,),
        in_specs=[pl.BlockSpec((None, s, cb), lambda c: (0, 0, c)),
                  pl.BlockSpec((None, s, cb), lambda c: (0, 0, SLAB // cb + c)),
                  pl.BlockSpec((CONV_W, cb), lambda c: (0, c)), col],
        out_specs=[col, col, pl.BlockSpec((CONV_PAD, cb), lambda c: (0, c)), pl.BlockSpec((1, cb), lambda c: (0, c))],
        out_shape=[_sds((s, SLAB), F32), _sds((s, SLAB), F32), _sds((CONV_PAD, SLAB), F32), _sds((1, SLAB), F32)],
        scratch_shapes=[pltpu.VMEM((s + CONV_PAD, cb), F32), pltpu.VMEM((s + CONV_PAD, cb), F32)],
        compiler_params=_params(1))(p32, p32, w, dy)


SB_BLOCK = 256
N_HEAD = SLAB // HEAD


def _sb_logits(qm, k, tri):
    z = lax.dot_general(qm, k, NT, preferred_element_type=F32)
    sign_bit = jnp.uint32(0x80000000)
    neg_abs = lax.bitcast_convert_type(lax.bitcast_convert_type(z, jnp.uint32) | sign_bit, F32)
    lb = jnp.minimum(z, 0.0) - jnp.log(1.0 + jnp.exp(neg_abs))
    ln = lb - z
    if tri is not None:
        ln = jnp.where(tri, ln, 0.0)
    return lb, ln


def _first_col(x):
    return jnp.broadcast_to(x[:, 0:1], (x.shape[0], 128))


def _head_stack(dst, x, lane_head, bq):
    for h in range(N_HEAD):
        dst[h * bq:(h + 1) * bq, :] = jnp.where(lane_head == h, x, jnp.zeros_like(x))


def _sb_fwd(name, p16):
    s = p16.shape[1]
    bq = min(SB_BLOCK, s)
    nq = s // bq

    def kern(q_ref, k_ref, v_ref, o_ref, w_hbm, lb_hbm, qm_ref, v4_refs, w4_refs, ws_ref, lbs_ref, acc_ref, r_ref,
             sem):
        g, qi = pl.program_id(0), pl.program_id(1)
        lane_head = lax.broadcasted_iota(jnp.int32, (1, SLAB), 1) // HEAD
        _head_stack(qm_ref, (q_ref[...].astype(F32) * (HEAD ** -0.5)).astype(BF16), lane_head, bq)
        row = lax.broadcasted_iota(jnp.int32, (bq, bq), 0)
        col = lax.broadcasted_iota(jnp.int32, (bq, bq), 1)
        after = (row > col).astype(BF16)
        tri = col < row
        acc_ref[...] = jnp.zeros((bq, SLAB), F32)
        r_ref[...] = jnp.zeros((N_HEAD, bq, 128), F32)

        def saves(slot, kb):
            return (pltpu.make_async_copy(ws_ref.at[slot], w_hbm.at[g, qi, kb], sem.at[0, slot]),
                    pltpu.make_async_copy(lbs_ref.at[slot], lb_hbm.at[g, qi, kb], sem.at[1, slot]))

        def tile(i, masked, u):
            kb, slot = qi - i, i % 4
            v4_ref, w4_ref = v4_refs.at[u], w4_refs.at[u]
            rows = pl.ds(pl.multiple_of(kb * bq, bq), bq)
            k = k_ref[rows, :]
            _head_stack(v4_ref, v_ref[rows, :], lane_head, bq)
            for h in range(N_HEAD):
                mine = pl.ds(h * bq, bq)
                lb, ln = _sb_logits(qm_ref[h * bq:(h + 1) * bq, :], k, tri if masked else None)
                rem = jnp.dot(ln.astype(BF16), after, preferred_element_type=F32)
                w = jnp.exp(lb + rem + r_ref[h][:, 0:1])
                if masked:
                    w = jnp.where(tri, w, 0.0)
                wb = w.astype(BF16)
                w4_ref[:, h * bq:(h + 1) * bq] = wb
                ws_ref[slot, mine, :] = wb
                lbs_ref[slot, mine, :] = lb.astype(BF16)
                r_ref[h] += _first_col(rem[:, 0:128] + ln[:, 0:128])
            acc_ref[...] += jnp.dot(w4_ref[...], v4_ref[...], preferred_element_type=F32)

        def save(i, start):
            for cp in saves(i % 4, qi - i):
                cp.start() if start else cp.wait()

        @pl.when(qi == 0)
        def _():
            tile(0, True, 0)
            save(0, True)
            save(0, False)

        @pl.when(qi >= 1)
        def _():
            tile(0, True, 0)
            tile(1, False, 1)
            save(0, True)
            save(1, True)

        def pair(j, c):
            tile(2 * j, False, 0)
            tile(2 * j + 1, False, 1)
            save(2 * j - 2, False)
            save(2 * j - 1, False)
            save(2 * j, True)
            save(2 * j + 1, True)
            return c

        n_pair = (qi + 1) // 2
        lax.fori_loop(1, n_pair, pair, 0)

        @pl.when(jnp.logical_and(qi >= 1, qi % 2 == 1))
        def _():
            save(qi - 1, False)
            save(qi, False)

        @pl.when(jnp.logical_and(qi >= 2, qi % 2 == 0))
        def _():
            tile(qi, False, 0)
            save(qi - 2, False)
            save(qi - 1, False)
            save(qi, True)
            save(qi, False)

        o_ref[...] = acc_ref[...]

    saved = _sds((2, nq, nq, N_HEAD * bq, bq), BF16)
    return pl.pallas_call(
        kern, name=name, grid=(2, nq),
        in_specs=[pl.BlockSpec((None, bq, SLAB), lambda g, i: ((2 + g) // PER_SHARD, i, (2 + g) % PER_SHARD)),
                  pl.BlockSpec((None, s, SLAB), lambda g, i: ((4 + g) // PER_SHARD, 0, (4 + g) % PER_SHARD)),
                  pl.BlockSpec((None, s, SLAB), lambda g, i: ((6 + g) // PER_SHARD, 0, (6 + g) % PER_SHARD))],
        out_specs=[pl.BlockSpec((None, bq, SLAB), lambda g, i: (g, i, 0)),
                   pl.BlockSpec(memory_space=pl.ANY), pl.BlockSpec(memory_space=pl.ANY)],
        out_shape=[_sds((2, s, SLAB), F32), saved, saved],
        scratch_shapes=[pltpu.VMEM((N_HEAD * bq, SLAB), BF16), pltpu.VMEM((2, N_HEAD * bq, SLAB), BF16),
                        pltpu.VMEM((2, bq, N_HEAD * bq), BF16), pltpu.VMEM((4, N_HEAD * bq, bq), BF16),
                        pltpu.VMEM((4, N_HEAD * bq, bq), BF16), pltpu.VMEM((bq, SLAB), F32),
                        pltpu.VMEM((N_HEAD, bq, 128), F32), pltpu.SemaphoreType.DMA((2, 4))],
        compiler_params=_params(2))(p16, p16, p16)


def _sb_bwd(name, p16, w_saved, lb_saved, dcat):
    s = p16.shape[1]
    bq = min(SB_BLOCK, s)
    nq = s // bq

    def kern(q_ref, k_ref, v_ref, do_ref, w_hbm, lb_hbm, dq_ref, dk_hbm, dv_hbm, dk_acc, dv_acc, dq_acc,
             qm_ref, dom_ref, k4_refs, dzc_refs, dzs_refs, ws_ref, lbs_ref, c_ref, sem, lsem):
        g, qi = pl.program_id(0), pl.program_id(1)

        @pl.when(qi == 0)
        def _():
            dk_acc[...] = jnp.zeros((s, SLAB), F32)
            dv_acc[...] = jnp.zeros((s, SLAB), F32)

        lane_head = lax.broadcasted_iota(jnp.int32, (1, SLAB), 1) // HEAD
        _head_stack(qm_ref, (q_ref[...].astype(F32) * (HEAD ** -0.5)).astype(BF16), lane_head, bq)
        _head_stack(dom_ref, do_ref[...].astype(BF16), lane_head, bq)
        row = lax.broadcasted_iota(jnp.int32, (bq, bq), 0)
        col = lax.broadcasted_iota(jnp.int32, (bq, bq), 1)
        earlier = (row < col).astype(BF16)
        tri = col < row
        dq_acc[...] = jnp.zeros((bq, SLAB), F32)
        c_ref[...] = jnp.zeros((N_HEAD, bq, 128), F32)

        def loads(kb):
            slot, kb = kb % 4, jnp.minimum(kb, qi)
            return (pltpu.make_async_copy(w_hbm.at[g, qi, kb], ws_ref.at[slot], lsem.at[0, slot]),
                    pltpu.make_async_copy(lb_hbm.at[g, qi, kb], lbs_ref.at[slot], lsem.at[1, slot]))

        def tile(kb, masked, u):
            slot = kb % 4
            k4_ref, dzc_ref, dzs_ref = k4_refs.at[u], dzc_refs.at[u], dzs_refs.at[u]
            rows = pl.ds(pl.multiple_of(kb * bq, bq), bq)
            k, v = k_ref[rows, :], v_ref[rows, :]
            _head_stack(k4_ref, k, lane_head, bq)
            for h in range(N_HEAD):
                mine = slice(h * bq, (h + 1) * bq)
                wb = ws_ref[slot, pl.ds(h * bq, bq), :]
                dl = wb.astype(F32) * lax.dot_general(dom_ref[mine, :], v, NT, preferred_element_type=F32)
                prefix = jnp.dot(dl.astype(BF16), earlier, preferred_element_type=F32)
                before = prefix + c_ref[h][:, 0:1]
                sig = jnp.exp(lbs_ref[slot, pl.ds(h * bq, bq), :].astype(F32))
                dz = dl - sig * (dl + before)
                if masked:
                    dz = jnp.where(tri, dz, 0.0)
                dzb = dz.astype(BF16)
                dzc_ref[:, mine] = dzb
                dzs_ref[mine, :] = dzb
                tail = prefix[:, bq - 128:] + dl[:, bq - 128:]
                c_ref[h] += jnp.broadcast_to(tail[:, 127:128], (bq, 128))
            dq_acc[...] += jnp.dot(dzc_ref[...], k4_ref[...], preferred_element_type=F32)
            dk_acc[rows, :] += lax.dot_general(dzs_ref[...], qm_ref[...], TN, preferred_element_type=F32)
            dv_acc[rows, :] += lax.dot_general(ws_ref[slot], dom_ref[...], TN, preferred_element_type=F32)

        for kb in (0, 1):
            for cp in loads(kb):
                cp.start()

        def pair(j, c):
            for kb in (2 * j + 2, 2 * j + 3):
                for cp in loads(kb):
                    cp.start()
            for kb in (2 * j, 2 * j + 1):
                for cp in loads(kb):
                    cp.wait()
            tile(2 * j, False, 0)
            tile(2 * j + 1, False, 1)
            return c

        lax.fori_loop(0, qi // 2, pair, 0)
        for kb in (qi - qi % 2, qi - qi % 2 + 1):
            for cp in loads(kb):
                cp.wait()

        @pl.when(qi % 2 == 1)
        def _():
            tile(qi - 1, False, 0)
            tile(qi, True, 1)

        @pl.when(qi % 2 == 0)
        def _():
            tile(qi, True, 0)

        dq_ref[...] = dq_acc[...] * (HEAD ** -0.5)

        @pl.when(qi == nq - 1)
        def _():
            ck = pltpu.make_async_copy(dk_acc, dk_hbm.at[g], sem.at[0])
            cv = pltpu.make_async_copy(dv_acc, dv_hbm.at[g], sem.at[1])
            ck.start()
            cv.start()
            ck.wait()
            cv.wait()

    blk = lambda j0: pl.BlockSpec((None, bq, SLAB), lambda g, i: (j0 + g, i, 0))
    full = lambda j0: pl.BlockSpec((None, s, SLAB), lambda g, i: ((j0 + g) // PER_SHARD, 0, (j0 + g) % PER_SHARD))
    q_blk = pl.BlockSpec((None, bq, SLAB), lambda g, i: ((2 + g) // PER_SHARD, i, (2 + g) % PER_SHARD))
    stack16 = pltpu.VMEM((N_HEAD * bq, SLAB), BF16)
    return pl.pallas_call(
        kern, name=name, grid=(2, nq),
        in_specs=[q_blk, full(4), full(6), blk(1),
                  pl.BlockSpec(memory_space=pl.ANY), pl.BlockSpec(memory_space=pl.ANY)],
        out_specs=[blk(0), pl.BlockSpec(memory_space=pl.ANY), pl.BlockSpec(memory_space=pl.ANY)],
        out_shape=[_sds((2, s, SLAB), F32)] * 3,
        scratch_shapes=[pltpu.VMEM((s, SLAB), F32), pltpu.VMEM((s, SLAB), F32), pltpu.VMEM((bq, SLAB), F32),
                        stack16, stack16, pltpu.VMEM((2, N_HEAD * bq, SLAB), BF16),
                        pltpu.VMEM((2, bq, N_HEAD * bq), BF16), pltpu.VMEM((2, N_HEAD * bq, bq), BF16),
                        pltpu.VMEM((4, N_HEAD * bq, bq), BF16), pltpu.VMEM((4, N_HEAD * bq, bq), BF16),
                        pltpu.VMEM((N_HEAD, bq, 128), F32),
                        pltpu.SemaphoreType.DMA((2,)), pltpu.SemaphoreType.DMA((2, 4))],
        compiler_params=_params(2))(p16, p16, p16, dcat, w_saved, lb_saved)


RET_BLOCK = 256


def _ret_tables(s, bl):
    nh = SLAB // HEAD
    lane_h = np.arange(SLAB) // HEAD
    log_gamma = np.log1p(-np.exp2(-5.0 - np.arange(nh, dtype=np.float64)))
    lg_lane = log_gamma[lane_h]
    half = HEAD // 2
    inv = 1.0 / (ROPE_BASE ** (np.arange(half, dtype=np.float64) / half))
    ang = np.arange(s, dtype=np.float64)[:, None] * inv[None, :]
    within = np.arange(SLAB) % HEAD
    cos = np.cos(ang)[:, within % half]
    sin = np.sin(ang)[:, within % half] * np.where(within < half, -1.0, 1.0)[None, :]
    perm = np.zeros((SLAB, SLAB))
    partner = np.where(within < half, np.arange(SLAB) + half, np.arange(SLAB) - half)
    perm[partner, np.arange(SLAB)] = 1.0
    i = np.arange(bl)
    diff = i[:, None] - i[None, :]
    same = (i[:, None] // CHUNK) == (i[None, :] // CHUNK)
    earlier = (i[None, :] // CHUNK) < (i[:, None] // CHUNK)
    decay = np.zeros((nh, bl, bl))
    for h in range(nh):
        decay[h] = np.where(same, np.exp(log_gamma[h] * np.abs(diff)),
                            np.where(earlier, np.exp(log_gamma[h] * diff), 0.0))
    qd = np.exp(lg_lane[None, :] * (i[:, None] + 1.0))
    kd = np.exp(lg_lane[None, :] * (bl - 1.0 - i[:, None]))
    gam = np.exp(lg_lane * bl)[:, None] * np.ones((1, SLAB))
    bd = (lane_h[:, None] == lane_h[None, :]).astype(np.float64)
    f = lambda a: jnp.asarray(a, F32)
    return f(cos), f(sin), f(perm), f(decay), f(qd), f(kd), f(gam), f(bd)


def _ret_block(q, k, v, state, cos, sin, perm, decay, qd, kd, gam, bd, hm):
    qr = (q * cos + jnp.dot(q, perm, preferred_element_type=F32) * sin) * (HEAD ** -0.5)
    kr = k * cos + jnp.dot(k, perm, preferred_element_type=F32) * sin
    y = jnp.dot(qr * qd, state, preferred_element_type=F32)
    for h in range(SLAB // HEAD):
        m = hm[h:h + 1]
        sc = lax.dot_general(qr * m, kr, NT, preferred_element_type=F32) * decay[h]
        y = y + jnp.dot(sc, v * m, preferred_element_type=F32)
    new_state = gam * state + lax.dot_general(kr * kd, v, TN, preferred_element_type=F32) * bd
    return y, new_state


def _ret_specs(s, bl, rev):
    nb = s // bl
    pos = (lambda n: nb - 1 - n) if rev else (lambda n: n)
    slab = lambda j: pl.BlockSpec((None, bl, SLAB), lambda n: (j // PER_SHARD, pos(n), j % PER_SHARD))
    const2 = lambda r: pl.BlockSpec((r, SLAB), lambda n: (0, 0))
    tab = [pl.BlockSpec((bl, SLAB), lambda n: (pos(n), 0))] * 2 + [
        const2(SLAB), pl.BlockSpec((SLAB // HEAD, bl, bl), lambda n: (0, 0, 0)),
        const2(bl), const2(bl), const2(SLAB), const2(SLAB), const2(8)]
    return nb, pos, slab, tab


def _ret_fwd(name, p32):
    s = p32.shape[1]
    bl = min(RET_BLOCK, s)
    nb, pos, slab, tab = _ret_specs(s, bl, False)
    tables = _ret_tables(s, bl) + (_head_masks(),)

    def kern(q_ref, k_ref, v_ref, *rest):
        t_refs, (y_ref, st_ref, state) = rest[:9], rest[9:]

        @pl.when(pl.program_id(0) == 0)
        def _():
            state[...] = jnp.zeros((SLAB, SLAB), F32)

        st_ref[...] = state[...]
        y, new = _ret_block(q_ref[...], k_ref[...], v_ref[...], state[...], *[t[...] for t in t_refs])
        y_ref[...] = y
        state[...] = new

    return pl.pallas_call(
        kern, name=name, grid=(nb,), in_specs=[slab(8), slab(9), slab(10)] + tab,
        out_specs=[pl.BlockSpec((bl, SLAB), lambda n: (n, 0)), pl.BlockSpec((None, SLAB, SLAB), lambda n: (n, 0, 0))],
        out_shape=[_sds((s, SLAB), F32), _sds((nb, SLAB, SLAB), F32)],
        scratch_shapes=[pltpu.VMEM((SLAB, SLAB), F32)], compiler_params=_params(1))(p32, p32, p32, *tables)


def _ret_bwd(name, p32, states, dy):
    s = p32.shape[1]
    bl = min(RET_BLOCK, s)
    nb, pos, slab, tab = _ret_specs(s, bl, True)
    tables = _ret_tables(s, bl) + (_head_masks(),)
    rowblk = pl.BlockSpec((bl, SLAB), lambda n: (pos(n), 0))

    def kern(q_ref, k_ref, v_ref, st_ref, dy_ref, *rest):
        t_refs, (dq_ref, dk_ref, dv_ref, dstate) = rest[:9], rest[9:]

        @pl.when(pl.program_id(0) == 0)
        def _():
            dstate[...] = jnp.zeros((SLAB, SLAB), F32)

        tv = [t[...] for t in t_refs]
        _, vjp = jax.vjp(lambda a, b, c, d: _ret_block(a, b, c, d, *tv),
                         q_ref[...], k_ref[...], v_ref[...], st_ref[...])
        dq, dk, dv, ds = vjp((dy_ref[...], dstate[...]))
        dq_ref[...] = dq
        dk_ref[...] = dk
        dv_ref[...] = dv
        dstate[...] = ds

    return pl.pallas_call(
        kern, name=name, grid=(nb,),
        in_specs=[slab(8), slab(9), slab(10), pl.BlockSpec((None, SLAB, SLAB), lambda n: (pos(n), 0, 0)), rowblk] + tab,
        out_specs=[rowblk] * 3, out_shape=[_sds((s, SLAB), F32)] * 3,
        scratch_shapes=[pltpu.VMEM((SLAB, SLAB), F32)], compiler_params=_params(1))(p32, p32, p32, states, dy, *tables)


TM_FFN = 1024
TM_SLAB = 2048
TM_NORM = 256
TM_OUT = 512
TM_GRAD = 2048
TM_RW = 256
TM_FF = 256


def _ffn_fwd(tag, x, g, w_in, w_out, l):
    tm = min(TM_FFN, x.shape[0])
    h, (u,) = _norm_proj_in(tag + "_in", x, g, w_in, l, tm, [BF16])
    a = _swiglu_fwd(tag + "_act", u, TM_FF)
    w_out2 = w_out.reshape(w_out.shape[0], 2, FF_SHARD, D_MODEL)
    xn = _proj_out(tag + "_out", a, w_out2, l, x, 0.5, min(TM_OUT, x.shape[0]))
    return xn, (x, h, u, a)


def _ffn_bwd(tag, saved, dxn, g, w_in, w_out, l):
    x, h, u, a = saved
    tm = min(TM_FFN, x.shape[0])
    w_out2 = w_out.reshape(w_out.shape[0], 2, FF_SHARD, D_MODEL)
    da = _back_out(tag + "_dact", dxn, w_out2, l, 0.5, tm, BF16)
    tk = min(TM_GRAD, x.shape[0])
    dw_out = _grad_out(tag + "_dwout", a, dxn, 0.5, tk)
    du = _swiglu_bwd(tag + "_dswi", u, da, TM_FF)
    dx, dg = _back_in_norm(tag + "_dh", du, w_in, l, min(TM_NORM, x.shape[0]), x, dxn, g)
    dw_in = _grad_in(tag + "_dwin", h, du, tk)
    return dx, dg, dw_in, dw_out.reshape(N_SHARD, D_FF // N_SHARD, D_MODEL)


def _mix_fwd(tag, x, sm, w_in, w_out, l, wl):
    h, (p32, p16) = _norm_proj_in(tag + "_in", x, sm["mix_norm"][l:l + 1], w_in, wl, min(TM_FFN, x.shape[0]),
                                  [F32, BF16])
    ypre = _conv_fwd(tag + "_conv", p32, sm["conv_w"][l], sm["conv_b"][l:l + 1])
    yconv = _ln_silu_fwd(tag + "_ln", ypre, sm["conv_ln_g"][l:l + 1], sm["conv_ln_b"][l:l + 1], TM_RW)
    osb, w_sb, lb_sb = _sb_fwd(tag + "_sb", p16)
    yr, states = _ret_fwd(tag + "_ret", p32)
    yret = _ghn_fwd(tag + "_ghn", yr, p32, sm["ret_norm_g"][l:l + 1], TM_RW)
    ycat = _assemble(tag + "_cat", [(yconv, None), (osb, 0), (osb, 1), (yret, None)], 1, TM_RW)
    xn = _proj_out(tag + "_out", ycat, w_out, wl, x, 1.0, min(TM_OUT, x.shape[0]))
    return xn, (x, h, p32, p16, ypre, (osb, w_sb, lb_sb), yr, states, ycat)


def _mix_bwd(tag, saved, dxn, sm, w_in, w_out, l, wl):
    x, h, p32, p16, ypre, osb, yr, states, ycat = saved
    ts = min(TM_SLAB, x.shape[0])
    dcat = _back_out(tag + "_dcat", dxn, w_out, wl, 1.0, ts, F32)
    dw_out = _grad_out(tag + "_dwout", ycat, dxn, 1.0, ts)
    dypre, dlg, dlb = _ln_silu_bwd(tag + "_dln", ypre, dcat, sm["conv_ln_g"][l:l + 1], sm["conv_ln_b"][l:l + 1], TM_RW)
    da, db, dcw, dcb = _conv_bwd(tag + "_dconv", p32, sm["conv_w"][l], dypre)
    dq, dk, dv = _sb_bwd(tag + "_dsb", p16, osb[1], osb[2], dcat)
    dyr, dgate, drg = _ghn_bwd(tag + "_dghn", yr, p32, dcat, sm["ret_norm_g"][l:l + 1], TM_RW)
    dqr, dkr, dvr = _ret_bwd(tag + "_dret", p32, states, dyr)
    dp = _assemble(tag + "_dp", [(da, None), (db, None), (dq, 0), (dq, 1), (dk, 0), (dk, 1), (dv, 0), (dv, 1),
                                 (dqr, None), (dkr, None), (dvr, None), (dgate, None)], PER_SHARD, TM_RW)
    dx, dg = _back_in_norm(tag + "_dh", dp, w_in, wl, min(TM_NORM, x.shape[0]), x, dxn, sm["mix_norm"][l:l + 1])
    dw_in = _grad_in(tag + "_dwin", h, dp, min(TM_GRAD, x.shape[0]))
    small = dict(mix_norm=dg, conv_w=dcw[0:CONV_W], conv_b=dcb, conv_ln_g=dlg, conv_ln_b=dlb, ret_norm_g=drg)
    return dx, small, dw_in, dw_out


def _local_step(x, tgt, early, last, sm, on_last_layer=None):
    saved = []
    weights = [(early, l) for l in range(DEPTH - 1)]
    for l in range(DEPTH):
        if l == DEPTH - 1:
            weights.append((last(x), 0))
        wt, wl = weights[l]
        x, s1 = _ffn_fwd(f"l{l}f1", x, sm["ffn1_norm"][l:l + 1], wt["ffn1_w_in"], wt["ffn1_w_out"], wl)
        x, s2 = _mix_fwd(f"l{l}mx", x, sm, wt["mix_w_in"], wt["mix_w_out"], l, wl)
        x, s3 = _ffn_fwd(f"l{l}f2", x, sm["ffn2_norm"][l:l + 1], wt["ffn2_w_in"], wt["ffn2_w_out"], wl)
        saved.append((s1, s2, s3))
    dx, dfinal, loss = _final("final", x, tgt, sm["final_norm"][None, :], TM_RW)
    big = [None] * DEPTH
    small = [None] * DEPTH
    for l in reversed(range(DEPTH)):
        s1, s2, s3 = saved[l]
        wt, wl = weights[l]
        dx, dg3, dwi3, dwo3 = _ffn_bwd(f"l{l}f2", s3, dx, sm["ffn2_norm"][l:l + 1], wt["ffn2_w_in"], wt["ffn2_w_out"], wl)
        dx, sml, dwi2, dwo2 = _mix_bwd(f"l{l}mx", s2, dx, sm, wt["mix_w_in"], wt["mix_w_out"], l, wl)
        dx, dg1, dwi1, dwo1 = _ffn_bwd(f"l{l}f1", s1, dx, sm["ffn1_norm"][l:l + 1], wt["ffn1_w_in"], wt["ffn1_w_out"], wl)
        big[l] = dict(ffn1_w_in=dwi1, ffn1_w_out=dwo1, mix_w_in=dwi2, mix_w_out=dwo2, ffn2_w_in=dwi3, ffn2_w_out=dwo3)
        sml.update(ffn1_norm=dg1, ffn2_norm=dg3)
        small[l] = sml
        if on_last_layer is not None and l == DEPTH - 1:
            dx = dx + on_last_layer(big[l])[0:1, 0:1]
    return loss, dx, big, small, dfinal


MESH = pl.DeviceIdType.MESH
ANY = pl.BlockSpec(memory_space=pl.ANY)
BIG = ("ffn1_w_in", "ffn1_w_out", "mix_w_in", "mix_w_out", "ffn2_w_in", "ffn2_w_out")


def _place():
    x, y, c = lax.axis_index("x"), lax.axis_index("y"), lax.axis_index("c")
    chips = [(1 - x, y), (x, 1 - y), (1 - x, 1 - y)]
    return x, y, c, chips


def _gather_weights(w16):
    n = len(w16)

    def kern(*refs):
        dst = refs[n:2 * n]
        send, recv = refs[2 * n:]
        x, y, c, chips = _place()
        mine = 2 * x + y
        firsts, passes = [], []
        for a in range(n):
            h = dst[a].shape[2] // 2
            own = dst[a].at[:, mine, pl.ds(c * h, h)]
            for j, (cx, cy) in enumerate(chips):
                cp = pltpu.make_async_remote_copy(
                    src_ref=own, dst_ref=own, send_sem=send.at[6 * a + j], recv_sem=recv.at[6 * a + j],
                    device_id=(cx, cy, c), device_id_type=MESH)
                cp.start()
                firsts.append(cp)
        for a in range(n):
            h = dst[a].shape[2] // 2
            half = pl.ds(c * h, h)
            for j, (cx, cy) in enumerate(chips):
                theirs = dst[a].at[:, 2 * cx + cy, half]
                pltpu.make_async_remote_copy(
                    src_ref=theirs, dst_ref=theirs, send_sem=send.at[6 * a + j], recv_sem=recv.at[6 * a + j],
                    device_id=(cx, cy, c), device_id_type=MESH).wait_recv()
                fw = pltpu.make_async_remote_copy(
                    src_ref=theirs, dst_ref=theirs, send_sem=send.at[6 * a + 3 + j], recv_sem=recv.at[6 * a + 3 + j],
                    device_id=(x, y, 1 - c), device_id_type=MESH)
                fw.start()
                passes.append(fw)
        for a in range(n):
            h = dst[a].shape[2] // 2
            other = pl.ds((1 - c) * h, h)
            for j, (cx, cy) in enumerate(chips):
                got = dst[a].at[:, 2 * cx + cy, other]
                pltpu.make_async_remote_copy(
                    src_ref=got, dst_ref=got, send_sem=send.at[6 * a + 3 + j], recv_sem=recv.at[6 * a + 3 + j],
                    device_id=(x, y, 1 - c), device_id_type=MESH).wait_recv()
        for cp in firsts + passes:
            cp.wait_send()

    return pl.pallas_call(
        kern, name="gather_weights", in_specs=[ANY] * n, out_specs=[ANY] * n,
        out_shape=[_sds(w.shape, w.dtype) for w in w16], input_output_aliases={a: a for a in range(n)},
        scratch_shapes=[pltpu.SemaphoreType.DMA((6 * n,)), pltpu.SemaphoreType.DMA((6 * n,))])(*w16)


def _last_layer_copies(bufs, send, recv, x, y, c, chips):
    mine = 2 * x + y
    sends, arrivals = [], []
    for a, buf in enumerate(bufs):
        last = buf.shape[0] - 1
        h = buf.shape[2] // 2
        own = buf.at[last, mine, pl.ds(c * h, h)]
        for j, (cx, cy) in enumerate(chips):
            for k in range(2):
                sem = 6 * a + 2 * j + k
                sends.append(pltpu.make_async_remote_copy(
                    src_ref=own, dst_ref=own, send_sem=send.at[sem], recv_sem=recv.at[sem],
                    device_id=(cx, cy, c ^ k), device_id_type=MESH))
                theirs = buf.at[last, 2 * cx + cy, pl.ds((c ^ k) * h, h)]
                arrivals.append(pltpu.make_async_remote_copy(
                    src_ref=theirs, dst_ref=theirs, send_sem=send.at[sem], recv_sem=recv.at[sem],
                    device_id=(cx, cy, c ^ k), device_id_type=MESH))
    return sends, arrivals


def _gather_last_start(w16, after):
    n = len(w16)

    def kern(*refs):
        bufs, send, recv = refs[:n], refs[n + 1], refs[n + 2]
        x, y, c, chips = _place()
        for cp in _last_layer_copies(bufs, send, recv, x, y, c, chips)[0]:
            cp.start()

    out = pl.pallas_call(
        kern, name="gather_last_start",
        out_shape=(pltpu.SemaphoreType.DMA((6 * n,)), pltpu.SemaphoreType.DMA((6 * n,)),
                   *[pltpu.HBM(w.shape, w.dtype) for w in w16]),
        in_specs=[HBM] * n + [ANY], out_specs=(SEM, SEM, *[HBM] * n),
        input_output_aliases={i: 2 + i for i in range(n)},
        compiler_params=pltpu.CompilerParams(has_side_effects=EFFECT),
    )(*[pltpu.with_memory_space_constraint(w, pltpu.HBM) for w in w16], after)
    return out[0], out[1], out[2:]


def _gather_last_wait(send, recv, w16, after):
    n = len(w16)

    def kern(*refs):
        bufs, send_sem, recv_sem = refs[:n], refs[n], refs[n + 1]
        x, y, c, chips = _place()
        sends, arrivals = _last_layer_copies(bufs, send_sem, recv_sem, x, y, c, chips)
        for cp in sends:
            cp.wait_send()
        for cp in arrivals:
            cp.wait_recv()

    return pl.pallas_call(
        kern, name="gather_last_wait", out_shape=tuple(pltpu.HBM(w.shape, w.dtype) for w in w16),
        in_specs=[HBM] * n + [SEM, SEM, ANY], out_specs=tuple([HBM] * n),
        input_output_aliases={i: i for i in range(n)},
        compiler_params=pltpu.CompilerParams(has_side_effects=EFFECT),
    )(*w16, send, recv, after)


def _pair_exchange(name, grads):
    n = len(grads)

    def kern(*refs):
        src, got_o = refs[:n], refs[n:2 * n]
        send, recv = refs[2 * n:]
        x, y, c, _ = _place()
        cps = []
        for a in range(n):
            h = src[a].shape[1] // 2
            cp = pltpu.make_async_remote_copy(
                src_ref=src[a].at[:, pl.ds((1 - c) * h, h)], dst_ref=got_o[a],
                send_sem=send.at[a], recv_sem=recv.at[a], device_id=(x, y, 1 - c), device_id_type=MESH)
            cp.start()
            cps.append(cp)
        for cp in cps:
            cp.wait()

    halves = [_sds((g.shape[0], g.shape[1] // 2, g.shape[2]), g.dtype) for g in grads]
    return pl.pallas_call(
        kern, name=name, in_specs=[ANY] * n, out_specs=[ANY] * n, out_shape=halves,
        scratch_shapes=[pltpu.SemaphoreType.DMA((n,)), pltpu.SemaphoreType.DMA((n,))])(*grads)


def _chip_exchange(name, sums):
    n = len(sums)

    def kern(*refs):
        src, dst = refs[:n], refs[n:2 * n]
        send, recv = refs[2 * n:]
        x, y, c, chips = _place()
        cps = _chip_copies(src, dst, send, recv, x, y, c, chips)
        for cp in cps:
            cp.start()
        for cp in cps:
            cp.wait()

    return pl.pallas_call(
        kern, name=name, in_specs=[ANY] * n, out_specs=[ANY] * n,
        out_shape=[_sds((3,) + s_.shape[1:], s_.dtype) for s_ in sums],
        scratch_shapes=[pltpu.SemaphoreType.DMA((3 * n,)), pltpu.SemaphoreType.DMA((3 * n,))])(*sums)


HBM = pl.BlockSpec(memory_space=pltpu.HBM)
SEM = pl.BlockSpec(memory_space=pltpu.SEMAPHORE)
EFFECT = pltpu.SideEffectType.DATAFLOW_SIDE_EFFECTING


def _chip_copies(src, land, send, recv, x, y, c, chips):
    return [pltpu.make_async_remote_copy(
        src_ref=src[a].at[2 * cx + cy], dst_ref=land[a].at[j], send_sem=send.at[3 * a + j],
        recv_sem=recv.at[3 * a + j], device_id=(cx, cy, c), device_id_type=MESH)
        for a in range(len(src)) for j, (cx, cy) in enumerate(chips)]


def _chip_exchange_start(sums):
    n = len(sums)

    def kern(*refs):
        src, land = refs[:n], refs[n:2 * n]
        send, recv, token = refs[2 * n], refs[2 * n + 1], refs[-1]
        x, y, c, chips = _place()
        for cp in _chip_copies(src, land, send, recv, x, y, c, chips):
            cp.start()
        token[...] = jnp.zeros_like(token)

    lands = [(3,) + s_.shape[1:] for s_ in sums]
    out = pl.pallas_call(
        kern, name="chip_exchange_start",
        out_shape=(pltpu.SemaphoreType.DMA((3 * n,)), pltpu.SemaphoreType.DMA((3 * n,)),
                   *[pltpu.HBM(s_.shape, s_.dtype) for s_ in sums],
                   *[pltpu.HBM(shp, s_.dtype) for shp, s_ in zip(lands, sums)], _sds((8, 128), F32)),
        in_specs=[HBM] * (2 * n),
        out_specs=(SEM, SEM, *[HBM] * (2 * n), pl.BlockSpec(memory_space=pltpu.VMEM)),
        input_output_aliases={i: 2 + i for i in range(2 * n)},
        compiler_params=pltpu.CompilerParams(has_side_effects=EFFECT),
    )(*[pltpu.with_memory_space_constraint(s_, pltpu.HBM) for s_ in sums],
      *[pltpu.with_memory_space_constraint(lax.empty(shp, s_.dtype), pltpu.HBM) for shp, s_ in zip(lands, sums)])
    return out[0], out[1], out[2:2 + n], out[2 + n:2 + 2 * n], out[-1]


def _chip_exchange_wait(send, recv, sums, lands, after):
    n = len(sums)

    def kern(*refs):
        src, land = refs[:n], refs[n:2 * n]
        send_sem, recv_sem = refs[2 * n], refs[2 * n + 1]
        x, y, c, chips = _place()
        for cp in _chip_copies(src, land, send_sem, recv_sem, x, y, c, chips):
            cp.wait_send()
            cp.wait_recv()

    out = pl.pallas_call(
        kern, name="chip_exchange_wait",
        out_shape=tuple(pltpu.HBM(t.shape, t.dtype) for t in list(sums) + list(lands)),
        in_specs=[HBM] * (2 * n) + [SEM, SEM, ANY], out_specs=tuple([HBM] * (2 * n)),
        input_output_aliases={i: i for i in range(2 * n)},
        compiler_params=pltpu.CompilerParams(has_side_effects=EFFECT),
    )(*sums, *lands, send, recv, after)
    return out[:n], out[n:]


def _pair_join(full):
    n = len(full)

    def kern(*refs):
        dst = refs[n:2 * n]
        send, recv = refs[2 * n:]
        x, y, c, _ = _place()
        cps = []
        for a in range(n):
            h = dst[a].shape[1] // 2
            mine = dst[a].at[:, pl.ds(c * h, h)]
            cp = pltpu.make_async_remote_copy(
                src_ref=mine, dst_ref=mine, send_sem=send.at[a], recv_sem=recv.at[a],
                device_id=(x, y, 1 - c), device_id_type=MESH)
            cp.start()
            cps.append(cp)
        for a, cp in enumerate(cps):
            cp.wait_send()
            h = dst[a].shape[1] // 2
            got = dst[a].at[:, pl.ds((1 - c) * h, h)]
            pltpu.make_async_remote_copy(
                src_ref=got, dst_ref=got, send_sem=send.at[a], recv_sem=recv.at[a],
                device_id=(x, y, 1 - c), device_id_type=MESH).wait_recv()

    return pl.pallas_call(
        kern, name="pair_join", in_specs=[ANY] * n, out_specs=[ANY] * n,
        out_shape=[_sds(f.shape, f.dtype) for f in full], input_output_aliases={a: a for a in range(n)},
        scratch_shapes=[pltpu.SemaphoreType.DMA((n,)), pltpu.SemaphoreType.DMA((n,))])(*full)


def _all_sum(name, v):
    r = v.shape[0]

    def kern(v_ref, o_ref, buf, send, recv):
        x, y, c, _ = _place()
        me = 4 * x + 2 * y + c
        buf[me] = v_ref[...]
        cps = []
        for k in range(1, 8):
            peer = (x ^ (k >> 2), y ^ ((k >> 1) & 1), c ^ (k & 1))
            cp = pltpu.make_async_remote_copy(
                src_ref=v_ref, dst_ref=buf.at[me], send_sem=send.at[k - 1], recv_sem=recv.at[k - 1],
                device_id=peer, device_id_type=MESH)
            cp.start()
            cps.append(cp)
        for k in range(1, 8):
            peer_id = me ^ k
            pltpu.make_async_remote_copy(
                src_ref=v_ref, dst_ref=buf.at[peer_id], send_sem=send.at[k - 1], recv_sem=recv.at[k - 1],
                device_id=(x, y, c), device_id_type=MESH).wait_recv()
        for cp in cps:
            cp.wait_send()
        acc = buf[0]
        for d in range(1, 8):
            acc = acc + buf[d]
        o_ref[...] = acc

    vm = pl.BlockSpec(memory_space=pltpu.VMEM)
    return pl.pallas_call(
        kern, name=name, in_specs=[vm], out_specs=vm, out_shape=_sds((r, 128), F32),
        scratch_shapes=[pltpu.VMEM((8, r, 128), F32), pltpu.SemaphoreType.DMA((7,)),
                        pltpu.SemaphoreType.DMA((7,))])(v)


def _my_chip():
    return 2 * lax.axis_index("x") + lax.axis_index("y")


def _my_core():
    return lax.axis_index("c")


def _cast_place(name, w, first, count):
    _, r, c = w.shape
    tr = r // 4
    return _rw(name, lambda wb: ((wb,), ()), (count, r // tr), [w],
               [pl.BlockSpec((None, tr, c), lambda j, i: (first + j, i, 0))],
               [_sds((count, N_SHARD, r, c), BF16)],
               [pl.BlockSpec((None, None, tr, c), lambda j, i: (j, _my_chip(), i, 0))])[0]


HALF_STEPS = 2


def _add_halves(name, g, got):
    n, h, c = got.shape
    tr, nt = h // HALF_STEPS, HALF_STEPS
    return _rw(name, lambda ab, bb: ((ab.astype(F32) + bb.astype(F32),), ()), (nt,), [g, got],
               [pl.BlockSpec((n, tr, c), lambda i: (0, _my_core() * nt + i, 0)),
                pl.BlockSpec((n, tr, c), lambda i: (0, i, 0))],
               [_sds((n, h, c), BF16)], [pl.BlockSpec((n, tr, c), lambda i: (0, i, 0))])[0]


def _sum_parts(name, sums, parts, full, layer, n_layer):
    _, h, c = sums.shape
    tr, nt = h // HALF_STEPS, HALF_STEPS

    def body(own, pb):
        acc = own.astype(F32)
        for j in range(pb.shape[0]):
            acc = acc + pb[j].astype(F32)
        return (acc,), ()

    ins = [sums, parts] + ([full] if full is not None else [])
    in_specs = [pl.BlockSpec((None, tr, c), lambda i: (_my_chip(), i, 0)),
                pl.BlockSpec((parts.shape[0], tr, c), lambda i: (0, i, 0))] + ([ANY] if full is not None else [])
    return _rw(name, body, (nt,), ins, in_specs, [_sds((n_layer, 2 * h, c), F32)],
               [pl.BlockSpec((None, tr, c), lambda i: (layer, _my_core() * nt + i, 0))],
               aliases={2: 0} if full is not None else None)[0]


def _adamw_math(w, g, m, v):
    m = B1 * m + (1.0 - B1) * g
    v = B2 * v + (1.0 - B2) * (g * g)
    m_hat = m / (1.0 - B1 ** STEP)
    v_hat = v / (1.0 - B2 ** STEP)
    delta = -LR * (m_hat / (jnp.sqrt(v_hat) + ADAM_EPS) + WD * w)
    return delta, m, v


def _adamw(name, w, g, m, v):
    r, c = w.shape
    tr = 64 if r % 64 == 0 else 8
    spec = _row_spec(tr, c)
    return _rw(name, lambda *b: (_adamw_math(*b), ()), (r // tr,), [w, g, m, v], [spec] * 4,
               [_sds((r, c), F32)] * 3, [spec] * 3)


SMALL = (("ffn1_norm", (DEPTH, D_MODEL)), ("mix_norm", (DEPTH, D_MODEL)), ("ffn2_norm", (DEPTH, D_MODEL)),
         ("conv_b", (DEPTH, SLAB)), ("conv_ln_g", (DEPTH, SLAB)), ("conv_ln_b", (DEPTH, SLAB)),
         ("ret_norm_g", (DEPTH, SLAB)), ("final_norm", (D_MODEL,)), ("conv_w", (DEPTH, CONV_W, SLAB)))


def _pack(parts, rows):
    flat = jnp.concatenate([p.reshape(-1) for p in parts])
    return jnp.pad(flat, (0, rows * 128 - flat.shape[0])).reshape(rows, 128)


def _unpack(packed, shapes):
    flat = packed.reshape(-1)
    out, off = [], 0
    for shp in shapes:
        n = int(np.prod(shp))
        out.append(flat[off:off + n].reshape(shp))
        off += n
    return out


def kernel(x, ffn1_norm, ffn1_w_in, ffn1_w_out, mix_norm, mix_w_in, conv_w, conv_b, conv_ln_g, conv_ln_b, ret_norm_g, mix_w_out, ffn2_norm, ffn2_w_in, ffn2_w_out, final_norm, loss_target, m_ffn1_norm, m_ffn1_w_in, m_ffn1_w_out, m_mix_norm, m_mix_w_in, m_conv_w, m_conv_b, m_conv_ln_g, m_conv_ln_b, m_ret_norm_g, m_mix_w_out, m_ffn2_norm, m_ffn2_w_in, m_ffn2_w_out, m_final_norm, v_ffn1_norm, v_ffn1_w_in, v_ffn1_w_out, v_mix_norm, v_mix_w_in, v_conv_w, v_conv_b, v_conv_ln_g, v_conv_ln_b, v_ret_norm_g, v_mix_w_out, v_ffn2_norm, v_ffn2_w_in, v_ffn2_w_out, v_final_norm):
    given = dict(locals())
    names = [n for n, _ in SMALL] + list(BIG)
    chip = 2 * lax.axis_index("x") + lax.axis_index("y")
    core = lax.axis_index("c")

    cw_rows = 128
    placed = lax.dynamic_update_slice(jnp.zeros((DEPTH, CONV_W, SLAB), F32), conv_w, (0, 0, chip * HEAD))
    placed = placed * (core == 0).astype(F32)
    conv_w_full = _unpack(_all_sum("gather_conv_w", _pack([placed], cw_rows)), [(DEPTH, CONV_W, SLAB)])[0]

    early = dict(zip(BIG, _gather_weights([_cast_place("cast_" + n, given[n], 0, DEPTH - 1) for n in BIG])))
    w_send, w_recv, arriving = _gather_last_start(
        [_cast_place("cast_last_" + n, given[n], DEPTH - 1, 1) for n in BIG], early[BIG[0]])
    sm = {n: given[n] for n, _ in SMALL}
    sm["conv_w"] = conv_w_full

    def last_weights(activations):
        return dict(zip(BIG, _gather_last_wait(w_send, w_recv, arriving, activations)))

    def chip_sums(tag, layer_grads):
        grads = [layer_grads[n] for n in BIG]
        theirs = _pair_exchange("pair_exchange_" + tag, grads)
        return [_add_halves(f"chipsum_{tag}{i}", a, b) for i, (a, b) in enumerate(zip(grads, theirs))]

    in_flight = []

    def start_last(layer_grads):
        in_flight.extend(_chip_exchange_start(chip_sums("last", layer_grads)))
        return in_flight[4]

    loss, dx, big, small, dfinal = _local_step(x[0], loss_target[0], early, last_weights, sm, start_last)
    sums, parts = [None] * DEPTH, [None] * DEPTH
    sums[DEPTH - 1], parts[DEPTH - 1] = _chip_exchange_wait(*in_flight[:4], dx)
    for l in range(DEPTH - 1):
        sums[l] = chip_sums(f"l{l}", big[l])
        parts[l] = _chip_exchange(f"chip_exchange_l{l}", sums[l])
    full = []
    for i in range(len(BIG)):
        f = None
        for l in range(DEPTH):
            f = _sum_parts(f"shardsum{DEPTH * i + l}", sums[l][i], parts[l][i], f, l, DEPTH)
        full.append(f)
    g_big = dict(zip(BIG, _pair_join(full)))

    small_parts = []
    for n, shp in SMALL:
        if n == "final_norm":
            small_parts.append(dfinal)
        else:
            small_parts.append(jnp.stack([small[l][n].reshape(shp[1:]) for l in range(DEPTH)]))
    g_small = dict(zip([n for n, _ in SMALL], _unpack(_all_sum("sum_small", _pack(small_parts, 200)), [s_ for _, s_ in SMALL])))
    g_small["conv_w"] = lax.dynamic_slice(g_small["conv_w"], (0, 0, chip * HEAD), (DEPTH, CONV_W, HEAD))

    grad, delta, new_m, new_v = dict(g_small), {}, {}, {}
    grad.update(g_big)
    for n in BIG:
        l, r, c = given[n].shape
        f = lambda t: t.reshape(l * r, c)
        d_, m_, v_ = _adamw("adamw_" + n, f(given[n]), f(grad[n]), f(given["m_" + n]), f(given["v_" + n]))
        delta[n], new_m[n], new_v[n] = d_.reshape(l, r, c), m_.reshape(l, r, c), v_.reshape(l, r, c)
    snames = [n for n, _ in SMALL]
    shapes = [given[n].shape for n in snames]
    rows = 104
    d_, m_, v_ = _adamw("adamw_small", _pack([given[n] for n in snames], rows), _pack([grad[n] for n in snames], rows),
                        _pack([given["m_" + n] for n in snames], rows), _pack([given["v_" + n] for n in snames], rows))
    for dst, packed in ((delta, d_), (new_m, m_), (new_v, v_)):
        dst.update(zip(snames, _unpack(packed, shapes)))

    total = lax.psum(loss[0, 0], ("x", "y", "c"))
    order = ["ffn1_norm", "ffn1_w_in", "ffn1_w_out", "mix_norm", "mix_w_in", "conv_w", "conv_b", "conv_ln_g",
             "conv_ln_b", "ret_norm_g", "mix_w_out", "ffn2_norm", "ffn2_w_in", "ffn2_w_out", "final_norm"]
    return (total, dx[None], *[grad[n] for n in order], *[delta[n] for n in order],
            *[new_m[n] for n in order], *[new_v[n] for n in order])
```

```python
import functools

import numpy as np
import jax
import jax.numpy as jnp
from jax import lax
from jax.experimental import pallas as pl
from jax.experimental.pallas import tpu as pltpu

F32 = jnp.float32
BF16 = jnp.bfloat16

D_MODEL = 1024
D_FF = 2816
N_SHARD = 4
FF_SHARD = 2 * D_FF // N_SHARD
MIX_SHARD = 3072 // N_SHARD
HEAD = 64
SLAB = 256
N_SLAB = 3072 // SLAB
CONV_W = 31
CONV_PAD = 32
CHUNK = 64
EPS = 1e-6
ROPE_BASE = 10000.0
DEPTH = 2

LR, B1, B2, ADAM_EPS, WD, STEP = 0.001, 0.9, 0.999, 1e-08, 0.01, 10

VMEM_LIMIT = 56 * 1024 * 1024


def _params(n_grid):
    return pltpu.CompilerParams(dimension_semantics=("arbitrary",) * n_grid, vmem_limit_bytes=VMEM_LIMIT)


def _rw(name, body, grid, ins, in_specs, rows=(), row_specs=(), accs=(), acc_specs=(), aliases=None):
    n_in, n_row = len(ins), len(rows)
    carried = sorted(aliases) if aliases else []

    def kern(*refs):
        vals = [r[...] for i, r in enumerate(refs[:n_in]) if i not in carried]
        row_vals, acc_vals = body(*vals)
        for r, v in zip(refs[n_in:n_in + n_row], row_vals):
            r[...] = v.astype(r.dtype)
        acc_refs = refs[n_in + n_row:]
        if acc_refs:
            first = functools.reduce(jnp.logical_and, [pl.program_id(a) == 0 for a in range(len(grid))])

            @pl.when(first)
            def _():
                for r in acc_refs:
                    r[...] = jnp.zeros(r.shape, r.dtype)

            for r, v in zip(acc_refs, acc_vals):
                r[...] += v.astype(r.dtype)

    return pl.pallas_call(
        kern, name=name, grid=grid, in_specs=list(in_specs), out_specs=list(row_specs) + list(acc_specs),
        out_shape=list(rows) + list(accs), input_output_aliases=dict(aliases or {}),
        compiler_params=_params(len(grid)))(*ins)


def _sds(shape, dtype):
    return jax.ShapeDtypeStruct(shape, dtype)


def _rms(x, g):
    return x * lax.rsqrt(jnp.mean(x * x, axis=-1, keepdims=True) + EPS) * g


def _row_spec(tm, c):
    return pl.BlockSpec((tm, c), lambda i: (i, 0))


def _vec_spec(c):
    return pl.BlockSpec((1, c), lambda i: (0, 0))


def _swiglu(gate, up):
    return jax.nn.silu(gate) * up


def _swiglu_fwd(name, u, tm):
    _, s, c = u.shape
    return _rw(name, lambda ub: ((_swiglu(ub[0:2].astype(F32), ub[2:4].astype(F32)),), ()), (s // tm,), [u],
               [pl.BlockSpec((4, tm, c), lambda i: (0, i, 0))],
               [_sds((2, s, c), BF16)], [pl.BlockSpec((2, tm, c), lambda i: (0, i, 0))])[0]


def _swiglu_bwd(name, u, da, tm):
    _, s, c = u.shape

    def body(ub, dab):
        _, vjp = jax.vjp(_swiglu, ub[0:2].astype(F32), ub[2:4].astype(F32))
        dg, du = vjp(dab.astype(F32))
        return (jnp.concatenate([dg, du], axis=0),), ()

    return _rw(name, body, (s // tm,), [u, da],
               [pl.BlockSpec((4, tm, c), lambda i: (0, i, 0)), pl.BlockSpec((2, tm, c), lambda i: (0, i, 0))],
               [_sds((4, s, c), BF16)], [pl.BlockSpec((4, tm, c), lambda i: (0, i, 0))])[0]


def _ln_silu(y, g, b):
    mu = jnp.mean(y, axis=-1, keepdims=True)
    yc = y - mu
    var = jnp.mean(yc * yc, axis=-1, keepdims=True)
    return jax.nn.silu(yc * lax.rsqrt(var + EPS) * g + b)


def _ln_silu_fwd(name, y, g, b, tm):
    s, c = y.shape
    return _rw(name, lambda yb, gb, bb: ((_ln_silu(yb, gb, bb),), ()), (s // tm,), [y, g, b],
               [_row_spec(tm, c), _vec_spec(c), _vec_spec(c)], [_sds((s, c), BF16)], [_row_spec(tm, c)])[0]


def _ln_silu_bwd(name, y, dcat, g, b, tm):
    s, c = y.shape

    def body(yb, dob, gb, bb):
        _, vjp = jax.vjp(_ln_silu, yb, gb, bb)
        dy, dg, db = vjp(dob)
        return (dy,), (dg, db)

    return _rw(name, body, (s // tm,), [y, dcat, g, b],
               [_row_spec(tm, c), pl.BlockSpec((None, tm, c), lambda i: (0, i, 0)), _vec_spec(c), _vec_spec(c)],
               [_sds((s, c), F32)], [_row_spec(tm, c)],
               [_sds((1, c), F32)] * 2, [_vec_spec(c)] * 2)


def _head_masks():
    lane = np.arange(SLAB) // HEAD
    m = np.zeros((8, SLAB), np.float32)
    for h in range(SLAB // HEAD):
        m[h] = (lane == h)
    return jnp.asarray(m)


def _gated_head_norm(y, gate, g, hm):
    mu = jnp.zeros_like(y)
    for h in range(SLAB // HEAD):
        mu = mu + hm[h:h + 1] * (jnp.sum(y * hm[h:h + 1], axis=-1, keepdims=True) / HEAD)
    yc = y - mu
    var = jnp.zeros_like(y)
    for h in range(SLAB // HEAD):
        var = var + hm[h:h + 1] * (jnp.sum(yc * yc * hm[h:h + 1], axis=-1, keepdims=True) / HEAD)
    return jax.nn.silu(gate) * (yc * lax.rsqrt(var + EPS) * g)


PER_SHARD = MIX_SHARD // SLAB


def _slab_spec(tm, j):
    return pl.BlockSpec((None, tm, SLAB), lambda i: (j, i, 0))


def _proj_slab_spec(tm, j):
    return pl.BlockSpec((None, tm, SLAB), lambda i: (j // PER_SHARD, i, j % PER_SHARD))


def _ghn_fwd(name, y, p32, g, tm):
    s, c = y.shape
    hm = _head_masks()
    return _rw(name, lambda yb, gb, wb, hb: ((_gated_head_norm(yb, gb, wb, hb),), ()), (s // tm,),
               [y, p32, g, hm],
               [_row_spec(tm, c), _proj_slab_spec(tm, 11), _vec_spec(c), pl.BlockSpec((8, c), lambda i: (0, 0))],
               [_sds((s, c), BF16)], [_row_spec(tm, c)])[0]


def _ghn_bwd(name, y, p32, dcat, g, tm):
    s, c = y.shape
    hm = _head_masks()

    def body(yb, gb, dob, wb, hb):
        _, vjp = jax.vjp(lambda a, b_, c_: _gated_head_norm(a, b_, c_, hb), yb, gb, wb)
        dy, dgate, dw = vjp(dob)
        return (dy, dgate), (dw,)

    return _rw(name, body, (s // tm,), [y, p32, dcat, g, hm],
               [_row_spec(tm, c), _proj_slab_spec(tm, 11), _slab_spec(tm, 3), _vec_spec(c),
                pl.BlockSpec((8, c), lambda i: (0, 0))],
               [_sds((s, c), F32)] * 2, [_row_spec(tm, c)] * 2,
               [_sds((1, c), F32)], [_vec_spec(c)])


def _assemble(name, parts, per, tm):
    s = parts[0][0].shape[-2]
    specs = [_row_spec(tm, SLAB) if j is None else pl.BlockSpec((None, tm, SLAB), lambda i, j=j: (j, i, 0))
             for _, j in parts]

    def body(*blocks):
        rows = [jnp.concatenate([b.astype(BF16) for b in blocks[per * q:per * (q + 1)]], axis=-1)
                for q in range(len(blocks) // per)]
        return (jnp.stack(rows),), ()

    nq = len(parts) // per
    return _rw(name, body, (s // tm,), [a for a, _ in parts], specs, [_sds((nq, s, per * SLAB), BF16)],
               [pl.BlockSpec((nq, tm, per * SLAB), lambda i: (0, i, 0))])[0]


def _final(name, x, tgt, g, tm):
    s, d = x.shape

    def body(xb, tb, gb):
        yf, vjp = jax.vjp(_rms, xb, gb)
        err = yf - tb
        dx, dg = vjp(err * (1.0 / d))
        part = 0.5 * jnp.sum(jnp.mean(err * err, axis=-1, keepdims=True), axis=0, keepdims=True)
        return (dx,), (dg, jnp.broadcast_to(part, (1, 128)))

    return _rw(name, body, (s // tm,), [x, tgt, g],
               [_row_spec(tm, d), _row_spec(tm, d), _vec_spec(d)],
               [_sds((s, d), F32)], [_row_spec(tm, d)],
               [_sds((1, d), F32), _sds((1, 128), F32)], [_vec_spec(d), _vec_spec(128)])


NN = (((1,), (0,)), ((), ()))
NT = (((1,), (1,)), ((), ()))
TN = (((0,), (0,)), ((), ()))


def _mm(name, a, b, grid, a_spec, b_spec, outs, out_specs, acc_shape, dims, alpha=1.0):
    nk = grid[-1]
    n_out = len(outs)

    def kern(*refs):
        a_ref, b_ref = refs[0], refs[1]
        o_refs = refs[2:2 + n_out]
        part = lax.dot_general(a_ref[...].astype(BF16), b_ref[...].astype(BF16), dims,
                               preferred_element_type=F32)

        def finish(r):
            if alpha != 1.0:
                r = r * alpha
            for o in o_refs:
                o[...] = r.astype(o.dtype)

        if nk == 1:
            finish(part)
            return
        acc_ref = refs[-1]
        k = pl.program_id(len(grid) - 1)

        @pl.when(k == 0)
        def _():
            acc_ref[...] = part

        @pl.when(jnp.logical_and(k > 0, k < nk - 1))
        def _():
            acc_ref[...] += part

        @pl.when(k == nk - 1)
        def _():
            finish(acc_ref[...] + part)

    return pl.pallas_call(
        kern, name=name, grid=grid, in_specs=[a_spec, b_spec], out_specs=list(out_specs), out_shape=list(outs),
        scratch_shapes=[pltpu.VMEM(acc_shape, F32)] if nk > 1 else [],
        compiler_params=_params(len(grid)))(a, b)


def _norm_proj_in(name, x, g, w, l, tm, dtypes):
    s, d = x.shape
    n = w.shape[-1]
    n_out = len(dtypes)

    def kern(x_ref, g_ref, w_ref, h_ref, *rest):
        o_refs, h_vmem = rest[:n_out], rest[n_out]

        @pl.when(pl.program_id(1) == 0)
        def _():
            h = _rms(x_ref[...], g_ref[...]).astype(BF16)
            h_vmem[...] = h
            h_ref[...] = h

        r = jnp.dot(h_vmem[...], w_ref[...], preferred_element_type=F32)
        for o in o_refs:
            o[...] = r.astype(o.dtype)

    out = pl.pallas_call(
        kern, name=name, grid=(s // tm, N_SHARD),
        in_specs=[pl.BlockSpec((tm, d), lambda i, b: (i, 0)), pl.BlockSpec((1, d), lambda i, b: (0, 0)),
                  pl.BlockSpec((None, None, d, n), lambda i, b: (l, b, 0, 0))],
        out_specs=[pl.BlockSpec((tm, d), lambda i, b: (i, 0))] +
                  [pl.BlockSpec((None, tm, n), lambda i, b: (b, i, 0))] * n_out,
        out_shape=[_sds((s, d), BF16)] + [_sds((N_SHARD, s, n), t) for t in dtypes],
        scratch_shapes=[pltpu.VMEM((tm, d), BF16)], compiler_params=_params(2))(x, g, w)
    return out[0], out[1:]


def _back_in_norm(name, du, w, l, tm, x, dres, g):
    nk, s, n = du.shape
    d = w.shape[2]

    def kern(du_ref, w_ref, x_ref, dres_ref, g_ref, dx_ref, dg_ref):
        dh = lax.dot_general(du_ref[0], w_ref[0], NT, preferred_element_type=F32)
        for k in range(1, nk):
            dh = dh + lax.dot_general(du_ref[k], w_ref[k], NT, preferred_element_type=F32)
        _, vjp = jax.vjp(_rms, x_ref[...], g_ref[...])
        dx, dg = vjp(dh)
        dx_ref[...] = dx + dres_ref[...]

        @pl.when(pl.program_id(0) == 0)
        def _():
            dg_ref[...] = dg

        @pl.when(pl.program_id(0) > 0)
        def _():
            dg_ref[...] += dg

    row = pl.BlockSpec((tm, d), lambda i: (i, 0))
    vec = pl.BlockSpec((1, d), lambda i: (0, 0))
    return pl.pallas_call(
        kern, name=name, grid=(s // tm,),
        in_specs=[pl.BlockSpec((nk, tm, n), lambda i: (0, i, 0)),
                  pl.BlockSpec((None, nk, d, n), lambda i: (l, 0, 0, 0)), row, row, vec],
        out_specs=[row, vec], out_shape=[_sds((s, d), F32), _sds((1, d), F32)],
        compiler_params=_params(1))(du, w, x, dres, g)


def _proj_out(name, a, w, l, res, alpha, tm):
    nk, s, r = a.shape
    d = w.shape[-1]

    def kern(a_ref, w_ref, res_ref, o_ref):
        y = jnp.dot(a_ref[0], w_ref[0], preferred_element_type=F32)
        for k in range(1, nk):
            y = y + jnp.dot(a_ref[k], w_ref[k], preferred_element_type=F32)
        o_ref[...] = res_ref[...] + (y * alpha if alpha != 1.0 else y)

    row = pl.BlockSpec((tm, d), lambda i: (i, 0))
    return pl.pallas_call(
        kern, name=name, grid=(s // tm,),
        in_specs=[pl.BlockSpec((nk, tm, r), lambda i: (0, i, 0)),
                  pl.BlockSpec((None, nk, r, d), lambda i: (l, 0, 0, 0)), row],
        out_specs=row, out_shape=_sds((s, d), F32), compiler_params=_params(1))(a, w, res)


def _back_out(name, dy, w, l, alpha, tm, out_dtype):
    s, d = dy.shape
    nk, r = w.shape[1], w.shape[2]
    return _mm(name, dy, w, (nk, s // tm, 1),
               pl.BlockSpec((tm, d), lambda b, i, k: (i, 0)),
               pl.BlockSpec((None, None, r, d), lambda b, i, k: (l, b, 0, 0)),
               [_sds((nk, s, r), out_dtype)], [pl.BlockSpec((None, tm, r), lambda b, i, k: (b, i, 0))],
               (tm, r), NT, alpha=alpha)[0]


def _grad_in(name, h, du, ts):
    s, d = h.shape
    nb, _, n = du.shape
    return _mm(name, h, du, (nb, 1, s // ts),
               pl.BlockSpec((ts, d), lambda b, i, k: (k, 0)),
               pl.BlockSpec((None, ts, n), lambda b, i, k: (b, k, 0)),
               [_sds((nb, d, n), BF16)], [pl.BlockSpec((None, d, n), lambda b, i, k: (b, 0, 0))],
               (d, n), TN)[0]


def _grad_out(name, a, dy, alpha, ts):
    nb, s, r = a.shape
    d = dy.shape[1]
    return _mm(name, a, dy, (nb, 1, s // ts),
               pl.BlockSpec((None, ts, r), lambda b, i, k: (b, k, 0)),
               pl.BlockSpec((ts, d), lambda b, i, k: (k, 0)),
               [_sds((nb, r, d), BF16)], [pl.BlockSpec((None, r, d), lambda b, i, k: (b, 0, 0))],
               (r, d), TN, alpha=alpha)[0]


CONV_TILE = 256


def _shifted(win, off, rows):
    n = win.shape[0]
    return pltpu.roll(win, (n - off) % n, 0)[0:rows] if off % n else win[0:rows]


def _conv_fwd(name, p32, w, bias):
    s = p32.shape[1]
    cb = 128
    nt = s // CONV_TILE

    def kern(a_ref, b_ref, w_ref, bias_ref, y_ref, vpad):
        vpad[0:CONV_PAD, :] = jnp.zeros((CONV_PAD, cb), F32)

        def fill(i, c):
            r = pl.multiple_of(i * CONV_TILE, CONV_TILE)
            vpad[pl.ds(CONV_PAD + r, CONV_TILE), :] = (
                a_ref[pl.ds(r, CONV_TILE), :] * jax.nn.sigmoid(b_ref[pl.ds(r, CONV_TILE), :]))
            return c

        lax.fori_loop(0, nt, fill, 0)

        def tile(i, c):
            r = pl.multiple_of(i * CONV_TILE, CONV_TILE)
            win = vpad[pl.ds(r, CONV_TILE + CONV_PAD), :]
            acc = jnp.broadcast_to(bias_ref[...], (CONV_TILE, cb))
            for j in range(CONV_W):
                acc = acc + w_ref[j:j + 1, :] * _shifted(win, j + 2, CONV_TILE)
            y_ref[pl.ds(r, CONV_TILE), :] = acc
            return c

        lax.fori_loop(0, nt, tile, 0)

    return pl.pallas_call(
        kern, name=name, grid=(SLAB // cb,),
        in_specs=[pl.BlockSpec((None, s, cb), lambda c: (0, 0, c)),
                  pl.BlockSpec((None, s, cb), lambda c: (0, 0, SLAB // cb + c)),
                  pl.BlockSpec((CONV_W, cb), lambda c: (0, c)),
                  pl.BlockSpec((1, cb), lambda c: (0, c))],
        out_specs=pl.BlockSpec((s, cb), lambda c: (0, c)),
        out_shape=_sds((s, SLAB), F32),
        scratch_shapes=[pltpu.VMEM((s + CONV_PAD, cb), F32)],
        compiler_params=_params(1))(p32, p32, w, bias)


def _conv_bwd(name, p32, w, dy):
    s = p32.shape[1]
    cb = 128
    nt = s // CONV_TILE

    def kern(a_ref, b_ref, w_ref, dy_ref, da_ref, db_ref, dw_ref, dbias_ref, vpad, dpad):
        vpad[0:CONV_PAD, :] = jnp.zeros((CONV_PAD, cb), F32)
        dpad[s:s + CONV_PAD, :] = jnp.zeros((CONV_PAD, cb), F32)
        dw_ref[...] = jnp.zeros((CONV_PAD, cb), F32)
        dbias_ref[...] = jnp.zeros((1, cb), F32)

        def fill(i, c):
            r = pl.multiple_of(i * CONV_TILE, CONV_TILE)
            vpad[pl.ds(CONV_PAD + r, CONV_TILE), :] = (
                a_ref[pl.ds(r, CONV_TILE), :] * jax.nn.sigmoid(b_ref[pl.ds(r, CONV_TILE), :]))
            dpad[pl.ds(r, CONV_TILE), :] = dy_ref[pl.ds(r, CONV_TILE), :]
            return c

        lax.fori_loop(0, nt, fill, 0)

        def tile(i, c):
            r = pl.multiple_of(i * CONV_TILE, CONV_TILE)
            dwin = dpad[pl.ds(r, CONV_TILE + CONV_PAD), :]
            vwin = vpad[pl.ds(r, CONV_TILE + CONV_PAD), :]
            dyt = dwin[0:CONV_TILE]
            dv = jnp.zeros((CONV_TILE, cb), F32)
            for j in range(CONV_W):
                dv = dv + w_ref[j:j + 1, :] * _shifted(dwin, CONV_W - 1 - j, CONV_TILE)
                dw_ref[j:j + 1, :] += jnp.sum(dyt * _shifted(vwin, j + 2, CONV_TILE), axis=0, keepdims=True)
            dbias_ref[...] += jnp.sum(dyt, axis=0, keepdims=True)
            a = a_ref[pl.ds(r, CONV_TILE), :]
            sg = jax.nn.sigmoid(b_ref[pl.ds(r, CONV_TILE), :])
            da_ref[pl.ds(r, CONV_TILE), :] = dv * sg
            db_ref[pl.ds(r, CONV_TILE), :] = dv * a * sg * (1.0 - sg)
            return c

        lax.fori_loop(0, nt, tile, 0)

    col = pl.BlockSpec((s, cb), lambda c: (0, c))
    return pl.pallas_call(
        kern, name=name, grid=(SLAB // cb,),
        in_specs=[pl.BlockSpec((None, s, cb), lambda c: (0, 0, c)),
                  pl.BlockSpec((None, s, cb), lambda c: (0, 0, SLAB // cb + c)),
                  pl.BlockSpec((CONV_W, cb), lambda c: (0, c)), col],
        out_specs=[col, col, pl.BlockSpec((CONV_PAD, cb), lambda c: (0, c)), pl.BlockSpec((1, cb), lambda c: (0, c))],
        out_shape=[_sds((s, SLAB), F32), _sds((s, SLAB), F32), _sds((CONV_PAD, SLAB), F32), _sds((1, SLAB), F32)],
        scratch_shapes=[pltpu.VMEM((s + CONV_PAD, cb), F32), pltpu.VMEM((s + CONV_PAD, cb), F32)],
        compiler_params=_params(1))(p32, p32, w, dy)


SB_BLOCK = 256
N_HEAD = SLAB // HEAD


def _sb_logits(qm, k, tri):
    z = lax.dot_general(qm, k, NT, preferred_element_type=F32)
    sign_bit = jnp.uint32(0x80000000)
    neg_abs = lax.bitcast_convert_type(lax.bitcast_convert_type(z, jnp.uint32) | sign_bit, F32)
    lb = jnp.minimum(z, 0.0) - jnp.log(1.0 + jnp.exp(neg_abs))
    ln = lb - z
    if tri is not None:
        ln = jnp.where(tri, ln, 0.0)
    return lb, ln


def _first_col(x):
    return jnp.broadcast_to(x[:, 0:1], (x.shape[0], 128))


def _head_stack(dst, x, lane_head, bq):
    for h in range(N_HEAD):
        dst[h * bq:(h + 1) * bq, :] = jnp.where(lane_head == h, x, jnp.zeros_like(x))


def _sb_fwd(name, p16):
    s = p16.shape[1]
    bq = min(SB_BLOCK, s)
    nq = s // bq

    def kern(q_ref, k_ref, v_ref, o_ref, w_hbm, lb_hbm, qm_ref, v4_refs, w4_refs, ws_ref, lbs_ref, acc_ref, r_ref,
             sem):
        g, qi = pl.program_id(0), pl.program_id(1)
        lane_head = lax.broadcasted_iota(jnp.int32, (1, SLAB), 1) // HEAD
        _head_stack(qm_ref, (q_ref[...].astype(F32) * (HEAD ** -0.5)).astype(BF16), lane_head, bq)
        row = lax.broadcasted_iota(jnp.int32, (bq, bq), 0)
        col = lax.broadcasted_iota(jnp.int32, (bq, bq), 1)
        after = (row > col).astype(BF16)
        tri = col < row
        acc_ref[...] = jnp.zeros((bq, SLAB), F32)
        r_ref[...] = jnp.zeros((N_HEAD, bq, 128), F32)

        def saves(slot, kb):
            return (pltpu.make_async_copy(ws_ref.at[slot], w_hbm.at[g, qi, kb], sem.at[0, slot]),
                    pltpu.make_async_copy(lbs_ref.at[slot], lb_hbm.at[g, qi, kb], sem.at[1, slot]))

        def tile(i, masked, u):
            kb, slot = qi - i, i % 4
            v4_ref, w4_ref = v4_refs.at[u], w4_refs.at[u]
            rows = pl.ds(pl.multiple_of(kb * bq, bq), bq)
            k = k_ref[rows, :]
            _head_stack(v4_ref, v_ref[rows, :], lane_head, bq)
            for h in range(N_HEAD):
                mine = pl.ds(h * bq, bq)
                lb, ln = _sb_logits(qm_ref[h * bq:(h + 1) * bq, :], k, tri if masked else None)
                rem = jnp.dot(ln.astype(BF16), after, preferred_element_type=F32)
                w = jnp.exp(lb + rem + r_ref[h][:, 0:1])
                if masked:
                    w = jnp.where(tri, w, 0.0)
                wb = w.astype(BF16)
                w4_ref[:, h * bq:(h + 1) * bq] = wb
                ws_ref[slot, mine, :] = wb
                lbs_ref[slot, mine, :] = lb.astype(BF16)
                r_ref[h] += _first_col(rem[:, 0:128] + ln[:, 0:128])
            acc_ref[...] += jnp.dot(w4_ref[...], v4_ref[...], preferred_element_type=F32)

        def save(i, start):
            for cp in saves(i % 4, qi - i):
                cp.start() if start else cp.wait()

        @pl.when(qi == 0)
        def _():
            tile(0, True, 0)
            save(0, True)
            save(0, False)

        @pl.when(qi >= 1)
        def _():
            tile(0, True, 0)
            tile(1, False, 1)
            save(0, True)
            save(1, True)

        def pair(j, c):
            tile(2 * j, False, 0)
            tile(2 * j + 1, False, 1)
            save(2 * j - 2, False)
            save(2 * j - 1, False)
            save(2 * j, True)
            save(2 * j + 1, True)
            return c

        n_pair = (qi + 1) // 2
        lax.fori_loop(1, n_pair, pair, 0)

        @pl.when(jnp.logical_and(qi >= 1, qi % 2 == 1))
        def _():
            save(qi - 1, False)
            save(qi, False)

        @pl.when(jnp.logical_and(qi >= 2, qi % 2 == 0))
        def _():
            tile(qi, False, 0)
            save(qi - 2, False)
            save(qi - 1, False)
            save(qi, True)
            save(qi, False)

        o_ref[...] = acc_ref[...]

    saved = _sds((2, nq, nq, N_HEAD * bq, bq), BF16)
    return pl.pallas_call(
        kern, name=name, grid=(2, nq),
        in_specs=[pl.BlockSpec((None, bq, SLAB), lambda g, i: ((2 + g) // PER_SHARD, i, (2 + g) % PER_SHARD)),
                  pl.BlockSpec((None, s, SLAB), lambda g, i: ((4 + g) // PER_SHARD, 0, (4 + g) % PER_SHARD)),
                  pl.BlockSpec((None, s, SLAB), lambda g, i: ((6 + g) // PER_SHARD, 0, (6 + g) % PER_SHARD))],
        out_specs=[pl.BlockSpec((None, bq, SLAB), lambda g, i: (g, i, 0)),
                   pl.BlockSpec(memory_space=pl.ANY), pl.BlockSpec(memory_space=pl.ANY)],
        out_shape=[_sds((2, s, SLAB), F32), saved, saved],
        scratch_shapes=[pltpu.VMEM((N_HEAD * bq, SLAB), BF16), pltpu.VMEM((2, N_HEAD * bq, SLAB), BF16),
                        pltpu.VMEM((2, bq, N_HEAD * bq), BF16), pltpu.VMEM((4, N_HEAD * bq, bq), BF16),
                        pltpu.VMEM((4, N_HEAD * bq, bq), BF16), pltpu.VMEM((bq, SLAB), F32),
                        pltpu.VMEM((N_HEAD, bq, 128), F32), pltpu.SemaphoreType.DMA((2, 4))],
        compiler_params=_params(2))(p16, p16, p16)


def _sb_bwd(name, p16, w_saved, lb_saved, dcat):
    s = p16.shape[1]
    bq = min(SB_BLOCK, s)
    nq = s // bq

    def kern(q_ref, k_ref, v_ref, do_ref, w_hbm, lb_hbm, dq_ref, dk_hbm, dv_hbm, dk_acc, dv_acc, dq_acc,
             qm_ref, dom_ref, k4_refs, dzc_refs, dzs_refs, ws_ref, lbs_ref, c_ref, sem, lsem):
        g, qi = pl.program_id(0), pl.program_id(1)

        @pl.when(qi == 0)
        def _():
            dk_acc[...] = jnp.zeros((s, SLAB), F32)
            dv_acc[...] = jnp.zeros((s, SLAB), F32)

        lane_head = lax.broadcasted_iota(jnp.int32, (1, SLAB), 1) // HEAD
        _head_stack(qm_ref, (q_ref[...].astype(F32) * (HEAD ** -0.5)).astype(BF16), lane_head, bq)
        _head_stack(dom_ref, do_ref[...].astype(BF16), lane_head, bq)
        row = lax.broadcasted_iota(jnp.int32, (bq, bq), 0)
        col = lax.broadcasted_iota(jnp.int32, (bq, bq), 1)
        earlier = (row < col).astype(BF16)
        tri = col < row
        dq_acc[...] = jnp.zeros((bq, SLAB), F32)
        c_ref[...] = jnp.zeros((N_HEAD, bq, 128), F32)

        def loads(kb):
            slot, kb = kb % 4, jnp.minimum(kb, qi)
            return (pltpu.make_async_copy(w_hbm.at[g, qi, kb], ws_ref.at[slot], lsem.at[0, slot]),
                    pltpu.make_async_copy(lb_hbm.at[g, qi, kb], lbs_ref.at[slot], lsem.at[1, slot]))

        def tile(kb, masked, u):
            slot = kb % 4
            k4_ref, dzc_ref, dzs_ref = k4_refs.at[u], dzc_refs.at[u], dzs_refs.at[u]
            rows = pl.ds(pl.multiple_of(kb * bq, bq), bq)
            k, v = k_ref[rows, :], v_ref[rows, :]
            _head_stack(k4_ref, k, lane_head, bq)
            for h in range(N_HEAD):
                mine = slice(h * bq, (h + 1) * bq)
                wb = ws_ref[slot, pl.ds(h * bq, bq), :]
                dl = wb.astype(F32) * lax.dot_general(dom_ref[mine, :], v, NT, preferred_element_type=F32)
                prefix = jnp.dot(dl.astype(BF16), earlier, preferred_element_type=F32)
                before = prefix + c_ref[h][:, 0:1]
                sig = jnp.exp(lbs_ref[slot, pl.ds(h * bq, bq), :].astype(F32))
                dz = dl - sig * (dl + before)
                if masked:
                    dz = jnp.where(tri, dz, 0.0)
                dzb = dz.astype(BF16)
                dzc_ref[:, mine] = dzb
                dzs_ref[mine, :] = dzb
                tail = prefix[:, bq - 128:] + dl[:, bq - 128:]
                c_ref[h] += jnp.broadcast_to(tail[:, 127:128], (bq, 128))
            dq_acc[...] += jnp.dot(dzc_ref[...], k4_ref[...], preferred_element_type=F32)
            dk_acc[rows, :] += lax.dot_general(dzs_ref[...], qm_ref[...], TN, preferred_element_type=F32)
            dv_acc[rows, :] += lax.dot_general(ws_ref[slot], dom_ref[...], TN, preferred_element_type=F32)

        for kb in (0, 1):
            for cp in loads(kb):
                cp.start()

        def pair(j, c):
            for kb in (2 * j + 2, 2 * j + 3):
                for cp in loads(kb):
                    cp.start()
            for kb in (2 * j, 2 * j + 1):
                for cp in loads(kb):
                    cp.wait()
            tile(2 * j, False, 0)
            tile(2 * j + 1, False, 1)
            return c

        lax.fori_loop(0, qi // 2, pair, 0)
        for kb in (qi - qi % 2, qi - qi % 2 + 1):
            for cp in loads(kb):
                cp.wait()

        @pl.when(qi % 2 == 1)
        def _():
            tile(qi - 1, False, 0)
            tile(qi, True, 1)

        @pl.when(qi % 2 == 0)
        def _():
            tile(qi, True, 0)

        dq_ref[...] = dq_acc[...] * (HEAD ** -0.5)

        @pl.when(qi == nq - 1)
        def _():
            ck = pltpu.make_async_copy(dk_acc, dk_hbm.at[g], sem.at[0])
            cv = pltpu.make_async_copy(dv_acc, dv_hbm.at[g], sem.at[1])
            ck.start()
            cv.start()
            ck.wait()
            cv.wait()

    blk = lambda j0: pl.BlockSpec((None, bq, SLAB), lambda g, i: (j0 + g, i, 0))
    full = lambda j0: pl.BlockSpec((None, s, SLAB), lambda g, i: ((j0 + g) // PER_SHARD, 0, (j0 + g) % PER_SHARD))
    q_blk = pl.BlockSpec((None, bq, SLAB), lambda g, i: ((2 + g) // PER_SHARD, i, (2 + g) % PER_SHARD))
    stack16 = pltpu.VMEM((N_HEAD * bq, SLAB), BF16)
    return pl.pallas_call(
        kern, name=name, grid=(2, nq),
        in_specs=[q_blk, full(4), full(6), blk(1),
                  pl.BlockSpec(memory_space=pl.ANY), pl.BlockSpec(memory_space=pl.ANY)],
        out_specs=[blk(0), pl.BlockSpec(memory_space=pl.ANY), pl.BlockSpec(memory_space=pl.ANY)],
        out_shape=[_sds((2, s, SLAB), F32)] * 3,
        scratch_shapes=[pltpu.VMEM((s, SLAB), F32), pltpu.VMEM((s, SLAB), F32), pltpu.VMEM((bq, SLAB), F32),
                        stack16, stack16, pltpu.VMEM((2, N_HEAD * bq, SLAB), BF16),
                        pltpu.VMEM((2, bq, N_HEAD * bq), BF16), pltpu.VMEM((2, N_HEAD * bq, bq), BF16),
                        pltpu.VMEM((4, N_HEAD * bq, bq), BF16), pltpu.VMEM((4, N_HEAD * bq, bq), BF16),
                        pltpu.VMEM((N_HEAD, bq, 128), F32),
                        pltpu.SemaphoreType.DMA((2,)), pltpu.SemaphoreType.DMA((2, 4))],
        compiler_params=_params(2))(p16, p16, p16, dcat, w_saved, lb_saved)


RET_BLOCK = 256


def _ret_tables(s, bl):
    nh = SLAB // HEAD
    lane_h = np.arange(SLAB) // HEAD
    log_gamma = np.log1p(-np.exp2(-5.0 - np.arange(nh, dtype=np.float64)))
    lg_lane = log_gamma[lane_h]
    half = HEAD // 2
    inv = 1.0 / (ROPE_BASE ** (np.arange(half, dtype=np.float64) / half))
    ang = np.arange(s, dtype=np.float64)[:, None] * inv[None, :]
    within = np.arange(SLAB) % HEAD
    cos = np.cos(ang)[:, within % half]
    sin = np.sin(ang)[:, within % half] * np.where(within < half, -1.0, 1.0)[None, :]
    perm = np.zeros((SLAB, SLAB))
    partner = np.where(within < half, np.arange(SLAB) + half, np.arange(SLAB) - half)
    perm[partner, np.arange(SLAB)] = 1.0
    i = np.arange(bl)
    diff = i[:, None] - i[None, :]
    same = (i[:, None] // CHUNK) == (i[None, :] // CHUNK)
    earlier = (i[None, :] // CHUNK) < (i[:, None] // CHUNK)
    decay = np.zeros((nh, bl, bl))
    for h in range(nh):
        decay[h] = np.where(same, np.exp(log_gamma[h] * np.abs(diff)),
                            np.where(earlier, np.exp(log_gamma[h] * diff), 0.0))
    qd = np.exp(lg_lane[None, :] * (i[:, None] + 1.0))
    kd = np.exp(lg_lane[None, :] * (bl - 1.0 - i[:, None]))
    gam = np.exp(lg_lane * bl)[:, None] * np.ones((1, SLAB))
    bd = (lane_h[:, None] == lane_h[None, :]).astype(np.float64)
    f = lambda a: jnp.asarray(a, F32)
    return f(cos), f(sin), f(perm), f(decay), f(qd), f(kd), f(gam), f(bd)


def _ret_block(q, k, v, state, cos, sin, perm, decay, qd, kd, gam, bd, hm):
    qr = (q * cos + jnp.dot(q, perm, preferred_element_type=F32) * sin) * (HEAD ** -0.5)
    kr = k * cos + jnp.dot(k, perm, preferred_element_type=F32) * sin
    y = jnp.dot(qr * qd, state, preferred_element_type=F32)
    for h in range(SLAB // HEAD):
        m = hm[h:h + 1]
        sc = lax.dot_general(qr * m, kr, NT, preferred_element_type=F32) * decay[h]
        y = y + jnp.dot(sc, v * m, preferred_element_type=F32)
    new_state = gam * state + lax.dot_general(kr * kd, v, TN, preferred_element_type=F32) * bd
    return y, new_state


def _ret_specs(s, bl, rev):
    nb = s // bl
    pos = (lambda n: nb - 1 - n) if rev else (lambda n: n)
    slab = lambda j: pl.BlockSpec((None, bl, SLAB), lambda n: (j // PER_SHARD, pos(n), j % PER_SHARD))
    const2 = lambda r: pl.BlockSpec((r, SLAB), lambda n: (0, 0))
    tab = [pl.BlockSpec((bl, SLAB), lambda n: (pos(n), 0))] * 2 + [
        const2(SLAB), pl.BlockSpec((SLAB // HEAD, bl, bl), lambda n: (0, 0, 0)),
        const2(bl), const2(bl), const2(SLAB), const2(SLAB), const2(8)]
    return nb, pos, slab, tab


def _ret_fwd(name, p32):
    s = p32.shape[1]
    bl = min(RET_BLOCK, s)
    nb, pos, slab, tab = _ret_specs(s, bl, False)
    tables = _ret_tables(s, bl) + (_head_masks(),)

    def kern(q_ref, k_ref, v_ref, *rest):
        t_refs, (y_ref, st_ref, state) = rest[:9], rest[9:]

        @pl.when(pl.program_id(0) == 0)
        def _():
            state[...] = jnp.zeros((SLAB, SLAB), F32)

        st_ref[...] = state[...]
        y, new = _ret_block(q_ref[...], k_ref[...], v_ref[...], state[...], *[t[...] for t in t_refs])
        y_ref[...] = y
        state[...] = new

    return pl.pallas_call(
        kern, name=name, grid=(nb,), in_specs=[slab(8), slab(9), slab(10)] + tab,
        out_specs=[pl.BlockSpec((bl, SLAB), lambda n: (n, 0)), pl.BlockSpec((None, SLAB, SLAB), lambda n: (n, 0, 0))],
        out_shape=[_sds((s, SLAB), F32), _sds((nb, SLAB, SLAB), F32)],
        scratch_shapes=[pltpu.VMEM((SLAB, SLAB), F32)], compiler_params=_params(1))(p32, p32, p32, *tables)


def _ret_bwd(name, p32, states, dy):
    s = p32.shape[1]
    bl = min(RET_BLOCK, s)
    nb, pos, slab, tab = _ret_specs(s, bl, True)
    tables = _ret_tables(s, bl) + (_head_masks(),)
    rowblk = pl.BlockSpec((bl, SLAB), lambda n: (pos(n), 0))

    def kern(q_ref, k_ref, v_ref, st_ref, dy_ref, *rest):
        t_refs, (dq_ref, dk_ref, dv_ref, dstate) = rest[:9], rest[9:]

        @pl.when(pl.program_id(0) == 0)
        def _():
            dstate[...] = jnp.zeros((SLAB, SLAB), F32)

        tv = [t[...] for t in t_refs]
        _, vjp = jax.vjp(lambda a, b, c, d: _ret_block(a, b, c, d, *tv),
                         q_ref[...], k_ref[...], v_ref[...], st_ref[...])
        dq, dk, dv, ds = vjp((dy_ref[...], dstate[...]))
        dq_ref[...] = dq
        dk_ref[...] = dk
        dv_ref[...] = dv
        dstate[...] = ds

    return pl.pallas_call(
        kern, name=name, grid=(nb,),
        in_specs=[slab(8), slab(9), slab(10), pl.BlockSpec((None, SLAB, SLAB), lambda n: (pos(n), 0, 0)), rowblk] + tab,
        out_specs=[rowblk] * 3, out_shape=[_sds((s, SLAB), F32)] * 3,
        scratch_shapes=[pltpu.VMEM((SLAB, SLAB), F32)], compiler_params=_params(1))(p32, p32, p32, states, dy, *tables)


TM_FFN = 1024
TM_SLAB = 2048
TM_NORM = 256
TM_OUT = 512
TM_GRAD = 2048
TM_RW = 256
TM_FF = 256


def _ffn_fwd(tag, x, g, w_in, w_out, l):
    tm = min(TM_FFN, x.shape[0])
    h, (u,) = _norm_proj_in(tag + "_in", x, g, w_in, l, tm, [BF16])
    a = _swiglu_fwd(tag + "_act", u, TM_FF)
    w_out2 = w_out.reshape(w_out.shape[0], 2, FF_SHARD, D_MODEL)
    xn = _proj_out(tag + "_out", a, w_out2, l, x, 0.5, min(TM_OUT, x.shape[0]))
    return xn, (x, h, u, a)


def _ffn_bwd(tag, saved, dxn, g, w_in, w_out, l):
    x, h, u, a = saved
    tm = min(TM_FFN, x.shape[0])
    w_out2 = w_out.reshape(w_out.shape[0], 2, FF_SHARD, D_MODEL)
    da = _back_out(tag + "_dact", dxn, w_out2, l, 0.5, tm, BF16)
    tk = min(TM_GRAD, x.shape[0])
    dw_out = _grad_out(tag + "_dwout", a, dxn, 0.5, tk)
    du = _swiglu_bwd(tag + "_dswi", u, da, TM_FF)
    dx, dg = _back_in_norm(tag + "_dh", du, w_in, l, min(TM_NORM, x.shape[0]), x, dxn, g)
    dw_in = _grad_in(tag + "_dwin", h, du, tk)
    return dx, dg, dw_in, dw_out.reshape(N_SHARD, D_FF // N_SHARD, D_MODEL)


def _mix_fwd(tag, x, sm, w_in, w_out, l, wl):
    h, (p32, p16) = _norm_proj_in(tag + "_in", x, sm["mix_norm"][l:l + 1], w_in, wl, min(TM_FFN, x.shape[0]),
                                  [F32, BF16])
    ypre = _conv_fwd(tag + "_conv", p32, sm["conv_w"][l], sm["conv_b"][l:l + 1])
    yconv = _ln_silu_fwd(tag + "_ln", ypre, sm["conv_ln_g"][l:l + 1], sm["conv_ln_b"][l:l + 1], TM_RW)
    osb, w_sb, lb_sb = _sb_fwd(tag + "_sb", p16)
    yr, states = _ret_fwd(tag + "_ret", p32)
    yret = _ghn_fwd(tag + "_ghn", yr, p32, sm["ret_norm_g"][l:l + 1], TM_RW)
    ycat = _assemble(tag + "_cat", [(yconv, None), (osb, 0), (osb, 1), (yret, None)], 1, TM_RW)
    xn = _proj_out(tag + "_out", ycat, w_out, wl, x, 1.0, min(TM_OUT, x.shape[0]))
    return xn, (x, h, p32, p16, ypre, (osb, w_sb, lb_sb), yr, states, ycat)


def _mix_bwd(tag, saved, dxn, sm, w_in, w_out, l, wl):
    x, h, p32, p16, ypre, osb, yr, states, ycat = saved
    ts = min(TM_SLAB, x.shape[0])
    dcat = _back_out(tag + "_dcat", dxn, w_out, wl, 1.0, ts, F32)
    dw_out = _grad_out(tag + "_dwout", ycat, dxn, 1.0, ts)
    dypre, dlg, dlb = _ln_silu_bwd(tag + "_dln", ypre, dcat, sm["conv_ln_g"][l:l + 1], sm["conv_ln_b"][l:l + 1], TM_RW)
    da, db, dcw, dcb = _conv_bwd(tag + "_dconv", p32, sm["conv_w"][l], dypre)
    dq, dk, dv = _sb_bwd(tag + "_dsb", p16, osb[1], osb[2], dcat)
    dyr, dgate, drg = _ghn_bwd(tag + "_dghn", yr, p32, dcat, sm["ret_norm_g"][l:l + 1], TM_RW)
    dqr, dkr, dvr = _ret_bwd(tag + "_dret", p32, states, dyr)
    dp = _assemble(tag + "_dp", [(da, None), (db, None), (dq, 0), (dq, 1), (dk, 0), (dk, 1), (dv, 0), (dv, 1),
                                 (dqr, None), (dkr, None), (dvr, None), (dgate, None)], PER_SHARD, TM_RW)
    dx, dg = _back_in_norm(tag + "_dh", dp, w_in, wl, min(TM_NORM, x.shape[0]), x, dxn, sm["mix_norm"][l:l + 1])
    dw_in = _grad_in(tag + "_dwin", h, dp, min(TM_GRAD, x.shape[0]))
    small = dict(mix_norm=dg, conv_w=dcw[0:CONV_W], conv_b=dcb, conv_ln_g=dlg, conv_ln_b=dlb, ret_norm_g=drg)
    return dx, small, dw_in, dw_out


def _local_step(x, tgt, early, last, sm, on_last_layer=None):
    saved = []
    weights = [(early, l) for l in range(DEPTH - 1)]
    for l in range(DEPTH):
        if l == DEPTH - 1:
            weights.append((last(x), 0))
        wt, wl = weights[l]
        x, s1 = _ffn_fwd(f"l{l}f1", x, sm["ffn1_norm"][l:l + 1], wt["ffn1_w_in"], wt["ffn1_w_out"], wl)
        x, s2 = _mix_fwd(f"l{l}mx", x, sm, wt["mix_w_in"], wt["mix_w_out"], l, wl)
        x, s3 = _ffn_fwd(f"l{l}f2", x, sm["ffn2_norm"][l:l + 1], wt["ffn2_w_in"], wt["ffn2_w_out"], wl)
        saved.append((s1, s2, s3))
    dx, dfinal, loss = _final("final", x, tgt, sm["final_norm"][None, :], TM_RW)
    big = [None] * DEPTH
    small = [None] * DEPTH
    for l in reversed(range(DEPTH)):
        s1, s2, s3 = saved[l]
        wt, wl = weights[l]
        dx, dg3, dwi3, dwo3 = _ffn_bwd(f"l{l}f2", s3, dx, sm["ffn2_norm"][l:l + 1], wt["ffn2_w_in"], wt["ffn2_w_out"], wl)
        dx, sml, dwi2, dwo2 = _mix_bwd(f"l{l}mx", s2, dx, sm, wt["mix_w_in"], wt["mix_w_out"], l, wl)
        dx, dg1, dwi1, dwo1 = _ffn_bwd(f"l{l}f1", s1, dx, sm["ffn1_norm"][l:l + 1], wt["ffn1_w_in"], wt["ffn1_w_out"], wl)
        big[l] = dict(ffn1_w_in=dwi1, ffn1_w_out=dwo1, mix_w_in=dwi2, mix_w_out=dwo2, ffn2_w_in=dwi3, ffn2_w_out=dwo3)
        sml.update(ffn1_norm=dg1, ffn2_norm=dg3)
        small[l] = sml
        if on_last_layer is not None and l == DEPTH - 1:
            dx = dx + on_last_layer(big[l])[0:1, 0:1]
    return loss, dx, big, small, dfinal


MESH = pl.DeviceIdType.MESH
ANY = pl.BlockSpec(memory_space=pl.ANY)
BIG = ("ffn1_w_in", "ffn1_w_out", "mix_w_in", "mix_w_out", "ffn2_w_in", "ffn2_w_out")


def _place():
    x, y, c = lax.axis_index("x"), lax.axis_index("y"), lax.axis_index("c")
    chips = [(1 - x, y), (x, 1 - y), (1 - x, 1 - y)]
    return x, y, c, chips


def _gather_weights(w16):
    n = len(w16)

    def kern(*refs):
        dst = refs[n:2 * n]
        send, recv = refs[2 * n:]
        x, y, c, chips = _place()
        mine = 2 * x + y
        firsts, passes = [], []
        for a in range(n):
            h = dst[a].shape[2] // 2
            own = dst[a].at[:, mine, pl.ds(c * h, h)]
            for j, (cx, cy) in enumerate(chips):
                cp = pltpu.make_async_remote_copy(
                    src_ref=own, dst_ref=own, send_sem=send.at[6 * a + j], recv_sem=recv.at[6 * a + j],
                    device_id=(cx, cy, c), device_id_type=MESH)
                cp.start()
                firsts.append(cp)
        for a in range(n):
            h = dst[a].shape[2] // 2
            half = pl.ds(c * h, h)
            for j, (cx, cy) in enumerate(chips):
                theirs = dst[a].at[:, 2 * cx + cy, half]
                pltpu.make_async_remote_copy(
                    src_ref=theirs, dst_ref=theirs, send_sem=send.at[6 * a + j], recv_sem=recv.at[6 * a + j],
                    device_id=(cx, cy, c), device_id_type=MESH).wait_recv()
                fw = pltpu.make_async_remote_copy(
                    src_ref=theirs, dst_ref=theirs, send_sem=send.at[6 * a + 3 + j], recv_sem=recv.at[6 * a + 3 + j],
                    device_id=(x, y, 1 - c), device_id_type=MESH)
                fw.start()
                passes.append(fw)
        for a in range(n):
            h = dst[a].shape[2] // 2
            other = pl.ds((1 - c) * h, h)
            for j, (cx, cy) in enumerate(chips):
                got = dst[a].at[:, 2 * cx + cy, other]
                pltpu.make_async_remote_copy(
                    src_ref=got, dst_ref=got, send_sem=send.at[6 * a + 3 + j], recv_sem=recv.at[6 * a + 3 + j],
                    device_id=(x, y, 1 - c), device_id_type=MESH).wait_recv()
        for cp in firsts + passes:
            cp.wait_send()

    return pl.pallas_call(
        kern, name="gather_weights", in_specs=[ANY] * n, out_specs=[ANY] * n,
        out_shape=[_sds(w.shape, w.dtype) for w in w16], input_output_aliases={a: a for a in range(n)},
        scratch_shapes=[pltpu.SemaphoreType.DMA((6 * n,)), pltpu.SemaphoreType.DMA((6 * n,))])(*w16)


def _last_layer_copies(bufs, send, recv, x, y, c, chips):
    mine = 2 * x + y
    sends, arrivals = [], []
    for a, buf in enumerate(bufs):
        last = buf.shape[0] - 1
        h = buf.shape[2] // 2
        own = buf.at[last, mine, pl.ds(c * h, h)]
        for j, (cx, cy) in enumerate(chips):
            for k in range(2):
                sem = 6 * a + 2 * j + k
                sends.append(pltpu.make_async_remote_copy(
                    src_ref=own, dst_ref=own, send_sem=send.at[sem], recv_sem=recv.at[sem],
                    device_id=(cx, cy, c ^ k), device_id_type=MESH))
                theirs = buf.at[last, 2 * cx + cy, pl.ds((c ^ k) * h, h)]
                arrivals.append(pltpu.make_async_remote_copy(
                    src_ref=theirs, dst_ref=theirs, send_sem=send.at[sem], recv_sem=recv.at[sem],
                    device_id=(cx, cy, c ^ k), device_id_type=MESH))
    return sends, arrivals


def _gather_last_start(w16, after):
    n = len(w16)

    def kern(*refs):
        bufs, send, recv, token = refs[:n], refs[n + 1], refs[n + 2], refs[-1]
        x, y, c, chips = _place()
        for cp in _last_layer_copies(bufs, send, recv, x, y, c, chips)[0]:
            cp.start()
        token[...] = jnp.zeros_like(token)

    out = pl.pallas_call(
        kern, name="gather_last_start",
        out_shape=(pltpu.SemaphoreType.DMA((6 * n,)), pltpu.SemaphoreType.DMA((6 * n,)),
                   *[pltpu.HBM(w.shape, w.dtype) for w in w16], _sds((8, 128), F32)),
        in_specs=[HBM] * n + [ANY],
        out_specs=(SEM, SEM, *[HBM] * n, pl.BlockSpec(memory_space=pltpu.VMEM)),
        input_output_aliases={i: 2 + i for i in range(n)},
        compiler_params=pltpu.CompilerParams(has_side_effects=EFFECT),
    )(*[pltpu.with_memory_space_constraint(w, pltpu.HBM) for w in w16], after)
    return out[0], out[1], out[2:2 + n], out[-1]


def _gather_last_wait(send, recv, w16, after):
    n = len(w16)

    def kern(*refs):
        bufs, send_sem, recv_sem = refs[:n], refs[n], refs[n + 1]
        x, y, c, chips = _place()
        sends, arrivals = _last_layer_copies(bufs, send_sem, recv_sem, x, y, c, chips)
        for cp in sends:
            cp.wait_send()
        for cp in arrivals:
            cp.wait_recv()

    return pl.pallas_call(
        kern, name="gather_last_wait", out_shape=tuple(pltpu.HBM(w.shape, w.dtype) for w in w16),
        in_specs=[HBM] * n + [SEM, SEM, ANY], out_specs=tuple([HBM] * n),
        input_output_aliases={i: i for i in range(n)},
        compiler_params=pltpu.CompilerParams(has_side_effects=EFFECT),
    )(*w16, send, recv, after)


def _pair_exchange(name, grads):
    n = len(grads)

    def kern(*refs):
        src, got_o = refs[:n], refs[n:2 * n]
        send, recv = refs[2 * n:]
        x, y, c, _ = _place()
        cps = []
        for a in range(n):
            h = src[a].shape[1] // 2
            cp = pltpu.make_async_remote_copy(
                src_ref=src[a].at[:, pl.ds((1 - c) * h, h)], dst_ref=got_o[a],
                send_sem=send.at[a], recv_sem=recv.at[a], device_id=(x, y, 1 - c), device_id_type=MESH)
            cp.start()
            cps.append(cp)
        for cp in cps:
            cp.wait()

    halves = [_sds((g.shape[0], g.shape[1] // 2, g.shape[2]), g.dtype) for g in grads]
    return pl.pallas_call(
        kern, name=name, in_specs=[ANY] * n, out_specs=[ANY] * n, out_shape=halves,
        scratch_shapes=[pltpu.SemaphoreType.DMA((n,)), pltpu.SemaphoreType.DMA((n,))])(*grads)


def _chip_exchange(name, sums):
    n = len(sums)

    def kern(*refs):
        src, dst = refs[:n], refs[n:2 * n]
        send, recv = refs[2 * n:]
        x, y, c, chips = _place()
        cps = _chip_copies(src, dst, send, recv, x, y, c, chips)
        for cp in cps:
            cp.start()
        for cp in cps:
            cp.wait()

    return pl.pallas_call(
        kern, name=name, in_specs=[ANY] * n, out_specs=[ANY] * n,
        out_shape=[_sds((3,) + s_.shape[1:], s_.dtype) for s_ in sums],
        scratch_shapes=[pltpu.SemaphoreType.DMA((3 * n,)), pltpu.SemaphoreType.DMA((3 * n,))])(*sums)


HBM = pl.BlockSpec(memory_space=pltpu.HBM)
SEM = pl.BlockSpec(memory_space=pltpu.SEMAPHORE)
EFFECT = pltpu.SideEffectType.DATAFLOW_SIDE_EFFECTING


def _chip_copies(src, land, send, recv, x, y, c, chips):
    return [pltpu.make_async_remote_copy(
        src_ref=src[a].at[2 * cx + cy], dst_ref=land[a].at[j], send_sem=send.at[3 * a + j],
        recv_sem=recv.at[3 * a + j], device_id=(cx, cy, c), device_id_type=MESH)
        for a in range(len(src)) for j, (cx, cy) in enumerate(chips)]


def _chip_exchange_start(sums):
    n = len(sums)

    def kern(*refs):
        src, land = refs[:n], refs[n:2 * n]
        send, recv, token = refs[2 * n], refs[2 * n + 1], refs[-1]
        x, y, c, chips = _place()
        for cp in _chip_copies(src, land, send, recv, x, y, c, chips):
            cp.start()
        token[...] = jnp.zeros_like(token)

    lands = [(3,) + s_.shape[1:] for s_ in sums]
    out = pl.pallas_call(
        kern, name="chip_exchange_start",
        out_shape=(pltpu.SemaphoreType.DMA((3 * n,)), pltpu.SemaphoreType.DMA((3 * n,)),
                   *[pltpu.HBM(s_.shape, s_.dtype) for s_ in sums],
                   *[pltpu.HBM(shp, s_.dtype) for shp, s_ in zip(lands, sums)], _sds((8, 128), F32)),
        in_specs=[HBM] * (2 * n),
        out_specs=(SEM, SEM, *[HBM] * (2 * n), pl.BlockSpec(memory_space=pltpu.VMEM)),
        input_output_aliases={i: 2 + i for i in range(2 * n)},
        compiler_params=pltpu.CompilerParams(has_side_effects=EFFECT),
    )(*[pltpu.with_memory_space_constraint(s_, pltpu.HBM) for s_ in sums],
      *[pltpu.with_memory_space_constraint(lax.empty(shp, s_.dtype), pltpu.HBM) for shp, s_ in zip(lands, sums)])
    return out[0], out[1], out[2:2 + n], out[2 + n:2 + 2 * n], out[-1]


def _chip_exchange_wait(send, recv, sums, lands, after):
    n = len(sums)

    def kern(*refs):
        src, land = refs[:n], refs[n:2 * n]
        send_sem, recv_sem = refs[2 * n], refs[2 * n + 1]
        x, y, c, chips = _place()
        for cp in _chip_copies(src, land, send_sem, recv_sem, x, y, c, chips):
            cp.wait_send()
            cp.wait_recv()

    out = pl.pallas_call(
        kern, name="chip_exchange_wait",
        out_shape=tuple(pltpu.HBM(t.shape, t.dtype) for t in list(sums) + list(lands)),
        in_specs=[HBM] * (2 * n) + [SEM, SEM, ANY], out_specs=tuple([HBM] * (2 * n)),
        input_output_aliases={i: i for i in range(2 * n)},
        compiler_params=pltpu.CompilerParams(has_side_effects=EFFECT),
    )(*sums, *lands, send, recv, after)
    return out[:n], out[n:]


def _pair_join(full):
    n = len(full)

    def kern(*refs):
        dst = refs[n:2 * n]
        send, recv = refs[2 * n:]
        x, y, c, _ = _place()
        cps = []
        for a in range(n):
            h = dst[a].shape[1] // 2
            mine = dst[a].at[:, pl.ds(c * h, h)]
            cp = pltpu.make_async_remote_copy(
                src_ref=mine, dst_ref=mine, send_sem=send.at[a], recv_sem=recv.at[a],
                device_id=(x, y, 1 - c), device_id_type=MESH)
            cp.start()
            cps.append(cp)
        for a, cp in enumerate(cps):
            cp.wait_send()
            h = dst[a].shape[1] // 2
            got = dst[a].at[:, pl.ds((1 - c) * h, h)]
            pltpu.make_async_remote_copy(
                src_ref=got, dst_ref=got, send_sem=send.at[a], recv_sem=recv.at[a],
                device_id=(x, y, 1 - c), device_id_type=MESH).wait_recv()

    return pl.pallas_call(
        kern, name="pair_join", in_specs=[ANY] * n, out_specs=[ANY] * n,
        out_shape=[_sds(f.shape, f.dtype) for f in full], input_output_aliases={a: a for a in range(n)},
        scratch_shapes=[pltpu.SemaphoreType.DMA((n,)), pltpu.SemaphoreType.DMA((n,))])(*full)


def _all_sum(name, v):
    r = v.shape[0]

    def kern(v_ref, o_ref, buf, send, recv):
        x, y, c, _ = _place()
        me = 4 * x + 2 * y + c
        buf[me] = v_ref[...]
        cps = []
        for k in range(1, 8):
            peer = (x ^ (k >> 2), y ^ ((k >> 1) & 1), c ^ (k & 1))
            cp = pltpu.make_async_remote_copy(
                src_ref=v_ref, dst_ref=buf.at[me], send_sem=send.at[k - 1], recv_sem=recv.at[k - 1],
                device_id=peer, device_id_type=MESH)
            cp.start()
            cps.append(cp)
        for k in range(1, 8):
            peer_id = me ^ k
            pltpu.make_async_remote_copy(
                src_ref=v_ref, dst_ref=buf.at[peer_id], send_sem=send.at[k - 1], recv_sem=recv.at[k - 1],
                device_id=(x, y, c), device_id_type=MESH).wait_recv()
        for cp in cps:
            cp.wait_send()
        acc = buf[0]
        for d in range(1, 8):
            acc = acc + buf[d]
        o_ref[...] = acc

    vm = pl.BlockSpec(memory_space=pltpu.VMEM)
    return pl.pallas_call(
        kern, name=name, in_specs=[vm], out_specs=vm, out_shape=_sds((r, 128), F32),
        scratch_shapes=[pltpu.VMEM((8, r, 128), F32), pltpu.SemaphoreType.DMA((7,)),
                        pltpu.SemaphoreType.DMA((7,))])(v)


def _my_chip():
    return 2 * lax.axis_index("x") + lax.axis_index("y")


def _my_core():
    return lax.axis_index("c")


def _cast_place(name, w, first, count):
    _, r, c = w.shape
    tr = r // 4
    return _rw(name, lambda wb: ((wb,), ()), (count, r // tr), [w],
               [pl.BlockSpec((None, tr, c), lambda j, i: (first + j, i, 0))],
               [_sds((count, N_SHARD, r, c), BF16)],
               [pl.BlockSpec((None, None, tr, c), lambda j, i: (j, _my_chip(), i, 0))])[0]


HALF_STEPS = 2


def _add_halves(name, g, got):
    n, h, c = got.shape
    tr, nt = h // HALF_STEPS, HALF_STEPS
    return _rw(name, lambda ab, bb: ((ab.astype(F32) + bb.astype(F32),), ()), (nt,), [g, got],
               [pl.BlockSpec((n, tr, c), lambda i: (0, _my_core() * nt + i, 0)),
                pl.BlockSpec((n, tr, c), lambda i: (0, i, 0))],
               [_sds((n, h, c), BF16)], [pl.BlockSpec((n, tr, c), lambda i: (0, i, 0))])[0]


def _sum_parts(name, sums, parts, full, layer, n_layer):
    _, h, c = sums.shape
    tr, nt = h // HALF_STEPS, HALF_STEPS

    def body(own, pb):
        acc = own.astype(F32)
        for j in range(pb.shape[0]):
            acc = acc + pb[j].astype(F32)
        return (acc,), ()

    ins = [sums, parts] + ([full] if full is not None else [])
    in_specs = [pl.BlockSpec((None, tr, c), lambda i: (_my_chip(), i, 0)),
                pl.BlockSpec((parts.shape[0], tr, c), lambda i: (0, i, 0))] + ([ANY] if full is not None else [])
    return _rw(name, body, (nt,), ins, in_specs, [_sds((n_layer, 2 * h, c), F32)],
               [pl.BlockSpec((None, tr, c), lambda i: (layer, _my_core() * nt + i, 0))],
               aliases={2: 0} if full is not None else None)[0]


def _adamw_math(w, g, m, v):
    m = B1 * m + (1.0 - B1) * g
    v = B2 * v + (1.0 - B2) * (g * g)
    m_hat = m / (1.0 - B1 ** STEP)
    v_hat = v / (1.0 - B2 ** STEP)
    delta = -LR * (m_hat / (jnp.sqrt(v_hat) + ADAM_EPS) + WD * w)
    return delta, m, v


def _adamw(name, w, g, m, v):
    r, c = w.shape
    tr = 64 if r % 64 == 0 else 8
    spec = _row_spec(tr, c)
    return _rw(name, lambda *b: (_adamw_math(*b), ()), (r // tr,), [w, g, m, v], [spec] * 4,
               [_sds((r, c), F32)] * 3, [spec] * 3)


SMALL = (("ffn1_norm", (DEPTH, D_MODEL)), ("mix_norm", (DEPTH, D_MODEL)), ("ffn2_norm", (DEPTH, D_MODEL)),
         ("conv_b", (DEPTH, SLAB)), ("conv_ln_g", (DEPTH, SLAB)), ("conv_ln_b", (DEPTH, SLAB)),
         ("ret_norm_g", (DEPTH, SLAB)), ("final_norm", (D_MODEL,)), ("conv_w", (DEPTH, CONV_W, SLAB)))


def _pack(parts, rows):
    flat = jnp.concatenate([p.reshape(-1) for p in parts])
    return jnp.pad(flat, (0, rows * 128 - flat.shape[0])).reshape(rows, 128)


def _unpack(packed, shapes):
    flat = packed.reshape(-1)
    out, off = [], 0
    for shp in shapes:
        n = int(np.prod(shp))
        out.append(flat[off:off + n].reshape(shp))
        off += n
    return out


def kernel(x, ffn1_norm, ffn1_w_in, ffn1_w_out, mix_norm, mix_w_in, conv_w, conv_b, conv_ln_g, conv_ln_b, ret_norm_g, mix_w_out, ffn2_norm, ffn2_w_in, ffn2_w_out, final_norm, loss_target, m_ffn1_norm, m_ffn1_w_in, m_ffn1_w_out, m_mix_norm, m_mix_w_in, m_conv_w, m_conv_b, m_conv_ln_g, m_conv_ln_b, m_ret_norm_g, m_mix_w_out, m_ffn2_norm, m_ffn2_w_in, m_ffn2_w_out, m_final_norm, v_ffn1_norm, v_ffn1_w_in, v_ffn1_w_out, v_mix_norm, v_mix_w_in, v_conv_w, v_conv_b, v_conv_ln_g, v_conv_ln_b, v_ret_norm_g, v_mix_w_out, v_ffn2_norm, v_ffn2_w_in, v_ffn2_w_out, v_final_norm):
    given = dict(locals())
    names = [n for n, _ in SMALL] + list(BIG)
    chip = 2 * lax.axis_index("x") + lax.axis_index("y")
    core = lax.axis_index("c")

    cw_rows = 128
    placed = lax.dynamic_update_slice(jnp.zeros((DEPTH, CONV_W, SLAB), F32), conv_w, (0, 0, chip * HEAD))
    placed = placed * (core == 0).astype(F32)
    conv_w_full = _unpack(_all_sum("gather_conv_w", _pack([placed], cw_rows)), [(DEPTH, CONV_W, SLAB)])[0]

    early = dict(zip(BIG, _gather_weights([_cast_place("cast_" + n, given[n], 0, DEPTH - 1) for n in BIG])))
    w_send, w_recv, arriving, started = _gather_last_start(
        [_cast_place("cast_last_" + n, given[n], DEPTH - 1, 1) for n in BIG], early[BIG[0]])
    sm = {n: given[n] for n, _ in SMALL}
    sm["conv_w"] = conv_w_full
    sm["ffn1_norm"] = ffn1_norm + started[0:1, 0:1]

    def last_weights(activations):
        return dict(zip(BIG, _gather_last_wait(w_send, w_recv, arriving, activations)))

    def chip_sums(tag, layer_grads):
        grads = [layer_grads[n] for n in BIG]
        theirs = _pair_exchange("pair_exchange_" + tag, grads)
        return [_add_halves(f"chipsum_{tag}{i}", a, b) for i, (a, b) in enumerate(zip(grads, theirs))]

    in_flight = []

    def start_last(layer_grads):
        in_flight.extend(_chip_exchange_start(chip_sums("last", layer_grads)))
        return in_flight[4]

    loss, dx, big, small, dfinal = _local_step(x[0], loss_target[0], early, last_weights, sm, start_last)
    sums, parts = [None] * DEPTH, [None] * DEPTH
    sums[DEPTH - 1], parts[DEPTH - 1] = _chip_exchange_wait(*in_flight[:4], dx)
    for l in range(DEPTH - 1):
        sums[l] = chip_sums(f"l{l}", big[l])
        parts[l] = _chip_exchange(f"chip_exchange_l{l}", sums[l])
    full = []
    for i in range(len(BIG)):
        f = None
        for l in range(DEPTH):
            f = _sum_parts(f"shardsum{DEPTH * i + l}", sums[l][i], parts[l][i], f, l, DEPTH)
        full.append(f)
    g_big = dict(zip(BIG, _pair_join(full)))

    small_parts = []
    for n, shp in SMALL:
        if n == "final_norm":
            small_parts.append(dfinal)
        else:
            small_parts.append(jnp.stack([small[l][n].reshape(shp[1:]) for l in range(DEPTH)]))
    g_small = dict(zip([n for n, _ in SMALL], _unpack(_all_sum("sum_small", _pack(small_parts, 200)), [s_ for _, s_ in SMALL])))
    g_small["conv_w"] = lax.dynamic_slice(g_small["conv_w"], (0, 0, chip * HEAD), (DEPTH, CONV_W, HEAD))

    grad, delta, new_m, new_v = dict(g_small), {}, {}, {}
    grad.update(g_big)
    for n in BIG:
        l, r, c = given[n].shape
        f = lambda t: t.reshape(l * r, c)
        d_, m_, v_ = _adamw("adamw_" + n, f(given[n]), f(grad[n]), f(given["m_" + n]), f(given["v_" + n]))
        delta[n], new_m[n], new_v[n] = d_.reshape(l, r, c), m_.reshape(l, r, c), v_.reshape(l, r, c)
    snames = [n for n, _ in SMALL]
    shapes = [given[n].shape for n in snames]
    rows = 104
    d_, m_, v_ = _adamw("adamw_small", _pack([given[n] for n in snames], rows), _pack([grad[n] for n in snames], rows),
                        _pack([given["m_" + n] for n in snames], rows), _pack([given["v_" + n] for n in snames], rows))
    for dst, packed in ((delta, d_), (new_m, m_), (new_v, v_)):
        dst.update(zip(snames, _unpack(packed, shapes)))

    total = lax.psum(loss[0, 0], ("x", "y", "c"))
    order = ["ffn1_norm", "ffn1_w_in", "ffn1_w_out", "mix_norm", "mix_w_in", "conv_w", "conv_b", "conv_ln_g",
             "conv_ln_b", "ret_norm_g", "mix_w_out", "ffn2_norm", "ffn2_w_in", "ffn2_w_out", "final_norm"]
    return (total, dx[None], *[grad[n] for n in order], *[delta[n] for n in order],
            *[new_m[n] for n in order], *[new_v[n] for n in order])
```

```python
import functools

import numpy as np
import jax
import jax.numpy as jnp
from jax import lax
from jax.experimental import pallas as pl
from jax.experimental.pallas import tpu as pltpu

F32 = jnp.float32
BF16 = jnp.bfloat16

D_MODEL = 1024
D_FF = 2816
N_SHARD = 4
FF_SHARD = 2 * D_FF // N_SHARD
MIX_SHARD = 3072 // N_SHARD
HEAD = 64
SLAB = 256
CONV_W = 31
CONV_PAD = 32
CHUNK = 64
EPS = 1e-6
ROPE_BASE = 10000.0
DEPTH = 2

LR, B1, B2, ADAM_EPS, WD, STEP = 0.001, 0.9, 0.999, 1e-08, 0.01, 10

VMEM_LIMIT = 56 * 1024 * 1024


def _params(n_grid):
    return pltpu.CompilerParams(dimension_semantics=("arbitrary",) * n_grid, vmem_limit_bytes=VMEM_LIMIT)


def _rw(name, body, grid, ins, in_specs, rows=(), row_specs=(), accs=(), acc_specs=(), aliases=None):
    n_in, n_row = len(ins), len(rows)
    carried = sorted(aliases) if aliases else []

    def kern(*refs):
        vals = [r[...] for i, r in enumerate(refs[:n_in]) if i not in carried]
        row_vals, acc_vals = body(*vals)
        for r, v in zip(refs[n_in:n_in + n_row], row_vals):
            r[...] = v.astype(r.dtype)
        acc_refs = refs[n_in + n_row:]
        if acc_refs:
            first = functools.reduce(jnp.logical_and, [pl.program_id(a) == 0 for a in range(len(grid))])

            @pl.when(first)
            def _():
                for r in acc_refs:
                    r[...] = jnp.zeros(r.shape, r.dtype)

            for r, v in zip(acc_refs, acc_vals):
                r[...] += v.astype(r.dtype)

    return pl.pallas_call(
        kern, name=name, grid=grid, in_specs=list(in_specs), out_specs=list(row_specs) + list(acc_specs),
        out_shape=list(rows) + list(accs), input_output_aliases=dict(aliases or {}),
        compiler_params=_params(len(grid)))(*ins)


def _sds(shape, dtype):
    return jax.ShapeDtypeStruct(shape, dtype)


def _rms(x, g):
    return x * lax.rsqrt(jnp.mean(x * x, axis=-1, keepdims=True) + EPS) * g


def _row_spec(tm, c):
    return pl.BlockSpec((tm, c), lambda i: (i, 0))


def _vec_spec(c):
    return pl.BlockSpec((1, c), lambda i: (0, 0))


def _swiglu(gate, up):
    return jax.nn.silu(gate) * up


def _swiglu_fwd(name, u, tm):
    _, s, c = u.shape
    return _rw(name, lambda ub: ((_swiglu(ub[0:2].astype(F32), ub[2:4].astype(F32)),), ()), (s // tm,), [u],
               [pl.BlockSpec((4, tm, c), lambda i: (0, i, 0))],
               [_sds((2, s, c), BF16)], [pl.BlockSpec((2, tm, c), lambda i: (0, i, 0))])[0]


def _swiglu_bwd(name, u, da, tm):
    _, s, c = u.shape

    def body(ub, dab):
        _, vjp = jax.vjp(_swiglu, ub[0:2].astype(F32), ub[2:4].astype(F32))
        dg, du = vjp(dab.astype(F32))
        return (jnp.concatenate([dg, du], axis=0),), ()

    return _rw(name, body, (s // tm,), [u, da],
               [pl.BlockSpec((4, tm, c), lambda i: (0, i, 0)), pl.BlockSpec((2, tm, c), lambda i: (0, i, 0))],
               [_sds((4, s, c), BF16)], [pl.BlockSpec((4, tm, c), lambda i: (0, i, 0))])[0]


def _ln_silu(y, g, b):
    mu = jnp.mean(y, axis=-1, keepdims=True)
    yc = y - mu
    var = jnp.mean(yc * yc, axis=-1, keepdims=True)
    return jax.nn.silu(yc * lax.rsqrt(var + EPS) * g + b)


def _ln_silu_fwd(name, y, g, b, tm):
    s, c = y.shape
    return _rw(name, lambda yb, gb, bb: ((_ln_silu(yb, gb, bb),), ()), (s // tm,), [y, g, b],
               [_row_spec(tm, c), _vec_spec(c), _vec_spec(c)], [_sds((s, c), BF16)], [_row_spec(tm, c)])[0]


def _ln_silu_bwd(name, y, dcat, g, b, tm):
    s, c = y.shape

    def body(yb, dob, gb, bb):
        _, vjp = jax.vjp(_ln_silu, yb, gb, bb)
        dy, dg, db = vjp(dob)
        return (dy,), (dg, db)

    return _rw(name, body, (s // tm,), [y, dcat, g, b],
               [_row_spec(tm, c), pl.BlockSpec((None, tm, c), lambda i: (0, i, 0)), _vec_spec(c), _vec_spec(c)],
               [_sds((s, c), F32)], [_row_spec(tm, c)],
               [_sds((1, c), F32)] * 2, [_vec_spec(c)] * 2)


def _head_masks():
    lane = np.arange(SLAB) // HEAD
    m = np.zeros((8, SLAB), np.float32)
    for h in range(SLAB // HEAD):
        m[h] = (lane == h)
    return jnp.asarray(m)


def _gated_head_norm(y, gate, g, hm):
    mu = jnp.zeros_like(y)
    for h in range(SLAB // HEAD):
        mu = mu + hm[h:h + 1] * (jnp.sum(y * hm[h:h + 1], axis=-1, keepdims=True) / HEAD)
    yc = y - mu
    var = jnp.zeros_like(y)
    for h in range(SLAB // HEAD):
        var = var + hm[h:h + 1] * (jnp.sum(yc * yc * hm[h:h + 1], axis=-1, keepdims=True) / HEAD)
    return jax.nn.silu(gate) * (yc * lax.rsqrt(var + EPS) * g)


PER_SHARD = MIX_SHARD // SLAB


def _slab_spec(tm, j):
    return pl.BlockSpec((None, tm, SLAB), lambda i: (j, i, 0))


def _proj_slab_spec(tm, j):
    return pl.BlockSpec((None, tm, SLAB), lambda i: (j // PER_SHARD, i, j % PER_SHARD))


def _ghn_fwd(name, y, p32, g, tm):
    s, c = y.shape
    hm = _head_masks()
    return _rw(name, lambda yb, gb, wb, hb: ((_gated_head_norm(yb, gb, wb, hb),), ()), (s // tm,),
               [y, p32, g, hm],
               [_row_spec(tm, c), _proj_slab_spec(tm, 11), _vec_spec(c), pl.BlockSpec((8, c), lambda i: (0, 0))],
               [_sds((s, c), BF16)], [_row_spec(tm, c)])[0]


def _ghn_bwd(name, y, p32, dcat, g, tm):
    s, c = y.shape
    hm = _head_masks()

    def body(yb, gb, dob, wb, hb):
        _, vjp = jax.vjp(lambda a, b_, c_: _gated_head_norm(a, b_, c_, hb), yb, gb, wb)
        dy, dgate, dw = vjp(dob)
        return (dy, dgate), (dw,)

    return _rw(name, body, (s // tm,), [y, p32, dcat, g, hm],
               [_row_spec(tm, c), _proj_slab_spec(tm, 11), _slab_spec(tm, 3), _vec_spec(c),
                pl.BlockSpec((8, c), lambda i: (0, 0))],
               [_sds((s, c), F32)] * 2, [_row_spec(tm, c)] * 2,
               [_sds((1, c), F32)], [_vec_spec(c)])


def _assemble(name, parts, per, tm):
    s = parts[0][0].shape[-2]
    specs = [_row_spec(tm, SLAB) if j is None else pl.BlockSpec((None, tm, SLAB), lambda i, j=j: (j, i, 0))
             for _, j in parts]

    def body(*blocks):
        rows = [jnp.concatenate([b.astype(BF16) for b in blocks[per * q:per * (q + 1)]], axis=-1)
                for q in range(len(blocks) // per)]
        return (jnp.stack(rows),), ()

    nq = len(parts) // per
    return _rw(name, body, (s // tm,), [a for a, _ in parts], specs, [_sds((nq, s, per * SLAB), BF16)],
               [pl.BlockSpec((nq, tm, per * SLAB), lambda i: (0, i, 0))])[0]


def _final(name, x, tgt, g, tm):
    s, d = x.shape

    def body(xb, tb, gb):
        yf, vjp = jax.vjp(_rms, xb, gb)
        err = yf - tb
        dx, dg = vjp(err * (1.0 / d))
        part = 0.5 * jnp.sum(jnp.mean(err * err, axis=-1, keepdims=True), axis=0, keepdims=True)
        return (dx,), (dg, jnp.broadcast_to(part, (1, 128)))

    return _rw(name, body, (s // tm,), [x, tgt, g],
               [_row_spec(tm, d), _row_spec(tm, d), _vec_spec(d)],
               [_sds((s, d), F32)], [_row_spec(tm, d)],
               [_sds((1, d), F32), _sds((1, 128), F32)], [_vec_spec(d), _vec_spec(128)])


NN = (((1,), (0,)), ((), ()))
NT = (((1,), (1,)), ((), ()))
TN = (((0,), (0,)), ((), ()))


def _mm(name, a, b, grid, a_spec, b_spec, outs, out_specs, acc_shape, dims, alpha=1.0):
    nk = grid[-1]
    n_out = len(outs)

    def kern(*refs):
        a_ref, b_ref = refs[0], refs[1]
        o_refs = refs[2:2 + n_out]
        part = lax.dot_general(a_ref[...].astype(BF16), b_ref[...].astype(BF16), dims,
                               preferred_element_type=F32)

        def finish(r):
            if alpha != 1.0:
                r = r * alpha
            for o in o_refs:
                o[...] = r.astype(o.dtype)

        if nk == 1:
            finish(part)
            return
        acc_ref = refs[-1]
        k = pl.program_id(len(grid) - 1)

        @pl.when(k == 0)
        def _():
            acc_ref[...] = part

        @pl.when(jnp.logical_and(k > 0, k < nk - 1))
        def _():
            acc_ref[...] += part

        @pl.when(k == nk - 1)
        def _():
            finish(acc_ref[...] + part)

    return pl.pallas_call(
        kern, name=name, grid=grid, in_specs=[a_spec, b_spec], out_specs=list(out_specs), out_shape=list(outs),
        scratch_shapes=[pltpu.VMEM(acc_shape, F32)] if nk > 1 else [],
        compiler_params=_params(len(grid)))(a, b)


def _norm_proj_in(name, x, g, w, l, tm, dtypes):
    s, d = x.shape
    n = w.shape[-1]
    n_out = len(dtypes)

    def kern(x_ref, g_ref, w_ref, h_ref, *rest):
        o_refs, h_vmem = rest[:n_out], rest[n_out]

        @pl.when(pl.program_id(1) == 0)
        def _():
            h = _rms(x_ref[...], g_ref[...]).astype(BF16)
            h_vmem[...] = h
            h_ref[...] = h

        r = jnp.dot(h_vmem[...], w_ref[...], preferred_element_type=F32)
        for o in o_refs:
            o[...] = r.astype(o.dtype)

    out = pl.pallas_call(
        kern, name=name, grid=(s // tm, N_SHARD),
        in_specs=[pl.BlockSpec((tm, d), lambda i, b: (i, 0)), pl.BlockSpec((1, d), lambda i, b: (0, 0)),
                  pl.BlockSpec((None, None, d, n), lambda i, b: (l, b, 0, 0))],
        out_specs=[pl.BlockSpec((tm, d), lambda i, b: (i, 0))] +
                  [pl.BlockSpec((None, tm, n), lambda i, b: (b, i, 0))] * n_out,
        out_shape=[_sds((s, d), BF16)] + [_sds((N_SHARD, s, n), t) for t in dtypes],
        scratch_shapes=[pltpu.VMEM((tm, d), BF16)], compiler_params=_params(2))(x, g, w)
    return out[0], out[1:]


def _back_in_norm(name, du, w, l, tm, x, dres, g):
    nk, s, n = du.shape
    d = w.shape[2]

    def kern(du_ref, w_ref, x_ref, dres_ref, g_ref, dx_ref, dg_ref):
        dh = lax.dot_general(du_ref[0], w_ref[0], NT, preferred_element_type=F32)
        for k in range(1, nk):
            dh = dh + lax.dot_general(du_ref[k], w_ref[k], NT, preferred_element_type=F32)
        _, vjp = jax.vjp(_rms, x_ref[...], g_ref[...])
        dx, dg = vjp(dh)
        dx_ref[...] = dx + dres_ref[...]

        @pl.when(pl.program_id(0) == 0)
        def _():
            dg_ref[...] = dg

        @pl.when(pl.program_id(0) > 0)
        def _():
            dg_ref[...] += dg

    row = pl.BlockSpec((tm, d), lambda i: (i, 0))
    vec = pl.BlockSpec((1, d), lambda i: (0, 0))
    return pl.pallas_call(
        kern, name=name, grid=(s // tm,),
        in_specs=[pl.BlockSpec((nk, tm, n), lambda i: (0, i, 0)),
                  pl.BlockSpec((None, nk, d, n), lambda i: (l, 0, 0, 0)), row, row, vec],
        out_specs=[row, vec], out_shape=[_sds((s, d), F32), _sds((1, d), F32)],
        compiler_params=_params(1))(du, w, x, dres, g)


def _proj_out(name, a, w, l, res, alpha, tm):
    nk, s, r = a.shape
    d = w.shape[-1]

    def kern(a_ref, w_ref, res_ref, o_ref):
        y = jnp.dot(a_ref[0], w_ref[0], preferred_element_type=F32)
        for k in range(1, nk):
            y = y + jnp.dot(a_ref[k], w_ref[k], preferred_element_type=F32)
        o_ref[...] = res_ref[...] + (y * alpha if alpha != 1.0 else y)

    row = pl.BlockSpec((tm, d), lambda i: (i, 0))
    return pl.pallas_call(
        kern, name=name, grid=(s // tm,),
        in_specs=[pl.BlockSpec((nk, tm, r), lambda i: (0, i, 0)),
                  pl.BlockSpec((None, nk, r, d), lambda i: (l, 0, 0, 0)), row],
        out_specs=row, out_shape=_sds((s, d), F32), compiler_params=_params(1))(a, w, res)


def _back_out(name, dy, w, l, alpha, tm, out_dtype):
    s, d = dy.shape
    nk, r = w.shape[1], w.shape[2]
    return _mm(name, dy, w, (nk, s // tm, 1),
               pl.BlockSpec((tm, d), lambda b, i, k: (i, 0)),
               pl.BlockSpec((None, None, r, d), lambda b, i, k: (l, b, 0, 0)),
               [_sds((nk, s, r), out_dtype)], [pl.BlockSpec((None, tm, r), lambda b, i, k: (b, i, 0))],
               (tm, r), NT, alpha=alpha)[0]


def _grad_in(name, h, du, ts):
    s, d = h.shape
    nb, _, n = du.shape
    return _mm(name, h, du, (nb, 1, s // ts),
               pl.BlockSpec((ts, d), lambda b, i, k: (k, 0)),
               pl.BlockSpec((None, ts, n), lambda b, i, k: (b, k, 0)),
               [_sds((nb, d, n), BF16)], [pl.BlockSpec((None, d, n), lambda b, i, k: (b, 0, 0))],
               (d, n), TN)[0]


def _grad_out(name, a, dy, alpha, ts):
    nb, s, r = a.shape
    d = dy.shape[1]
    return _mm(name, a, dy, (nb, 1, s // ts),
               pl.BlockSpec((None, ts, r), lambda b, i, k: (b, k, 0)),
               pl.BlockSpec((ts, d), lambda b, i, k: (k, 0)),
               [_sds((nb, r, d), BF16)], [pl.BlockSpec((None, r, d), lambda b, i, k: (b, 0, 0))],
               (r, d), TN, alpha=alpha)[0]


CONV_TILE = 256


def _shifted(win, off, rows):
    n = win.shape[0]
    return pltpu.roll(win, (n - off) % n, 0)[0:rows] if off % n else win[0:rows]


def _conv_fwd(name, p32, w, bias):
    s = p32.shape[1]
    cb = 128
    nt = s // CONV_TILE

    def kern(a_ref, b_ref, w_ref, bias_ref, y_ref, vpad):
        vpad[0:CONV_PAD, :] = jnp.zeros((CONV_PAD, cb), F32)

        def fill(i, c):
            r = pl.multiple_of(i * CONV_TILE, CONV_TILE)
            vpad[pl.ds(CONV_PAD + r, CONV_TILE), :] = (
                a_ref[pl.ds(r, CONV_TILE), :] * jax.nn.sigmoid(b_ref[pl.ds(r, CONV_TILE), :]))
            return c

        lax.fori_loop(0, nt, fill, 0)

        def tile(i, c):
            r = pl.multiple_of(i * CONV_TILE, CONV_TILE)
            win = vpad[pl.ds(r, CONV_TILE + CONV_PAD), :]
            acc = jnp.broadcast_to(bias_ref[...], (CONV_TILE, cb))
            for j in range(CONV_W):
                acc = acc + w_ref[j:j + 1, :] * _shifted(win, j + 2, CONV_TILE)
            y_ref[pl.ds(r, CONV_TILE), :] = acc
            return c

        lax.fori_loop(0, nt, tile, 0)

    return pl.pallas_call(
        kern, name=name, grid=(SLAB // cb,),
        in_specs=[pl.BlockSpec((None, s, cb), lambda c: (0, 0, c)),
                  pl.BlockSpec((None, s, cb), lambda c: (0, 0, SLAB // cb + c)),
                  pl.BlockSpec((CONV_W, cb), lambda c: (0, c)),
                  pl.BlockSpec((1, cb), lambda c: (0, c))],
        out_specs=pl.BlockSpec((s, cb), lambda c: (0, c)),
        out_shape=_sds((s, SLAB), F32),
        scratch_shapes=[pltpu.VMEM((s + CONV_PAD, cb), F32)],
        compiler_params=_params(1))(p32, p32, w, bias)


def _conv_bwd(name, p32, w, dy):
    s = p32.shape[1]
    cb = 128
    nt = s // CONV_TILE

    def kern(a_ref, b_ref, w_ref, dy_ref, da_ref, db_ref, dw_ref, dbias_ref, vpad, dpad):
        vpad[0:CONV_PAD, :] = jnp.zeros((CONV_PAD, cb), F32)
        dpad[s:s + CONV_PAD, :] = jnp.zeros((CONV_PAD, cb), F32)
        dw_ref[...] = jnp.zeros((CONV_PAD, cb), F32)
        dbias_ref[...] = jnp.zeros((1, cb), F32)

        def fill(i, c):
            r = pl.multiple_of(i * CONV_TILE, CONV_TILE)
            vpad[pl.ds(CONV_PAD + r, CONV_TILE), :] = (
                a_ref[pl.ds(r, CONV_TILE), :] * jax.nn.sigmoid(b_ref[pl.ds(r, CONV_TILE), :]))
            dpad[pl.ds(r, CONV_TILE), :] = dy_ref[pl.ds(r, CONV_TILE), :]
            return c

        lax.fori_loop(0, nt, fill, 0)

        def tile(i, c):
            r = pl.multiple_of(i * CONV_TILE, CONV_TILE)
            dwin = dpad[pl.ds(r, CONV_TILE + CONV_PAD), :]
            vwin = vpad[pl.ds(r, CONV_TILE + CONV_PAD), :]
            dyt = dwin[0:CONV_TILE]
            dv = jnp.zeros((CONV_TILE, cb), F32)
            for j in range(CONV_W):
                dv = dv + w_ref[j:j + 1, :] * _shifted(dwin, CONV_W - 1 - j, CONV_TILE)
                dw_ref[j:j + 1, :] += jnp.sum(dyt * _shifted(vwin, j + 2, CONV_TILE), axis=0, keepdims=True)
            dbias_ref[...] += jnp.sum(dyt, axis=0, keepdims=True)
            a = a_ref[pl.ds(r, CONV_TILE), :]
            sg = jax.nn.sigmoid(b_ref[pl.ds(r, CONV_TILE), :])
            da_ref[pl.ds(r, CONV_TILE), :] = dv * sg
            db_ref[pl.ds(r, CONV_TILE), :] = dv * a * sg * (1.0 - sg)
            return c

        lax.fori_loop(0, nt, tile, 0)

    col = pl.BlockSpec((s, cb), lambda c: (0, c))
    return pl.pallas_call(
        kern, name=name, grid=(SLAB // cb,),
        in_specs=[pl.BlockSpec((None, s, cb), lambda c: (0, 0, c)),
                  pl.BlockSpec((None, s, cb), lambda c: (0, 0, SLAB // cb + c)),
                  pl.BlockSpec((CONV_W, cb), lambda c: (0, c)), col],
        out_specs=[col, col, pl.BlockSpec((CONV_PAD, cb), lambda c: (0, c)), pl.BlockSpec((1, cb), lambda c: (0, c))],
        out_shape=[_sds((s, SLAB), F32), _sds((s, SLAB), F32), _sds((CONV_PAD, SLAB), F32), _sds((1, SLAB), F32)],
        scratch_shapes=[pltpu.VMEM((s + CONV_PAD, cb), F32), pltpu.VMEM((s + CONV_PAD, cb), F32)],
        compiler_params=_params(1))(p32, p32, w, dy)


SB_BLOCK = 256
N_HEAD = SLAB // HEAD


def _sb_logits(qm, k, tri):
    z = lax.dot_general(qm, k, NT, preferred_element_type=F32)
    sign_bit = jnp.uint32(0x80000000)
    neg_abs = lax.bitcast_convert_type(lax.bitcast_convert_type(z, jnp.uint32) | sign_bit, F32)
    lb = jnp.minimum(z, 0.0) - jnp.log(1.0 + jnp.exp(neg_abs))
    ln = lb - z
    if tri is not None:
        ln = jnp.where(tri, ln, 0.0)
    return lb, ln


def _first_col(x):
    return jnp.broadcast_to(x[:, 0:1], (x.shape[0], 128))


def _head_stack(dst, x, lane_head, bq):
    for h in range(N_HEAD):
        dst[h * bq:(h + 1) * bq, :] = jnp.where(lane_head == h, x, jnp.zeros_like(x))


def _sb_fwd(name, p16):
    s = p16.shape[1]
    bq = min(SB_BLOCK, s)
    nq = s // bq

    def kern(q_ref, k_ref, v_ref, o_ref, w_hbm, lb_hbm, qm_ref, v4_refs, w4_refs, ws_ref, lbs_ref, acc_ref, r_ref,
             sem):
        g, qi = pl.program_id(0), pl.program_id(1)
        lane_head = lax.broadcasted_iota(jnp.int32, (1, SLAB), 1) // HEAD
        _head_stack(qm_ref, (q_ref[...].astype(F32) * (HEAD ** -0.5)).astype(BF16), lane_head, bq)
        row = lax.broadcasted_iota(jnp.int32, (bq, bq), 0)
        col = lax.broadcasted_iota(jnp.int32, (bq, bq), 1)
        after = (row > col).astype(BF16)
        tri = col < row
        acc_ref[...] = jnp.zeros((bq, SLAB), F32)
        r_ref[...] = jnp.zeros((N_HEAD, bq, 128), F32)

        def saves(slot, kb):
            return (pltpu.make_async_copy(ws_ref.at[slot], w_hbm.at[g, qi, kb], sem.at[0, slot]),
                    pltpu.make_async_copy(lbs_ref.at[slot], lb_hbm.at[g, qi, kb], sem.at[1, slot]))

        def tile(i, masked, u):
            kb, slot = qi - i, i % 4
            v4_ref, w4_ref = v4_refs.at[u], w4_refs.at[u]
            rows = pl.ds(pl.multiple_of(kb * bq, bq), bq)
            k = k_ref[rows, :]
            _head_stack(v4_ref, v_ref[rows, :], lane_head, bq)
            for h in range(N_HEAD):
                mine = pl.ds(h * bq, bq)
                lb, ln = _sb_logits(qm_ref[h * bq:(h + 1) * bq, :], k, tri if masked else None)
                rem = jnp.dot(ln.astype(BF16), after, preferred_element_type=F32)
                w = jnp.exp(lb + rem + r_ref[h][:, 0:1])
                if masked:
                    w = jnp.where(tri, w, 0.0)
                wb = w.astype(BF16)
                w4_ref[:, h * bq:(h + 1) * bq] = wb
                ws_ref[slot, mine, :] = wb
                lbs_ref[slot, mine, :] = lb.astype(BF16)
                r_ref[h] += _first_col(rem[:, 0:128] + ln[:, 0:128])
            acc_ref[...] += jnp.dot(w4_ref[...], v4_ref[...], preferred_element_type=F32)

        def save(i, start):
            for cp in saves(i % 4, qi - i):
                cp.start() if start else cp.wait()

        @pl.when(qi == 0)
        def _():
            tile(0, True, 0)
            save(0, True)
            save(0, False)

        @pl.when(qi >= 1)
        def _():
            tile(0, True, 0)
            tile(1, False, 1)
            save(0, True)
            save(1, True)

        def pair(j, c):
            tile(2 * j, False, 0)
            tile(2 * j + 1, False, 1)
            save(2 * j - 2, False)
            save(2 * j - 1, False)
            save(2 * j, True)
            save(2 * j + 1, True)
            return c

        n_pair = (qi + 1) // 2
        lax.fori_loop(1, n_pair, pair, 0)

        @pl.when(jnp.logical_and(qi >= 1, qi % 2 == 1))
        def _():
            save(qi - 1, False)
            save(qi, False)

        @pl.when(jnp.logical_and(qi >= 2, qi % 2 == 0))
        def _():
            tile(qi, False, 0)
            save(qi - 2, False)
            save(qi - 1, False)
            save(qi, True)
            save(qi, False)

        o_ref[...] = acc_ref[...]

    saved = _sds((2, nq, nq, N_HEAD * bq, bq), BF16)
    return pl.pallas_call(
        kern, name=name, grid=(2, nq),
        in_specs=[pl.BlockSpec((None, bq, SLAB), lambda g, i: ((2 + g) // PER_SHARD, i, (2 + g) % PER_SHARD)),
                  pl.BlockSpec((None, s, SLAB), lambda g, i: ((4 + g) // PER_SHARD, 0, (4 + g) % PER_SHARD)),
                  pl.BlockSpec((None, s, SLAB), lambda g, i: ((6 + g) // PER_SHARD, 0, (6 + g) % PER_SHARD))],
        out_specs=[pl.BlockSpec((None, bq, SLAB), lambda g, i: (g, i, 0)),
                   pl.BlockSpec(memory_space=pl.ANY), pl.BlockSpec(memory_space=pl.ANY)],
        out_shape=[_sds((2, s, SLAB), F32), saved, saved],
        scratch_shapes=[pltpu.VMEM((N_HEAD * bq, SLAB), BF16), pltpu.VMEM((2, N_HEAD * bq, SLAB), BF16),
                        pltpu.VMEM((2, bq, N_HEAD * bq), BF16), pltpu.VMEM((4, N_HEAD * bq, bq), BF16),
                        pltpu.VMEM((4, N_HEAD * bq, bq), BF16), pltpu.VMEM((bq, SLAB), F32),
                        pltpu.VMEM((N_HEAD, bq, 128), F32), pltpu.SemaphoreType.DMA((2, 4))],
        compiler_params=_params(2))(p16, p16, p16)


def _sb_bwd(name, p16, w_saved, lb_saved, dcat):
    s = p16.shape[1]
    bq = min(SB_BLOCK, s)
    nq = s // bq

    def kern(q_ref, k_ref, v_ref, do_ref, w_hbm, lb_hbm, dq_ref, dk_hbm, dv_hbm, dk_acc, dv_acc, dq_acc,
             qm_ref, dom_ref, k4_refs, dzc_refs, dzs_refs, ws_ref, lbs_ref, c_ref, sem, lsem):
        g, qi = pl.program_id(0), pl.program_id(1)

        @pl.when(qi == 0)
        def _():
            dk_acc[...] = jnp.zeros((s, SLAB), F32)
            dv_acc[...] = jnp.zeros((s, SLAB), F32)

        lane_head = lax.broadcasted_iota(jnp.int32, (1, SLAB), 1) // HEAD
        _head_stack(qm_ref, (q_ref[...].astype(F32) * (HEAD ** -0.5)).astype(BF16), lane_head, bq)
        _head_stack(dom_ref, do_ref[...].astype(BF16), lane_head, bq)
        row = lax.broadcasted_iota(jnp.int32, (bq, bq), 0)
        col = lax.broadcasted_iota(jnp.int32, (bq, bq), 1)
        earlier = (row < col).astype(BF16)
        tri = col < row
        dq_acc[...] = jnp.zeros((bq, SLAB), F32)
        c_ref[...] = jnp.zeros((N_HEAD, bq, 128), F32)

        def loads(kb):
            slot, kb = kb % 4, jnp.minimum(kb, qi)
            return (pltpu.make_async_copy(w_hbm.at[g, qi, kb], ws_ref.at[slot], lsem.at[0, slot]),
                    pltpu.make_async_copy(lb_hbm.at[g, qi, kb], lbs_ref.at[slot], lsem.at[1, slot]))

        def tile(kb, masked, u):
            slot = kb % 4
            k4_ref, dzc_ref, dzs_ref = k4_refs.at[u], dzc_refs.at[u], dzs_refs.at[u]
            rows = pl.ds(pl.multiple_of(kb * bq, bq), bq)
            k, v = k_ref[rows, :], v_ref[rows, :]
            _head_stack(k4_ref, k, lane_head, bq)
            for h in range(N_HEAD):
                mine = slice(h * bq, (h + 1) * bq)
                wb = ws_ref[slot, pl.ds(h * bq, bq), :]
                dl = wb.astype(F32) * lax.dot_general(dom_ref[mine, :], v, NT, preferred_element_type=F32)
                prefix = jnp.dot(dl.astype(BF16), earlier, preferred_element_type=F32)
                before = prefix + c_ref[h][:, 0:1]
                sig = jnp.exp(lbs_ref[slot, pl.ds(h * bq, bq), :].astype(F32))
                dz = dl - sig * (dl + before)
                if masked:
                    dz = jnp.where(tri, dz, 0.0)
                dzb = dz.astype(BF16)
                dzc_ref[:, mine] = dzb
                dzs_ref[mine, :] = dzb
                tail = prefix[:, bq - 128:] + dl[:, bq - 128:]
                c_ref[h] += jnp.broadcast_to(tail[:, 127:128], (bq, 128))
            dq_acc[...] += jnp.dot(dzc_ref[...], k4_ref[...], preferred_element_type=F32)
            dk_acc[rows, :] += lax.dot_general(dzs_ref[...], qm_ref[...], TN, preferred_element_type=F32)
            dv_acc[rows, :] += lax.dot_general(ws_ref[slot], dom_ref[...], TN, preferred_element_type=F32)

        for kb in (0, 1):
            for cp in loads(kb):
                cp.start()

        def pair(j, c):
            for kb in (2 * j + 2, 2 * j + 3):
                for cp in loads(kb):
                    cp.start()
            for kb in (2 * j, 2 * j + 1):
                for cp in loads(kb):
                    cp.wait()
            tile(2 * j, False, 0)
            tile(2 * j + 1, False, 1)
            return c

        lax.fori_loop(0, qi // 2, pair, 0)
        for kb in (qi - qi % 2, qi - qi % 2 + 1):
            for cp in loads(kb):
                cp.wait()

        @pl.when(qi % 2 == 1)
        def _():
            tile(qi - 1, False, 0)
            tile(qi, True, 1)

        @pl.when(qi % 2 == 0)
        def _():
            tile(qi, True, 0)

        dq_ref[...] = dq_acc[...] * (HEAD ** -0.5)

        @pl.when(qi == nq - 1)
        def _():
            ck = pltpu.make_async_copy(dk_acc, dk_hbm.at[g], sem.at[0])
            cv = pltpu.make_async_copy(dv_acc, dv_hbm.at[g], sem.at[1])
            ck.start()
            cv.start()
            ck.wait()
            cv.wait()

    blk = lambda j0: pl.BlockSpec((None, bq, SLAB), lambda g, i: (j0 + g, i, 0))
    full = lambda j0: pl.BlockSpec((None, s, SLAB), lambda g, i: ((j0 + g) // PER_SHARD, 0, (j0 + g) % PER_SHARD))
    q_blk = pl.BlockSpec((None, bq, SLAB), lambda g, i: ((2 + g) // PER_SHARD, i, (2 + g) % PER_SHARD))
    stack16 = pltpu.VMEM((N_HEAD * bq, SLAB), BF16)
    return pl.pallas_call(
        kern, name=name, grid=(2, nq),
        in_specs=[q_blk, full(4), full(6), blk(1),
                  pl.BlockSpec(memory_space=pl.ANY), pl.BlockSpec(memory_space=pl.ANY)],
        out_specs=[blk(0), pl.BlockSpec(memory_space=pl.ANY), pl.BlockSpec(memory_space=pl.ANY)],
        out_shape=[_sds((2, s, SLAB), F32)] * 3,
        scratch_shapes=[pltpu.VMEM((s, SLAB), F32), pltpu.VMEM((s, SLAB), F32), pltpu.VMEM((bq, SLAB), F32),
                        stack16, stack16, pltpu.VMEM((2, N_HEAD * bq, SLAB), BF16),
                        pltpu.VMEM((2, bq, N_HEAD * bq), BF16), pltpu.VMEM((2, N_HEAD * bq, bq), BF16),
                        pltpu.VMEM((4, N_HEAD * bq, bq), BF16), pltpu.VMEM((4, N_HEAD * bq, bq), BF16),
                        pltpu.VMEM((N_HEAD, bq, 128), F32),
                        pltpu.SemaphoreType.DMA((2,)), pltpu.SemaphoreType.DMA((2, 4))],
        compiler_params=_params(2))(p16, p16, p16, dcat, w_saved, lb_saved)


RET_BLOCK = 256


def _ret_tables(s, bl):
    nh = SLAB // HEAD
    lane_h = np.arange(SLAB) // HEAD
    log_gamma = np.log1p(-np.exp2(-5.0 - np.arange(nh, dtype=np.float64)))
    lg_lane = log_gamma[lane_h]
    half = HEAD // 2
    inv = 1.0 / (ROPE_BASE ** (np.arange(half, dtype=np.float64) / half))
    ang = np.arange(s, dtype=np.float64)[:, None] * inv[None, :]
    within = np.arange(SLAB) % HEAD
    cos = np.cos(ang)[:, within % half]
    sin = np.sin(ang)[:, within % half] * np.where(within < half, -1.0, 1.0)[None, :]
    perm = np.zeros((SLAB, SLAB))
    partner = np.where(within < half, np.arange(SLAB) + half, np.arange(SLAB) - half)
    perm[partner, np.arange(SLAB)] = 1.0
    i = np.arange(bl)
    diff = i[:, None] - i[None, :]
    same = (i[:, None] // CHUNK) == (i[None, :] // CHUNK)
    earlier = (i[None, :] // CHUNK) < (i[:, None] // CHUNK)
    decay = np.zeros((nh, bl, bl))
    for h in range(nh):
        decay[h] = np.where(same, np.exp(log_gamma[h] * np.abs(diff)),
                            np.where(earlier, np.exp(log_gamma[h] * diff), 0.0))
    qd = np.exp(lg_lane[None, :] * (i[:, None] + 1.0))
    kd = np.exp(lg_lane[None, :] * (bl - 1.0 - i[:, None]))
    gam = np.exp(lg_lane * bl)[:, None] * np.ones((1, SLAB))
    bd = (lane_h[:, None] == lane_h[None, :]).astype(np.float64)
    f = lambda a: jnp.asarray(a, F32)
    return f(cos), f(sin), f(perm), f(decay), f(qd), f(kd), f(gam), f(bd)


def _ret_block(q, k, v, state, cos, sin, perm, decay, qd, kd, gam, bd, hm):
    qr = (q * cos + jnp.dot(q, perm, preferred_element_type=F32) * sin) * (HEAD ** -0.5)
    kr = k * cos + jnp.dot(k, perm, preferred_element_type=F32) * sin
    y = jnp.dot(qr * qd, state, preferred_element_type=F32)
    for h in range(SLAB // HEAD):
        m = hm[h:h + 1]
        sc = lax.dot_general(qr * m, kr, NT, preferred_element_type=F32) * decay[h]
        y = y + jnp.dot(sc, v * m, preferred_element_type=F32)
    new_state = gam * state + lax.dot_general(kr * kd, v, TN, preferred_element_type=F32) * bd
    return y, new_state


def _ret_specs(s, bl, rev):
    nb = s // bl
    pos = (lambda n: nb - 1 - n) if rev else (lambda n: n)
    slab = lambda j: pl.BlockSpec((None, bl, SLAB), lambda n: (j // PER_SHARD, pos(n), j % PER_SHARD))
    const2 = lambda r: pl.BlockSpec((r, SLAB), lambda n: (0, 0))
    tab = [pl.BlockSpec((bl, SLAB), lambda n: (pos(n), 0))] * 2 + [
        const2(SLAB), pl.BlockSpec((SLAB // HEAD, bl, bl), lambda n: (0, 0, 0)),
        const2(bl), const2(bl), const2(SLAB), const2(SLAB), const2(8)]
    return nb, pos, slab, tab


def _ret_fwd(name, p32):
    s = p32.shape[1]
    bl = min(RET_BLOCK, s)
    nb, pos, slab, tab = _ret_specs(s, bl, False)
    tables = _ret_tables(s, bl) + (_head_masks(),)

    def kern(q_ref, k_ref, v_ref, *rest):
        t_refs, (y_ref, st_ref, state) = rest[:9], rest[9:]

        @pl.when(pl.program_id(0) == 0)
        def _():
            state[...] = jnp.zeros((SLAB, SLAB), F32)

        st_ref[...] = state[...]
        y, new = _ret_block(q_ref[...], k_ref[...], v_ref[...], state[...], *[t[...] for t in t_refs])
        y_ref[...] = y
        state[...] = new

    return pl.pallas_call(
        kern, name=name, grid=(nb,), in_specs=[slab(8), slab(9), slab(10)] + tab,
        out_specs=[pl.BlockSpec((bl, SLAB), lambda n: (n, 0)), pl.BlockSpec((None, SLAB, SLAB), lambda n: (n, 0, 0))],
        out_shape=[_sds((s, SLAB), F32), _sds((nb, SLAB, SLAB), F32)],
        scratch_shapes=[pltpu.VMEM((SLAB, SLAB), F32)], compiler_params=_params(1))(p32, p32, p32, *tables)


def _ret_bwd(name, p32, states, dy):
    s = p32.shape[1]
    bl = min(RET_BLOCK, s)
    nb, pos, slab, tab = _ret_specs(s, bl, True)
    tables = _ret_tables(s, bl) + (_head_masks(),)
    rowblk = pl.BlockSpec((bl, SLAB), lambda n: (pos(n), 0))

    def kern(q_ref, k_ref, v_ref, st_ref, dy_ref, *rest):
        t_refs, (dq_ref, dk_ref, dv_ref, dstate) = rest[:9], rest[9:]

        @pl.when(pl.program_id(0) == 0)
        def _():
            dstate[...] = jnp.zeros((SLAB, SLAB), F32)

        tv = [t[...] for t in t_refs]
        _, vjp = jax.vjp(lambda a, b, c, d: _ret_block(a, b, c, d, *tv),
                         q_ref[...], k_ref[...], v_ref[...], st_ref[...])
        dq, dk, dv, ds = vjp((dy_ref[...], dstate[...]))
        dq_ref[...] = dq
        dk_ref[...] = dk
        dv_ref[...] = dv
        dstate[...] = ds

    return pl.pallas_call(
        kern, name=name, grid=(nb,),
        in_specs=[slab(8), slab(9), slab(10), pl.BlockSpec((None, SLAB, SLAB), lambda n: (pos(n), 0, 0)), rowblk] + tab,
        out_specs=[rowblk] * 3, out_shape=[_sds((s, SLAB), F32)] * 3,
        scratch_shapes=[pltpu.VMEM((SLAB, SLAB), F32)], compiler_params=_params(1))(p32, p32, p32, states, dy, *tables)


TM_FFN = 1024
TM_SLAB = 2048
TM_NORM = 256
TM_OUT = 512
TM_GRAD = 2048
TM_RW = 512
TM_FF = 256


def _ffn_fwd(tag, x, g, w_in, w_out, l):
    tm = min(TM_FFN, x.shape[0])
    h, (u,) = _norm_proj_in(tag + "_in", x, g, w_in, l, tm, [BF16])
    a = _swiglu_fwd(tag + "_act", u, TM_FF)
    w_out2 = w_out.reshape(w_out.shape[0], 2, FF_SHARD, D_MODEL)
    xn = _proj_out(tag + "_out", a, w_out2, l, x, 0.5, min(TM_OUT, x.shape[0]))
    return xn, (x, h, u, a)


def _ffn_bwd(tag, saved, dxn, g, w_in, w_out, l):
    x, h, u, a = saved
    tm = min(TM_FFN, x.shape[0])
    w_out2 = w_out.reshape(w_out.shape[0], 2, FF_SHARD, D_MODEL)
    da = _back_out(tag + "_dact", dxn, w_out2, l, 0.5, tm, BF16)
    tk = min(TM_GRAD, x.shape[0])
    dw_out = _grad_out(tag + "_dwout", a, dxn, 0.5, tk)
    du = _swiglu_bwd(tag + "_dswi", u, da, TM_FF)
    dx, dg = _back_in_norm(tag + "_dh", du, w_in, l, min(TM_NORM, x.shape[0]), x, dxn, g)
    dw_in = _grad_in(tag + "_dwin", h, du, tk)
    return dx, dg, dw_in, dw_out.reshape(N_SHARD, D_FF // N_SHARD, D_MODEL)


def _mix_fwd(tag, x, sm, w_in, w_out, l, wl):
    h, (p32, p16) = _norm_proj_in(tag + "_in", x, sm["mix_norm"][l:l + 1], w_in, wl, min(TM_FFN, x.shape[0]),
                                  [F32, BF16])
    ypre = _conv_fwd(tag + "_conv", p32, sm["conv_w"][l], sm["conv_b"][l:l + 1])
    yconv = _ln_silu_fwd(tag + "_ln", ypre, sm["conv_ln_g"][l:l + 1], sm["conv_ln_b"][l:l + 1], TM_RW)
    osb, w_sb, lb_sb = _sb_fwd(tag + "_sb", p16)
    yr, states = _ret_fwd(tag + "_ret", p32)
    yret = _ghn_fwd(tag + "_ghn", yr, p32, sm["ret_norm_g"][l:l + 1], TM_RW)
    ycat = _assemble(tag + "_cat", [(yconv, None), (osb, 0), (osb, 1), (yret, None)], 1, TM_RW)
    xn = _proj_out(tag + "_out", ycat, w_out, wl, x, 1.0, min(TM_OUT, x.shape[0]))
    return xn, (x, h, p32, p16, ypre, (osb, w_sb, lb_sb), yr, states, ycat)


def _mix_bwd(tag, saved, dxn, sm, w_in, w_out, l, wl):
    x, h, p32, p16, ypre, osb, yr, states, ycat = saved
    ts = min(TM_SLAB, x.shape[0])
    dcat = _back_out(tag + "_dcat", dxn, w_out, wl, 1.0, ts, F32)
    dw_out = _grad_out(tag + "_dwout", ycat, dxn, 1.0, ts)
    dypre, dlg, dlb = _ln_silu_bwd(tag + "_dln", ypre, dcat, sm["conv_ln_g"][l:l + 1], sm["conv_ln_b"][l:l + 1], TM_RW)
    da, db, dcw, dcb = _conv_bwd(tag + "_dconv", p32, sm["conv_w"][l], dypre)
    dq, dk, dv = _sb_bwd(tag + "_dsb", p16, osb[1], osb[2], dcat)
    dyr, dgate, drg = _ghn_bwd(tag + "_dghn", yr, p32, dcat, sm["ret_norm_g"][l:l + 1], TM_RW)
    dqr, dkr, dvr = _ret_bwd(tag + "_dret", p32, states, dyr)
    dp = _assemble(tag + "_dp", [(da, None), (db, None), (dq, 0), (dq, 1), (dk, 0), (dk, 1), (dv, 0), (dv, 1),
                                 (dqr, None), (dkr, None), (dvr, None), (dgate, None)], PER_SHARD, TM_RW)
    dx, dg = _back_in_norm(tag + "_dh", dp, w_in, wl, min(TM_NORM, x.shape[0]), x, dxn, sm["mix_norm"][l:l + 1])
    dw_in = _grad_in(tag + "_dwin", h, dp, min(TM_GRAD, x.shape[0]))
    small = dict(mix_norm=dg, conv_w=dcw[0:CONV_W], conv_b=dcb, conv_ln_g=dlg, conv_ln_b=dlb, ret_norm_g=drg)
    return dx, small, dw_in, dw_out


def _local_step(x, tgt, early, last, sm, on_last_layer=None):
    saved = []
    weights = [(early, l) for l in range(DEPTH - 1)]
    for l in range(DEPTH):
        if l == DEPTH - 1:
            weights.append((last(x), 0))
        wt, wl = weights[l]
        x, s1 = _ffn_fwd(f"l{l}f1", x, sm["ffn1_norm"][l:l + 1], wt["ffn1_w_in"], wt["ffn1_w_out"], wl)
        x, s2 = _mix_fwd(f"l{l}mx", x, sm, wt["mix_w_in"], wt["mix_w_out"], l, wl)
        x, s3 = _ffn_fwd(f"l{l}f2", x, sm["ffn2_norm"][l:l + 1], wt["ffn2_w_in"], wt["ffn2_w_out"], wl)
        saved.append((s1, s2, s3))
    dx, dfinal, loss = _final("final", x, tgt, sm["final_norm"][None, :], TM_RW)
    big = [None] * DEPTH
    small = [None] * DEPTH
    for l in reversed(range(DEPTH)):
        s1, s2, s3 = saved[l]
        wt, wl = weights[l]
        dx, dg3, dwi3, dwo3 = _ffn_bwd(f"l{l}f2", s3, dx, sm["ffn2_norm"][l:l + 1], wt["ffn2_w_in"], wt["ffn2_w_out"], wl)
        dx, sml, dwi2, dwo2 = _mix_bwd(f"l{l}mx", s2, dx, sm, wt["mix_w_in"], wt["mix_w_out"], l, wl)
        dx, dg1, dwi1, dwo1 = _ffn_bwd(f"l{l}f1", s1, dx, sm["ffn1_norm"][l:l + 1], wt["ffn1_w_in"], wt["ffn1_w_out"], wl)
        big[l] = dict(ffn1_w_in=dwi1, ffn1_w_out=dwo1, mix_w_in=dwi2, mix_w_out=dwo2, ffn2_w_in=dwi3, ffn2_w_out=dwo3)
        sml.update(ffn1_norm=dg1, ffn2_norm=dg3)
        small[l] = sml
        if on_last_layer is not None and l == DEPTH - 1:
            dx = dx + on_last_layer(big[l])[0:1, 0:1]
    return loss, dx, big, small, dfinal


MESH = pl.DeviceIdType.MESH
ANY = pl.BlockSpec(memory_space=pl.ANY)
BIG = ("ffn1_w_in", "ffn1_w_out", "mix_w_in", "mix_w_out", "ffn2_w_in", "ffn2_w_out")


def _place():
    x, y, c = lax.axis_index("x"), lax.axis_index("y"), lax.axis_index("c")
    chips = [(1 - x, y), (x, 1 - y), (1 - x, 1 - y)]
    return x, y, c, chips


def _gather_weights(w16):
    n = len(w16)

    def kern(*refs):
        dst = refs[n:2 * n]
        send, recv = refs[2 * n:]
        x, y, c, chips = _place()
        mine = 2 * x + y
        firsts, passes = [], []
        for a in range(n):
            h = dst[a].shape[2] // 2
            own = dst[a].at[:, mine, pl.ds(c * h, h)]
            for j, (cx, cy) in enumerate(chips):
                cp = pltpu.make_async_remote_copy(
                    src_ref=own, dst_ref=own, send_sem=send.at[6 * a + j], recv_sem=recv.at[6 * a + j],
                    device_id=(cx, cy, c), device_id_type=MESH)
                cp.start()
                firsts.append(cp)
        for a in range(n):
            h = dst[a].shape[2] // 2
            half = pl.ds(c * h, h)
            for j, (cx, cy) in enumerate(chips):
                theirs = dst[a].at[:, 2 * cx + cy, half]
                pltpu.make_async_remote_copy(
                    src_ref=theirs, dst_ref=theirs, send_sem=send.at[6 * a + j], recv_sem=recv.at[6 * a + j],
                    device_id=(cx, cy, c), device_id_type=MESH).wait_recv()
                fw = pltpu.make_async_remote_copy(
                    src_ref=theirs, dst_ref=theirs, send_sem=send.at[6 * a + 3 + j], recv_sem=recv.at[6 * a + 3 + j],
                    device_id=(x, y, 1 - c), device_id_type=MESH)
                fw.start()
                passes.append(fw)
        for a in range(n):
            h = dst[a].shape[2] // 2
            other = pl.ds((1 - c) * h, h)
            for j, (cx, cy) in enumerate(chips):
                got = dst[a].at[:, 2 * cx + cy, other]
                pltpu.make_async_remote_copy(
                    src_ref=got, dst_ref=got, send_sem=send.at[6 * a + 3 + j], recv_sem=recv.at[6 * a + 3 + j],
                    device_id=(x, y, 1 - c), device_id_type=MESH).wait_recv()
        for cp in firsts + passes:
            cp.wait_send()

    return pl.pallas_call(
        kern, name="gather_weights", in_specs=[ANY] * n, out_specs=[ANY] * n,
        out_shape=[_sds(w.shape, w.dtype) for w in w16], input_output_aliases={a: a for a in range(n)},
        scratch_shapes=[pltpu.SemaphoreType.DMA((6 * n,)), pltpu.SemaphoreType.DMA((6 * n,))])(*w16)


def _last_layer_copies(bufs, send, recv, x, y, c, chips):
    mine = 2 * x + y
    sends, arrivals = [], []
    for a, buf in enumerate(bufs):
        last = buf.shape[0] - 1
        h = buf.shape[2] // 2
        own = buf.at[last, mine, pl.ds(c * h, h)]
        for j, (cx, cy) in enumerate(chips):
            for k in range(2):
                sem = 6 * a + 2 * j + k
                sends.append(pltpu.make_async_remote_copy(
                    src_ref=own, dst_ref=own, send_sem=send.at[sem], recv_sem=recv.at[sem],
                    device_id=(cx, cy, c ^ k), device_id_type=MESH))
                theirs = buf.at[last, 2 * cx + cy, pl.ds((c ^ k) * h, h)]
                arrivals.append(pltpu.make_async_remote_copy(
                    src_ref=theirs, dst_ref=theirs, send_sem=send.at[sem], recv_sem=recv.at[sem],
                    device_id=(cx, cy, c ^ k), device_id_type=MESH))
    return sends, arrivals


def _gather_last_start(w16, after):
    n = len(w16)

    def kern(*refs):
        bufs, send, recv, token = refs[:n], refs[n + 1], refs[n + 2], refs[-1]
        x, y, c, chips = _place()
        for cp in _last_layer_copies(bufs, send, recv, x, y, c, chips)[0]:
            cp.start()
        token[...] = jnp.zeros_like(token)

    out = pl.pallas_call(
        kern, name="gather_last_start",
        out_shape=(pltpu.SemaphoreType.DMA((6 * n,)), pltpu.SemaphoreType.DMA((6 * n,)),
                   *[pltpu.HBM(w.shape, w.dtype) for w in w16], _sds((8, 128), F32)),
        in_specs=[HBM] * n + [ANY],
        out_specs=(SEM, SEM, *[HBM] * n, pl.BlockSpec(memory_space=pltpu.VMEM)),
        input_output_aliases={i: 2 + i for i in range(n)},
        compiler_params=pltpu.CompilerParams(has_side_effects=EFFECT),
    )(*[pltpu.with_memory_space_constraint(w, pltpu.HBM) for w in w16], after)
    return out[0], out[1], out[2:2 + n], out[-1]


def _gather_last_wait(send, recv, w16, after):
    n = len(w16)

    def kern(*refs):
        bufs, send_sem, recv_sem = refs[:n], refs[n], refs[n + 1]
        x, y, c, chips = _place()
        sends, arrivals = _last_layer_copies(bufs, send_sem, recv_sem, x, y, c, chips)
        for cp in sends:
            cp.wait_send()
        for cp in arrivals:
            cp.wait_recv()

    return pl.pallas_call(
        kern, name="gather_last_wait", out_shape=tuple(pltpu.HBM(w.shape, w.dtype) for w in w16),
        in_specs=[HBM] * n + [SEM, SEM, ANY], out_specs=tuple([HBM] * n),
        input_output_aliases={i: i for i in range(n)},
        compiler_params=pltpu.CompilerParams(has_side_effects=EFFECT),
    )(*w16, send, recv, after)


def _pair_exchange(name, grads):
    n = len(grads)

    def kern(*refs):
        src, got_o = refs[:n], refs[n:2 * n]
        send, recv = refs[2 * n:]
        x, y, c, _ = _place()
        cps = []
        for a in range(n):
            h = src[a].shape[1] // 2
            cp = pltpu.make_async_remote_copy(
                src_ref=src[a].at[:, pl.ds((1 - c) * h, h)], dst_ref=got_o[a],
                send_sem=send.at[a], recv_sem=recv.at[a], device_id=(x, y, 1 - c), device_id_type=MESH)
            cp.start()
            cps.append(cp)
        for cp in cps:
            cp.wait()

    halves = [_sds((g.shape[0], g.shape[1] // 2, g.shape[2]), g.dtype) for g in grads]
    return pl.pallas_call(
        kern, name=name, in_specs=[ANY] * n, out_specs=[ANY] * n, out_shape=halves,
        scratch_shapes=[pltpu.SemaphoreType.DMA((n,)), pltpu.SemaphoreType.DMA((n,))])(*grads)


def _chip_exchange(name, sums):
    n = len(sums)

    def kern(*refs):
        src, dst = refs[:n], refs[n:2 * n]
        send, recv = refs[2 * n:]
        x, y, c, chips = _place()
        cps = _chip_copies(src, dst, send, recv, x, y, c, chips)
        for cp in cps:
            cp.start()
        for cp in cps:
            cp.wait()

    return pl.pallas_call(
        kern, name=name, in_specs=[ANY] * n, out_specs=[ANY] * n,
        out_shape=[_sds((3,) + s_.shape[1:], s_.dtype) for s_ in sums],
        scratch_shapes=[pltpu.SemaphoreType.DMA((3 * n,)), pltpu.SemaphoreType.DMA((3 * n,))])(*sums)


HBM = pl.BlockSpec(memory_space=pltpu.HBM)
SEM = pl.BlockSpec(memory_space=pltpu.SEMAPHORE)
EFFECT = pltpu.SideEffectType.DATAFLOW_SIDE_EFFECTING


def _chip_copies(src, land, send, recv, x, y, c, chips):
    return [pltpu.make_async_remote_copy(
        src_ref=src[a].at[2 * cx + cy], dst_ref=land[a].at[j], send_sem=send.at[3 * a + j],
        recv_sem=recv.at[3 * a + j], device_id=(cx, cy, c), device_id_type=MESH)
        for a in range(len(src)) for j, (cx, cy) in enumerate(chips)]


def _chip_exchange_start(sums):
    n = len(sums)

    def kern(*refs):
        src, land = refs[:n], refs[n:2 * n]
        send, recv, token = refs[2 * n], refs[2 * n + 1], refs[-1]
        x, y, c, chips = _place()
        for cp in _chip_copies(src, land, send, recv, x, y, c, chips):
            cp.start()
        token[...] = jnp.zeros_like(token)

    lands = [(3,) + s_.shape[1:] for s_ in sums]
    out = pl.pallas_call(
        kern, name="chip_exchange_start",
        out_shape=(pltpu.SemaphoreType.DMA((3 * n,)), pltpu.SemaphoreType.DMA((3 * n,)),
                   *[pltpu.HBM(s_.shape, s_.dtype) for s_ in sums],
                   *[pltpu.HBM(shp, s_.dtype) for shp, s_ in zip(lands, sums)], _sds((8, 128), F32)),
        in_specs=[HBM] * (2 * n),
        out_specs=(SEM, SEM, *[HBM] * (2 * n), pl.BlockSpec(memory_space=pltpu.VMEM)),
        input_output_aliases={i: 2 + i for i in range(2 * n)},
        compiler_params=pltpu.CompilerParams(has_side_effects=EFFECT),
    )(*[pltpu.with_memory_space_constraint(s_, pltpu.HBM) for s_ in sums],
      *[pltpu.with_memory_space_constraint(lax.empty(shp, s_.dtype), pltpu.HBM) for shp, s_ in zip(lands, sums)])
    return out[0], out[1], out[2:2 + n], out[2 + n:2 + 2 * n], out[-1]


def _chip_exchange_wait(send, recv, sums, lands, after):
    n = len(sums)

    def kern(*refs):
        src, land = refs[:n], refs[n:2 * n]
        send_sem, recv_sem = refs[2 * n], refs[2 * n + 1]
        x, y, c, chips = _place()
        for cp in _chip_copies(src, land, send_sem, recv_sem, x, y, c, chips):
            cp.wait_send()
            cp.wait_recv()

    out = pl.pallas_call(
        kern, name="chip_exchange_wait",
        out_shape=tuple(pltpu.HBM(t.shape, t.dtype) for t in list(sums) + list(lands)),
        in_specs=[HBM] * (2 * n) + [SEM, SEM, ANY], out_specs=tuple([HBM] * (2 * n)),
        input_output_aliases={i: i for i in range(2 * n)},
        compiler_params=pltpu.CompilerParams(has_side_effects=EFFECT),
    )(*sums, *lands, send, recv, after)
    return out[:n], out[n:]


def _pair_join(full):
    n = len(full)

    def kern(*refs):
        dst = refs[n:2 * n]
        send, recv = refs[2 * n:]
        x, y, c, _ = _place()
        cps = []
        for a in range(n):
            h = dst[a].shape[1] // 2
            mine = dst[a].at[:, pl.ds(c * h, h)]
            cp = pltpu.make_async_remote_copy(
                src_ref=mine, dst_ref=mine, send_sem=send.at[a], recv_sem=recv.at[a],
                device_id=(x, y, 1 - c), device_id_type=MESH)
            cp.start()
            cps.append(cp)
        for a, cp in enumerate(cps):
            cp.wait_send()
            h = dst[a].shape[1] // 2
            got = dst[a].at[:, pl.ds((1 - c) * h, h)]
            pltpu.make_async_remote_copy(
                src_ref=got, dst_ref=got, send_sem=send.at[a], recv_sem=recv.at[a],
                device_id=(x, y, 1 - c), device_id_type=MESH).wait_recv()

    return pl.pallas_call(
        kern, name="pair_join", in_specs=[ANY] * n, out_specs=[ANY] * n,
        out_shape=[_sds(f.shape, f.dtype) for f in full], input_output_aliases={a: a for a in range(n)},
        scratch_shapes=[pltpu.SemaphoreType.DMA((n,)), pltpu.SemaphoreType.DMA((n,))])(*full)


def _all_sum(name, v):
    r = v.shape[0]

    def kern(v_ref, o_ref, buf, send, recv):
        x, y, c, _ = _place()
        me = 4 * x + 2 * y + c
        buf[me] = v_ref[...]
        cps = []
        for k in range(1, 8):
            peer = (x ^ (k >> 2), y ^ ((k >> 1) & 1), c ^ (k & 1))
            cp = pltpu.make_async_remote_copy(
                src_ref=v_ref, dst_ref=buf.at[me], send_sem=send.at[k - 1], recv_sem=recv.at[k - 1],
                device_id=peer, device_id_type=MESH)
            cp.start()
            cps.append(cp)
        for k in range(1, 8):
            peer_id = me ^ k
            pltpu.make_async_remote_copy(
                src_ref=v_ref, dst_ref=buf.at[peer_id], send_sem=send.at[k - 1], recv_sem=recv.at[k - 1],
                device_id=(x, y, c), device_id_type=MESH).wait_recv()
        for cp in cps:
            cp.wait_send()
        acc = buf[0]
        for d in range(1, 8):
            acc = acc + buf[d]
        o_ref[...] = acc

    vm = pl.BlockSpec(memory_space=pltpu.VMEM)
    return pl.pallas_call(
        kern, name=name, in_specs=[vm], out_specs=vm, out_shape=_sds((r, 128), F32),
        scratch_shapes=[pltpu.VMEM((8, r, 128), F32), pltpu.SemaphoreType.DMA((7,)),
                        pltpu.SemaphoreType.DMA((7,))])(v)


def _my_chip():
    return 2 * lax.axis_index("x") + lax.axis_index("y")


def _my_core():
    return lax.axis_index("c")


def _cast_place(name, w, first, count):
    _, r, c = w.shape
    tr = r // 4
    return _rw(name, lambda wb: ((wb,), ()), (count, r // tr), [w],
               [pl.BlockSpec((None, tr, c), lambda j, i: (first + j, i, 0))],
               [_sds((count, N_SHARD, r, c), BF16)],
               [pl.BlockSpec((None, None, tr, c), lambda j, i: (j, _my_chip(), i, 0))])[0]


HALF_STEPS = 2


def _add_halves(name, g, got):
    n, h, c = got.shape
    tr, nt = h // HALF_STEPS, HALF_STEPS
    return _rw(name, lambda ab, bb: ((ab.astype(F32) + bb.astype(F32),), ()), (nt,), [g, got],
               [pl.BlockSpec((n, tr, c), lambda i: (0, _my_core() * nt + i, 0)),
                pl.BlockSpec((n, tr, c), lambda i: (0, i, 0))],
               [_sds((n, h, c), BF16)], [pl.BlockSpec((n, tr, c), lambda i: (0, i, 0))])[0]


def _sum_parts(name, sums, parts, full, layer, n_layer):
    _, h, c = sums.shape
    tr, nt = h // HALF_STEPS, HALF_STEPS

    def body(own, pb):
        acc = own.astype(F32)
        for j in range(pb.shape[0]):
            acc = acc + pb[j].astype(F32)
        return (acc,), ()

    ins = [sums, parts] + ([full] if full is not None else [])
    in_specs = [pl.BlockSpec((None, tr, c), lambda i: (_my_chip(), i, 0)),
                pl.BlockSpec((parts.shape[0], tr, c), lambda i: (0, i, 0))] + ([ANY] if full is not None else [])
    return _rw(name, body, (nt,), ins, in_specs, [_sds((n_layer, 2 * h, c), F32)],
               [pl.BlockSpec((None, tr, c), lambda i: (layer, _my_core() * nt + i, 0))],
               aliases={2: 0} if full is not None else None)[0]


def _adamw_math(w, g, m, v):
    m = B1 * m + (1.0 - B1) * g
    v = B2 * v + (1.0 - B2) * (g * g)
    m_hat = m / (1.0 - B1 ** STEP)
    v_hat = v / (1.0 - B2 ** STEP)
    delta = -LR * (m_hat / (jnp.sqrt(v_hat) + ADAM_EPS) + WD * w)
    return delta, m, v


def _adamw(name, w, g, m, v):
    r, c = w.shape
    tr = 64 if r % 64 == 0 else 8
    spec = _row_spec(tr, c)
    return _rw(name, lambda *b: (_adamw_math(*b), ()), (r // tr,), [w, g, m, v], [spec] * 4,
               [_sds((r, c), F32)] * 3, [spec] * 3)


SMALL = (("ffn1_norm", (DEPTH, D_MODEL)), ("mix_norm", (DEPTH, D_MODEL)), ("ffn2_norm", (DEPTH, D_MODEL)),
         ("conv_b", (DEPTH, SLAB)), ("conv_ln_g", (DEPTH, SLAB)), ("conv_ln_b", (DEPTH, SLAB)),
         ("ret_norm_g", (DEPTH, SLAB)), ("final_norm", (D_MODEL,)), ("conv_w", (DEPTH, CONV_W, SLAB)))


def _pack(parts, rows):
    flat = jnp.concatenate([p.reshape(-1) for p in parts])
    return jnp.pad(flat, (0, rows * 128 - flat.shape[0])).reshape(rows, 128)


def _unpack(packed, shapes):
    flat = packed.reshape(-1)
    out, off = [], 0
    for shp in shapes:
        n = int(np.prod(shp))
        out.append(flat[off:off + n].reshape(shp))
        off += n
    return out


def kernel(x, ffn1_norm, ffn1_w_in, ffn1_w_out, mix_norm, mix_w_in, conv_w, conv_b, conv_ln_g, conv_ln_b, ret_norm_g, mix_w_out, ffn2_norm, ffn2_w_in, ffn2_w_out, final_norm, loss_target, m_ffn1_norm, m_ffn1_w_in, m_ffn1_w_out, m_mix_norm, m_mix_w_in, m_conv_w, m_conv_b, m_conv_ln_g, m_conv_ln_b, m_ret_norm_g, m_mix_w_out, m_ffn2_norm, m_ffn2_w_in, m_ffn2_w_out, m_final_norm, v_ffn1_norm, v_ffn1_w_in, v_ffn1_w_out, v_mix_norm, v_mix_w_in, v_conv_w, v_conv_b, v_conv_ln_g, v_conv_ln_b, v_ret_norm_g, v_mix_w_out, v_ffn2_norm, v_ffn2_w_in, v_ffn2_w_out, v_final_norm):
    given = dict(locals())
    chip = 2 * lax.axis_index("x") + lax.axis_index("y")
    core = lax.axis_index("c")

    cw_rows = 128
    placed = lax.dynamic_update_slice(jnp.zeros((DEPTH, CONV_W, SLAB), F32), conv_w, (0, 0, chip * HEAD))
    placed = placed * (core == 0).astype(F32)
    conv_w_full = _unpack(_all_sum("gather_conv_w", _pack([placed], cw_rows)), [(DEPTH, CONV_W, SLAB)])[0]

    early = dict(zip(BIG, _gather_weights([_cast_place("cast_" + n, given[n], 0, DEPTH - 1) for n in BIG])))
    w_send, w_recv, arriving, started = _gather_last_start(
        [_cast_place("cast_last_" + n, given[n], DEPTH - 1, 1) for n in BIG], early[BIG[0]])
    sm = {n: given[n] for n, _ in SMALL}
    sm["conv_w"] = conv_w_full
    sm["ffn1_norm"] = ffn1_norm + started[0:1, 0:1]

    def last_weights(activations):
        return dict(zip(BIG, _gather_last_wait(w_send, w_recv, arriving, activations)))

    def chip_sums(tag, layer_grads):
        grads = [layer_grads[n] for n in BIG]
        theirs = _pair_exchange("pair_exchange_" + tag, grads)
        return [_add_halves(f"chipsum_{tag}{i}", a, b) for i, (a, b) in enumerate(zip(grads, theirs))]

    in_flight = []

    def start_last(layer_grads):
        in_flight.extend(_chip_exchange_start(chip_sums("last", layer_grads)))
        return in_flight[4]

    loss, dx, big, small, dfinal = _local_step(x[0], loss_target[0], early, last_weights, sm, start_last)
    sums, parts = [None] * DEPTH, [None] * DEPTH
    sums[DEPTH - 1], parts[DEPTH - 1] = _chip_exchange_wait(*in_flight[:4], dx)
    for l in range(DEPTH - 1):
        sums[l] = chip_sums(f"l{l}", big[l])
        parts[l] = _chip_exchange(f"chip_exchange_l{l}", sums[l])
    full = []
    for i in range(len(BIG)):
        f = None
        for l in range(DEPTH):
            f = _sum_parts(f"shardsum{DEPTH * i + l}", sums[l][i], parts[l][i], f, l, DEPTH)
        full.append(f)
    g_big = dict(zip(BIG, _pair_join(full)))

    small_parts = []
    for n, shp in SMALL:
        if n == "final_norm":
            small_parts.append(dfinal)
        else:
            small_parts.append(jnp.stack([small[l][n].reshape(shp[1:]) for l in range(DEPTH)]))
    g_small = dict(zip([n for n, _ in SMALL], _unpack(_all_sum("sum_small", _pack(small_parts, 200)), [s_ for _, s_ in SMALL])))
    g_small["conv_w"] = lax.dynamic_slice(g_small["conv_w"], (0, 0, chip * HEAD), (DEPTH, CONV_W, HEAD))

    grad, delta, new_m, new_v = dict(g_small), {}, {}, {}
    grad.update(g_big)
    for n in BIG:
        l, r, c = given[n].shape
        f = lambda t: t.reshape(l * r, c)
        d_, m_, v_ = _adamw("adamw_" + n, f(given[n]), f(grad[n]), f(given["m_" + n]), f(given["v_" + n]))
        delta[n], new_m[n], new_v[n] = d_.reshape(l, r, c), m_.reshape(l, r, c), v_.reshape(l, r, c)
    snames = [n for n, _ in SMALL]
    shapes = [given[n].shape for n in snames]
    rows = 104
    d_, m_, v_ = _adamw("adamw_small", _pack([given[n] for n in snames], rows), _pack([grad[n] for n in snames], rows),
                        _pack([given["m_" + n] for n in snames], rows), _pack([given["v_" + n] for n in snames], rows))
    for dst, packed in ((delta, d_), (new_m, m_), (new_v, v_)):
        dst.update(zip(snames, _unpack(packed, shapes)))

    total = lax.psum(loss[0, 0], ("x", "y", "c"))
    order = ["ffn1_norm", "ffn1_w_in", "ffn1_w_out", "mix_norm", "mix_w_in", "conv_w", "conv_b", "conv_ln_g",
             "conv_ln_b", "ret_norm_g", "mix_w_out", "ffn2_norm", "ffn2_w_in", "ffn2_w_out", "final_norm"]
    return (total, dx[None], *[grad[n] for n in order], *[delta[n] for n in order],
            *[new_m[n] for n in order], *[new_v[n] for n in order])
```

```python
import functools

import numpy as np
import jax
import jax.numpy as jnp
from jax import lax
from jax.experimental import pallas as pl
from jax.experimental.pallas import tpu as pltpu

F32 = jnp.float32
BF16 = jnp.bfloat16

D_MODEL = 1024
D_FF = 2816
N_SHARD = 4
FF_SHARD = 2 * D_FF // N_SHARD
MIX_SHARD = 3072 // N_SHARD
HEAD = 64
SLAB = 256
CONV_W = 31
CONV_PAD = 32
CHUNK = 64
EPS = 1e-6
ROPE_BASE = 10000.0
DEPTH = 2

LR, B1, B2, ADAM_EPS, WD, STEP = 0.001, 0.9, 0.999, 1e-08, 0.01, 10

VMEM_LIMIT = 56 * 1024 * 1024


def _params(n_grid):
    return pltpu.CompilerParams(dimension_semantics=("arbitrary",) * n_grid, vmem_limit_bytes=VMEM_LIMIT)


def _rw(name, body, grid, ins, in_specs, rows=(), row_specs=(), accs=(), acc_specs=(), aliases=None):
    n_in, n_row = len(ins), len(rows)
    carried = sorted(aliases) if aliases else []

    def kern(*refs):
        vals = [r[...] for i, r in enumerate(refs[:n_in]) if i not in carried]
        row_vals, acc_vals = body(*vals)
        for r, v in zip(refs[n_in:n_in + n_row], row_vals):
            r[...] = v.astype(r.dtype)
        acc_refs = refs[n_in + n_row:]
        if acc_refs:
            first = functools.reduce(jnp.logical_and, [pl.program_id(a) == 0 for a in range(len(grid))])

            @pl.when(first)
            def _():
                for r in acc_refs:
                    r[...] = jnp.zeros(r.shape, r.dtype)

            for r, v in zip(acc_refs, acc_vals):
                r[...] += v.astype(r.dtype)

    return pl.pallas_call(
        kern, name=name, grid=grid, in_specs=list(in_specs), out_specs=list(row_specs) + list(acc_specs),
        out_shape=list(rows) + list(accs), input_output_aliases=dict(aliases or {}),
        compiler_params=_params(len(grid)))(*ins)


def _sds(shape, dtype):
    return jax.ShapeDtypeStruct(shape, dtype)


def _rms(x, g):
    return x * lax.rsqrt(jnp.mean(x * x, axis=-1, keepdims=True) + EPS) * g


def _row_spec(tm, c):
    return pl.BlockSpec((tm, c), lambda i: (i, 0))


def _vec_spec(c):
    return pl.BlockSpec((1, c), lambda i: (0, 0))


def _swiglu(gate, up):
    return jax.nn.silu(gate) * up


def _swiglu_fwd(name, u, tm):
    _, s, c = u.shape
    return _rw(name, lambda ub: ((_swiglu(ub[0:2].astype(F32), ub[2:4].astype(F32)),), ()), (s // tm,), [u],
               [pl.BlockSpec((4, tm, c), lambda i: (0, i, 0))],
               [_sds((2, s, c), BF16)], [pl.BlockSpec((2, tm, c), lambda i: (0, i, 0))])[0]


def _swiglu_bwd(name, u, da, tm):
    _, s, c = u.shape

    def body(ub, dab):
        _, vjp = jax.vjp(_swiglu, ub[0:2].astype(F32), ub[2:4].astype(F32))
        dg, du = vjp(dab.astype(F32))
        return (jnp.concatenate([dg, du], axis=0),), ()

    return _rw(name, body, (s // tm,), [u, da],
               [pl.BlockSpec((4, tm, c), lambda i: (0, i, 0)), pl.BlockSpec((2, tm, c), lambda i: (0, i, 0))],
               [_sds((4, s, c), BF16)], [pl.BlockSpec((4, tm, c), lambda i: (0, i, 0))])[0]


def _ln_silu(y, g, b):
    mu = jnp.mean(y, axis=-1, keepdims=True)
    yc = y - mu
    var = jnp.mean(yc * yc, axis=-1, keepdims=True)
    return jax.nn.silu(yc * lax.rsqrt(var + EPS) * g + b)


def _ln_silu_fwd(name, y, g, b, tm):
    s, c = y.shape
    return _rw(name, lambda yb, gb, bb: ((_ln_silu(yb, gb, bb),), ()), (s // tm,), [y, g, b],
               [_row_spec(tm, c), _vec_spec(c), _vec_spec(c)], [_sds((s, c), BF16)], [_row_spec(tm, c)])[0]


def _ln_silu_bwd(name, y, dcat, g, b, tm):
    s, c = y.shape

    def body(yb, dob, gb, bb):
        _, vjp = jax.vjp(_ln_silu, yb, gb, bb)
        dy, dg, db = vjp(dob)
        return (dy,), (dg, db)

    return _rw(name, body, (s // tm,), [y, dcat, g, b],
               [_row_spec(tm, c), pl.BlockSpec((None, tm, c), lambda i: (0, i, 0)), _vec_spec(c), _vec_spec(c)],
               [_sds((s, c), F32)], [_row_spec(tm, c)],
               [_sds((1, c), F32)] * 2, [_vec_spec(c)] * 2)


def _head_masks():
    lane = np.arange(SLAB) // HEAD
    m = np.zeros((8, SLAB), np.float32)
    for h in range(SLAB // HEAD):
        m[h] = (lane == h)
    return jnp.asarray(m)


def _gated_head_norm(y, gate, g, hm):
    mu = jnp.zeros_like(y)
    for h in range(SLAB // HEAD):
        mu = mu + hm[h:h + 1] * (jnp.sum(y * hm[h:h + 1], axis=-1, keepdims=True) / HEAD)
    yc = y - mu
    var = jnp.zeros_like(y)
    for h in range(SLAB // HEAD):
        var = var + hm[h:h + 1] * (jnp.sum(yc * yc * hm[h:h + 1], axis=-1, keepdims=True) / HEAD)
    return jax.nn.silu(gate) * (yc * lax.rsqrt(var + EPS) * g)


PER_SHARD = MIX_SHARD // SLAB


def _slab_spec(tm, j):
    return pl.BlockSpec((None, tm, SLAB), lambda i: (j, i, 0))


def _proj_slab_spec(tm, j):
    return pl.BlockSpec((None, tm, SLAB), lambda i: (j // PER_SHARD, i, j % PER_SHARD))


def _ghn_fwd(name, y, p32, g, tm):
    s, c = y.shape
    hm = _head_masks()
    return _rw(name, lambda yb, gb, wb, hb: ((_gated_head_norm(yb, gb, wb, hb),), ()), (s // tm,),
               [y, p32, g, hm],
               [_row_spec(tm, c), _proj_slab_spec(tm, 11), _vec_spec(c), pl.BlockSpec((8, c), lambda i: (0, 0))],
               [_sds((s, c), BF16)], [_row_spec(tm, c)])[0]


def _ghn_bwd(name, y, p32, dcat, g, tm):
    s, c = y.shape
    hm = _head_masks()

    def body(yb, gb, dob, wb, hb):
        _, vjp = jax.vjp(lambda a, b_, c_: _gated_head_norm(a, b_, c_, hb), yb, gb, wb)
        dy, dgate, dw = vjp(dob)
        return (dy, dgate), (dw,)

    return _rw(name, body, (s // tm,), [y, p32, dcat, g, hm],
               [_row_spec(tm, c), _proj_slab_spec(tm, 11), _slab_spec(tm, 3), _vec_spec(c),
                pl.BlockSpec((8, c), lambda i: (0, 0))],
               [_sds((s, c), F32)] * 2, [_row_spec(tm, c)] * 2,
               [_sds((1, c), F32)], [_vec_spec(c)])


def _assemble(name, parts, per, tm):
    s = parts[0][0].shape[-2]
    specs = [_row_spec(tm, SLAB) if j is None else pl.BlockSpec((None, tm, SLAB), lambda i, j=j: (j, i, 0))
             for _, j in parts]

    def body(*blocks):
        rows = [jnp.concatenate([b.astype(BF16) for b in blocks[per * q:per * (q + 1)]], axis=-1)
                for q in range(len(blocks) // per)]
        return (jnp.stack(rows),), ()

    nq = len(parts) // per
    return _rw(name, body, (s // tm,), [a for a, _ in parts], specs, [_sds((nq, s, per * SLAB), BF16)],
               [pl.BlockSpec((nq, tm, per * SLAB), lambda i: (0, i, 0))])[0]


def _final(name, x, tgt, g, tm):
    s, d = x.shape

    def body(xb, tb, gb):
        yf, vjp = jax.vjp(_rms, xb, gb)
        err = yf - tb
        dx, dg = vjp(err * (1.0 / d))
        part = 0.5 * jnp.sum(jnp.mean(err * err, axis=-1, keepdims=True), axis=0, keepdims=True)
        return (dx,), (dg, jnp.broadcast_to(part, (1, 128)))

    return _rw(name, body, (s // tm,), [x, tgt, g],
               [_row_spec(tm, d), _row_spec(tm, d), _vec_spec(d)],
               [_sds((s, d), F32)], [_row_spec(tm, d)],
               [_sds((1, d), F32), _sds((1, 128), F32)], [_vec_spec(d), _vec_spec(128)])


NN = (((1,), (0,)), ((), ()))
NT = (((1,), (1,)), ((), ()))
TN = (((0,), (0,)), ((), ()))


def _mm(name, a, b, grid, a_spec, b_spec, outs, out_specs, acc_shape, dims, alpha=1.0):
    nk = grid[-1]
    n_out = len(outs)

    def kern(*refs):
        a_ref, b_ref = refs[0], refs[1]
        o_refs = refs[2:2 + n_out]
        part = lax.dot_general(a_ref[...].astype(BF16), b_ref[...].astype(BF16), dims,
                               preferred_element_type=F32)

        def finish(r):
            if alpha != 1.0:
                r = r * alpha
            for o in o_refs:
                o[...] = r.astype(o.dtype)

        if nk == 1:
            finish(part)
            return
        acc_ref = refs[-1]
        k = pl.program_id(len(grid) - 1)

        @pl.when(k == 0)
        def _():
            acc_ref[...] = part

        @pl.when(jnp.logical_and(k > 0, k < nk - 1))
        def _():
            acc_ref[...] += part

        @pl.when(k == nk - 1)
        def _():
            finish(acc_ref[...] + part)

    return pl.pallas_call(
        kern, name=name, grid=grid, in_specs=[a_spec, b_spec], out_specs=list(out_specs), out_shape=list(outs),
        scratch_shapes=[pltpu.VMEM(acc_shape, F32)] if nk > 1 else [],
        compiler_params=_params(len(grid)))(a, b)


def _norm_proj_in(name, x, g, w, l, tm, dtypes):
    s, d = x.shape
    n = w.shape[-1]
    n_out = len(dtypes)

    def kern(x_ref, g_ref, w_ref, h_ref, *rest):
        o_refs, h_vmem = rest[:n_out], rest[n_out]

        @pl.when(pl.program_id(1) == 0)
        def _():
            h = _rms(x_ref[...], g_ref[...]).astype(BF16)
            h_vmem[...] = h
            h_ref[...] = h

        r = jnp.dot(h_vmem[...], w_ref[...], preferred_element_type=F32)
        for o in o_refs:
            o[...] = r.astype(o.dtype)

    out = pl.pallas_call(
        kern, name=name, grid=(s // tm, N_SHARD),
        in_specs=[pl.BlockSpec((tm, d), lambda i, b: (i, 0)), pl.BlockSpec((1, d), lambda i, b: (0, 0)),
                  pl.BlockSpec((None, None, d, n), lambda i, b: (l, b, 0, 0))],
        out_specs=[pl.BlockSpec((tm, d), lambda i, b: (i, 0))] +
                  [pl.BlockSpec((None, tm, n), lambda i, b: (b, i, 0))] * n_out,
        out_shape=[_sds((s, d), BF16)] + [_sds((N_SHARD, s, n), t) for t in dtypes],
        scratch_shapes=[pltpu.VMEM((tm, d), BF16)], compiler_params=_params(2))(x, g, w)
    return out[0], out[1:]


def _back_in_norm(name, du, w, l, tm, x, dres, g):
    nk, s, n = du.shape
    d = w.shape[2]

    def kern(du_ref, w_ref, x_ref, dres_ref, g_ref, dx_ref, dg_ref):
        dh = lax.dot_general(du_ref[0], w_ref[0], NT, preferred_element_type=F32)
        for k in range(1, nk):
            dh = dh + lax.dot_general(du_ref[k], w_ref[k], NT, preferred_element_type=F32)
        _, vjp = jax.vjp(_rms, x_ref[...], g_ref[...])
        dx, dg = vjp(dh)
        dx_ref[...] = dx + dres_ref[...]

        @pl.when(pl.program_id(0) == 0)
        def _():
            dg_ref[...] = dg

        @pl.when(pl.program_id(0) > 0)
        def _():
            dg_ref[...] += dg

    row = pl.BlockSpec((tm, d), lambda i: (i, 0))
    vec = pl.BlockSpec((1, d), lambda i: (0, 0))
    return pl.pallas_call(
        kern, name=name, grid=(s // tm,),
        in_specs=[pl.BlockSpec((nk, tm, n), lambda i: (0, i, 0)),
                  pl.BlockSpec((None, nk, d, n), lambda i: (l, 0, 0, 0)), row, row, vec],
        out_specs=[row, vec], out_shape=[_sds((s, d), F32), _sds((1, d), F32)],
        compiler_params=_params(1))(du, w, x, dres, g)


def _proj_out(name, a, w, l, res, alpha, tm):
    nk, s, r = a.shape
    d = w.shape[-1]

    def kern(a_ref, w_ref, res_ref, o_ref):
        y = jnp.dot(a_ref[0], w_ref[0], preferred_element_type=F32)
        for k in range(1, nk):
            y = y + jnp.dot(a_ref[k], w_ref[k], preferred_element_type=F32)
        o_ref[...] = res_ref[...] + (y * alpha if alpha != 1.0 else y)

    row = pl.BlockSpec((tm, d), lambda i: (i, 0))
    return pl.pallas_call(
        kern, name=name, grid=(s // tm,),
        in_specs=[pl.BlockSpec((nk, tm, r), lambda i: (0, i, 0)),
                  pl.BlockSpec((None, nk, r, d), lambda i: (l, 0, 0, 0)), row],
        out_specs=row, out_shape=_sds((s, d), F32), compiler_params=_params(1))(a, w, res)


def _back_out(name, dy, w, l, alpha, tm, out_dtype):
    s, d = dy.shape
    nk, r = w.shape[1], w.shape[2]
    return _mm(name, dy, w, (nk, s // tm, 1),
               pl.BlockSpec((tm, d), lambda b, i, k: (i, 0)),
               pl.BlockSpec((None, None, r, d), lambda b, i, k: (l, b, 0, 0)),
               [_sds((nk, s, r), out_dtype)], [pl.BlockSpec((None, tm, r), lambda b, i, k: (b, i, 0))],
               (tm, r), NT, alpha=alpha)[0]


def _grad_in(name, h, du, ts):
    s, d = h.shape
    nb, _, n = du.shape
    return _mm(name, h, du, (nb, 1, s // ts),
               pl.BlockSpec((ts, d), lambda b, i, k: (k, 0)),
               pl.BlockSpec((None, ts, n), lambda b, i, k: (b, k, 0)),
               [_sds((nb, d, n), BF16)], [pl.BlockSpec((None, d, n), lambda b, i, k: (b, 0, 0))],
               (d, n), TN)[0]


def _grad_out(name, a, dy, alpha, ts):
    nb, s, r = a.shape
    d = dy.shape[1]
    return _mm(name, a, dy, (nb, 1, s // ts),
               pl.BlockSpec((None, ts, r), lambda b, i, k: (b, k, 0)),
               pl.BlockSpec((ts, d), lambda b, i, k: (k, 0)),
               [_sds((nb, r, d), BF16)], [pl.BlockSpec((None, r, d), lambda b, i, k: (b, 0, 0))],
               (r, d), TN, alpha=alpha)[0]


CONV_TILE = 256


def _shifted(win, off, rows):
    n = win.shape[0]
    return pltpu.roll(win, (n - off) % n, 0)[0:rows] if off % n else win[0:rows]


def _conv_fwd(name, p32, w, bias):
    s = p32.shape[1]
    cb = 128
    nt = s // CONV_TILE

    def kern(a_ref, b_ref, w_ref, bias_ref, y_ref, vpad):
        vpad[0:CONV_PAD, :] = jnp.zeros((CONV_PAD, cb), F32)

        def fill(i, c):
            r = pl.multiple_of(i * CONV_TILE, CONV_TILE)
            vpad[pl.ds(CONV_PAD + r, CONV_TILE), :] = (
                a_ref[pl.ds(r, CONV_TILE), :] * jax.nn.sigmoid(b_ref[pl.ds(r, CONV_TILE), :]))
            return c

        lax.fori_loop(0, nt, fill, 0)

        def tile(i, c):
            r = pl.multiple_of(i * CONV_TILE, CONV_TILE)
            win = vpad[pl.ds(r, CONV_TILE + CONV_PAD), :]
            acc = jnp.broadcast_to(bias_ref[...], (CONV_TILE, cb))
            for j in range(CONV_W):
                acc = acc + w_ref[j:j + 1, :] * _shifted(win, j + 2, CONV_TILE)
            y_ref[pl.ds(r, CONV_TILE), :] = acc
            return c

        lax.fori_loop(0, nt, tile, 0)

    return pl.pallas_call(
        kern, name=name, grid=(SLAB // cb,),
        in_specs=[pl.BlockSpec((None, s, cb), lambda c: (0, 0, c)),
                  pl.BlockSpec((None, s, cb), lambda c: (0, 0, SLAB // cb + c)),
                  pl.BlockSpec((CONV_W, cb), lambda c: (0, c)),
                  pl.BlockSpec((1, cb), lambda c: (0, c))],
        out_specs=pl.BlockSpec((s, cb), lambda c: (0, c)),
        out_shape=_sds((s, SLAB), F32),
        scratch_shapes=[pltpu.VMEM((s + CONV_PAD, cb), F32)],
        compiler_params=_params(1))(p32, p32, w, bias)


def _conv_bwd(name, p32, w, dy):
    s = p32.shape[1]
    cb = 128
    nt = s // CONV_TILE

    def kern(a_ref, b_ref, w_ref, dy_ref, da_ref, db_ref, dw_ref, dbias_ref, vpad, dpad):
        vpad[0:CONV_PAD, :] = jnp.zeros((CONV_PAD, cb), F32)
        dpad[s:s + CONV_PAD, :] = jnp.zeros((CONV_PAD, cb), F32)
        dw_ref[...] = jnp.zeros((CONV_PAD, cb), F32)
        dbias_ref[...] = jnp.zeros((1, cb), F32)

        def fill(i, c):
            r = pl.multiple_of(i * CONV_TILE, CONV_TILE)
            vpad[pl.ds(CONV_PAD + r, CONV_TILE), :] = (
                a_ref[pl.ds(r, CONV_TILE), :] * jax.nn.sigmoid(b_ref[pl.ds(r, CONV_TILE), :]))
            dpad[pl.ds(r, CONV_TILE), :] = dy_ref[pl.ds(r, CONV_TILE), :]
            return c

        lax.fori_loop(0, nt, fill, 0)

        def tile(i, c):
            r = pl.multiple_of(i * CONV_TILE, CONV_TILE)
            dwin = dpad[pl.ds(r, CONV_TILE + CONV_PAD), :]
            vwin = vpad[pl.ds(r, CONV_TILE + CONV_PAD), :]
            dyt = dwin[0:CONV_TILE]
            dv = jnp.zeros((CONV_TILE, cb), F32)
            for j in range(CONV_W):
                dv = dv + w_ref[j:j + 1, :] * _shifted(dwin, CONV_W - 1 - j, CONV_TILE)
                dw_ref[j:j + 1, :] += jnp.sum(dyt * _shifted(vwin, j + 2, CONV_TILE), axis=0, keepdims=True)
            dbias_ref[...] += jnp.sum(dyt, axis=0, keepdims=True)
            a = a_ref[pl.ds(r, CONV_TILE), :]
            sg = jax.nn.sigmoid(b_ref[pl.ds(r, CONV_TILE), :])
            da_ref[pl.ds(r, CONV_TILE), :] = dv * sg
            db_ref[pl.ds(r, CONV_TILE), :] = dv * a * sg * (1.0 - sg)
            return c

        lax.fori_loop(0, nt, tile, 0)

    col = pl.BlockSpec((s, cb), lambda c: (0, c))
    return pl.pallas_call(
        kern, name=name, grid=(SLAB // cb,),
        in_specs=[pl.BlockSpec((None, s, cb), lambda c: (0, 0, c)),
                  pl.BlockSpec((None, s, cb), lambda c: (0, 0, SLAB // cb + c)),
                  pl.BlockSpec((CONV_W, cb), lambda c: (0, c)), col],
        out_specs=[col, col, pl.BlockSpec((CONV_PAD, cb), lambda c: (0, c)), pl.BlockSpec((1, cb), lambda c: (0, c))],
        out_shape=[_sds((s, SLAB), F32), _sds((s, SLAB), F32), _sds((CONV_PAD, SLAB), F32), _sds((1, SLAB), F32)],
        scratch_shapes=[pltpu.VMEM((s + CONV_PAD, cb), F32), pltpu.VMEM((s + CONV_PAD, cb), F32)],
        compiler_params=_params(1))(p32, p32, w, dy)


SB_BLOCK = 256
N_HEAD = SLAB // HEAD


def _sb_logits(qm, k, tri):
    z = lax.dot_general(qm, k, NT, preferred_element_type=F32)
    sign_bit = jnp.uint32(0x80000000)
    neg_abs = lax.bitcast_convert_type(lax.bitcast_convert_type(z, jnp.uint32) | sign_bit, F32)
    lb = jnp.minimum(z, 0.0) - jnp.log(1.0 + jnp.exp(neg_abs))
    ln = lb - z
    if tri is not None:
        ln = jnp.where(tri, ln, 0.0)
    return lb, ln


def _first_col(x):
    return jnp.broadcast_to(x[:, 0:1], (x.shape[0], 128))


def _head_stack(dst, x, lane_head, bq):
    for h in range(N_HEAD):
        dst[h * bq:(h + 1) * bq, :] = jnp.where(lane_head == h, x, jnp.zeros_like(x))


def _sb_fwd(name, p16):
    s = p16.shape[1]
    bq = min(SB_BLOCK, s)
    nq = s // bq

    def kern(q_ref, k_ref, v_ref, o_ref, w_hbm, lb_hbm, qm_ref, v4_refs, w4_refs, ws_ref, lbs_ref, acc_ref, r_ref,
             sem):
        g, qi = pl.program_id(0), pl.program_id(1)
        lane_head = lax.broadcasted_iota(jnp.int32, (1, SLAB), 1) // HEAD
        _head_stack(qm_ref, (q_ref[...].astype(F32) * (HEAD ** -0.5)).astype(BF16), lane_head, bq)
        row = lax.broadcasted_iota(jnp.int32, (bq, bq), 0)
        col = lax.broadcasted_iota(jnp.int32, (bq, bq), 1)
        after = (row > col).astype(BF16)
        tri = col < row
        acc_ref[...] = jnp.zeros((bq, SLAB), F32)
        r_ref[...] = jnp.zeros((N_HEAD, bq, 128), F32)

        def saves(slot, kb):
            return (pltpu.make_async_copy(ws_ref.at[slot], w_hbm.at[g, qi, kb], sem.at[0, slot]),
                    pltpu.make_async_copy(lbs_ref.at[slot], lb_hbm.at[g, qi, kb], sem.at[1, slot]))

        def tile(i, masked, u):
            kb, slot = qi - i, i % 4
            v4_ref, w4_ref = v4_refs.at[u], w4_refs.at[u]
            rows = pl.ds(pl.multiple_of(kb * bq, bq), bq)
            k = k_ref[rows, :]
            _head_stack(v4_ref, v_ref[rows, :], lane_head, bq)
            for h in range(N_HEAD):
                mine = pl.ds(h * bq, bq)
                lb, ln = _sb_logits(qm_ref[h * bq:(h + 1) * bq, :], k, tri if masked else None)
                rem = jnp.dot(ln.astype(BF16), after, preferred_element_type=F32)
                w = jnp.exp(lb + rem + r_ref[h][:, 0:1])
                if masked:
                    w = jnp.where(tri, w, 0.0)
                wb = w.astype(BF16)
                w4_ref[:, h * bq:(h + 1) * bq] = wb
                ws_ref[slot, mine, :] = wb
                lbs_ref[slot, mine, :] = lb.astype(BF16)
                r_ref[h] += _first_col(rem[:, 0:128] + ln[:, 0:128])
            acc_ref[...] += jnp.dot(w4_ref[...], v4_ref[...], preferred_element_type=F32)

        def save(i, start):
            for cp in saves(i % 4, qi - i):
                cp.start() if start else cp.wait()

        @pl.when(qi == 0)
        def _():
            tile(0, True, 0)
            save(0, True)
            save(0, False)

        @pl.when(qi >= 1)
        def _():
            tile(0, True, 0)
            tile(1, False, 1)
            save(0, True)
            save(1, True)

        def pair(j, c):
            tile(2 * j, False, 0)
            tile(2 * j + 1, False, 1)
            save(2 * j - 2, False)
            save(2 * j - 1, False)
            save(2 * j, True)
            save(2 * j + 1, True)
            return c

        n_pair = (qi + 1) // 2
        lax.fori_loop(1, n_pair, pair, 0)

        @pl.when(jnp.logical_and(qi >= 1, qi % 2 == 1))
        def _():
            save(qi - 1, False)
            save(qi, False)

        @pl.when(jnp.logical_and(qi >= 2, qi % 2 == 0))
        def _():
            tile(qi, False, 0)
            save(qi - 2, False)
            save(qi - 1, False)
            save(qi, True)
            save(qi, False)

        o_ref[...] = acc_ref[...]

    saved = _sds((2, nq, nq, N_HEAD * bq, bq), BF16)
    return pl.pallas_call(
        kern, name=name, grid=(2, nq),
        in_specs=[pl.BlockSpec((None, bq, SLAB), lambda g, i: ((2 + g) // PER_SHARD, i, (2 + g) % PER_SHARD)),
                  pl.BlockSpec((None, s, SLAB), lambda g, i: ((4 + g) // PER_SHARD, 0, (4 + g) % PER_SHARD)),
                  pl.BlockSpec((None, s, SLAB), lambda g, i: ((6 + g) // PER_SHARD, 0, (6 + g) % PER_SHARD))],
        out_specs=[pl.BlockSpec((None, bq, SLAB), lambda g, i: (g, i, 0)),
                   pl.BlockSpec(memory_space=pl.ANY), pl.BlockSpec(memory_space=pl.ANY)],
        out_shape=[_sds((2, s, SLAB), F32), saved, saved],
        scratch_shapes=[pltpu.VMEM((N_HEAD * bq, SLAB), BF16), pltpu.VMEM((2, N_HEAD * bq, SLAB), BF16),
                        pltpu.VMEM((2, bq, N_HEAD * bq), BF16), pltpu.VMEM((4, N_HEAD * bq, bq), BF16),
                        pltpu.VMEM((4, N_HEAD * bq, bq), BF16), pltpu.VMEM((bq, SLAB), F32),
                        pltpu.VMEM((N_HEAD, bq, 128), F32), pltpu.SemaphoreType.DMA((2, 4))],
        compiler_params=_params(2))(p16, p16, p16)


def _sb_bwd(name, p16, w_saved, lb_saved, dcat):
    s = p16.shape[1]
    bq = min(SB_BLOCK, s)
    nq = s // bq

    def kern(q_ref, k_ref, v_ref, do_ref, w_hbm, lb_hbm, dq_ref, dk_hbm, dv_hbm, dk_acc, dv_acc, dq_acc,
             qm_ref, dom_ref, k4_refs, dzc_refs, dzs_refs, ws_ref, lbs_ref, c_ref, sem, lsem):
        g, qi = pl.program_id(0), pl.program_id(1)

        @pl.when(qi == 0)
        def _():
            dk_acc[...] = jnp.zeros((s, SLAB), F32)
            dv_acc[...] = jnp.zeros((s, SLAB), F32)

        lane_head = lax.broadcasted_iota(jnp.int32, (1, SLAB), 1) // HEAD
        _head_stack(qm_ref, (q_ref[...].astype(F32) * (HEAD ** -0.5)).astype(BF16), lane_head, bq)
        _head_stack(dom_ref, do_ref[...].astype(BF16), lane_head, bq)
        row = lax.broadcasted_iota(jnp.int32, (bq, bq), 0)
        col = lax.broadcasted_iota(jnp.int32, (bq, bq), 1)
        earlier = (row < col).astype(BF16)
        tri = col < row
        dq_acc[...] = jnp.zeros((bq, SLAB), F32)
        c_ref[...] = jnp.zeros((N_HEAD, bq, 128), F32)

        def loads(kb):
            slot, kb = kb % 4, jnp.minimum(kb, qi)
            return (pltpu.make_async_copy(w_hbm.at[g, qi, kb], ws_ref.at[slot], lsem.at[0, slot]),
                    pltpu.make_async_copy(lb_hbm.at[g, qi, kb], lbs_ref.at[slot], lsem.at[1, slot]))

        def tile(kb, masked, u):
            slot = kb % 4
            k4_ref, dzc_ref, dzs_ref = k4_refs.at[u], dzc_refs.at[u], dzs_refs.at[u]
            rows = pl.ds(pl.multiple_of(kb * bq, bq), bq)
            k, v = k_ref[rows, :], v_ref[rows, :]
            _head_stack(k4_ref, k, lane_head, bq)
            for h in range(N_HEAD):
                mine = slice(h * bq, (h + 1) * bq)
                wb = ws_ref[slot, pl.ds(h * bq, bq), :]
                dl = wb.astype(F32) * lax.dot_general(dom_ref[mine, :], v, NT, preferred_element_type=F32)
                prefix = jnp.dot(dl.astype(BF16), earlier, preferred_element_type=F32)
                before = prefix + c_ref[h][:, 0:1]
                sig = jnp.exp(lbs_ref[slot, pl.ds(h * bq, bq), :].astype(F32))
                dz = dl - sig * (dl + before)
                if masked:
                    dz = jnp.where(tri, dz, 0.0)
                dzb = dz.astype(BF16)
                dzc_ref[:, mine] = dzb
                dzs_ref[mine, :] = dzb
                tail = prefix[:, bq - 128:] + dl[:, bq - 128:]
                c_ref[h] += jnp.broadcast_to(tail[:, 127:128], (bq, 128))
            dq_acc[...] += jnp.dot(dzc_ref[...], k4_ref[...], preferred_element_type=F32)
            dk_acc[rows, :] += lax.dot_general(dzs_ref[...], qm_ref[...], TN, preferred_element_type=F32)
            dv_acc[rows, :] += lax.dot_general(ws_ref[slot], dom_ref[...], TN, preferred_element_type=F32)

        for kb in (0, 1):
            for cp in loads(kb):
                cp.start()

        def pair(j, c):
            for kb in (2 * j + 2, 2 * j + 3):
                for cp in loads(kb):
                    cp.start()
            for kb in (2 * j, 2 * j + 1):
                for cp in loads(kb):
                    cp.wait()
            tile(2 * j, False, 0)
            tile(2 * j + 1, False, 1)
            return c

        lax.fori_loop(0, qi // 2, pair, 0)
        for kb in (qi - qi % 2, qi - qi % 2 + 1):
            for cp in loads(kb):
                cp.wait()

        @pl.when(qi % 2 == 1)
        def _():
            tile(qi - 1, False, 0)
            tile(qi, True, 1)

        @pl.when(qi % 2 == 0)
        def _():
            tile(qi, True, 0)

        dq_ref[...] = dq_acc[...] * (HEAD ** -0.5)

        @pl.when(qi == nq - 1)
        def _():
            ck = pltpu.make_async_copy(dk_acc, dk_hbm.at[g], sem.at[0])
            cv = pltpu.make_async_copy(dv_acc, dv_hbm.at[g], sem.at[1])
            ck.start()
            cv.start()
            ck.wait()
            cv.wait()

    blk = lambda j0: pl.BlockSpec((None, bq, SLAB), lambda g, i: (j0 + g, i, 0))
    full = lambda j0: pl.BlockSpec((None, s, SLAB), lambda g, i: ((j0 + g) // PER_SHARD, 0, (j0 + g) % PER_SHARD))
    q_blk = pl.BlockSpec((None, bq, SLAB), lambda g, i: ((2 + g) // PER_SHARD, i, (2 + g) % PER_SHARD))
    stack16 = pltpu.VMEM((N_HEAD * bq, SLAB), BF16)
    return pl.pallas_call(
        kern, name=name, grid=(2, nq),
        in_specs=[q_blk, full(4), full(6), blk(1),
                  pl.BlockSpec(memory_space=pl.ANY), pl.BlockSpec(memory_space=pl.ANY)],
        out_specs=[blk(0), pl.BlockSpec(memory_space=pl.ANY), pl.BlockSpec(memory_space=pl.ANY)],
        out_shape=[_sds((2, s, SLAB), F32)] * 3,
        scratch_shapes=[pltpu.VMEM((s, SLAB), F32), pltpu.VMEM((s, SLAB), F32), pltpu.VMEM((bq, SLAB), F32),
                        stack16, stack16, pltpu.VMEM((2, N_HEAD * bq, SLAB), BF16),
                        pltpu.VMEM((2, bq, N_HEAD * bq), BF16), pltpu.VMEM((2, N_HEAD * bq, bq), BF16),
                        pltpu.VMEM((4, N_HEAD * bq, bq), BF16), pltpu.VMEM((4, N_HEAD * bq, bq), BF16),
                        pltpu.VMEM((N_HEAD, bq, 128), F32),
                        pltpu.SemaphoreType.DMA((2,)), pltpu.SemaphoreType.DMA((2, 4))],
        compiler_params=_params(2))(p16, p16, p16, dcat, w_saved, lb_saved)


RET_BLOCK = 256


def _ret_tables(s, bl):
    nh = SLAB // HEAD
    lane_h = np.arange(SLAB) // HEAD
    log_gamma = np.log1p(-np.exp2(-5.0 - np.arange(nh, dtype=np.float64)))
    lg_lane = log_gamma[lane_h]
    half = HEAD // 2
    inv = 1.0 / (ROPE_BASE ** (np.arange(half, dtype=np.float64) / half))
    ang = np.arange(s, dtype=np.float64)[:, None] * inv[None, :]
    within = np.arange(SLAB) % HEAD
    cos = np.cos(ang)[:, within % half]
    sin = np.sin(ang)[:, within % half] * np.where(within < half, -1.0, 1.0)[None, :]
    perm = np.zeros((SLAB, SLAB))
    partner = np.where(within < half, np.arange(SLAB) + half, np.arange(SLAB) - half)
    perm[partner, np.arange(SLAB)] = 1.0
    i = np.arange(bl)
    diff = i[:, None] - i[None, :]
    same = (i[:, None] // CHUNK) == (i[None, :] // CHUNK)
    earlier = (i[None, :] // CHUNK) < (i[:, None] // CHUNK)
    decay = np.zeros((nh, bl, bl))
    for h in range(nh):
        decay[h] = np.where(same, np.exp(log_gamma[h] * np.abs(diff)),
                            np.where(earlier, np.exp(log_gamma[h] * diff), 0.0))
    qd = np.exp(lg_lane[None, :] * (i[:, None] + 1.0))
    kd = np.exp(lg_lane[None, :] * (bl - 1.0 - i[:, None]))
    gam = np.exp(lg_lane * bl)[:, None] * np.ones((1, SLAB))
    bd = (lane_h[:, None] == lane_h[None, :]).astype(np.float64)
    f = lambda a: jnp.asarray(a, F32)
    return f(cos), f(sin), f(perm), f(decay), f(qd), f(kd), f(gam), f(bd)


def _ret_block(q, k, v, state, cos, sin, perm, decay, qd, kd, gam, bd, hm):
    qr = (q * cos + jnp.dot(q, perm, preferred_element_type=F32) * sin) * (HEAD ** -0.5)
    kr = k * cos + jnp.dot(k, perm, preferred_element_type=F32) * sin
    y = jnp.dot(qr * qd, state, preferred_element_type=F32)
    for h in range(SLAB // HEAD):
        m = hm[h:h + 1]
        sc = lax.dot_general(qr * m, kr, NT, preferred_element_type=F32) * decay[h]
        y = y + jnp.dot(sc, v * m, preferred_element_type=F32)
    new_state = gam * state + lax.dot_general(kr * kd, v, TN, preferred_element_type=F32) * bd
    return y, new_state


def _ret_specs(s, bl, rev):
    nb = s // bl
    pos = (lambda n: nb - 1 - n) if rev else (lambda n: n)
    slab = lambda j: pl.BlockSpec((None, bl, SLAB), lambda n: (j // PER_SHARD, pos(n), j % PER_SHARD))
    const2 = lambda r: pl.BlockSpec((r, SLAB), lambda n: (0, 0))
    tab = [pl.BlockSpec((bl, SLAB), lambda n: (pos(n), 0))] * 2 + [
        const2(SLAB), pl.BlockSpec((SLAB // HEAD, bl, bl), lambda n: (0, 0, 0)),
        const2(bl), const2(bl), const2(SLAB), const2(SLAB), const2(8)]
    return nb, pos, slab, tab


def _ret_fwd(name, p32):
    s = p32.shape[1]
    bl = min(RET_BLOCK, s)
    nb, pos, slab, tab = _ret_specs(s, bl, False)
    tables = _ret_tables(s, bl) + (_head_masks(),)

    def kern(q_ref, k_ref, v_ref, *rest):
        t_refs, (y_ref, st_ref, state) = rest[:9], rest[9:]

        @pl.when(pl.program_id(0) == 0)
        def _():
            state[...] = jnp.zeros((SLAB, SLAB), F32)

        st_ref[...] = state[...]
        y, new = _ret_block(q_ref[...], k_ref[...], v_ref[...], state[...], *[t[...] for t in t_refs])
        y_ref[...] = y
        state[...] = new

    return pl.pallas_call(
        kern, name=name, grid=(nb,), in_specs=[slab(8), slab(9), slab(10)] + tab,
        out_specs=[pl.BlockSpec((bl, SLAB), lambda n: (n, 0)), pl.BlockSpec((None, SLAB, SLAB), lambda n: (n, 0, 0))],
        out_shape=[_sds((s, SLAB), F32), _sds((nb, SLAB, SLAB), F32)],
        scratch_shapes=[pltpu.VMEM((SLAB, SLAB), F32)], compiler_params=_params(1))(p32, p32, p32, *tables)


def _ret_bwd(name, p32, states, dy):
    s = p32.shape[1]
    bl = min(RET_BLOCK, s)
    nb, pos, slab, tab = _ret_specs(s, bl, True)
    tables = _ret_tables(s, bl) + (_head_masks(),)
    rowblk = pl.BlockSpec((bl, SLAB), lambda n: (pos(n), 0))

    def kern(q_ref, k_ref, v_ref, st_ref, dy_ref, *rest):
        t_refs, (dq_ref, dk_ref, dv_ref, dstate) = rest[:9], rest[9:]

        @pl.when(pl.program_id(0) == 0)
        def _():
            dstate[...] = jnp.zeros((SLAB, SLAB), F32)

        tv = [t[...] for t in t_refs]
        _, vjp = jax.vjp(lambda a, b, c, d: _ret_block(a, b, c, d, *tv),
                         q_ref[...], k_ref[...], v_ref[...], st_ref[...])
        dq, dk, dv, ds = vjp((dy_ref[...], dstate[...]))
        dq_ref[...] = dq
        dk_ref[...] = dk
        dv_ref[...] = dv
        dstate[...] = ds

    return pl.pallas_call(
        kern, name=name, grid=(nb,),
        in_specs=[slab(8), slab(9), slab(10), pl.BlockSpec((None, SLAB, SLAB), lambda n: (pos(n), 0, 0)), rowblk] + tab,
        out_specs=[rowblk] * 3, out_shape=[_sds((s, SLAB), F32)] * 3,
        scratch_shapes=[pltpu.VMEM((SLAB, SLAB), F32)], compiler_params=_params(1))(p32, p32, p32, states, dy, *tables)


TM_FFN = 1024
TM_SLAB = 2048
TM_NORM = 256
TM_OUT = 512
TM_GRAD = 2048
TM_RW = 512
TM_FF = 256


def _ffn_fwd(tag, x, g, w_in, w_out, l):
    tm = min(TM_FFN, x.shape[0])
    h, (u,) = _norm_proj_in(tag + "_in", x, g, w_in, l, tm, [BF16])
    a = _swiglu_fwd(tag + "_act", u, TM_FF)
    w_out2 = w_out.reshape(w_out.shape[0], 2, FF_SHARD, D_MODEL)
    xn = _proj_out(tag + "_out", a, w_out2, l, x, 0.5, min(TM_OUT, x.shape[0]))
    return xn, (x, h, u, a)


def _ffn_bwd(tag, saved, dxn, g, w_in, w_out, l):
    x, h, u, a = saved
    tm = min(TM_FFN, x.shape[0])
    w_out2 = w_out.reshape(w_out.shape[0], 2, FF_SHARD, D_MODEL)
    da = _back_out(tag + "_dact", dxn, w_out2, l, 0.5, tm, BF16)
    tk = min(TM_GRAD, x.shape[0])
    dw_out = _grad_out(tag + "_dwout", a, dxn, 0.5, tk)
    du = _swiglu_bwd(tag + "_dswi", u, da, TM_FF)
    dx, dg = _back_in_norm(tag + "_dh", du, w_in, l, min(TM_NORM, x.shape[0]), x, dxn, g)
    dw_in = _grad_in(tag + "_dwin", h, du, tk)
    return dx, dg, dw_in, dw_out.reshape(N_SHARD, D_FF // N_SHARD, D_MODEL)


def _mix_fwd(tag, x, sm, w_in, w_out, l, wl):
    h, (p32, p16) = _norm_proj_in(tag + "_in", x, sm["mix_norm"][l:l + 1], w_in, wl, min(TM_FFN, x.shape[0]),
                                  [F32, BF16])
    ypre = _conv_fwd(tag + "_conv", p32, sm["conv_w"][l], sm["conv_b"][l:l + 1])
    yconv = _ln_silu_fwd(tag + "_ln", ypre, sm["conv_ln_g"][l:l + 1], sm["conv_ln_b"][l:l + 1], TM_RW)
    osb, w_sb, lb_sb = _sb_fwd(tag + "_sb", p16)
    yr, states = _ret_fwd(tag + "_ret", p32)
    yret = _ghn_fwd(tag + "_ghn", yr, p32, sm["ret_norm_g"][l:l + 1], TM_RW)
    ycat = _assemble(tag + "_cat", [(yconv, None), (osb, 0), (osb, 1), (yret, None)], 1, TM_RW)
    xn = _proj_out(tag + "_out", ycat, w_out, wl, x, 1.0, min(TM_OUT, x.shape[0]))
    return xn, (x, h, p32, p16, ypre, (osb, w_sb, lb_sb), yr, states, ycat)


def _mix_bwd(tag, saved, dxn, sm, w_in, w_out, l, wl):
    x, h, p32, p16, ypre, osb, yr, states, ycat = saved
    ts = min(TM_SLAB, x.shape[0])
    dcat = _back_out(tag + "_dcat", dxn, w_out, wl, 1.0, ts, F32)
    dw_out = _grad_out(tag + "_dwout", ycat, dxn, 1.0, ts)
    dypre, dlg, dlb = _ln_silu_bwd(tag + "_dln", ypre, dcat, sm["conv_ln_g"][l:l + 1], sm["conv_ln_b"][l:l + 1], TM_RW)
    da, db, dcw, dcb = _conv_bwd(tag + "_dconv", p32, sm["conv_w"][l], dypre)
    dq, dk, dv = _sb_bwd(tag + "_dsb", p16, osb[1], osb[2], dcat)
    dyr, dgate, drg = _ghn_bwd(tag + "_dghn", yr, p32, dcat, sm["ret_norm_g"][l:l + 1], TM_RW)
    dqr, dkr, dvr = _ret_bwd(tag + "_dret", p32, states, dyr)
    dp = _assemble(tag + "_dp", [(da, None), (db, None), (dq, 0), (dq, 1), (dk, 0), (dk, 1), (dv, 0), (dv, 1),
                                 (dqr, None), (dkr, None), (dvr, None), (dgate, None)], PER_SHARD, TM_RW)
    dx, dg = _back_in_norm(tag + "_dh", dp, w_in, wl, min(TM_NORM, x.shape[0]), x, dxn, sm["mix_norm"][l:l + 1])
    dw_in = _grad_in(tag + "_dwin", h, dp, min(TM_GRAD, x.shape[0]))
    small = dict(mix_norm=dg, conv_w=dcw[0:CONV_W], conv_b=dcb, conv_ln_g=dlg, conv_ln_b=dlb, ret_norm_g=drg)
    return dx, small, dw_in, dw_out


def _local_step(x, tgt, early, last, sm, on_last_layer=None):
    saved = []
    weights = [(early, l) for l in range(DEPTH - 1)]
    for l in range(DEPTH):
        if l == DEPTH - 1:
            weights.append((last(x), 0))
        wt, wl = weights[l]
        x, s1 = _ffn_fwd(f"l{l}f1", x, sm["ffn1_norm"][l:l + 1], wt["ffn1_w_in"], wt["ffn1_w_out"], wl)
        x, s2 = _mix_fwd(f"l{l}mx", x, sm, wt["mix_w_in"], wt["mix_w_out"], l, wl)
        x, s3 = _ffn_fwd(f"l{l}f2", x, sm["ffn2_norm"][l:l + 1], wt["ffn2_w_in"], wt["ffn2_w_out"], wl)
        saved.append((s1, s2, s3))
    dx, dfinal, loss = _final("final", x, tgt, sm["final_norm"][None, :], TM_RW)
    big = [None] * DEPTH
    small = [None] * DEPTH
    started = None
    for l in reversed(range(DEPTH)):
        s1, s2, s3 = saved[l]
        wt, wl = weights[l]
        g2 = sm["ffn2_norm"][l:l + 1] if started is None else sm["ffn2_norm"][l:l + 1] + started
        dx, dg3, dwi3, dwo3 = _ffn_bwd(f"l{l}f2", s3, dx, g2, wt["ffn2_w_in"], wt["ffn2_w_out"], wl)
        dx, sml, dwi2, dwo2 = _mix_bwd(f"l{l}mx", s2, dx, sm, wt["mix_w_in"], wt["mix_w_out"], l, wl)
        dx, dg1, dwi1, dwo1 = _ffn_bwd(f"l{l}f1", s1, dx, sm["ffn1_norm"][l:l + 1], wt["ffn1_w_in"], wt["ffn1_w_out"], wl)
        big[l] = dict(ffn1_w_in=dwi1, ffn1_w_out=dwo1, mix_w_in=dwi2, mix_w_out=dwo2, ffn2_w_in=dwi3, ffn2_w_out=dwo3)
        sml.update(ffn1_norm=dg1, ffn2_norm=dg3)
        small[l] = sml
        if on_last_layer is not None and l == DEPTH - 1:
            started = on_last_layer(big[l])[0:1, 0:1]
    return loss, dx, big, small, dfinal


MESH = pl.DeviceIdType.MESH
ANY = pl.BlockSpec(memory_space=pl.ANY)
BIG = ("ffn1_w_in", "ffn1_w_out", "mix_w_in", "mix_w_out", "ffn2_w_in", "ffn2_w_out")


def _place():
    x, y, c = lax.axis_index("x"), lax.axis_index("y"), lax.axis_index("c")
    chips = [(1 - x, y), (x, 1 - y), (1 - x, 1 - y)]
    return x, y, c, chips


def _gather_weights(w16):
    n = len(w16)

    def kern(*refs):
        dst = refs[n:2 * n]
        send, recv = refs[2 * n:]
        x, y, c, chips = _place()
        mine = 2 * x + y
        firsts, passes = [], []
        for a in range(n):
            h = dst[a].shape[2] // 2
            own = dst[a].at[:, mine, pl.ds(c * h, h)]
            for j, (cx, cy) in enumerate(chips):
                cp = pltpu.make_async_remote_copy(
                    src_ref=own, dst_ref=own, send_sem=send.at[6 * a + j], recv_sem=recv.at[6 * a + j],
                    device_id=(cx, cy, c), device_id_type=MESH)
                cp.start()
                firsts.append(cp)
        for a in range(n):
            h = dst[a].shape[2] // 2
            half = pl.ds(c * h, h)
            for j, (cx, cy) in enumerate(chips):
                theirs = dst[a].at[:, 2 * cx + cy, half]
                pltpu.make_async_remote_copy(
                    src_ref=theirs, dst_ref=theirs, send_sem=send.at[6 * a + j], recv_sem=recv.at[6 * a + j],
                    device_id=(cx, cy, c), device_id_type=MESH).wait_recv()
                fw = pltpu.make_async_remote_copy(
                    src_ref=theirs, dst_ref=theirs, send_sem=send.at[6 * a + 3 + j], recv_sem=recv.at[6 * a + 3 + j],
                    device_id=(x, y, 1 - c), device_id_type=MESH)
                fw.start()
                passes.append(fw)
        for a in range(n):
            h = dst[a].shape[2] // 2
            other = pl.ds((1 - c) * h, h)
            for j, (cx, cy) in enumerate(chips):
                got = dst[a].at[:, 2 * cx + cy, other]
                pltpu.make_async_remote_copy(
                    src_ref=got, dst_ref=got, send_sem=send.at[6 * a + 3 + j], recv_sem=recv.at[6 * a + 3 + j],
                    device_id=(x, y, 1 - c), device_id_type=MESH).wait_recv()
        for cp in firsts + passes:
            cp.wait_send()

    return pl.pallas_call(
        kern, name="gather_weights", in_specs=[ANY] * n, out_specs=[ANY] * n,
        out_shape=[_sds(w.shape, w.dtype) for w in w16], input_output_aliases={a: a for a in range(n)},
        scratch_shapes=[pltpu.SemaphoreType.DMA((6 * n,)), pltpu.SemaphoreType.DMA((6 * n,))])(*w16)


def _last_layer_copies(bufs, send, recv, x, y, c, chips):
    mine = 2 * x + y
    sends, arrivals = [], []
    for a, buf in enumerate(bufs):
        last = buf.shape[0] - 1
        h = buf.shape[2] // 2
        own = buf.at[last, mine, pl.ds(c * h, h)]
        for j, (cx, cy) in enumerate(chips):
            for k in range(2):
                sem = 6 * a + 2 * j + k
                sends.append(pltpu.make_async_remote_copy(
                    src_ref=own, dst_ref=own, send_sem=send.at[sem], recv_sem=recv.at[sem],
                    device_id=(cx, cy, c ^ k), device_id_type=MESH))
                theirs = buf.at[last, 2 * cx + cy, pl.ds((c ^ k) * h, h)]
                arrivals.append(pltpu.make_async_remote_copy(
                    src_ref=theirs, dst_ref=theirs, send_sem=send.at[sem], recv_sem=recv.at[sem],
                    device_id=(cx, cy, c ^ k), device_id_type=MESH))
    return sends, arrivals


def _gather_last_start(w16, after):
    n = len(w16)

    def kern(*refs):
        bufs, send, recv, token = refs[:n], refs[n + 1], refs[n + 2], refs[-1]
        x, y, c, chips = _place()
        for cp in _last_layer_copies(bufs, send, recv, x, y, c, chips)[0]:
            cp.start()
        token[...] = jnp.zeros_like(token)

    out = pl.pallas_call(
        kern, name="gather_last_start",
        out_shape=(pltpu.SemaphoreType.DMA((6 * n,)), pltpu.SemaphoreType.DMA((6 * n,)),
                   *[pltpu.HBM(w.shape, w.dtype) for w in w16], _sds((8, 128), F32)),
        in_specs=[HBM] * n + [ANY],
        out_specs=(SEM, SEM, *[HBM] * n, pl.BlockSpec(memory_space=pltpu.VMEM)),
        input_output_aliases={i: 2 + i for i in range(n)},
        compiler_params=pltpu.CompilerParams(has_side_effects=EFFECT),
    )(*[pltpu.with_memory_space_constraint(w, pltpu.HBM) for w in w16], after)
    return out[0], out[1], out[2:2 + n], out[-1]


def _gather_last_wait(send, recv, w16, after):
    n = len(w16)

    def kern(*refs):
        bufs, send_sem, recv_sem = refs[:n], refs[n], refs[n + 1]
        x, y, c, chips = _place()
        sends, arrivals = _last_layer_copies(bufs, send_sem, recv_sem, x, y, c, chips)
        for cp in sends:
            cp.wait_send()
        for cp in arrivals:
            cp.wait_recv()

    return pl.pallas_call(
        kern, name="gather_last_wait", out_shape=tuple(pltpu.HBM(w.shape, w.dtype) for w in w16),
        in_specs=[HBM] * n + [SEM, SEM, ANY], out_specs=tuple([HBM] * n),
        input_output_aliases={i: i for i in range(n)},
        compiler_params=pltpu.CompilerParams(has_side_effects=EFFECT),
    )(*w16, send, recv, after)


def _pair_exchange(name, grads):
    n = len(grads)

    def kern(*refs):
        src, got_o = refs[:n], refs[n:2 * n]
        send, recv = refs[2 * n:]
        x, y, c, _ = _place()
        cps = []
        for a in range(n):
            h = src[a].shape[1] // 2
            cp = pltpu.make_async_remote_copy(
                src_ref=src[a].at[:, pl.ds((1 - c) * h, h)], dst_ref=got_o[a],
                send_sem=send.at[a], recv_sem=recv.at[a], device_id=(x, y, 1 - c), device_id_type=MESH)
            cp.start()
            cps.append(cp)
        for cp in cps:
            cp.wait()

    halves = [_sds((g.shape[0], g.shape[1] // 2, g.shape[2]), g.dtype) for g in grads]
    return pl.pallas_call(
        kern, name=name, in_specs=[ANY] * n, out_specs=[ANY] * n, out_shape=halves,
        scratch_shapes=[pltpu.SemaphoreType.DMA((n,)), pltpu.SemaphoreType.DMA((n,))])(*grads)


def _chip_exchange(name, sums):
    n = len(sums)

    def kern(*refs):
        src, dst = refs[:n], refs[n:2 * n]
        send, recv = refs[2 * n:]
        x, y, c, chips = _place()
        cps = _chip_copies(src, dst, send, recv, x, y, c, chips)
        for cp in cps:
            cp.start()
        for cp in cps:
            cp.wait()

    return pl.pallas_call(
        kern, name=name, in_specs=[ANY] * n, out_specs=[ANY] * n,
        out_shape=[_sds((3,) + s_.shape[1:], s_.dtype) for s_ in sums],
        scratch_shapes=[pltpu.SemaphoreType.DMA((3 * n,)), pltpu.SemaphoreType.DMA((3 * n,))])(*sums)


HBM = pl.BlockSpec(memory_space=pltpu.HBM)
SEM = pl.BlockSpec(memory_space=pltpu.SEMAPHORE)
EFFECT = pltpu.SideEffectType.DATAFLOW_SIDE_EFFECTING


def _chip_copies(src, land, send, recv, x, y, c, chips):
    return [pltpu.make_async_remote_copy(
        src_ref=src[a].at[2 * cx + cy], dst_ref=land[a].at[j], send_sem=send.at[3 * a + j],
        recv_sem=recv.at[3 * a + j], device_id=(cx, cy, c), device_id_type=MESH)
        for a in range(len(src)) for j, (cx, cy) in enumerate(chips)]


def _chip_exchange_start(sums):
    n = len(sums)

    def kern(*refs):
        src, land = refs[:n], refs[n:2 * n]
        send, recv, token = refs[2 * n], refs[2 * n + 1], refs[-1]
        x, y, c, chips = _place()
        for cp in _chip_copies(src, land, send, recv, x, y, c, chips):
            cp.start()
        token[...] = jnp.zeros_like(token)

    lands = [(3,) + s_.shape[1:] for s_ in sums]
    out = pl.pallas_call(
        kern, name="chip_exchange_start",
        out_shape=(pltpu.SemaphoreType.DMA((3 * n,)), pltpu.SemaphoreType.DMA((3 * n,)),
                   *[pltpu.HBM(s_.shape, s_.dtype) for s_ in sums],
                   *[pltpu.HBM(shp, s_.dtype) for shp, s_ in zip(lands, sums)], _sds((8, 128), F32)),
        in_specs=[HBM] * (2 * n),
        out_specs=(SEM, SEM, *[HBM] * (2 * n), pl.BlockSpec(memory_space=pltpu.VMEM)),
        input_output_aliases={i: 2 + i for i in range(2 * n)},
        compiler_params=pltpu.CompilerParams(has_side_effects=EFFECT),
    )(*[pltpu.with_memory_space_constraint(s_, pltpu.HBM) for s_ in sums],
      *[pltpu.with_memory_space_constraint(lax.empty(shp, s_.dtype), pltpu.HBM) for shp, s_ in zip(lands, sums)])
    return out[0], out[1], out[2:2 + n], out[2 + n:2 + 2 * n], out[-1]


def _chip_exchange_wait(send, recv, sums, lands, after):
    n = len(sums)

    def kern(*refs):
        src, land = refs[:n], refs[n:2 * n]
        send_sem, recv_sem = refs[2 * n], refs[2 * n + 1]
        x, y, c, chips = _place()
        for cp in _chip_copies(src, land, send_sem, recv_sem, x, y, c, chips):
            cp.wait_send()
            cp.wait_recv()

    out = pl.pallas_call(
        kern, name="chip_exchange_wait",
        out_shape=tuple(pltpu.HBM(t.shape, t.dtype) for t in list(sums) + list(lands)),
        in_specs=[HBM] * (2 * n) + [SEM, SEM, ANY], out_specs=tuple([HBM] * (2 * n)),
        input_output_aliases={i: i for i in range(2 * n)},
        compiler_params=pltpu.CompilerParams(has_side_effects=EFFECT),
    )(*sums, *lands, send, recv, after)
    return out[:n], out[n:]


def _pair_join(full):
    n = len(full)

    def kern(*refs):
        dst = refs[n:2 * n]
        send, recv = refs[2 * n:]
        x, y, c, _ = _place()
        cps = []
        for a in range(n):
            h = dst[a].shape[1] // 2
            mine = dst[a].at[:, pl.ds(c * h, h)]
            cp = pltpu.make_async_remote_copy(
                src_ref=mine, dst_ref=mine, send_sem=send.at[a], recv_sem=recv.at[a],
                device_id=(x, y, 1 - c), device_id_type=MESH)
            cp.start()
            cps.append(cp)
        for a, cp in enumerate(cps):
            cp.wait_send()
            h = dst[a].shape[1] // 2
            got = dst[a].at[:, pl.ds((1 - c) * h, h)]
            pltpu.make_async_remote_copy(
                src_ref=got, dst_ref=got, send_sem=send.at[a], recv_sem=recv.at[a],
                device_id=(x, y, 1 - c), device_id_type=MESH).wait_recv()

    return pl.pallas_call(
        kern, name="pair_join", in_specs=[ANY] * n, out_specs=[ANY] * n,
        out_shape=[_sds(f.shape, f.dtype) for f in full], input_output_aliases={a: a for a in range(n)},
        scratch_shapes=[pltpu.SemaphoreType.DMA((n,)), pltpu.SemaphoreType.DMA((n,))])(*full)


def _all_sum(name, v):
    r = v.shape[0]

    def kern(v_ref, o_ref, buf, send, recv):
        x, y, c, _ = _place()
        me = 4 * x + 2 * y + c
        buf[me] = v_ref[...]
        cps = []
        for k in range(1, 8):
            peer = (x ^ (k >> 2), y ^ ((k >> 1) & 1), c ^ (k & 1))
            cp = pltpu.make_async_remote_copy(
                src_ref=v_ref, dst_ref=buf.at[me], send_sem=send.at[k - 1], recv_sem=recv.at[k - 1],
                device_id=peer, device_id_type=MESH)
            cp.start()
            cps.append(cp)
        for k in range(1, 8):
            peer_id = me ^ k
            pltpu.make_async_remote_copy(
                src_ref=v_ref, dst_ref=buf.at[peer_id], send_sem=send.at[k - 1], recv_sem=recv.at[k - 1],
                device_id=(x, y, c), device_id_type=MESH).wait_recv()
        for cp in cps:
            cp.wait_send()
        acc = buf[0]
        for d in range(1, 8):
            acc = acc + buf[d]
        o_ref[...] = acc

    vm = pl.BlockSpec(memory_space=pltpu.VMEM)
    return pl.pallas_call(
        kern, name=name, in_specs=[vm], out_specs=vm, out_shape=_sds((r, 128), F32),
        scratch_shapes=[pltpu.VMEM((8, r, 128), F32), pltpu.SemaphoreType.DMA((7,)),
                        pltpu.SemaphoreType.DMA((7,))])(v)


def _my_chip():
    return 2 * lax.axis_index("x") + lax.axis_index("y")


def _my_core():
    return lax.axis_index("c")


def _cast_place(name, w, first, count):
    _, r, c = w.shape
    tr = r // 4
    return _rw(name, lambda wb: ((wb,), ()), (count, r // tr), [w],
               [pl.BlockSpec((None, tr, c), lambda j, i: (first + j, i, 0))],
               [_sds((count, N_SHARD, r, c), BF16)],
               [pl.BlockSpec((None, None, tr, c), lambda j, i: (j, _my_chip(), i, 0))])[0]


HALF_STEPS = 2


def _add_halves(name, g, got):
    n, h, c = got.shape
    tr, nt = h // HALF_STEPS, HALF_STEPS
    return _rw(name, lambda ab, bb: ((ab.astype(F32) + bb.astype(F32),), ()), (nt,), [g, got],
               [pl.BlockSpec((n, tr, c), lambda i: (0, _my_core() * nt + i, 0)),
                pl.BlockSpec((n, tr, c), lambda i: (0, i, 0))],
               [_sds((n, h, c), BF16)], [pl.BlockSpec((n, tr, c), lambda i: (0, i, 0))])[0]


def _sum_parts(name, sums, parts, full, layer, n_layer):
    _, h, c = sums.shape
    tr, nt = h // HALF_STEPS, HALF_STEPS

    def body(own, pb):
        acc = own.astype(F32)
        for j in range(pb.shape[0]):
            acc = acc + pb[j].astype(F32)
        return (acc,), ()

    ins = [sums, parts] + ([full] if full is not None else [])
    in_specs = [pl.BlockSpec((None, tr, c), lambda i: (_my_chip(), i, 0)),
                pl.BlockSpec((parts.shape[0], tr, c), lambda i: (0, i, 0))] + ([ANY] if full is not None else [])
    return _rw(name, body, (nt,), ins, in_specs, [_sds((n_layer, 2 * h, c), F32)],
               [pl.BlockSpec((None, tr, c), lambda i: (layer, _my_core() * nt + i, 0))],
               aliases={2: 0} if full is not None else None)[0]


def _adamw_math(w, g, m, v):
    m = B1 * m + (1.0 - B1) * g
    v = B2 * v + (1.0 - B2) * (g * g)
    m_hat = m / (1.0 - B1 ** STEP)
    v_hat = v / (1.0 - B2 ** STEP)
    delta = -LR * (m_hat / (jnp.sqrt(v_hat) + ADAM_EPS) + WD * w)
    return delta, m, v


def _adamw(name, w, g, m, v):
    r, c = w.shape
    tr = 128 if r % 128 == 0 else 8
    spec = _row_spec(tr, c)
    return _rw(name, lambda *b: (_adamw_math(*b), ()), (r // tr,), [w, g, m, v], [spec] * 4,
               [_sds((r, c), F32)] * 3, [spec] * 3)


SMALL = (("ffn1_norm", (DEPTH, D_MODEL)), ("mix_norm", (DEPTH, D_MODEL)), ("ffn2_norm", (DEPTH, D_MODEL)),
         ("conv_b", (DEPTH, SLAB)), ("conv_ln_g", (DEPTH, SLAB)), ("conv_ln_b", (DEPTH, SLAB)),
         ("ret_norm_g", (DEPTH, SLAB)), ("final_norm", (D_MODEL,)), ("conv_w", (DEPTH, CONV_W, SLAB)))


def _pack(parts, rows):
    flat = jnp.concatenate([p.reshape(-1) for p in parts])
    return jnp.pad(flat, (0, rows * 128 - flat.shape[0])).reshape(rows, 128)


def _unpack(packed, shapes):
    flat = packed.reshape(-1)
    out, off = [], 0
    for shp in shapes:
        n = int(np.prod(shp))
        out.append(flat[off:off + n].reshape(shp))
        off += n
    return out


def kernel(x, ffn1_norm, ffn1_w_in, ffn1_w_out, mix_norm, mix_w_in, conv_w, conv_b, conv_ln_g, conv_ln_b, ret_norm_g, mix_w_out, ffn2_norm, ffn2_w_in, ffn2_w_out, final_norm, loss_target, m_ffn1_norm, m_ffn1_w_in, m_ffn1_w_out, m_mix_norm, m_mix_w_in, m_conv_w, m_conv_b, m_conv_ln_g, m_conv_ln_b, m_ret_norm_g, m_mix_w_out, m_ffn2_norm, m_ffn2_w_in, m_ffn2_w_out, m_final_norm, v_ffn1_norm, v_ffn1_w_in, v_ffn1_w_out, v_mix_norm, v_mix_w_in, v_conv_w, v_conv_b, v_conv_ln_g, v_conv_ln_b, v_ret_norm_g, v_mix_w_out, v_ffn2_norm, v_ffn2_w_in, v_ffn2_w_out, v_final_norm):
    given = dict(locals())
    chip = 2 * lax.axis_index("x") + lax.axis_index("y")
    core = lax.axis_index("c")

    cw_rows = 128
    placed = lax.dynamic_update_slice(jnp.zeros((DEPTH, CONV_W, SLAB), F32), conv_w, (0, 0, chip * HEAD))
    placed = placed * (core == 0).astype(F32)
    conv_w_full = _unpack(_all_sum("gather_conv_w", _pack([placed], cw_rows)), [(DEPTH, CONV_W, SLAB)])[0]

    early = dict(zip(BIG, _gather_weights([_cast_place("cast_" + n, given[n], 0, DEPTH - 1) for n in BIG])))
    w_send, w_recv, arriving, started = _gather_last_start(
        [_cast_place("cast_last_" + n, given[n], DEPTH - 1, 1) for n in BIG], early[BIG[0]])
    sm = {n: given[n] for n, _ in SMALL}
    sm["conv_w"] = conv_w_full
    sm["ffn1_norm"] = ffn1_norm + started[0:1, 0:1]

    def last_weights(activations):
        return dict(zip(BIG, _gather_last_wait(w_send, w_recv, arriving, activations)))

    def chip_sums(tag, layer_grads):
        grads = [layer_grads[n] for n in BIG]
        theirs = _pair_exchange("pair_exchange_" + tag, grads)
        return [_add_halves(f"chipsum_{tag}{i}", a, b) for i, (a, b) in enumerate(zip(grads, theirs))]

    in_flight = []

    def start_last(layer_grads):
        in_flight.extend(_chip_exchange_start(chip_sums("last", layer_grads)))
        return in_flight[4]

    loss, dx, big, small, dfinal = _local_step(x[0], loss_target[0], early, last_weights, sm, start_last)
    sums, parts = [None] * DEPTH, [None] * DEPTH
    sums[DEPTH - 1], parts[DEPTH - 1] = _chip_exchange_wait(*in_flight[:4], dx)
    for l in range(DEPTH - 1):
        sums[l] = chip_sums(f"l{l}", big[l])
        parts[l] = _chip_exchange(f"chip_exchange_l{l}", sums[l])
    full = []
    for i in range(len(BIG)):
        f = None
        for l in range(DEPTH):
            f = _sum_parts(f"shardsum{DEPTH * i + l}", sums[l][i], parts[l][i], f, l, DEPTH)
        full.append(f)
    g_big = dict(zip(BIG, _pair_join(full)))

    small_parts = []
    for n, shp in SMALL:
        if n == "final_norm":
            small_parts.append(dfinal)
        else:
            small_parts.append(jnp.stack([small[l][n].reshape(shp[1:]) for l in range(DEPTH)]))
    g_small = dict(zip([n for n, _ in SMALL], _unpack(_all_sum("sum_small", _pack(small_parts, 200)), [s_ for _, s_ in SMALL])))
    g_small["conv_w"] = lax.dynamic_slice(g_small["conv_w"], (0, 0, chip * HEAD), (DEPTH, CONV_W, HEAD))

    grad, delta, new_m, new_v = dict(g_small), {}, {}, {}
    grad.update(g_big)
    for n in BIG:
        l, r, c = given[n].shape
        f = lambda t: t.reshape(l * r, c)
        d_, m_, v_ = _adamw("adamw_" + n, f(given[n]), f(grad[n]), f(given["m_" + n]), f(given["v_" + n]))
        delta[n], new_m[n], new_v[n] = d_.reshape(l, r, c), m_.reshape(l, r, c), v_.reshape(l, r, c)
    snames = [n for n, _ in SMALL]
    shapes = [given[n].shape for n in snames]
    rows = 104
    d_, m_, v_ = _adamw("adamw_small", _pack([given[n] for n in snames], rows), _pack([grad[n] for n in snames], rows),
                        _pack([given["m_" + n] for n in snames], rows), _pack([given["v_" + n] for n in snames], rows))
    for dst, packed in ((delta, d_), (new_m, m_), (new_v, v_)):
        dst.update(zip(snames, _unpack(packed, shapes)))

    total = lax.psum(loss[0, 0], ("x", "y", "c"))
    order = ["ffn1_norm", "ffn1_w_in", "ffn1_w_out", "mix_norm", "mix_w_in", "conv_w", "conv_b", "conv_ln_g",
             "conv_ln_b", "ret_norm_g", "mix_w_out", "ffn2_norm", "ffn2_w_in", "ffn2_w_out", "final_norm"]
    return (total, dx[None], *[grad[n] for n in order], *[delta[n] for n in order],
            *[new_m[n] for n in order], *[new_v[n] for n in order])
```

```python
import functools

import numpy as np
import jax
import jax.numpy as jnp
from jax import lax
from jax.experimental import pallas as pl
from jax.experimental.pallas import tpu as pltpu

F32 = jnp.float32
BF16 = jnp.bfloat16

D_MODEL = 1024
D_FF = 2816
N_SHARD = 4
FF_SHARD = 2 * D_FF // N_SHARD
MIX_SHARD = 3072 // N_SHARD
HEAD = 64
SLAB = 256
CONV_W = 31
CONV_PAD = 32
CHUNK = 64
EPS = 1e-6
ROPE_BASE = 10000.0
DEPTH = 2

LR, B1, B2, ADAM_EPS, WD, STEP = 0.001, 0.9, 0.999, 1e-08, 0.01, 10

VMEM_LIMIT = 56 * 1024 * 1024


def _params(n_grid):
    return pltpu.CompilerParams(dimension_semantics=("arbitrary",) * n_grid, vmem_limit_bytes=VMEM_LIMIT)


def _rw(name, body, grid, ins, in_specs, rows=(), row_specs=(), accs=(), acc_specs=(), aliases=None, after=()):
    n_in, n_row = len(ins), len(rows)
    carried = sorted(set(aliases or {}) | set(after))

    def kern(*refs):
        vals = [r[...] for i, r in enumerate(refs[:n_in]) if i not in carried]
        row_vals, acc_vals = body(*vals)
        for r, v in zip(refs[n_in:n_in + n_row], row_vals):
            r[...] = v.astype(r.dtype)
        acc_refs = refs[n_in + n_row:]
        if acc_refs:
            first = functools.reduce(jnp.logical_and, [pl.program_id(a) == 0 for a in range(len(grid))])

            @pl.when(first)
            def _():
                for r in acc_refs:
                    r[...] = jnp.zeros(r.shape, r.dtype)

            for r, v in zip(acc_refs, acc_vals):
                r[...] += v.astype(r.dtype)

    return pl.pallas_call(
        kern, name=name, grid=grid, in_specs=list(in_specs), out_specs=list(row_specs) + list(acc_specs),
        out_shape=list(rows) + list(accs), input_output_aliases=dict(aliases or {}),
        compiler_params=_params(len(grid)))(*ins)


def _sds(shape, dtype):
    return jax.ShapeDtypeStruct(shape, dtype)


def _rms(x, g):
    return x * lax.rsqrt(jnp.mean(x * x, axis=-1, keepdims=True) + EPS) * g


def _row_spec(tm, c):
    return pl.BlockSpec((tm, c), lambda i: (i, 0))


def _vec_spec(c):
    return pl.BlockSpec((1, c), lambda i: (0, 0))


def _swiglu(gate, up):
    return jax.nn.silu(gate) * up


def _swiglu_fwd(name, u, tm):
    _, s, c = u.shape
    return _rw(name, lambda ub: ((_swiglu(ub[0:2].astype(F32), ub[2:4].astype(F32)),), ()), (s // tm,), [u],
               [pl.BlockSpec((4, tm, c), lambda i: (0, i, 0))],
               [_sds((2, s, c), BF16)], [pl.BlockSpec((2, tm, c), lambda i: (0, i, 0))])[0]


def _swiglu_bwd(name, u, da, tm):
    _, s, c = u.shape

    def body(ub, dab):
        _, vjp = jax.vjp(_swiglu, ub[0:2].astype(F32), ub[2:4].astype(F32))
        dg, du = vjp(dab.astype(F32))
        return (jnp.concatenate([dg, du], axis=0),), ()

    return _rw(name, body, (s // tm,), [u, da],
               [pl.BlockSpec((4, tm, c), lambda i: (0, i, 0)), pl.BlockSpec((2, tm, c), lambda i: (0, i, 0))],
               [_sds((4, s, c), BF16)], [pl.BlockSpec((4, tm, c), lambda i: (0, i, 0))])[0]


def _ln_silu(y, g, b):
    mu = jnp.mean(y, axis=-1, keepdims=True)
    yc = y - mu
    var = jnp.mean(yc * yc, axis=-1, keepdims=True)
    return jax.nn.silu(yc * lax.rsqrt(var + EPS) * g + b)


def _ln_silu_fwd(name, y, g, b, tm):
    s, c = y.shape
    return _rw(name, lambda yb, gb, bb: ((_ln_silu(yb, gb, bb),), ()), (s // tm,), [y, g, b],
               [_row_spec(tm, c), _vec_spec(c), _vec_spec(c)], [_sds((s, c), BF16)], [_row_spec(tm, c)])[0]


def _ln_silu_bwd(name, y, dcat, g, b, tm):
    s, c = y.shape

    def body(yb, dob, gb, bb):
        _, vjp = jax.vjp(_ln_silu, yb, gb, bb)
        dy, dg, db = vjp(dob)
        return (dy,), (dg, db)

    return _rw(name, body, (s // tm,), [y, dcat, g, b],
               [_row_spec(tm, c), pl.BlockSpec((None, tm, c), lambda i: (0, i, 0)), _vec_spec(c), _vec_spec(c)],
               [_sds((s, c), F32)], [_row_spec(tm, c)],
               [_sds((1, c), F32)] * 2, [_vec_spec(c)] * 2)


def _head_masks():
    lane = np.arange(SLAB) // HEAD
    m = np.zeros((8, SLAB), np.float32)
    for h in range(SLAB // HEAD):
        m[h] = (lane == h)
    return jnp.asarray(m)


def _gated_head_norm(y, gate, g, hm):
    mu = jnp.zeros_like(y)
    for h in range(SLAB // HEAD):
        mu = mu + hm[h:h + 1] * (jnp.sum(y * hm[h:h + 1], axis=-1, keepdims=True) / HEAD)
    yc = y - mu
    var = jnp.zeros_like(y)
    for h in range(SLAB // HEAD):
        var = var + hm[h:h + 1] * (jnp.sum(yc * yc * hm[h:h + 1], axis=-1, keepdims=True) / HEAD)
    return jax.nn.silu(gate) * (yc * lax.rsqrt(var + EPS) * g)


PER_SHARD = MIX_SHARD // SLAB


def _slab_spec(tm, j):
    return pl.BlockSpec((None, tm, SLAB), lambda i: (j, i, 0))


def _proj_slab_spec(tm, j):
    return pl.BlockSpec((None, tm, SLAB), lambda i: (j // PER_SHARD, i, j % PER_SHARD))


def _ghn_fwd(name, y, p32, g, tm):
    s, c = y.shape
    hm = _head_masks()
    return _rw(name, lambda yb, gb, wb, hb: ((_gated_head_norm(yb, gb, wb, hb),), ()), (s // tm,),
               [y, p32, g, hm],
               [_row_spec(tm, c), _proj_slab_spec(tm, 11), _vec_spec(c), pl.BlockSpec((8, c), lambda i: (0, 0))],
               [_sds((s, c), BF16)], [_row_spec(tm, c)])[0]


def _ghn_bwd(name, y, p32, dcat, g, tm):
    s, c = y.shape
    hm = _head_masks()

    def body(yb, gb, dob, wb, hb):
        _, vjp = jax.vjp(lambda a, b_, c_: _gated_head_norm(a, b_, c_, hb), yb, gb, wb)
        dy, dgate, dw = vjp(dob)
        return (dy, dgate), (dw,)

    return _rw(name, body, (s // tm,), [y, p32, dcat, g, hm],
               [_row_spec(tm, c), _proj_slab_spec(tm, 11), _slab_spec(tm, 3), _vec_spec(c),
                pl.BlockSpec((8, c), lambda i: (0, 0))],
               [_sds((s, c), F32)] * 2, [_row_spec(tm, c)] * 2,
               [_sds((1, c), F32)], [_vec_spec(c)])


def _assemble(name, parts, per, tm):
    s = parts[0][0].shape[-2]
    specs = [_row_spec(tm, SLAB) if j is None else pl.BlockSpec((None, tm, SLAB), lambda i, j=j: (j, i, 0))
             for _, j in parts]

    def body(*blocks):
        rows = [jnp.concatenate([b.astype(BF16) for b in blocks[per * q:per * (q + 1)]], axis=-1)
                for q in range(len(blocks) // per)]
        return (jnp.stack(rows),), ()

    nq = len(parts) // per
    return _rw(name, body, (s // tm,), [a for a, _ in parts], specs, [_sds((nq, s, per * SLAB), BF16)],
               [pl.BlockSpec((nq, tm, per * SLAB), lambda i: (0, i, 0))])[0]


def _final(name, x, tgt, g, tm):
    s, d = x.shape

    def body(xb, tb, gb):
        yf, vjp = jax.vjp(_rms, xb, gb)
        err = yf - tb
        dx, dg = vjp(err * (1.0 / d))
        part = 0.5 * jnp.sum(jnp.mean(err * err, axis=-1, keepdims=True), axis=0, keepdims=True)
        return (dx,), (dg, jnp.broadcast_to(part, (1, 128)))

    return _rw(name, body, (s // tm,), [x, tgt, g],
               [_row_spec(tm, d), _row_spec(tm, d), _vec_spec(d)],
               [_sds((s, d), F32)], [_row_spec(tm, d)],
               [_sds((1, d), F32), _sds((1, 128), F32)], [_vec_spec(d), _vec_spec(128)])


NN = (((1,), (0,)), ((), ()))
NT = (((1,), (1,)), ((), ()))
TN = (((0,), (0,)), ((), ()))


def _mm(name, a, b, grid, a_spec, b_spec, outs, out_specs, acc_shape, dims, alpha=1.0):
    nk = grid[-1]
    n_out = len(outs)

    def kern(*refs):
        a_ref, b_ref = refs[0], refs[1]
        o_refs = refs[2:2 + n_out]
        part = lax.dot_general(a_ref[...].astype(BF16), b_ref[...].astype(BF16), dims,
                               preferred_element_type=F32)

        def finish(r):
            if alpha != 1.0:
                r = r * alpha
            for o in o_refs:
                o[...] = r.astype(o.dtype)

        if nk == 1:
            finish(part)
            return
        acc_ref = refs[-1]
        k = pl.program_id(len(grid) - 1)

        @pl.when(k == 0)
        def _():
            acc_ref[...] = part

        @pl.when(jnp.logical_and(k > 0, k < nk - 1))
        def _():
            acc_ref[...] += part

        @pl.when(k == nk - 1)
        def _():
            finish(acc_ref[...] + part)

    return pl.pallas_call(
        kern, name=name, grid=grid, in_specs=[a_spec, b_spec], out_specs=list(out_specs), out_shape=list(outs),
        scratch_shapes=[pltpu.VMEM(acc_shape, F32)] if nk > 1 else [],
        compiler_params=_params(len(grid)))(a, b)


def _norm_proj_in(name, x, g, w, l, tm, dtypes):
    s, d = x.shape
    n = w.shape[-1]
    n_out = len(dtypes)

    def kern(x_ref, g_ref, w_ref, h_ref, *rest):
        o_refs, h_vmem = rest[:n_out], rest[n_out]

        @pl.when(pl.program_id(1) == 0)
        def _():
            h = _rms(x_ref[...], g_ref[...]).astype(BF16)
            h_vmem[...] = h
            h_ref[...] = h

        r = jnp.dot(h_vmem[...], w_ref[...], preferred_element_type=F32)
        for o in o_refs:
            o[...] = r.astype(o.dtype)

    out = pl.pallas_call(
        kern, name=name, grid=(s // tm, N_SHARD),
        in_specs=[pl.BlockSpec((tm, d), lambda i, b: (i, 0)), pl.BlockSpec((1, d), lambda i, b: (0, 0)),
                  pl.BlockSpec((None, None, d, n), lambda i, b: (l, b, 0, 0))],
        out_specs=[pl.BlockSpec((tm, d), lambda i, b: (i, 0))] +
                  [pl.BlockSpec((None, tm, n), lambda i, b: (b, i, 0))] * n_out,
        out_shape=[_sds((s, d), BF16)] + [_sds((N_SHARD, s, n), t) for t in dtypes],
        scratch_shapes=[pltpu.VMEM((tm, d), BF16)], compiler_params=_params(2))(x, g, w)
    return out[0], out[1:]


def _back_in_norm(name, du, w, l, tm, x, dres, g):
    nk, s, n = du.shape
    d = w.shape[2]

    def kern(du_ref, w_ref, x_ref, dres_ref, g_ref, dx_ref, dg_ref):
        dh = lax.dot_general(du_ref[0], w_ref[0], NT, preferred_element_type=F32)
        for k in range(1, nk):
            dh = dh + lax.dot_general(du_ref[k], w_ref[k], NT, preferred_element_type=F32)
        _, vjp = jax.vjp(_rms, x_ref[...], g_ref[...])
        dx, dg = vjp(dh)
        dx_ref[...] = dx + dres_ref[...]

        @pl.when(pl.program_id(0) == 0)
        def _():
            dg_ref[...] = dg

        @pl.when(pl.program_id(0) > 0)
        def _():
            dg_ref[...] += dg

    row = pl.BlockSpec((tm, d), lambda i: (i, 0))
    vec = pl.BlockSpec((1, d), lambda i: (0, 0))
    return pl.pallas_call(
        kern, name=name, grid=(s // tm,),
        in_specs=[pl.BlockSpec((nk, tm, n), lambda i: (0, i, 0)),
                  pl.BlockSpec((None, nk, d, n), lambda i: (l, 0, 0, 0)), row, row, vec],
        out_specs=[row, vec], out_shape=[_sds((s, d), F32), _sds((1, d), F32)],
        compiler_params=_params(1))(du, w, x, dres, g)


def _proj_out(name, a, w, l, res, alpha, tm):
    nk, s, r = a.shape
    d = w.shape[-1]

    def kern(a_ref, w_ref, res_ref, o_ref):
        y = jnp.dot(a_ref[0], w_ref[0], preferred_element_type=F32)
        for k in range(1, nk):
            y = y + jnp.dot(a_ref[k], w_ref[k], preferred_element_type=F32)
        o_ref[...] = res_ref[...] + (y * alpha if alpha != 1.0 else y)

    row = pl.BlockSpec((tm, d), lambda i: (i, 0))
    return pl.pallas_call(
        kern, name=name, grid=(s // tm,),
        in_specs=[pl.BlockSpec((nk, tm, r), lambda i: (0, i, 0)),
                  pl.BlockSpec((None, nk, r, d), lambda i: (l, 0, 0, 0)), row],
        out_specs=row, out_shape=_sds((s, d), F32), compiler_params=_params(1))(a, w, res)


def _back_out(name, dy, w, l, alpha, tm, out_dtype):
    s, d = dy.shape
    nk, r = w.shape[1], w.shape[2]
    return _mm(name, dy, w, (nk, s // tm, 1),
               pl.BlockSpec((tm, d), lambda b, i, k: (i, 0)),
               pl.BlockSpec((None, None, r, d), lambda b, i, k: (l, b, 0, 0)),
               [_sds((nk, s, r), out_dtype)], [pl.BlockSpec((None, tm, r), lambda b, i, k: (b, i, 0))],
               (tm, r), NT, alpha=alpha)[0]


def _grad_in(name, h, du, ts):
    s, d = h.shape
    nb, _, n = du.shape
    return _mm(name, h, du, (nb, 1, s // ts),
               pl.BlockSpec((ts, d), lambda b, i, k: (k, 0)),
               pl.BlockSpec((None, ts, n), lambda b, i, k: (b, k, 0)),
               [_sds((nb, d, n), BF16)], [pl.BlockSpec((None, d, n), lambda b, i, k: (b, 0, 0))],
               (d, n), TN)[0]


def _grad_out(name, a, dy, alpha, ts):
    nb, s, r = a.shape
    d = dy.shape[1]
    return _mm(name, a, dy, (nb, 1, s // ts),
               pl.BlockSpec((None, ts, r), lambda b, i, k: (b, k, 0)),
               pl.BlockSpec((ts, d), lambda b, i, k: (k, 0)),
               [_sds((nb, r, d), BF16)], [pl.BlockSpec((None, r, d), lambda b, i, k: (b, 0, 0))],
               (r, d), TN, alpha=alpha)[0]


CONV_TILE = 256


def _shifted(win, off, rows):
    n = win.shape[0]
    return pltpu.roll(win, (n - off) % n, 0)[0:rows] if off % n else win[0:rows]


def _conv_fwd(name, p32, w, bias):
    s = p32.shape[1]
    cb = 128
    nt = s // CONV_TILE

    def kern(a_ref, b_ref, w_ref, bias_ref, y_ref, vpad):
        vpad[0:CONV_PAD, :] = jnp.zeros((CONV_PAD, cb), F32)

        def fill(i, c):
            r = pl.multiple_of(i * CONV_TILE, CONV_TILE)
            vpad[pl.ds(CONV_PAD + r, CONV_TILE), :] = (
                a_ref[pl.ds(r, CONV_TILE), :] * jax.nn.sigmoid(b_ref[pl.ds(r, CONV_TILE), :]))
            return c

        lax.fori_loop(0, nt, fill, 0)

        def tile(i, c):
            r = pl.multiple_of(i * CONV_TILE, CONV_TILE)
            win = vpad[pl.ds(r, CONV_TILE + CONV_PAD), :]
            acc = jnp.broadcast_to(bias_ref[...], (CONV_TILE, cb))
            for j in range(CONV_W):
                acc = acc + w_ref[j:j + 1, :] * _shifted(win, j + 2, CONV_TILE)
            y_ref[pl.ds(r, CONV_TILE), :] = acc
            return c

        lax.fori_loop(0, nt, tile, 0)

    return pl.pallas_call(
        kern, name=name, grid=(SLAB // cb,),
        in_specs=[pl.BlockSpec((None, s, cb), lambda c: (0, 0, c)),
                  pl.BlockSpec((None, s, cb), lambda c: (0, 0, SLAB // cb + c)),
                  pl.BlockSpec((CONV_W, cb), lambda c: (0, c)),
                  pl.BlockSpec((1, cb), lambda c: (0, c))],
        out_specs=pl.BlockSpec((s, cb), lambda c: (0, c)),
        out_shape=_sds((s, SLAB), F32),
        scratch_shapes=[pltpu.VMEM((s + CONV_PAD, cb), F32)],
        compiler_params=_params(1))(p32, p32, w, bias)


def _conv_bwd(name, p32, w, dy):
    s = p32.shape[1]
    cb = 128
    nt = s // CONV_TILE

    def kern(a_ref, b_ref, w_ref, dy_ref, da_ref, db_ref, dw_ref, dbias_ref, vpad, dpad):
        vpad[0:CONV_PAD, :] = jnp.zeros((CONV_PAD, cb), F32)
        dpad[s:s + CONV_PAD, :] = jnp.zeros((CONV_PAD, cb), F32)
        dw_ref[...] = jnp.zeros((CONV_PAD, cb), F32)
        dbias_ref[...] = jnp.zeros((1, cb), F32)

        def fill(i, c):
            r = pl.multiple_of(i * CONV_TILE, CONV_TILE)
            vpad[pl.ds(CONV_PAD + r, CONV_TILE), :] = (
                a_ref[pl.ds(r, CONV_TILE), :] * jax.nn.sigmoid(b_ref[pl.ds(r, CONV_TILE), :]))
            dpad[pl.ds(r, CONV_TILE), :] = dy_ref[pl.ds(r, CONV_TILE), :]
            return c

        lax.fori_loop(0, nt, fill, 0)

        def tile(i, c):
            r = pl.multiple_of(i * CONV_TILE, CONV_TILE)
            dwin = dpad[pl.ds(r, CONV_TILE + CONV_PAD), :]
            vwin = vpad[pl.ds(r, CONV_TILE + CONV_PAD), :]
            dyt = dwin[0:CONV_TILE]
            dv = jnp.zeros((CONV_TILE, cb), F32)
            for j in range(CONV_W):
                dv = dv + w_ref[j:j + 1, :] * _shifted(dwin, CONV_W - 1 - j, CONV_TILE)
                dw_ref[j:j + 1, :] += jnp.sum(dyt * _shifted(vwin, j + 2, CONV_TILE), axis=0, keepdims=True)
            dbias_ref[...] += jnp.sum(dyt, axis=0, keepdims=True)
            a = a_ref[pl.ds(r, CONV_TILE), :]
            sg = jax.nn.sigmoid(b_ref[pl.ds(r, CONV_TILE), :])
            da_ref[pl.ds(r, CONV_TILE), :] = dv * sg
            db_ref[pl.ds(r, CONV_TILE), :] = dv * a * sg * (1.0 - sg)
            return c

        lax.fori_loop(0, nt, tile, 0)

    col = pl.BlockSpec((s, cb), lambda c: (0, c))
    return pl.pallas_call(
        kern, name=name, grid=(SLAB // cb,),
        in_specs=[pl.BlockSpec((None, s, cb), lambda c: (0, 0, c)),
                  pl.BlockSpec((None, s, cb), lambda c: (0, 0, SLAB // cb + c)),
                  pl.BlockSpec((CONV_W, cb), lambda c: (0, c)), col],
        out_specs=[col, col, pl.BlockSpec((CONV_PAD, cb), lambda c: (0, c)), pl.BlockSpec((1, cb), lambda c: (0, c))],
        out_shape=[_sds((s, SLAB), F32), _sds((s, SLAB), F32), _sds((CONV_PAD, SLAB), F32), _sds((1, SLAB), F32)],
        scratch_shapes=[pltpu.VMEM((s + CONV_PAD, cb), F32), pltpu.VMEM((s + CONV_PAD, cb), F32)],
        compiler_params=_params(1))(p32, p32, w, dy)


SB_BLOCK = 256
N_HEAD = SLAB // HEAD


def _sb_logits(qm, k, tri):
    z = lax.dot_general(qm, k, NT, preferred_element_type=F32)
    sign_bit = jnp.uint32(0x80000000)
    neg_abs = lax.bitcast_convert_type(lax.bitcast_convert_type(z, jnp.uint32) | sign_bit, F32)
    lb = jnp.minimum(z, 0.0) - jnp.log(1.0 + jnp.exp(neg_abs))
    ln = lb - z
    if tri is not None:
        ln = jnp.where(tri, ln, 0.0)
    return lb, ln


def _first_col(x):
    return jnp.broadcast_to(x[:, 0:1], (x.shape[0], 128))


def _head_stack(dst, x, lane_head, bq):
    for h in range(N_HEAD):
        dst[h * bq:(h + 1) * bq, :] = jnp.where(lane_head == h, x, jnp.zeros_like(x))


def _sb_fwd(name, p16):
    s = p16.shape[1]
    bq = min(SB_BLOCK, s)
    nq = s // bq

    def kern(q_ref, k_ref, v_ref, o_ref, w_hbm, lb_hbm, qm_ref, v4_refs, w4_refs, ws_ref, lbs_ref, acc_ref, r_ref,
             sem):
        g, qi = pl.program_id(0), pl.program_id(1)
        lane_head = lax.broadcasted_iota(jnp.int32, (1, SLAB), 1) // HEAD
        _head_stack(qm_ref, (q_ref[...].astype(F32) * (HEAD ** -0.5)).astype(BF16), lane_head, bq)
        row = lax.broadcasted_iota(jnp.int32, (bq, bq), 0)
        col = lax.broadcasted_iota(jnp.int32, (bq, bq), 1)
        after = (row > col).astype(BF16)
        tri = col < row
        acc_ref[...] = jnp.zeros((bq, SLAB), F32)
        r_ref[...] = jnp.zeros((N_HEAD, bq, 128), F32)

        def saves(slot, kb):
            return (pltpu.make_async_copy(ws_ref.at[slot], w_hbm.at[g, qi, kb], sem.at[0, slot]),
                    pltpu.make_async_copy(lbs_ref.at[slot], lb_hbm.at[g, qi, kb], sem.at[1, slot]))

        def tile(i, masked, u):
            kb, slot = qi - i, i % 4
            v4_ref, w4_ref = v4_refs.at[u], w4_refs.at[u]
            rows = pl.ds(pl.multiple_of(kb * bq, bq), bq)
            k = k_ref[rows, :]
            _head_stack(v4_ref, v_ref[rows, :], lane_head, bq)
            for h in range(N_HEAD):
                mine = pl.ds(h * bq, bq)
                lb, ln = _sb_logits(qm_ref[h * bq:(h + 1) * bq, :], k, tri if masked else None)
                rem = jnp.dot(ln.astype(BF16), after, preferred_element_type=F32)
                w = jnp.exp(lb + rem + r_ref[h][:, 0:1])
                if masked:
                    w = jnp.where(tri, w, 0.0)
                wb = w.astype(BF16)
                w4_ref[:, h * bq:(h + 1) * bq] = wb
                ws_ref[slot, mine, :] = wb
                lbs_ref[slot, mine, :] = lb.astype(BF16)
                r_ref[h] += _first_col(rem[:, 0:128] + ln[:, 0:128])
            acc_ref[...] += jnp.dot(w4_ref[...], v4_ref[...], preferred_element_type=F32)

        def save(i, start):
            for cp in saves(i % 4, qi - i):
                cp.start() if start else cp.wait()

        @pl.when(qi == 0)
        def _():
            tile(0, True, 0)
            save(0, True)
            save(0, False)

        @pl.when(qi >= 1)
        def _():
            tile(0, True, 0)
            tile(1, False, 1)
            save(0, True)
            save(1, True)

        def pair(j, c):
            tile(2 * j, False, 0)
            tile(2 * j + 1, False, 1)
            save(2 * j - 2, False)
            save(2 * j - 1, False)
            save(2 * j, True)
            save(2 * j + 1, True)
            return c

        n_pair = (qi + 1) // 2
        lax.fori_loop(1, n_pair, pair, 0)

        @pl.when(jnp.logical_and(qi >= 1, qi % 2 == 1))
        def _():
            save(qi - 1, False)
            save(qi, False)

        @pl.when(jnp.logical_and(qi >= 2, qi % 2 == 0))
        def _():
            tile(qi, False, 0)
            save(qi - 2, False)
            save(qi - 1, False)
            save(qi, True)
            save(qi, False)

        o_ref[...] = acc_ref[...]

    saved = _sds((2, nq, nq, N_HEAD * bq, bq), BF16)
    return pl.pallas_call(
        kern, name=name, grid=(2, nq),
        in_specs=[pl.BlockSpec((None, bq, SLAB), lambda g, i: ((2 + g) // PER_SHARD, i, (2 + g) % PER_SHARD)),
                  pl.BlockSpec((None, s, SLAB), lambda g, i: ((4 + g) // PER_SHARD, 0, (4 + g) % PER_SHARD)),
                  pl.BlockSpec((None, s, SLAB), lambda g, i: ((6 + g) // PER_SHARD, 0, (6 + g) % PER_SHARD))],
        out_specs=[pl.BlockSpec((None, bq, SLAB), lambda g, i: (g, i, 0)),
                   pl.BlockSpec(memory_space=pl.ANY), pl.BlockSpec(memory_space=pl.ANY)],
        out_shape=[_sds((2, s, SLAB), F32), saved, saved],
        scratch_shapes=[pltpu.VMEM((N_HEAD * bq, SLAB), BF16), pltpu.VMEM((2, N_HEAD * bq, SLAB), BF16),
                        pltpu.VMEM((2, bq, N_HEAD * bq), BF16), pltpu.VMEM((4, N_HEAD * bq, bq), BF16),
                        pltpu.VMEM((4, N_HEAD * bq, bq), BF16), pltpu.VMEM((bq, SLAB), F32),
                        pltpu.VMEM((N_HEAD, bq, 128), F32), pltpu.SemaphoreType.DMA((2, 4))],
        compiler_params=_params(2))(p16, p16, p16)


def _sb_bwd(name, p16, w_saved, lb_saved, dcat):
    s = p16.shape[1]
    bq = min(SB_BLOCK, s)
    nq = s // bq

    def kern(q_ref, k_ref, v_ref, do_ref, w_hbm, lb_hbm, dq_ref, dk_hbm, dv_hbm, dk_acc, dv_acc, dq_acc,
             qm_ref, dom_ref, k4_refs, dzc_refs, dzs_refs, ws_ref, lbs_ref, c_ref, sem, lsem):
        g, qi = pl.program_id(0), pl.program_id(1)

        @pl.when(qi == 0)
        def _():
            dk_acc[...] = jnp.zeros((s, SLAB), F32)
            dv_acc[...] = jnp.zeros((s, SLAB), F32)

        lane_head = lax.broadcasted_iota(jnp.int32, (1, SLAB), 1) // HEAD
        _head_stack(qm_ref, (q_ref[...].astype(F32) * (HEAD ** -0.5)).astype(BF16), lane_head, bq)
        _head_stack(dom_ref, do_ref[...].astype(BF16), lane_head, bq)
        row = lax.broadcasted_iota(jnp.int32, (bq, bq), 0)
        col = lax.broadcasted_iota(jnp.int32, (bq, bq), 1)
        earlier = (row < col).astype(BF16)
        tri = col < row
        dq_acc[...] = jnp.zeros((bq, SLAB), F32)
        c_ref[...] = jnp.zeros((N_HEAD, bq, 128), F32)

        def loads(kb):
            slot, kb = kb % 4, jnp.minimum(kb, qi)
            return (pltpu.make_async_copy(w_hbm.at[g, qi, kb], ws_ref.at[slot], lsem.at[0, slot]),
                    pltpu.make_async_copy(lb_hbm.at[g, qi, kb], lbs_ref.at[slot], lsem.at[1, slot]))

        def tile(kb, masked, u):
            slot = kb % 4
            k4_ref, dzc_ref, dzs_ref = k4_refs.at[u], dzc_refs.at[u], dzs_refs.at[u]
            rows = pl.ds(pl.multiple_of(kb * bq, bq), bq)
            k, v = k_ref[rows, :], v_ref[rows, :]
            _head_stack(k4_ref, k, lane_head, bq)
            for h in range(N_HEAD):
                mine = slice(h * bq, (h + 1) * bq)
                wb = ws_ref[slot, pl.ds(h * bq, bq), :]
                dl = wb.astype(F32) * lax.dot_general(dom_ref[mine, :], v, NT, preferred_element_type=F32)
                prefix = jnp.dot(dl.astype(BF16), earlier, preferred_element_type=F32)
                before = prefix + c_ref[h][:, 0:1]
                sig = jnp.exp(lbs_ref[slot, pl.ds(h * bq, bq), :].astype(F32))
                dz = dl - sig * (dl + before)
                if masked:
                    dz = jnp.where(tri, dz, 0.0)
                dzb = dz.astype(BF16)
                dzc_ref[:, mine] = dzb
                dzs_ref[mine, :] = dzb
                tail = prefix[:, bq - 128:] + dl[:, bq - 128:]
                c_ref[h] += jnp.broadcast_to(tail[:, 127:128], (bq, 128))
            dq_acc[...] += jnp.dot(dzc_ref[...], k4_ref[...], preferred_element_type=F32)
            dk_acc[rows, :] += lax.dot_general(dzs_ref[...], qm_ref[...], TN, preferred_element_type=F32)
            dv_acc[rows, :] += lax.dot_general(ws_ref[slot], dom_ref[...], TN, preferred_element_type=F32)

        for kb in (0, 1):
            for cp in loads(kb):
                cp.start()

        def pair(j, c):
            for kb in (2 * j + 2, 2 * j + 3):
                for cp in loads(kb):
                    cp.start()
            for kb in (2 * j, 2 * j + 1):
                for cp in loads(kb):
                    cp.wait()
            tile(2 * j, False, 0)
            tile(2 * j + 1, False, 1)
            return c

        lax.fori_loop(0, qi // 2, pair, 0)
        for kb in (qi - qi % 2, qi - qi % 2 + 1):
            for cp in loads(kb):
                cp.wait()

        @pl.when(qi % 2 == 1)
        def _():
            tile(qi - 1, False, 0)
            tile(qi, True, 1)

        @pl.when(qi % 2 == 0)
        def _():
            tile(qi, True, 0)

        dq_ref[...] = dq_acc[...] * (HEAD ** -0.5)

        @pl.when(qi == nq - 1)
        def _():
            ck = pltpu.make_async_copy(dk_acc, dk_hbm.at[g], sem.at[0])
            cv = pltpu.make_async_copy(dv_acc, dv_hbm.at[g], sem.at[1])
            ck.start()
            cv.start()
            ck.wait()
            cv.wait()

    blk = lambda j0: pl.BlockSpec((None, bq, SLAB), lambda g, i: (j0 + g, i, 0))
    full = lambda j0: pl.BlockSpec((None, s, SLAB), lambda g, i: ((j0 + g) // PER_SHARD, 0, (j0 + g) % PER_SHARD))
    q_blk = pl.BlockSpec((None, bq, SLAB), lambda g, i: ((2 + g) // PER_SHARD, i, (2 + g) % PER_SHARD))
    stack16 = pltpu.VMEM((N_HEAD * bq, SLAB), BF16)
    return pl.pallas_call(
        kern, name=name, grid=(2, nq),
        in_specs=[q_blk, full(4), full(6), blk(1),
                  pl.BlockSpec(memory_space=pl.ANY), pl.BlockSpec(memory_space=pl.ANY)],
        out_specs=[blk(0), pl.BlockSpec(memory_space=pl.ANY), pl.BlockSpec(memory_space=pl.ANY)],
        out_shape=[_sds((2, s, SLAB), F32)] * 3,
        scratch_shapes=[pltpu.VMEM((s, SLAB), F32), pltpu.VMEM((s, SLAB), F32), pltpu.VMEM((bq, SLAB), F32),
                        stack16, stack16, pltpu.VMEM((2, N_HEAD * bq, SLAB), BF16),
                        pltpu.VMEM((2, bq, N_HEAD * bq), BF16), pltpu.VMEM((2, N_HEAD * bq, bq), BF16),
                        pltpu.VMEM((4, N_HEAD * bq, bq), BF16), pltpu.VMEM((4, N_HEAD * bq, bq), BF16),
                        pltpu.VMEM((N_HEAD, bq, 128), F32),
                        pltpu.SemaphoreType.DMA((2,)), pltpu.SemaphoreType.DMA((2, 4))],
        compiler_params=_params(2))(p16, p16, p16, dcat, w_saved, lb_saved)


RET_BLOCK = 256


def _ret_tables(s, bl):
    nh = SLAB // HEAD
    lane_h = np.arange(SLAB) // HEAD
    log_gamma = np.log1p(-np.exp2(-5.0 - np.arange(nh, dtype=np.float64)))
    lg_lane = log_gamma[lane_h]
    half = HEAD // 2
    inv = 1.0 / (ROPE_BASE ** (np.arange(half, dtype=np.float64) / half))
    ang = np.arange(s, dtype=np.float64)[:, None] * inv[None, :]
    within = np.arange(SLAB) % HEAD
    cos = np.cos(ang)[:, within % half]
    sin = np.sin(ang)[:, within % half] * np.where(within < half, -1.0, 1.0)[None, :]
    perm = np.zeros((SLAB, SLAB))
    partner = np.where(within < half, np.arange(SLAB) + half, np.arange(SLAB) - half)
    perm[partner, np.arange(SLAB)] = 1.0
    i = np.arange(bl)
    diff = i[:, None] - i[None, :]
    same = (i[:, None] // CHUNK) == (i[None, :] // CHUNK)
    earlier = (i[None, :] // CHUNK) < (i[:, None] // CHUNK)
    decay = np.zeros((nh, bl, bl))
    for h in range(nh):
        decay[h] = np.where(same, np.exp(log_gamma[h] * np.abs(diff)),
                            np.where(earlier, np.exp(log_gamma[h] * diff), 0.0))
    qd = np.exp(lg_lane[None, :] * (i[:, None] + 1.0))
    kd = np.exp(lg_lane[None, :] * (bl - 1.0 - i[:, None]))
    gam = np.exp(lg_lane * bl)[:, None] * np.ones((1, SLAB))
    bd = (lane_h[:, None] == lane_h[None, :]).astype(np.float64)
    f = lambda a: jnp.asarray(a, F32)
    return f(cos), f(sin), f(perm), f(decay), f(qd), f(kd), f(gam), f(bd)


def _ret_block(q, k, v, state, cos, sin, perm, decay, qd, kd, gam, bd, hm):
    qr = (q * cos + jnp.dot(q, perm, preferred_element_type=F32) * sin) * (HEAD ** -0.5)
    kr = k * cos + jnp.dot(k, perm, preferred_element_type=F32) * sin
    y = jnp.dot(qr * qd, state, preferred_element_type=F32)
    for h in range(SLAB // HEAD):
        m = hm[h:h + 1]
        sc = lax.dot_general(qr * m, kr, NT, preferred_element_type=F32) * decay[h]
        y = y + jnp.dot(sc, v * m, preferred_element_type=F32)
    new_state = gam * state + lax.dot_general(kr * kd, v, TN, preferred_element_type=F32) * bd
    return y, new_state


def _ret_specs(s, bl, rev):
    nb = s // bl
    pos = (lambda n: nb - 1 - n) if rev else (lambda n: n)
    slab = lambda j: pl.BlockSpec((None, bl, SLAB), lambda n: (j // PER_SHARD, pos(n), j % PER_SHARD))
    const2 = lambda r: pl.BlockSpec((r, SLAB), lambda n: (0, 0))
    tab = [pl.BlockSpec((bl, SLAB), lambda n: (pos(n), 0))] * 2 + [
        const2(SLAB), pl.BlockSpec((SLAB // HEAD, bl, bl), lambda n: (0, 0, 0)),
        const2(bl), const2(bl), const2(SLAB), const2(SLAB), const2(8)]
    return nb, pos, slab, tab


def _ret_fwd(name, p32):
    s = p32.shape[1]
    bl = min(RET_BLOCK, s)
    nb, pos, slab, tab = _ret_specs(s, bl, False)
    tables = _ret_tables(s, bl) + (_head_masks(),)

    def kern(q_ref, k_ref, v_ref, *rest):
        t_refs, (y_ref, st_ref, state) = rest[:9], rest[9:]

        @pl.when(pl.program_id(0) == 0)
        def _():
            state[...] = jnp.zeros((SLAB, SLAB), F32)

        st_ref[...] = state[...]
        y, new = _ret_block(q_ref[...], k_ref[...], v_ref[...], state[...], *[t[...] for t in t_refs])
        y_ref[...] = y
        state[...] = new

    return pl.pallas_call(
        kern, name=name, grid=(nb,), in_specs=[slab(8), slab(9), slab(10)] + tab,
        out_specs=[pl.BlockSpec((bl, SLAB), lambda n: (n, 0)), pl.BlockSpec((None, SLAB, SLAB), lambda n: (n, 0, 0))],
        out_shape=[_sds((s, SLAB), F32), _sds((nb, SLAB, SLAB), F32)],
        scratch_shapes=[pltpu.VMEM((SLAB, SLAB), F32)], compiler_params=_params(1))(p32, p32, p32, *tables)


def _ret_bwd(name, p32, states, dy):
    s = p32.shape[1]
    bl = min(RET_BLOCK, s)
    nb, pos, slab, tab = _ret_specs(s, bl, True)
    tables = _ret_tables(s, bl) + (_head_masks(),)
    rowblk = pl.BlockSpec((bl, SLAB), lambda n: (pos(n), 0))

    def kern(q_ref, k_ref, v_ref, st_ref, dy_ref, *rest):
        t_refs, (dq_ref, dk_ref, dv_ref, dstate) = rest[:9], rest[9:]

        @pl.when(pl.program_id(0) == 0)
        def _():
            dstate[...] = jnp.zeros((SLAB, SLAB), F32)

        tv = [t[...] for t in t_refs]
        _, vjp = jax.vjp(lambda a, b, c, d: _ret_block(a, b, c, d, *tv),
                         q_ref[...], k_ref[...], v_ref[...], st_ref[...])
        dq, dk, dv, ds = vjp((dy_ref[...], dstate[...]))
        dq_ref[...] = dq
        dk_ref[...] = dk
        dv_ref[...] = dv
        dstate[...] = ds

    return pl.pallas_call(
        kern, name=name, grid=(nb,),
        in_specs=[slab(8), slab(9), slab(10), pl.BlockSpec((None, SLAB, SLAB), lambda n: (pos(n), 0, 0)), rowblk] + tab,
        out_specs=[rowblk] * 3, out_shape=[_sds((s, SLAB), F32)] * 3,
        scratch_shapes=[pltpu.VMEM((SLAB, SLAB), F32)], compiler_params=_params(1))(p32, p32, p32, states, dy, *tables)


TM_FFN = 1024
TM_SLAB = 2048
TM_NORM = 256
TM_OUT = 512
TM_GRAD = 2048
TM_RW = 512
TM_FF = 256


def _ffn_fwd(tag, x, g, w_in, w_out, l):
    tm = min(TM_FFN, x.shape[0])
    h, (u,) = _norm_proj_in(tag + "_in", x, g, w_in, l, tm, [BF16])
    a = _swiglu_fwd(tag + "_act", u, TM_FF)
    w_out2 = w_out.reshape(w_out.shape[0], 2, FF_SHARD, D_MODEL)
    xn = _proj_out(tag + "_out", a, w_out2, l, x, 0.5, min(TM_OUT, x.shape[0]))
    return xn, (x, h, u, a)


def _ffn_bwd(tag, saved, dxn, g, w_in, w_out, l):
    x, h, u, a = saved
    tm = min(TM_FFN, x.shape[0])
    w_out2 = w_out.reshape(w_out.shape[0], 2, FF_SHARD, D_MODEL)
    da = _back_out(tag + "_dact", dxn, w_out2, l, 0.5, tm, BF16)
    tk = min(TM_GRAD, x.shape[0])
    dw_out = _grad_out(tag + "_dwout", a, dxn, 0.5, tk)
    du = _swiglu_bwd(tag + "_dswi", u, da, TM_FF)
    dx, dg = _back_in_norm(tag + "_dh", du, w_in, l, min(TM_NORM, x.shape[0]), x, dxn, g)
    dw_in = _grad_in(tag + "_dwin", h, du, tk)
    return dx, dg, dw_in, dw_out.reshape(N_SHARD, D_FF // N_SHARD, D_MODEL)


def _mix_fwd(tag, x, sm, w_in, w_out, l, wl):
    h, (p32, p16) = _norm_proj_in(tag + "_in", x, sm["mix_norm"][l:l + 1], w_in, wl, min(TM_FFN, x.shape[0]),
                                  [F32, BF16])
    ypre = _conv_fwd(tag + "_conv", p32, sm["conv_w"][l], sm["conv_b"][l:l + 1])
    yconv = _ln_silu_fwd(tag + "_ln", ypre, sm["conv_ln_g"][l:l + 1], sm["conv_ln_b"][l:l + 1], TM_RW)
    osb, w_sb, lb_sb = _sb_fwd(tag + "_sb", p16)
    yr, states = _ret_fwd(tag + "_ret", p32)
    yret = _ghn_fwd(tag + "_ghn", yr, p32, sm["ret_norm_g"][l:l + 1], TM_RW)
    ycat = _assemble(tag + "_cat", [(yconv, None), (osb, 0), (osb, 1), (yret, None)], 1, TM_RW)
    xn = _proj_out(tag + "_out", ycat, w_out, wl, x, 1.0, min(TM_OUT, x.shape[0]))
    return xn, (x, h, p32, p16, ypre, (osb, w_sb, lb_sb), yr, states, ycat)


def _mix_bwd(tag, saved, dxn, sm, w_in, w_out, l, wl):
    x, h, p32, p16, ypre, osb, yr, states, ycat = saved
    ts = min(TM_SLAB, x.shape[0])
    dcat = _back_out(tag + "_dcat", dxn, w_out, wl, 1.0, ts, F32)
    dw_out = _grad_out(tag + "_dwout", ycat, dxn, 1.0, ts)
    dypre, dlg, dlb = _ln_silu_bwd(tag + "_dln", ypre, dcat, sm["conv_ln_g"][l:l + 1], sm["conv_ln_b"][l:l + 1], TM_RW)
    da, db, dcw, dcb = _conv_bwd(tag + "_dconv", p32, sm["conv_w"][l], dypre)
    dq, dk, dv = _sb_bwd(tag + "_dsb", p16, osb[1], osb[2], dcat)
    dyr, dgate, drg = _ghn_bwd(tag + "_dghn", yr, p32, dcat, sm["ret_norm_g"][l:l + 1], TM_RW)
    dqr, dkr, dvr = _ret_bwd(tag + "_dret", p32, states, dyr)
    dp = _assemble(tag + "_dp", [(da, None), (db, None), (dq, 0), (dq, 1), (dk, 0), (dk, 1), (dv, 0), (dv, 1),
                                 (dqr, None), (dkr, None), (dvr, None), (dgate, None)], PER_SHARD, TM_RW)
    dx, dg = _back_in_norm(tag + "_dh", dp, w_in, wl, min(TM_NORM, x.shape[0]), x, dxn, sm["mix_norm"][l:l + 1])
    dw_in = _grad_in(tag + "_dwin", h, dp, min(TM_GRAD, x.shape[0]))
    small = dict(mix_norm=dg, conv_w=dcw[0:CONV_W], conv_b=dcb, conv_ln_g=dlg, conv_ln_b=dlb, ret_norm_g=drg)
    return dx, small, dw_in, dw_out


def _local_step(x, tgt, early, last, sm, on_last_layer=None):
    saved = []
    weights = [(early, l) for l in range(DEPTH - 1)]
    for l in range(DEPTH):
        if l == DEPTH - 1:
            weights.append((last(x), 0))
        wt, wl = weights[l]
        x, s1 = _ffn_fwd(f"l{l}f1", x, sm["ffn1_norm"][l:l + 1], wt["ffn1_w_in"], wt["ffn1_w_out"], wl)
        x, s2 = _mix_fwd(f"l{l}mx", x, sm, wt["mix_w_in"], wt["mix_w_out"], l, wl)
        x, s3 = _ffn_fwd(f"l{l}f2", x, sm["ffn2_norm"][l:l + 1], wt["ffn2_w_in"], wt["ffn2_w_out"], wl)
        saved.append((s1, s2, s3))
    dx, dfinal, loss = _final("final", x, tgt, sm["final_norm"][None, :], TM_RW)
    big = [None] * DEPTH
    small = [None] * DEPTH
    started = None
    for l in reversed(range(DEPTH)):
        s1, s2, s3 = saved[l]
        wt, wl = weights[l]
        g2 = sm["ffn2_norm"][l:l + 1] if started is None else sm["ffn2_norm"][l:l + 1] + started
        dx, dg3, dwi3, dwo3 = _ffn_bwd(f"l{l}f2", s3, dx, g2, wt["ffn2_w_in"], wt["ffn2_w_out"], wl)
        dx, sml, dwi2, dwo2 = _mix_bwd(f"l{l}mx", s2, dx, sm, wt["mix_w_in"], wt["mix_w_out"], l, wl)
        dx, dg1, dwi1, dwo1 = _ffn_bwd(f"l{l}f1", s1, dx, sm["ffn1_norm"][l:l + 1], wt["ffn1_w_in"], wt["ffn1_w_out"], wl)
        big[l] = dict(ffn1_w_in=dwi1, ffn1_w_out=dwo1, mix_w_in=dwi2, mix_w_out=dwo2, ffn2_w_in=dwi3, ffn2_w_out=dwo3)
        sml.update(ffn1_norm=dg1, ffn2_norm=dg3)
        small[l] = sml
        if on_last_layer is not None and l == DEPTH - 1:
            started = on_last_layer(big[l])[0:1, 0:1]
    return loss, dx, big, small, dfinal


MESH = pl.DeviceIdType.MESH
ANY = pl.BlockSpec(memory_space=pl.ANY)
BIG = ("ffn1_w_in", "ffn1_w_out", "mix_w_in", "mix_w_out", "ffn2_w_in", "ffn2_w_out")


def _place():
    x, y, c = lax.axis_index("x"), lax.axis_index("y"), lax.axis_index("c")
    chips = [(1 - x, y), (x, 1 - y), (1 - x, 1 - y)]
    return x, y, c, chips


def _gather_weights(w16):
    n = len(w16)

    def kern(*refs):
        dst = refs[n:2 * n]
        send, recv = refs[2 * n:]
        x, y, c, chips = _place()
        mine = 2 * x + y
        firsts, passes = [], []
        for a in range(n):
            h = dst[a].shape[2] // 2
            own = dst[a].at[:, mine, pl.ds(c * h, h)]
            for j, (cx, cy) in enumerate(chips):
                cp = pltpu.make_async_remote_copy(
                    src_ref=own, dst_ref=own, send_sem=send.at[6 * a + j], recv_sem=recv.at[6 * a + j],
                    device_id=(cx, cy, c), device_id_type=MESH)
                cp.start()
                firsts.append(cp)
        for a in range(n):
            h = dst[a].shape[2] // 2
            half = pl.ds(c * h, h)
            for j, (cx, cy) in enumerate(chips):
                theirs = dst[a].at[:, 2 * cx + cy, half]
                pltpu.make_async_remote_copy(
                    src_ref=theirs, dst_ref=theirs, send_sem=send.at[6 * a + j], recv_sem=recv.at[6 * a + j],
                    device_id=(cx, cy, c), device_id_type=MESH).wait_recv()
                fw = pltpu.make_async_remote_copy(
                    src_ref=theirs, dst_ref=theirs, send_sem=send.at[6 * a + 3 + j], recv_sem=recv.at[6 * a + 3 + j],
                    device_id=(x, y, 1 - c), device_id_type=MESH)
                fw.start()
                passes.append(fw)
        for a in range(n):
            h = dst[a].shape[2] // 2
            other = pl.ds((1 - c) * h, h)
            for j, (cx, cy) in enumerate(chips):
                got = dst[a].at[:, 2 * cx + cy, other]
                pltpu.make_async_remote_copy(
                    src_ref=got, dst_ref=got, send_sem=send.at[6 * a + 3 + j], recv_sem=recv.at[6 * a + 3 + j],
                    device_id=(x, y, 1 - c), device_id_type=MESH).wait_recv()
        for cp in firsts + passes:
            cp.wait_send()

    return pl.pallas_call(
        kern, name="gather_weights", in_specs=[ANY] * n, out_specs=[ANY] * n,
        out_shape=[_sds(w.shape, w.dtype) for w in w16], input_output_aliases={a: a for a in range(n)},
        scratch_shapes=[pltpu.SemaphoreType.DMA((6 * n,)), pltpu.SemaphoreType.DMA((6 * n,))])(*w16)


def _last_layer_copies(bufs, send, recv, x, y, c, chips):
    mine = 2 * x + y
    sends, arrivals = [], []
    for a, buf in enumerate(bufs):
        last = buf.shape[0] - 1
        h = buf.shape[2] // 2
        own = buf.at[last, mine, pl.ds(c * h, h)]
        for j, (cx, cy) in enumerate(chips):
            for k in range(2):
                sem = 6 * a + 2 * j + k
                sends.append(pltpu.make_async_remote_copy(
                    src_ref=own, dst_ref=own, send_sem=send.at[sem], recv_sem=recv.at[sem],
                    device_id=(cx, cy, c ^ k), device_id_type=MESH))
                theirs = buf.at[last, 2 * cx + cy, pl.ds((c ^ k) * h, h)]
                arrivals.append(pltpu.make_async_remote_copy(
                    src_ref=theirs, dst_ref=theirs, send_sem=send.at[sem], recv_sem=recv.at[sem],
                    device_id=(cx, cy, c ^ k), device_id_type=MESH))
    return sends, arrivals


def _gather_last_start(w16, after):
    n = len(w16)

    def kern(*refs):
        bufs, send, recv, token = refs[:n], refs[n + 1], refs[n + 2], refs[-1]
        x, y, c, chips = _place()
        for cp in _last_layer_copies(bufs, send, recv, x, y, c, chips)[0]:
            cp.start()
        token[...] = jnp.zeros_like(token)

    out = pl.pallas_call(
        kern, name="gather_last_start",
        out_shape=(pltpu.SemaphoreType.DMA((6 * n,)), pltpu.SemaphoreType.DMA((6 * n,)),
                   *[pltpu.HBM(w.shape, w.dtype) for w in w16], _sds((8, 128), F32)),
        in_specs=[HBM] * n + [ANY],
        out_specs=(SEM, SEM, *[HBM] * n, pl.BlockSpec(memory_space=pltpu.VMEM)),
        input_output_aliases={i: 2 + i for i in range(n)},
        compiler_params=pltpu.CompilerParams(has_side_effects=EFFECT),
    )(*[pltpu.with_memory_space_constraint(w, pltpu.HBM) for w in w16], after)
    return out[0], out[1], out[2:2 + n], out[-1]


def _gather_last_wait(send, recv, w16, after):
    n = len(w16)

    def kern(*refs):
        bufs, send_sem, recv_sem = refs[:n], refs[n], refs[n + 1]
        x, y, c, chips = _place()
        sends, arrivals = _last_layer_copies(bufs, send_sem, recv_sem, x, y, c, chips)
        for cp in sends:
            cp.wait_send()
        for cp in arrivals:
            cp.wait_recv()

    return pl.pallas_call(
        kern, name="gather_last_wait", out_shape=tuple(pltpu.HBM(w.shape, w.dtype) for w in w16),
        in_specs=[HBM] * n + [SEM, SEM, ANY], out_specs=tuple([HBM] * n),
        input_output_aliases={i: i for i in range(n)},
        compiler_params=pltpu.CompilerParams(has_side_effects=EFFECT),
    )(*w16, send, recv, after)


def _pair_exchange(name, grads):
    n = len(grads)

    def kern(*refs):
        src, got_o = refs[:n], refs[n:2 * n]
        send, recv = refs[2 * n:]
        x, y, c, _ = _place()
        cps = []
        for a in range(n):
            h = src[a].shape[1] // 2
            cp = pltpu.make_async_remote_copy(
                src_ref=src[a].at[:, pl.ds((1 - c) * h, h)], dst_ref=got_o[a],
                send_sem=send.at[a], recv_sem=recv.at[a], device_id=(x, y, 1 - c), device_id_type=MESH)
            cp.start()
            cps.append(cp)
        for cp in cps:
            cp.wait()

    halves = [_sds((g.shape[0], g.shape[1] // 2, g.shape[2]), g.dtype) for g in grads]
    return pl.pallas_call(
        kern, name=name, in_specs=[ANY] * n, out_specs=[ANY] * n, out_shape=halves,
        scratch_shapes=[pltpu.SemaphoreType.DMA((n,)), pltpu.SemaphoreType.DMA((n,))])(*grads)


def _chip_exchange(name, sums):
    n = len(sums)

    def kern(*refs):
        src, dst = refs[:n], refs[n:2 * n]
        send, recv = refs[2 * n:]
        x, y, c, chips = _place()
        cps = _chip_copies(src, dst, send, recv, x, y, c, chips)
        for cp in cps:
            cp.start()
        for cp in cps:
            cp.wait()

    return pl.pallas_call(
        kern, name=name, in_specs=[ANY] * n, out_specs=[ANY] * n,
        out_shape=[_sds((3,) + s_.shape[1:], s_.dtype) for s_ in sums],
        scratch_shapes=[pltpu.SemaphoreType.DMA((3 * n,)), pltpu.SemaphoreType.DMA((3 * n,))])(*sums)


HBM = pl.BlockSpec(memory_space=pltpu.HBM)
SEM = pl.BlockSpec(memory_space=pltpu.SEMAPHORE)
EFFECT = pltpu.SideEffectType.DATAFLOW_SIDE_EFFECTING


def _chip_copies(src, land, send, recv, x, y, c, chips):
    return [pltpu.make_async_remote_copy(
        src_ref=src[a].at[2 * cx + cy], dst_ref=land[a].at[j], send_sem=send.at[3 * a + j],
        recv_sem=recv.at[3 * a + j], device_id=(cx, cy, c), device_id_type=MESH)
        for a in range(len(src)) for j, (cx, cy) in enumerate(chips)]


def _chip_exchange_start(name, sums):
    n = len(sums)

    def kern(*refs):
        src, land = refs[:n], refs[n:2 * n]
        send, recv, token = refs[2 * n], refs[2 * n + 1], refs[-1]
        x, y, c, chips = _place()
        for cp in _chip_copies(src, land, send, recv, x, y, c, chips):
            cp.start()
        token[...] = jnp.zeros_like(token)

    lands = [(3,) + s_.shape[1:] for s_ in sums]
    out = pl.pallas_call(
        kern, name=name,
        out_shape=(pltpu.SemaphoreType.DMA((3 * n,)), pltpu.SemaphoreType.DMA((3 * n,)),
                   *[pltpu.HBM(s_.shape, s_.dtype) for s_ in sums],
                   *[pltpu.HBM(shp, s_.dtype) for shp, s_ in zip(lands, sums)], _sds((8, 128), F32)),
        in_specs=[HBM] * (2 * n),
        out_specs=(SEM, SEM, *[HBM] * (2 * n), pl.BlockSpec(memory_space=pltpu.VMEM)),
        input_output_aliases={i: 2 + i for i in range(2 * n)},
        compiler_params=pltpu.CompilerParams(has_side_effects=EFFECT),
    )(*[pltpu.with_memory_space_constraint(s_, pltpu.HBM) for s_ in sums],
      *[pltpu.with_memory_space_constraint(lax.empty(shp, s_.dtype), pltpu.HBM) for shp, s_ in zip(lands, sums)])
    return out[0], out[1], out[2:2 + n], out[2 + n:2 + 2 * n], out[-1]


def _chip_exchange_wait(name, send, recv, sums, lands, after):
    n = len(sums)

    def kern(*refs):
        src, land = refs[:n], refs[n:2 * n]
        send_sem, recv_sem = refs[2 * n], refs[2 * n + 1]
        x, y, c, chips = _place()
        for cp in _chip_copies(src, land, send_sem, recv_sem, x, y, c, chips):
            cp.wait_send()
            cp.wait_recv()

    out = pl.pallas_call(
        kern, name=name,
        out_shape=tuple(pltpu.HBM(t.shape, t.dtype) for t in list(sums) + list(lands)),
        in_specs=[HBM] * (2 * n) + [SEM, SEM, ANY], out_specs=tuple([HBM] * (2 * n)),
        input_output_aliases={i: i for i in range(2 * n)},
        compiler_params=pltpu.CompilerParams(has_side_effects=EFFECT),
    )(*sums, *lands, send, recv, after)
    return out[:n], out[n:]


def _pair_join(full):
    n = len(full)

    def kern(*refs):
        dst = refs[n:2 * n]
        send, recv = refs[2 * n:]
        x, y, c, _ = _place()
        cps = []
        for a in range(n):
            h = dst[a].shape[1] // 2
            mine = dst[a].at[:, pl.ds(c * h, h)]
            cp = pltpu.make_async_remote_copy(
                src_ref=mine, dst_ref=mine, send_sem=send.at[a], recv_sem=recv.at[a],
                device_id=(x, y, 1 - c), device_id_type=MESH)
            cp.start()
            cps.append(cp)
        for a, cp in enumerate(cps):
            cp.wait_send()
            h = dst[a].shape[1] // 2
            got = dst[a].at[:, pl.ds((1 - c) * h, h)]
            pltpu.make_async_remote_copy(
                src_ref=got, dst_ref=got, send_sem=send.at[a], recv_sem=recv.at[a],
                device_id=(x, y, 1 - c), device_id_type=MESH).wait_recv()

    return pl.pallas_call(
        kern, name="pair_join", in_specs=[ANY] * n, out_specs=[ANY] * n,
        out_shape=[_sds(f.shape, f.dtype) for f in full], input_output_aliases={a: a for a in range(n)},
        scratch_shapes=[pltpu.SemaphoreType.DMA((n,)), pltpu.SemaphoreType.DMA((n,))])(*full)


def _all_sum(name, v):
    r = v.shape[0]

    def kern(v_ref, o_ref, buf, send, recv):
        x, y, c, _ = _place()
        me = 4 * x + 2 * y + c
        buf[me] = v_ref[...]
        cps = []
        for k in range(1, 8):
            peer = (x ^ (k >> 2), y ^ ((k >> 1) & 1), c ^ (k & 1))
            cp = pltpu.make_async_remote_copy(
                src_ref=v_ref, dst_ref=buf.at[me], send_sem=send.at[k - 1], recv_sem=recv.at[k - 1],
                device_id=peer, device_id_type=MESH)
            cp.start()
            cps.append(cp)
        for k in range(1, 8):
            peer_id = me ^ k
            pltpu.make_async_remote_copy(
                src_ref=v_ref, dst_ref=buf.at[peer_id], send_sem=send.at[k - 1], recv_sem=recv.at[k - 1],
                device_id=(x, y, c), device_id_type=MESH).wait_recv()
        for cp in cps:
            cp.wait_send()
        acc = buf[0]
        for d in range(1, 8):
            acc = acc + buf[d]
        o_ref[...] = acc

    vm = pl.BlockSpec(memory_space=pltpu.VMEM)
    return pl.pallas_call(
        kern, name=name, in_specs=[vm], out_specs=vm, out_shape=_sds((r, 128), F32),
        scratch_shapes=[pltpu.VMEM((8, r, 128), F32), pltpu.SemaphoreType.DMA((7,)),
                        pltpu.SemaphoreType.DMA((7,))])(v)


def _my_chip():
    return 2 * lax.axis_index("x") + lax.axis_index("y")


def _my_core():
    return lax.axis_index("c")


def _cast_place(name, w, first, count):
    _, r, c = w.shape
    tr = r // 4
    return _rw(name, lambda wb: ((wb,), ()), (count, r // tr), [w],
               [pl.BlockSpec((None, tr, c), lambda j, i: (first + j, i, 0))],
               [_sds((count, N_SHARD, r, c), BF16)],
               [pl.BlockSpec((None, None, tr, c), lambda j, i: (j, _my_chip(), i, 0))])[0]


HALF_STEPS = 2


def _add_halves(name, g, got):
    n, h, c = got.shape
    tr, nt = h // HALF_STEPS, HALF_STEPS
    return _rw(name, lambda ab, bb: ((ab.astype(F32) + bb.astype(F32),), ()), (nt,), [g, got],
               [pl.BlockSpec((n, tr, c), lambda i: (0, _my_core() * nt + i, 0)),
                pl.BlockSpec((n, tr, c), lambda i: (0, i, 0))],
               [_sds((n, h, c), BF16)], [pl.BlockSpec((n, tr, c), lambda i: (0, i, 0))])[0]


def _sum_parts(name, sums, parts, full, layer, n_layer, after=None):
    _, h, c = sums.shape
    tr, nt = h // HALF_STEPS, HALF_STEPS

    def body(own, pb):
        acc = own.astype(F32)
        for j in range(pb.shape[0]):
            acc = acc + pb[j].astype(F32)
        return (acc,), ()

    extra = ([full] if full is not None else []) + ([after] if after is not None else [])
    ins = [sums, parts] + extra
    in_specs = [pl.BlockSpec((None, tr, c), lambda i: (_my_chip(), i, 0)),
                pl.BlockSpec((parts.shape[0], tr, c), lambda i: (0, i, 0))] + [ANY] * len(extra)
    return _rw(name, body, (nt,), ins, in_specs, [_sds((n_layer, 2 * h, c), F32)],
               [pl.BlockSpec((None, tr, c), lambda i: (layer, _my_core() * nt + i, 0))],
               aliases={2: 0} if full is not None else None,
               after=[len(ins) - 1] if after is not None else ())[0]


def _adamw_math(w, g, m, v):
    m = B1 * m + (1.0 - B1) * g
    v = B2 * v + (1.0 - B2) * (g * g)
    m_hat = m / (1.0 - B1 ** STEP)
    v_hat = v / (1.0 - B2 ** STEP)
    delta = -LR * (m_hat / (jnp.sqrt(v_hat) + ADAM_EPS) + WD * w)
    return delta, m, v


def _adamw(name, w, g, m, v):
    r, c = w.shape
    tr = 128 if r % 128 == 0 else 8
    spec = _row_spec(tr, c)
    return _rw(name, lambda *b: (_adamw_math(*b), ()), (r // tr,), [w, g, m, v], [spec] * 4,
               [_sds((r, c), F32)] * 3, [spec] * 3)


SMALL = (("ffn1_norm", (DEPTH, D_MODEL)), ("mix_norm", (DEPTH, D_MODEL)), ("ffn2_norm", (DEPTH, D_MODEL)),
         ("conv_b", (DEPTH, SLAB)), ("conv_ln_g", (DEPTH, SLAB)), ("conv_ln_b", (DEPTH, SLAB)),
         ("ret_norm_g", (DEPTH, SLAB)), ("final_norm", (D_MODEL,)), ("conv_w", (DEPTH, CONV_W, SLAB)))


def _pack(parts, rows):
    flat = jnp.concatenate([p.reshape(-1) for p in parts])
    return jnp.pad(flat, (0, rows * 128 - flat.shape[0])).reshape(rows, 128)


def _unpack(packed, shapes):
    flat = packed.reshape(-1)
    out, off = [], 0
    for shp in shapes:
        n = int(np.prod(shp))
        out.append(flat[off:off + n].reshape(shp))
        off += n
    return out


def kernel(x, ffn1_norm, ffn1_w_in, ffn1_w_out, mix_norm, mix_w_in, conv_w, conv_b, conv_ln_g, conv_ln_b, ret_norm_g, mix_w_out, ffn2_norm, ffn2_w_in, ffn2_w_out, final_norm, loss_target, m_ffn1_norm, m_ffn1_w_in, m_ffn1_w_out, m_mix_norm, m_mix_w_in, m_conv_w, m_conv_b, m_conv_ln_g, m_conv_ln_b, m_ret_norm_g, m_mix_w_out, m_ffn2_norm, m_ffn2_w_in, m_ffn2_w_out, m_final_norm, v_ffn1_norm, v_ffn1_w_in, v_ffn1_w_out, v_mix_norm, v_mix_w_in, v_conv_w, v_conv_b, v_conv_ln_g, v_conv_ln_b, v_ret_norm_g, v_mix_w_out, v_ffn2_norm, v_ffn2_w_in, v_ffn2_w_out, v_final_norm):
    given = dict(locals())
    chip = 2 * lax.axis_index("x") + lax.axis_index("y")
    core = lax.axis_index("c")

    cw_rows = 128
    placed = lax.dynamic_update_slice(jnp.zeros((DEPTH, CONV_W, SLAB), F32), conv_w, (0, 0, chip * HEAD))
    placed = placed * (core == 0).astype(F32)
    conv_w_full = _unpack(_all_sum("gather_conv_w", _pack([placed], cw_rows)), [(DEPTH, CONV_W, SLAB)])[0]

    early = dict(zip(BIG, _gather_weights([_cast_place("cast_" + n, given[n], 0, DEPTH - 1) for n in BIG])))
    w_send, w_recv, arriving, started = _gather_last_start(
        [_cast_place("cast_last_" + n, given[n], DEPTH - 1, 1) for n in BIG], early[BIG[0]])
    sm = {n: given[n] for n, _ in SMALL}
    sm["conv_w"] = conv_w_full
    sm["ffn1_norm"] = ffn1_norm + started[0:1, 0:1]

    def last_weights(activations):
        return dict(zip(BIG, _gather_last_wait(w_send, w_recv, arriving, activations)))

    def chip_sums(tag, layer_grads):
        grads = [layer_grads[n] for n in BIG]
        theirs = _pair_exchange("pair_exchange_" + tag, grads)
        return [_add_halves(f"chipsum_{tag}{i}", a, b) for i, (a, b) in enumerate(zip(grads, theirs))]

    in_flight = []

    def start_last(layer_grads):
        in_flight.extend(_chip_exchange_start("chip_exchange_start", chip_sums("last", layer_grads)))
        return in_flight[4]

    loss, dx, big, small, dfinal = _local_step(x[0], loss_target[0], early, last_weights, sm, start_last)
    sums, parts = [None] * DEPTH, [None] * DEPTH
    sums[DEPTH - 1], parts[DEPTH - 1] = _chip_exchange_wait("chip_exchange_wait", *in_flight[:4], dx)
    first = _chip_exchange_start("chip_exchange_start_l0", chip_sums("l0", big[0]))
    full = [_sum_parts(f"shardsum_last{i}", sums[DEPTH - 1][i], parts[DEPTH - 1][i], None, DEPTH - 1, DEPTH,
                       after=first[4]) for i in range(len(BIG))]
    sums[0], parts[0] = _chip_exchange_wait("chip_exchange_wait_l0", *first[:4], full[-1])
    for l in range(1, DEPTH - 1):
        sums[l] = chip_sums(f"l{l}", big[l])
        parts[l] = _chip_exchange(f"chip_exchange_l{l}", sums[l])
    for l in range(DEPTH - 1):
        full = [_sum_parts(f"shardsum{DEPTH * i + l}", sums[l][i], parts[l][i], full[i], l, DEPTH)
                for i in range(len(BIG))]
    g_big = dict(zip(BIG, _pair_join(full)))

    small_parts = []
    for n, shp in SMALL:
        if n == "final_norm":
            small_parts.append(dfinal)
        else:
            small_parts.append(jnp.stack([small[l][n].reshape(shp[1:]) for l in range(DEPTH)]))
    g_small = dict(zip([n for n, _ in SMALL], _unpack(_all_sum("sum_small", _pack(small_parts, 200)), [s_ for _, s_ in SMALL])))
    g_small["conv_w"] = lax.dynamic_slice(g_small["conv_w"], (0, 0, chip * HEAD), (DEPTH, CONV_W, HEAD))

    grad, delta, new_m, new_v = dict(g_small), {}, {}, {}
    grad.update(g_big)
    for n in BIG:
        l, r, c = given[n].shape
        f = lambda t: t.reshape(l * r, c)
        d_, m_, v_ = _adamw("adamw_" + n, f(given[n]), f(grad[n]), f(given["m_" + n]), f(given["v_" + n]))
        delta[n], new_m[n], new_v[n] = d_.reshape(l, r, c), m_.reshape(l, r, c), v_.reshape(l, r, c)
    snames = [n for n, _ in SMALL]
    shapes = [given[n].shape for n in snames]
    rows = 104
    d_, m_, v_ = _adamw("adamw_small", _pack([given[n] for n in snames], rows), _pack([grad[n] for n in snames], rows),
                        _pack([given["m_" + n] for n in snames], rows), _pack([given["v_" + n] for n in snames], rows))
    for dst, packed in ((delta, d_), (new_m, m_), (new_v, v_)):
        dst.update(zip(snames, _unpack(packed, shapes)))

    total = lax.psum(loss[0, 0], ("x", "y", "c"))
    order = ["ffn1_norm", "ffn1_w_in", "ffn1_w_out", "mix_norm", "mix_w_in", "conv_w", "conv_b", "conv_ln_g",
             "conv_ln_b", "ret_norm_g", "mix_w_out", "ffn2_norm", "ffn2_w_in", "ffn2_w_out", "final_norm"]
    return (total, dx[None], *[grad[n] for n in order], *[delta[n] for n in order],
            *[new_m[n] for n in order], *[new_v[n] for n in order])
```
